```python
import math
import jax, jax.numpy as jnp
from jax import lax
import numpy as np

D_MODEL = 1024
BATCH = 8
SEQ = 8192
DEPTH = 1

CHUNK = 64
CONV_K = 4
EPS = 1e-6
ML_HEADS = 8
ML_INNER = 2 * D_MODEL
ML_DV = ML_INNER // ML_HEADS
ML_DQK = ML_DV // 2
ML_QK2 = 2 * ML_HEADS * ML_DQK
SSM_INNER = 2 * D_MODEL
SSM_HEADDIM = 64
SSM_HEADS = SSM_INNER // SSM_HEADDIM
SSM_GROUPS = 4
SSM_STATE = 128
SSM_XBC = SSM_INNER + 2 * SSM_GROUPS * SSM_STATE
SEG_SIZES = (ML_QK2, ML_INNER, ML_INNER, ML_INNER, ML_HEADS, ML_HEADS,
             SSM_XBC, SSM_INNER, SSM_HEADS, 2 * D_MODEL)
IN_WIDTH = sum(SEG_SIZES)
SEG_SPLITS = tuple(np.cumsum(SEG_SIZES)[:-1].tolist())

kernel_name = 'hybrid_mlstm_ssd_gated_merge'


def _rmsnorm(x, w):
    xf = x.astype(jnp.float32)
    y = xf * lax.rsqrt(jnp.mean(xf * xf, axis=-1, keepdims=True) + EPS)
    return (y * w.astype(jnp.float32)).astype(x.dtype)


def _causal_dwconv(x, w, b):
    s = x.shape[1]
    k = w.shape[0]
    xp = jnp.pad(x, ((0, 0), (k - 1, 0), (0, 0)))
    y = b
    for j in range(k):
        y = y + xp[:, j:j + s, :] * w[j]
    return y


def _chunk_heads(t):
    b, s = t.shape[:2]
    t = t.reshape(b, s // CHUNK, CHUNK, *t.shape[2:])
    perm = (1, 0, 3, 2) + tuple(range(4, t.ndim))
    return jnp.transpose(t, perm)


def _mlstm(q, k, v, i_pre, f_pre):
    f32 = jnp.float32
    b, s, h, dqk = q.shape
    dv = v.shape[-1]
    qc = _chunk_heads(q.astype(f32) * (dqk ** -0.5))
    kc = _chunk_heads(k.astype(f32))
    vc = _chunk_heads(v.astype(f32))
    ic = _chunk_heads(i_pre.astype(f32))
    fc = _chunk_heads(jax.nn.log_sigmoid(f_pre.astype(f32)))
    causal = jnp.tril(jnp.ones((CHUNK, CHUNK), dtype=bool))

    def step(carry, inp):
        c_mat, n_vec, m = carry
        qb, kb, vb, ib, fb = inp
        a = jnp.cumsum(fb, axis=-1)
        dmat = jnp.where(causal, a[..., :, None] - a[..., None, :] + ib[..., None, :], -jnp.inf)
        inter = a + m[..., None]
        m_t = jnp.maximum(inter, jnp.max(dmat, axis=-1))
        w_intra = jnp.exp(dmat - m_t[..., None])
        w_inter = jnp.exp(inter - m_t)
        sc = jnp.einsum('bhld,bhsd->bhls', qb, kb) * w_intra
        num = jnp.einsum('bhls,bhsv->bhlv', sc, vb) + w_inter[..., None] * jnp.einsum('bhld,bhdv->bhlv', qb, c_mat)
        den = jnp.sum(sc, axis=-1) + w_inter * jnp.einsum('bhld,bhd->bhl', qb, n_vec)
        hb = num / jnp.maximum(jnp.abs(den), jnp.exp(-m_t))[..., None]
        a_last = a[..., -1]
        g = a_last[..., None] - a + ib
        m_new = jnp.maximum(a_last + m, jnp.max(g, axis=-1))
        wk = jnp.exp(g - m_new[..., None])
        decay = jnp.exp(a_last + m - m_new)
        c_new = decay[..., None, None] * c_mat + jnp.einsum('bhl,bhld,bhlv->bhdv', wk, kb, vb)
        n_new = decay[..., None] * n_vec + jnp.einsum('bhl,bhld->bhd', wk, kb)
        return (c_new, n_new, m_new), hb

    init = (jnp.zeros((b, h, dqk, dv), f32), jnp.zeros((b, h, dqk), f32), jnp.zeros((b, h), f32))
    _, hs = lax.scan(step, init, (qc, kc, vc, ic, fc))
    return jnp.transpose(hs, (1, 0, 3, 2, 4)).reshape(b, s, h, dv)


def _ssd(x, dt, a_coef, bm, cm):
    f32 = jnp.float32
    b, s, h, p = x.shape
    g, n = bm.shape[2:]
    hg = h // g
    nc = s // CHUNK
    xc = _chunk_heads(x.astype(f32)).reshape(nc, b, g, hg, CHUNK, p)
    dtc = _chunk_heads(dt.astype(f32)).reshape(nc, b, g, hg, CHUNK)
    bc = _chunk_heads(bm.astype(f32))
    cc = _chunk_heads(cm.astype(f32))
    a_g = a_coef.astype(f32).reshape(g, hg)
    causal = jnp.tril(jnp.ones((CHUNK, CHUNK), dtype=bool))

    def step(state, inp):
        xb, dtb, bb, cb = inp
        a = jnp.cumsum(dtb * a_g[..., None], axis=-1)
        decay = jnp.exp(jnp.where(causal, a[..., :, None] - a[..., None, :], -jnp.inf))
        scores = jnp.einsum('bgln,bgsn->bgls', cb, bb)
        mix = decay * scores[:, :, None] * dtb[..., None, :]
        y = jnp.einsum('bghls,bghsp->bghlp', mix, xb)
        y = y + jnp.exp(a)[..., None] * jnp.einsum('bgln,bghpn->bghlp', cb, state)
        a_last = a[..., -1:]
        wts = jnp.exp(a_last - a) * dtb
        state = jnp.exp(a_last)[..., None] * state + jnp.einsum('bghs,bgsn,bghsp->bghpn', wts, bb, xb)
        return state, y

    init = jnp.zeros((b, g, hg, p, n), f32)
    _, ys = lax.scan(step, init, (xc, dtc, bc, cc))
    return jnp.transpose(ys, (1, 0, 4, 2, 3, 5)).reshape(b, s, h, p)


def _fwd_setup_inputs(seed: int = 0) -> dict:
    key = jax.random.key(seed)
    ks = jax.random.split(key, 24)
    f32 = jnp.float32

    def nrm(k, shape, scale):
        return jax.random.normal(k, shape, f32) * scale

    x = nrm(ks[0], (BATCH, SEQ, D_MODEL), 1.0)
    c = nrm(ks[1], (BATCH, D_MODEL), 1.0)
    norm_w = 1.0 + nrm(ks[2], (DEPTH, D_MODEL), 0.02)
    ada_w = nrm(ks[3], (DEPTH, D_MODEL, 3 * D_MODEL), 0.1 * D_MODEL ** -0.5)
    ada_b = nrm(ks[4], (DEPTH, 3 * D_MODEL), 0.02)
    w_in = nrm(ks[5], (DEPTH, D_MODEL, IN_WIDTH), D_MODEL ** -0.5)
    b_base = nrm(ks[6], (DEPTH, IN_WIDTH), 0.02)
    f_bias = jnp.linspace(3.0, 6.0, ML_HEADS, dtype=f32) + nrm(ks[7], (DEPTH, ML_HEADS), 0.1)
    dt0 = jnp.exp(jax.random.uniform(ks[8], (DEPTH, SSM_HEADS), f32, math.log(1e-3), math.log(1e-1)))
    dt_bias = dt0 + jnp.log(-jnp.expm1(-dt0))
    f_off = sum(SEG_SIZES[:5])
    dt_off = sum(SEG_SIZES[:8])
    b_in = b_base.at[:, f_off:f_off + ML_HEADS].set(f_bias).at[:, dt_off:dt_off + SSM_HEADS].set(dt_bias)
    ml_conv_w = nrm(ks[9], (DEPTH, CONV_K, ML_QK2), CONV_K ** -0.5)
    ml_conv_b = nrm(ks[10], (DEPTH, ML_QK2), 0.02)
    ml_norm_w = 1.0 + nrm(ks[11], (DEPTH, ML_INNER), 0.02)
    ssm_conv_w = nrm(ks[12], (DEPTH, CONV_K, SSM_XBC), CONV_K ** -0.5)
    ssm_conv_b = nrm(ks[13], (DEPTH, SSM_XBC), 0.02)
    ssm_a_log = jnp.log(jax.random.uniform(ks[14], (DEPTH, SSM_HEADS), f32, 1.0, 16.0))
    ssm_d = 1.0 + nrm(ks[15], (DEPTH, SSM_HEADS), 0.1)
    ssm_norm_w = 1.0 + nrm(ks[16], (DEPTH, SSM_INNER), 0.02)
    w_proj_m = nrm(ks[17], (DEPTH, ML_INNER, D_MODEL), ML_INNER ** -0.5)
    w_proj_s = nrm(ks[18], (DEPTH, SSM_INNER, D_MODEL), SSM_INNER ** -0.5)
    w_out = nrm(ks[19], (DEPTH, D_MODEL, D_MODEL), D_MODEL ** -0.5)
    final_w = 1.0 + nrm(ks[20], (D_MODEL,), 0.02)
    return {'x': x, 'c': c, 'norm_w': norm_w, 'ada_w': ada_w, 'ada_b': ada_b,
            'w_in': w_in, 'b_in': b_in, 'ml_conv_w': ml_conv_w, 'ml_conv_b': ml_conv_b,
            'ml_norm_w': ml_norm_w, 'ssm_conv_w': ssm_conv_w, 'ssm_conv_b': ssm_conv_b,
            'ssm_a_log': ssm_a_log, 'ssm_d': ssm_d, 'ssm_norm_w': ssm_norm_w,
            'w_proj_m': w_proj_m, 'w_proj_s': w_proj_s, 'w_out': w_out, 'final_w': final_w}


def _fwd_reference(x, c, norm_w, ada_w, ada_b, w_in, b_in, ml_conv_w, ml_conv_b, ml_norm_w,
              ssm_conv_w, ssm_conv_b, ssm_a_log, ssm_d, ssm_norm_w, w_proj_m, w_proj_s,
              w_out, final_w):
    b, s, _ = x.shape
    for l in range(DEPTH):
        mod = jax.nn.silu(c) @ ada_w[l] + ada_b[l]
        shift, scale, gate = jnp.split(mod, 3, axis=-1)
        u = _rmsnorm(x, norm_w[l]) * (1.0 + scale[:, None, :]) + shift[:, None, :]
        proj = u @ w_in[l] + b_in[l]
        qk, v, o_pre, z_m, i_pre, f_pre, xbc, z_s, dt_raw, merge_pre = jnp.split(proj, SEG_SPLITS, axis=-1)

        qk = jax.nn.silu(_causal_dwconv(qk, ml_conv_w[l], ml_conv_b[l]))
        q, k = jnp.split(qk, 2, axis=-1)
        h_m = _mlstm(q.reshape(b, s, ML_HEADS, ML_DQK), k.reshape(b, s, ML_HEADS, ML_DQK),
                     v.reshape(b, s, ML_HEADS, ML_DV), i_pre, f_pre)
        h_m = _rmsnorm(h_m, ml_norm_w[l].reshape(ML_HEADS, ML_DV)).reshape(b, s, ML_INNER).astype(x.dtype)
        y_m = jax.nn.sigmoid(o_pre) * h_m * jax.nn.silu(z_m)

        xbc = jax.nn.silu(_causal_dwconv(xbc, ssm_conv_w[l], ssm_conv_b[l]))
        xs, bs, cs = jnp.split(xbc, (SSM_INNER, SSM_INNER + SSM_GROUPS * SSM_STATE), axis=-1)
        dt = jax.nn.softplus(dt_raw.astype(jnp.float32))
        a_coef = -jnp.exp(ssm_a_log[l].astype(jnp.float32))
        xh = xs.reshape(b, s, SSM_HEADS, SSM_HEADDIM)
        y_s = _ssd(xh, dt, a_coef, bs.reshape(b, s, SSM_GROUPS, SSM_STATE), cs.reshape(b, s, SSM_GROUPS, SSM_STATE))
        y_s = y_s + ssm_d[l][:, None] * xh
        y_s = (y_s.reshape(b, s, SSM_INNER) * jax.nn.silu(z_s)).reshape(b, s, SSM_GROUPS, SSM_INNER // SSM_GROUPS)
        y_s = _rmsnorm(y_s, ssm_norm_w[l].reshape(SSM_GROUPS, SSM_INNER // SSM_GROUPS)).reshape(b, s, SSM_INNER).astype(x.dtype)

        gate_m, gate_s = jnp.split(jax.nn.sigmoid(merge_pre), 2, axis=-1)
        merged = gate_m * (y_m @ w_proj_m[l]) + gate_s * (y_s @ w_proj_s[l])
        x = x + gate[:, None, :] * (merged @ w_out[l])
    return _rmsnorm(x, final_w)


import jax as _jax
import jax.numpy as _jnp

TWIN_FORMAT = 'train_step'
FWD_PARAMS = ['x', 'c', 'norm_w', 'ada_w', 'ada_b', 'w_in', 'b_in', 'ml_conv_w', 'ml_conv_b', 'ml_norm_w', 'ssm_conv_w', 'ssm_conv_b', 'ssm_a_log', 'ssm_d', 'ssm_norm_w', 'w_proj_m', 'w_proj_s', 'w_out', 'final_w']
TWIN_WEIGHTS = ['norm_w', 'ada_w', 'ada_b', 'w_in', 'b_in', 'ml_conv_w', 'ml_conv_b', 'ml_norm_w', 'ssm_conv_w', 'ssm_conv_b', 'ssm_a_log', 'ssm_d', 'ssm_norm_w', 'w_proj_m', 'w_proj_s', 'w_out', 'final_w']
TWIN_DIFF_INPUT = 'x'
TWIN_INPUTS = ['x', 'c', 'norm_w', 'ada_w', 'ada_b', 'w_in', 'b_in', 'ml_conv_w', 'ml_conv_b', 'ml_norm_w', 'ssm_conv_w', 'ssm_conv_b', 'ssm_a_log', 'ssm_d', 'ssm_norm_w', 'w_proj_m', 'w_proj_s', 'w_out', 'final_w', 'loss_target', 'm_norm_w', 'm_ada_w', 'm_ada_b', 'm_w_in', 'm_b_in', 'm_ml_conv_w', 'm_ml_conv_b', 'm_ml_norm_w', 'm_ssm_conv_w', 'm_ssm_conv_b', 'm_ssm_a_log', 'm_ssm_d', 'm_ssm_norm_w', 'm_w_proj_m', 'm_w_proj_s', 'm_w_out', 'm_final_w', 'v_norm_w', 'v_ada_w', 'v_ada_b', 'v_w_in', 'v_b_in', 'v_ml_conv_w', 'v_ml_conv_b', 'v_ml_norm_w', 'v_ssm_conv_w', 'v_ssm_conv_b', 'v_ssm_a_log', 'v_ssm_d', 'v_ssm_norm_w', 'v_w_proj_m', 'v_w_proj_s', 'v_w_out', 'v_final_w']
TWIN_OUTPUTS = ['loss', 'grad_x', 'grad_norm_w', 'grad_ada_w', 'grad_ada_b', 'grad_w_in', 'grad_b_in', 'grad_ml_conv_w', 'grad_ml_conv_b', 'grad_ml_norm_w', 'grad_ssm_conv_w', 'grad_ssm_conv_b', 'grad_ssm_a_log', 'grad_ssm_d', 'grad_ssm_norm_w', 'grad_w_proj_m', 'grad_w_proj_s', 'grad_w_out', 'grad_final_w', 'delta_norm_w', 'delta_ada_w', 'delta_ada_b', 'delta_w_in', 'delta_b_in', 'delta_ml_conv_w', 'delta_ml_conv_b', 'delta_ml_norm_w', 'delta_ssm_conv_w', 'delta_ssm_conv_b', 'delta_ssm_a_log', 'delta_ssm_d', 'delta_ssm_norm_w', 'delta_w_proj_m', 'delta_w_proj_s', 'delta_w_out', 'delta_final_w', 'new_m_norm_w', 'new_m_ada_w', 'new_m_ada_b', 'new_m_w_in', 'new_m_b_in', 'new_m_ml_conv_w', 'new_m_ml_conv_b', 'new_m_ml_norm_w', 'new_m_ssm_conv_w', 'new_m_ssm_conv_b', 'new_m_ssm_a_log', 'new_m_ssm_d', 'new_m_ssm_norm_w', 'new_m_w_proj_m', 'new_m_w_proj_s', 'new_m_w_out', 'new_m_final_w', 'new_v_norm_w', 'new_v_ada_w', 'new_v_ada_b', 'new_v_w_in', 'new_v_b_in', 'new_v_ml_conv_w', 'new_v_ml_conv_b', 'new_v_ml_norm_w', 'new_v_ssm_conv_w', 'new_v_ssm_conv_b', 'new_v_ssm_a_log', 'new_v_ssm_d', 'new_v_ssm_norm_w', 'new_v_w_proj_m', 'new_v_w_proj_s', 'new_v_w_out', 'new_v_final_w']
TWIN_LEAF_KINDS = {'loss': 'loss', 'grad_x': 'grad_x', 'grad_norm_w': 'grad_w', 'grad_ada_w': 'grad_w', 'grad_ada_b': 'grad_w', 'grad_w_in': 'grad_w', 'grad_b_in': 'grad_w', 'grad_ml_conv_w': 'grad_w', 'grad_ml_conv_b': 'grad_w', 'grad_ml_norm_w': 'grad_w', 'grad_ssm_conv_w': 'grad_w', 'grad_ssm_conv_b': 'grad_w', 'grad_ssm_a_log': 'grad_w', 'grad_ssm_d': 'grad_w', 'grad_ssm_norm_w': 'grad_w', 'grad_w_proj_m': 'grad_w', 'grad_w_proj_s': 'grad_w', 'grad_w_out': 'grad_w', 'grad_final_w': 'grad_w', 'delta_norm_w': 'delta_w', 'delta_ada_w': 'delta_w', 'delta_ada_b': 'delta_w', 'delta_w_in': 'delta_w', 'delta_b_in': 'delta_w', 'delta_ml_conv_w': 'delta_w', 'delta_ml_conv_b': 'delta_w', 'delta_ml_norm_w': 'delta_w', 'delta_ssm_conv_w': 'delta_w', 'delta_ssm_conv_b': 'delta_w', 'delta_ssm_a_log': 'delta_w', 'delta_ssm_d': 'delta_w', 'delta_ssm_norm_w': 'delta_w', 'delta_w_proj_m': 'delta_w', 'delta_w_proj_s': 'delta_w', 'delta_w_out': 'delta_w', 'delta_final_w': 'delta_w', 'new_m_norm_w': 'new_m', 'new_m_ada_w': 'new_m', 'new_m_ada_b': 'new_m', 'new_m_w_in': 'new_m', 'new_m_b_in': 'new_m', 'new_m_ml_conv_w': 'new_m', 'new_m_ml_conv_b': 'new_m', 'new_m_ml_norm_w': 'new_m', 'new_m_ssm_conv_w': 'new_m', 'new_m_ssm_conv_b': 'new_m', 'new_m_ssm_a_log': 'new_m', 'new_m_ssm_d': 'new_m', 'new_m_ssm_norm_w': 'new_m', 'new_m_w_proj_m': 'new_m', 'new_m_w_proj_s': 'new_m', 'new_m_w_out': 'new_m', 'new_m_final_w': 'new_m', 'new_v_norm_w': 'new_v', 'new_v_ada_w': 'new_v', 'new_v_ada_b': 'new_v', 'new_v_w_in': 'new_v', 'new_v_b_in': 'new_v', 'new_v_ml_conv_w': 'new_v', 'new_v_ml_conv_b': 'new_v', 'new_v_ml_norm_w': 'new_v', 'new_v_ssm_conv_w': 'new_v', 'new_v_ssm_conv_b': 'new_v', 'new_v_ssm_a_log': 'new_v', 'new_v_ssm_d': 'new_v', 'new_v_ssm_norm_w': 'new_v', 'new_v_w_proj_m': 'new_v', 'new_v_w_proj_s': 'new_v', 'new_v_w_out': 'new_v', 'new_v_final_w': 'new_v'}


def _forward(args):
    return _fwd_reference(*[args[k] for k in FWD_PARAMS])


def _output_shape():
    def fwd():
        inp = _fwd_setup_inputs(0)
        return _fwd_reference(*[inp[k] for k in FWD_PARAMS])
    out = _jax.eval_shape(fwd)
    return out.shape, out.dtype

N_MICROBATCH = 1
ADAM_LR = 0.001
ADAM_B1 = 0.9
ADAM_B2 = 0.999
ADAM_EPS = 1e-08
ADAM_WD = 0.01
ADAM_STEP = 10
PER_EXAMPLE_BATCH_AXIS = {'x': 0, 'c': 0, 'loss_target': 0}
SHARED_INPUTS = []
_WEIGHT_DTYPES = {'norm_w': _jnp.float32, 'ada_w': _jnp.float32, 'ada_b': _jnp.float32, 'w_in': _jnp.float32, 'b_in': _jnp.float32, 'ml_conv_w': _jnp.float32, 'ml_conv_b': _jnp.float32, 'ml_norm_w': _jnp.float32, 'ssm_conv_w': _jnp.float32, 'ssm_conv_b': _jnp.float32, 'ssm_a_log': _jnp.float32, 'ssm_d': _jnp.float32, 'ssm_norm_w': _jnp.float32, 'w_proj_m': _jnp.float32, 'w_proj_s': _jnp.float32, 'w_out': _jnp.float32, 'final_w': _jnp.float32}
MOMENT_SCALE = {'norm_w': 1.510612e-02, 'ada_w': 5.044566e-02, 'ada_b': 8.433127e-02, 'w_in': 3.723964e-03, 'b_in': 6.578400e-03, 'ml_conv_w': 1.970066e-03, 'ml_conv_b': 1.793345e-03, 'ml_norm_w': 1.997041e-03, 'ssm_conv_w': 5.278449e-03, 'ssm_conv_b': 7.298546e-03, 'ssm_a_log': 2.807037e-02, 'ssm_d': 4.302431e-02, 'ssm_norm_w': 6.694193e-03, 'w_proj_m': 2.799349e-03, 'w_proj_s': 8.648324e-03, 'w_out': 9.073891e-03, 'final_w': 6.405133e+01}


def _to_microbatches(a, axis):
    t = _jnp.moveaxis(a, axis, 0)
    t = t.reshape((N_MICROBATCH, t.shape[0] // N_MICROBATCH) + t.shape[1:])
    return _jnp.moveaxis(t, 1, axis + 1)


def setup_inputs(seed: int = 0) -> dict:
    inp = _fwd_setup_inputs(seed)
    key = _jax.random.fold_in(_jax.random.key(seed), 7919)
    shape, _ = _output_shape()
    out = dict(inp)
    out["loss_target"] = _jax.random.normal(_jax.random.fold_in(key, 0), shape, _jnp.float32)
    for i, name in enumerate(TWIN_WEIGHTS):
        w = inp[name].astype(_jnp.float32)
        if MOMENT_SCALE is None:
            s = _jnp.sqrt(_jnp.mean(_jnp.square(w)) + 1e-30)
        else:
            s = MOMENT_SCALE[name]
        km, kv = _jax.random.split(_jax.random.fold_in(key, i + 1))
        out[name] = w
        out["m_" + name] = s * _jax.random.normal(km, w.shape, _jnp.float32)
        out["v_" + name] = (s * s) * _jax.random.uniform(kv, w.shape, _jnp.float32, 0.5, 1.5)
    if N_MICROBATCH > 1:
        for name, axis in PER_EXAMPLE_BATCH_AXIS.items():
            out[name] = _to_microbatches(out[name], axis)
    return {'x': out['x'], 'c': out['c'], 'norm_w': out['norm_w'], 'ada_w': out['ada_w'], 'ada_b': out['ada_b'], 'w_in': out['w_in'], 'b_in': out['b_in'], 'ml_conv_w': out['ml_conv_w'], 'ml_conv_b': out['ml_conv_b'], 'ml_norm_w': out['ml_norm_w'], 'ssm_conv_w': out['ssm_conv_w'], 'ssm_conv_b': out['ssm_conv_b'], 'ssm_a_log': out['ssm_a_log'], 'ssm_d': out['ssm_d'], 'ssm_norm_w': out['ssm_norm_w'], 'w_proj_m': out['w_proj_m'], 'w_proj_s': out['w_proj_s'], 'w_out': out['w_out'], 'final_w': out['final_w'], 'loss_target': out['loss_target'], 'm_norm_w': out['m_norm_w'], 'm_ada_w': out['m_ada_w'], 'm_ada_b': out['m_ada_b'], 'm_w_in': out['m_w_in'], 'm_b_in': out['m_b_in'], 'm_ml_conv_w': out['m_ml_conv_w'], 'm_ml_conv_b': out['m_ml_conv_b'], 'm_ml_norm_w': out['m_ml_norm_w'], 'm_ssm_conv_w': out['m_ssm_conv_w'], 'm_ssm_conv_b': out['m_ssm_conv_b'], 'm_ssm_a_log': out['m_ssm_a_log'], 'm_ssm_d': out['m_ssm_d'], 'm_ssm_norm_w': out['m_ssm_norm_w'], 'm_w_proj_m': out['m_w_proj_m'], 'm_w_proj_s': out['m_w_proj_s'], 'm_w_out': out['m_w_out'], 'm_final_w': out['m_final_w'], 'v_norm_w': out['v_norm_w'], 'v_ada_w': out['v_ada_w'], 'v_ada_b': out['v_ada_b'], 'v_w_in': out['v_w_in'], 'v_b_in': out['v_b_in'], 'v_ml_conv_w': out['v_ml_conv_w'], 'v_ml_conv_b': out['v_ml_conv_b'], 'v_ml_norm_w': out['v_ml_norm_w'], 'v_ssm_conv_w': out['v_ssm_conv_w'], 'v_ssm_conv_b': out['v_ssm_conv_b'], 'v_ssm_a_log': out['v_ssm_a_log'], 'v_ssm_d': out['v_ssm_d'], 'v_ssm_norm_w': out['v_ssm_norm_w'], 'v_w_proj_m': out['v_w_proj_m'], 'v_w_proj_s': out['v_w_proj_s'], 'v_w_out': out['v_w_out'], 'v_final_w': out['v_final_w']}


def _loss(weights, diff, rest, loss_target):
    with _jax.named_scope("forward"):
        args = {**rest, TWIN_DIFF_INPUT: diff, **{k: w.astype(_WEIGHT_DTYPES[k]) for k, w in weights.items()}}
        y = _forward(args)
    with _jax.named_scope("loss_head"):
        err = _jnp.square(y.astype(_jnp.float32) - loss_target)
        return 0.5 * _jnp.sum(_jnp.mean(err, axis=-1)) if err.ndim else 0.5 * err


def _adamw(w, g, m, v):
    m = ADAM_B1 * m + (1.0 - ADAM_B1) * g
    v = ADAM_B2 * v + (1.0 - ADAM_B2) * _jnp.square(g)
    m_hat = m / (1.0 - ADAM_B1 ** ADAM_STEP)
    v_hat = v / (1.0 - ADAM_B2 ** ADAM_STEP)
    delta = -ADAM_LR * (m_hat / (_jnp.sqrt(v_hat) + ADAM_EPS) + ADAM_WD * w)
    return delta, m, v


def reference(x, c, norm_w, ada_w, ada_b, w_in, b_in, ml_conv_w, ml_conv_b, ml_norm_w, ssm_conv_w, ssm_conv_b, ssm_a_log, ssm_d, ssm_norm_w, w_proj_m, w_proj_s, w_out, final_w, loss_target, m_norm_w, m_ada_w, m_ada_b, m_w_in, m_b_in, m_ml_conv_w, m_ml_conv_b, m_ml_norm_w, m_ssm_conv_w, m_ssm_conv_b, m_ssm_a_log, m_ssm_d, m_ssm_norm_w, m_w_proj_m, m_w_proj_s, m_w_out, m_final_w, v_norm_w, v_ada_w, v_ada_b, v_w_in, v_b_in, v_ml_conv_w, v_ml_conv_b, v_ml_norm_w, v_ssm_conv_w, v_ssm_conv_b, v_ssm_a_log, v_ssm_d, v_ssm_norm_w, v_w_proj_m, v_w_proj_s, v_w_out, v_final_w):
    given = dict(x=x, c=c, norm_w=norm_w, ada_w=ada_w, ada_b=ada_b, w_in=w_in, b_in=b_in, ml_conv_w=ml_conv_w, ml_conv_b=ml_conv_b, ml_norm_w=ml_norm_w, ssm_conv_w=ssm_conv_w, ssm_conv_b=ssm_conv_b, ssm_a_log=ssm_a_log, ssm_d=ssm_d, ssm_norm_w=ssm_norm_w, w_proj_m=w_proj_m, w_proj_s=w_proj_s, w_out=w_out, final_w=final_w, loss_target=loss_target, m_norm_w=m_norm_w, m_ada_w=m_ada_w, m_ada_b=m_ada_b, m_w_in=m_w_in, m_b_in=m_b_in, m_ml_conv_w=m_ml_conv_w, m_ml_conv_b=m_ml_conv_b, m_ml_norm_w=m_ml_norm_w, m_ssm_conv_w=m_ssm_conv_w, m_ssm_conv_b=m_ssm_conv_b, m_ssm_a_log=m_ssm_a_log, m_ssm_d=m_ssm_d, m_ssm_norm_w=m_ssm_norm_w, m_w_proj_m=m_w_proj_m, m_w_proj_s=m_w_proj_s, m_w_out=m_w_out, m_final_w=m_final_w, v_norm_w=v_norm_w, v_ada_w=v_ada_w, v_ada_b=v_ada_b, v_w_in=v_w_in, v_b_in=v_b_in, v_ml_conv_w=v_ml_conv_w, v_ml_conv_b=v_ml_conv_b, v_ml_norm_w=v_ml_norm_w, v_ssm_conv_w=v_ssm_conv_w, v_ssm_conv_b=v_ssm_conv_b, v_ssm_a_log=v_ssm_a_log, v_ssm_d=v_ssm_d, v_ssm_norm_w=v_ssm_norm_w, v_w_proj_m=v_w_proj_m, v_w_proj_s=v_w_proj_s, v_w_out=v_w_out, v_final_w=v_final_w)
    weights = {n: given[n] for n in TWIN_WEIGHTS}
    shared = {n: given[n] for n in SHARED_INPUTS}
    per_example = {n: given[n] for n in ['x', 'c']}
    grad_fn = _jax.value_and_grad(_loss, argnums=(0, 1))

    def one_microbatch(ex, loss_target):
        ex = dict(ex)
        diff = ex.pop(TWIN_DIFF_INPUT)
        return grad_fn(weights, diff, {**shared, **ex}, loss_target)

    if N_MICROBATCH == 1:
        loss, (grad_w, grad_x) = one_microbatch(per_example, given["loss_target"])
    else:
        def body(carry, xs):
            loss_sum, grad_sum = carry
            l_k, (gw_k, gx_k) = one_microbatch(xs[0], xs[1])
            with _jax.named_scope("update"):
                return (loss_sum + l_k, _jax.tree.map(_jnp.add, grad_sum, gw_k)), gx_k

        init = (_jnp.zeros((), _jnp.float32), _jax.tree.map(_jnp.zeros_like, weights))
        (loss, grad_w), grad_x = _jax.lax.scan(body, init, (per_example, given["loss_target"]))
    with _jax.named_scope("update"):
        delta_w, new_m, new_v = {}, {}, {}
        for n in TWIN_WEIGHTS:
            delta_w[n], new_m[n], new_v[n] = _adamw(weights[n], grad_w[n], given["m_" + n], given["v_" + n])
    return (loss, grad_x, *[grad_w[n] for n in TWIN_WEIGHTS], *[delta_w[n] for n in TWIN_WEIGHTS],
            *[new_m[n] for n in TWIN_WEIGHTS], *[new_v[n] for n in TWIN_WEIGHTS])
```

```python
import functools
import math

import jax
import jax.numpy as jnp
from jax import lax
from jax.experimental import pallas as pl
from jax.experimental.pallas import tpu as pltpu

F32 = jnp.float32
BF16 = jnp.bfloat16
HI = lax.Precision.HIGHEST
MESH = pl.DeviceIdType.MESH

D_MODEL = 1024
EPS = 1e-6
CONV_K = 4
ML_HEADS = 8
ML_DQK = 128
ML_DV = 256
SSM_HEADS = 32
SSM_HEADDIM = 64
SSM_GROUPS = 4
SSM_STATE = 128
IN_WIDTH = 15408
N_CHIPS = 4
N_DEV = 8
ADAM_LR, ADAM_B1, ADAM_B2, ADAM_EPS, ADAM_WD, ADAM_STEP = 0.001, 0.9, 0.999, 1e-08, 0.01, 10

O_QK, O_V, O_O, O_ZM, O_ZS, O_MG, O_XBC, O_IF, O_DT = 0, 2048, 4096, 6144, 8192, 10240, 12288, 15360, 15488
NP = 15872
LANE = 128
CHUNK = 128
NEG = -1e30
VMEM_LIMIT = 48 * 1024 * 1024


def _cparams(*sem):
    return pltpu.CompilerParams(dimension_semantics=sem, vmem_limit_bytes=VMEM_LIMIT)


def _pad_cols(w):
    z = lambda n: jnp.zeros(w.shape[:-1] + (n,), w.dtype)
    return jnp.concatenate([w[..., :8192], w[..., 11280:13328], w[..., 13360:15408], w[..., 8208:11280],
                            w[..., 8192:8208], z(112), w[..., 13328:13360], z(96 + NP - 15616)], axis=-1)


def _unpad_cols(g):
    return jnp.concatenate([g[..., :8192], g[..., O_IF:O_IF + 16], g[..., O_XBC:O_XBC + 3072],
                            g[..., O_ZS:O_ZS + 2048], g[..., O_DT:O_DT + 32], g[..., O_MG:O_MG + 2048]], axis=-1)


def _sigmoid(x):
    return 1.0 / (1.0 + jnp.exp(-x))


def _silu(x):
    return x * _sigmoid(x)


def _dsilu(x):
    s = _sigmoid(x)
    return s + x * s * (1.0 - s)


def _softplus(x):
    return jnp.maximum(x, 0.0) + jnp.log(1.0 + jnp.exp(-jnp.abs(x)))


def _logsigmoid(x):
    return jnp.minimum(x, 0.0) - jnp.log(1.0 + jnp.exp(-jnp.abs(x)))


def _dot(a, b, dims, precision=None):
    return lax.dot_general(a, b, (dims, ((), ())), preferred_element_type=F32, precision=precision)


def _nn(a, b, precision=None):
    return _dot(a, b, ((1,), (0,)), precision)


def _nt(a, b, precision=None):
    return _dot(a, b, ((1,), (1,)), precision)


def _tn(a, b, precision=None):
    return _dot(a, b, ((0,), (0,)), precision)


def _bf(x):
    return x.astype(BF16)


def _lane_col(x, lane):
    idx = lax.broadcasted_iota(jnp.int32, x.shape, 1)
    return jnp.sum(jnp.where(idx == lane, x, 0.0), axis=1, keepdims=True)


def _tri(n, upper):
    r = lax.broadcasted_iota(jnp.int32, (n, n), 0)
    c = lax.broadcasted_iota(jnp.int32, (n, n), 1)
    return jnp.where((r <= c) if upper else (r >= c), 1.0, 0.0).astype(F32)


def _eye(n):
    return jnp.where(lax.broadcasted_iota(jnp.int32, (n, n), 0) == lax.broadcasted_iota(jnp.int32, (n, n), 1), 1.0, 0.0)


def _sum_all(x):
    return jnp.sum(jnp.sum(x, axis=1, keepdims=True), axis=0, keepdims=True)


def _crossing(p):
    L = p.shape[0]
    hi = _bf(p)
    lo = _bf(p - hi.astype(F32))
    upper = _bf(_tri(L, True))
    below = _nn(upper, hi) + _nn(upper, lo)
    strict = lax.broadcasted_iota(jnp.int32, (L, L), 0) > lax.broadcasted_iota(jnp.int32, (L, L), 1)
    return jnp.sum(jnp.where(strict, below, 0.0), axis=1, keepdims=True)


def _matmul_bias(a, w, bias, tm, tn):
    m, k = a.shape
    n = w.shape[1]

    def body(a_ref, w_ref, b_ref, o_ref):
        o_ref[...] = _nn(a_ref[...], w_ref[...]) + b_ref[...]

    return pl.pallas_call(
        body, name="matmul_bias", grid=(m // tm, n // tn),
        in_specs=[pl.BlockSpec((tm, k), lambda i, j: (i, 0)), pl.BlockSpec((k, tn), lambda i, j: (0, j)),
                  pl.BlockSpec((1, tn), lambda i, j: (0, j))],
        out_specs=pl.BlockSpec((tm, tn), lambda i, j: (i, j)),
        out_shape=jax.ShapeDtypeStruct((m, n), F32),
        compiler_params=_cparams("parallel", "arbitrary"))(a, w, bias)


def _matmul_nt(a, w, tm, tk):
    m, n = a.shape
    k = w.shape[0]

    def body(a_ref, w_ref, o_ref):
        @pl.when(pl.program_id(1) == 0)
        def _():
            o_ref[...] = jnp.zeros_like(o_ref)
        o_ref[...] += _nt(a_ref[...], w_ref[...])

    return pl.pallas_call(
        body, name="matmul_nt", grid=(m // tm, n // tk),
        in_specs=[pl.BlockSpec((tm, tk), lambda i, j: (i, j)), pl.BlockSpec((k, tk), lambda i, j: (0, j))],
        out_specs=pl.BlockSpec((tm, k), lambda i, j: (i, 0)),
        out_shape=jax.ShapeDtypeStruct((m, k), F32),
        compiler_params=_cparams("parallel", "arbitrary"))(a, w)


def _matmul_tn(a, b, tm, tn, with_colsum=False):
    m, k = a.shape
    n = b.shape[1]

    def body(a_ref, b_ref, o_ref, *rest):
        first = pl.program_id(1) == 0

        @pl.when(first)
        def _():
            o_ref[...] = jnp.zeros_like(o_ref)
        o_ref[...] += _tn(a_ref[...], b_ref[...])
        if with_colsum:
            s_ref = rest[0]

            @pl.when(first)
            def _():
                s_ref[...] = jnp.zeros_like(s_ref)
            s_ref[...] += jnp.sum(b_ref[...].astype(F32), axis=0, keepdims=True)

    out_specs = [pl.BlockSpec((k, tn), lambda j, i: (0, j))]
    out_shape = [jax.ShapeDtypeStruct((k, n), F32)]
    if with_colsum:
        out_specs.append(pl.BlockSpec((1, tn), lambda j, i: (0, j)))
        out_shape.append(jax.ShapeDtypeStruct((1, n), F32))
    out = pl.pallas_call(
        body, name="matmul_tn", grid=(n // tn, m // tm),
        in_specs=[pl.BlockSpec((tm, k), lambda j, i: (i, 0)), pl.BlockSpec((tm, tn), lambda j, i: (i, j))],
        out_specs=out_specs, out_shape=out_shape,
        compiler_params=_cparams("parallel", "arbitrary"))(a, b)
    return out if with_colsum else out[0]


def _ada_fwd(c_all, ada_w, ada_b):
    def body(c_ref, w_ref, b_ref, o_ref):
        o_ref[...] = _nn(_bf(_silu(c_ref[...])), _bf(w_ref[...])) + b_ref[...]

    return pl.pallas_call(body, name="ada_fwd", out_shape=jax.ShapeDtypeStruct((c_all.shape[0], ada_w.shape[1]), F32),
                          compiler_params=_cparams())(c_all, ada_w, ada_b)


def _ada_bwd(c_all, dmod):
    def body(c_ref, d_ref, o_ref):
        o_ref[...] = _tn(_bf(_silu(c_ref[...])), _bf(d_ref[...]))

    return pl.pallas_call(body, name="ada_bwd", out_shape=jax.ShapeDtypeStruct((c_all.shape[1], dmod.shape[1]), F32),
                          compiler_params=_cparams())(c_all, dmod)


def _prenorm_fwd(x, norm_w, scale, shift, ts):
    s, d = x.shape

    def body(x_ref, nw_ref, sc_ref, sh_ref, u_ref):
        xv = x_ref[...]
        r = lax.rsqrt(jnp.mean(xv * xv, axis=1, keepdims=True) + EPS)
        u_ref[...] = _bf(xv * r * nw_ref[...] * (1.0 + sc_ref[...]) + sh_ref[...])

    row = pl.BlockSpec((1, d), lambda i: (0, 0))
    return pl.pallas_call(
        body, name="prenorm_fwd", grid=(s // ts,),
        in_specs=[pl.BlockSpec((ts, d), lambda i: (i, 0)), row, row, row],
        out_specs=pl.BlockSpec((ts, d), lambda i: (i, 0)), out_shape=jax.ShapeDtypeStruct((s, d), BF16),
        compiler_params=_cparams("parallel"))(x, norm_w, scale, shift)


def _prenorm_bwd(du, x, dxres, norm_w, scale, ts):
    s, d = x.shape

    def body(du_ref, x_ref, dr_ref, nw_ref, sc_ref, gx_ref, acc_ref):
        @pl.when(pl.program_id(0) == 0)
        def _():
            acc_ref[...] = jnp.zeros_like(acc_ref)
        xv, duv = x_ref[...], du_ref[...]
        r = lax.rsqrt(jnp.mean(xv * xv, axis=1, keepdims=True) + EPS)
        xn = xv * r
        nw, sc1 = nw_ref[...], 1.0 + sc_ref[...]
        dxn = duv * (nw * sc1)
        gx_ref[...] = r * (dxn - xn * jnp.mean(dxn * xn, axis=1, keepdims=True)) + dr_ref[...]
        t = duv * xn
        acc_ref[0:1, :] += jnp.sum(t, axis=0, keepdims=True) * sc1
        acc_ref[1:2, :] += jnp.sum(t, axis=0, keepdims=True) * nw
        acc_ref[2:3, :] += jnp.sum(duv, axis=0, keepdims=True)

    tile = pl.BlockSpec((ts, d), lambda i: (i, 0))
    row = pl.BlockSpec((1, d), lambda i: (0, 0))
    return pl.pallas_call(
        body, name="prenorm_bwd", grid=(s // ts,),
        in_specs=[tile, tile, tile, row, row],
        out_specs=[tile, pl.BlockSpec((8, d), lambda i: (0, 0))],
        out_shape=[jax.ShapeDtypeStruct((s, d), F32), jax.ShapeDtypeStruct((8, d), F32)],
        compiler_params=_cparams("arbitrary"))(du, x, dxres, norm_w, scale)


CONV_CB = 512


def _conv_taps(buf_ref, ts):
    return [buf_ref[pl.ds(8 - (CONV_K - 1) + j, ts), :] for j in range(CONV_K)]


def _conv_fwd(proj, col0, width, w8, b, ts):
    s = proj.shape[0]
    cb = CONV_CB
    nt = s // ts

    def body(x_ref, w_ref, b_ref, o_ref, buf_ref):
        @pl.when(pl.program_id(1) == 0)
        def _():
            buf_ref[0:8, :] = jnp.zeros((8, cb), F32)
        buf_ref[pl.ds(8, ts), :] = x_ref[...]
        acc = b_ref[...] + jnp.zeros((ts, cb), F32)
        for j, tap in enumerate(_conv_taps(buf_ref, ts)):
            acc = acc + tap * w_ref[j:j + 1, :]
        o_ref[...] = _silu(acc)
        buf_ref[0:8, :] = x_ref[pl.ds(ts - 8, 8), :]

    c0 = col0 // cb
    return pl.pallas_call(
        body, name="conv_fwd", grid=(width // cb, nt),
        in_specs=[pl.BlockSpec((ts, cb), lambda c, i: (i, c0 + c)), pl.BlockSpec((8, cb), lambda c, i: (0, c)),
                  pl.BlockSpec((1, cb), lambda c, i: (0, c))],
        out_specs=pl.BlockSpec((ts, cb), lambda c, i: (i, c)),
        out_shape=jax.ShapeDtypeStruct((s, width), F32),
        scratch_shapes=[pltpu.VMEM((ts + 8, cb), F32)],
        compiler_params=_cparams("parallel", "arbitrary"))(proj, w8, b)


def _conv_bwd(proj, col0, width, w8, b, dpost, ts):
    s = proj.shape[0]
    cb = CONV_CB
    nt = s // ts
    c0 = col0 // cb

    def body(x_ref, xh_ref, dp_ref, w_ref, b_ref, dx_ref, acc_ref, buf_ref, dbuf_ref):
        step = pl.program_id(1)
        tile = nt - 1 - step

        @pl.when(step == 0)
        def _():
            acc_ref[...] = jnp.zeros_like(acc_ref)
            dbuf_ref[pl.ds(ts, 8), :] = jnp.zeros((8, cb), F32)
        buf_ref[0:8, :] = jnp.where(tile == 0, 0.0, xh_ref[...])
        buf_ref[pl.ds(8, ts), :] = x_ref[...]
        taps = _conv_taps(buf_ref, ts)
        acc = b_ref[...] + jnp.zeros((ts, cb), F32)
        for j in range(CONV_K):
            acc = acc + taps[j] * w_ref[j:j + 1, :]
        dconv = dp_ref[...] * _dsilu(acc)
        acc_ref[4:5, :] += jnp.sum(dconv, axis=0, keepdims=True)
        for j in range(CONV_K):
            acc_ref[j:j + 1, :] += jnp.sum(taps[j] * dconv, axis=0, keepdims=True)
        dbuf_ref[pl.ds(0, ts), :] = dconv
        dx = jnp.zeros((ts, cb), F32)
        for j in range(CONV_K):
            dx = dx + dbuf_ref[pl.ds(CONV_K - 1 - j, ts), :] * w_ref[j:j + 1, :]
        dx_ref[...] = _bf(dx)
        dbuf_ref[pl.ds(ts, 8), :] = dconv[0:8, :]

    rows8 = ts // 8
    return pl.pallas_call(
        body, name="conv_bwd", grid=(width // cb, nt),
        in_specs=[pl.BlockSpec((ts, cb), lambda c, i: (nt - 1 - i, c0 + c)),
                  pl.BlockSpec((8, cb), lambda c, i: (jnp.maximum((nt - 1 - i) * rows8 - 1, 0), c0 + c)),
                  pl.BlockSpec((ts, cb), lambda c, i: (nt - 1 - i, c)),
                  pl.BlockSpec((8, cb), lambda c, i: (0, c)), pl.BlockSpec((1, cb), lambda c, i: (0, c))],
        out_specs=[pl.BlockSpec((ts, cb), lambda c, i: (nt - 1 - i, c)), pl.BlockSpec((8, cb), lambda c, i: (0, c))],
        out_shape=[jax.ShapeDtypeStruct((s, width), BF16), jax.ShapeDtypeStruct((8, width), F32)],
        scratch_shapes=[pltpu.VMEM((ts + 8, cb), F32), pltpu.VMEM((ts + 8, cb), F32)],
        compiler_params=_cparams("parallel", "arbitrary"))(proj, proj, dpost, w8, b)


def _mlstm_gates(gif_ref, gt_ref, a_scr, at_scr):
    L = gif_ref.shape[0]
    fb = _logsigmoid(gif_ref[...])
    a_scr[...] = _nn(_tri(L, False), fb, HI)
    at_scr[...] = _nn(_logsigmoid(gt_ref[...]), _tri(L, True), HI)
    return jnp.sum(fb, axis=0, keepdims=True)


def _mlstm_head(h, qk_ref, v_ref, gif, gt_ref, a, at_scr, a_last_row, c_mat, n_row, m_prev):
    L = gif.shape[0]
    q = qk_ref[:, h * ML_DQK:(h + 1) * ML_DQK] * (ML_DQK ** -0.5)
    k = qk_ref[:, (ML_HEADS + h) * ML_DQK:(ML_HEADS + h + 1) * ML_DQK]
    v = v_ref[:, h * ML_DV:(h + 1) * ML_DV]
    i_col, a_col = _lane_col(gif, h), _lane_col(a, ML_HEADS + h)
    i_row, a_row = gt_ref[h:h + 1, :], at_scr[ML_HEADS + h:ML_HEADS + h + 1, :]
    causal = lax.broadcasted_iota(jnp.int32, (L, L), 0) >= lax.broadcasted_iota(jnp.int32, (L, L), 1)
    dmat = jnp.where(causal, a_col - a_row + i_row, NEG)
    inter = a_col + m_prev
    m_t = jnp.maximum(inter, jnp.max(dmat, axis=1, keepdims=True))
    w_intra = jnp.exp(dmat - m_t)
    w_inter = jnp.exp(inter - m_t)
    sc = _nt(_bf(q), _bf(k)) * w_intra
    den = jnp.sum(sc, axis=1, keepdims=True) + w_inter * jnp.sum(q * n_row, axis=1, keepdims=True)
    floor = jnp.exp(-m_t)
    a_last = _lane_col(a_last_row, ML_HEADS + h)
    g = a_last - a_col + i_col
    m_new = jnp.maximum(a_last + m_prev, jnp.max(g, axis=0, keepdims=True))
    wk = jnp.exp(g - m_new)
    decay = jnp.exp(a_last + m_prev - m_new)
    return dict(q=q, k=k, v=v, w_intra=w_intra, w_inter=w_inter, sc=sc, den=den, floor=floor, m_new=m_new, wk=wk,
                decay=decay)


def _state_tile(n_row, m11):
    r = lax.broadcasted_iota(jnp.int32, (8, LANE), 0)
    return jnp.where(r == 0, n_row, jnp.where(r == 1, m11, 0.0))


def _mlstm_fwd(qk, proj, gt):
    s = qk.shape[0]
    L = CHUNK
    nc = s // L

    def body(qk_ref, v_ref, gif_ref, gt_ref, h_ref, cst_ref, nm_ref, c_scr, nm_scr, a_scr, at_scr):
        @pl.when(pl.program_id(0) == 0)
        def _():
            c_scr[...] = jnp.zeros_like(c_scr)
            nm_scr[...] = jnp.zeros_like(nm_scr)
        a_last_row = _mlstm_gates(gif_ref, gt_ref, a_scr, at_scr)
        gif, a = gif_ref[...], a_scr[...]
        for h in range(ML_HEADS):
            c_mat, n_row = c_scr[h], nm_scr[h, 0:1, :]
            m_prev = jnp.max(nm_scr[h, 1:2, :], axis=1, keepdims=True)
            cst_ref[0, h] = c_mat
            nm_ref[0, h] = nm_scr[h]
            t = _mlstm_head(h, qk_ref, v_ref, gif, gt_ref, a, at_scr, a_last_row, c_mat, n_row, m_prev)
            num = _nn(_bf(t["sc"]), _bf(t["v"])) + t["w_inter"] * _nn(_bf(t["q"]), _bf(c_mat))
            h_ref[:, h * ML_DV:(h + 1) * ML_DV] = num / jnp.maximum(jnp.abs(t["den"]), t["floor"])
            kw = t["k"] * t["wk"]
            c_scr[h] = t["decay"] * c_mat + _tn(_bf(kw), _bf(t["v"]))
            nm_scr[h] = _state_tile(t["decay"] * n_row + jnp.sum(kw, axis=0, keepdims=True), t["m_new"])

    return pl.pallas_call(
        body, name="mlstm_fwd", grid=(nc,),
        in_specs=[pl.BlockSpec((L, 2048), lambda c: (c, 0)), pl.BlockSpec((L, 2048), lambda c: (c, O_V // 2048)),
                  pl.BlockSpec((L, LANE), lambda c: (c, O_IF // LANE)), pl.BlockSpec((LANE, L), lambda c: (0, c))],
        out_specs=[pl.BlockSpec((L, 2048), lambda c: (c, 0)),
                   pl.BlockSpec((1, ML_HEADS, ML_DQK, ML_DV), lambda c: (c, 0, 0, 0)),
                   pl.BlockSpec((1, ML_HEADS, 8, LANE), lambda c: (c, 0, 0, 0))],
        out_shape=[jax.ShapeDtypeStruct((s, 2048), F32), jax.ShapeDtypeStruct((nc, ML_HEADS, ML_DQK, ML_DV), F32),
                   jax.ShapeDtypeStruct((nc, ML_HEADS, 8, LANE), F32)],
        scratch_shapes=[pltpu.VMEM((ML_HEADS, ML_DQK, ML_DV), F32), pltpu.VMEM((ML_HEADS, 8, LANE), F32),
                        pltpu.VMEM((L, LANE), F32), pltpu.VMEM((LANE, L), F32)],
        compiler_params=_cparams("arbitrary"))(qk, proj, proj, gt)


def _mlstm_bwd(qk, proj, gt, hout, dh, cst, nm):
    s = qk.shape[0]
    L = CHUNK
    nc = s // L

    def body(qk_ref, v_ref, gif_ref, gt_ref, h_ref, dh_ref, cst_ref, nm_ref, dqk_ref, dv_ref, dif_ref,
             dc_scr, dn_scr, a_scr, at_scr):
        @pl.when(pl.program_id(0) == 0)
        def _():
            dc_scr[...] = jnp.zeros_like(dc_scr)
            dn_scr[...] = jnp.zeros_like(dn_scr)
        a_last_row = _mlstm_gates(gif_ref, gt_ref, a_scr, at_scr)
        gif, a = gif_ref[...], a_scr[...]
        lane = lax.broadcasted_iota(jnp.int32, (L, LANE), 1)
        last = lax.broadcasted_iota(jnp.int32, (L, 1), 0) == L - 1
        di_tile = jnp.zeros((L, LANE), F32)
        cross = [jnp.zeros((L, LANE), F32)] * 3
        for h in range(ML_HEADS):
            c_mat, n_row = cst_ref[0, h], nm_ref[0, h, 0:1, :]
            m_prev = jnp.max(nm_ref[0, h, 1:2, :], axis=1, keepdims=True)
            t = _mlstm_head(h, qk_ref, v_ref, gif, gt_ref, a, at_scr, a_last_row, c_mat, n_row, m_prev)
            q, k, v, den = t["q"], t["k"], t["v"], t["den"]
            dhh = dh_ref[:, h * ML_DV:(h + 1) * ML_DV]
            hh = h_ref[:, h * ML_DV:(h + 1) * ML_DV]
            dnorm = jnp.maximum(jnp.abs(den), t["floor"])
            dnum = dhh / dnorm
            d_dn = -jnp.sum(dhh * hh, axis=1, keepdims=True) / dnorm
            dden = jnp.where(jnp.abs(den) >= t["floor"], jnp.where(den >= 0.0, d_dn, -d_dn), 0.0)
            dsc = _nt(_bf(dnum), _bf(v)) + dden
            ds = dsc * t["w_intra"]
            dq_inter = t["w_inter"] * (_nt(_bf(dnum), _bf(c_mat)) + dden * n_row)
            dq = _nn(_bf(ds), _bf(k)) + dq_inter
            dc, dn_row = dc_scr[h], dn_scr[h, 0:1, :]
            dk_state = t["wk"] * (_nt(_bf(v), _bf(dc)) + dn_row)
            dk = _tn(_bf(ds), _bf(q)) + dk_state
            dv = _tn(_bf(t["sc"]), _bf(dnum)) + t["wk"] * _nn(_bf(k), _bf(dc))
            qi = q * t["w_inter"]
            dc_scr[h] = t["decay"] * dc + _tn(_bf(qi), _bf(dnum))
            dn_scr[h] = jnp.broadcast_to(t["decay"] * dn_row + jnp.sum(qi * dden, axis=0, keepdims=True), (8, LANE))
            dqk_ref[:, h * ML_DQK:(h + 1) * ML_DQK] = dq * (ML_DQK ** -0.5)
            dqk_ref[:, (ML_HEADS + h) * ML_DQK:(ML_HEADS + h + 1) * ML_DQK] = dk
            dv_ref[:, h * ML_DV:(h + 1) * ML_DV] = _bf(dv)
            di_tile = di_tile + jnp.where(lane == h, jnp.sum(k * dk, axis=1, keepdims=True), 0.0)
            carried = t["decay"] * (_sum_all(dc * c_mat) + jnp.sum(dn_row * n_row, axis=1, keepdims=True))
            parts = (_crossing(dsc * t["sc"]),
                     jnp.sum(q * dq_inter, axis=1, keepdims=True) + jnp.where(last, carried, 0.0),
                     jnp.sum(k * dk_state, axis=1, keepdims=True))
            cross = [c + jnp.where(lane == ML_HEADS + h, p, 0.0) for c, p in zip(cross, parts)]
        dfb = cross[0] + _nn(_tri(L, True), cross[1], HI) + _nn(_tri(L, False) - _eye(L), cross[2], HI)
        dif_ref[...] = di_tile + dfb * _sigmoid(-gif)

    rev = lambda c: nc - 1 - c
    return pl.pallas_call(
        body, name="mlstm_bwd", grid=(nc,),
        in_specs=[pl.BlockSpec((L, 2048), lambda c: (rev(c), 0)), pl.BlockSpec((L, 2048), lambda c: (rev(c), O_V // 2048)),
                  pl.BlockSpec((L, LANE), lambda c: (rev(c), O_IF // LANE)), pl.BlockSpec((LANE, L), lambda c: (0, rev(c))),
                  pl.BlockSpec((L, 2048), lambda c: (rev(c), 0)), pl.BlockSpec((L, 2048), lambda c: (rev(c), 0)),
                  pl.BlockSpec((1, ML_HEADS, ML_DQK, ML_DV), lambda c: (rev(c), 0, 0, 0)),
                  pl.BlockSpec((1, ML_HEADS, 8, LANE), lambda c: (rev(c), 0, 0, 0))],
        out_specs=[pl.BlockSpec((L, 2048), lambda c: (rev(c), 0)), pl.BlockSpec((L, 2048), lambda c: (rev(c), 0)),
                   pl.BlockSpec((L, LANE), lambda c: (rev(c), 0))],
        out_shape=[jax.ShapeDtypeStruct((s, 2048), F32), jax.ShapeDtypeStruct((s, 2048), BF16),
                   jax.ShapeDtypeStruct((s, LANE), F32)],
        scratch_shapes=[pltpu.VMEM((ML_HEADS, ML_DQK, ML_DV), F32), pltpu.VMEM((ML_HEADS, 8, LANE), F32),
                        pltpu.VMEM((L, LANE), F32), pltpu.VMEM((LANE, L), F32)],
        compiler_params=_cparams("arbitrary"))(qk, proj, proj, gt, hout, dh, cst, nm)


GROUP_W = SSM_HEADS // SSM_GROUPS * SSM_HEADDIM
O_B = SSM_HEADS * SSM_HEADDIM
O_C = O_B + SSM_GROUPS * SSM_STATE


def _head_expand():
    r = jnp.arange(LANE)[:, None]
    c = jnp.arange(SSM_HEADS * SSM_HEADDIM)[None, :] // SSM_HEADDIM
    return (r == c).astype(F32)


def _ssd_gates(dt_ref, dtt_ref, alog_row_ref, alog_col_ref, at_scr):
    L = dt_ref.shape[0]
    dt = _softplus(dt_ref[...])
    acoef = -jnp.exp(alog_row_ref[...])
    a = _nn(_tri(L, False), dt * acoef, HI)
    at_scr[...] = _nn(_softplus(dtt_ref[...]) * (-jnp.exp(alog_col_ref[...])), _tri(L, True), HI)
    return dt, acoef, a


def _ssd_group(g, xbc_ref, dt, a, e_ref, ax_scr):
    eg = e_ref[:, g * GROUP_W:(g + 1) * GROUP_W]
    ax_scr[...] = _nn(a, eg, HI)
    ax = ax_scr[...]
    alx = ax_scr[ax.shape[0] - 1:ax.shape[0], :]
    dtx = _nn(dt, eg, HI)
    xg = xbc_ref[:, g * GROUP_W:(g + 1) * GROUP_W]
    bg = xbc_ref[:, O_B + g * SSM_STATE:O_B + (g + 1) * SSM_STATE]
    cg = xbc_ref[:, O_C + g * SSM_STATE:O_C + (g + 1) * SSM_STATE]
    return dict(ax=ax, alx=alx, dtx=dtx, xg=xg, bg=bg, cg=cg, xdt=xg * dtx, gmat=_nt(_bf(cg), _bf(bg)))


def _ssd_decay(hh, a, at_scr):
    L = a.shape[0]
    causal = lax.broadcasted_iota(jnp.int32, (L, L), 0) >= lax.broadcasted_iota(jnp.int32, (L, L), 1)
    return jnp.exp(jnp.where(causal, _lane_col(a, hh) - at_scr[hh:hh + 1, :], NEG))


def _ssd_fwd(xbc, proj, dtt, alog_row, alog_col, dskip_x, expand):
    s = xbc.shape[0]
    L = CHUNK
    nc = s // L
    half = SSM_HEADDIM

    def body(xbc_ref, dt_ref, dtt_ref, ar_ref, ac_ref, dk_ref, e_ref, y_ref, st_ref, st_scr, at_scr, ax_scr):
        @pl.when(pl.program_id(0) == 0)
        def _():
            st_scr[...] = jnp.zeros_like(st_scr)
        dt, _, a = _ssd_gates(dt_ref, dtt_ref, ar_ref, ac_ref, at_scr)
        lane = lax.broadcasted_iota(jnp.int32, (L, LANE), 1)
        for g in range(SSM_GROUPS):
            t = _ssd_group(g, xbc_ref, dt, a, e_ref, ax_scr)
            st = st_scr[g]
            st_ref[0, g] = st
            pairs = []
            for j in range(GROUP_W // LANE):
                xp = _bf(t["xdt"][:, j * LANE:(j + 1) * LANE])
                hh = g * (SSM_HEADS // SSM_GROUPS) + 2 * j
                y0 = _nn(_bf(t["gmat"] * _ssd_decay(hh, a, at_scr)), xp)
                y1 = _nn(_bf(t["gmat"] * _ssd_decay(hh + 1, a, at_scr)), xp)
                pairs.append(jnp.where(lane < half, y0, y1))
            y = jnp.concatenate(pairs, axis=1) + _nn(_bf(t["cg"]), _bf(st)) * jnp.exp(t["ax"])
            y_ref[:, g * GROUP_W:(g + 1) * GROUP_W] = y + dk_ref[:, g * GROUP_W:(g + 1) * GROUP_W] * t["xg"]
            wts = jnp.exp(t["alx"] - t["ax"])
            st_scr[g] = jnp.exp(t["alx"]) * st + _tn(_bf(t["bg"]), _bf(t["xdt"] * wts))

    row = lambda w: pl.BlockSpec((1, w), lambda c: (0, 0))
    return pl.pallas_call(
        body, name="ssd_fwd", grid=(nc,),
        in_specs=[pl.BlockSpec((L, 3072), lambda c: (c, 0)), pl.BlockSpec((L, LANE), lambda c: (c, O_DT // LANE)),
                  pl.BlockSpec((LANE, L), lambda c: (0, c)), row(LANE), pl.BlockSpec((LANE, 1), lambda c: (0, 0)),
                  row(2048), pl.BlockSpec((LANE, 2048), lambda c: (0, 0))],
        out_specs=[pl.BlockSpec((L, 2048), lambda c: (c, 0)),
                   pl.BlockSpec((1, SSM_GROUPS, SSM_STATE, GROUP_W), lambda c: (c, 0, 0, 0))],
        out_shape=[jax.ShapeDtypeStruct((s, 2048), F32),
                   jax.ShapeDtypeStruct((nc, SSM_GROUPS, SSM_STATE, GROUP_W), F32)],
        scratch_shapes=[pltpu.VMEM((SSM_GROUPS, SSM_STATE, GROUP_W), F32), pltpu.VMEM((LANE, L), F32),
                        pltpu.VMEM((L, GROUP_W), F32)],
        compiler_params=_cparams("arbitrary"))(xbc, proj, dtt, alog_row, alog_col, dskip_x, expand)


def _ssd_bwd(xbc, proj, dtt, alog_row, alog_col, dskip_x, expand, expand_t, dy, states):
    s = xbc.shape[0]
    L = CHUNK
    nc = s // L
    half = SSM_HEADDIM

    def body(xbc_ref, dt_ref, dtt_ref, ar_ref, ac_ref, dk_ref, e_ref, et_ref, dy_ref, st_ref,
             dxbc_ref, ddt_ref, accd_ref, acca_ref, dst_scr, at_scr, ax_scr):
        @pl.when(pl.program_id(0) == 0)
        def _():
            dst_scr[...] = jnp.zeros_like(dst_scr)
            accd_ref[...] = jnp.zeros_like(accd_ref)
            acca_ref[...] = jnp.zeros_like(acca_ref)
        dt, acoef, a = _ssd_gates(dt_ref, dtt_ref, ar_ref, ac_ref, at_scr)
        lane = lax.broadcasted_iota(jnp.int32, (L, LANE), 1)
        low = lane < half
        last = lax.broadcasted_iota(jnp.int32, (L, 1), 0) == L - 1
        cross = [jnp.zeros((L, LANE), F32)] * 3
        ddt_tile = jnp.zeros((L, LANE), F32)
        for g in range(SSM_GROUPS):
            t = _ssd_group(g, xbc_ref, dt, a, e_ref, ax_scr)
            xg, bg, cg, xdt, gmat = t["xg"], t["bg"], t["cg"], t["xdt"], t["gmat"]
            st, dst = st_ref[0, g], dst_scr[g]
            dyg = dy_ref[:, g * GROUP_W:(g + 1) * GROUP_W]
            ea, eal = jnp.exp(t["ax"]), jnp.exp(t["alx"])
            wts = jnp.exp(t["alx"] - t["ax"])
            dyi = dyg * ea
            y_inter = _nn(_bf(cg), _bf(st)) * ea
            dc = _nt(_bf(dyi), _bf(st))
            d_xdt_state = _nn(_bf(bg), _bf(dst)) * wts
            db = _nt(_bf(xdt * wts), _bf(dst))
            dst_scr[g] = eal * dst + _tn(_bf(cg), _bf(dyi))
            dg = jnp.zeros((L, L), F32)
            dx_pairs = []
            for j in range(GROUP_W // LANE):
                xp = _bf(xdt[:, j * LANE:(j + 1) * LANE])
                dyp = dyg[:, j * LANE:(j + 1) * LANE]
                dxs = []
                for b in range(2):
                    hh = g * (SSM_HEADS // SSM_GROUPS) + 2 * j + b
                    dec = _ssd_decay(hh, a, at_scr)
                    w = gmat * dec
                    dxs.append(_tn(_bf(w), _bf(dyp)))
                    dw = _nt(_bf(jnp.where(low if b == 0 else ~low, dyp, 0.0)), xp)
                    dg = dg + dw * dec
                    cross[0] = cross[0] + jnp.where(lane == hh, _crossing(dw * w), 0.0)
                dx_pairs.append(jnp.where(low, dxs[0], dxs[1]))
            d_xdt = d_xdt_state + jnp.concatenate(dx_pairs, axis=1)
            dc = dc + _nn(_bf(dg), _bf(bg))
            db = db + _tn(_bf(dg), _bf(cg))
            etg = et_ref[g * GROUP_W:(g + 1) * GROUP_W, :]
            carried = jnp.sum(dst * st, axis=0, keepdims=True) * eal
            cross[1] = cross[1] + _nn(dyg * y_inter + jnp.where(last, carried, 0.0), etg, HI)
            cross[2] = cross[2] + _nn(xdt * d_xdt_state, etg, HI)
            ddt_tile = ddt_tile + _nn(d_xdt * xg, etg, HI)
            dxbc_ref[:, g * GROUP_W:(g + 1) * GROUP_W] = d_xdt * t["dtx"] + dk_ref[:, g * GROUP_W:(g + 1) * GROUP_W] * dyg
            dxbc_ref[:, O_B + g * SSM_STATE:O_B + (g + 1) * SSM_STATE] = db
            dxbc_ref[:, O_C + g * SSM_STATE:O_C + (g + 1) * SSM_STATE] = dc
            accd_ref[0:1, g * GROUP_W:(g + 1) * GROUP_W] += jnp.sum(dyg * xg, axis=0, keepdims=True)
        d_da = cross[0] + _nn(_tri(L, True), cross[1], HI) + _nn(_tri(L, False) - _eye(L), cross[2], HI)
        acca_ref[0:1, :] += jnp.sum(d_da * dt, axis=0, keepdims=True)
        ddt_ref[...] = (ddt_tile + d_da * acoef) * _sigmoid(dt_ref[...])

    rev = lambda c: nc - 1 - c
    row = lambda w: pl.BlockSpec((1, w), lambda c: (0, 0))
    return pl.pallas_call(
        body, name="ssd_bwd", grid=(nc,),
        in_specs=[pl.BlockSpec((L, 3072), lambda c: (rev(c), 0)), pl.BlockSpec((L, LANE), lambda c: (rev(c), O_DT // LANE)),
                  pl.BlockSpec((LANE, L), lambda c: (0, rev(c))), row(LANE), pl.BlockSpec((LANE, 1), lambda c: (0, 0)),
                  row(2048), pl.BlockSpec((LANE, 2048), lambda c: (0, 0)), pl.BlockSpec((2048, LANE), lambda c: (0, 0)),
                  pl.BlockSpec((L, 2048), lambda c: (rev(c), 0)),
                  pl.BlockSpec((1, SSM_GROUPS, SSM_STATE, GROUP_W), lambda c: (rev(c), 0, 0, 0))],
        out_specs=[pl.BlockSpec((L, 3072), lambda c: (rev(c), 0)), pl.BlockSpec((L, LANE), lambda c: (rev(c), 0)),
                   pl.BlockSpec((8, 2048), lambda c: (0, 0)), pl.BlockSpec((8, LANE), lambda c: (0, 0))],
        out_shape=[jax.ShapeDtypeStruct((s, 3072), F32), jax.ShapeDtypeStruct((s, LANE), F32),
                   jax.ShapeDtypeStruct((8, 2048), F32), jax.ShapeDtypeStruct((8, LANE), F32)],
        scratch_shapes=[pltpu.VMEM((SSM_GROUPS, SSM_STATE, GROUP_W), F32),
                        pltpu.VMEM((LANE, L), F32), pltpu.VMEM((L, GROUP_W), F32)],
        compiler_params=_cparams("arbitrary"))(xbc, proj, dtt, alog_row, alog_col, dskip_x, expand, expand_t, dy, states)


def _group_norm(v, width):
    outs, rs = [], []
    for k in range(v.shape[1] // width):
        blk = v[:, k * width:(k + 1) * width]
        r = lax.rsqrt(jnp.mean(blk * blk, axis=1, keepdims=True) + EPS)
        outs.append(blk * r)
        rs.append(jnp.broadcast_to(r, blk.shape))
    return jnp.concatenate(outs, axis=1), jnp.concatenate(rs, axis=1)


def _group_mean(v, width):
    return jnp.concatenate([jnp.broadcast_to(jnp.mean(v[:, k * width:(k + 1) * width], axis=1, keepdims=True),
                                             (v.shape[0], width)) for k in range(v.shape[1] // width)], axis=1)


def _post_fwd(hm, yssd, proj, ml_norm_w, ssm_norm_w, ts):
    s = hm.shape[0]

    def body(h_ref, ys_ref, o_ref, zm_ref, zs_ref, wm_ref, ws_ref, ym_ref, yso_ref):
        hn, _ = _group_norm(h_ref[...], ML_DV)
        ym_ref[...] = _bf(_sigmoid(o_ref[...]) * hn * wm_ref[...] * _silu(zm_ref[...]))
        pn, _ = _group_norm(ys_ref[...] * _silu(zs_ref[...]), GROUP_W)
        yso_ref[...] = _bf(pn * ws_ref[...])

    tile = pl.BlockSpec((ts, 2048), lambda i: (i, 0))
    col = lambda off: pl.BlockSpec((ts, 2048), lambda i: (i, off // 2048))
    row = pl.BlockSpec((1, 2048), lambda i: (0, 0))
    return pl.pallas_call(
        body, name="post_fwd", grid=(s // ts,),
        in_specs=[tile, tile, col(O_O), col(O_ZM), col(O_ZS), row, row],
        out_specs=[tile, tile],
        out_shape=[jax.ShapeDtypeStruct((s, 2048), BF16)] * 2,
        compiler_params=_cparams("parallel"))(hm, yssd, proj, proj, proj, ml_norm_w, ssm_norm_w)


def _post_bwd(dym, dys, hm, yssd, proj, ml_norm_w, ssm_norm_w, ts):
    s = hm.shape[0]

    def body(dym_ref, dys_ref, h_ref, ys_ref, o_ref, zm_ref, zs_ref, wm_ref, ws_ref,
             dh_ref, dyssd_ref, do_ref, dzm_ref, dzs_ref, acc_ref):
        @pl.when(pl.program_id(0) == 0)
        def _():
            acc_ref[...] = jnp.zeros_like(acc_ref)
        hn, r = _group_norm(h_ref[...], ML_DV)
        so, zm, wm, d_ym = _sigmoid(o_ref[...]), zm_ref[...], wm_ref[...], dym_ref[...]
        sz = _silu(zm)
        hnw = hn * wm
        do_ref[...] = _bf(d_ym * hnw * sz * so * (1.0 - so))
        dzm_ref[...] = _bf(d_ym * so * hnw * _dsilu(zm))
        dhnw = d_ym * so * sz
        acc_ref[0:1, :] += jnp.sum(dhnw * hn, axis=0, keepdims=True)
        dhn = dhnw * wm
        dh_ref[...] = r * (dhn - hn * _group_mean(dhn * hn, ML_DV))
        ysv, zs, d_ys = ys_ref[...], zs_ref[...], dys_ref[...]
        szs = _silu(zs)
        pn, r2 = _group_norm(ysv * szs, GROUP_W)
        acc_ref[1:2, :] += jnp.sum(d_ys * pn, axis=0, keepdims=True)
        dpn = d_ys * ws_ref[...]
        dp = r2 * (dpn - pn * _group_mean(dpn * pn, GROUP_W))
        dyssd_ref[...] = dp * szs
        dzs_ref[...] = _bf(dp * ysv * _dsilu(zs))

    tile = pl.BlockSpec((ts, 2048), lambda i: (i, 0))
    col = lambda off: pl.BlockSpec((ts, 2048), lambda i: (i, off // 2048))
    row = pl.BlockSpec((1, 2048), lambda i: (0, 0))
    sds = lambda dt: jax.ShapeDtypeStruct((s, 2048), dt)
    return pl.pallas_call(
        body, name="post_bwd", grid=(s // ts,),
        in_specs=[tile, tile, tile, tile, col(O_O), col(O_ZM), col(O_ZS), row, row],
        out_specs=[tile, tile, tile, tile, tile, pl.BlockSpec((8, 2048), lambda i: (0, 0))],
        out_shape=[sds(F32), sds(F32), sds(BF16), sds(BF16), sds(BF16), jax.ShapeDtypeStruct((8, 2048), F32)],
        compiler_params=_cparams("arbitrary"))(dym, dys, hm, yssd, proj, proj, proj, ml_norm_w, ssm_norm_w)


def _merge(x, ym, ys, proj, target, gate, final_w, wpm, wps, wo, ts):
    s, d = x.shape

    def body(x_ref, ym_ref, ys_ref, mg_ref, t_ref, gate_ref, fw_ref, wpm_ref, wps_ref, wo_ref,
             dres_ref, mer_ref, dmo_ref, dpm_ref, dps_ref, dym_ref, dys_ref, dmg_ref, acc_ref):
        @pl.when(pl.program_id(0) == 0)
        def _():
            acc_ref[...] = jnp.zeros_like(acc_ref)
        gm, gs = _sigmoid(mg_ref[:, 0:d]), _sigmoid(mg_ref[:, d:2 * d])
        pm = _nn(ym_ref[...], wpm_ref[...])
        ps = _nn(ys_ref[...], wps_ref[...])
        merged = _bf(gm * pm + gs * ps)
        mer_ref[...] = merged
        mo = _nn(merged, wo_ref[...])
        gate, fw = gate_ref[...], fw_ref[...]
        out = x_ref[...] + gate * mo
        r = lax.rsqrt(jnp.mean(out * out, axis=1, keepdims=True) + EPS)
        on = out * r
        diff = on * fw - t_ref[...]
        acc_ref[0:1, :] += jnp.sum(0.5 * jnp.sum(diff * diff, axis=1, keepdims=True) / d, axis=0, keepdims=True)
        dyv = diff * (1.0 / d)
        acc_ref[1:2, :] += jnp.sum(dyv * on, axis=0, keepdims=True)
        don = dyv * fw
        dout = r * (don - on * jnp.mean(don * on, axis=1, keepdims=True))
        dres_ref[...] = dout
        acc_ref[2:3, :] += jnp.sum(dout * mo, axis=0, keepdims=True)
        dmo = _bf(dout * gate)
        dmo_ref[...] = dmo
        dmer = _nt(dmo, wo_ref[...])
        dpm, dps = _bf(dmer * gm), _bf(dmer * gs)
        dpm_ref[...] = dpm
        dps_ref[...] = dps
        dmg_ref[:, 0:d] = _bf(dmer * pm * gm * (1.0 - gm))
        dmg_ref[:, d:2 * d] = _bf(dmer * ps * gs * (1.0 - gs))
        dym_ref[...] = _nt(dpm, wpm_ref[...])
        dys_ref[...] = _nt(dps, wps_ref[...])

    t1 = pl.BlockSpec((ts, d), lambda i: (i, 0))
    t2 = pl.BlockSpec((ts, 2 * d), lambda i: (i, 0))
    row = pl.BlockSpec((1, d), lambda i: (0, 0))
    whole = pl.BlockSpec(memory_space=pltpu.VMEM)
    sd = lambda w, dt: jax.ShapeDtypeStruct((s, w), dt)
    return pl.pallas_call(
        body, name="merge_fwd_bwd", grid=(s // ts,),
        in_specs=[t1, t2, t2, pl.BlockSpec((ts, 2 * d), lambda i: (i, O_MG // (2 * d))), t1, row, row, whole, whole, whole],
        out_specs=[t1, t1, t1, t1, t1, t2, t2, t2, pl.BlockSpec((8, d), lambda i: (0, 0))],
        out_shape=[sd(d, F32), sd(d, BF16), sd(d, BF16), sd(d, BF16), sd(d, BF16), sd(2 * d, F32), sd(2 * d, F32),
                   sd(2 * d, BF16), jax.ShapeDtypeStruct((8, d), F32)],
        compiler_params=_cparams("arbitrary"))(x, ym, ys, proj, target, gate, final_w, wpm, wps, wo)


def _adamw(w, g, m, v, tr):
    rows, cols = w.shape

    def body(w_ref, g_ref, m_ref, v_ref, d_ref, nm_ref, nv_ref):
        gv = g_ref[...]
        m2 = ADAM_B1 * m_ref[...] + (1.0 - ADAM_B1) * gv
        v2 = ADAM_B2 * v_ref[...] + (1.0 - ADAM_B2) * (gv * gv)
        m_hat = m2 / (1.0 - ADAM_B1 ** ADAM_STEP)
        v_hat = v2 / (1.0 - ADAM_B2 ** ADAM_STEP)
        d_ref[...] = -ADAM_LR * (m_hat / (jnp.sqrt(v_hat) + ADAM_EPS) + ADAM_WD * w_ref[...])
        nm_ref[...] = m2
        nv_ref[...] = v2

    tile = pl.BlockSpec((tr, cols), lambda i: (i, 0))
    return pl.pallas_call(
        body, name="adamw", grid=(rows // tr,), in_specs=[tile] * 4, out_specs=[tile] * 3,
        out_shape=[jax.ShapeDtypeStruct((rows, cols), F32)] * 3,
        compiler_params=_cparams("parallel"))(w, g, m, v)


def _sum_parts(own, parts, tr):
    p, rows, cols = parts.shape

    def body(*refs):
        p_ref, o_ref = refs[-2], refs[-1]
        acc = p_ref[0] if own is None else refs[0][...] + p_ref[0]
        for i in range(1, p):
            acc = acc + p_ref[i]
        o_ref[...] = acc

    tile = pl.BlockSpec((tr, cols), lambda i: (i, 0))
    ins = ([] if own is None else [tile]) + [pl.BlockSpec((p, tr, cols), lambda i: (0, i, 0))]
    args = ([] if own is None else [own]) + [parts]
    return pl.pallas_call(
        body, name="sum_parts", grid=(rows // tr,), in_specs=ins, out_specs=tile,
        out_shape=jax.ShapeDtypeStruct((rows, cols), F32), compiler_params=_cparams("parallel"))(*args)


def _position():
    return lax.axis_index("x"), lax.axis_index("y"), lax.axis_index("c")


def _flip(pos, k):
    return tuple(1 - p if (k >> s) & 1 else p for p, s in zip(pos, (2, 1, 0)))


def _allgather8(block):
    rows, cols = block.shape

    def body(x_ref, o_ref, send_sems, recv_sems, local_sem):
        pos = _position()
        me = 4 * pos[0] + 2 * pos[1] + pos[2]
        mine = pltpu.make_async_copy(x_ref, o_ref.at[me], local_sem)
        mine.start()
        copies = [pltpu.make_async_remote_copy(src_ref=x_ref, dst_ref=o_ref.at[me], send_sem=send_sems.at[k - 1],
                                               recv_sem=recv_sems.at[k - 1], device_id=_flip(pos, k), device_id_type=MESH)
                  for k in range(1, N_DEV)]
        for cp in copies:
            cp.start()
        for cp in copies:
            cp.wait()
        mine.wait()

    vmem = pl.BlockSpec(memory_space=pltpu.VMEM)
    return pl.pallas_call(
        body, name="allgather8", in_specs=[vmem], out_specs=vmem,
        out_shape=jax.ShapeDtypeStruct((N_DEV, rows, cols), block.dtype),
        scratch_shapes=[pltpu.SemaphoreType.DMA((N_DEV - 1,)), pltpu.SemaphoreType.DMA((N_DEV - 1,)),
                        pltpu.SemaphoreType.DMA],
        compiler_params=pltpu.CompilerParams(vmem_limit_bytes=VMEM_LIMIT))(block)


def _weight_gather(shards):
    n = len(shards)

    def body(*refs):
        ins, outs = refs[:n], refs[n:2 * n]
        send_sems, recv_sems, local_sems = refs[2 * n:]
        pos = _position()
        chip = 2 * pos[0] + pos[1]
        copies = []
        for a in range(n):
            mine = pltpu.make_async_copy(ins[a], outs[a].at[chip], local_sems.at[a])
            mine.start()
            copies.append(mine)
            for k in range(1, N_CHIPS):
                cp = pltpu.make_async_remote_copy(src_ref=ins[a], dst_ref=outs[a].at[chip], send_sem=send_sems.at[a, k - 1],
                                                  recv_sem=recv_sems.at[a, k - 1], device_id=_flip(pos, 2 * k),
                                                  device_id_type=MESH)
                cp.start()
                copies.append(cp)
        for cp in copies:
            cp.wait()

    hbm = pl.BlockSpec(memory_space=pl.ANY)
    return pl.pallas_call(
        body, name="weight_gather", in_specs=[hbm] * n, out_specs=[hbm] * n,
        out_shape=[jax.ShapeDtypeStruct((N_CHIPS,) + a.shape, a.dtype) for a in shards],
        scratch_shapes=[pltpu.SemaphoreType.DMA((n, N_CHIPS - 1)), pltpu.SemaphoreType.DMA((n, N_CHIPS - 1)),
                        pltpu.SemaphoreType.DMA((n,))],
        compiler_params=pltpu.CompilerParams(has_side_effects=True))(*shards)


def _grad_scatter(grads):
    n = len(grads)

    def body(*refs):
        ins, outs = refs[:n], refs[n:2 * n]
        send_sems, recv_sems = refs[2 * n:]
        pos = _position()
        copies = []
        for a in range(n):
            for k in range(1, N_DEV):
                to = _flip(pos, k)
                cp = pltpu.make_async_remote_copy(src_ref=ins[a].at[2 * to[0] + to[1], to[2]], dst_ref=outs[a].at[k - 1],
                                                  send_sem=send_sems.at[a, k - 1], recv_sem=recv_sems.at[a, k - 1],
                                                  device_id=to, device_id_type=MESH)
                cp.start()
                copies.append(cp)
        for cp in copies:
            cp.wait()

    hbm = pl.BlockSpec(memory_space=pl.ANY)
    return pl.pallas_call(
        body, name="grad_scatter", in_specs=[hbm] * n, out_specs=[hbm] * n,
        out_shape=[jax.ShapeDtypeStruct((N_DEV - 1,) + g.shape[2:], g.dtype) for g in grads],
        scratch_shapes=[pltpu.SemaphoreType.DMA((n, N_DEV - 1)), pltpu.SemaphoreType.DMA((n, N_DEV - 1))],
        compiler_params=pltpu.CompilerParams(has_side_effects=True))(*grads)


def _pair_exchange(halves):
    n = len(halves)

    def body(*refs):
        ins, outs = refs[:n], refs[n:2 * n]
        send_sems, recv_sems, local_sems = refs[2 * n:]
        pos = _position()
        copies = []
        for a in range(n):
            mine = pltpu.make_async_copy(ins[a], outs[a].at[pos[2]], local_sems.at[a])
            mine.start()
            cp = pltpu.make_async_remote_copy(src_ref=ins[a], dst_ref=outs[a].at[pos[2]], send_sem=send_sems.at[a],
                                              recv_sem=recv_sems.at[a], device_id=_flip(pos, 1), device_id_type=MESH)
            cp.start()
            copies += [mine, cp]
        for cp in copies:
            cp.wait()

    hbm = pl.BlockSpec(memory_space=pl.ANY)
    return pl.pallas_call(
        body, name="pair_exchange", in_specs=[hbm] * n, out_specs=[hbm] * n,
        out_shape=[jax.ShapeDtypeStruct((2,) + h.shape, h.dtype) for h in halves],
        scratch_shapes=[pltpu.SemaphoreType.DMA((n,)), pltpu.SemaphoreType.DMA((n,)), pltpu.SemaphoreType.DMA((n,))],
        compiler_params=pltpu.CompilerParams(has_side_effects=True))(*halves)


def _pack(arrays):
    flat = jnp.concatenate([a.reshape(-1).astype(F32) for a in arrays])
    size = -(-flat.shape[0] // (8 * LANE)) * (8 * LANE)
    return jnp.pad(flat, (0, size - flat.shape[0])).reshape(size // LANE, LANE)


def _unpack(buf, shapes):
    flat = buf.reshape(-1)
    out, off = [], 0
    for shp in shapes:
        n = math.prod(shp)
        out.append(flat[off:off + n].reshape(shp))
        off += n
    return out


def _unpack_rows(bufs, shapes):
    flat = bufs.reshape(bufs.shape[0], -1)
    out, off = [], 0
    for shp in shapes:
        n = math.prod(shp)
        out.append(flat[:, off:off + n].reshape((bufs.shape[0],) + shp))
        off += n
    return out


def _taps8(w):
    return jnp.pad(w, ((0, 8 - CONV_K), (0, 0)))


def _local_step(xs, tgt, scale, shift, gate, norm_w, w_in_p, b_in_p, ml_conv_w, ml_conv_b, ml_norm_w, ssm_conv_w,
                ssm_conv_b, ssm_a_log, ssm_d, ssm_norm_w, wpm, wps, wo, final_w):
    s = xs.shape[0]
    ts = min(512, s)
    tm = min(1024, s)
    u = _prenorm_fwd(xs, norm_w, scale, shift, ts)
    proj = _matmul_bias(u, w_in_p, b_in_p, tm, 512)
    mlw8, ssw8 = _taps8(ml_conv_w), _taps8(ssm_conv_w)
    qk = _conv_fwd(proj, O_QK, 2048, mlw8, ml_conv_b, ts)
    xbc = _conv_fwd(proj, O_XBC, 3072, ssw8, ssm_conv_b, ts)
    gt = proj[:, O_IF:O_IF + LANE].T
    dtt = proj[:, O_DT:O_DT + LANE].T
    hm, cst, nm = _mlstm_fwd(qk, proj, gt)
    alog_row = jnp.pad(ssm_a_log, ((0, 0), (0, LANE - SSM_HEADS)))
    alog_col = alog_row.reshape(LANE, 1)
    dskip_x = jnp.repeat(ssm_d[0], SSM_HEADDIM)[None]
    expand = _head_expand()
    yssd, sst = _ssd_fwd(xbc, proj, dtt, alog_row, alog_col, dskip_x, expand)
    tp = min(128, s)
    ym, ys = _post_fwd(hm, yssd, proj, ml_norm_w, ssm_norm_w, tp)
    dxres, merged, dmo, dpm, dps, dym, dys, dmg, acc_m = _merge(xs, ym, ys, proj, tgt, gate, final_w, wpm, wps, wo, tp)
    dh, dyssd, d_o, d_zm, d_zs, acc_p = _post_bwd(dym, dys, hm, yssd, proj, ml_norm_w, ssm_norm_w, tp)
    dqk, dv, dif = _mlstm_bwd(qk, proj, gt, hm, dh, cst, nm)
    dxbc, ddt, accd, acca = _ssd_bwd(xbc, proj, dtt, alog_row, alog_col, dskip_x, expand, expand.T, dyssd, sst)
    dqk_pre, acc_cq = _conv_bwd(proj, O_QK, 2048, mlw8, ml_conv_b, dqk, ts)
    dxbc_pre, acc_cx = _conv_bwd(proj, O_XBC, 3072, ssw8, ssm_conv_b, dxbc, ts)
    dproj = jnp.concatenate([dqk_pre, dv, d_o, d_zm, d_zs, dmg, dxbc_pre, _bf(dif), _bf(ddt),
                             jnp.zeros((s, NP - O_DT - LANE), BF16)], axis=1)
    gw_in_p, gb_in_p = _matmul_tn(u, dproj, tm, 512, with_colsum=True)
    du = _matmul_nt(dproj, w_in_p, tm, 512)
    grad_x, acc_n = _prenorm_bwd(du, xs, dxres, norm_w, scale, ts)
    g_wpm = _matmul_tn(ym, dpm, tm, 512)
    g_wps = _matmul_tn(ys, dps, tm, 512)
    g_wo = _matmul_tn(merged, dmo, tm, 512)
    a_coef = -jnp.exp(ssm_a_log[0])
    small = dict(
        mod=jnp.concatenate([acc_n[2], acc_n[1], acc_m[2]]), norm_w=acc_n[0], b_in=_unpad_cols(gb_in_p[0]),
        ml_conv_w=acc_cq[0:CONV_K], ml_conv_b=acc_cq[CONV_K], ml_norm_w=acc_p[0], ssm_conv_w=acc_cx[0:CONV_K],
        ssm_conv_b=acc_cx[CONV_K], ssm_a_log=acca[0, :SSM_HEADS] * a_coef,
        ssm_d=accd[0].reshape(SSM_HEADS, SSM_HEADDIM).sum(axis=1), ssm_norm_w=acc_p[1], final_w=acc_m[1], loss=acc_m[0, 0:1])
    return grad_x, small, _unpad_cols(gw_in_p), g_wpm, g_wps, g_wo


WEIGHTS = ("norm_w", "ada_w", "ada_b", "w_in", "b_in", "ml_conv_w", "ml_conv_b", "ml_norm_w", "ssm_conv_w", "ssm_conv_b",
           "ssm_a_log", "ssm_d", "ssm_norm_w", "w_proj_m", "w_proj_s", "w_out", "final_w")
LARGE = ("ada_w", "w_in", "w_proj_m", "w_proj_s", "w_out")
SMALL_SUMS = (("mod", (3 * D_MODEL,)), ("norm_w", (D_MODEL,)), ("b_in", (IN_WIDTH,)), ("ml_conv_w", (CONV_K, 2048)),
              ("ml_conv_b", (2048,)), ("ml_norm_w", (2048,)), ("ssm_conv_w", (CONV_K, 3072)), ("ssm_conv_b", (3072,)),
              ("ssm_a_log", (SSM_HEADS,)), ("ssm_d", (SSM_HEADS,)), ("ssm_norm_w", (2048,)), ("final_w", (D_MODEL,)),
              ("loss", (1,)))


def kernel(x, c, norm_w, ada_w, ada_b, w_in, b_in, ml_conv_w, ml_conv_b, ml_norm_w, ssm_conv_w, ssm_conv_b, ssm_a_log, ssm_d, ssm_norm_w, w_proj_m, w_proj_s, w_out, final_w, loss_target, m_norm_w, m_ada_w, m_ada_b, m_w_in, m_b_in, m_ml_conv_w, m_ml_conv_b, m_ml_norm_w, m_ssm_conv_w, m_ssm_conv_b, m_ssm_a_log, m_ssm_d, m_ssm_norm_w, m_w_proj_m, m_w_proj_s, m_w_out, m_final_w, v_norm_w, v_ada_w, v_ada_b, v_w_in, v_b_in, v_ml_conv_w, v_ml_conv_b, v_ml_norm_w, v_ssm_conv_w, v_ssm_conv_b, v_ssm_a_log, v_ssm_d, v_ssm_norm_w, v_w_proj_m, v_w_proj_s, v_w_out, v_final_w):
    w = dict(norm_w=norm_w, ada_w=ada_w, ada_b=ada_b, w_in=w_in, b_in=b_in, ml_conv_w=ml_conv_w, ml_conv_b=ml_conv_b,
             ml_norm_w=ml_norm_w, ssm_conv_w=ssm_conv_w, ssm_conv_b=ssm_conv_b, ssm_a_log=ssm_a_log, ssm_d=ssm_d,
             ssm_norm_w=ssm_norm_w, w_proj_m=w_proj_m, w_proj_s=w_proj_s, w_out=w_out, final_w=final_w)
    m = dict(zip(WEIGHTS, (m_norm_w, m_ada_w, m_ada_b, m_w_in, m_b_in, m_ml_conv_w, m_ml_conv_b, m_ml_norm_w, m_ssm_conv_w,
                           m_ssm_conv_b, m_ssm_a_log, m_ssm_d, m_ssm_norm_w, m_w_proj_m, m_w_proj_s, m_w_out, m_final_w)))
    v = dict(zip(WEIGHTS, (v_norm_w, v_ada_w, v_ada_b, v_w_in, v_b_in, v_ml_conv_w, v_ml_conv_b, v_ml_norm_w, v_ssm_conv_w,
                           v_ssm_conv_b, v_ssm_a_log, v_ssm_d, v_ssm_norm_w, v_w_proj_m, v_w_proj_s, v_w_out, v_final_w)))
    pos = _position()
    chip = 2 * pos[0] + pos[1]
    dev = 2 * chip + pos[2]
    mlw_cols, ssw_cols, ada_cols = ml_conv_w.shape[2], ssm_conv_w.shape[2], ada_w.shape[2]

    g0 = _allgather8(_pack([c, ml_conv_w, ssm_conv_w]))
    c_all, mlw_all, ssw_all = _unpack_rows(g0, [(D_MODEL,), (CONV_K, mlw_cols), (CONV_K, ssw_cols)])
    ml_conv_full = mlw_all[0::2].transpose(1, 0, 2).reshape(CONV_K, N_CHIPS * mlw_cols)
    ssm_conv_full = ssw_all[0::2].transpose(1, 0, 2).reshape(CONV_K, N_CHIPS * ssw_cols)

    ada_b_mine = lax.dynamic_slice_in_dim(ada_b, chip * ada_cols, ada_cols, axis=1)
    g1 = _allgather8(_ada_fwd(c_all, ada_w[0], ada_b_mine))
    mod = lax.dynamic_index_in_dim(g1[0::2], dev, axis=1, keepdims=False).reshape(1, 3 * D_MODEL)
    shift, scale, gate = mod[:, :D_MODEL], mod[:, D_MODEL:2 * D_MODEL], mod[:, 2 * D_MODEL:]

    gw = _weight_gather([_bf(w_in[0]), _bf(w_proj_m[0]), _bf(w_proj_s[0]), _bf(w_out[0])])
    w_in_p = _pad_cols(gw[0].transpose(1, 0, 2).reshape(D_MODEL, IN_WIDTH))
    wpm, wps, wo = (a.reshape(-1, D_MODEL) for a in gw[1:])

    grad_x, small, g_w_in, g_wpm, g_wps, g_wo = _local_step(
        x[0], loss_target[0], scale, shift, gate, norm_w, w_in_p, _pad_cols(b_in), ml_conv_full, ml_conv_b, ml_norm_w,
        ssm_conv_full, ssm_conv_b, ssm_a_log, ssm_d, ssm_norm_w, wpm, wps, wo, final_w[None])

    g2 = _allgather8(_pack([small[name] for name, _ in SMALL_SUMS]))
    total = dict(zip([name for name, _ in SMALL_SUMS], _unpack(_sum_parts(None, g2, g2.shape[1]), [s for _, s in SMALL_SUMS])))
    dmod_all = g2[:, :3 * D_MODEL // LANE].reshape(N_DEV, 3 * D_MODEL)
    grads = dict(total)
    grads["ada_b"] = total["mod"]
    grads["ml_conv_w"] = lax.dynamic_slice_in_dim(total["ml_conv_w"], chip * mlw_cols, mlw_cols, axis=1)
    grads["ssm_conv_w"] = lax.dynamic_slice_in_dim(total["ssm_conv_w"], chip * ssw_cols, ssw_cols, axis=1)
    grads["ada_w"] = _ada_bwd(c_all, lax.dynamic_slice_in_dim(dmod_all, chip * ada_cols, ada_cols, axis=1))

    split = lambda g, rows: g.reshape(N_CHIPS, 2, rows // (2 * N_CHIPS), g.shape[-1])
    slabs = [split(g_w_in.reshape(D_MODEL, N_CHIPS, -1).transpose(1, 0, 2), N_CHIPS * D_MODEL),
             split(g_wpm, g_wpm.shape[0]), split(g_wps, g_wps.shape[0]), split(g_wo, g_wo.shape[0])]
    received = _grad_scatter(slabs)
    halves = []
    for slab, rec in zip(slabs, received):
        own = lax.dynamic_index_in_dim(lax.dynamic_index_in_dim(slab, chip, 0, keepdims=False), pos[2], 0, keepdims=False)
        halves.append(_sum_parts(own, rec, 32))
    for name, full in zip(("w_in", "w_proj_m", "w_proj_s", "w_out"), _pair_exchange(halves)):
        grads[name] = full.reshape(-1, full.shape[-1])

    delta, new_m, new_v = {}, {}, {}
    for name in LARGE:
        d2, m2, v2 = _adamw(w[name][0], grads[name], m[name][0], v[name][0], 64)
        delta[name], new_m[name], new_v[name] = d2[None], m2[None], v2[None]
    rest = [name for name in WEIGHTS if name not in LARGE]
    packed = [_pack([t[name] for name in rest]) for t in (w, grads, m, v)]
    for out, buf in zip((delta, new_m, new_v), _adamw(*packed, packed[0].shape[0])):
        out.update(zip(rest, _unpack(buf, [w[name].shape for name in rest])))
    loss = total["loss"][0]
    return (loss, grad_x[None], *[grads[name].reshape(w[name].shape) for name in WEIGHTS], *[delta[name] for name in WEIGHTS],
            *[new_m[name] for name in WEIGHTS], *[new_v[name] for name in WEIGHTS])
```

```python
import functools
import math

import jax
import jax.numpy as jnp
from jax import lax
from jax.experimental import pallas as pl
from jax.experimental.pallas import tpu as pltpu

F32 = jnp.float32
BF16 = jnp.bfloat16
HI = lax.Precision.HIGHEST
MESH = pl.DeviceIdType.MESH

D_MODEL = 1024
EPS = 1e-6
CONV_K = 4
ML_HEADS = 8
ML_DQK = 128
ML_DV = 256
SSM_HEADS = 32
SSM_HEADDIM = 64
SSM_GROUPS = 4
SSM_STATE = 128
IN_WIDTH = 15408
N_CHIPS = 4
N_DEV = 8
ADAM_LR, ADAM_B1, ADAM_B2, ADAM_EPS, ADAM_WD, ADAM_STEP = 0.001, 0.9, 0.999, 1e-08, 0.01, 10

O_O, O_ZM, O_ZS, O_MG, O_QK, O_V, O_XBC, O_IF, O_DT = 0, 2048, 4096, 6144, 8192, 10240, 12288, 15360, 15616
SMALL_W = 256
NP = 15872
LANE = 128
CHUNK = 128
NEG = -1e30
VMEM_LIMIT = 48 * 1024 * 1024


def _cparams(*sem):
    return pltpu.CompilerParams(dimension_semantics=sem, vmem_limit_bytes=VMEM_LIMIT)


def _pad_cols(w):
    z = lambda n: jnp.zeros(w.shape[:-1] + (n,), w.dtype)
    return jnp.concatenate([w[..., 4096:8192], w[..., 11280:13328], w[..., 13360:15408], w[..., :4096], w[..., 8208:11280],
                            w[..., 8192:8208], z(SMALL_W - 16), w[..., 13328:13360], z(SMALL_W - 32)], axis=-1)


def _unpad_cols(g):
    return jnp.concatenate([g[..., O_QK:O_QK + 4096], g[..., O_O:O_O + 4096], g[..., O_IF:O_IF + 16],
                            g[..., O_XBC:O_XBC + 3072], g[..., O_ZS:O_ZS + 2048], g[..., O_DT:O_DT + 32],
                            g[..., O_MG:O_MG + 2048]], axis=-1)


def _sigmoid(x):
    return 1.0 / (1.0 + jnp.exp(-x))


def _silu(x):
    return x * _sigmoid(x)


def _dsilu(x):
    s = _sigmoid(x)
    return s + x * s * (1.0 - s)


def _softplus(x):
    return jnp.maximum(x, 0.0) + jnp.log(1.0 + jnp.exp(-jnp.abs(x)))


def _logsigmoid(x):
    return jnp.minimum(x, 0.0) - jnp.log(1.0 + jnp.exp(-jnp.abs(x)))


def _dot(a, b, dims, precision=None):
    return lax.dot_general(a, b, (dims, ((), ())), preferred_element_type=F32, precision=precision)


def _nn(a, b, precision=None):
    return _dot(a, b, ((1,), (0,)), precision)


def _nt(a, b, precision=None):
    return _dot(a, b, ((1,), (1,)), precision)


def _tn(a, b, precision=None):
    return _dot(a, b, ((0,), (0,)), precision)


def _bf(x):
    return x.astype(BF16)


def _split(x, terms):
    parts = []
    for _ in range(terms):
        part = _bf(x)
        parts.append(part)
        x = x - part.astype(F32)
    return parts


def _pick_right(x, pick, terms):
    pick = _bf(pick)
    out = None
    for part in _split(x, terms):
        out = _nn(part, pick) if out is None else out + _nn(part, pick)
    return out


def _pick_left(pick, x, terms):
    pick = _bf(pick)
    out = None
    for part in _split(x, terms):
        out = _nn(pick, part) if out is None else out + _nn(pick, part)
    return out


def _lane_col(x, lane):
    idx = lax.broadcasted_iota(jnp.int32, x.shape, 1)
    return jnp.sum(jnp.where(idx == lane, x, 0.0), axis=1, keepdims=True)


def _tri(n, upper):
    r = lax.broadcasted_iota(jnp.int32, (n, n), 0)
    c = lax.broadcasted_iota(jnp.int32, (n, n), 1)
    return jnp.where((r <= c) if upper else (r >= c), 1.0, 0.0).astype(F32)


def _eye(n):
    return jnp.where(lax.broadcasted_iota(jnp.int32, (n, n), 0) == lax.broadcasted_iota(jnp.int32, (n, n), 1), 1.0, 0.0)


def _sum_all(x):
    return jnp.sum(jnp.sum(x, axis=1, keepdims=True), axis=0, keepdims=True)


def _crossing(p):
    L = p.shape[0]
    hi = _bf(p)
    lo = _bf(p - hi.astype(F32))
    upper = _bf(_tri(L, True))
    below = _nn(upper, hi) + _nn(upper, lo)
    strict = lax.broadcasted_iota(jnp.int32, (L, L), 0) > lax.broadcasted_iota(jnp.int32, (L, L), 1)
    return jnp.sum(jnp.where(strict, below, 0.0), axis=1, keepdims=True)


def _matmul_bias(a, w, bias, tm, tn):
    m, k = a.shape
    n = w.shape[1]

    def body(a_ref, w_ref, b_ref, o_ref):
        o_ref[...] = _nn(a_ref[...], w_ref[...]) + b_ref[...]

    return pl.pallas_call(
        body, name="matmul_bias", grid=(m // tm, n // tn),
        in_specs=[pl.BlockSpec((tm, k), lambda i, j: (i, 0)), pl.BlockSpec((k, tn), lambda i, j: (0, j)),
                  pl.BlockSpec((1, tn), lambda i, j: (0, j))],
        out_specs=pl.BlockSpec((tm, tn), lambda i, j: (i, j)),
        out_shape=jax.ShapeDtypeStruct((m, n), F32),
        compiler_params=_cparams("parallel", "arbitrary"))(a, w, bias)


def _matmul_nt(a, w, tm, tk):
    m, n = a.shape
    k = w.shape[0]

    def body(a_ref, w_ref, o_ref):
        @pl.when(pl.program_id(1) == 0)
        def _():
            o_ref[...] = jnp.zeros_like(o_ref)
        o_ref[...] += _nt(a_ref[...], w_ref[...])

    return pl.pallas_call(
        body, name="matmul_nt", grid=(m // tm, n // tk),
        in_specs=[pl.BlockSpec((tm, tk), lambda i, j: (i, j)), pl.BlockSpec((k, tk), lambda i, j: (0, j))],
        out_specs=pl.BlockSpec((tm, k), lambda i, j: (i, 0)),
        out_shape=jax.ShapeDtypeStruct((m, k), F32),
        compiler_params=_cparams("parallel", "arbitrary"))(a, w)


def _matmul_tn(a, b, tm, tn, with_colsum=False):
    m, k = a.shape
    n = b.shape[1]

    def body(a_ref, b_ref, o_ref, *rest):
        first = pl.program_id(1) == 0

        @pl.when(first)
        def _():
            o_ref[...] = jnp.zeros_like(o_ref)
        o_ref[...] += _tn(a_ref[...], b_ref[...])
        if with_colsum:
            s_ref = rest[0]

            @pl.when(first)
            def _():
                s_ref[...] = jnp.zeros_like(s_ref)
            s_ref[...] += jnp.sum(b_ref[...].astype(F32), axis=0, keepdims=True)

    out_specs = [pl.BlockSpec((k, tn), lambda j, i: (0, j))]
    out_shape = [jax.ShapeDtypeStruct((k, n), F32)]
    if with_colsum:
        out_specs.append(pl.BlockSpec((1, tn), lambda j, i: (0, j)))
        out_shape.append(jax.ShapeDtypeStruct((1, n), F32))
    out = pl.pallas_call(
        body, name="matmul_tn", grid=(n // tn, m // tm),
        in_specs=[pl.BlockSpec((tm, k), lambda j, i: (i, 0)), pl.BlockSpec((tm, tn), lambda j, i: (i, j))],
        out_specs=out_specs, out_shape=out_shape,
        compiler_params=_cparams("parallel", "arbitrary"))(a, b)
    return out if with_colsum else out[0]


def _ada_fwd(c_all, ada_w, ada_b):
    def body(c_ref, w_ref, b_ref, o_ref):
        o_ref[...] = _nn(_bf(_silu(c_ref[...])), _bf(w_ref[...])) + b_ref[...]

    return pl.pallas_call(body, name="ada_fwd", out_shape=jax.ShapeDtypeStruct((c_all.shape[0], ada_w.shape[1]), F32),
                          compiler_params=_cparams())(c_all, ada_w, ada_b)


def _ada_bwd(c_all, dmod):
    def body(c_ref, d_ref, o_ref):
        o_ref[...] = _tn(_bf(_silu(c_ref[...])), _bf(d_ref[...]))

    return pl.pallas_call(body, name="ada_bwd", out_shape=jax.ShapeDtypeStruct((c_all.shape[1], dmod.shape[1]), F32),
                          compiler_params=_cparams())(c_all, dmod)


def _prenorm_fwd(x, norm_w, scale, shift, ts):
    s, d = x.shape

    def body(x_ref, nw_ref, sc_ref, sh_ref, u_ref):
        xv = x_ref[...]
        r = lax.rsqrt(jnp.mean(xv * xv, axis=1, keepdims=True) + EPS)
        u_ref[...] = _bf(xv * r * nw_ref[...] * (1.0 + sc_ref[...]) + sh_ref[...])

    row = pl.BlockSpec((1, d), lambda i: (0, 0))
    return pl.pallas_call(
        body, name="prenorm_fwd", grid=(s // ts,),
        in_specs=[pl.BlockSpec((ts, d), lambda i: (i, 0)), row, row, row],
        out_specs=pl.BlockSpec((ts, d), lambda i: (i, 0)), out_shape=jax.ShapeDtypeStruct((s, d), BF16),
        compiler_params=_cparams("parallel"))(x, norm_w, scale, shift)


def _prenorm_bwd(du, x, dxres, norm_w, scale, ts):
    s, d = x.shape

    def body(du_ref, x_ref, dr_ref, nw_ref, sc_ref, gx_ref, acc_ref):
        @pl.when(pl.program_id(0) == 0)
        def _():
            acc_ref[...] = jnp.zeros_like(acc_ref)
        xv, duv = x_ref[...], du_ref[...]
        r = lax.rsqrt(jnp.mean(xv * xv, axis=1, keepdims=True) + EPS)
        xn = xv * r
        nw, sc1 = nw_ref[...], 1.0 + sc_ref[...]
        dxn = duv * (nw * sc1)
        gx_ref[...] = r * (dxn - xn * jnp.mean(dxn * xn, axis=1, keepdims=True)) + dr_ref[...]
        t = duv * xn
        acc_ref[0:1, :] += jnp.sum(t, axis=0, keepdims=True) * sc1
        acc_ref[1:2, :] += jnp.sum(t, axis=0, keepdims=True) * nw
        acc_ref[2:3, :] += jnp.sum(duv, axis=0, keepdims=True)

    tile = pl.BlockSpec((ts, d), lambda i: (i, 0))
    row = pl.BlockSpec((1, d), lambda i: (0, 0))
    return pl.pallas_call(
        body, name="prenorm_bwd", grid=(s // ts,),
        in_specs=[tile, tile, tile, row, row],
        out_specs=[tile, pl.BlockSpec((8, d), lambda i: (0, 0))],
        out_shape=[jax.ShapeDtypeStruct((s, d), F32), jax.ShapeDtypeStruct((8, d), F32)],
        compiler_params=_cparams("arbitrary"))(du, x, dxres, norm_w, scale)


CONV_CB = 512


def _conv_taps(buf_ref, ts):
    return [buf_ref[pl.ds(8 - (CONV_K - 1) + j, ts), :] for j in range(CONV_K)]


def _conv_fwd(proj, col0, width, w8, b, ts):
    s = proj.shape[0]
    cb = CONV_CB
    nt = s // ts

    def body(x_ref, w_ref, b_ref, o_ref, buf_ref):
        @pl.when(pl.program_id(1) == 0)
        def _():
            buf_ref[0:8, :] = jnp.zeros((8, cb), F32)
        buf_ref[pl.ds(8, ts), :] = x_ref[...]
        acc = b_ref[...] + jnp.zeros((ts, cb), F32)
        for j, tap in enumerate(_conv_taps(buf_ref, ts)):
            acc = acc + tap * w_ref[j:j + 1, :]
        o_ref[...] = _silu(acc)
        buf_ref[0:8, :] = x_ref[pl.ds(ts - 8, 8), :]

    c0 = col0 // cb
    return pl.pallas_call(
        body, name="conv_fwd", grid=(width // cb, nt),
        in_specs=[pl.BlockSpec((ts, cb), lambda c, i: (i, c0 + c)), pl.BlockSpec((8, cb), lambda c, i: (0, c)),
                  pl.BlockSpec((1, cb), lambda c, i: (0, c))],
        out_specs=pl.BlockSpec((ts, cb), lambda c, i: (i, c)),
        out_shape=jax.ShapeDtypeStruct((s, width), F32),
        scratch_shapes=[pltpu.VMEM((ts + 8, cb), F32)],
        compiler_params=_cparams("parallel", "arbitrary"))(proj, w8, b)


def _conv_bwd(proj, col0, width, w8, b, dpost, dproj, ts):
    s = proj.shape[0]
    cb = CONV_CB
    nt = s // ts
    c0 = col0 // cb

    def body(x_ref, xh_ref, dp_ref, w_ref, b_ref, _, dx_ref, acc_ref, buf_ref, dbuf_ref):
        step = pl.program_id(1)
        tile = nt - 1 - step

        @pl.when(step == 0)
        def _():
            acc_ref[...] = jnp.zeros_like(acc_ref)
            dbuf_ref[pl.ds(ts, 8), :] = jnp.zeros((8, cb), F32)
        buf_ref[0:8, :] = jnp.where(tile == 0, 0.0, xh_ref[...])
        buf_ref[pl.ds(8, ts), :] = x_ref[...]
        taps = _conv_taps(buf_ref, ts)
        acc = b_ref[...] + jnp.zeros((ts, cb), F32)
        for j in range(CONV_K):
            acc = acc + taps[j] * w_ref[j:j + 1, :]
        dconv = dp_ref[...] * _dsilu(acc)
        acc_ref[4:5, :] += jnp.sum(dconv, axis=0, keepdims=True)
        for j in range(CONV_K):
            acc_ref[j:j + 1, :] += jnp.sum(taps[j] * dconv, axis=0, keepdims=True)
        dbuf_ref[pl.ds(0, ts), :] = dconv
        dx = jnp.zeros((ts, cb), F32)
        for j in range(CONV_K):
            dx = dx + dbuf_ref[pl.ds(CONV_K - 1 - j, ts), :] * w_ref[j:j + 1, :]
        dx_ref[...] = _bf(dx)
        dbuf_ref[pl.ds(ts, 8), :] = dconv[0:8, :]

    rows8 = ts // 8
    return pl.pallas_call(
        body, name="conv_bwd", grid=(width // cb, nt),
        in_specs=[pl.BlockSpec((ts, cb), lambda c, i: (nt - 1 - i, c0 + c)),
                  pl.BlockSpec((8, cb), lambda c, i: (jnp.maximum((nt - 1 - i) * rows8 - 1, 0), c0 + c)),
                  pl.BlockSpec((ts, cb), lambda c, i: (nt - 1 - i, c)),
                  pl.BlockSpec((8, cb), lambda c, i: (0, c)), pl.BlockSpec((1, cb), lambda c, i: (0, c)),
                  pl.BlockSpec(memory_space=pl.ANY)],
        out_specs=[pl.BlockSpec((ts, cb), lambda c, i: (nt - 1 - i, c0 + c)), pl.BlockSpec((8, cb), lambda c, i: (0, c))],
        out_shape=[jax.ShapeDtypeStruct(dproj.shape, dproj.dtype), jax.ShapeDtypeStruct((8, width), F32)],
        input_output_aliases={5: 0},
        scratch_shapes=[pltpu.VMEM((ts + 8, cb), F32), pltpu.VMEM((ts + 8, cb), F32)],
        compiler_params=_cparams("parallel", "arbitrary"))(proj, proj, dpost, w8, b, dproj)


def _mlstm_gates(gif_ref, gt_ref, a_scr, at_scr):
    L = gif_ref.shape[0]
    fb = _logsigmoid(gif_ref[...])
    a_scr[...] = _pick_left(_tri(L, False), fb, 3)
    at_scr[...] = _pick_right(_logsigmoid(gt_ref[...]), _tri(L, True), 3)
    return jnp.sum(fb, axis=0, keepdims=True)


def _mlstm_head(h, qk_ref, v_ref, gif, gt_ref, a, at_scr, a_last_row, c_mat, n_row, m_prev):
    L = gif.shape[0]
    q = qk_ref[:, h * ML_DQK:(h + 1) * ML_DQK] * (ML_DQK ** -0.5)
    k = qk_ref[:, (ML_HEADS + h) * ML_DQK:(ML_HEADS + h + 1) * ML_DQK]
    v = v_ref[:, h * ML_DV:(h + 1) * ML_DV]
    i_col, a_col = _lane_col(gif, h), _lane_col(a, ML_HEADS + h)
    i_row, a_row = gt_ref[h:h + 1, :], at_scr[ML_HEADS + h:ML_HEADS + h + 1, :]
    causal = lax.broadcasted_iota(jnp.int32, (L, L), 0) >= lax.broadcasted_iota(jnp.int32, (L, L), 1)
    dmat = jnp.where(causal, a_col - a_row + i_row, NEG)
    inter = a_col + m_prev
    m_t = jnp.maximum(inter, jnp.max(dmat, axis=1, keepdims=True))
    w_intra = jnp.exp(dmat - m_t)
    w_inter = jnp.exp(inter - m_t)
    sc = _nt(_bf(q), _bf(k)) * w_intra
    den = jnp.sum(sc, axis=1, keepdims=True) + w_inter * jnp.sum(q * n_row, axis=1, keepdims=True)
    floor = jnp.exp(-m_t)
    a_last = _lane_col(a_last_row, ML_HEADS + h)
    g = a_last - a_col + i_col
    m_new = jnp.maximum(a_last + m_prev, jnp.max(g, axis=0, keepdims=True))
    wk = jnp.exp(g - m_new)
    decay = jnp.exp(a_last + m_prev - m_new)
    return dict(q=q, k=k, v=v, w_intra=w_intra, w_inter=w_inter, sc=sc, den=den, floor=floor, m_new=m_new, wk=wk,
                decay=decay)


def _state_tile(n_row, m11):
    r = lax.broadcasted_iota(jnp.int32, (8, LANE), 0)
    return jnp.where(r == 0, n_row, jnp.where(r == 1, m11, 0.0))


def _mlstm_fwd(qk, proj, gt):
    s = qk.shape[0]
    L = CHUNK
    nc = s // L

    def body(qk_ref, v_ref, gif_ref, gt_ref, h_ref, cst_ref, nm_ref, c_scr, nm_scr, a_scr, at_scr):
        @pl.when(pl.program_id(0) == 0)
        def _():
            c_scr[...] = jnp.zeros_like(c_scr)
            nm_scr[...] = jnp.zeros_like(nm_scr)
        a_last_row = _mlstm_gates(gif_ref, gt_ref, a_scr, at_scr)
        gif, a = gif_ref[...], a_scr[...]
        for h in range(ML_HEADS):
            c_mat, n_row = c_scr[h], nm_scr[h, 0:1, :]
            m_prev = jnp.max(nm_scr[h, 1:2, :], axis=1, keepdims=True)
            cst_ref[0, h] = c_mat
            nm_ref[0, h] = nm_scr[h]
            t = _mlstm_head(h, qk_ref, v_ref, gif, gt_ref, a, at_scr, a_last_row, c_mat, n_row, m_prev)
            num = _nn(_bf(t["sc"]), _bf(t["v"])) + t["w_inter"] * _nn(_bf(t["q"]), _bf(c_mat))
            h_ref[:, h * ML_DV:(h + 1) * ML_DV] = num / jnp.maximum(jnp.abs(t["den"]), t["floor"])
            kw = t["k"] * t["wk"]
            c_scr[h] = t["decay"] * c_mat + _tn(_bf(kw), _bf(t["v"]))
            nm_scr[h] = _state_tile(t["decay"] * n_row + jnp.sum(kw, axis=0, keepdims=True), t["m_new"])

    return pl.pallas_call(
        body, name="mlstm_fwd", grid=(nc,),
        in_specs=[pl.BlockSpec((L, 2048), lambda c: (c, 0)), pl.BlockSpec((L, 2048), lambda c: (c, O_V // 2048)),
                  pl.BlockSpec((L, LANE), lambda c: (c, O_IF // LANE)), pl.BlockSpec((LANE, L), lambda c: (0, c))],
        out_specs=[pl.BlockSpec((L, 2048), lambda c: (c, 0)),
                   pl.BlockSpec((1, ML_HEADS, ML_DQK, ML_DV), lambda c: (c, 0, 0, 0)),
                   pl.BlockSpec((1, ML_HEADS, 8, LANE), lambda c: (c, 0, 0, 0))],
        out_shape=[jax.ShapeDtypeStruct((s, 2048), F32), jax.ShapeDtypeStruct((nc, ML_HEADS, ML_DQK, ML_DV), F32),
                   jax.ShapeDtypeStruct((nc, ML_HEADS, 8, LANE), F32)],
        scratch_shapes=[pltpu.VMEM((ML_HEADS, ML_DQK, ML_DV), F32), pltpu.VMEM((ML_HEADS, 8, LANE), F32),
                        pltpu.VMEM((L, LANE), F32), pltpu.VMEM((LANE, L), F32)],
        compiler_params=_cparams("arbitrary"))(qk, proj, proj, gt)


def _mlstm_bwd(qk, proj, gt, hout, dh, cst, nm, dproj):
    s = qk.shape[0]
    L = CHUNK
    nc = s // L

    def body(qk_ref, v_ref, gif_ref, gt_ref, h_ref, dh_ref, cst_ref, nm_ref, _, dqk_ref, dv_ref, dif_ref,
             dc_scr, dn_scr, a_scr, at_scr):
        @pl.when(pl.program_id(0) == 0)
        def _():
            dc_scr[...] = jnp.zeros_like(dc_scr)
            dn_scr[...] = jnp.zeros_like(dn_scr)
        a_last_row = _mlstm_gates(gif_ref, gt_ref, a_scr, at_scr)
        gif, a = gif_ref[...], a_scr[...]
        lane = lax.broadcasted_iota(jnp.int32, (L, LANE), 1)
        last = lax.broadcasted_iota(jnp.int32, (L, 1), 0) == L - 1
        di_tile = jnp.zeros((L, LANE), F32)
        cross = [jnp.zeros((L, LANE), F32)] * 3
        for h in range(ML_HEADS):
            c_mat, n_row = cst_ref[0, h], nm_ref[0, h, 0:1, :]
            m_prev = jnp.max(nm_ref[0, h, 1:2, :], axis=1, keepdims=True)
            t = _mlstm_head(h, qk_ref, v_ref, gif, gt_ref, a, at_scr, a_last_row, c_mat, n_row, m_prev)
            q, k, v, den = t["q"], t["k"], t["v"], t["den"]
            dhh = dh_ref[:, h * ML_DV:(h + 1) * ML_DV]
            hh = h_ref[:, h * ML_DV:(h + 1) * ML_DV]
            dnorm = jnp.maximum(jnp.abs(den), t["floor"])
            dnum = dhh / dnorm
            d_dn = -jnp.sum(dhh * hh, axis=1, keepdims=True) / dnorm
            dden = jnp.where(jnp.abs(den) >= t["floor"], jnp.where(den >= 0.0, d_dn, -d_dn), 0.0)
            dsc = _nt(_bf(dnum), _bf(v)) + dden
            ds = dsc * t["w_intra"]
            dq_inter = t["w_inter"] * (_nt(_bf(dnum), _bf(c_mat)) + dden * n_row)
            dq = _nn(_bf(ds), _bf(k)) + dq_inter
            dc, dn_row = dc_scr[h], dn_scr[h, 0:1, :]
            dk_state = t["wk"] * (_nt(_bf(v), _bf(dc)) + dn_row)
            dk = _tn(_bf(ds), _bf(q)) + dk_state
            dv = _tn(_bf(t["sc"]), _bf(dnum)) + t["wk"] * _nn(_bf(k), _bf(dc))
            qi = q * t["w_inter"]
            dc_scr[h] = t["decay"] * dc + _tn(_bf(qi), _bf(dnum))
            dn_scr[h] = jnp.broadcast_to(t["decay"] * dn_row + jnp.sum(qi * dden, axis=0, keepdims=True), (8, LANE))
            dqk_ref[:, h * ML_DQK:(h + 1) * ML_DQK] = dq * (ML_DQK ** -0.5)
            dqk_ref[:, (ML_HEADS + h) * ML_DQK:(ML_HEADS + h + 1) * ML_DQK] = dk
            dv_ref[:, h * ML_DV:(h + 1) * ML_DV] = _bf(dv)
            di_tile = di_tile + jnp.where(lane == h, jnp.sum(k * dk, axis=1, keepdims=True), 0.0)
            carried = t["decay"] * (_sum_all(dc * c_mat) + jnp.sum(dn_row * n_row, axis=1, keepdims=True))
            parts = (_crossing(dsc * t["sc"]),
                     jnp.sum(q * dq_inter, axis=1, keepdims=True) + jnp.where(last, carried, 0.0),
                     jnp.sum(k * dk_state, axis=1, keepdims=True))
            cross = [c + jnp.where(lane == ML_HEADS + h, p, 0.0) for c, p in zip(cross, parts)]
        dfb = cross[0] + _pick_left(_tri(L, True), cross[1], 2) + _pick_left(_tri(L, False) - _eye(L), cross[2], 2)
        dif_ref[:, 0:LANE] = _bf(di_tile + dfb * _sigmoid(-gif))
        dif_ref[:, LANE:SMALL_W] = jnp.zeros((L, SMALL_W - LANE), BF16)

    rev = lambda c: nc - 1 - c
    return pl.pallas_call(
        body, name="mlstm_bwd", grid=(nc,),
        in_specs=[pl.BlockSpec((L, 2048), lambda c: (rev(c), 0)), pl.BlockSpec((L, 2048), lambda c: (rev(c), O_V // 2048)),
                  pl.BlockSpec((L, LANE), lambda c: (rev(c), O_IF // LANE)), pl.BlockSpec((LANE, L), lambda c: (0, rev(c))),
                  pl.BlockSpec((L, 2048), lambda c: (rev(c), 0)), pl.BlockSpec((L, 2048), lambda c: (rev(c), 0)),
                  pl.BlockSpec((1, ML_HEADS, ML_DQK, ML_DV), lambda c: (rev(c), 0, 0, 0)),
                  pl.BlockSpec((1, ML_HEADS, 8, LANE), lambda c: (rev(c), 0, 0, 0)), pl.BlockSpec(memory_space=pl.ANY)],
        out_specs=[pl.BlockSpec((L, 2048), lambda c: (rev(c), 0)), pl.BlockSpec((L, 2048), lambda c: (rev(c), O_V // 2048)),
                   pl.BlockSpec((L, SMALL_W), lambda c: (rev(c), 0))],
        out_shape=[jax.ShapeDtypeStruct((s, 2048), F32), jax.ShapeDtypeStruct(dproj.shape, dproj.dtype),
                   jax.ShapeDtypeStruct((s, SMALL_W), BF16)],
        input_output_aliases={8: 1},
        scratch_shapes=[pltpu.VMEM((ML_HEADS, ML_DQK, ML_DV), F32), pltpu.VMEM((ML_HEADS, 8, LANE), F32),
                        pltpu.VMEM((L, LANE), F32), pltpu.VMEM((LANE, L), F32)],
        compiler_params=_cparams("arbitrary"))(qk, proj, proj, gt, hout, dh, cst, nm, dproj)


GROUP_W = SSM_HEADS // SSM_GROUPS * SSM_HEADDIM
O_B = SSM_HEADS * SSM_HEADDIM
O_C = O_B + SSM_GROUPS * SSM_STATE


def _head_expand():
    r = jnp.arange(LANE)[:, None]
    c = jnp.arange(SSM_HEADS * SSM_HEADDIM)[None, :] // SSM_HEADDIM
    return (r == c).astype(F32)


def _ssd_gates(dt_ref, dtt_ref, alog_row_ref, alog_col_ref, at_scr):
    L = dt_ref.shape[0]
    dt = _softplus(dt_ref[...])
    acoef = -jnp.exp(alog_row_ref[...])
    a = _pick_left(_tri(L, False), dt * acoef, 3)
    at_scr[...] = _pick_right(_softplus(dtt_ref[...]) * (-jnp.exp(alog_col_ref[...])), _tri(L, True), 3)
    return dt, acoef, a


def _ssd_group(g, xbc_ref, dt, a, e_ref, ax_scr):
    eg = e_ref[:, g * GROUP_W:(g + 1) * GROUP_W]
    ax_scr[...] = _pick_right(a, eg, 3)
    ax = ax_scr[...]
    alx = ax_scr[ax.shape[0] - 1:ax.shape[0], :]
    dtx = _pick_right(dt, eg, 2)
    xg = xbc_ref[:, g * GROUP_W:(g + 1) * GROUP_W]
    bg = xbc_ref[:, O_B + g * SSM_STATE:O_B + (g + 1) * SSM_STATE]
    cg = xbc_ref[:, O_C + g * SSM_STATE:O_C + (g + 1) * SSM_STATE]
    return dict(ax=ax, alx=alx, dtx=dtx, xg=xg, bg=bg, cg=cg, xdt=xg * dtx, gmat=_nt(_bf(cg), _bf(bg)))


def _ssd_decay(hh, a, at_scr):
    L = a.shape[0]
    causal = lax.broadcasted_iota(jnp.int32, (L, L), 0) >= lax.broadcasted_iota(jnp.int32, (L, L), 1)
    return jnp.exp(jnp.where(causal, _lane_col(a, hh) - at_scr[hh:hh + 1, :], NEG))


def _ssd_fwd(xbc, proj, dtt, alog_row, alog_col, dskip_x, expand):
    s = xbc.shape[0]
    L = CHUNK
    nc = s // L
    half = SSM_HEADDIM

    def body(xbc_ref, dt_ref, dtt_ref, ar_ref, ac_ref, dk_ref, e_ref, y_ref, st_ref, st_scr, at_scr, ax_scr):
        @pl.when(pl.program_id(0) == 0)
        def _():
            st_scr[...] = jnp.zeros_like(st_scr)
        dt, _, a = _ssd_gates(dt_ref, dtt_ref, ar_ref, ac_ref, at_scr)
        lane = lax.broadcasted_iota(jnp.int32, (L, LANE), 1)
        for g in range(SSM_GROUPS):
            t = _ssd_group(g, xbc_ref, dt, a, e_ref, ax_scr)
            st = st_scr[g]
            st_ref[0, g] = st
            pairs = []
            for j in range(GROUP_W // LANE):
                xp = _bf(t["xdt"][:, j * LANE:(j + 1) * LANE])
                hh = g * (SSM_HEADS // SSM_GROUPS) + 2 * j
                y0 = _nn(_bf(t["gmat"] * _ssd_decay(hh, a, at_scr)), xp)
                y1 = _nn(_bf(t["gmat"] * _ssd_decay(hh + 1, a, at_scr)), xp)
                pairs.append(jnp.where(lane < half, y0, y1))
            y = jnp.concatenate(pairs, axis=1) + _nn(_bf(t["cg"]), _bf(st)) * jnp.exp(t["ax"])
            y_ref[:, g * GROUP_W:(g + 1) * GROUP_W] = y + dk_ref[:, g * GROUP_W:(g + 1) * GROUP_W] * t["xg"]
            wts = jnp.exp(t["alx"] - t["ax"])
            st_scr[g] = jnp.exp(t["alx"]) * st + _tn(_bf(t["bg"]), _bf(t["xdt"] * wts))

    row = lambda w: pl.BlockSpec((1, w), lambda c: (0, 0))
    return pl.pallas_call(
        body, name="ssd_fwd", grid=(nc,),
        in_specs=[pl.BlockSpec((L, 3072), lambda c: (c, 0)), pl.BlockSpec((L, LANE), lambda c: (c, O_DT // LANE)),
                  pl.BlockSpec((LANE, L), lambda c: (0, c)), row(LANE), pl.BlockSpec((LANE, 1), lambda c: (0, 0)),
                  row(2048), pl.BlockSpec((LANE, 2048), lambda c: (0, 0))],
        out_specs=[pl.BlockSpec((L, 2048), lambda c: (c, 0)),
                   pl.BlockSpec((1, SSM_GROUPS, SSM_STATE, GROUP_W), lambda c: (c, 0, 0, 0))],
        out_shape=[jax.ShapeDtypeStruct((s, 2048), F32),
                   jax.ShapeDtypeStruct((nc, SSM_GROUPS, SSM_STATE, GROUP_W), F32)],
        scratch_shapes=[pltpu.VMEM((SSM_GROUPS, SSM_STATE, GROUP_W), F32), pltpu.VMEM((LANE, L), F32),
                        pltpu.VMEM((L, GROUP_W), F32)],
        compiler_params=_cparams("arbitrary"))(xbc, proj, dtt, alog_row, alog_col, dskip_x, expand)


def _ssd_bwd(xbc, proj, dtt, alog_row, alog_col, dskip_x, expand, expand_t, dy, states):
    s = xbc.shape[0]
    L = CHUNK
    nc = s // L
    half = SSM_HEADDIM

    def body(xbc_ref, dt_ref, dtt_ref, ar_ref, ac_ref, dk_ref, e_ref, et_ref, dy_ref, st_ref,
             dxbc_ref, ddt_ref, accd_ref, acca_ref, dst_scr, at_scr, ax_scr):
        @pl.when(pl.program_id(0) == 0)
        def _():
            dst_scr[...] = jnp.zeros_like(dst_scr)
            accd_ref[...] = jnp.zeros_like(accd_ref)
            acca_ref[...] = jnp.zeros_like(acca_ref)
        dt, acoef, a = _ssd_gates(dt_ref, dtt_ref, ar_ref, ac_ref, at_scr)
        lane = lax.broadcasted_iota(jnp.int32, (L, LANE), 1)
        low = lane < half
        last = lax.broadcasted_iota(jnp.int32, (L, 1), 0) == L - 1
        cross = [jnp.zeros((L, LANE), F32)] * 3
        ddt_tile = jnp.zeros((L, LANE), F32)
        for g in range(SSM_GROUPS):
            t = _ssd_group(g, xbc_ref, dt, a, e_ref, ax_scr)
            xg, bg, cg, xdt, gmat = t["xg"], t["bg"], t["cg"], t["xdt"], t["gmat"]
            st, dst = st_ref[0, g], dst_scr[g]
            dyg = dy_ref[:, g * GROUP_W:(g + 1) * GROUP_W]
            ea, eal = jnp.exp(t["ax"]), jnp.exp(t["alx"])
            wts = jnp.exp(t["alx"] - t["ax"])
            dyi = dyg * ea
            y_inter = _nn(_bf(cg), _bf(st)) * ea
            dc = _nt(_bf(dyi), _bf(st))
            d_xdt_state = _nn(_bf(bg), _bf(dst)) * wts
            db = _nt(_bf(xdt * wts), _bf(dst))
            dst_scr[g] = eal * dst + _tn(_bf(cg), _bf(dyi))
            dg = jnp.zeros((L, L), F32)
            dx_pairs = []
            for j in range(GROUP_W // LANE):
                xp = _bf(xdt[:, j * LANE:(j + 1) * LANE])
                dyp = dyg[:, j * LANE:(j + 1) * LANE]
                dxs = []
                for b in range(2):
                    hh = g * (SSM_HEADS // SSM_GROUPS) + 2 * j + b
                    dec = _ssd_decay(hh, a, at_scr)
                    w = gmat * dec
                    dxs.append(_tn(_bf(w), _bf(dyp)))
                    dw = _nt(_bf(jnp.where(low if b == 0 else ~low, dyp, 0.0)), xp)
                    dg = dg + dw * dec
                    cross[0] = cross[0] + jnp.where(lane == hh, _crossing(dw * w), 0.0)
                dx_pairs.append(jnp.where(low, dxs[0], dxs[1]))
            d_xdt = d_xdt_state + jnp.concatenate(dx_pairs, axis=1)
            dc = dc + _nn(_bf(dg), _bf(bg))
            db = db + _tn(_bf(dg), _bf(cg))
            etg = et_ref[g * GROUP_W:(g + 1) * GROUP_W, :]
            carried = jnp.sum(dst * st, axis=0, keepdims=True) * eal
            cross[1] = cross[1] + _pick_right(dyg * y_inter + jnp.where(last, carried, 0.0), etg, 2)
            cross[2] = cross[2] + _pick_right(xdt * d_xdt_state, etg, 2)
            ddt_tile = ddt_tile + _pick_right(d_xdt * xg, etg, 2)
            dxbc_ref[:, g * GROUP_W:(g + 1) * GROUP_W] = d_xdt * t["dtx"] + dk_ref[:, g * GROUP_W:(g + 1) * GROUP_W] * dyg
            dxbc_ref[:, O_B + g * SSM_STATE:O_B + (g + 1) * SSM_STATE] = db
            dxbc_ref[:, O_C + g * SSM_STATE:O_C + (g + 1) * SSM_STATE] = dc
            accd_ref[0:1, g * GROUP_W:(g + 1) * GROUP_W] += jnp.sum(dyg * xg, axis=0, keepdims=True)
        d_da = cross[0] + _pick_left(_tri(L, True), cross[1], 2) + _pick_left(_tri(L, False) - _eye(L), cross[2], 2)
        acca_ref[0:1, :] += jnp.sum(d_da * dt, axis=0, keepdims=True)
        ddt_ref[:, 0:LANE] = _bf((ddt_tile + d_da * acoef) * _sigmoid(dt_ref[...]))
        ddt_ref[:, LANE:SMALL_W] = jnp.zeros((L, SMALL_W - LANE), BF16)

    rev = lambda c: nc - 1 - c
    row = lambda w: pl.BlockSpec((1, w), lambda c: (0, 0))
    return pl.pallas_call(
        body, name="ssd_bwd", grid=(nc,),
        in_specs=[pl.BlockSpec((L, 3072), lambda c: (rev(c), 0)), pl.BlockSpec((L, LANE), lambda c: (rev(c), O_DT // LANE)),
                  pl.BlockSpec((LANE, L), lambda c: (0, rev(c))), row(LANE), pl.BlockSpec((LANE, 1), lambda c: (0, 0)),
                  row(2048), pl.BlockSpec((LANE, 2048), lambda c: (0, 0)), pl.BlockSpec((2048, LANE), lambda c: (0, 0)),
                  pl.BlockSpec((L, 2048), lambda c: (rev(c), 0)),
                  pl.BlockSpec((1, SSM_GROUPS, SSM_STATE, GROUP_W), lambda c: (rev(c), 0, 0, 0))],
        out_specs=[pl.BlockSpec((L, 3072), lambda c: (rev(c), 0)), pl.BlockSpec((L, SMALL_W), lambda c: (rev(c), 0)),
                   pl.BlockSpec((8, 2048), lambda c: (0, 0)), pl.BlockSpec((8, LANE), lambda c: (0, 0))],
        out_shape=[jax.ShapeDtypeStruct((s, 3072), F32), jax.ShapeDtypeStruct((s, SMALL_W), BF16),
                   jax.ShapeDtypeStruct((8, 2048), F32), jax.ShapeDtypeStruct((8, LANE), F32)],
        scratch_shapes=[pltpu.VMEM((SSM_GROUPS, SSM_STATE, GROUP_W), F32),
                        pltpu.VMEM((LANE, L), F32), pltpu.VMEM((L, GROUP_W), F32)],
        compiler_params=_cparams("arbitrary"))(xbc, proj, dtt, alog_row, alog_col, dskip_x, expand, expand_t, dy, states)


def _group_norm(v, width):
    outs, rs = [], []
    for k in range(v.shape[1] // width):
        blk = v[:, k * width:(k + 1) * width]
        r = lax.rsqrt(jnp.mean(blk * blk, axis=1, keepdims=True) + EPS)
        outs.append(blk * r)
        rs.append(jnp.broadcast_to(r, blk.shape))
    return jnp.concatenate(outs, axis=1), jnp.concatenate(rs, axis=1)


def _group_mean(v, width):
    return jnp.concatenate([jnp.broadcast_to(jnp.mean(v[:, k * width:(k + 1) * width], axis=1, keepdims=True),
                                             (v.shape[0], width)) for k in range(v.shape[1] // width)], axis=1)


def _post_fwd(hm, yssd, proj, ml_norm_w, ssm_norm_w, ts):
    s = hm.shape[0]

    def body(h_ref, ys_ref, o_ref, zm_ref, zs_ref, wm_ref, ws_ref, ym_ref, yso_ref):
        hn, _ = _group_norm(h_ref[...], ML_DV)
        ym_ref[...] = _bf(_sigmoid(o_ref[...]) * hn * wm_ref[...] * _silu(zm_ref[...]))
        pn, _ = _group_norm(ys_ref[...] * _silu(zs_ref[...]), GROUP_W)
        yso_ref[...] = _bf(pn * ws_ref[...])

    tile = pl.BlockSpec((ts, 2048), lambda i: (i, 0))
    col = lambda off: pl.BlockSpec((ts, 2048), lambda i: (i, off // 2048))
    row = pl.BlockSpec((1, 2048), lambda i: (0, 0))
    return pl.pallas_call(
        body, name="post_fwd", grid=(s // ts,),
        in_specs=[tile, tile, col(O_O), col(O_ZM), col(O_ZS), row, row],
        out_specs=[tile, tile],
        out_shape=[jax.ShapeDtypeStruct((s, 2048), BF16)] * 2,
        compiler_params=_cparams("parallel"))(hm, yssd, proj, proj, proj, ml_norm_w, ssm_norm_w)


def _post_bwd(dym, dys, hm, yssd, proj, ml_norm_w, ssm_norm_w, dproj, ts):
    s = hm.shape[0]

    def body(dym_ref, dys_ref, h_ref, ys_ref, o_ref, zm_ref, zs_ref, wm_ref, ws_ref, _,
             dh_ref, dyssd_ref, dp_ref, acc_ref):
        @pl.when(pl.program_id(0) == 0)
        def _():
            acc_ref[...] = jnp.zeros_like(acc_ref)
        hn, r = _group_norm(h_ref[...], ML_DV)
        so, zm, wm, d_ym = _sigmoid(o_ref[...]), zm_ref[...], wm_ref[...], dym_ref[...]
        sz = _silu(zm)
        hnw = hn * wm
        dp_ref[:, O_O:O_O + 2048] = _bf(d_ym * hnw * sz * so * (1.0 - so))
        dp_ref[:, O_ZM:O_ZM + 2048] = _bf(d_ym * so * hnw * _dsilu(zm))
        dhnw = d_ym * so * sz
        acc_ref[0:1, :] += jnp.sum(dhnw * hn, axis=0, keepdims=True)
        dhn = dhnw * wm
        dh_ref[...] = r * (dhn - hn * _group_mean(dhn * hn, ML_DV))
        ysv, zs, d_ys = ys_ref[...], zs_ref[...], dys_ref[...]
        szs = _silu(zs)
        pn, r2 = _group_norm(ysv * szs, GROUP_W)
        acc_ref[1:2, :] += jnp.sum(d_ys * pn, axis=0, keepdims=True)
        dpn = d_ys * ws_ref[...]
        dp = r2 * (dpn - pn * _group_mean(dpn * pn, GROUP_W))
        dyssd_ref[...] = dp * szs
        dp_ref[:, O_ZS:O_ZS + 2048] = _bf(dp * ysv * _dsilu(zs))

    tile = pl.BlockSpec((ts, 2048), lambda i: (i, 0))
    col = lambda off: pl.BlockSpec((ts, 2048), lambda i: (i, off // 2048))
    row = pl.BlockSpec((1, 2048), lambda i: (0, 0))
    sds = lambda dt: jax.ShapeDtypeStruct((s, 2048), dt)
    return pl.pallas_call(
        body, name="post_bwd", grid=(s // ts,),
        in_specs=[tile, tile, tile, tile, col(O_O), col(O_ZM), col(O_ZS), row, row, pl.BlockSpec(memory_space=pl.ANY)],
        out_specs=[tile, tile, pl.BlockSpec((ts, O_MG), lambda i: (i, 0)), pl.BlockSpec((8, 2048), lambda i: (0, 0))],
        out_shape=[sds(F32), sds(F32), jax.ShapeDtypeStruct(dproj.shape, dproj.dtype), jax.ShapeDtypeStruct((8, 2048), F32)],
        input_output_aliases={9: 2},
        compiler_params=_cparams("arbitrary"))(dym, dys, hm, yssd, proj, proj, proj, ml_norm_w, ssm_norm_w, dproj)


def _merge(x, ym, ys, proj, target, gate, final_w, wpm, wps, wo, ts):
    wpm_t, wps_t, wo_t = wpm.T, wps.T, wo.T
    s, d = x.shape

    def body(x_ref, ym_ref, ys_ref, mg_ref, t_ref, gate_ref, fw_ref, wpm_ref, wps_ref, wo_ref, wpmt_ref, wpst_ref, wot_ref,
             dres_ref, mer_ref, dmo_ref, dpm_ref, dps_ref, dym_ref, dys_ref, dmg_ref, acc_ref):
        @pl.when(pl.program_id(0) == 0)
        def _():
            acc_ref[...] = jnp.zeros_like(acc_ref)
        gm, gs = _sigmoid(mg_ref[:, 0:d]), _sigmoid(mg_ref[:, d:2 * d])
        pm = _nn(ym_ref[...], wpm_ref[...])
        ps = _nn(ys_ref[...], wps_ref[...])
        merged = _bf(gm * pm + gs * ps)
        mer_ref[...] = merged
        mo = _nn(merged, wo_ref[...])
        gate, fw = gate_ref[...], fw_ref[...]
        out = x_ref[...] + gate * mo
        r = lax.rsqrt(jnp.mean(out * out, axis=1, keepdims=True) + EPS)
        on = out * r
        diff = on * fw - t_ref[...]
        acc_ref[0:1, :] += jnp.sum(0.5 * jnp.sum(diff * diff, axis=1, keepdims=True) / d, axis=0, keepdims=True)
        dyv = diff * (1.0 / d)
        acc_ref[1:2, :] += jnp.sum(dyv * on, axis=0, keepdims=True)
        don = dyv * fw
        dout = r * (don - on * jnp.mean(don * on, axis=1, keepdims=True))
        dres_ref[...] = dout
        acc_ref[2:3, :] += jnp.sum(dout * mo, axis=0, keepdims=True)
        dmo = _bf(dout * gate)
        dmo_ref[...] = dmo
        dmer = _nn(dmo, wot_ref[...])
        dpm, dps = _bf(dmer * gm), _bf(dmer * gs)
        dpm_ref[...] = dpm
        dps_ref[...] = dps
        dmg_ref[:, 0:d] = _bf(dmer * pm * gm * (1.0 - gm))
        dmg_ref[:, d:2 * d] = _bf(dmer * ps * gs * (1.0 - gs))
        dym_ref[...] = _nn(dpm, wpmt_ref[...])
        dys_ref[...] = _nn(dps, wpst_ref[...])

    t1 = pl.BlockSpec((ts, d), lambda i: (i, 0))
    t2 = pl.BlockSpec((ts, 2 * d), lambda i: (i, 0))
    row = pl.BlockSpec((1, d), lambda i: (0, 0))
    whole = pl.BlockSpec(memory_space=pltpu.VMEM)
    sd = lambda w, dt: jax.ShapeDtypeStruct((s, w), dt)
    return pl.pallas_call(
        body, name="merge_fwd_bwd", grid=(s // ts,),
        in_specs=[t1, t2, t2, pl.BlockSpec((ts, 2 * d), lambda i: (i, O_MG // (2 * d))), t1, row, row] + [whole] * 6,
        out_specs=[t1, t1, t1, t1, t1, t2, t2, pl.BlockSpec((ts, 2 * d), lambda i: (i, O_MG // (2 * d))),
                   pl.BlockSpec((8, d), lambda i: (0, 0))],
        out_shape=[sd(d, F32), sd(d, BF16), sd(d, BF16), sd(d, BF16), sd(d, BF16), sd(2 * d, F32), sd(2 * d, F32),
                   sd(NP, BF16), jax.ShapeDtypeStruct((8, d), F32)],
        compiler_params=_cparams("arbitrary"))(x, ym, ys, proj, target, gate, final_w, wpm, wps, wo, wpm_t, wps_t, wo_t)


def _adamw(w, g, m, v, tr):
    rows, cols = w.shape

    def body(w_ref, g_ref, m_ref, v_ref, d_ref, nm_ref, nv_ref):
        gv = g_ref[...]
        m2 = ADAM_B1 * m_ref[...] + (1.0 - ADAM_B1) * gv
        v2 = ADAM_B2 * v_ref[...] + (1.0 - ADAM_B2) * (gv * gv)
        m_hat = m2 / (1.0 - ADAM_B1 ** ADAM_STEP)
        v_hat = v2 / (1.0 - ADAM_B2 ** ADAM_STEP)
        d_ref[...] = -ADAM_LR * (m_hat / (jnp.sqrt(v_hat) + ADAM_EPS) + ADAM_WD * w_ref[...])
        nm_ref[...] = m2
        nv_ref[...] = v2

    tile = pl.BlockSpec((tr, cols), lambda i: (i, 0))
    return pl.pallas_call(
        body, name="adamw", grid=(rows // tr,), in_specs=[tile] * 4, out_specs=[tile] * 3,
        out_shape=[jax.ShapeDtypeStruct((rows, cols), F32)] * 3,
        compiler_params=_cparams("parallel"))(w, g, m, v)


def _sum_parts(own, parts, tr):
    p, rows, cols = parts.shape

    def body(*refs):
        p_ref, o_ref = refs[-2], refs[-1]
        acc = p_ref[0].astype(F32) if own is None else refs[0][...].astype(F32) + p_ref[0].astype(F32)
        for i in range(1, p):
            acc = acc + p_ref[i].astype(F32)
        o_ref[...] = acc

    tile = pl.BlockSpec((tr, cols), lambda i: (i, 0))
    ins = ([] if own is None else [tile]) + [pl.BlockSpec((p, tr, cols), lambda i: (0, i, 0))]
    args = ([] if own is None else [own]) + [parts]
    return pl.pallas_call(
        body, name="sum_parts", grid=(rows // tr,), in_specs=ins, out_specs=tile,
        out_shape=jax.ShapeDtypeStruct((rows, cols), F32), compiler_params=_cparams("parallel"))(*args)


def _position():
    return lax.axis_index("x"), lax.axis_index("y"), lax.axis_index("c")


def _flip(pos, k):
    return tuple(1 - p if (k >> s) & 1 else p for p, s in zip(pos, (2, 1, 0)))


def _allgather8(block):
    rows, cols = block.shape

    def body(x_ref, o_ref, send_sems, recv_sems, local_sem):
        pos = _position()
        me = 4 * pos[0] + 2 * pos[1] + pos[2]
        mine = pltpu.make_async_copy(x_ref, o_ref.at[me], local_sem)
        mine.start()
        copies = [pltpu.make_async_remote_copy(src_ref=x_ref, dst_ref=o_ref.at[me], send_sem=send_sems.at[k - 1],
                                               recv_sem=recv_sems.at[k - 1], device_id=_flip(pos, k), device_id_type=MESH)
                  for k in range(1, N_DEV)]
        for cp in copies:
            cp.start()
        for cp in copies:
            cp.wait()
        mine.wait()

    vmem = pl.BlockSpec(memory_space=pltpu.VMEM)
    return pl.pallas_call(
        body, name="allgather8", in_specs=[vmem], out_specs=vmem,
        out_shape=jax.ShapeDtypeStruct((N_DEV, rows, cols), block.dtype),
        scratch_shapes=[pltpu.SemaphoreType.DMA((N_DEV - 1,)), pltpu.SemaphoreType.DMA((N_DEV - 1,)),
                        pltpu.SemaphoreType.DMA],
        compiler_params=pltpu.CompilerParams(vmem_limit_bytes=VMEM_LIMIT))(block)


def _weight_gather(shards):
    n = len(shards)

    def body(*refs):
        ins, outs = refs[:n], refs[n:2 * n]
        send_sems, recv_sems, local_sems = refs[2 * n:]
        pos = _position()
        chip = 2 * pos[0] + pos[1]
        copies = []
        for a in range(n):
            mine = pltpu.make_async_copy(ins[a], outs[a].at[chip], local_sems.at[a])
            mine.start()
            copies.append(mine)
            for k in range(1, N_CHIPS):
                cp = pltpu.make_async_remote_copy(src_ref=ins[a], dst_ref=outs[a].at[chip], send_sem=send_sems.at[a, k - 1],
                                                  recv_sem=recv_sems.at[a, k - 1], device_id=_flip(pos, 2 * k),
                                                  device_id_type=MESH)
                cp.start()
                copies.append(cp)
        for cp in copies:
            cp.wait()

    hbm = pl.BlockSpec(memory_space=pl.ANY)
    return pl.pallas_call(
        body, name="weight_gather", in_specs=[hbm] * n, out_specs=[hbm] * n,
        out_shape=[jax.ShapeDtypeStruct((N_CHIPS,) + a.shape, a.dtype) for a in shards],
        scratch_shapes=[pltpu.SemaphoreType.DMA((n, N_CHIPS - 1)), pltpu.SemaphoreType.DMA((n, N_CHIPS - 1)),
                        pltpu.SemaphoreType.DMA((n,))],
        compiler_params=pltpu.CompilerParams(has_side_effects=True))(*shards)


def _grad_scatter(grads):
    n = len(grads)

    def body(*refs):
        ins, outs = refs[:n], refs[n:2 * n]
        send_sems, recv_sems = refs[2 * n:]
        pos = _position()
        copies = []
        for a in range(n):
            for k in range(1, N_DEV):
                to = _flip(pos, k)
                cp = pltpu.make_async_remote_copy(src_ref=ins[a].at[2 * to[0] + to[1], to[2]], dst_ref=outs[a].at[k - 1],
                                                  send_sem=send_sems.at[a, k - 1], recv_sem=recv_sems.at[a, k - 1],
                                                  device_id=to, device_id_type=MESH)
                cp.start()
                copies.append(cp)
        for cp in copies:
            cp.wait()

    hbm = pl.BlockSpec(memory_space=pl.ANY)
    return pl.pallas_call(
        body, name="grad_scatter", in_specs=[hbm] * n, out_specs=[hbm] * n,
        out_shape=[jax.ShapeDtypeStruct((N_DEV - 1,) + g.shape[2:], g.dtype) for g in grads],
        scratch_shapes=[pltpu.SemaphoreType.DMA((n, N_DEV - 1)), pltpu.SemaphoreType.DMA((n, N_DEV - 1))],
        compiler_params=pltpu.CompilerParams(has_side_effects=True))(*grads)


def _pair_exchange(halves):
    n = len(halves)

    def body(*refs):
        ins, outs = refs[:n], refs[n:2 * n]
        send_sems, recv_sems, local_sems = refs[2 * n:]
        pos = _position()
        copies = []
        for a in range(n):
            mine = pltpu.make_async_copy(ins[a], outs[a].at[pos[2]], local_sems.at[a])
            mine.start()
            cp = pltpu.make_async_remote_copy(src_ref=ins[a], dst_ref=outs[a].at[pos[2]], send_sem=send_sems.at[a],
                                              recv_sem=recv_sems.at[a], device_id=_flip(pos, 1), device_id_type=MESH)
            cp.start()
            copies += [mine, cp]
        for cp in copies:
            cp.wait()

    hbm = pl.BlockSpec(memory_space=pl.ANY)
    return pl.pallas_call(
        body, name="pair_exchange", in_specs=[hbm] * n, out_specs=[hbm] * n,
        out_shape=[jax.ShapeDtypeStruct((2,) + h.shape, h.dtype) for h in halves],
        scratch_shapes=[pltpu.SemaphoreType.DMA((n,)), pltpu.SemaphoreType.DMA((n,)), pltpu.SemaphoreType.DMA((n,))],
        compiler_params=pltpu.CompilerParams(has_side_effects=True))(*halves)


def _pack(arrays):
    flat = jnp.concatenate([a.reshape(-1).astype(F32) for a in arrays])
    size = -(-flat.shape[0] // (8 * LANE)) * (8 * LANE)
    return jnp.pad(flat, (0, size - flat.shape[0])).reshape(size // LANE, LANE)


def _unpack(buf, shapes):
    flat = buf.reshape(-1)
    out, off = [], 0
    for shp in shapes:
        n = math.prod(shp)
        out.append(flat[off:off + n].reshape(shp))
        off += n
    return out


def _unpack_rows(bufs, shapes):
    flat = bufs.reshape(bufs.shape[0], -1)
    out, off = [], 0
    for shp in shapes:
        n = math.prod(shp)
        out.append(flat[:, off:off + n].reshape((bufs.shape[0],) + shp))
        off += n
    return out


def _taps8(w):
    return jnp.pad(w, ((0, 8 - CONV_K), (0, 0)))


def _local_step(xs, tgt, scale, shift, gate, norm_w, w_in_p, b_in_p, ml_conv_w, ml_conv_b, ml_norm_w, ssm_conv_w,
                ssm_conv_b, ssm_a_log, ssm_d, ssm_norm_w, wpm, wps, wo, final_w):
    s = xs.shape[0]
    ts = min(512, s)
    tm = min(1024, s)
    u = _prenorm_fwd(xs, norm_w, scale, shift, ts)
    proj = _matmul_bias(u, w_in_p, b_in_p, tm, 512)
    mlw8, ssw8 = _taps8(ml_conv_w), _taps8(ssm_conv_w)
    qk = _conv_fwd(proj, O_QK, 2048, mlw8, ml_conv_b, ts)
    xbc = _conv_fwd(proj, O_XBC, 3072, ssw8, ssm_conv_b, ts)
    gt = proj[:, O_IF:O_IF + LANE].T
    dtt = proj[:, O_DT:O_DT + LANE].T
    hm, cst, nm = _mlstm_fwd(qk, proj, gt)
    alog_row = jnp.pad(ssm_a_log, ((0, 0), (0, LANE - SSM_HEADS)))
    alog_col = alog_row.reshape(LANE, 1)
    dskip_x = jnp.repeat(ssm_d[0], SSM_HEADDIM)[None]
    expand = _head_expand()
    yssd, sst = _ssd_fwd(xbc, proj, dtt, alog_row, alog_col, dskip_x, expand)
    tp = min(128, s)
    ym, ys = _post_fwd(hm, yssd, proj, ml_norm_w, ssm_norm_w, tp)
    dxres, merged, dmo, dpm, dps, dym, dys, dproj, acc_m = _merge(xs, ym, ys, proj, tgt, gate, final_w, wpm, wps, wo, tp)
    dh, dyssd, dproj, acc_p = _post_bwd(dym, dys, hm, yssd, proj, ml_norm_w, ssm_norm_w, dproj, tp)
    dqk, dproj, dif = _mlstm_bwd(qk, proj, gt, hm, dh, cst, nm, dproj)
    dxbc, ddt, accd, acca = _ssd_bwd(xbc, proj, dtt, alog_row, alog_col, dskip_x, expand, expand.T, dyssd, sst)
    dproj, acc_cq = _conv_bwd(proj, O_QK, 2048, mlw8, ml_conv_b, dqk, dproj, ts)
    dproj, acc_cx = _conv_bwd(proj, O_XBC, 3072, ssw8, ssm_conv_b, dxbc, dproj, ts)
    dproj = dproj.at[:, O_IF:O_IF + SMALL_W].set(dif).at[:, O_DT:O_DT + SMALL_W].set(ddt)
    gw_in_p, gb_in_p = _matmul_tn(u, dproj, tm, 512, with_colsum=True)
    du = _matmul_nt(dproj, w_in_p, tm, 512)
    grad_x, acc_n = _prenorm_bwd(du, xs, dxres, norm_w, scale, ts)
    g_wpm = _matmul_tn(ym, dpm, tm, 512)
    g_wps = _matmul_tn(ys, dps, tm, 512)
    g_wo = _matmul_tn(merged, dmo, tm, 512)
    a_coef = -jnp.exp(ssm_a_log[0])
    small = dict(
        mod=jnp.concatenate([acc_n[2], acc_n[1], acc_m[2]]), norm_w=acc_n[0], b_in=_unpad_cols(gb_in_p[0]),
        ml_conv_w=acc_cq[0:CONV_K], ml_conv_b=acc_cq[CONV_K], ml_norm_w=acc_p[0], ssm_conv_w=acc_cx[0:CONV_K],
        ssm_conv_b=acc_cx[CONV_K], ssm_a_log=acca[0, :SSM_HEADS] * a_coef,
        ssm_d=accd[0].reshape(SSM_HEADS, SSM_HEADDIM).sum(axis=1), ssm_norm_w=acc_p[1], final_w=acc_m[1], loss=acc_m[0, 0:1])
    return grad_x, small, _unpad_cols(gw_in_p), g_wpm, g_wps, g_wo


WEIGHTS = ("norm_w", "ada_w", "ada_b", "w_in", "b_in", "ml_conv_w", "ml_conv_b", "ml_norm_w", "ssm_conv_w", "ssm_conv_b",
           "ssm_a_log", "ssm_d", "ssm_norm_w", "w_proj_m", "w_proj_s", "w_out", "final_w")
LARGE = ("ada_w", "w_in", "w_proj_m", "w_proj_s", "w_out")
SMALL_SUMS = (("mod", (3 * D_MODEL,)), ("norm_w", (D_MODEL,)), ("b_in", (IN_WIDTH,)), ("ml_conv_w", (CONV_K, 2048)),
              ("ml_conv_b", (2048,)), ("ml_norm_w", (2048,)), ("ssm_conv_w", (CONV_K, 3072)), ("ssm_conv_b", (3072,)),
              ("ssm_a_log", (SSM_HEADS,)), ("ssm_d", (SSM_HEADS,)), ("ssm_norm_w", (2048,)), ("final_w", (D_MODEL,)),
              ("loss", (1,)))


def kernel(x, c, norm_w, ada_w, ada_b, w_in, b_in, ml_conv_w, ml_conv_b, ml_norm_w, ssm_conv_w, ssm_conv_b, ssm_a_log, ssm_d, ssm_norm_w, w_proj_m, w_proj_s, w_out, final_w, loss_target, m_norm_w, m_ada_w, m_ada_b, m_w_in, m_b_in, m_ml_conv_w, m_ml_conv_b, m_ml_norm_w, m_ssm_conv_w, m_ssm_conv_b, m_ssm_a_log, m_ssm_d, m_ssm_norm_w, m_w_proj_m, m_w_proj_s, m_w_out, m_final_w, v_norm_w, v_ada_w, v_ada_b, v_w_in, v_b_in, v_ml_conv_w, v_ml_conv_b, v_ml_norm_w, v_ssm_conv_w, v_ssm_conv_b, v_ssm_a_log, v_ssm_d, v_ssm_norm_w, v_w_proj_m, v_w_proj_s, v_w_out, v_final_w):
    w = dict(norm_w=norm_w, ada_w=ada_w, ada_b=ada_b, w_in=w_in, b_in=b_in, ml_conv_w=ml_conv_w, ml_conv_b=ml_conv_b,
             ml_norm_w=ml_norm_w, ssm_conv_w=ssm_conv_w, ssm_conv_b=ssm_conv_b, ssm_a_log=ssm_a_log, ssm_d=ssm_d,
             ssm_norm_w=ssm_norm_w, w_proj_m=w_proj_m, w_proj_s=w_proj_s, w_out=w_out, final_w=final_w)
    m = dict(zip(WEIGHTS, (m_norm_w, m_ada_w, m_ada_b, m_w_in, m_b_in, m_ml_conv_w, m_ml_conv_b, m_ml_norm_w, m_ssm_conv_w,
                           m_ssm_conv_b, m_ssm_a_log, m_ssm_d, m_ssm_norm_w, m_w_proj_m, m_w_proj_s, m_w_out, m_final_w)))
    v = dict(zip(WEIGHTS, (v_norm_w, v_ada_w, v_ada_b, v_w_in, v_b_in, v_ml_conv_w, v_ml_conv_b, v_ml_norm_w, v_ssm_conv_w,
                           v_ssm_conv_b, v_ssm_a_log, v_ssm_d, v_ssm_norm_w, v_w_proj_m, v_w_proj_s, v_w_out, v_final_w)))
    pos = _position()
    chip = 2 * pos[0] + pos[1]
    dev = 2 * chip + pos[2]
    mlw_cols, ssw_cols, ada_cols = ml_conv_w.shape[2], ssm_conv_w.shape[2], ada_w.shape[2]

    g0 = _allgather8(_pack([c, ml_conv_w, ssm_conv_w]))
    c_all, mlw_all, ssw_all = _unpack_rows(g0, [(D_MODEL,), (CONV_K, mlw_cols), (CONV_K, ssw_cols)])
    ml_conv_full = mlw_all[0::2].transpose(1, 0, 2).reshape(CONV_K, N_CHIPS * mlw_cols)
    ssm_conv_full = ssw_all[0::2].transpose(1, 0, 2).reshape(CONV_K, N_CHIPS * ssw_cols)

    ada_b_mine = lax.dynamic_slice_in_dim(ada_b, chip * ada_cols, ada_cols, axis=1)
    g1 = _allgather8(_ada_fwd(c_all, ada_w[0], ada_b_mine))
    mod = lax.dynamic_index_in_dim(g1[0::2], dev, axis=1, keepdims=False).reshape(1, 3 * D_MODEL)
    shift, scale, gate = mod[:, :D_MODEL], mod[:, D_MODEL:2 * D_MODEL], mod[:, 2 * D_MODEL:]

    gw = _weight_gather([_bf(w_in[0]), _bf(w_proj_m[0]), _bf(w_proj_s[0]), _bf(w_out[0])])
    w_in_p = _pad_cols(gw[0].transpose(1, 0, 2).reshape(D_MODEL, IN_WIDTH))
    wpm, wps, wo = (a.reshape(-1, D_MODEL) for a in gw[1:])

    grad_x, small, g_w_in, g_wpm, g_wps, g_wo = _local_step(
        x[0], loss_target[0], scale, shift, gate, norm_w, w_in_p, _pad_cols(b_in), ml_conv_full, ml_conv_b, ml_norm_w,
        ssm_conv_full, ssm_conv_b, ssm_a_log, ssm_d, ssm_norm_w, wpm, wps, wo, final_w[None])

    g2 = _allgather8(_pack([small[name] for name, _ in SMALL_SUMS]))
    total = dict(zip([name for name, _ in SMALL_SUMS], _unpack(_sum_parts(None, g2, g2.shape[1]), [s for _, s in SMALL_SUMS])))
    dmod_all = g2[:, :3 * D_MODEL // LANE].reshape(N_DEV, 3 * D_MODEL)
    grads = dict(total)
    grads["ada_b"] = total["mod"]
    grads["ml_conv_w"] = lax.dynamic_slice_in_dim(total["ml_conv_w"], chip * mlw_cols, mlw_cols, axis=1)
    grads["ssm_conv_w"] = lax.dynamic_slice_in_dim(total["ssm_conv_w"], chip * ssw_cols, ssw_cols, axis=1)
    grads["ada_w"] = _ada_bwd(c_all, lax.dynamic_slice_in_dim(dmod_all, chip * ada_cols, ada_cols, axis=1))

    split = lambda g, rows: _bf(g).reshape(N_CHIPS, 2, rows // (2 * N_CHIPS), g.shape[-1])
    slabs = [split(g_w_in.reshape(D_MODEL, N_CHIPS, -1).transpose(1, 0, 2), N_CHIPS * D_MODEL),
             split(g_wpm, g_wpm.shape[0]), split(g_wps, g_wps.shape[0]), split(g_wo, g_wo.shape[0])]
    received = _grad_scatter(slabs)
    halves = []
    for slab, rec in zip(slabs, received):
        own = lax.dynamic_index_in_dim(lax.dynamic_index_in_dim(slab, chip, 0, keepdims=False), pos[2], 0, keepdims=False)
        halves.append(_sum_parts(own, rec, 32))
    for name, full in zip(("w_in", "w_proj_m", "w_proj_s", "w_out"), _pair_exchange(halves)):
        grads[name] = full.reshape(-1, full.shape[-1])

    delta, new_m, new_v = {}, {}, {}
    for name in LARGE:
        d2, m2, v2 = _adamw(w[name][0], grads[name], m[name][0], v[name][0], 64)
        delta[name], new_m[name], new_v[name] = d2[None], m2[None], v2[None]
    rest = [name for name in WEIGHTS if name not in LARGE]
    packed = [_pack([t[name] for name in rest]) for t in (w, grads, m, v)]
    for out, buf in zip((delta, new_m, new_v), _adamw(*packed, packed[0].shape[0])):
        out.update(zip(rest, _unpack(buf, [w[name].shape for name in rest])))
    loss = total["loss"][0]
    return (loss, grad_x[None], *[grads[name].reshape(w[name].shape) for name in WEIGHTS], *[delta[name] for name in WEIGHTS],
            *[new_m[name] for name in WEIGHTS], *[new_v[name] for name in WEIGHTS])
```

```python
import functools
import math

import jax
import jax.numpy as jnp
from jax import lax
from jax.experimental import pallas as pl
from jax.experimental.pallas import tpu as pltpu

F32 = jnp.float32
BF16 = jnp.bfloat16
HI = lax.Precision.HIGHEST
MESH = pl.DeviceIdType.MESH

D_MODEL = 1024
EPS = 1e-6
CONV_K = 4
ML_HEADS = 8
ML_DQK = 128
ML_DV = 256
SSM_HEADS = 32
SSM_HEADDIM = 64
SSM_GROUPS = 4
SSM_STATE = 128
IN_WIDTH = 15408
N_CHIPS = 4
N_DEV = 8
ADAM_LR, ADAM_B1, ADAM_B2, ADAM_EPS, ADAM_WD, ADAM_STEP = 0.001, 0.9, 0.999, 1e-08, 0.01, 10

O_O, O_ZM, O_ZS, O_MG, O_QK, O_V, O_XBC, O_IF, O_DT = 0, 2048, 4096, 6144, 8192, 10240, 12288, 15360, 15616
SMALL_W = 256
NP = 15872
LANE = 128
CHUNK = 128
NEG = -1e30
VMEM_LIMIT = 48 * 1024 * 1024


def _cparams(*sem):
    return pltpu.CompilerParams(dimension_semantics=sem, vmem_limit_bytes=VMEM_LIMIT)


def _pad_cols(w):
    z = lambda n: jnp.zeros(w.shape[:-1] + (n,), w.dtype)
    return jnp.concatenate([w[..., 4096:8192], w[..., 11280:13328], w[..., 13360:15408], w[..., :4096], w[..., 8208:11280],
                            w[..., 8192:8208], z(SMALL_W - 16), w[..., 13328:13360], z(SMALL_W - 32)], axis=-1)


def _unpad_cols(g):
    return jnp.concatenate([g[..., O_QK:O_QK + 4096], g[..., O_O:O_O + 4096], g[..., O_IF:O_IF + 16],
                            g[..., O_XBC:O_XBC + 3072], g[..., O_ZS:O_ZS + 2048], g[..., O_DT:O_DT + 32],
                            g[..., O_MG:O_MG + 2048]], axis=-1)


def _sigmoid(x):
    return 1.0 / (1.0 + jnp.exp(-x))


def _silu(x):
    return x * _sigmoid(x)


def _dsilu(x):
    s = _sigmoid(x)
    return s + x * s * (1.0 - s)


def _softplus(x):
    return jnp.maximum(x, 0.0) + jnp.log(1.0 + jnp.exp(-jnp.abs(x)))


def _logsigmoid(x):
    return jnp.minimum(x, 0.0) - jnp.log(1.0 + jnp.exp(-jnp.abs(x)))


def _dot(a, b, dims, precision=None):
    return lax.dot_general(a, b, (dims, ((), ())), preferred_element_type=F32, precision=precision)


def _nn(a, b, precision=None):
    return _dot(a, b, ((1,), (0,)), precision)


def _nt(a, b, precision=None):
    return _dot(a, b, ((1,), (1,)), precision)


def _tn(a, b, precision=None):
    return _dot(a, b, ((0,), (0,)), precision)


def _bf(x):
    return x.astype(BF16)


def _split(x, terms):
    parts = []
    for _ in range(terms):
        part = _bf(x)
        parts.append(part)
        x = x - part.astype(F32)
    return parts


def _pick_right(x, pick, terms):
    pick = _bf(pick)
    out = None
    for part in _split(x, terms):
        out = _nn(part, pick) if out is None else out + _nn(part, pick)
    return out


def _pick_left(pick, x, terms):
    pick = _bf(pick)
    out = None
    for part in _split(x, terms):
        out = _nn(pick, part) if out is None else out + _nn(pick, part)
    return out


def _lane_col(x, lane):
    idx = lax.broadcasted_iota(jnp.int32, x.shape, 1)
    return jnp.sum(jnp.where(idx == lane, x, 0.0), axis=1, keepdims=True)


def _tri(n, upper):
    r = lax.broadcasted_iota(jnp.int32, (n, n), 0)
    c = lax.broadcasted_iota(jnp.int32, (n, n), 1)
    return jnp.where((r <= c) if upper else (r >= c), 1.0, 0.0).astype(F32)


def _eye(n):
    return jnp.where(lax.broadcasted_iota(jnp.int32, (n, n), 0) == lax.broadcasted_iota(jnp.int32, (n, n), 1), 1.0, 0.0)


def _sum_all(x):
    return jnp.sum(jnp.sum(x, axis=1, keepdims=True), axis=0, keepdims=True)


def _crossing(p):
    L = p.shape[0]
    hi = _bf(p)
    lo = _bf(p - hi.astype(F32))
    upper = _bf(_tri(L, True))
    below = _nn(upper, hi) + _nn(upper, lo)
    strict = lax.broadcasted_iota(jnp.int32, (L, L), 0) > lax.broadcasted_iota(jnp.int32, (L, L), 1)
    return jnp.sum(jnp.where(strict, below, 0.0), axis=1, keepdims=True)


def _matmul_bias(a, w, bias, tm, tn):
    m, k = a.shape
    n = w.shape[1]

    def body(a_ref, w_ref, b_ref, o_ref):
        o_ref[...] = _nn(a_ref[...], w_ref[...]) + b_ref[...]

    return pl.pallas_call(
        body, name="matmul_bias", grid=(m // tm, n // tn),
        in_specs=[pl.BlockSpec((tm, k), lambda i, j: (i, 0)), pl.BlockSpec((k, tn), lambda i, j: (0, j)),
                  pl.BlockSpec((1, tn), lambda i, j: (0, j))],
        out_specs=pl.BlockSpec((tm, tn), lambda i, j: (i, j)),
        out_shape=jax.ShapeDtypeStruct((m, n), F32),
        compiler_params=_cparams("parallel", "arbitrary"))(a, w, bias)


def _matmul_nt(a, w, tm, tk):
    m, n = a.shape
    k = w.shape[0]

    def body(a_ref, w_ref, o_ref):
        @pl.when(pl.program_id(1) == 0)
        def _():
            o_ref[...] = jnp.zeros_like(o_ref)
        o_ref[...] += _nt(a_ref[...], w_ref[...])

    return pl.pallas_call(
        body, name="matmul_nt", grid=(m // tm, n // tk),
        in_specs=[pl.BlockSpec((tm, tk), lambda i, j: (i, j)), pl.BlockSpec((k, tk), lambda i, j: (0, j))],
        out_specs=pl.BlockSpec((tm, k), lambda i, j: (i, 0)),
        out_shape=jax.ShapeDtypeStruct((m, k), F32),
        compiler_params=_cparams("parallel", "arbitrary"))(a, w)


def _matmul_tn(a, b, tm, tn, with_colsum=False):
    m, k = a.shape
    n = b.shape[1]

    def body(a_ref, b_ref, o_ref, *rest):
        first = pl.program_id(1) == 0

        @pl.when(first)
        def _():
            o_ref[...] = jnp.zeros_like(o_ref)
        o_ref[...] += _tn(a_ref[...], b_ref[...])
        if with_colsum:
            s_ref = rest[0]

            @pl.when(first)
            def _():
                s_ref[...] = jnp.zeros_like(s_ref)
            s_ref[...] += jnp.sum(b_ref[...].astype(F32), axis=0, keepdims=True)

    out_specs = [pl.BlockSpec((k, tn), lambda j, i: (0, j))]
    out_shape = [jax.ShapeDtypeStruct((k, n), F32)]
    if with_colsum:
        out_specs.append(pl.BlockSpec((1, tn), lambda j, i: (0, j)))
        out_shape.append(jax.ShapeDtypeStruct((1, n), F32))
    out = pl.pallas_call(
        body, name="matmul_tn", grid=(n // tn, m // tm),
        in_specs=[pl.BlockSpec((tm, k), lambda j, i: (i, 0)), pl.BlockSpec((tm, tn), lambda j, i: (i, j))],
        out_specs=out_specs, out_shape=out_shape,
        compiler_params=_cparams("parallel", "arbitrary"))(a, b)
    return out if with_colsum else out[0]


def _ada_fwd(c_all, ada_w, ada_b):
    def body(c_ref, w_ref, b_ref, o_ref):
        o_ref[...] = _nn(_bf(_silu(c_ref[...])), _bf(w_ref[...])) + b_ref[...]

    return pl.pallas_call(body, name="ada_fwd", out_shape=jax.ShapeDtypeStruct((c_all.shape[0], ada_w.shape[1]), F32),
                          compiler_params=_cparams())(c_all, ada_w, ada_b)


def _ada_bwd(c_all, dmod):
    def body(c_ref, d_ref, o_ref):
        o_ref[...] = _tn(_bf(_silu(c_ref[...])), _bf(d_ref[...]))

    return pl.pallas_call(body, name="ada_bwd", out_shape=jax.ShapeDtypeStruct((c_all.shape[1], dmod.shape[1]), F32),
                          compiler_params=_cparams())(c_all, dmod)


def _prenorm_fwd(x, norm_w, scale, shift, ts):
    s, d = x.shape

    def body(x_ref, nw_ref, sc_ref, sh_ref, u_ref):
        xv = x_ref[...]
        r = lax.rsqrt(jnp.mean(xv * xv, axis=1, keepdims=True) + EPS)
        u_ref[...] = _bf(xv * r * nw_ref[...] * (1.0 + sc_ref[...]) + sh_ref[...])

    row = pl.BlockSpec((1, d), lambda i: (0, 0))
    return pl.pallas_call(
        body, name="prenorm_fwd", grid=(s // ts,),
        in_specs=[pl.BlockSpec((ts, d), lambda i: (i, 0)), row, row, row],
        out_specs=pl.BlockSpec((ts, d), lambda i: (i, 0)), out_shape=jax.ShapeDtypeStruct((s, d), BF16),
        compiler_params=_cparams("parallel"))(x, norm_w, scale, shift)


def _prenorm_bwd(du, x, dxres, norm_w, scale, ts):
    s, d = x.shape

    def body(du_ref, x_ref, dr_ref, nw_ref, sc_ref, gx_ref, acc_ref):
        @pl.when(pl.program_id(0) == 0)
        def _():
            acc_ref[...] = jnp.zeros_like(acc_ref)
        xv, duv = x_ref[...], du_ref[...]
        r = lax.rsqrt(jnp.mean(xv * xv, axis=1, keepdims=True) + EPS)
        xn = xv * r
        nw, sc1 = nw_ref[...], 1.0 + sc_ref[...]
        dxn = duv * (nw * sc1)
        gx_ref[...] = r * (dxn - xn * jnp.mean(dxn * xn, axis=1, keepdims=True)) + dr_ref[...]
        t = duv * xn
        acc_ref[0:1, :] += jnp.sum(t, axis=0, keepdims=True) * sc1
        acc_ref[1:2, :] += jnp.sum(t, axis=0, keepdims=True) * nw
        acc_ref[2:3, :] += jnp.sum(duv, axis=0, keepdims=True)

    tile = pl.BlockSpec((ts, d), lambda i: (i, 0))
    row = pl.BlockSpec((1, d), lambda i: (0, 0))
    return pl.pallas_call(
        body, name="prenorm_bwd", grid=(s // ts,),
        in_specs=[tile, tile, tile, row, row],
        out_specs=[tile, pl.BlockSpec((8, d), lambda i: (0, 0))],
        out_shape=[jax.ShapeDtypeStruct((s, d), F32), jax.ShapeDtypeStruct((8, d), F32)],
        compiler_params=_cparams("arbitrary"))(du, x, dxres, norm_w, scale)


CONV_CB = 512


def _conv_taps(buf_ref, ts):
    return [buf_ref[pl.ds(8 - (CONV_K - 1) + j, ts), :] for j in range(CONV_K)]


def _conv_fwd(proj, col0, width, w8, b, ts):
    s = proj.shape[0]
    cb = CONV_CB
    nt = s // ts

    def body(x_ref, w_ref, b_ref, o_ref, buf_ref):
        @pl.when(pl.program_id(1) == 0)
        def _():
            buf_ref[0:8, :] = jnp.zeros((8, cb), F32)
        buf_ref[pl.ds(8, ts), :] = x_ref[...]
        acc = b_ref[...] + jnp.zeros((ts, cb), F32)
        for j, tap in enumerate(_conv_taps(buf_ref, ts)):
            acc = acc + tap * w_ref[j:j + 1, :]
        o_ref[...] = _silu(acc)
        buf_ref[0:8, :] = x_ref[pl.ds(ts - 8, 8), :]

    c0 = col0 // cb
    return pl.pallas_call(
        body, name="conv_fwd", grid=(width // cb, nt),
        in_specs=[pl.BlockSpec((ts, cb), lambda c, i: (i, c0 + c)), pl.BlockSpec((8, cb), lambda c, i: (0, c)),
                  pl.BlockSpec((1, cb), lambda c, i: (0, c))],
        out_specs=pl.BlockSpec((ts, cb), lambda c, i: (i, c)),
        out_shape=jax.ShapeDtypeStruct((s, width), F32),
        scratch_shapes=[pltpu.VMEM((ts + 8, cb), F32)],
        compiler_params=_cparams("parallel", "arbitrary"))(proj, w8, b)


def _conv_bwd(proj, col0, width, w8, b, dpost, dproj, ts):
    s = proj.shape[0]
    cb = CONV_CB
    nt = s // ts
    c0 = col0 // cb

    def body(x_ref, xh_ref, dp_ref, w_ref, b_ref, _, dx_ref, acc_ref, buf_ref, dbuf_ref):
        step = pl.program_id(1)
        tile = nt - 1 - step

        @pl.when(step == 0)
        def _():
            acc_ref[...] = jnp.zeros_like(acc_ref)
            dbuf_ref[pl.ds(ts, 8), :] = jnp.zeros((8, cb), F32)
        buf_ref[0:8, :] = jnp.where(tile == 0, 0.0, xh_ref[...])
        buf_ref[pl.ds(8, ts), :] = x_ref[...]
        taps = _conv_taps(buf_ref, ts)
        acc = b_ref[...] + jnp.zeros((ts, cb), F32)
        for j in range(CONV_K):
            acc = acc + taps[j] * w_ref[j:j + 1, :]
        dconv = dp_ref[...] * _dsilu(acc)
        acc_ref[4:5, :] += jnp.sum(dconv, axis=0, keepdims=True)
        for j in range(CONV_K):
            acc_ref[j:j + 1, :] += jnp.sum(taps[j] * dconv, axis=0, keepdims=True)
        dbuf_ref[pl.ds(0, ts), :] = dconv
        dx = jnp.zeros((ts, cb), F32)
        for j in range(CONV_K):
            dx = dx + dbuf_ref[pl.ds(CONV_K - 1 - j, ts), :] * w_ref[j:j + 1, :]
        dx_ref[...] = _bf(dx)
        dbuf_ref[pl.ds(ts, 8), :] = dconv[0:8, :]

    rows8 = ts // 8
    return pl.pallas_call(
        body, name="conv_bwd", grid=(width // cb, nt),
        in_specs=[pl.BlockSpec((ts, cb), lambda c, i: (nt - 1 - i, c0 + c)),
                  pl.BlockSpec((8, cb), lambda c, i: (jnp.maximum((nt - 1 - i) * rows8 - 1, 0), c0 + c)),
                  pl.BlockSpec((ts, cb), lambda c, i: (nt - 1 - i, c)),
                  pl.BlockSpec((8, cb), lambda c, i: (0, c)), pl.BlockSpec((1, cb), lambda c, i: (0, c)),
                  pl.BlockSpec(memory_space=pl.ANY)],
        out_specs=[pl.BlockSpec((ts, cb), lambda c, i: (nt - 1 - i, c0 + c)), pl.BlockSpec((8, cb), lambda c, i: (0, c))],
        out_shape=[jax.ShapeDtypeStruct(dproj.shape, dproj.dtype), jax.ShapeDtypeStruct((8, width), F32)],
        input_output_aliases={5: 0},
        scratch_shapes=[pltpu.VMEM((ts + 8, cb), F32), pltpu.VMEM((ts + 8, cb), F32)],
        compiler_params=_cparams("parallel", "arbitrary"))(proj, proj, dpost, w8, b, dproj)


def _mlstm_gates(gif_ref, gt_ref, a_scr, at_scr):
    L = gif_ref.shape[0]
    fb = _logsigmoid(gif_ref[...])
    a_scr[...] = _pick_left(_tri(L, False), fb, 3)
    at_scr[...] = _pick_right(_logsigmoid(gt_ref[...]), _tri(L, True), 3)
    return jnp.sum(fb, axis=0, keepdims=True)


def _mlstm_head(h, qk_ref, v_ref, gif, gt_ref, a, at_scr, a_last_row, c_mat, n_row, m_prev):
    L = gif.shape[0]
    q = qk_ref[:, h * ML_DQK:(h + 1) * ML_DQK] * (ML_DQK ** -0.5)
    k = qk_ref[:, (ML_HEADS + h) * ML_DQK:(ML_HEADS + h + 1) * ML_DQK]
    v = v_ref[:, h * ML_DV:(h + 1) * ML_DV]
    i_col, a_col = _lane_col(gif, h), _lane_col(a, ML_HEADS + h)
    i_row, a_row = gt_ref[h:h + 1, :], at_scr[ML_HEADS + h:ML_HEADS + h + 1, :]
    causal = lax.broadcasted_iota(jnp.int32, (L, L), 0) >= lax.broadcasted_iota(jnp.int32, (L, L), 1)
    dmat = jnp.where(causal, a_col - a_row + i_row, NEG)
    inter = a_col + m_prev
    m_t = jnp.maximum(inter, jnp.max(dmat, axis=1, keepdims=True))
    w_intra = jnp.exp(dmat - m_t)
    w_inter = jnp.exp(inter - m_t)
    sc = _nt(_bf(q), _bf(k)) * w_intra
    den = jnp.sum(sc, axis=1, keepdims=True) + w_inter * jnp.sum(q * n_row, axis=1, keepdims=True)
    floor = jnp.exp(-m_t)
    a_last = _lane_col(a_last_row, ML_HEADS + h)
    g = a_last - a_col + i_col
    m_new = jnp.maximum(a_last + m_prev, jnp.max(g, axis=0, keepdims=True))
    wk = jnp.exp(g - m_new)
    decay = jnp.exp(a_last + m_prev - m_new)
    return dict(q=q, k=k, v=v, w_intra=w_intra, w_inter=w_inter, sc=sc, den=den, floor=floor, m_new=m_new, wk=wk,
                decay=decay)


def _state_tile(n_row, m11):
    r = lax.broadcasted_iota(jnp.int32, (8, LANE), 0)
    return jnp.where(r == 0, n_row, jnp.where(r == 1, m11, 0.0))


def _mlstm_fwd(qk, proj, gt):
    s = qk.shape[0]
    L = CHUNK
    nc = s // L

    def body(qk_ref, v_ref, gif_ref, gt_ref, h_ref, cst_ref, nm_ref, c_scr, nm_scr, a_scr, at_scr):
        @pl.when(pl.program_id(0) == 0)
        def _():
            c_scr[...] = jnp.zeros_like(c_scr)
            nm_scr[...] = jnp.zeros_like(nm_scr)
        a_last_row = _mlstm_gates(gif_ref, gt_ref, a_scr, at_scr)
        gif, a = gif_ref[...], a_scr[...]
        for h in range(ML_HEADS):
            c_mat, n_row = c_scr[h], nm_scr[h, 0:1, :]
            m_prev = jnp.max(nm_scr[h, 1:2, :], axis=1, keepdims=True)
            cst_ref[0, h] = c_mat
            nm_ref[0, h] = nm_scr[h]
            t = _mlstm_head(h, qk_ref, v_ref, gif, gt_ref, a, at_scr, a_last_row, c_mat, n_row, m_prev)
            num = _nn(_bf(t["sc"]), _bf(t["v"])) + t["w_inter"] * _nn(_bf(t["q"]), _bf(c_mat))
            h_ref[:, h * ML_DV:(h + 1) * ML_DV] = num / jnp.maximum(jnp.abs(t["den"]), t["floor"])
            kw = t["k"] * t["wk"]
            c_scr[h] = t["decay"] * c_mat + _tn(_bf(kw), _bf(t["v"]))
            nm_scr[h] = _state_tile(t["decay"] * n_row + jnp.sum(kw, axis=0, keepdims=True), t["m_new"])

    return pl.pallas_call(
        body, name="mlstm_fwd", grid=(nc,),
        in_specs=[pl.BlockSpec((L, 2048), lambda c: (c, 0)), pl.BlockSpec((L, 2048), lambda c: (c, O_V // 2048)),
                  pl.BlockSpec((L, LANE), lambda c: (c, O_IF // LANE)), pl.BlockSpec((LANE, L), lambda c: (0, c))],
        out_specs=[pl.BlockSpec((L, 2048), lambda c: (c, 0)),
                   pl.BlockSpec((1, ML_HEADS, ML_DQK, ML_DV), lambda c: (c, 0, 0, 0)),
                   pl.BlockSpec((1, ML_HEADS, 8, LANE), lambda c: (c, 0, 0, 0))],
        out_shape=[jax.ShapeDtypeStruct((s, 2048), F32), jax.ShapeDtypeStruct((nc, ML_HEADS, ML_DQK, ML_DV), F32),
                   jax.ShapeDtypeStruct((nc, ML_HEADS, 8, LANE), F32)],
        scratch_shapes=[pltpu.VMEM((ML_HEADS, ML_DQK, ML_DV), F32), pltpu.VMEM((ML_HEADS, 8, LANE), F32),
                        pltpu.VMEM((L, LANE), F32), pltpu.VMEM((LANE, L), F32)],
        compiler_params=_cparams("arbitrary"))(qk, proj, proj, gt)


def _mlstm_bwd(qk, proj, gt, hout, dh, cst, nm, dproj):
    s = qk.shape[0]
    L = CHUNK
    nc = s // L

    def body(qk_ref, v_ref, gif_ref, gt_ref, h_ref, dh_ref, cst_ref, nm_ref, _, dqk_ref, dv_ref, dif_ref,
             dc_scr, dn_scr, a_scr, at_scr):
        @pl.when(pl.program_id(0) == 0)
        def _():
            dc_scr[...] = jnp.zeros_like(dc_scr)
            dn_scr[...] = jnp.zeros_like(dn_scr)
        a_last_row = _mlstm_gates(gif_ref, gt_ref, a_scr, at_scr)
        gif, a = gif_ref[...], a_scr[...]
        lane = lax.broadcasted_iota(jnp.int32, (L, LANE), 1)
        last = lax.broadcasted_iota(jnp.int32, (L, 1), 0) == L - 1
        di_tile = jnp.zeros((L, LANE), F32)
        cross = [jnp.zeros((L, LANE), F32)] * 3
        for h in range(ML_HEADS):
            c_mat, n_row = cst_ref[0, h], nm_ref[0, h, 0:1, :]
            m_prev = jnp.max(nm_ref[0, h, 1:2, :], axis=1, keepdims=True)
            t = _mlstm_head(h, qk_ref, v_ref, gif, gt_ref, a, at_scr, a_last_row, c_mat, n_row, m_prev)
            q, k, v, den = t["q"], t["k"], t["v"], t["den"]
            dhh = dh_ref[:, h * ML_DV:(h + 1) * ML_DV]
            hh = h_ref[:, h * ML_DV:(h + 1) * ML_DV]
            dnorm = jnp.maximum(jnp.abs(den), t["floor"])
            dnum = dhh / dnorm
            d_dn = -jnp.sum(dhh * hh, axis=1, keepdims=True) / dnorm
            dden = jnp.where(jnp.abs(den) >= t["floor"], jnp.where(den >= 0.0, d_dn, -d_dn), 0.0)
            dsc = _nt(_bf(dnum), _bf(v)) + dden
            ds = dsc * t["w_intra"]
            dq_inter = t["w_inter"] * (_nt(_bf(dnum), _bf(c_mat)) + dden * n_row)
            dq = _nn(_bf(ds), _bf(k)) + dq_inter
            dc, dn_row = dc_scr[h], dn_scr[h, 0:1, :]
            dk_state = t["wk"] * (_nt(_bf(v), _bf(dc)) + dn_row)
            dk = _tn(_bf(ds), _bf(q)) + dk_state
            dv = _tn(_bf(t["sc"]), _bf(dnum)) + t["wk"] * _nn(_bf(k), _bf(dc))
            qi = q * t["w_inter"]
            dc_scr[h] = t["decay"] * dc + _tn(_bf(qi), _bf(dnum))
            dn_scr[h] = jnp.broadcast_to(t["decay"] * dn_row + jnp.sum(qi * dden, axis=0, keepdims=True), (8, LANE))
            dqk_ref[:, h * ML_DQK:(h + 1) * ML_DQK] = dq * (ML_DQK ** -0.5)
            dqk_ref[:, (ML_HEADS + h) * ML_DQK:(ML_HEADS + h + 1) * ML_DQK] = dk
            dv_ref[:, h * ML_DV:(h + 1) * ML_DV] = _bf(dv)
            di_tile = di_tile + jnp.where(lane == h, jnp.sum(k * dk, axis=1, keepdims=True), 0.0)
            carried = t["decay"] * (_sum_all(dc * c_mat) + jnp.sum(dn_row * n_row, axis=1, keepdims=True))
            parts = (_crossing(dsc * t["sc"]),
                     jnp.sum(q * dq_inter, axis=1, keepdims=True) + jnp.where(last, carried, 0.0),
                     jnp.sum(k * dk_state, axis=1, keepdims=True))
            cross = [c + jnp.where(lane == ML_HEADS + h, p, 0.0) for c, p in zip(cross, parts)]
        dfb = cross[0] + _pick_left(_tri(L, True), cross[1], 2) + _pick_left(_tri(L, False) - _eye(L), cross[2], 2)
        dif_ref[:, 0:LANE] = _bf(di_tile + dfb * _sigmoid(-gif))
        dif_ref[:, LANE:SMALL_W] = jnp.zeros((L, SMALL_W - LANE), BF16)

    rev = lambda c: nc - 1 - c
    return pl.pallas_call(
        body, name="mlstm_bwd", grid=(nc,),
        in_specs=[pl.BlockSpec((L, 2048), lambda c: (rev(c), 0)), pl.BlockSpec((L, 2048), lambda c: (rev(c), O_V // 2048)),
                  pl.BlockSpec((L, LANE), lambda c: (rev(c), O_IF // LANE)), pl.BlockSpec((LANE, L), lambda c: (0, rev(c))),
                  pl.BlockSpec((L, 2048), lambda c: (rev(c), 0)), pl.BlockSpec((L, 2048), lambda c: (rev(c), 0)),
                  pl.BlockSpec((1, ML_HEADS, ML_DQK, ML_DV), lambda c: (rev(c), 0, 0, 0)),
                  pl.BlockSpec((1, ML_HEADS, 8, LANE), lambda c: (rev(c), 0, 0, 0)), pl.BlockSpec(memory_space=pl.ANY)],
        out_specs=[pl.BlockSpec((L, 2048), lambda c: (rev(c), 0)), pl.BlockSpec((L, 2048), lambda c: (rev(c), O_V // 2048)),
                   pl.BlockSpec((L, SMALL_W), lambda c: (rev(c), 0))],
        out_shape=[jax.ShapeDtypeStruct((s, 2048), F32), jax.ShapeDtypeStruct(dproj.shape, dproj.dtype),
                   jax.ShapeDtypeStruct((s, SMALL_W), BF16)],
        input_output_aliases={8: 1},
        scratch_shapes=[pltpu.VMEM((ML_HEADS, ML_DQK, ML_DV), F32), pltpu.VMEM((ML_HEADS, 8, LANE), F32),
                        pltpu.VMEM((L, LANE), F32), pltpu.VMEM((LANE, L), F32)],
        compiler_params=_cparams("arbitrary"))(qk, proj, proj, gt, hout, dh, cst, nm, dproj)


GROUP_W = SSM_HEADS // SSM_GROUPS * SSM_HEADDIM
O_B = SSM_HEADS * SSM_HEADDIM
O_C = O_B + SSM_GROUPS * SSM_STATE


def _head_expand():
    r = jnp.arange(LANE)[:, None]
    c = jnp.arange(SSM_HEADS * SSM_HEADDIM)[None, :] // SSM_HEADDIM
    return (r == c).astype(F32)


def _ssd_gates(dt_ref, dtt_ref, alog_row_ref, alog_col_ref, at_scr):
    L = dt_ref.shape[0]
    dt = _softplus(dt_ref[...])
    acoef = -jnp.exp(alog_row_ref[...])
    a = _pick_left(_tri(L, False), dt * acoef, 3)
    at_scr[...] = _pick_right(_softplus(dtt_ref[...]) * (-jnp.exp(alog_col_ref[...])), _tri(L, True), 3)
    return dt, acoef, a


def _ssd_group(g, xbc_ref, dt, a, e_ref, ax_scr):
    eg = e_ref[:, g * GROUP_W:(g + 1) * GROUP_W]
    ax_scr[...] = _pick_right(a, eg, 3)
    ax = ax_scr[...]
    alx = ax_scr[ax.shape[0] - 1:ax.shape[0], :]
    dtx = _pick_right(dt, eg, 2)
    xg = xbc_ref[:, g * GROUP_W:(g + 1) * GROUP_W]
    bg = xbc_ref[:, O_B + g * SSM_STATE:O_B + (g + 1) * SSM_STATE]
    cg = xbc_ref[:, O_C + g * SSM_STATE:O_C + (g + 1) * SSM_STATE]
    return dict(ax=ax, alx=alx, dtx=dtx, xg=xg, bg=bg, cg=cg, xdt=xg * dtx, gmat=_nt(_bf(cg), _bf(bg)))


def _ssd_decay(hh, a, at_scr):
    L = a.shape[0]
    causal = lax.broadcasted_iota(jnp.int32, (L, L), 0) >= lax.broadcasted_iota(jnp.int32, (L, L), 1)
    return jnp.exp(jnp.where(causal, _lane_col(a, hh) - at_scr[hh:hh + 1, :], NEG))


def _ssd_fwd(xbc, proj, dtt, alog_row, alog_col, dskip_x, expand):
    s = xbc.shape[0]
    L = CHUNK
    nc = s // L
    half = SSM_HEADDIM

    def body(xbc_ref, dt_ref, dtt_ref, ar_ref, ac_ref, dk_ref, e_ref, y_ref, st_ref, st_scr, at_scr, ax_scr):
        @pl.when(pl.program_id(0) == 0)
        def _():
            st_scr[...] = jnp.zeros_like(st_scr)
        dt, _, a = _ssd_gates(dt_ref, dtt_ref, ar_ref, ac_ref, at_scr)
        lane = lax.broadcasted_iota(jnp.int32, (L, LANE), 1)
        for g in range(SSM_GROUPS):
            t = _ssd_group(g, xbc_ref, dt, a, e_ref, ax_scr)
            st = st_scr[g]
            st_ref[0, g] = st
            pairs = []
            for j in range(GROUP_W // LANE):
                xp = _bf(t["xdt"][:, j * LANE:(j + 1) * LANE])
                hh = g * (SSM_HEADS // SSM_GROUPS) + 2 * j
                y0 = _nn(_bf(t["gmat"] * _ssd_decay(hh, a, at_scr)), xp)
                y1 = _nn(_bf(t["gmat"] * _ssd_decay(hh + 1, a, at_scr)), xp)
                pairs.append(jnp.where(lane < half, y0, y1))
            y = jnp.concatenate(pairs, axis=1) + _nn(_bf(t["cg"]), _bf(st)) * jnp.exp(t["ax"])
            y_ref[:, g * GROUP_W:(g + 1) * GROUP_W] = y + dk_ref[:, g * GROUP_W:(g + 1) * GROUP_W] * t["xg"]
            wts = jnp.exp(t["alx"] - t["ax"])
            st_scr[g] = jnp.exp(t["alx"]) * st + _tn(_bf(t["bg"]), _bf(t["xdt"] * wts))

    row = lambda w: pl.BlockSpec((1, w), lambda c: (0, 0))
    return pl.pallas_call(
        body, name="ssd_fwd", grid=(nc,),
        in_specs=[pl.BlockSpec((L, 3072), lambda c: (c, 0)), pl.BlockSpec((L, LANE), lambda c: (c, O_DT // LANE)),
                  pl.BlockSpec((LANE, L), lambda c: (0, c)), row(LANE), pl.BlockSpec((LANE, 1), lambda c: (0, 0)),
                  row(2048), pl.BlockSpec((LANE, 2048), lambda c: (0, 0))],
        out_specs=[pl.BlockSpec((L, 2048), lambda c: (c, 0)),
                   pl.BlockSpec((1, SSM_GROUPS, SSM_STATE, GROUP_W), lambda c: (c, 0, 0, 0))],
        out_shape=[jax.ShapeDtypeStruct((s, 2048), F32),
                   jax.ShapeDtypeStruct((nc, SSM_GROUPS, SSM_STATE, GROUP_W), F32)],
        scratch_shapes=[pltpu.VMEM((SSM_GROUPS, SSM_STATE, GROUP_W), F32), pltpu.VMEM((LANE, L), F32),
                        pltpu.VMEM((L, GROUP_W), F32)],
        compiler_params=_cparams("arbitrary"))(xbc, proj, dtt, alog_row, alog_col, dskip_x, expand)


def _ssd_bwd(xbc, proj, dtt, alog_row, alog_col, dskip_x, expand, expand_t, dy, states):
    s = xbc.shape[0]
    L = CHUNK
    nc = s // L
    half = SSM_HEADDIM

    def body(xbc_ref, dt_ref, dtt_ref, ar_ref, ac_ref, dk_ref, e_ref, et_ref, dy_ref, st_ref,
             dxbc_ref, ddt_ref, accd_ref, acca_ref, dst_scr, at_scr, ax_scr):
        @pl.when(pl.program_id(0) == 0)
        def _():
            dst_scr[...] = jnp.zeros_like(dst_scr)
            accd_ref[...] = jnp.zeros_like(accd_ref)
            acca_ref[...] = jnp.zeros_like(acca_ref)
        dt, acoef, a = _ssd_gates(dt_ref, dtt_ref, ar_ref, ac_ref, at_scr)
        lane = lax.broadcasted_iota(jnp.int32, (L, LANE), 1)
        low = lane < half
        last = lax.broadcasted_iota(jnp.int32, (L, 1), 0) == L - 1
        cross = [jnp.zeros((L, LANE), F32)] * 3
        ddt_tile = jnp.zeros((L, LANE), F32)
        for g in range(SSM_GROUPS):
            t = _ssd_group(g, xbc_ref, dt, a, e_ref, ax_scr)
            xg, bg, cg, xdt, gmat = t["xg"], t["bg"], t["cg"], t["xdt"], t["gmat"]
            st, dst = st_ref[0, g], dst_scr[g]
            dyg = dy_ref[:, g * GROUP_W:(g + 1) * GROUP_W]
            ea, eal = jnp.exp(t["ax"]), jnp.exp(t["alx"])
            wts = jnp.exp(t["alx"] - t["ax"])
            dyi = dyg * ea
            y_inter = _nn(_bf(cg), _bf(st)) * ea
            dc = _nt(_bf(dyi), _bf(st))
            d_xdt_state = _nn(_bf(bg), _bf(dst)) * wts
            db = _nt(_bf(xdt * wts), _bf(dst))
            dst_scr[g] = eal * dst + _tn(_bf(cg), _bf(dyi))
            dg = jnp.zeros((L, L), F32)
            dx_pairs = []
            for j in range(GROUP_W // LANE):
                xp = _bf(xdt[:, j * LANE:(j + 1) * LANE])
                dyp = dyg[:, j * LANE:(j + 1) * LANE]
                dxs = []
                for b in range(2):
                    hh = g * (SSM_HEADS // SSM_GROUPS) + 2 * j + b
                    dec = _ssd_decay(hh, a, at_scr)
                    w = gmat * dec
                    dxs.append(_tn(_bf(w), _bf(dyp)))
                    dw = _nt(_bf(jnp.where(low if b == 0 else ~low, dyp, 0.0)), xp)
                    dg = dg + dw * dec
                    cross[0] = cross[0] + jnp.where(lane == hh, _crossing(dw * w), 0.0)
                dx_pairs.append(jnp.where(low, dxs[0], dxs[1]))
            d_xdt = d_xdt_state + jnp.concatenate(dx_pairs, axis=1)
            dc = dc + _nn(_bf(dg), _bf(bg))
            db = db + _tn(_bf(dg), _bf(cg))
            etg = et_ref[g * GROUP_W:(g + 1) * GROUP_W, :]
            carried = jnp.sum(dst * st, axis=0, keepdims=True) * eal
            cross[1] = cross[1] + _pick_right(dyg * y_inter + jnp.where(last, carried, 0.0), etg, 2)
            cross[2] = cross[2] + _pick_right(xdt * d_xdt_state, etg, 2)
            ddt_tile = ddt_tile + _pick_right(d_xdt * xg, etg, 2)
            dxbc_ref[:, g * GROUP_W:(g + 1) * GROUP_W] = d_xdt * t["dtx"] + dk_ref[:, g * GROUP_W:(g + 1) * GROUP_W] * dyg
            dxbc_ref[:, O_B + g * SSM_STATE:O_B + (g + 1) * SSM_STATE] = db
            dxbc_ref[:, O_C + g * SSM_STATE:O_C + (g + 1) * SSM_STATE] = dc
            accd_ref[0:1, g * GROUP_W:(g + 1) * GROUP_W] += jnp.sum(dyg * xg, axis=0, keepdims=True)
        d_da = cross[0] + _pick_left(_tri(L, True), cross[1], 2) + _pick_left(_tri(L, False) - _eye(L), cross[2], 2)
        acca_ref[0:1, :] += jnp.sum(d_da * dt, axis=0, keepdims=True)
        ddt_ref[:, 0:LANE] = _bf((ddt_tile + d_da * acoef) * _sigmoid(dt_ref[...]))
        ddt_ref[:, LANE:SMALL_W] = jnp.zeros((L, SMALL_W - LANE), BF16)

    rev = lambda c: nc - 1 - c
    row = lambda w: pl.BlockSpec((1, w), lambda c: (0, 0))
    return pl.pallas_call(
        body, name="ssd_bwd", grid=(nc,),
        in_specs=[pl.BlockSpec((L, 3072), lambda c: (rev(c), 0)), pl.BlockSpec((L, LANE), lambda c: (rev(c), O_DT // LANE)),
                  pl.BlockSpec((LANE, L), lambda c: (0, rev(c))), row(LANE), pl.BlockSpec((LANE, 1), lambda c: (0, 0)),
                  row(2048), pl.BlockSpec((LANE, 2048), lambda c: (0, 0)), pl.BlockSpec((2048, LANE), lambda c: (0, 0)),
                  pl.BlockSpec((L, 2048), lambda c: (rev(c), 0)),
                  pl.BlockSpec((1, SSM_GROUPS, SSM_STATE, GROUP_W), lambda c: (rev(c), 0, 0, 0))],
        out_specs=[pl.BlockSpec((L, 3072), lambda c: (rev(c), 0)), pl.BlockSpec((L, SMALL_W), lambda c: (rev(c), 0)),
                   pl.BlockSpec((8, 2048), lambda c: (0, 0)), pl.BlockSpec((8, LANE), lambda c: (0, 0))],
        out_shape=[jax.ShapeDtypeStruct((s, 3072), F32), jax.ShapeDtypeStruct((s, SMALL_W), BF16),
                   jax.ShapeDtypeStruct((8, 2048), F32), jax.ShapeDtypeStruct((8, LANE), F32)],
        scratch_shapes=[pltpu.VMEM((SSM_GROUPS, SSM_STATE, GROUP_W), F32),
                        pltpu.VMEM((LANE, L), F32), pltpu.VMEM((L, GROUP_W), F32)],
        compiler_params=_cparams("arbitrary"))(xbc, proj, dtt, alog_row, alog_col, dskip_x, expand, expand_t, dy, states)


def _group_norm(v, width):
    outs, rs = [], []
    for k in range(v.shape[1] // width):
        blk = v[:, k * width:(k + 1) * width]
        r = lax.rsqrt(jnp.mean(blk * blk, axis=1, keepdims=True) + EPS)
        outs.append(blk * r)
        rs.append(jnp.broadcast_to(r, blk.shape))
    return jnp.concatenate(outs, axis=1), jnp.concatenate(rs, axis=1)


def _group_mean(v, width):
    return jnp.concatenate([jnp.broadcast_to(jnp.mean(v[:, k * width:(k + 1) * width], axis=1, keepdims=True),
                                             (v.shape[0], width)) for k in range(v.shape[1] // width)], axis=1)


def _post_fwd(hm, yssd, proj, ml_norm_w, ssm_norm_w, ts):
    s = hm.shape[0]

    def body(h_ref, ys_ref, o_ref, zm_ref, zs_ref, wm_ref, ws_ref, ym_ref, yso_ref):
        hn, _ = _group_norm(h_ref[...], ML_DV)
        ym_ref[...] = _bf(_sigmoid(o_ref[...]) * hn * wm_ref[...] * _silu(zm_ref[...]))
        pn, _ = _group_norm(ys_ref[...] * _silu(zs_ref[...]), GROUP_W)
        yso_ref[...] = _bf(pn * ws_ref[...])

    tile = pl.BlockSpec((ts, 2048), lambda i: (i, 0))
    col = lambda off: pl.BlockSpec((ts, 2048), lambda i: (i, off // 2048))
    row = pl.BlockSpec((1, 2048), lambda i: (0, 0))
    return pl.pallas_call(
        body, name="post_fwd", grid=(s // ts,),
        in_specs=[tile, tile, col(O_O), col(O_ZM), col(O_ZS), row, row],
        out_specs=[tile, tile],
        out_shape=[jax.ShapeDtypeStruct((s, 2048), BF16)] * 2,
        compiler_params=_cparams("parallel"))(hm, yssd, proj, proj, proj, ml_norm_w, ssm_norm_w)


def _post_bwd(dym, dys, hm, yssd, proj, ml_norm_w, ssm_norm_w, dproj, ts):
    s = hm.shape[0]

    def body(dym_ref, dys_ref, h_ref, ys_ref, o_ref, zm_ref, zs_ref, wm_ref, ws_ref, _,
             dh_ref, dyssd_ref, dp_ref, acc_ref):
        @pl.when(pl.program_id(0) == 0)
        def _():
            acc_ref[...] = jnp.zeros_like(acc_ref)
        hn, r = _group_norm(h_ref[...], ML_DV)
        so, zm, wm, d_ym = _sigmoid(o_ref[...]), zm_ref[...], wm_ref[...], dym_ref[...]
        sz = _silu(zm)
        hnw = hn * wm
        dp_ref[:, O_O:O_O + 2048] = _bf(d_ym * hnw * sz * so * (1.0 - so))
        dp_ref[:, O_ZM:O_ZM + 2048] = _bf(d_ym * so * hnw * _dsilu(zm))
        dhnw = d_ym * so * sz
        acc_ref[0:1, :] += jnp.sum(dhnw * hn, axis=0, keepdims=True)
        dhn = dhnw * wm
        dh_ref[...] = r * (dhn - hn * _group_mean(dhn * hn, ML_DV))
        ysv, zs, d_ys = ys_ref[...], zs_ref[...], dys_ref[...]
        szs = _silu(zs)
        pn, r2 = _group_norm(ysv * szs, GROUP_W)
        acc_ref[1:2, :] += jnp.sum(d_ys * pn, axis=0, keepdims=True)
        dpn = d_ys * ws_ref[...]
        dp = r2 * (dpn - pn * _group_mean(dpn * pn, GROUP_W))
        dyssd_ref[...] = dp * szs
        dp_ref[:, O_ZS:O_ZS + 2048] = _bf(dp * ysv * _dsilu(zs))

    tile = pl.BlockSpec((ts, 2048), lambda i: (i, 0))
    col = lambda off: pl.BlockSpec((ts, 2048), lambda i: (i, off // 2048))
    row = pl.BlockSpec((1, 2048), lambda i: (0, 0))
    sds = lambda dt: jax.ShapeDtypeStruct((s, 2048), dt)
    return pl.pallas_call(
        body, name="post_bwd", grid=(s // ts,),
        in_specs=[tile, tile, tile, tile, col(O_O), col(O_ZM), col(O_ZS), row, row, pl.BlockSpec(memory_space=pl.ANY)],
        out_specs=[tile, tile, pl.BlockSpec((ts, O_MG), lambda i: (i, 0)), pl.BlockSpec((8, 2048), lambda i: (0, 0))],
        out_shape=[sds(F32), sds(F32), jax.ShapeDtypeStruct(dproj.shape, dproj.dtype), jax.ShapeDtypeStruct((8, 2048), F32)],
        input_output_aliases={9: 2},
        compiler_params=_cparams("arbitrary"))(dym, dys, hm, yssd, proj, proj, proj, ml_norm_w, ssm_norm_w, dproj)


def _merge(x, ym, ys, proj, target, gate, final_w, wpm, wps, wo, ts):
    wpm_t, wps_t, wo_t = wpm.T, wps.T, wo.T
    s, d = x.shape

    def body(x_ref, ym_ref, ys_ref, mg_ref, t_ref, gate_ref, fw_ref, wpm_ref, wps_ref, wo_ref, wpmt_ref, wpst_ref, wot_ref,
             dres_ref, mer_ref, dmo_ref, dpm_ref, dps_ref, dym_ref, dys_ref, dmg_ref, acc_ref):
        @pl.when(pl.program_id(0) == 0)
        def _():
            acc_ref[...] = jnp.zeros_like(acc_ref)
        gm, gs = _sigmoid(mg_ref[:, 0:d]), _sigmoid(mg_ref[:, d:2 * d])
        pm = _nn(ym_ref[...], wpm_ref[...])
        ps = _nn(ys_ref[...], wps_ref[...])
        merged = _bf(gm * pm + gs * ps)
        mer_ref[...] = merged
        mo = _nn(merged, wo_ref[...])
        gate, fw = gate_ref[...], fw_ref[...]
        out = x_ref[...] + gate * mo
        r = lax.rsqrt(jnp.mean(out * out, axis=1, keepdims=True) + EPS)
        on = out * r
        diff = on * fw - t_ref[...]
        acc_ref[0:1, :] += jnp.sum(0.5 * jnp.sum(diff * diff, axis=1, keepdims=True) / d, axis=0, keepdims=True)
        dyv = diff * (1.0 / d)
        acc_ref[1:2, :] += jnp.sum(dyv * on, axis=0, keepdims=True)
        don = dyv * fw
        dout = r * (don - on * jnp.mean(don * on, axis=1, keepdims=True))
        dres_ref[...] = dout
        acc_ref[2:3, :] += jnp.sum(dout * mo, axis=0, keepdims=True)
        dmo = _bf(dout * gate)
        dmo_ref[...] = dmo
        dmer = _nn(dmo, wot_ref[...])
        dpm, dps = _bf(dmer * gm), _bf(dmer * gs)
        dpm_ref[...] = dpm
        dps_ref[...] = dps
        dmg_ref[:, 0:d] = _bf(dmer * pm * gm * (1.0 - gm))
        dmg_ref[:, d:2 * d] = _bf(dmer * ps * gs * (1.0 - gs))
        dym_ref[...] = _nn(dpm, wpmt_ref[...])
        dys_ref[...] = _nn(dps, wpst_ref[...])

    t1 = pl.BlockSpec((ts, d), lambda i: (i, 0))
    t2 = pl.BlockSpec((ts, 2 * d), lambda i: (i, 0))
    row = pl.BlockSpec((1, d), lambda i: (0, 0))
    whole = pl.BlockSpec(memory_space=pltpu.VMEM)
    sd = lambda w, dt: jax.ShapeDtypeStruct((s, w), dt)
    return pl.pallas_call(
        body, name="merge_fwd_bwd", grid=(s // ts,),
        in_specs=[t1, t2, t2, pl.BlockSpec((ts, 2 * d), lambda i: (i, O_MG // (2 * d))), t1, row, row] + [whole] * 6,
        out_specs=[t1, t1, t1, t1, t1, t2, t2, pl.BlockSpec((ts, 2 * d), lambda i: (i, O_MG // (2 * d))),
                   pl.BlockSpec((8, d), lambda i: (0, 0))],
        out_shape=[sd(d, F32), sd(d, BF16), sd(d, BF16), sd(d, BF16), sd(d, BF16), sd(2 * d, F32), sd(2 * d, F32),
                   sd(NP, BF16), jax.ShapeDtypeStruct((8, d), F32)],
        compiler_params=_cparams("arbitrary"))(x, ym, ys, proj, target, gate, final_w, wpm, wps, wo, wpm_t, wps_t, wo_t)


def _adamw(w, g, m, v, tr):
    rows, cols = w.shape

    def body(w_ref, g_ref, m_ref, v_ref, d_ref, nm_ref, nv_ref):
        gv = g_ref[...]
        m2 = ADAM_B1 * m_ref[...] + (1.0 - ADAM_B1) * gv
        v2 = ADAM_B2 * v_ref[...] + (1.0 - ADAM_B2) * (gv * gv)
        m_hat = m2 / (1.0 - ADAM_B1 ** ADAM_STEP)
        v_hat = v2 / (1.0 - ADAM_B2 ** ADAM_STEP)
        d_ref[...] = -ADAM_LR * (m_hat / (jnp.sqrt(v_hat) + ADAM_EPS) + ADAM_WD * w_ref[...])
        nm_ref[...] = m2
        nv_ref[...] = v2

    tile = pl.BlockSpec((tr, cols), lambda i: (i, 0))
    return pl.pallas_call(
        body, name="adamw", grid=(rows // tr,), in_specs=[tile] * 4, out_specs=[tile] * 3,
        out_shape=[jax.ShapeDtypeStruct((rows, cols), F32)] * 3,
        compiler_params=_cparams("parallel"))(w, g, m, v)


def _sum_parts(own, parts, tr, dtype=F32):
    p, rows, cols = parts.shape

    def body(*refs):
        p_ref, o_ref = refs[-2], refs[-1]
        acc = p_ref[0].astype(F32) if own is None else refs[0][...].astype(F32) + p_ref[0].astype(F32)
        for i in range(1, p):
            acc = acc + p_ref[i].astype(F32)
        o_ref[...] = acc.astype(dtype)

    tile = pl.BlockSpec((tr, cols), lambda i: (i, 0))
    ins = ([] if own is None else [tile]) + [pl.BlockSpec((p, tr, cols), lambda i: (0, i, 0))]
    args = ([] if own is None else [own]) + [parts]
    return pl.pallas_call(
        body, name="sum_parts", grid=(rows // tr,), in_specs=ins, out_specs=tile,
        out_shape=jax.ShapeDtypeStruct((rows, cols), dtype), compiler_params=_cparams("parallel"))(*args)


def _position():
    return lax.axis_index("x"), lax.axis_index("y"), lax.axis_index("c")


def _flip(pos, k):
    return tuple(1 - p if (k >> s) & 1 else p for p, s in zip(pos, (2, 1, 0)))


def _allgather8(block):
    rows, cols = block.shape

    def body(x_ref, o_ref, send_sems, recv_sems, local_sem):
        pos = _position()
        me = 4 * pos[0] + 2 * pos[1] + pos[2]
        mine = pltpu.make_async_copy(x_ref, o_ref.at[me], local_sem)
        mine.start()
        copies = [pltpu.make_async_remote_copy(src_ref=x_ref, dst_ref=o_ref.at[me], send_sem=send_sems.at[k - 1],
                                               recv_sem=recv_sems.at[k - 1], device_id=_flip(pos, k), device_id_type=MESH)
                  for k in range(1, N_DEV)]
        for cp in copies:
            cp.start()
        for cp in copies:
            cp.wait()
        mine.wait()

    vmem = pl.BlockSpec(memory_space=pltpu.VMEM)
    return pl.pallas_call(
        body, name="allgather8", in_specs=[vmem], out_specs=vmem,
        out_shape=jax.ShapeDtypeStruct((N_DEV, rows, cols), block.dtype),
        scratch_shapes=[pltpu.SemaphoreType.DMA((N_DEV - 1,)), pltpu.SemaphoreType.DMA((N_DEV - 1,)),
                        pltpu.SemaphoreType.DMA],
        compiler_params=pltpu.CompilerParams(vmem_limit_bytes=VMEM_LIMIT))(block)


COPY_BYTES = 1 << 20


def _row_chunks(rows, row_bytes):
    n = max(1, min(rows // 16, -(-rows * row_bytes // COPY_BYTES)))
    while rows % (16 * n):
        n -= 1
    return [(i * (rows // n), rows // n) for i in range(n)]


def _weight_gather(shards):
    n = len(shards)
    pieces = [_row_chunks(a.shape[1], a.shape[2] * a.dtype.itemsize) for a in shards]
    plan = [(a, k, r0, nr) for a in range(n) for k in range(1, N_CHIPS) for r0, nr in pieces[a]]

    def body(*refs):
        ins, outs = refs[:n], refs[n:2 * n]
        ici_send, ici_recv, d2d_send, d2d_recv, local_sems = refs[2 * n:]
        pos = _position()
        chip, core = 2 * pos[0] + pos[1], pos[2]
        sibling = _flip(pos, 1)
        mine = [pltpu.make_async_copy(ins[a], outs[a].at[chip], local_sems.at[a]) for a in range(n)]
        for cp in mine:
            cp.start()
        sent = []
        for i, (a, k, r0, nr) in enumerate(plan):
            cp = pltpu.make_async_remote_copy(
                src_ref=ins[a].at[core, pl.ds(r0, nr)], dst_ref=outs[a].at[chip, core, pl.ds(r0, nr)],
                send_sem=ici_send.at[i], recv_sem=ici_recv.at[i], device_id=_flip(pos, 2 * k), device_id_type=MESH)
            cp.start()
            sent.append(cp)
        passed = []
        for i, (a, k, r0, nr) in enumerate(plan):
            there = _flip(pos, 2 * k)
            landed = outs[a].at[2 * there[0] + there[1], core, pl.ds(r0, nr)]
            sent[i].wait_recv()
            cp = pltpu.make_async_remote_copy(src_ref=landed, dst_ref=landed, send_sem=d2d_send.at[i],
                                              recv_sem=d2d_recv.at[i], device_id=sibling, device_id_type=MESH)
            cp.start()
            passed.append(cp)
        for cp in passed:
            cp.wait()
        for cp in sent:
            cp.wait_send()
        for cp in mine:
            cp.wait()

    hbm = pl.BlockSpec(memory_space=pl.ANY)
    sems = pltpu.SemaphoreType.DMA((len(plan),))
    return pl.pallas_call(
        body, name="weight_gather", in_specs=[hbm] * n, out_specs=[hbm] * n,
        out_shape=[jax.ShapeDtypeStruct((N_CHIPS,) + a.shape, a.dtype) for a in shards],
        scratch_shapes=[sems, sems, sems, sems, pltpu.SemaphoreType.DMA((n,))],
        compiler_params=pltpu.CompilerParams(has_side_effects=True))(*shards)


def _exchange(name, arrays, out_shapes, plan, n_remote, n_local):
    n, m = len(arrays), len(out_shapes)

    def body(*refs):
        send_sems, recv_sems, local_sems = refs[n + m:]
        remote, local = plan(_position(), refs[:n], refs[n:n + m])
        assert (len(remote), len(local)) == (n_remote, n_local)
        copies = [pltpu.make_async_copy(src, dst, local_sems.at[i]) for i, (src, dst) in enumerate(local)]
        copies += [pltpu.make_async_remote_copy(src_ref=src, dst_ref=dst, send_sem=send_sems.at[i], recv_sem=recv_sems.at[i],
                                                device_id=dev, device_id_type=MESH)
                   for i, (src, dst, dev) in enumerate(remote)]
        for cp in copies:
            cp.start()
        for cp in copies:
            cp.wait()

    hbm = pl.BlockSpec(memory_space=pl.ANY)
    return pl.pallas_call(
        body, name=name, in_specs=[hbm] * n, out_specs=[hbm] * m, out_shape=out_shapes,
        scratch_shapes=[pltpu.SemaphoreType.DMA((n_remote,)), pltpu.SemaphoreType.DMA((n_remote,)),
                        pltpu.SemaphoreType.DMA((max(n_local, 1),))],
        compiler_params=pltpu.CompilerParams(has_side_effects=True))(*arrays)


def _pair_send(slabs):
    n = len(slabs)
    pieces = [_row_chunks(g.shape[2], g.shape[3] * g.dtype.itemsize) for g in slabs]

    def plan(pos, ins, outs):
        return [(ins[a].at[j, 1 - pos[2], pl.ds(r0, nr)], outs[a].at[j, pl.ds(r0, nr)], _flip(pos, 1))
                for a in range(n) for j in range(N_CHIPS) for r0, nr in pieces[a]], []

    return _exchange("pair_send", slabs, [jax.ShapeDtypeStruct((N_CHIPS,) + g.shape[2:], g.dtype) for g in slabs], plan,
                     N_CHIPS * sum(len(p) for p in pieces), 0)


def _chip_scatter(sums):
    n = len(sums)
    pieces = [_row_chunks(g.shape[1], g.shape[2] * g.dtype.itemsize) for g in sums]

    def plan(pos, ins, outs):
        remote = []
        for a in range(n):
            for k in range(1, N_CHIPS):
                to = _flip(pos, 2 * k)
                remote += [(ins[a].at[2 * to[0] + to[1], pl.ds(r0, nr)], outs[a].at[k - 1, pl.ds(r0, nr)], to)
                           for r0, nr in pieces[a]]
        return remote, []

    return _exchange("chip_scatter", sums, [jax.ShapeDtypeStruct((N_CHIPS - 1,) + g.shape[1:], g.dtype) for g in sums],
                     plan, (N_CHIPS - 1) * sum(len(p) for p in pieces), 0)


def _pair_exchange(halves):
    n = len(halves)
    pieces = [_row_chunks(h.shape[0], h.shape[1] * h.dtype.itemsize) for h in halves]

    def plan(pos, ins, outs):
        remote = [(ins[a].at[pl.ds(r0, nr)], outs[a].at[pos[2], pl.ds(r0, nr)], _flip(pos, 1))
                  for a in range(n) for r0, nr in pieces[a]]
        return remote, [(ins[a], outs[a].at[pos[2]]) for a in range(n)]

    return _exchange("pair_exchange", halves, [jax.ShapeDtypeStruct((2,) + h.shape, h.dtype) for h in halves], plan,
                     sum(len(p) for p in pieces), n)


def _pack(arrays):
    flat = jnp.concatenate([a.reshape(-1).astype(F32) for a in arrays])
    size = -(-flat.shape[0] // (8 * LANE)) * (8 * LANE)
    return jnp.pad(flat, (0, size - flat.shape[0])).reshape(size // LANE, LANE)


def _unpack(buf, shapes):
    flat = buf.reshape(-1)
    out, off = [], 0
    for shp in shapes:
        n = math.prod(shp)
        out.append(flat[off:off + n].reshape(shp))
        off += n
    return out


def _unpack_rows(bufs, shapes):
    flat = bufs.reshape(bufs.shape[0], -1)
    out, off = [], 0
    for shp in shapes:
        n = math.prod(shp)
        out.append(flat[:, off:off + n].reshape((bufs.shape[0],) + shp))
        off += n
    return out


def _taps8(w):
    return jnp.pad(w, ((0, 8 - CONV_K), (0, 0)))


def _local_step(xs, tgt, scale, shift, gate, norm_w, w_in_p, b_in_p, ml_conv_w, ml_conv_b, ml_norm_w, ssm_conv_w,
                ssm_conv_b, ssm_a_log, ssm_d, ssm_norm_w, wpm, wps, wo, final_w):
    s = xs.shape[0]
    ts = min(512, s)
    tm = min(1024, s)
    u = _prenorm_fwd(xs, norm_w, scale, shift, ts)
    proj = _matmul_bias(u, w_in_p, b_in_p, tm, 512)
    mlw8, ssw8 = _taps8(ml_conv_w), _taps8(ssm_conv_w)
    qk = _conv_fwd(proj, O_QK, 2048, mlw8, ml_conv_b, ts)
    xbc = _conv_fwd(proj, O_XBC, 3072, ssw8, ssm_conv_b, ts)
    gt = proj[:, O_IF:O_IF + LANE].T
    dtt = proj[:, O_DT:O_DT + LANE].T
    hm, cst, nm = _mlstm_fwd(qk, proj, gt)
    alog_row = jnp.pad(ssm_a_log, ((0, 0), (0, LANE - SSM_HEADS)))
    alog_col = alog_row.reshape(LANE, 1)
    dskip_x = jnp.repeat(ssm_d[0], SSM_HEADDIM)[None]
    expand = _head_expand()
    yssd, sst = _ssd_fwd(xbc, proj, dtt, alog_row, alog_col, dskip_x, expand)
    tp = min(128, s)
    ym, ys = _post_fwd(hm, yssd, proj, ml_norm_w, ssm_norm_w, tp)
    dxres, merged, dmo, dpm, dps, dym, dys, dproj, acc_m = _merge(xs, ym, ys, proj, tgt, gate, final_w, wpm, wps, wo, tp)
    dh, dyssd, dproj, acc_p = _post_bwd(dym, dys, hm, yssd, proj, ml_norm_w, ssm_norm_w, dproj, tp)
    dqk, dproj, dif = _mlstm_bwd(qk, proj, gt, hm, dh, cst, nm, dproj)
    dxbc, ddt, accd, acca = _ssd_bwd(xbc, proj, dtt, alog_row, alog_col, dskip_x, expand, expand.T, dyssd, sst)
    dproj, acc_cq = _conv_bwd(proj, O_QK, 2048, mlw8, ml_conv_b, dqk, dproj, ts)
    dproj, acc_cx = _conv_bwd(proj, O_XBC, 3072, ssw8, ssm_conv_b, dxbc, dproj, ts)
    dproj = dproj.at[:, O_IF:O_IF + SMALL_W].set(dif).at[:, O_DT:O_DT + SMALL_W].set(ddt)
    gw_in_p, gb_in_p = _matmul_tn(u, dproj, tm, 512, with_colsum=True)
    du = _matmul_nt(dproj, w_in_p, tm, 512)
    grad_x, acc_n = _prenorm_bwd(du, xs, dxres, norm_w, scale, ts)
    g_wpm = _matmul_tn(ym, dpm, tm, 512)
    g_wps = _matmul_tn(ys, dps, tm, 512)
    g_wo = _matmul_tn(merged, dmo, tm, 512)
    a_coef = -jnp.exp(ssm_a_log[0])
    small = dict(
        mod=jnp.concatenate([acc_n[2], acc_n[1], acc_m[2]]), norm_w=acc_n[0], b_in=_unpad_cols(gb_in_p[0]),
        ml_conv_w=acc_cq[0:CONV_K], ml_conv_b=acc_cq[CONV_K], ml_norm_w=acc_p[0], ssm_conv_w=acc_cx[0:CONV_K],
        ssm_conv_b=acc_cx[CONV_K], ssm_a_log=acca[0, :SSM_HEADS] * a_coef,
        ssm_d=accd[0].reshape(SSM_HEADS, SSM_HEADDIM).sum(axis=1), ssm_norm_w=acc_p[1], final_w=acc_m[1], loss=acc_m[0, 0:1])
    return grad_x, small, _unpad_cols(gw_in_p), g_wpm, g_wps, g_wo


WEIGHTS = ("norm_w", "ada_w", "ada_b", "w_in", "b_in", "ml_conv_w", "ml_conv_b", "ml_norm_w", "ssm_conv_w", "ssm_conv_b",
           "ssm_a_log", "ssm_d", "ssm_norm_w", "w_proj_m", "w_proj_s", "w_out", "final_w")
LARGE = ("ada_w", "w_in", "w_proj_m", "w_proj_s", "w_out")
SMALL_SUMS = (("mod", (3 * D_MODEL,)), ("norm_w", (D_MODEL,)), ("b_in", (IN_WIDTH,)), ("ml_conv_w", (CONV_K, 2048)),
              ("ml_conv_b", (2048,)), ("ml_norm_w", (2048,)), ("ssm_conv_w", (CONV_K, 3072)), ("ssm_conv_b", (3072,)),
              ("ssm_a_log", (SSM_HEADS,)), ("ssm_d", (SSM_HEADS,)), ("ssm_norm_w", (2048,)), ("final_w", (D_MODEL,)),
              ("loss", (1,)))


def kernel(x, c, norm_w, ada_w, ada_b, w_in, b_in, ml_conv_w, ml_conv_b, ml_norm_w, ssm_conv_w, ssm_conv_b, ssm_a_log, ssm_d, ssm_norm_w, w_proj_m, w_proj_s, w_out, final_w, loss_target, m_norm_w, m_ada_w, m_ada_b, m_w_in, m_b_in, m_ml_conv_w, m_ml_conv_b, m_ml_norm_w, m_ssm_conv_w, m_ssm_conv_b, m_ssm_a_log, m_ssm_d, m_ssm_norm_w, m_w_proj_m, m_w_proj_s, m_w_out, m_final_w, v_norm_w, v_ada_w, v_ada_b, v_w_in, v_b_in, v_ml_conv_w, v_ml_conv_b, v_ml_norm_w, v_ssm_conv_w, v_ssm_conv_b, v_ssm_a_log, v_ssm_d, v_ssm_norm_w, v_w_proj_m, v_w_proj_s, v_w_out, v_final_w):
    w = dict(norm_w=norm_w, ada_w=ada_w, ada_b=ada_b, w_in=w_in, b_in=b_in, ml_conv_w=ml_conv_w, ml_conv_b=ml_conv_b,
             ml_norm_w=ml_norm_w, ssm_conv_w=ssm_conv_w, ssm_conv_b=ssm_conv_b, ssm_a_log=ssm_a_log, ssm_d=ssm_d,
             ssm_norm_w=ssm_norm_w, w_proj_m=w_proj_m, w_proj_s=w_proj_s, w_out=w_out, final_w=final_w)
    m = dict(zip(WEIGHTS, (m_norm_w, m_ada_w, m_ada_b, m_w_in, m_b_in, m_ml_conv_w, m_ml_conv_b, m_ml_norm_w, m_ssm_conv_w,
                           m_ssm_conv_b, m_ssm_a_log, m_ssm_d, m_ssm_norm_w, m_w_proj_m, m_w_proj_s, m_w_out, m_final_w)))
    v = dict(zip(WEIGHTS, (v_norm_w, v_ada_w, v_ada_b, v_w_in, v_b_in, v_ml_conv_w, v_ml_conv_b, v_ml_norm_w, v_ssm_conv_w,
                           v_ssm_conv_b, v_ssm_a_log, v_ssm_d, v_ssm_norm_w, v_w_proj_m, v_w_proj_s, v_w_out, v_final_w)))
    pos = _position()
    chip = 2 * pos[0] + pos[1]
    dev = 2 * chip + pos[2]
    mlw_cols, ssw_cols, ada_cols = ml_conv_w.shape[2], ssm_conv_w.shape[2], ada_w.shape[2]

    g0 = _allgather8(_pack([c, ml_conv_w, ssm_conv_w]))
    c_all, mlw_all, ssw_all = _unpack_rows(g0, [(D_MODEL,), (CONV_K, mlw_cols), (CONV_K, ssw_cols)])
    ml_conv_full = mlw_all[0::2].transpose(1, 0, 2).reshape(CONV_K, N_CHIPS * mlw_cols)
    ssm_conv_full = ssw_all[0::2].transpose(1, 0, 2).reshape(CONV_K, N_CHIPS * ssw_cols)

    ada_b_mine = lax.dynamic_slice_in_dim(ada_b, chip * ada_cols, ada_cols, axis=1)
    g1 = _allgather8(_ada_fwd(c_all, ada_w[0], ada_b_mine))
    mod = lax.dynamic_index_in_dim(g1[0::2], dev, axis=1, keepdims=False).reshape(1, 3 * D_MODEL)
    shift, scale, gate = mod[:, :D_MODEL], mod[:, D_MODEL:2 * D_MODEL], mod[:, 2 * D_MODEL:]

    halved = lambda a: _bf(a[0]).reshape(2, a.shape[1] // 2, a.shape[2])
    gw = [a.reshape(N_CHIPS, -1, a.shape[-1]) for a in _weight_gather([halved(w_in), halved(w_proj_m), halved(w_proj_s), halved(w_out)])]
    w_in_p = _pad_cols(gw[0].transpose(1, 0, 2).reshape(D_MODEL, IN_WIDTH))
    wpm, wps, wo = (a.reshape(-1, D_MODEL) for a in gw[1:])

    grad_x, small, g_w_in, g_wpm, g_wps, g_wo = _local_step(
        x[0], loss_target[0], scale, shift, gate, norm_w, w_in_p, _pad_cols(b_in), ml_conv_full, ml_conv_b, ml_norm_w,
        ssm_conv_full, ssm_conv_b, ssm_a_log, ssm_d, ssm_norm_w, wpm, wps, wo, final_w[None])

    g2 = _allgather8(_pack([small[name] for name, _ in SMALL_SUMS]))
    total = dict(zip([name for name, _ in SMALL_SUMS], _unpack(_sum_parts(None, g2, g2.shape[1]), [s for _, s in SMALL_SUMS])))
    dmod_all = g2[:, :3 * D_MODEL // LANE].reshape(N_DEV, 3 * D_MODEL)
    grads = dict(total)
    grads["ada_b"] = total["mod"]
    grads["ml_conv_w"] = lax.dynamic_slice_in_dim(total["ml_conv_w"], chip * mlw_cols, mlw_cols, axis=1)
    grads["ssm_conv_w"] = lax.dynamic_slice_in_dim(total["ssm_conv_w"], chip * ssw_cols, ssw_cols, axis=1)
    grads["ada_w"] = _ada_bwd(c_all, lax.dynamic_slice_in_dim(dmod_all, chip * ada_cols, ada_cols, axis=1))

    split = lambda g, rows: _bf(g).reshape(N_CHIPS, 2, rows // (2 * N_CHIPS), g.shape[-1])
    slabs = [split(g_w_in.reshape(D_MODEL, N_CHIPS, -1).transpose(1, 0, 2), N_CHIPS * D_MODEL),
             split(g_wpm, g_wpm.shape[0]), split(g_wps, g_wps.shape[0]), split(g_wo, g_wo.shape[0])]
    pair_sums = []
    for slab, rec in zip(slabs, _pair_send(slabs)):
        kept = lax.dynamic_index_in_dim(slab, pos[2], 1, keepdims=False)
        rows = kept.shape[0] * kept.shape[1]
        both = _sum_parts(kept.reshape(rows, -1), rec.reshape(1, rows, -1), 32, BF16)
        pair_sums.append(both.reshape(kept.shape))
    halves = []
    for both, rec in zip(pair_sums, _chip_scatter(pair_sums)):
        halves.append(_sum_parts(lax.dynamic_index_in_dim(both, chip, 0, keepdims=False), rec, 32))
    for name, full in zip(("w_in", "w_proj_m", "w_proj_s", "w_out"), _pair_exchange(halves)):
        grads[name] = full.reshape(-1, full.shape[-1])

    delta, new_m, new_v = {}, {}, {}
    for name in LARGE:
        d2, m2, v2 = _adamw(w[name][0], grads[name], m[name][0], v[name][0], 64)
        delta[name], new_m[name], new_v[name] = d2[None], m2[None], v2[None]
    rest = [name for name in WEIGHTS if name not in LARGE]
    packed = [_pack([t[name] for name in rest]) for t in (w, grads, m, v)]
    for out, buf in zip((delta, new_m, new_v), _adamw(*packed, packed[0].shape[0])):
        out.update(zip(rest, _unpack(buf, [w[name].shape for name in rest])))
    loss = total["loss"][0]
    return (loss, grad_x[None], *[grads[name].reshape(w[name].shape) for name in WEIGHTS], *[delta[name] for name in WEIGHTS],
            *[new_m[name] for name in WEIGHTS], *[new_v[name] for name in WEIGHTS])
```

```python
import functools
import math

import jax
import jax.numpy as jnp
from jax import lax
from jax.experimental import pallas as pl
from jax.experimental.pallas import tpu as pltpu

F32 = jnp.float32
BF16 = jnp.bfloat16
HI = lax.Precision.HIGHEST
MESH = pl.DeviceIdType.MESH

D_MODEL = 1024
EPS = 1e-6
CONV_K = 4
ML_HEADS = 8
ML_DQK = 128
ML_DV = 256
SSM_HEADS = 32
SSM_HEADDIM = 64
SSM_GROUPS = 4
SSM_STATE = 128
IN_WIDTH = 15408
N_CHIPS = 4
N_DEV = 8
ADAM_LR, ADAM_B1, ADAM_B2, ADAM_EPS, ADAM_WD, ADAM_STEP = 0.001, 0.9, 0.999, 1e-08, 0.01, 10

O_O, O_ZM, O_ZS, O_MG, O_QK, O_V, O_XBC, O_IF, O_DT = 0, 2048, 4096, 6144, 8192, 10240, 12288, 15360, 15616
SMALL_W = 256
NP = 15872
LANE = 128
CHUNK = 128
NEG = -1e30
VMEM_LIMIT = 48 * 1024 * 1024


def _cparams(*sem):
    return pltpu.CompilerParams(dimension_semantics=sem, vmem_limit_bytes=VMEM_LIMIT)


def _pad_cols(w):
    z = lambda n: jnp.zeros(w.shape[:-1] + (n,), w.dtype)
    return jnp.concatenate([w[..., 4096:8192], w[..., 11280:13328], w[..., 13360:15408], w[..., :4096], w[..., 8208:11280],
                            w[..., 8192:8208], z(SMALL_W - 16), w[..., 13328:13360], z(SMALL_W - 32)], axis=-1)


def _unpad_cols(g):
    return jnp.concatenate([g[..., O_QK:O_QK + 4096], g[..., O_O:O_O + 4096], g[..., O_IF:O_IF + 16],
                            g[..., O_XBC:O_XBC + 3072], g[..., O_ZS:O_ZS + 2048], g[..., O_DT:O_DT + 32],
                            g[..., O_MG:O_MG + 2048]], axis=-1)


PADDED_SEGMENTS = ((4096, 8192, 0), (11280, 13328, 0), (13360, 15408, 0), (0, 4096, 0), (8208, 11280, 0),
                   (8192, 8208, SMALL_W - 16), (13328, 13360, SMALL_W - 32))
SHARD_W = IN_WIDTH // N_CHIPS


def _shards_to_padded(shards):
    parts = []
    for first, last, pad in PADDED_SEGMENTS:
        for j in range(N_CHIPS):
            lo, hi = max(first, j * SHARD_W), min(last, (j + 1) * SHARD_W)
            if lo < hi:
                parts.append(shards[j][:, lo - j * SHARD_W:hi - j * SHARD_W])
        if pad:
            parts.append(jnp.zeros((shards.shape[1], pad), shards.dtype))
    return jnp.concatenate(parts, axis=1)


def _padded_to_shards(g):
    offsets, off = {}, 0
    for first, last, pad in PADDED_SEGMENTS:
        offsets[first] = off
        off += last - first + pad
    shards = []
    for j in range(N_CHIPS):
        parts = []
        for first, last, _ in sorted(PADDED_SEGMENTS):
            lo, hi = max(first, j * SHARD_W), min(last, (j + 1) * SHARD_W)
            if lo < hi:
                parts.append(g[:, offsets[first] + lo - first:offsets[first] + hi - first])
        shards.append(jnp.concatenate(parts, axis=1))
    return jnp.stack(shards)


def _sigmoid(x):
    return 1.0 / (1.0 + jnp.exp(-x))


def _silu(x):
    return x * _sigmoid(x)


def _dsilu(x):
    s = _sigmoid(x)
    return s + x * s * (1.0 - s)


def _softplus(x):
    return jnp.maximum(x, 0.0) + jnp.log(1.0 + jnp.exp(-jnp.abs(x)))


def _logsigmoid(x):
    return jnp.minimum(x, 0.0) - jnp.log(1.0 + jnp.exp(-jnp.abs(x)))


def _dot(a, b, dims, precision=None):
    return lax.dot_general(a, b, (dims, ((), ())), preferred_element_type=F32, precision=precision)


def _nn(a, b, precision=None):
    return _dot(a, b, ((1,), (0,)), precision)


def _nt(a, b, precision=None):
    return _dot(a, b, ((1,), (1,)), precision)


def _tn(a, b, precision=None):
    return _dot(a, b, ((0,), (0,)), precision)


def _bf(x):
    return x.astype(BF16)


def _split(x, terms):
    parts = []
    for _ in range(terms):
        part = _bf(x)
        parts.append(part)
        x = x - part.astype(F32)
    return parts


def _pick_right(x, pick, terms):
    pick = _bf(pick)
    out = None
    for part in _split(x, terms):
        out = _nn(part, pick) if out is None else out + _nn(part, pick)
    return out


def _pick_left(pick, x, terms):
    pick = _bf(pick)
    out = None
    for part in _split(x, terms):
        out = _nn(pick, part) if out is None else out + _nn(pick, part)
    return out


def _lane_col(x, lane):
    idx = lax.broadcasted_iota(jnp.int32, x.shape, 1)
    return jnp.sum(jnp.where(idx == lane, x, 0.0), axis=1, keepdims=True)


def _tri(n, upper):
    r = lax.broadcasted_iota(jnp.int32, (n, n), 0)
    c = lax.broadcasted_iota(jnp.int32, (n, n), 1)
    return jnp.where((r <= c) if upper else (r >= c), 1.0, 0.0).astype(F32)


def _eye(n):
    return jnp.where(lax.broadcasted_iota(jnp.int32, (n, n), 0) == lax.broadcasted_iota(jnp.int32, (n, n), 1), 1.0, 0.0)


def _sum_all(x):
    return jnp.sum(jnp.sum(x, axis=1, keepdims=True), axis=0, keepdims=True)


def _crossing(p):
    L = p.shape[0]
    hi = _bf(p)
    lo = _bf(p - hi.astype(F32))
    upper = _bf(_tri(L, True))
    below = _nn(upper, hi) + _nn(upper, lo)
    strict = lax.broadcasted_iota(jnp.int32, (L, L), 0) > lax.broadcasted_iota(jnp.int32, (L, L), 1)
    return jnp.sum(jnp.where(strict, below, 0.0), axis=1, keepdims=True)


def _matmul_bias(a, w, bias, tm, tn):
    m, k = a.shape
    n = w.shape[1]

    def body(a_ref, w_ref, b_ref, o_ref):
        o_ref[...] = _nn(a_ref[...], w_ref[...]) + b_ref[...]

    return pl.pallas_call(
        body, name="matmul_bias", grid=(m // tm, n // tn),
        in_specs=[pl.BlockSpec((tm, k), lambda i, j: (i, 0)), pl.BlockSpec((k, tn), lambda i, j: (0, j)),
                  pl.BlockSpec((1, tn), lambda i, j: (0, j))],
        out_specs=pl.BlockSpec((tm, tn), lambda i, j: (i, j)),
        out_shape=jax.ShapeDtypeStruct((m, n), F32),
        compiler_params=_cparams("parallel", "arbitrary"))(a, w, bias)


def _matmul_nt(a, w, tm, tk):
    m, n = a.shape
    k = w.shape[0]

    def body(a_ref, w_ref, o_ref):
        @pl.when(pl.program_id(1) == 0)
        def _():
            o_ref[...] = jnp.zeros_like(o_ref)
        o_ref[...] += _nt(a_ref[...], w_ref[...])

    return pl.pallas_call(
        body, name="matmul_nt", grid=(m // tm, n // tk),
        in_specs=[pl.BlockSpec((tm, tk), lambda i, j: (i, j)), pl.BlockSpec((k, tk), lambda i, j: (0, j))],
        out_specs=pl.BlockSpec((tm, k), lambda i, j: (i, 0)),
        out_shape=jax.ShapeDtypeStruct((m, k), F32),
        compiler_params=_cparams("parallel", "arbitrary"))(a, w)


def _matmul_tn(a, b, tm, tn, with_colsum=False):
    m, k = a.shape
    n = b.shape[1]

    def body(a_ref, b_ref, o_ref, *rest):
        first = pl.program_id(1) == 0

        @pl.when(first)
        def _():
            o_ref[...] = jnp.zeros_like(o_ref)
        o_ref[...] += _tn(a_ref[...], b_ref[...])
        if with_colsum:
            s_ref = rest[0]

            @pl.when(first)
            def _():
                s_ref[...] = jnp.zeros_like(s_ref)
            s_ref[...] += jnp.sum(b_ref[...].astype(F32), axis=0, keepdims=True)

    out_specs = [pl.BlockSpec((k, tn), lambda j, i: (0, j))]
    out_shape = [jax.ShapeDtypeStruct((k, n), F32)]
    if with_colsum:
        out_specs.append(pl.BlockSpec((1, tn), lambda j, i: (0, j)))
        out_shape.append(jax.ShapeDtypeStruct((1, n), F32))
    out = pl.pallas_call(
        body, name="matmul_tn", grid=(n // tn, m // tm),
        in_specs=[pl.BlockSpec((tm, k), lambda j, i: (i, 0)), pl.BlockSpec((tm, tn), lambda j, i: (i, j))],
        out_specs=out_specs, out_shape=out_shape,
        compiler_params=_cparams("parallel", "arbitrary"))(a, b)
    return out if with_colsum else out[0]


def _ada_fwd(c_all, ada_w, ada_b):
    def body(c_ref, w_ref, b_ref, o_ref):
        o_ref[...] = _nn(_bf(_silu(c_ref[...])), _bf(w_ref[...])) + b_ref[...]

    return pl.pallas_call(body, name="ada_fwd", out_shape=jax.ShapeDtypeStruct((c_all.shape[0], ada_w.shape[1]), F32),
                          compiler_params=_cparams())(c_all, ada_w, ada_b)


def _ada_bwd(c_all, dmod):
    def body(c_ref, d_ref, o_ref):
        o_ref[...] = _tn(_bf(_silu(c_ref[...])), _bf(d_ref[...]))

    return pl.pallas_call(body, name="ada_bwd", out_shape=jax.ShapeDtypeStruct((c_all.shape[1], dmod.shape[1]), F32),
                          compiler_params=_cparams())(c_all, dmod)


def _prenorm_fwd(x, norm_w, scale, shift, ts):
    s, d = x.shape

    def body(x_ref, nw_ref, sc_ref, sh_ref, u_ref):
        xv = x_ref[...]
        r = lax.rsqrt(jnp.mean(xv * xv, axis=1, keepdims=True) + EPS)
        u_ref[...] = _bf(xv * r * nw_ref[...] * (1.0 + sc_ref[...]) + sh_ref[...])

    row = pl.BlockSpec((1, d), lambda i: (0, 0))
    return pl.pallas_call(
        body, name="prenorm_fwd", grid=(s // ts,),
        in_specs=[pl.BlockSpec((ts, d), lambda i: (i, 0)), row, row, row],
        out_specs=pl.BlockSpec((ts, d), lambda i: (i, 0)), out_shape=jax.ShapeDtypeStruct((s, d), BF16),
        compiler_params=_cparams("parallel"))(x, norm_w, scale, shift)


def _prenorm_bwd(du, x, dxres, norm_w, scale, ts):
    s, d = x.shape

    def body(du_ref, x_ref, dr_ref, nw_ref, sc_ref, gx_ref, acc_ref):
        @pl.when(pl.program_id(0) == 0)
        def _():
            acc_ref[...] = jnp.zeros_like(acc_ref)
        xv, duv = x_ref[...], du_ref[...]
        r = lax.rsqrt(jnp.mean(xv * xv, axis=1, keepdims=True) + EPS)
        xn = xv * r
        nw, sc1 = nw_ref[...], 1.0 + sc_ref[...]
        dxn = duv * (nw * sc1)
        gx_ref[...] = r * (dxn - xn * jnp.mean(dxn * xn, axis=1, keepdims=True)) + dr_ref[...]
        t = duv * xn
        acc_ref[0:1, :] += jnp.sum(t, axis=0, keepdims=True) * sc1
        acc_ref[1:2, :] += jnp.sum(t, axis=0, keepdims=True) * nw
        acc_ref[2:3, :] += jnp.sum(duv, axis=0, keepdims=True)

    tile = pl.BlockSpec((ts, d), lambda i: (i, 0))
    row = pl.BlockSpec((1, d), lambda i: (0, 0))
    return pl.pallas_call(
        body, name="prenorm_bwd", grid=(s // ts,),
        in_specs=[tile, tile, tile, row, row],
        out_specs=[tile, pl.BlockSpec((8, d), lambda i: (0, 0))],
        out_shape=[jax.ShapeDtypeStruct((s, d), F32), jax.ShapeDtypeStruct((8, d), F32)],
        compiler_params=_cparams("arbitrary"))(du, x, dxres, norm_w, scale)


CONV_CB = 512


def _conv_taps(buf_ref, ts):
    return [buf_ref[pl.ds(8 - (CONV_K - 1) + j, ts), :] for j in range(CONV_K)]


def _conv_fwd(proj, col0, width, w8, b, ts):
    s = proj.shape[0]
    cb = CONV_CB
    nt = s // ts

    def body(x_ref, w_ref, b_ref, o_ref, buf_ref):
        @pl.when(pl.program_id(1) == 0)
        def _():
            buf_ref[0:8, :] = jnp.zeros((8, cb), F32)
        buf_ref[pl.ds(8, ts), :] = x_ref[...]
        acc = b_ref[...] + jnp.zeros((ts, cb), F32)
        for j, tap in enumerate(_conv_taps(buf_ref, ts)):
            acc = acc + tap * w_ref[j:j + 1, :]
        o_ref[...] = _silu(acc)
        buf_ref[0:8, :] = x_ref[pl.ds(ts - 8, 8), :]

    c0 = col0 // cb
    return pl.pallas_call(
        body, name="conv_fwd", grid=(width // cb, nt),
        in_specs=[pl.BlockSpec((ts, cb), lambda c, i: (i, c0 + c)), pl.BlockSpec((8, cb), lambda c, i: (0, c)),
                  pl.BlockSpec((1, cb), lambda c, i: (0, c))],
        out_specs=pl.BlockSpec((ts, cb), lambda c, i: (i, c)),
        out_shape=jax.ShapeDtypeStruct((s, width), F32),
        scratch_shapes=[pltpu.VMEM((ts + 8, cb), F32)],
        compiler_params=_cparams("parallel", "arbitrary"))(proj, w8, b)


def _conv_bwd(proj, col0, width, w8, b, dpost, dproj, ts):
    s = proj.shape[0]
    cb = CONV_CB
    nt = s // ts
    c0 = col0 // cb

    def body(x_ref, xh_ref, dp_ref, w_ref, b_ref, _, dx_ref, acc_ref, buf_ref, dbuf_ref):
        step = pl.program_id(1)
        tile = nt - 1 - step

        @pl.when(step == 0)
        def _():
            acc_ref[...] = jnp.zeros_like(acc_ref)
            dbuf_ref[pl.ds(ts, 8), :] = jnp.zeros((8, cb), F32)
        buf_ref[0:8, :] = jnp.where(tile == 0, 0.0, xh_ref[...])
        buf_ref[pl.ds(8, ts), :] = x_ref[...]
        taps = _conv_taps(buf_ref, ts)
        acc = b_ref[...] + jnp.zeros((ts, cb), F32)
        for j in range(CONV_K):
            acc = acc + taps[j] * w_ref[j:j + 1, :]
        dconv = dp_ref[...] * _dsilu(acc)
        acc_ref[4:5, :] += jnp.sum(dconv, axis=0, keepdims=True)
        for j in range(CONV_K):
            acc_ref[j:j + 1, :] += jnp.sum(taps[j] * dconv, axis=0, keepdims=True)
        dbuf_ref[pl.ds(0, ts), :] = dconv
        dx = jnp.zeros((ts, cb), F32)
        for j in range(CONV_K):
            dx = dx + dbuf_ref[pl.ds(CONV_K - 1 - j, ts), :] * w_ref[j:j + 1, :]
        dx_ref[...] = _bf(dx)
        dbuf_ref[pl.ds(ts, 8), :] = dconv[0:8, :]

    rows8 = ts // 8
    return pl.pallas_call(
        body, name="conv_bwd", grid=(width // cb, nt),
        in_specs=[pl.BlockSpec((ts, cb), lambda c, i: (nt - 1 - i, c0 + c)),
                  pl.BlockSpec((8, cb), lambda c, i: (jnp.maximum((nt - 1 - i) * rows8 - 1, 0), c0 + c)),
                  pl.BlockSpec((ts, cb), lambda c, i: (nt - 1 - i, c)),
                  pl.BlockSpec((8, cb), lambda c, i: (0, c)), pl.BlockSpec((1, cb), lambda c, i: (0, c)),
                  pl.BlockSpec(memory_space=pl.ANY)],
        out_specs=[pl.BlockSpec((ts, cb), lambda c, i: (nt - 1 - i, c0 + c)), pl.BlockSpec((8, cb), lambda c, i: (0, c))],
        out_shape=[jax.ShapeDtypeStruct(dproj.shape, dproj.dtype), jax.ShapeDtypeStruct((8, width), F32)],
        input_output_aliases={5: 0},
        scratch_shapes=[pltpu.VMEM((ts + 8, cb), F32), pltpu.VMEM((ts + 8, cb), F32)],
        compiler_params=_cparams("parallel", "arbitrary"))(proj, proj, dpost, w8, b, dproj)


def _mlstm_gates(gif_ref, gt_ref, a_scr, at_scr):
    L = gif_ref.shape[0]
    fb = _logsigmoid(gif_ref[...])
    a_scr[...] = _pick_left(_tri(L, False), fb, 3)
    at_scr[...] = _pick_right(_logsigmoid(gt_ref[...]), _tri(L, True), 3)
    return jnp.sum(fb, axis=0, keepdims=True)


def _mlstm_head(h, qk_ref, v_ref, gif, gt_ref, a, at_scr, a_last_row, c_mat, n_row, m_prev):
    L = gif.shape[0]
    q = qk_ref[:, h * ML_DQK:(h + 1) * ML_DQK] * (ML_DQK ** -0.5)
    k = qk_ref[:, (ML_HEADS + h) * ML_DQK:(ML_HEADS + h + 1) * ML_DQK]
    v = v_ref[:, h * ML_DV:(h + 1) * ML_DV]
    i_col, a_col = _lane_col(gif, h), _lane_col(a, ML_HEADS + h)
    i_row, a_row = gt_ref[h:h + 1, :], at_scr[ML_HEADS + h:ML_HEADS + h + 1, :]
    causal = lax.broadcasted_iota(jnp.int32, (L, L), 0) >= lax.broadcasted_iota(jnp.int32, (L, L), 1)
    dmat = jnp.where(causal, a_col - a_row + i_row, NEG)
    inter = a_col + m_prev
    m_t = jnp.maximum(inter, jnp.max(dmat, axis=1, keepdims=True))
    w_intra = jnp.exp(dmat - m_t)
    w_inter = jnp.exp(inter - m_t)
    sc = _nt(_bf(q), _bf(k)) * w_intra
    den = jnp.sum(sc, axis=1, keepdims=True) + w_inter * jnp.sum(q * n_row, axis=1, keepdims=True)
    floor = jnp.exp(-m_t)
    a_last = _lane_col(a_last_row, ML_HEADS + h)
    g = a_last - a_col + i_col
    m_new = jnp.maximum(a_last + m_prev, jnp.max(g, axis=0, keepdims=True))
    wk = jnp.exp(g - m_new)
    decay = jnp.exp(a_last + m_prev - m_new)
    return dict(q=q, k=k, v=v, w_intra=w_intra, w_inter=w_inter, sc=sc, den=den, floor=floor, m_new=m_new, wk=wk,
                decay=decay)


def _state_tile(n_row, m11):
    r = lax.broadcasted_iota(jnp.int32, (8, LANE), 0)
    return jnp.where(r == 0, n_row, jnp.where(r == 1, m11, 0.0))


def _mlstm_fwd(qk, proj, gt):
    s = qk.shape[0]
    L = CHUNK
    nc = s // L

    def body(qk_ref, v_ref, gif_ref, gt_ref, h_ref, cst_ref, nm_ref, c_scr, nm_scr, a_scr, at_scr):
        @pl.when(pl.program_id(0) == 0)
        def _():
            c_scr[...] = jnp.zeros_like(c_scr)
            nm_scr[...] = jnp.zeros_like(nm_scr)
        a_last_row = _mlstm_gates(gif_ref, gt_ref, a_scr, at_scr)
        gif, a = gif_ref[...], a_scr[...]
        for h in range(ML_HEADS):
            c_mat, n_row = c_scr[h], nm_scr[h, 0:1, :]
            m_prev = jnp.max(nm_scr[h, 1:2, :], axis=1, keepdims=True)
            cst_ref[0, h] = c_mat
            nm_ref[0, h] = nm_scr[h]
            t = _mlstm_head(h, qk_ref, v_ref, gif, gt_ref, a, at_scr, a_last_row, c_mat, n_row, m_prev)
            num = _nn(_bf(t["sc"]), _bf(t["v"])) + t["w_inter"] * _nn(_bf(t["q"]), _bf(c_mat))
            h_ref[:, h * ML_DV:(h + 1) * ML_DV] = num / jnp.maximum(jnp.abs(t["den"]), t["floor"])
            kw = t["k"] * t["wk"]
            c_scr[h] = t["decay"] * c_mat + _tn(_bf(kw), _bf(t["v"]))
            nm_scr[h] = _state_tile(t["decay"] * n_row + jnp.sum(kw, axis=0, keepdims=True), t["m_new"])

    return pl.pallas_call(
        body, name="mlstm_fwd", grid=(nc,),
        in_specs=[pl.BlockSpec((L, 2048), lambda c: (c, 0)), pl.BlockSpec((L, 2048), lambda c: (c, O_V // 2048)),
                  pl.BlockSpec((L, LANE), lambda c: (c, O_IF // LANE)), pl.BlockSpec((LANE, L), lambda c: (0, c))],
        out_specs=[pl.BlockSpec((L, 2048), lambda c: (c, 0)),
                   pl.BlockSpec((1, ML_HEADS, ML_DQK, ML_DV), lambda c: (c, 0, 0, 0)),
                   pl.BlockSpec((1, ML_HEADS, 8, LANE), lambda c: (c, 0, 0, 0))],
        out_shape=[jax.ShapeDtypeStruct((s, 2048), F32), jax.ShapeDtypeStruct((nc, ML_HEADS, ML_DQK, ML_DV), F32),
                   jax.ShapeDtypeStruct((nc, ML_HEADS, 8, LANE), F32)],
        scratch_shapes=[pltpu.VMEM((ML_HEADS, ML_DQK, ML_DV), F32), pltpu.VMEM((ML_HEADS, 8, LANE), F32),
                        pltpu.VMEM((L, LANE), F32), pltpu.VMEM((LANE, L), F32)],
        compiler_params=_cparams("arbitrary"))(qk, proj, proj, gt)


def _mlstm_bwd(qk, proj, gt, hout, dh, cst, nm, dproj):
    s = qk.shape[0]
    L = CHUNK
    nc = s // L

    def body(qk_ref, v_ref, gif_ref, gt_ref, h_ref, dh_ref, cst_ref, nm_ref, _, dqk_ref, dv_ref, dif_ref,
             dc_scr, dn_scr, a_scr, at_scr):
        @pl.when(pl.program_id(0) == 0)
        def _():
            dc_scr[...] = jnp.zeros_like(dc_scr)
            dn_scr[...] = jnp.zeros_like(dn_scr)
        a_last_row = _mlstm_gates(gif_ref, gt_ref, a_scr, at_scr)
        gif, a = gif_ref[...], a_scr[...]
        lane = lax.broadcasted_iota(jnp.int32, (L, LANE), 1)
        last = lax.broadcasted_iota(jnp.int32, (L, 1), 0) == L - 1
        di_tile = jnp.zeros((L, LANE), F32)
        cross = [jnp.zeros((L, LANE), F32)] * 3
        for h in range(ML_HEADS):
            c_mat, n_row = cst_ref[0, h], nm_ref[0, h, 0:1, :]
            m_prev = jnp.max(nm_ref[0, h, 1:2, :], axis=1, keepdims=True)
            t = _mlstm_head(h, qk_ref, v_ref, gif, gt_ref, a, at_scr, a_last_row, c_mat, n_row, m_prev)
            q, k, v, den = t["q"], t["k"], t["v"], t["den"]
            dhh = dh_ref[:, h * ML_DV:(h + 1) * ML_DV]
            hh = h_ref[:, h * ML_DV:(h + 1) * ML_DV]
            dnorm = jnp.maximum(jnp.abs(den), t["floor"])
            dnum = dhh / dnorm
            d_dn = -jnp.sum(dhh * hh, axis=1, keepdims=True) / dnorm
            dden = jnp.where(jnp.abs(den) >= t["floor"], jnp.where(den >= 0.0, d_dn, -d_dn), 0.0)
            dsc = _nt(_bf(dnum), _bf(v)) + dden
            ds = dsc * t["w_intra"]
            dq_inter = t["w_inter"] * (_nt(_bf(dnum), _bf(c_mat)) + dden * n_row)
            dq = _nn(_bf(ds), _bf(k)) + dq_inter
            dc, dn_row = dc_scr[h], dn_scr[h, 0:1, :]
            dk_state = t["wk"] * (_nt(_bf(v), _bf(dc)) + dn_row)
            dk = _tn(_bf(ds), _bf(q)) + dk_state
            dv = _tn(_bf(t["sc"]), _bf(dnum)) + t["wk"] * _nn(_bf(k), _bf(dc))
            qi = q * t["w_inter"]
            dc_scr[h] = t["decay"] * dc + _tn(_bf(qi), _bf(dnum))
            dn_scr[h] = jnp.broadcast_to(t["decay"] * dn_row + jnp.sum(qi * dden, axis=0, keepdims=True), (8, LANE))
            dqk_ref[:, h * ML_DQK:(h + 1) * ML_DQK] = dq * (ML_DQK ** -0.5)
            dqk_ref[:, (ML_HEADS + h) * ML_DQK:(ML_HEADS + h + 1) * ML_DQK] = dk
            dv_ref[:, h * ML_DV:(h + 1) * ML_DV] = _bf(dv)
            di_tile = di_tile + jnp.where(lane == h, jnp.sum(k * dk, axis=1, keepdims=True), 0.0)
            carried = t["decay"] * (_sum_all(dc * c_mat) + jnp.sum(dn_row * n_row, axis=1, keepdims=True))
            parts = (_crossing(dsc * t["sc"]),
                     jnp.sum(q * dq_inter, axis=1, keepdims=True) + jnp.where(last, carried, 0.0),
                     jnp.sum(k * dk_state, axis=1, keepdims=True))
            cross = [c + jnp.where(lane == ML_HEADS + h, p, 0.0) for c, p in zip(cross, parts)]
        dfb = cross[0] + _pick_left(_tri(L, True), cross[1], 2) + _pick_left(_tri(L, False) - _eye(L), cross[2], 2)
        dif_ref[:, 0:LANE] = _bf(di_tile + dfb * _sigmoid(-gif))
        dif_ref[:, LANE:SMALL_W] = jnp.zeros((L, SMALL_W - LANE), BF16)

    rev = lambda c: nc - 1 - c
    return pl.pallas_call(
        body, name="mlstm_bwd", grid=(nc,),
        in_specs=[pl.BlockSpec((L, 2048), lambda c: (rev(c), 0)), pl.BlockSpec((L, 2048), lambda c: (rev(c), O_V // 2048)),
                  pl.BlockSpec((L, LANE), lambda c: (rev(c), O_IF // LANE)), pl.BlockSpec((LANE, L), lambda c: (0, rev(c))),
                  pl.BlockSpec((L, 2048), lambda c: (rev(c), 0)), pl.BlockSpec((L, 2048), lambda c: (rev(c), 0)),
                  pl.BlockSpec((1, ML_HEADS, ML_DQK, ML_DV), lambda c: (rev(c), 0, 0, 0)),
                  pl.BlockSpec((1, ML_HEADS, 8, LANE), lambda c: (rev(c), 0, 0, 0)), pl.BlockSpec(memory_space=pl.ANY)],
        out_specs=[pl.BlockSpec((L, 2048), lambda c: (rev(c), 0)), pl.BlockSpec((L, 2048), lambda c: (rev(c), O_V // 2048)),
                   pl.BlockSpec((L, SMALL_W), lambda c: (rev(c), 0))],
        out_shape=[jax.ShapeDtypeStruct((s, 2048), F32), jax.ShapeDtypeStruct(dproj.shape, dproj.dtype),
                   jax.ShapeDtypeStruct((s, SMALL_W), BF16)],
        input_output_aliases={8: 1},
        scratch_shapes=[pltpu.VMEM((ML_HEADS, ML_DQK, ML_DV), F32), pltpu.VMEM((ML_HEADS, 8, LANE), F32),
                        pltpu.VMEM((L, LANE), F32), pltpu.VMEM((LANE, L), F32)],
        compiler_params=_cparams("arbitrary"))(qk, proj, proj, gt, hout, dh, cst, nm, dproj)


GROUP_W = SSM_HEADS // SSM_GROUPS * SSM_HEADDIM
O_B = SSM_HEADS * SSM_HEADDIM
O_C = O_B + SSM_GROUPS * SSM_STATE


def _head_expand():
    r = jnp.arange(LANE)[:, None]
    c = jnp.arange(SSM_HEADS * SSM_HEADDIM)[None, :] // SSM_HEADDIM
    return (r == c).astype(F32)


def _ssd_gates(dt_ref, dtt_ref, alog_row_ref, alog_col_ref, at_scr):
    L = dt_ref.shape[0]
    dt = _softplus(dt_ref[...])
    acoef = -jnp.exp(alog_row_ref[...])
    a = _pick_left(_tri(L, False), dt * acoef, 3)
    at_scr[...] = _pick_right(_softplus(dtt_ref[...]) * (-jnp.exp(alog_col_ref[...])), _tri(L, True), 3)
    return dt, acoef, a


def _ssd_group(g, xbc_ref, dt, a, e_ref, ax_scr):
    eg = e_ref[:, g * GROUP_W:(g + 1) * GROUP_W]
    ax_scr[...] = _pick_right(a, eg, 3)
    ax = ax_scr[...]
    alx = ax_scr[ax.shape[0] - 1:ax.shape[0], :]
    dtx = _pick_right(dt, eg, 2)
    xg = xbc_ref[:, g * GROUP_W:(g + 1) * GROUP_W]
    bg = xbc_ref[:, O_B + g * SSM_STATE:O_B + (g + 1) * SSM_STATE]
    cg = xbc_ref[:, O_C + g * SSM_STATE:O_C + (g + 1) * SSM_STATE]
    return dict(ax=ax, alx=alx, dtx=dtx, xg=xg, bg=bg, cg=cg, xdt=xg * dtx, gmat=_nt(_bf(cg), _bf(bg)))


def _ssd_decay(hh, a, at_scr):
    L = a.shape[0]
    causal = lax.broadcasted_iota(jnp.int32, (L, L), 0) >= lax.broadcasted_iota(jnp.int32, (L, L), 1)
    return jnp.exp(jnp.where(causal, _lane_col(a, hh) - at_scr[hh:hh + 1, :], NEG))


def _ssd_fwd(xbc, proj, dtt, alog_row, alog_col, dskip_x, expand):
    s = xbc.shape[0]
    L = CHUNK
    nc = s // L
    half = SSM_HEADDIM

    def body(xbc_ref, dt_ref, dtt_ref, ar_ref, ac_ref, dk_ref, e_ref, y_ref, st_ref, st_scr, at_scr, ax_scr):
        @pl.when(pl.program_id(0) == 0)
        def _():
            st_scr[...] = jnp.zeros_like(st_scr)
        dt, _, a = _ssd_gates(dt_ref, dtt_ref, ar_ref, ac_ref, at_scr)
        lane = lax.broadcasted_iota(jnp.int32, (L, LANE), 1)
        for g in range(SSM_GROUPS):
            t = _ssd_group(g, xbc_ref, dt, a, e_ref, ax_scr)
            st = st_scr[g]
            st_ref[0, g] = st
            pairs = []
            for j in range(GROUP_W // LANE):
                xp = _bf(t["xdt"][:, j * LANE:(j + 1) * LANE])
                hh = g * (SSM_HEADS // SSM_GROUPS) + 2 * j
                y0 = _nn(_bf(t["gmat"] * _ssd_decay(hh, a, at_scr)), xp)
                y1 = _nn(_bf(t["gmat"] * _ssd_decay(hh + 1, a, at_scr)), xp)
                pairs.append(jnp.where(lane < half, y0, y1))
            y = jnp.concatenate(pairs, axis=1) + _nn(_bf(t["cg"]), _bf(st)) * jnp.exp(t["ax"])
            y_ref[:, g * GROUP_W:(g + 1) * GROUP_W] = y + dk_ref[:, g * GROUP_W:(g + 1) * GROUP_W] * t["xg"]
            wts = jnp.exp(t["alx"] - t["ax"])
            st_scr[g] = jnp.exp(t["alx"]) * st + _tn(_bf(t["bg"]), _bf(t["xdt"] * wts))

    row = lambda w: pl.BlockSpec((1, w), lambda c: (0, 0))
    return pl.pallas_call(
        body, name="ssd_fwd", grid=(nc,),
        in_specs=[pl.BlockSpec((L, 3072), lambda c: (c, 0)), pl.BlockSpec((L, LANE), lambda c: (c, O_DT // LANE)),
                  pl.BlockSpec((LANE, L), lambda c: (0, c)), row(LANE), pl.BlockSpec((LANE, 1), lambda c: (0, 0)),
                  row(2048), pl.BlockSpec((LANE, 2048), lambda c: (0, 0))],
        out_specs=[pl.BlockSpec((L, 2048), lambda c: (c, 0)),
                   pl.BlockSpec((1, SSM_GROUPS, SSM_STATE, GROUP_W), lambda c: (c, 0, 0, 0))],
        out_shape=[jax.ShapeDtypeStruct((s, 2048), F32),
                   jax.ShapeDtypeStruct((nc, SSM_GROUPS, SSM_STATE, GROUP_W), F32)],
        scratch_shapes=[pltpu.VMEM((SSM_GROUPS, SSM_STATE, GROUP_W), F32), pltpu.VMEM((LANE, L), F32),
                        pltpu.VMEM((L, GROUP_W), F32)],
        compiler_params=_cparams("arbitrary"))(xbc, proj, dtt, alog_row, alog_col, dskip_x, expand)


def _ssd_bwd(xbc, proj, dtt, alog_row, alog_col, dskip_x, expand, expand_t, dy, states):
    s = xbc.shape[0]
    L = CHUNK
    nc = s // L
    half = SSM_HEADDIM

    def body(xbc_ref, dt_ref, dtt_ref, ar_ref, ac_ref, dk_ref, e_ref, et_ref, dy_ref, st_ref,
             dxbc_ref, ddt_ref, accd_ref, acca_ref, dst_scr, at_scr, ax_scr):
        @pl.when(pl.program_id(0) == 0)
        def _():
            dst_scr[...] = jnp.zeros_like(dst_scr)
            accd_ref[...] = jnp.zeros_like(accd_ref)
            acca_ref[...] = jnp.zeros_like(acca_ref)
        dt, acoef, a = _ssd_gates(dt_ref, dtt_ref, ar_ref, ac_ref, at_scr)
        lane = lax.broadcasted_iota(jnp.int32, (L, LANE), 1)
        low = lane < half
        last = lax.broadcasted_iota(jnp.int32, (L, 1), 0) == L - 1
        cross = [jnp.zeros((L, LANE), F32)] * 3
        ddt_tile = jnp.zeros((L, LANE), F32)
        for g in range(SSM_GROUPS):
            t = _ssd_group(g, xbc_ref, dt, a, e_ref, ax_scr)
            xg, bg, cg, xdt, gmat = t["xg"], t["bg"], t["cg"], t["xdt"], t["gmat"]
            st, dst = st_ref[0, g], dst_scr[g]
            dyg = dy_ref[:, g * GROUP_W:(g + 1) * GROUP_W]
            ea, eal = jnp.exp(t["ax"]), jnp.exp(t["alx"])
            wts = jnp.exp(t["alx"] - t["ax"])
            dyi = dyg * ea
            y_inter = _nn(_bf(cg), _bf(st)) * ea
            dc = _nt(_bf(dyi), _bf(st))
            d_xdt_state = _nn(_bf(bg), _bf(dst)) * wts
            db = _nt(_bf(xdt * wts), _bf(dst))
            dst_scr[g] = eal * dst + _tn(_bf(cg), _bf(dyi))
            dg = jnp.zeros((L, L), F32)
            dx_pairs = []
            for j in range(GROUP_W // LANE):
                xp = _bf(xdt[:, j * LANE:(j + 1) * LANE])
                dyp = dyg[:, j * LANE:(j + 1) * LANE]
                dxs = []
                for b in range(2):
                    hh = g * (SSM_HEADS // SSM_GROUPS) + 2 * j + b
                    dec = _ssd_decay(hh, a, at_scr)
                    w = gmat * dec
                    dxs.append(_tn(_bf(w), _bf(dyp)))
                    dw = _nt(_bf(jnp.where(low if b == 0 else ~low, dyp, 0.0)), xp)
                    dg = dg + dw * dec
                    cross[0] = cross[0] + jnp.where(lane == hh, _crossing(dw * w), 0.0)
                dx_pairs.append(jnp.where(low, dxs[0], dxs[1]))
            d_xdt = d_xdt_state + jnp.concatenate(dx_pairs, axis=1)
            dc = dc + _nn(_bf(dg), _bf(bg))
            db = db + _tn(_bf(dg), _bf(cg))
            etg = et_ref[g * GROUP_W:(g + 1) * GROUP_W, :]
            carried = jnp.sum(dst * st, axis=0, keepdims=True) * eal
            cross[1] = cross[1] + _pick_right(dyg * y_inter + jnp.where(last, carried, 0.0), etg, 2)
            cross[2] = cross[2] + _pick_right(xdt * d_xdt_state, etg, 2)
            ddt_tile = ddt_tile + _pick_right(d_xdt * xg, etg, 2)
            dxbc_ref[:, g * GROUP_W:(g + 1) * GROUP_W] = d_xdt * t["dtx"] + dk_ref[:, g * GROUP_W:(g + 1) * GROUP_W] * dyg
            dxbc_ref[:, O_B + g * SSM_STATE:O_B + (g + 1) * SSM_STATE] = db
            dxbc_ref[:, O_C + g * SSM_STATE:O_C + (g + 1) * SSM_STATE] = dc
            accd_ref[0:1, g * GROUP_W:(g + 1) * GROUP_W] += jnp.sum(dyg * xg, axis=0, keepdims=True)
        d_da = cross[0] + _pick_left(_tri(L, True), cross[1], 2) + _pick_left(_tri(L, False) - _eye(L), cross[2], 2)
        acca_ref[0:1, :] += jnp.sum(d_da * dt, axis=0, keepdims=True)
        ddt_ref[:, 0:LANE] = _bf((ddt_tile + d_da * acoef) * _sigmoid(dt_ref[...]))
        ddt_ref[:, LANE:SMALL_W] = jnp.zeros((L, SMALL_W - LANE), BF16)

    rev = lambda c: nc - 1 - c
    row = lambda w: pl.BlockSpec((1, w), lambda c: (0, 0))
    return pl.pallas_call(
        body, name="ssd_bwd", grid=(nc,),
        in_specs=[pl.BlockSpec((L, 3072), lambda c: (rev(c), 0)), pl.BlockSpec((L, LANE), lambda c: (rev(c), O_DT // LANE)),
                  pl.BlockSpec((LANE, L), lambda c: (0, rev(c))), row(LANE), pl.BlockSpec((LANE, 1), lambda c: (0, 0)),
                  row(2048), pl.BlockSpec((LANE, 2048), lambda c: (0, 0)), pl.BlockSpec((2048, LANE), lambda c: (0, 0)),
                  pl.BlockSpec((L, 2048), lambda c: (rev(c), 0)),
                  pl.BlockSpec((1, SSM_GROUPS, SSM_STATE, GROUP_W), lambda c: (rev(c), 0, 0, 0))],
        out_specs=[pl.BlockSpec((L, 3072), lambda c: (rev(c), 0)), pl.BlockSpec((L, SMALL_W), lambda c: (rev(c), 0)),
                   pl.BlockSpec((8, 2048), lambda c: (0, 0)), pl.BlockSpec((8, LANE), lambda c: (0, 0))],
        out_shape=[jax.ShapeDtypeStruct((s, 3072), F32), jax.ShapeDtypeStruct((s, SMALL_W), BF16),
                   jax.ShapeDtypeStruct((8, 2048), F32), jax.ShapeDtypeStruct((8, LANE), F32)],
        scratch_shapes=[pltpu.VMEM((SSM_GROUPS, SSM_STATE, GROUP_W), F32),
                        pltpu.VMEM((LANE, L), F32), pltpu.VMEM((L, GROUP_W), F32)],
        compiler_params=_cparams("arbitrary"))(xbc, proj, dtt, alog_row, alog_col, dskip_x, expand, expand_t, dy, states)


def _group_norm(v, width):
    outs, rs = [], []
    for k in range(v.shape[1] // width):
        blk = v[:, k * width:(k + 1) * width]
        r = lax.rsqrt(jnp.mean(blk * blk, axis=1, keepdims=True) + EPS)
        outs.append(blk * r)
        rs.append(jnp.broadcast_to(r, blk.shape))
    return jnp.concatenate(outs, axis=1), jnp.concatenate(rs, axis=1)


def _group_mean(v, width):
    return jnp.concatenate([jnp.broadcast_to(jnp.mean(v[:, k * width:(k + 1) * width], axis=1, keepdims=True),
                                             (v.shape[0], width)) for k in range(v.shape[1] // width)], axis=1)


def _post_fwd(hm, yssd, proj, ml_norm_w, ssm_norm_w, ts):
    s = hm.shape[0]

    def body(h_ref, ys_ref, o_ref, zm_ref, zs_ref, wm_ref, ws_ref, ym_ref, yso_ref):
        hn, _ = _group_norm(h_ref[...], ML_DV)
        ym_ref[...] = _bf(_sigmoid(o_ref[...]) * hn * wm_ref[...] * _silu(zm_ref[...]))
        pn, _ = _group_norm(ys_ref[...] * _silu(zs_ref[...]), GROUP_W)
        yso_ref[...] = _bf(pn * ws_ref[...])

    tile = pl.BlockSpec((ts, 2048), lambda i: (i, 0))
    col = lambda off: pl.BlockSpec((ts, 2048), lambda i: (i, off // 2048))
    row = pl.BlockSpec((1, 2048), lambda i: (0, 0))
    return pl.pallas_call(
        body, name="post_fwd", grid=(s // ts,),
        in_specs=[tile, tile, col(O_O), col(O_ZM), col(O_ZS), row, row],
        out_specs=[tile, tile],
        out_shape=[jax.ShapeDtypeStruct((s, 2048), BF16)] * 2,
        compiler_params=_cparams("parallel"))(hm, yssd, proj, proj, proj, ml_norm_w, ssm_norm_w)


def _post_bwd(dym, dys, hm, yssd, proj, ml_norm_w, ssm_norm_w, dproj, ts):
    s = hm.shape[0]

    def body(dym_ref, dys_ref, h_ref, ys_ref, o_ref, zm_ref, zs_ref, wm_ref, ws_ref, _,
             dh_ref, dyssd_ref, dp_ref, acc_ref):
        @pl.when(pl.program_id(0) == 0)
        def _():
            acc_ref[...] = jnp.zeros_like(acc_ref)
        hn, r = _group_norm(h_ref[...], ML_DV)
        so, zm, wm, d_ym = _sigmoid(o_ref[...]), zm_ref[...], wm_ref[...], dym_ref[...]
        sz = _silu(zm)
        hnw = hn * wm
        dp_ref[:, O_O:O_O + 2048] = _bf(d_ym * hnw * sz * so * (1.0 - so))
        dp_ref[:, O_ZM:O_ZM + 2048] = _bf(d_ym * so * hnw * _dsilu(zm))
        dhnw = d_ym * so * sz
        acc_ref[0:1, :] += jnp.sum(dhnw * hn, axis=0, keepdims=True)
        dhn = dhnw * wm
        dh_ref[...] = r * (dhn - hn * _group_mean(dhn * hn, ML_DV))
        ysv, zs, d_ys = ys_ref[...], zs_ref[...], dys_ref[...]
        szs = _silu(zs)
        pn, r2 = _group_norm(ysv * szs, GROUP_W)
        acc_ref[1:2, :] += jnp.sum(d_ys * pn, axis=0, keepdims=True)
        dpn = d_ys * ws_ref[...]
        dp = r2 * (dpn - pn * _group_mean(dpn * pn, GROUP_W))
        dyssd_ref[...] = dp * szs
        dp_ref[:, O_ZS:O_ZS + 2048] = _bf(dp * ysv * _dsilu(zs))

    tile = pl.BlockSpec((ts, 2048), lambda i: (i, 0))
    col = lambda off: pl.BlockSpec((ts, 2048), lambda i: (i, off // 2048))
    row = pl.BlockSpec((1, 2048), lambda i: (0, 0))
    sds = lambda dt: jax.ShapeDtypeStruct((s, 2048), dt)
    return pl.pallas_call(
        body, name="post_bwd", grid=(s // ts,),
        in_specs=[tile, tile, tile, tile, col(O_O), col(O_ZM), col(O_ZS), row, row, pl.BlockSpec(memory_space=pl.ANY)],
        out_specs=[tile, tile, pl.BlockSpec((ts, O_MG), lambda i: (i, 0)), pl.BlockSpec((8, 2048), lambda i: (0, 0))],
        out_shape=[sds(F32), sds(F32), jax.ShapeDtypeStruct(dproj.shape, dproj.dtype), jax.ShapeDtypeStruct((8, 2048), F32)],
        input_output_aliases={9: 2},
        compiler_params=_cparams("arbitrary"))(dym, dys, hm, yssd, proj, proj, proj, ml_norm_w, ssm_norm_w, dproj)


def _merge(x, ym, ys, proj, target, gate, final_w, wpm, wps, wo, ts):
    wpm_t, wps_t, wo_t = wpm.T, wps.T, wo.T
    s, d = x.shape

    def body(x_ref, ym_ref, ys_ref, mg_ref, t_ref, gate_ref, fw_ref, wpm_ref, wps_ref, wo_ref, wpmt_ref, wpst_ref, wot_ref,
             dres_ref, mer_ref, dmo_ref, dpm_ref, dps_ref, dym_ref, dys_ref, dmg_ref, acc_ref):
        @pl.when(pl.program_id(0) == 0)
        def _():
            acc_ref[...] = jnp.zeros_like(acc_ref)
        gm, gs = _sigmoid(mg_ref[:, 0:d]), _sigmoid(mg_ref[:, d:2 * d])
        pm = _nn(ym_ref[...], wpm_ref[...])
        ps = _nn(ys_ref[...], wps_ref[...])
        merged = _bf(gm * pm + gs * ps)
        mer_ref[...] = merged
        mo = _nn(merged, wo_ref[...])
        gate, fw = gate_ref[...], fw_ref[...]
        out = x_ref[...] + gate * mo
        r = lax.rsqrt(jnp.mean(out * out, axis=1, keepdims=True) + EPS)
        on = out * r
        diff = on * fw - t_ref[...]
        acc_ref[0:1, :] += jnp.sum(0.5 * jnp.sum(diff * diff, axis=1, keepdims=True) / d, axis=0, keepdims=True)
        dyv = diff * (1.0 / d)
        acc_ref[1:2, :] += jnp.sum(dyv * on, axis=0, keepdims=True)
        don = dyv * fw
        dout = r * (don - on * jnp.mean(don * on, axis=1, keepdims=True))
        dres_ref[...] = dout
        acc_ref[2:3, :] += jnp.sum(dout * mo, axis=0, keepdims=True)
        dmo = _bf(dout * gate)
        dmo_ref[...] = dmo
        dmer = _nn(dmo, wot_ref[...])
        dpm, dps = _bf(dmer * gm), _bf(dmer * gs)
        dpm_ref[...] = dpm
        dps_ref[...] = dps
        dmg_ref[:, 0:d] = _bf(dmer * pm * gm * (1.0 - gm))
        dmg_ref[:, d:2 * d] = _bf(dmer * ps * gs * (1.0 - gs))
        dym_ref[...] = _nn(dpm, wpmt_ref[...])
        dys_ref[...] = _nn(dps, wpst_ref[...])

    t1 = pl.BlockSpec((ts, d), lambda i: (i, 0))
    t2 = pl.BlockSpec((ts, 2 * d), lambda i: (i, 0))
    row = pl.BlockSpec((1, d), lambda i: (0, 0))
    whole = pl.BlockSpec(memory_space=pltpu.VMEM)
    sd = lambda w, dt: jax.ShapeDtypeStruct((s, w), dt)
    return pl.pallas_call(
        body, name="merge_fwd_bwd", grid=(s // ts,),
        in_specs=[t1, t2, t2, pl.BlockSpec((ts, 2 * d), lambda i: (i, O_MG // (2 * d))), t1, row, row] + [whole] * 6,
        out_specs=[t1, t1, t1, t1, t1, t2, t2, pl.BlockSpec((ts, 2 * d), lambda i: (i, O_MG // (2 * d))),
                   pl.BlockSpec((8, d), lambda i: (0, 0))],
        out_shape=[sd(d, F32), sd(d, BF16), sd(d, BF16), sd(d, BF16), sd(d, BF16), sd(2 * d, F32), sd(2 * d, F32),
                   sd(NP, BF16), jax.ShapeDtypeStruct((8, d), F32)],
        compiler_params=_cparams("arbitrary"))(x, ym, ys, proj, target, gate, final_w, wpm, wps, wo, wpm_t, wps_t, wo_t)


def _adamw(w, g, m, v, tr):
    rows, cols = w.shape[-2:]
    lead = (None,) * (w.ndim - 2)

    def body(w_ref, g_ref, m_ref, v_ref, d_ref, nm_ref, nv_ref):
        gv = g_ref[...]
        m2 = ADAM_B1 * m_ref[...] + (1.0 - ADAM_B1) * gv
        v2 = ADAM_B2 * v_ref[...] + (1.0 - ADAM_B2) * (gv * gv)
        m_hat = m2 / (1.0 - ADAM_B1 ** ADAM_STEP)
        v_hat = v2 / (1.0 - ADAM_B2 ** ADAM_STEP)
        d_ref[...] = -ADAM_LR * (m_hat / (jnp.sqrt(v_hat) + ADAM_EPS) + ADAM_WD * w_ref[...])
        nm_ref[...] = m2
        nv_ref[...] = v2

    tile = pl.BlockSpec(lead + (tr, cols), lambda i: (0,) * len(lead) + (i, 0))
    return pl.pallas_call(
        body, name="adamw", grid=(rows // tr,), in_specs=[tile] * 4, out_specs=[tile] * 3,
        out_shape=[jax.ShapeDtypeStruct(w.shape, F32)] * 3,
        compiler_params=_cparams("parallel"))(w, g.reshape(w.shape), m, v)


def _sum_parts(own, parts, tr, dtype=F32):
    p, rows, cols = parts.shape

    def body(*refs):
        p_ref, o_ref = refs[-2], refs[-1]
        acc = p_ref[0].astype(F32) if own is None else refs[0][...].astype(F32) + p_ref[0].astype(F32)
        for i in range(1, p):
            acc = acc + p_ref[i].astype(F32)
        o_ref[...] = acc.astype(dtype)

    tile = pl.BlockSpec((tr, cols), lambda i: (i, 0))
    ins = ([] if own is None else [tile]) + [pl.BlockSpec((p, tr, cols), lambda i: (0, i, 0))]
    args = ([] if own is None else [own]) + [parts]
    return pl.pallas_call(
        body, name="sum_parts", grid=(rows // tr,), in_specs=ins, out_specs=tile,
        out_shape=jax.ShapeDtypeStruct((rows, cols), dtype), compiler_params=_cparams("parallel"))(*args)


def _position():
    return lax.axis_index("x"), lax.axis_index("y"), lax.axis_index("c")


def _flip(pos, k):
    return tuple(1 - p if (k >> s) & 1 else p for p, s in zip(pos, (2, 1, 0)))


def _allgather8(block):
    rows, cols = block.shape

    def body(x_ref, o_ref, send_sems, recv_sems, local_sem):
        pos = _position()
        me = 4 * pos[0] + 2 * pos[1] + pos[2]
        mine = pltpu.make_async_copy(x_ref, o_ref.at[me], local_sem)
        mine.start()
        copies = [pltpu.make_async_remote_copy(src_ref=x_ref, dst_ref=o_ref.at[me], send_sem=send_sems.at[k - 1],
                                               recv_sem=recv_sems.at[k - 1], device_id=_flip(pos, k), device_id_type=MESH)
                  for k in range(1, N_DEV)]
        for cp in copies:
            cp.start()
        for cp in copies:
            cp.wait()
        mine.wait()

    vmem = pl.BlockSpec(memory_space=pltpu.VMEM)
    return pl.pallas_call(
        body, name="allgather8", in_specs=[vmem], out_specs=vmem,
        out_shape=jax.ShapeDtypeStruct((N_DEV, rows, cols), block.dtype),
        scratch_shapes=[pltpu.SemaphoreType.DMA((N_DEV - 1,)), pltpu.SemaphoreType.DMA((N_DEV - 1,)),
                        pltpu.SemaphoreType.DMA],
        compiler_params=pltpu.CompilerParams(vmem_limit_bytes=VMEM_LIMIT))(block)


COPY_BYTES = 1 << 20


def _row_chunks(rows, row_bytes):
    n = max(1, min(rows // 16, -(-rows * row_bytes // COPY_BYTES)))
    while rows % (16 * n):
        n -= 1
    return [(i * (rows // n), rows // n) for i in range(n)]


def _weight_gather(shards):
    n = len(shards)
    pieces = [_row_chunks(a.shape[1], a.shape[2] * a.dtype.itemsize) for a in shards]
    plan = [(a, k, r0, nr) for a in range(n) for k in range(1, N_CHIPS) for r0, nr in pieces[a]]

    def body(*refs):
        ins, outs = refs[:n], refs[n:2 * n]
        ici_send, ici_recv, d2d_send, d2d_recv = refs[2 * n:]
        pos = _position()
        chip, core = 2 * pos[0] + pos[1], pos[2]
        sibling = _flip(pos, 1)
        sent = []
        for i, (a, k, r0, nr) in enumerate(plan):
            cp = pltpu.make_async_remote_copy(
                src_ref=ins[a].at[core, pl.ds(r0, nr)], dst_ref=outs[a].at[chip, core, pl.ds(r0, nr)],
                send_sem=ici_send.at[i], recv_sem=ici_recv.at[i], device_id=_flip(pos, 2 * k), device_id_type=MESH)
            cp.start()
            sent.append(cp)
        passed = []
        for i, (a, k, r0, nr) in enumerate(plan):
            there = _flip(pos, 2 * k)
            landed = outs[a].at[2 * there[0] + there[1], core, pl.ds(r0, nr)]
            sent[i].wait_recv()
            cp = pltpu.make_async_remote_copy(src_ref=landed, dst_ref=landed, send_sem=d2d_send.at[i],
                                              recv_sem=d2d_recv.at[i], device_id=sibling, device_id_type=MESH)
            cp.start()
            passed.append(cp)
        for cp in passed:
            cp.wait()
        for cp in sent:
            cp.wait_send()

    hbm = pl.BlockSpec(memory_space=pl.ANY)
    sems = pltpu.SemaphoreType.DMA((len(plan),))
    return pl.pallas_call(
        body, name="weight_gather", in_specs=[hbm] * n, out_specs=[hbm] * n,
        out_shape=[jax.ShapeDtypeStruct((N_CHIPS,) + a.shape, a.dtype) for a in shards],
        scratch_shapes=[sems, sems, sems, sems],
        compiler_params=pltpu.CompilerParams(has_side_effects=True))(*shards)


def _exchange(name, arrays, out_shapes, plan, n_remote, n_local):
    n, m = len(arrays), len(out_shapes)

    def body(*refs):
        send_sems, recv_sems, local_sems = refs[n + m:]
        remote, local = plan(_position(), refs[:n], refs[n:n + m])
        assert (len(remote), len(local)) == (n_remote, n_local)
        copies = [pltpu.make_async_copy(src, dst, local_sems.at[i]) for i, (src, dst) in enumerate(local)]
        copies += [pltpu.make_async_remote_copy(src_ref=src, dst_ref=dst, send_sem=send_sems.at[i], recv_sem=recv_sems.at[i],
                                                device_id=dev, device_id_type=MESH)
                   for i, (src, dst, dev) in enumerate(remote)]
        for cp in copies:
            cp.start()
        for cp in copies:
            cp.wait()

    hbm = pl.BlockSpec(memory_space=pl.ANY)
    return pl.pallas_call(
        body, name=name, in_specs=[hbm] * n, out_specs=[hbm] * m, out_shape=out_shapes,
        scratch_shapes=[pltpu.SemaphoreType.DMA((n_remote,)), pltpu.SemaphoreType.DMA((n_remote,)),
                        pltpu.SemaphoreType.DMA((max(n_local, 1),))],
        compiler_params=pltpu.CompilerParams(has_side_effects=True))(*arrays)


def _pair_send(slabs):
    n = len(slabs)
    pieces = [_row_chunks(g.shape[2], g.shape[3] * g.dtype.itemsize) for g in slabs]

    def plan(pos, ins, outs):
        return [(ins[a].at[j, 1 - pos[2], pl.ds(r0, nr)], outs[a].at[j, pl.ds(r0, nr)], _flip(pos, 1))
                for a in range(n) for j in range(N_CHIPS) for r0, nr in pieces[a]], []

    return _exchange("pair_send", slabs, [jax.ShapeDtypeStruct((N_CHIPS,) + g.shape[2:], g.dtype) for g in slabs], plan,
                     N_CHIPS * sum(len(p) for p in pieces), 0)


def _chip_scatter(sums):
    n = len(sums)
    pieces = [_row_chunks(g.shape[1], g.shape[2] * g.dtype.itemsize) for g in sums]

    def plan(pos, ins, outs):
        remote = []
        for a in range(n):
            for k in range(1, N_CHIPS):
                to = _flip(pos, 2 * k)
                remote += [(ins[a].at[2 * to[0] + to[1], pl.ds(r0, nr)], outs[a].at[k - 1, pl.ds(r0, nr)], to)
                           for r0, nr in pieces[a]]
        return remote, []

    return _exchange("chip_scatter", sums, [jax.ShapeDtypeStruct((N_CHIPS - 1,) + g.shape[1:], g.dtype) for g in sums],
                     plan, (N_CHIPS - 1) * sum(len(p) for p in pieces), 0)


def _pair_exchange(halves):
    n = len(halves)
    pieces = [_row_chunks(h.shape[0], h.shape[1] * h.dtype.itemsize) for h in halves]

    def plan(pos, ins, outs):
        return [(ins[a].at[pl.ds(r0, nr)], outs[a].at[pl.ds(r0, nr)], _flip(pos, 1))
                for a in range(n) for r0, nr in pieces[a]], []

    return _exchange("pair_exchange", halves, [jax.ShapeDtypeStruct(h.shape, h.dtype) for h in halves], plan,
                     sum(len(p) for p in pieces), 0)


def _pack(arrays):
    flat = jnp.concatenate([a.reshape(-1).astype(F32) for a in arrays])
    size = -(-flat.shape[0] // (8 * LANE)) * (8 * LANE)
    return jnp.pad(flat, (0, size - flat.shape[0])).reshape(size // LANE, LANE)


def _unpack(buf, shapes):
    flat = buf.reshape(-1)
    out, off = [], 0
    for shp in shapes:
        n = math.prod(shp)
        out.append(flat[off:off + n].reshape(shp))
        off += n
    return out


def _unpack_rows(bufs, shapes):
    flat = bufs.reshape(bufs.shape[0], -1)
    out, off = [], 0
    for shp in shapes:
        n = math.prod(shp)
        out.append(flat[:, off:off + n].reshape((bufs.shape[0],) + shp))
        off += n
    return out


def _taps8(w):
    return jnp.pad(w, ((0, 8 - CONV_K), (0, 0)))


def _local_step(xs, tgt, scale, shift, gate, norm_w, w_in_p, b_in_p, ml_conv_w, ml_conv_b, ml_norm_w, ssm_conv_w,
                ssm_conv_b, ssm_a_log, ssm_d, ssm_norm_w, wpm, wps, wo, final_w):
    s = xs.shape[0]
    ts = min(512, s)
    tm = min(1024, s)
    u = _prenorm_fwd(xs, norm_w, scale, shift, ts)
    proj = _matmul_bias(u, w_in_p, b_in_p, tm, 512)
    mlw8, ssw8 = _taps8(ml_conv_w), _taps8(ssm_conv_w)
    qk = _conv_fwd(proj, O_QK, 2048, mlw8, ml_conv_b, ts)
    xbc = _conv_fwd(proj, O_XBC, 3072, ssw8, ssm_conv_b, ts)
    gt = proj[:, O_IF:O_IF + LANE].T
    dtt = proj[:, O_DT:O_DT + LANE].T
    hm, cst, nm = _mlstm_fwd(qk, proj, gt)
    alog_row = jnp.pad(ssm_a_log, ((0, 0), (0, LANE - SSM_HEADS)))
    alog_col = alog_row.reshape(LANE, 1)
    dskip_x = jnp.repeat(ssm_d[0], SSM_HEADDIM)[None]
    expand = _head_expand()
    yssd, sst = _ssd_fwd(xbc, proj, dtt, alog_row, alog_col, dskip_x, expand)
    tp = min(128, s)
    ym, ys = _post_fwd(hm, yssd, proj, ml_norm_w, ssm_norm_w, tp)
    dxres, merged, dmo, dpm, dps, dym, dys, dproj, acc_m = _merge(xs, ym, ys, proj, tgt, gate, final_w, wpm, wps, wo, tp)
    dh, dyssd, dproj, acc_p = _post_bwd(dym, dys, hm, yssd, proj, ml_norm_w, ssm_norm_w, dproj, tp)
    dqk, dproj, dif = _mlstm_bwd(qk, proj, gt, hm, dh, cst, nm, dproj)
    dxbc, ddt, accd, acca = _ssd_bwd(xbc, proj, dtt, alog_row, alog_col, dskip_x, expand, expand.T, dyssd, sst)
    dproj, acc_cq = _conv_bwd(proj, O_QK, 2048, mlw8, ml_conv_b, dqk, dproj, ts)
    dproj, acc_cx = _conv_bwd(proj, O_XBC, 3072, ssw8, ssm_conv_b, dxbc, dproj, ts)
    dproj = dproj.at[:, O_IF:O_IF + SMALL_W].set(dif).at[:, O_DT:O_DT + SMALL_W].set(ddt)
    gw_in_p, gb_in_p = _matmul_tn(u, dproj, tm, 512, with_colsum=True)
    du = _matmul_nt(dproj, w_in_p, tm, 512)
    grad_x, acc_n = _prenorm_bwd(du, xs, dxres, norm_w, scale, ts)
    g_wpm = _matmul_tn(ym, dpm, tm, 512)
    g_wps = _matmul_tn(ys, dps, tm, 512)
    g_wo = _matmul_tn(merged, dmo, tm, 512)
    a_coef = -jnp.exp(ssm_a_log[0])
    small = dict(
        mod=jnp.concatenate([acc_n[2], acc_n[1], acc_m[2]]), norm_w=acc_n[0], b_in=_unpad_cols(gb_in_p[0]),
        ml_conv_w=acc_cq[0:CONV_K], ml_conv_b=acc_cq[CONV_K], ml_norm_w=acc_p[0], ssm_conv_w=acc_cx[0:CONV_K],
        ssm_conv_b=acc_cx[CONV_K], ssm_a_log=acca[0, :SSM_HEADS] * a_coef,
        ssm_d=accd[0].reshape(SSM_HEADS, SSM_HEADDIM).sum(axis=1), ssm_norm_w=acc_p[1], final_w=acc_m[1], loss=acc_m[0, 0:1])
    return grad_x, small, gw_in_p, g_wpm, g_wps, g_wo


WEIGHTS = ("norm_w", "ada_w", "ada_b", "w_in", "b_in", "ml_conv_w", "ml_conv_b", "ml_norm_w", "ssm_conv_w", "ssm_conv_b",
           "ssm_a_log", "ssm_d", "ssm_norm_w", "w_proj_m", "w_proj_s", "w_out", "final_w")
LARGE = ("ada_w", "w_in", "w_proj_m", "w_proj_s", "w_out")
SMALL_SUMS = (("mod", (3 * D_MODEL,)), ("norm_w", (D_MODEL,)), ("b_in", (IN_WIDTH,)), ("ml_conv_w", (CONV_K, 2048)),
              ("ml_conv_b", (2048,)), ("ml_norm_w", (2048,)), ("ssm_conv_w", (CONV_K, 3072)), ("ssm_conv_b", (3072,)),
              ("ssm_a_log", (SSM_HEADS,)), ("ssm_d", (SSM_HEADS,)), ("ssm_norm_w", (2048,)), ("final_w", (D_MODEL,)),
              ("loss", (1,)))


def kernel(x, c, norm_w, ada_w, ada_b, w_in, b_in, ml_conv_w, ml_conv_b, ml_norm_w, ssm_conv_w, ssm_conv_b, ssm_a_log, ssm_d, ssm_norm_w, w_proj_m, w_proj_s, w_out, final_w, loss_target, m_norm_w, m_ada_w, m_ada_b, m_w_in, m_b_in, m_ml_conv_w, m_ml_conv_b, m_ml_norm_w, m_ssm_conv_w, m_ssm_conv_b, m_ssm_a_log, m_ssm_d, m_ssm_norm_w, m_w_proj_m, m_w_proj_s, m_w_out, m_final_w, v_norm_w, v_ada_w, v_ada_b, v_w_in, v_b_in, v_ml_conv_w, v_ml_conv_b, v_ml_norm_w, v_ssm_conv_w, v_ssm_conv_b, v_ssm_a_log, v_ssm_d, v_ssm_norm_w, v_w_proj_m, v_w_proj_s, v_w_out, v_final_w):
    w = dict(norm_w=norm_w, ada_w=ada_w, ada_b=ada_b, w_in=w_in, b_in=b_in, ml_conv_w=ml_conv_w, ml_conv_b=ml_conv_b,
             ml_norm_w=ml_norm_w, ssm_conv_w=ssm_conv_w, ssm_conv_b=ssm_conv_b, ssm_a_log=ssm_a_log, ssm_d=ssm_d,
             ssm_norm_w=ssm_norm_w, w_proj_m=w_proj_m, w_proj_s=w_proj_s, w_out=w_out, final_w=final_w)
    m = dict(zip(WEIGHTS, (m_norm_w, m_ada_w, m_ada_b, m_w_in, m_b_in, m_ml_conv_w, m_ml_conv_b, m_ml_norm_w, m_ssm_conv_w,
                           m_ssm_conv_b, m_ssm_a_log, m_ssm_d, m_ssm_norm_w, m_w_proj_m, m_w_proj_s, m_w_out, m_final_w)))
    v = dict(zip(WEIGHTS, (v_norm_w, v_ada_w, v_ada_b, v_w_in, v_b_in, v_ml_conv_w, v_ml_conv_b, v_ml_norm_w, v_ssm_conv_w,
                           v_ssm_conv_b, v_ssm_a_log, v_ssm_d, v_ssm_norm_w, v_w_proj_m, v_w_proj_s, v_w_out, v_final_w)))
    pos = _position()
    chip = 2 * pos[0] + pos[1]
    dev = 2 * chip + pos[2]
    mlw_cols, ssw_cols, ada_cols = ml_conv_w.shape[2], ssm_conv_w.shape[2], ada_w.shape[2]

    g0 = _allgather8(_pack([c, ml_conv_w, ssm_conv_w]))
    c_all, mlw_all, ssw_all = _unpack_rows(g0, [(D_MODEL,), (CONV_K, mlw_cols), (CONV_K, ssw_cols)])
    ml_conv_full = mlw_all[0::2].transpose(1, 0, 2).reshape(CONV_K, N_CHIPS * mlw_cols)
    ssm_conv_full = ssw_all[0::2].transpose(1, 0, 2).reshape(CONV_K, N_CHIPS * ssw_cols)

    ada_b_mine = lax.dynamic_slice_in_dim(ada_b, chip * ada_cols, ada_cols, axis=1)
    g1 = _allgather8(_ada_fwd(c_all, ada_w[0], ada_b_mine))
    mod = lax.dynamic_index_in_dim(g1[0::2], dev, axis=1, keepdims=False).reshape(1, 3 * D_MODEL)
    shift, scale, gate = mod[:, :D_MODEL], mod[:, D_MODEL:2 * D_MODEL], mod[:, 2 * D_MODEL:]

    mine = [_bf(a[0]).reshape(2, a.shape[1] // 2, a.shape[2]) for a in (w_in, w_proj_m, w_proj_s, w_out)]
    gw = [lax.dynamic_update_index_in_dim(got, own, chip, 0).reshape(N_CHIPS, -1, own.shape[-1])
          for got, own in zip(_weight_gather(mine), mine)]
    w_in_p = _shards_to_padded(gw[0])
    wpm, wps, wo = (a.reshape(-1, D_MODEL) for a in gw[1:])

    grad_x, small, g_w_in, g_wpm, g_wps, g_wo = _local_step(
        x[0], loss_target[0], scale, shift, gate, norm_w, w_in_p, _pad_cols(b_in), ml_conv_full, ml_conv_b, ml_norm_w,
        ssm_conv_full, ssm_conv_b, ssm_a_log, ssm_d, ssm_norm_w, wpm, wps, wo, final_w[None])

    g2 = _allgather8(_pack([small[name] for name, _ in SMALL_SUMS]))
    total = dict(zip([name for name, _ in SMALL_SUMS], _unpack(_sum_parts(None, g2, g2.shape[1]), [s for _, s in SMALL_SUMS])))
    dmod_all = g2[:, :3 * D_MODEL // LANE].reshape(N_DEV, 3 * D_MODEL)
    grads = dict(total)
    grads["ada_b"] = total["mod"]
    grads["ml_conv_w"] = lax.dynamic_slice_in_dim(total["ml_conv_w"], chip * mlw_cols, mlw_cols, axis=1)
    grads["ssm_conv_w"] = lax.dynamic_slice_in_dim(total["ssm_conv_w"], chip * ssw_cols, ssw_cols, axis=1)
    grads["ada_w"] = _ada_bwd(c_all, lax.dynamic_slice_in_dim(dmod_all, chip * ada_cols, ada_cols, axis=1))

    split = lambda g, rows: _bf(g).reshape(N_CHIPS, 2, rows // (2 * N_CHIPS), g.shape[-1])
    slabs = [split(_padded_to_shards(_bf(g_w_in)), N_CHIPS * D_MODEL),
             split(g_wpm, g_wpm.shape[0]), split(g_wps, g_wps.shape[0]), split(g_wo, g_wo.shape[0])]
    pair_sums = []
    for slab, rec in zip(slabs, _pair_send(slabs)):
        kept = lax.dynamic_index_in_dim(slab, pos[2], 1, keepdims=False)
        rows = kept.shape[0] * kept.shape[1]
        both = _sum_parts(kept.reshape(rows, -1), rec.reshape(1, rows, -1), 32, BF16)
        pair_sums.append(both.reshape(kept.shape))
    halves = []
    for both, rec in zip(pair_sums, _chip_scatter(pair_sums)):
        halves.append(_sum_parts(lax.dynamic_index_in_dim(both, chip, 0, keepdims=False), rec, 32))
    for name, half, other in zip(("w_in", "w_proj_m", "w_proj_s", "w_out"), halves, _pair_exchange(halves)):
        grads[name] = jnp.where(pos[2] == 0, jnp.concatenate([half, other]), jnp.concatenate([other, half]))

    delta, new_m, new_v = {}, {}, {}
    for name in LARGE:
        delta[name], new_m[name], new_v[name] = _adamw(w[name], grads[name], m[name], v[name], 64)
    rest = [name for name in WEIGHTS if name not in LARGE]
    packed = [_pack([t[name] for name in rest]) for t in (w, grads, m, v)]
    for out, buf in zip((delta, new_m, new_v), _adamw(*packed, packed[0].shape[0])):
        out.update(zip(rest, _unpack(buf, [w[name].shape for name in rest])))
    loss = total["loss"][0]
    return (loss, grad_x[None], *[grads[name].reshape(w[name].shape) for name in WEIGHTS], *[delta[name] for name in WEIGHTS],
            *[new_m[name] for name in WEIGHTS], *[new_v[name] for name in WEIGHTS])
```

```python
import functools
import math

import jax
import jax.numpy as jnp
from jax import lax
from jax.experimental import pallas as pl
from jax.experimental.pallas import tpu as pltpu

F32 = jnp.float32
BF16 = jnp.bfloat16
HI = lax.Precision.HIGHEST
MESH = pl.DeviceIdType.MESH

D_MODEL = 1024
EPS = 1e-6
CONV_K = 4
ML_HEADS = 8
ML_DQK = 128
ML_DV = 256
SSM_HEADS = 32
SSM_HEADDIM = 64
SSM_GROUPS = 4
SSM_STATE = 128
IN_WIDTH = 15408
N_CHIPS = 4
N_DEV = 8
ADAM_LR, ADAM_B1, ADAM_B2, ADAM_EPS, ADAM_WD, ADAM_STEP = 0.001, 0.9, 0.999, 1e-08, 0.01, 10

O_O, O_ZM, O_ZS, O_MG, O_QK, O_V, O_XBC, O_IF, O_DT = 0, 2048, 4096, 6144, 8192, 10240, 12288, 15360, 15616
SMALL_W = 256
NP = 15872
LANE = 128
CHUNK = 128
NEG = -1e30
VMEM_LIMIT = 48 * 1024 * 1024


def _cparams(*sem):
    return pltpu.CompilerParams(dimension_semantics=sem, vmem_limit_bytes=VMEM_LIMIT)


def _pad_cols(w):
    z = lambda n: jnp.zeros(w.shape[:-1] + (n,), w.dtype)
    return jnp.concatenate([w[..., 4096:8192], w[..., 11280:13328], w[..., 13360:15408], w[..., :4096], w[..., 8208:11280],
                            w[..., 8192:8208], z(SMALL_W - 16), w[..., 13328:13360], z(SMALL_W - 32)], axis=-1)


def _unpad_cols(g):
    return jnp.concatenate([g[..., O_QK:O_QK + 4096], g[..., O_O:O_O + 4096], g[..., O_IF:O_IF + 16],
                            g[..., O_XBC:O_XBC + 3072], g[..., O_ZS:O_ZS + 2048], g[..., O_DT:O_DT + 32],
                            g[..., O_MG:O_MG + 2048]], axis=-1)


PADDED_SEGMENTS = ((4096, 8192, 0), (11280, 13328, 0), (13360, 15408, 0), (0, 4096, 0), (8208, 11280, 0),
                   (8192, 8208, SMALL_W - 16), (13328, 13360, SMALL_W - 32))
SHARD_W = IN_WIDTH // N_CHIPS


def _shards_to_padded(shards):
    parts = []
    for first, last, pad in PADDED_SEGMENTS:
        for j in range(N_CHIPS):
            lo, hi = max(first, j * SHARD_W), min(last, (j + 1) * SHARD_W)
            if lo < hi:
                parts.append(shards[j][:, lo - j * SHARD_W:hi - j * SHARD_W])
        if pad:
            parts.append(jnp.zeros((shards.shape[1], pad), shards.dtype))
    return jnp.concatenate(parts, axis=1)


def _padded_to_shards(g):
    offsets, off = {}, 0
    for first, last, pad in PADDED_SEGMENTS:
        offsets[first] = off
        off += last - first + pad
    shards = []
    for j in range(N_CHIPS):
        parts = []
        for first, last, _ in sorted(PADDED_SEGMENTS):
            lo, hi = max(first, j * SHARD_W), min(last, (j + 1) * SHARD_W)
            if lo < hi:
                parts.append(g[:, offsets[first] + lo - first:offsets[first] + hi - first])
        shards.append(jnp.concatenate(parts, axis=1))
    return jnp.stack(shards)


def _sigmoid(x):
    return 1.0 / (1.0 + jnp.exp(-x))


def _silu(x):
    return x * _sigmoid(x)


def _dsilu(x):
    s = _sigmoid(x)
    return s + x * s * (1.0 - s)


def _softplus(x):
    return jnp.maximum(x, 0.0) + jnp.log(1.0 + jnp.exp(-jnp.abs(x)))


def _logsigmoid(x):
    return jnp.minimum(x, 0.0) - jnp.log(1.0 + jnp.exp(-jnp.abs(x)))


def _dot(a, b, dims, precision=None):
    return lax.dot_general(a, b, (dims, ((), ())), preferred_element_type=F32, precision=precision)


def _nn(a, b, precision=None):
    return _dot(a, b, ((1,), (0,)), precision)


def _nt(a, b, precision=None):
    return _dot(a, b, ((1,), (1,)), precision)


def _tn(a, b, precision=None):
    return _dot(a, b, ((0,), (0,)), precision)


def _bf(x):
    return x.astype(BF16)


def _split(x, terms):
    parts = []
    for _ in range(terms):
        part = _bf(x)
        parts.append(part)
        x = x - part.astype(F32)
    return parts


def _pick_right(x, pick, terms):
    pick = _bf(pick)
    out = None
    for part in _split(x, terms):
        out = _nn(part, pick) if out is None else out + _nn(part, pick)
    return out


def _pick_left(pick, x, terms):
    pick = _bf(pick)
    out = None
    for part in _split(x, terms):
        out = _nn(pick, part) if out is None else out + _nn(pick, part)
    return out


def _lane_col(x, lane):
    idx = lax.broadcasted_iota(jnp.int32, x.shape, 1)
    return jnp.sum(jnp.where(idx == lane, x, 0.0), axis=1, keepdims=True)


def _tri(n, upper):
    r = lax.broadcasted_iota(jnp.int32, (n, n), 0)
    c = lax.broadcasted_iota(jnp.int32, (n, n), 1)
    return jnp.where((r <= c) if upper else (r >= c), 1.0, 0.0).astype(F32)


def _eye(n):
    return jnp.where(lax.broadcasted_iota(jnp.int32, (n, n), 0) == lax.broadcasted_iota(jnp.int32, (n, n), 1), 1.0, 0.0)


def _sum_all(x):
    return jnp.sum(jnp.sum(x, axis=1, keepdims=True), axis=0, keepdims=True)


def _crossing(p):
    L = p.shape[0]
    hi = _bf(p)
    lo = _bf(p - hi.astype(F32))
    upper = _bf(_tri(L, True))
    below = _nn(upper, hi) + _nn(upper, lo)
    strict = lax.broadcasted_iota(jnp.int32, (L, L), 0) > lax.broadcasted_iota(jnp.int32, (L, L), 1)
    return jnp.sum(jnp.where(strict, below, 0.0), axis=1, keepdims=True)


def _matmul_bias(a, w, bias, tm, tn):
    m, k = a.shape
    n = w.shape[1]

    def body(a_ref, w_ref, b_ref, o_ref):
        o_ref[...] = _nn(a_ref[...], w_ref[...]) + b_ref[...]

    return pl.pallas_call(
        body, name="matmul_bias", grid=(m // tm, n // tn),
        in_specs=[pl.BlockSpec((tm, k), lambda i, j: (i, 0)), pl.BlockSpec((k, tn), lambda i, j: (0, j)),
                  pl.BlockSpec((1, tn), lambda i, j: (0, j))],
        out_specs=pl.BlockSpec((tm, tn), lambda i, j: (i, j)),
        out_shape=jax.ShapeDtypeStruct((m, n), F32),
        compiler_params=_cparams("parallel", "arbitrary"))(a, w, bias)


def _matmul_nt(a, w, tm, tk):
    m, n = a.shape
    k = w.shape[0]

    def body(a_ref, w_ref, o_ref):
        @pl.when(pl.program_id(1) == 0)
        def _():
            o_ref[...] = jnp.zeros_like(o_ref)
        o_ref[...] += _nt(a_ref[...], w_ref[...])

    return pl.pallas_call(
        body, name="matmul_nt", grid=(m // tm, n // tk),
        in_specs=[pl.BlockSpec((tm, tk), lambda i, j: (i, j)), pl.BlockSpec((k, tk), lambda i, j: (0, j))],
        out_specs=pl.BlockSpec((tm, k), lambda i, j: (i, 0)),
        out_shape=jax.ShapeDtypeStruct((m, k), F32),
        compiler_params=_cparams("parallel", "arbitrary"))(a, w)


def _matmul_tn(a, b, tm, tn, with_colsum=False, a_is_transposed=False):
    k, m = a.shape if a_is_transposed else a.shape[::-1]
    n = b.shape[1]

    def body(a_ref, b_ref, o_ref, *rest):
        first = pl.program_id(1) == 0

        @pl.when(first)
        def _():
            o_ref[...] = jnp.zeros_like(o_ref)
        o_ref[...] += _nn(a_ref[...], b_ref[...]) if a_is_transposed else _tn(a_ref[...], b_ref[...])
        if with_colsum:
            s_ref = rest[0]

            @pl.when(first)
            def _():
                s_ref[...] = jnp.zeros_like(s_ref)
            s_ref[...] += jnp.sum(b_ref[...].astype(F32), axis=0, keepdims=True)

    out_specs = [pl.BlockSpec((k, tn), lambda j, i: (0, j))]
    out_shape = [jax.ShapeDtypeStruct((k, n), F32)]
    if with_colsum:
        out_specs.append(pl.BlockSpec((1, tn), lambda j, i: (0, j)))
        out_shape.append(jax.ShapeDtypeStruct((1, n), F32))
    out = pl.pallas_call(
        body, name="matmul_tn", grid=(n // tn, m // tm),
        in_specs=[pl.BlockSpec((k, tm), lambda j, i: (0, i)) if a_is_transposed else pl.BlockSpec((tm, k), lambda j, i: (i, 0)),
                  pl.BlockSpec((tm, tn), lambda j, i: (i, j))],
        out_specs=out_specs, out_shape=out_shape,
        compiler_params=_cparams("parallel", "arbitrary"))(a, b)
    return out if with_colsum else out[0]


def _ada_fwd(c_all, ada_w, ada_b):
    def body(c_ref, w_ref, b_ref, o_ref):
        o_ref[...] = _nn(_bf(_silu(c_ref[...])), _bf(w_ref[...])) + b_ref[...]

    return pl.pallas_call(body, name="ada_fwd", out_shape=jax.ShapeDtypeStruct((c_all.shape[0], ada_w.shape[1]), F32),
                          compiler_params=_cparams())(c_all, ada_w, ada_b)


def _ada_bwd(c_all, dmod):
    def body(c_ref, d_ref, o_ref):
        o_ref[...] = _tn(_bf(_silu(c_ref[...])), _bf(d_ref[...]))

    return pl.pallas_call(body, name="ada_bwd", out_shape=jax.ShapeDtypeStruct((c_all.shape[1], dmod.shape[1]), F32),
                          compiler_params=_cparams())(c_all, dmod)


def _prenorm_fwd(x, norm_w, scale, shift, ts):
    s, d = x.shape

    def body(x_ref, nw_ref, sc_ref, sh_ref, u_ref):
        xv = x_ref[...]
        r = lax.rsqrt(jnp.mean(xv * xv, axis=1, keepdims=True) + EPS)
        u_ref[...] = _bf(xv * r * nw_ref[...] * (1.0 + sc_ref[...]) + sh_ref[...])

    row = pl.BlockSpec((1, d), lambda i: (0, 0))
    return pl.pallas_call(
        body, name="prenorm_fwd", grid=(s // ts,),
        in_specs=[pl.BlockSpec((ts, d), lambda i: (i, 0)), row, row, row],
        out_specs=pl.BlockSpec((ts, d), lambda i: (i, 0)), out_shape=jax.ShapeDtypeStruct((s, d), BF16),
        compiler_params=_cparams("parallel"))(x, norm_w, scale, shift)


def _prenorm_bwd(du, x, dxres, norm_w, scale, ts):
    s, d = x.shape

    def body(du_ref, x_ref, dr_ref, nw_ref, sc_ref, gx_ref, acc_ref):
        @pl.when(pl.program_id(0) == 0)
        def _():
            acc_ref[...] = jnp.zeros_like(acc_ref)
        xv, duv = x_ref[...], du_ref[...]
        r = lax.rsqrt(jnp.mean(xv * xv, axis=1, keepdims=True) + EPS)
        xn = xv * r
        nw, sc1 = nw_ref[...], 1.0 + sc_ref[...]
        dxn = duv * (nw * sc1)
        gx_ref[...] = r * (dxn - xn * jnp.mean(dxn * xn, axis=1, keepdims=True)) + dr_ref[...]
        t = duv * xn
        acc_ref[0:1, :] += jnp.sum(t, axis=0, keepdims=True) * sc1
        acc_ref[1:2, :] += jnp.sum(t, axis=0, keepdims=True) * nw
        acc_ref[2:3, :] += jnp.sum(duv, axis=0, keepdims=True)

    tile = pl.BlockSpec((ts, d), lambda i: (i, 0))
    row = pl.BlockSpec((1, d), lambda i: (0, 0))
    return pl.pallas_call(
        body, name="prenorm_bwd", grid=(s // ts,),
        in_specs=[tile, tile, tile, row, row],
        out_specs=[tile, pl.BlockSpec((8, d), lambda i: (0, 0))],
        out_shape=[jax.ShapeDtypeStruct((s, d), F32), jax.ShapeDtypeStruct((8, d), F32)],
        compiler_params=_cparams("arbitrary"))(du, x, dxres, norm_w, scale)


CONV_CB = 512


def _conv_taps(buf_ref, ts):
    return [buf_ref[pl.ds(8 - (CONV_K - 1) + j, ts), :] for j in range(CONV_K)]


def _conv_fwd(proj, col0, width, w8, b, ts):
    s = proj.shape[0]
    cb = CONV_CB
    nt = s // ts

    def body(x_ref, w_ref, b_ref, o_ref, buf_ref):
        @pl.when(pl.program_id(1) == 0)
        def _():
            buf_ref[0:8, :] = jnp.zeros((8, cb), F32)
        buf_ref[pl.ds(8, ts), :] = x_ref[...]
        acc = b_ref[...] + jnp.zeros((ts, cb), F32)
        for j, tap in enumerate(_conv_taps(buf_ref, ts)):
            acc = acc + tap * w_ref[j:j + 1, :]
        o_ref[...] = _silu(acc)
        buf_ref[0:8, :] = x_ref[pl.ds(ts - 8, 8), :]

    c0 = col0 // cb
    return pl.pallas_call(
        body, name="conv_fwd", grid=(width // cb, nt),
        in_specs=[pl.BlockSpec((ts, cb), lambda c, i: (i, c0 + c)), pl.BlockSpec((8, cb), lambda c, i: (0, c)),
                  pl.BlockSpec((1, cb), lambda c, i: (0, c))],
        out_specs=pl.BlockSpec((ts, cb), lambda c, i: (i, c)),
        out_shape=jax.ShapeDtypeStruct((s, width), F32),
        scratch_shapes=[pltpu.VMEM((ts + 8, cb), F32)],
        compiler_params=_cparams("parallel", "arbitrary"))(proj, w8, b)


def _conv_bwd(proj, col0, width, w8, b, dpost, dproj, ts):
    s = proj.shape[0]
    cb = CONV_CB
    nt = s // ts
    c0 = col0 // cb

    def body(x_ref, xh_ref, dp_ref, w_ref, b_ref, _, dx_ref, acc_ref, buf_ref, dbuf_ref):
        step = pl.program_id(1)
        tile = nt - 1 - step

        @pl.when(step == 0)
        def _():
            acc_ref[...] = jnp.zeros_like(acc_ref)
            dbuf_ref[pl.ds(ts, 8), :] = jnp.zeros((8, cb), F32)
        buf_ref[0:8, :] = jnp.where(tile == 0, 0.0, xh_ref[...])
        buf_ref[pl.ds(8, ts), :] = x_ref[...]
        taps = _conv_taps(buf_ref, ts)
        acc = b_ref[...] + jnp.zeros((ts, cb), F32)
        for j in range(CONV_K):
            acc = acc + taps[j] * w_ref[j:j + 1, :]
        dconv = dp_ref[...] * _dsilu(acc)
        acc_ref[4:5, :] += jnp.sum(dconv, axis=0, keepdims=True)
        for j in range(CONV_K):
            acc_ref[j:j + 1, :] += jnp.sum(taps[j] * dconv, axis=0, keepdims=True)
        dbuf_ref[pl.ds(0, ts), :] = dconv
        dx = jnp.zeros((ts, cb), F32)
        for j in range(CONV_K):
            dx = dx + dbuf_ref[pl.ds(CONV_K - 1 - j, ts), :] * w_ref[j:j + 1, :]
        dx_ref[...] = _bf(dx)
        dbuf_ref[pl.ds(ts, 8), :] = dconv[0:8, :]

    rows8 = ts // 8
    return pl.pallas_call(
        body, name="conv_bwd", grid=(width // cb, nt),
        in_specs=[pl.BlockSpec((ts, cb), lambda c, i: (nt - 1 - i, c0 + c)),
                  pl.BlockSpec((8, cb), lambda c, i: (jnp.maximum((nt - 1 - i) * rows8 - 1, 0), c0 + c)),
                  pl.BlockSpec((ts, cb), lambda c, i: (nt - 1 - i, c)),
                  pl.BlockSpec((8, cb), lambda c, i: (0, c)), pl.BlockSpec((1, cb), lambda c, i: (0, c)),
                  pl.BlockSpec(memory_space=pl.ANY)],
        out_specs=[pl.BlockSpec((ts, cb), lambda c, i: (nt - 1 - i, c0 + c)), pl.BlockSpec((8, cb), lambda c, i: (0, c))],
        out_shape=[jax.ShapeDtypeStruct(dproj.shape, dproj.dtype), jax.ShapeDtypeStruct((8, width), F32)],
        input_output_aliases={5: 0},
        scratch_shapes=[pltpu.VMEM((ts + 8, cb), F32), pltpu.VMEM((ts + 8, cb), F32)],
        compiler_params=_cparams("parallel", "arbitrary"))(proj, proj, dpost, w8, b, dproj)


def _mlstm_gates(gif_ref, gt_ref, a_scr, at_scr):
    L = gif_ref.shape[0]
    fb = _logsigmoid(gif_ref[...])
    a_scr[...] = _pick_left(_tri(L, False), fb, 3)
    at_scr[...] = _pick_right(_logsigmoid(gt_ref[...]), _tri(L, True), 3)
    return jnp.sum(fb, axis=0, keepdims=True)


def _mlstm_head(h, qk_ref, v_ref, gif, gt_ref, a, at_scr, a_last_row, c_mat, n_row, m_prev):
    L = gif.shape[0]
    q = qk_ref[:, h * ML_DQK:(h + 1) * ML_DQK] * (ML_DQK ** -0.5)
    k = qk_ref[:, (ML_HEADS + h) * ML_DQK:(ML_HEADS + h + 1) * ML_DQK]
    v = v_ref[:, h * ML_DV:(h + 1) * ML_DV]
    i_col, a_col = _lane_col(gif, h), _lane_col(a, ML_HEADS + h)
    i_row, a_row = gt_ref[h:h + 1, :], at_scr[ML_HEADS + h:ML_HEADS + h + 1, :]
    causal = lax.broadcasted_iota(jnp.int32, (L, L), 0) >= lax.broadcasted_iota(jnp.int32, (L, L), 1)
    dmat = jnp.where(causal, a_col - a_row + i_row, NEG)
    inter = a_col + m_prev
    m_t = jnp.maximum(inter, jnp.max(dmat, axis=1, keepdims=True))
    w_intra = jnp.exp(dmat - m_t)
    w_inter = jnp.exp(inter - m_t)
    sc = _nt(_bf(q), _bf(k)) * w_intra
    den = jnp.sum(sc, axis=1, keepdims=True) + w_inter * jnp.sum(q * n_row, axis=1, keepdims=True)
    floor = jnp.exp(-m_t)
    a_last = _lane_col(a_last_row, ML_HEADS + h)
    g = a_last - a_col + i_col
    m_new = jnp.maximum(a_last + m_prev, jnp.max(g, axis=0, keepdims=True))
    wk = jnp.exp(g - m_new)
    decay = jnp.exp(a_last + m_prev - m_new)
    return dict(q=q, k=k, v=v, w_intra=w_intra, w_inter=w_inter, sc=sc, den=den, floor=floor, m_new=m_new, wk=wk,
                decay=decay)


def _state_tile(n_row, m11):
    r = lax.broadcasted_iota(jnp.int32, (8, LANE), 0)
    return jnp.where(r == 0, n_row, jnp.where(r == 1, m11, 0.0))


def _mlstm_fwd(qk, proj, gt):
    s = qk.shape[0]
    L = CHUNK
    nc = s // L

    def body(qk_ref, v_ref, gif_ref, gt_ref, h_ref, cst_ref, nm_ref, c_scr, nm_scr, a_scr, at_scr):
        @pl.when(pl.program_id(0) == 0)
        def _():
            c_scr[...] = jnp.zeros_like(c_scr)
            nm_scr[...] = jnp.zeros_like(nm_scr)
        a_last_row = _mlstm_gates(gif_ref, gt_ref, a_scr, at_scr)
        gif, a = gif_ref[...], a_scr[...]
        for h in range(ML_HEADS):
            c_mat, n_row = c_scr[h], nm_scr[h, 0:1, :]
            m_prev = jnp.max(nm_scr[h, 1:2, :], axis=1, keepdims=True)
            cst_ref[0, h] = c_mat
            nm_ref[0, h] = nm_scr[h]
            t = _mlstm_head(h, qk_ref, v_ref, gif, gt_ref, a, at_scr, a_last_row, c_mat, n_row, m_prev)
            num = _nn(_bf(t["sc"]), _bf(t["v"])) + t["w_inter"] * _nn(_bf(t["q"]), _bf(c_mat))
            h_ref[:, h * ML_DV:(h + 1) * ML_DV] = num / jnp.maximum(jnp.abs(t["den"]), t["floor"])
            kw = t["k"] * t["wk"]
            c_scr[h] = t["decay"] * c_mat + _tn(_bf(kw), _bf(t["v"]))
            nm_scr[h] = _state_tile(t["decay"] * n_row + jnp.sum(kw, axis=0, keepdims=True), t["m_new"])

    return pl.pallas_call(
        body, name="mlstm_fwd", grid=(nc,),
        in_specs=[pl.BlockSpec((L, 2048), lambda c: (c, 0)), pl.BlockSpec((L, 2048), lambda c: (c, O_V // 2048)),
                  pl.BlockSpec((L, LANE), lambda c: (c, O_IF // LANE)), pl.BlockSpec((LANE, L), lambda c: (0, c))],
        out_specs=[pl.BlockSpec((L, 2048), lambda c: (c, 0)),
                   pl.BlockSpec((1, ML_HEADS, ML_DQK, ML_DV), lambda c: (c, 0, 0, 0)),
                   pl.BlockSpec((1, ML_HEADS, 8, LANE), lambda c: (c, 0, 0, 0))],
        out_shape=[jax.ShapeDtypeStruct((s, 2048), F32), jax.ShapeDtypeStruct((nc, ML_HEADS, ML_DQK, ML_DV), F32),
                   jax.ShapeDtypeStruct((nc, ML_HEADS, 8, LANE), F32)],
        scratch_shapes=[pltpu.VMEM((ML_HEADS, ML_DQK, ML_DV), F32), pltpu.VMEM((ML_HEADS, 8, LANE), F32),
                        pltpu.VMEM((L, LANE), F32), pltpu.VMEM((LANE, L), F32)],
        compiler_params=_cparams("arbitrary"))(qk, proj, proj, gt)


def _mlstm_bwd(qk, proj, gt, hout, dh, cst, nm, dproj):
    s = qk.shape[0]
    L = CHUNK
    nc = s // L

    def body(qk_ref, v_ref, gif_ref, gt_ref, h_ref, dh_ref, cst_ref, nm_ref, _, dqk_ref, dv_ref, dif_ref,
             dc_scr, dn_scr, a_scr, at_scr):
        @pl.when(pl.program_id(0) == 0)
        def _():
            dc_scr[...] = jnp.zeros_like(dc_scr)
            dn_scr[...] = jnp.zeros_like(dn_scr)
        a_last_row = _mlstm_gates(gif_ref, gt_ref, a_scr, at_scr)
        gif, a = gif_ref[...], a_scr[...]
        lane = lax.broadcasted_iota(jnp.int32, (L, LANE), 1)
        last = lax.broadcasted_iota(jnp.int32, (L, 1), 0) == L - 1
        di_tile = jnp.zeros((L, LANE), F32)
        cross = [jnp.zeros((L, LANE), F32)] * 3
        for h in range(ML_HEADS):
            c_mat, n_row = cst_ref[0, h], nm_ref[0, h, 0:1, :]
            m_prev = jnp.max(nm_ref[0, h, 1:2, :], axis=1, keepdims=True)
            t = _mlstm_head(h, qk_ref, v_ref, gif, gt_ref, a, at_scr, a_last_row, c_mat, n_row, m_prev)
            q, k, v, den = t["q"], t["k"], t["v"], t["den"]
            dhh = dh_ref[:, h * ML_DV:(h + 1) * ML_DV]
            hh = h_ref[:, h * ML_DV:(h + 1) * ML_DV]
            dnorm = jnp.maximum(jnp.abs(den), t["floor"])
            dnum = dhh / dnorm
            d_dn = -jnp.sum(dhh * hh, axis=1, keepdims=True) / dnorm
            dden = jnp.where(jnp.abs(den) >= t["floor"], jnp.where(den >= 0.0, d_dn, -d_dn), 0.0)
            dsc = _nt(_bf(dnum), _bf(v)) + dden
            ds = dsc * t["w_intra"]
            dq_inter = t["w_inter"] * (_nt(_bf(dnum), _bf(c_mat)) + dden * n_row)
            dq = _nn(_bf(ds), _bf(k)) + dq_inter
            dc, dn_row = dc_scr[h], dn_scr[h, 0:1, :]
            dk_state = t["wk"] * (_nt(_bf(v), _bf(dc)) + dn_row)
            dk = _tn(_bf(ds), _bf(q)) + dk_state
            dv = _tn(_bf(t["sc"]), _bf(dnum)) + t["wk"] * _nn(_bf(k), _bf(dc))
            qi = q * t["w_inter"]
            dc_scr[h] = t["decay"] * dc + _tn(_bf(qi), _bf(dnum))
            dn_scr[h] = jnp.broadcast_to(t["decay"] * dn_row + jnp.sum(qi * dden, axis=0, keepdims=True), (8, LANE))
            dqk_ref[:, h * ML_DQK:(h + 1) * ML_DQK] = dq * (ML_DQK ** -0.5)
            dqk_ref[:, (ML_HEADS + h) * ML_DQK:(ML_HEADS + h + 1) * ML_DQK] = dk
            dv_ref[:, h * ML_DV:(h + 1) * ML_DV] = _bf(dv)
            di_tile = di_tile + jnp.where(lane == h, jnp.sum(k * dk, axis=1, keepdims=True), 0.0)
            carried = t["decay"] * (_sum_all(dc * c_mat) + jnp.sum(dn_row * n_row, axis=1, keepdims=True))
            parts = (_crossing(dsc * t["sc"]),
                     jnp.sum(q * dq_inter, axis=1, keepdims=True) + jnp.where(last, carried, 0.0),
                     jnp.sum(k * dk_state, axis=1, keepdims=True))
            cross = [c + jnp.where(lane == ML_HEADS + h, p, 0.0) for c, p in zip(cross, parts)]
        dfb = cross[0] + _pick_left(_tri(L, True), cross[1], 2) + _pick_left(_tri(L, False) - _eye(L), cross[2], 2)
        dif_ref[:, 0:LANE] = _bf(di_tile + dfb * _sigmoid(-gif))
        dif_ref[:, LANE:SMALL_W] = jnp.zeros((L, SMALL_W - LANE), BF16)

    rev = lambda c: nc - 1 - c
    return pl.pallas_call(
        body, name="mlstm_bwd", grid=(nc,),
        in_specs=[pl.BlockSpec((L, 2048), lambda c: (rev(c), 0)), pl.BlockSpec((L, 2048), lambda c: (rev(c), O_V // 2048)),
                  pl.BlockSpec((L, LANE), lambda c: (rev(c), O_IF // LANE)), pl.BlockSpec((LANE, L), lambda c: (0, rev(c))),
                  pl.BlockSpec((L, 2048), lambda c: (rev(c), 0)), pl.BlockSpec((L, 2048), lambda c: (rev(c), 0)),
                  pl.BlockSpec((1, ML_HEADS, ML_DQK, ML_DV), lambda c: (rev(c), 0, 0, 0)),
                  pl.BlockSpec((1, ML_HEADS, 8, LANE), lambda c: (rev(c), 0, 0, 0)), pl.BlockSpec(memory_space=pl.ANY)],
        out_specs=[pl.BlockSpec((L, 2048), lambda c: (rev(c), 0)), pl.BlockSpec((L, 2048), lambda c: (rev(c), O_V // 2048)),
                   pl.BlockSpec((L, SMALL_W), lambda c: (rev(c), 0))],
        out_shape=[jax.ShapeDtypeStruct((s, 2048), F32), jax.ShapeDtypeStruct(dproj.shape, dproj.dtype),
                   jax.ShapeDtypeStruct((s, SMALL_W), BF16)],
        input_output_aliases={8: 1},
        scratch_shapes=[pltpu.VMEM((ML_HEADS, ML_DQK, ML_DV), F32), pltpu.VMEM((ML_HEADS, 8, LANE), F32),
                        pltpu.VMEM((L, LANE), F32), pltpu.VMEM((LANE, L), F32)],
        compiler_params=_cparams("arbitrary"))(qk, proj, proj, gt, hout, dh, cst, nm, dproj)


GROUP_W = SSM_HEADS // SSM_GROUPS * SSM_HEADDIM
O_B = SSM_HEADS * SSM_HEADDIM
O_C = O_B + SSM_GROUPS * SSM_STATE


def _head_expand():
    r = jnp.arange(LANE)[:, None]
    c = jnp.arange(SSM_HEADS * SSM_HEADDIM)[None, :] // SSM_HEADDIM
    return (r == c).astype(F32)


def _ssd_gates(dt_ref, dtt_ref, alog_row_ref, alog_col_ref, at_scr):
    L = dt_ref.shape[0]
    dt = _softplus(dt_ref[...])
    acoef = -jnp.exp(alog_row_ref[...])
    a = _pick_left(_tri(L, False), dt * acoef, 3)
    at_scr[...] = _pick_right(_softplus(dtt_ref[...]) * (-jnp.exp(alog_col_ref[...])), _tri(L, True), 3)
    return dt, acoef, a


def _ssd_group(g, xbc_ref, dt, a, e_ref, ax_scr):
    eg = e_ref[:, g * GROUP_W:(g + 1) * GROUP_W]
    ax_scr[...] = _pick_right(a, eg, 3)
    ax = ax_scr[...]
    alx = ax_scr[ax.shape[0] - 1:ax.shape[0], :]
    dtx = _pick_right(dt, eg, 2)
    xg = xbc_ref[:, g * GROUP_W:(g + 1) * GROUP_W]
    bg = xbc_ref[:, O_B + g * SSM_STATE:O_B + (g + 1) * SSM_STATE]
    cg = xbc_ref[:, O_C + g * SSM_STATE:O_C + (g + 1) * SSM_STATE]
    return dict(ax=ax, alx=alx, dtx=dtx, xg=xg, bg=bg, cg=cg, xdt=xg * dtx, gmat=_nt(_bf(cg), _bf(bg)))


def _ssd_decay(hh, a, at_scr):
    L = a.shape[0]
    causal = lax.broadcasted_iota(jnp.int32, (L, L), 0) >= lax.broadcasted_iota(jnp.int32, (L, L), 1)
    return jnp.exp(jnp.where(causal, _lane_col(a, hh) - at_scr[hh:hh + 1, :], NEG))


def _ssd_fwd(xbc, proj, dtt, alog_row, alog_col, dskip_x, expand):
    s = xbc.shape[0]
    L = CHUNK
    nc = s // L
    half = SSM_HEADDIM

    def body(xbc_ref, dt_ref, dtt_ref, ar_ref, ac_ref, dk_ref, e_ref, y_ref, st_ref, st_scr, at_scr, ax_scr):
        @pl.when(pl.program_id(0) == 0)
        def _():
            st_scr[...] = jnp.zeros_like(st_scr)
        dt, _, a = _ssd_gates(dt_ref, dtt_ref, ar_ref, ac_ref, at_scr)
        lane = lax.broadcasted_iota(jnp.int32, (L, LANE), 1)
        for g in range(SSM_GROUPS):
            t = _ssd_group(g, xbc_ref, dt, a, e_ref, ax_scr)
            st = st_scr[g]
            st_ref[0, g] = st
            pairs = []
            for j in range(GROUP_W // LANE):
                xp = _bf(t["xdt"][:, j * LANE:(j + 1) * LANE])
                hh = g * (SSM_HEADS // SSM_GROUPS) + 2 * j
                y0 = _nn(_bf(t["gmat"] * _ssd_decay(hh, a, at_scr)), xp)
                y1 = _nn(_bf(t["gmat"] * _ssd_decay(hh + 1, a, at_scr)), xp)
                pairs.append(jnp.where(lane < half, y0, y1))
            y = jnp.concatenate(pairs, axis=1) + _nn(_bf(t["cg"]), _bf(st)) * jnp.exp(t["ax"])
            y_ref[:, g * GROUP_W:(g + 1) * GROUP_W] = y + dk_ref[:, g * GROUP_W:(g + 1) * GROUP_W] * t["xg"]
            wts = jnp.exp(t["alx"] - t["ax"])
            st_scr[g] = jnp.exp(t["alx"]) * st + _tn(_bf(t["bg"]), _bf(t["xdt"] * wts))

    row = lambda w: pl.BlockSpec((1, w), lambda c: (0, 0))
    return pl.pallas_call(
        body, name="ssd_fwd", grid=(nc,),
        in_specs=[pl.BlockSpec((L, 3072), lambda c: (c, 0)), pl.BlockSpec((L, LANE), lambda c: (c, O_DT // LANE)),
                  pl.BlockSpec((LANE, L), lambda c: (0, c)), row(LANE), pl.BlockSpec((LANE, 1), lambda c: (0, 0)),
                  row(2048), pl.BlockSpec((LANE, 2048), lambda c: (0, 0))],
        out_specs=[pl.BlockSpec((L, 2048), lambda c: (c, 0)),
                   pl.BlockSpec((1, SSM_GROUPS, SSM_STATE, GROUP_W), lambda c: (c, 0, 0, 0))],
        out_shape=[jax.ShapeDtypeStruct((s, 2048), F32),
                   jax.ShapeDtypeStruct((nc, SSM_GROUPS, SSM_STATE, GROUP_W), F32)],
        scratch_shapes=[pltpu.VMEM((SSM_GROUPS, SSM_STATE, GROUP_W), F32), pltpu.VMEM((LANE, L), F32),
                        pltpu.VMEM((L, GROUP_W), F32)],
        compiler_params=_cparams("arbitrary"))(xbc, proj, dtt, alog_row, alog_col, dskip_x, expand)


def _ssd_bwd(xbc, proj, dtt, alog_row, alog_col, dskip_x, expand, expand_t, dy, states):
    s = xbc.shape[0]
    L = CHUNK
    nc = s // L
    half = SSM_HEADDIM

    def body(xbc_ref, dt_ref, dtt_ref, ar_ref, ac_ref, dk_ref, e_ref, et_ref, dy_ref, st_ref,
             dxbc_ref, ddt_ref, accd_ref, acca_ref, dst_scr, at_scr, ax_scr):
        @pl.when(pl.program_id(0) == 0)
        def _():
            dst_scr[...] = jnp.zeros_like(dst_scr)
            accd_ref[...] = jnp.zeros_like(accd_ref)
            acca_ref[...] = jnp.zeros_like(acca_ref)
        dt, acoef, a = _ssd_gates(dt_ref, dtt_ref, ar_ref, ac_ref, at_scr)
        lane = lax.broadcasted_iota(jnp.int32, (L, LANE), 1)
        low = lane < half
        last = lax.broadcasted_iota(jnp.int32, (L, 1), 0) == L - 1
        cross = [jnp.zeros((L, LANE), F32)] * 3
        ddt_tile = jnp.zeros((L, LANE), F32)
        for g in range(SSM_GROUPS):
            t = _ssd_group(g, xbc_ref, dt, a, e_ref, ax_scr)
            xg, bg, cg, xdt, gmat = t["xg"], t["bg"], t["cg"], t["xdt"], t["gmat"]
            st, dst = st_ref[0, g], dst_scr[g]
            dyg = dy_ref[:, g * GROUP_W:(g + 1) * GROUP_W]
            ea, eal = jnp.exp(t["ax"]), jnp.exp(t["alx"])
            wts = jnp.exp(t["alx"] - t["ax"])
            dyi = dyg * ea
            y_inter = _nn(_bf(cg), _bf(st)) * ea
            dc = _nt(_bf(dyi), _bf(st))
            d_xdt_state = _nn(_bf(bg), _bf(dst)) * wts
            db = _nt(_bf(xdt * wts), _bf(dst))
            dst_scr[g] = eal * dst + _tn(_bf(cg), _bf(dyi))
            dg = jnp.zeros((L, L), F32)
            dx_pairs = []
            for j in range(GROUP_W // LANE):
                xp = _bf(xdt[:, j * LANE:(j + 1) * LANE])
                dyp = dyg[:, j * LANE:(j + 1) * LANE]
                dxs = []
                for b in range(2):
                    hh = g * (SSM_HEADS // SSM_GROUPS) + 2 * j + b
                    dec = _ssd_decay(hh, a, at_scr)
                    w = gmat * dec
                    dxs.append(_tn(_bf(w), _bf(dyp)))
                    dw = _nt(_bf(jnp.where(low if b == 0 else ~low, dyp, 0.0)), xp)
                    dg = dg + dw * dec
                    cross[0] = cross[0] + jnp.where(lane == hh, _crossing(dw * w), 0.0)
                dx_pairs.append(jnp.where(low, dxs[0], dxs[1]))
            d_xdt = d_xdt_state + jnp.concatenate(dx_pairs, axis=1)
            dc = dc + _nn(_bf(dg), _bf(bg))
            db = db + _tn(_bf(dg), _bf(cg))
            etg = et_ref[g * GROUP_W:(g + 1) * GROUP_W, :]
            carried = jnp.sum(dst * st, axis=0, keepdims=True) * eal
            cross[1] = cross[1] + _pick_right(dyg * y_inter + jnp.where(last, carried, 0.0), etg, 2)
            cross[2] = cross[2] + _pick_right(xdt * d_xdt_state, etg, 2)
            ddt_tile = ddt_tile + _pick_right(d_xdt * xg, etg, 2)
            dxbc_ref[:, g * GROUP_W:(g + 1) * GROUP_W] = d_xdt * t["dtx"] + dk_ref[:, g * GROUP_W:(g + 1) * GROUP_W] * dyg
            dxbc_ref[:, O_B + g * SSM_STATE:O_B + (g + 1) * SSM_STATE] = db
            dxbc_ref[:, O_C + g * SSM_STATE:O_C + (g + 1) * SSM_STATE] = dc
            accd_ref[0:1, g * GROUP_W:(g + 1) * GROUP_W] += jnp.sum(dyg * xg, axis=0, keepdims=True)
        d_da = cross[0] + _pick_left(_tri(L, True), cross[1], 2) + _pick_left(_tri(L, False) - _eye(L), cross[2], 2)
        acca_ref[0:1, :] += jnp.sum(d_da * dt, axis=0, keepdims=True)
        ddt_ref[:, 0:LANE] = _bf((ddt_tile + d_da * acoef) * _sigmoid(dt_ref[...]))
        ddt_ref[:, LANE:SMALL_W] = jnp.zeros((L, SMALL_W - LANE), BF16)

    rev = lambda c: nc - 1 - c
    row = lambda w: pl.BlockSpec((1, w), lambda c: (0, 0))
    return pl.pallas_call(
        body, name="ssd_bwd", grid=(nc,),
        in_specs=[pl.BlockSpec((L, 3072), lambda c: (rev(c), 0)), pl.BlockSpec((L, LANE), lambda c: (rev(c), O_DT // LANE)),
                  pl.BlockSpec((LANE, L), lambda c: (0, rev(c))), row(LANE), pl.BlockSpec((LANE, 1), lambda c: (0, 0)),
                  row(2048), pl.BlockSpec((LANE, 2048), lambda c: (0, 0)), pl.BlockSpec((2048, LANE), lambda c: (0, 0)),
                  pl.BlockSpec((L, 2048), lambda c: (rev(c), 0)),
                  pl.BlockSpec((1, SSM_GROUPS, SSM_STATE, GROUP_W), lambda c: (rev(c), 0, 0, 0))],
        out_specs=[pl.BlockSpec((L, 3072), lambda c: (rev(c), 0)), pl.BlockSpec((L, SMALL_W), lambda c: (rev(c), 0)),
                   pl.BlockSpec((8, 2048), lambda c: (0, 0)), pl.BlockSpec((8, LANE), lambda c: (0, 0))],
        out_shape=[jax.ShapeDtypeStruct((s, 3072), F32), jax.ShapeDtypeStruct((s, SMALL_W), BF16),
                   jax.ShapeDtypeStruct((8, 2048), F32), jax.ShapeDtypeStruct((8, LANE), F32)],
        scratch_shapes=[pltpu.VMEM((SSM_GROUPS, SSM_STATE, GROUP_W), F32),
                        pltpu.VMEM((LANE, L), F32), pltpu.VMEM((L, GROUP_W), F32)],
        compiler_params=_cparams("arbitrary"))(xbc, proj, dtt, alog_row, alog_col, dskip_x, expand, expand_t, dy, states)


def _group_norm(v, width):
    outs, rs = [], []
    for k in range(v.shape[1] // width):
        blk = v[:, k * width:(k + 1) * width]
        r = lax.rsqrt(jnp.mean(blk * blk, axis=1, keepdims=True) + EPS)
        outs.append(blk * r)
        rs.append(jnp.broadcast_to(r, blk.shape))
    return jnp.concatenate(outs, axis=1), jnp.concatenate(rs, axis=1)


def _group_mean(v, width):
    return jnp.concatenate([jnp.broadcast_to(jnp.mean(v[:, k * width:(k + 1) * width], axis=1, keepdims=True),
                                             (v.shape[0], width)) for k in range(v.shape[1] // width)], axis=1)


def _post_fwd(hm, yssd, proj, ml_norm_w, ssm_norm_w, ts):
    s = hm.shape[0]

    def body(h_ref, ys_ref, o_ref, zm_ref, zs_ref, wm_ref, ws_ref, ym_ref, yso_ref):
        hn, _ = _group_norm(h_ref[...], ML_DV)
        ym_ref[...] = _bf(_sigmoid(o_ref[...]) * hn * wm_ref[...] * _silu(zm_ref[...]))
        pn, _ = _group_norm(ys_ref[...] * _silu(zs_ref[...]), GROUP_W)
        yso_ref[...] = _bf(pn * ws_ref[...])

    tile = pl.BlockSpec((ts, 2048), lambda i: (i, 0))
    col = lambda off: pl.BlockSpec((ts, 2048), lambda i: (i, off // 2048))
    row = pl.BlockSpec((1, 2048), lambda i: (0, 0))
    return pl.pallas_call(
        body, name="post_fwd", grid=(s // ts,),
        in_specs=[tile, tile, col(O_O), col(O_ZM), col(O_ZS), row, row],
        out_specs=[tile, tile],
        out_shape=[jax.ShapeDtypeStruct((s, 2048), BF16)] * 2,
        compiler_params=_cparams("parallel"))(hm, yssd, proj, proj, proj, ml_norm_w, ssm_norm_w)


def _post_bwd(dym, dys, hm, yssd, proj, ml_norm_w, ssm_norm_w, dproj, ts):
    s = hm.shape[0]

    def body(dym_ref, dys_ref, h_ref, ys_ref, o_ref, zm_ref, zs_ref, wm_ref, ws_ref, _,
             dh_ref, dyssd_ref, dp_ref, acc_ref):
        @pl.when(pl.program_id(0) == 0)
        def _():
            acc_ref[...] = jnp.zeros_like(acc_ref)
        hn, r = _group_norm(h_ref[...], ML_DV)
        so, zm, wm, d_ym = _sigmoid(o_ref[...]), zm_ref[...], wm_ref[...], dym_ref[...]
        sz = _silu(zm)
        hnw = hn * wm
        dp_ref[:, O_O:O_O + 2048] = _bf(d_ym * hnw * sz * so * (1.0 - so))
        dp_ref[:, O_ZM:O_ZM + 2048] = _bf(d_ym * so * hnw * _dsilu(zm))
        dhnw = d_ym * so * sz
        acc_ref[0:1, :] += jnp.sum(dhnw * hn, axis=0, keepdims=True)
        dhn = dhnw * wm
        dh_ref[...] = r * (dhn - hn * _group_mean(dhn * hn, ML_DV))
        ysv, zs, d_ys = ys_ref[...], zs_ref[...], dys_ref[...]
        szs = _silu(zs)
        pn, r2 = _group_norm(ysv * szs, GROUP_W)
        acc_ref[1:2, :] += jnp.sum(d_ys * pn, axis=0, keepdims=True)
        dpn = d_ys * ws_ref[...]
        dp = r2 * (dpn - pn * _group_mean(dpn * pn, GROUP_W))
        dyssd_ref[...] = dp * szs
        dp_ref[:, O_ZS:O_ZS + 2048] = _bf(dp * ysv * _dsilu(zs))

    tile = pl.BlockSpec((ts, 2048), lambda i: (i, 0))
    col = lambda off: pl.BlockSpec((ts, 2048), lambda i: (i, off // 2048))
    row = pl.BlockSpec((1, 2048), lambda i: (0, 0))
    sds = lambda dt: jax.ShapeDtypeStruct((s, 2048), dt)
    return pl.pallas_call(
        body, name="post_bwd", grid=(s // ts,),
        in_specs=[tile, tile, tile, tile, col(O_O), col(O_ZM), col(O_ZS), row, row, pl.BlockSpec(memory_space=pl.ANY)],
        out_specs=[tile, tile, pl.BlockSpec((ts, O_MG), lambda i: (i, 0)), pl.BlockSpec((8, 2048), lambda i: (0, 0))],
        out_shape=[sds(F32), sds(F32), jax.ShapeDtypeStruct(dproj.shape, dproj.dtype), jax.ShapeDtypeStruct((8, 2048), F32)],
        input_output_aliases={9: 2},
        compiler_params=_cparams("arbitrary"))(dym, dys, hm, yssd, proj, proj, proj, ml_norm_w, ssm_norm_w, dproj)


def _merge(x, ym, ys, proj, target, gate, final_w, wpm, wps, wo, ts):
    wpm_t, wps_t, wo_t = wpm.T, wps.T, wo.T
    s, d = x.shape

    def body(x_ref, ym_ref, ys_ref, mg_ref, t_ref, gate_ref, fw_ref, wpm_ref, wps_ref, wo_ref, wpmt_ref, wpst_ref, wot_ref,
             dres_ref, mer_ref, dmo_ref, dpm_ref, dps_ref, dym_ref, dys_ref, dmg_ref, acc_ref):
        @pl.when(pl.program_id(0) == 0)
        def _():
            acc_ref[...] = jnp.zeros_like(acc_ref)
        gm, gs = _sigmoid(mg_ref[:, 0:d]), _sigmoid(mg_ref[:, d:2 * d])
        pm = _nn(ym_ref[...], wpm_ref[...])
        ps = _nn(ys_ref[...], wps_ref[...])
        merged = _bf(gm * pm + gs * ps)
        mer_ref[...] = merged
        mo = _nn(merged, wo_ref[...])
        gate, fw = gate_ref[...], fw_ref[...]
        out = x_ref[...] + gate * mo
        r = lax.rsqrt(jnp.mean(out * out, axis=1, keepdims=True) + EPS)
        on = out * r
        diff = on * fw - t_ref[...]
        acc_ref[0:1, :] += jnp.sum(0.5 * jnp.sum(diff * diff, axis=1, keepdims=True) / d, axis=0, keepdims=True)
        dyv = diff * (1.0 / d)
        acc_ref[1:2, :] += jnp.sum(dyv * on, axis=0, keepdims=True)
        don = dyv * fw
        dout = r * (don - on * jnp.mean(don * on, axis=1, keepdims=True))
        dres_ref[...] = dout
        acc_ref[2:3, :] += jnp.sum(dout * mo, axis=0, keepdims=True)
        dmo = _bf(dout * gate)
        dmo_ref[...] = dmo
        dmer = _nn(dmo, wot_ref[...])
        dpm, dps = _bf(dmer * gm), _bf(dmer * gs)
        dpm_ref[...] = dpm
        dps_ref[...] = dps
        dmg_ref[:, 0:d] = _bf(dmer * pm * gm * (1.0 - gm))
        dmg_ref[:, d:2 * d] = _bf(dmer * ps * gs * (1.0 - gs))
        dym_ref[...] = _nn(dpm, wpmt_ref[...])
        dys_ref[...] = _nn(dps, wpst_ref[...])

    t1 = pl.BlockSpec((ts, d), lambda i: (i, 0))
    t2 = pl.BlockSpec((ts, 2 * d), lambda i: (i, 0))
    row = pl.BlockSpec((1, d), lambda i: (0, 0))
    whole = pl.BlockSpec(memory_space=pltpu.VMEM)
    sd = lambda w, dt: jax.ShapeDtypeStruct((s, w), dt)
    return pl.pallas_call(
        body, name="merge_fwd_bwd", grid=(s // ts,),
        in_specs=[t1, t2, t2, pl.BlockSpec((ts, 2 * d), lambda i: (i, O_MG // (2 * d))), t1, row, row] + [whole] * 6,
        out_specs=[t1, t1, t1, t1, t1, t2, t2, pl.BlockSpec((ts, 2 * d), lambda i: (i, O_MG // (2 * d))),
                   pl.BlockSpec((8, d), lambda i: (0, 0))],
        out_shape=[sd(d, F32), sd(d, BF16), sd(d, BF16), sd(d, BF16), sd(d, BF16), sd(2 * d, F32), sd(2 * d, F32),
                   sd(NP, BF16), jax.ShapeDtypeStruct((8, d), F32)],
        compiler_params=_cparams("arbitrary"))(x, ym, ys, proj, target, gate, final_w, wpm, wps, wo, wpm_t, wps_t, wo_t)


def _adamw(w, g, m, v, tr):
    rows, cols = w.shape[-2:]
    lead = (None,) * (w.ndim - 2)

    def body(w_ref, g_ref, m_ref, v_ref, d_ref, nm_ref, nv_ref):
        gv = g_ref[...]
        m2 = ADAM_B1 * m_ref[...] + (1.0 - ADAM_B1) * gv
        v2 = ADAM_B2 * v_ref[...] + (1.0 - ADAM_B2) * (gv * gv)
        m_hat = m2 / (1.0 - ADAM_B1 ** ADAM_STEP)
        v_hat = v2 / (1.0 - ADAM_B2 ** ADAM_STEP)
        d_ref[...] = -ADAM_LR * (m_hat / (jnp.sqrt(v_hat) + ADAM_EPS) + ADAM_WD * w_ref[...])
        nm_ref[...] = m2
        nv_ref[...] = v2

    tile = pl.BlockSpec(lead + (tr, cols), lambda i: (0,) * len(lead) + (i, 0))
    return pl.pallas_call(
        body, name="adamw", grid=(rows // tr,), in_specs=[tile] * 4, out_specs=[tile] * 3,
        out_shape=[jax.ShapeDtypeStruct(w.shape, F32)] * 3,
        compiler_params=_cparams("parallel"))(w, g.reshape(w.shape), m, v)


def _sum_parts(own, parts, tr, dtype=F32):
    p, rows, cols = parts.shape

    def body(*refs):
        p_ref, o_ref = refs[-2], refs[-1]
        acc = p_ref[0].astype(F32) if own is None else refs[0][...].astype(F32) + p_ref[0].astype(F32)
        for i in range(1, p):
            acc = acc + p_ref[i].astype(F32)
        o_ref[...] = acc.astype(dtype)

    tile = pl.BlockSpec((tr, cols), lambda i: (i, 0))
    ins = ([] if own is None else [tile]) + [pl.BlockSpec((p, tr, cols), lambda i: (0, i, 0))]
    args = ([] if own is None else [own]) + [parts]
    return pl.pallas_call(
        body, name="sum_parts", grid=(rows // tr,), in_specs=ins, out_specs=tile,
        out_shape=jax.ShapeDtypeStruct((rows, cols), dtype), compiler_params=_cparams("parallel"))(*args)


def _position():
    return lax.axis_index("x"), lax.axis_index("y"), lax.axis_index("c")


def _flip(pos, k):
    return tuple(1 - p if (k >> s) & 1 else p for p, s in zip(pos, (2, 1, 0)))


def _allgather8(block):
    rows, cols = block.shape

    def body(x_ref, o_ref, send_sems, recv_sems, local_sem):
        pos = _position()
        me = 4 * pos[0] + 2 * pos[1] + pos[2]
        mine = pltpu.make_async_copy(x_ref, o_ref.at[me], local_sem)
        mine.start()
        copies = [pltpu.make_async_remote_copy(src_ref=x_ref, dst_ref=o_ref.at[me], send_sem=send_sems.at[k - 1],
                                               recv_sem=recv_sems.at[k - 1], device_id=_flip(pos, k), device_id_type=MESH)
                  for k in range(1, N_DEV)]
        for cp in copies:
            cp.start()
        for cp in copies:
            cp.wait()
        mine.wait()

    vmem = pl.BlockSpec(memory_space=pltpu.VMEM)
    return pl.pallas_call(
        body, name="allgather8", in_specs=[vmem], out_specs=vmem,
        out_shape=jax.ShapeDtypeStruct((N_DEV, rows, cols), block.dtype),
        scratch_shapes=[pltpu.SemaphoreType.DMA((N_DEV - 1,)), pltpu.SemaphoreType.DMA((N_DEV - 1,)),
                        pltpu.SemaphoreType.DMA],
        compiler_params=pltpu.CompilerParams(vmem_limit_bytes=VMEM_LIMIT))(block)


COPY_BYTES = 1 << 20


def _row_chunks(rows, row_bytes):
    n = max(1, min(rows // 16, -(-rows * row_bytes // COPY_BYTES)))
    while rows % (16 * n):
        n -= 1
    return [(i * (rows // n), rows // n) for i in range(n)]


def _weight_gather(shards):
    n = len(shards)
    pieces = [_row_chunks(a.shape[1], a.shape[2] * a.dtype.itemsize) for a in shards]
    plan = [(a, k, r0, nr) for a in range(n) for k in range(1, N_CHIPS) for r0, nr in pieces[a]]

    def body(*refs):
        ins, outs = refs[:n], refs[n:2 * n]
        ici_send, ici_recv, d2d_send, d2d_recv = refs[2 * n:]
        pos = _position()
        chip, core = 2 * pos[0] + pos[1], pos[2]
        sibling = _flip(pos, 1)
        sent = []
        for i, (a, k, r0, nr) in enumerate(plan):
            cp = pltpu.make_async_remote_copy(
                src_ref=ins[a].at[core, pl.ds(r0, nr)], dst_ref=outs[a].at[chip, core, pl.ds(r0, nr)],
                send_sem=ici_send.at[i], recv_sem=ici_recv.at[i], device_id=_flip(pos, 2 * k), device_id_type=MESH)
            cp.start()
            sent.append(cp)
        passed = []
        for i, (a, k, r0, nr) in enumerate(plan):
            there = _flip(pos, 2 * k)
            landed = outs[a].at[2 * there[0] + there[1], core, pl.ds(r0, nr)]
            sent[i].wait_recv()
            cp = pltpu.make_async_remote_copy(src_ref=landed, dst_ref=landed, send_sem=d2d_send.at[i],
                                              recv_sem=d2d_recv.at[i], device_id=sibling, device_id_type=MESH)
            cp.start()
            passed.append(cp)
        for cp in passed:
            cp.wait()
        for cp in sent:
            cp.wait_send()

    hbm = pl.BlockSpec(memory_space=pl.ANY)
    sems = pltpu.SemaphoreType.DMA((len(plan),))
    return pl.pallas_call(
        body, name="weight_gather", in_specs=[hbm] * n, out_specs=[hbm] * n,
        out_shape=[jax.ShapeDtypeStruct((N_CHIPS,) + a.shape, a.dtype) for a in shards],
        scratch_shapes=[sems, sems, sems, sems],
        compiler_params=pltpu.CompilerParams(has_side_effects=True))(*shards)


def _exchange(name, arrays, out_shapes, plan, n_remote, n_local):
    n, m = len(arrays), len(out_shapes)

    def body(*refs):
        send_sems, recv_sems, local_sems = refs[n + m:]
        remote, local = plan(_position(), refs[:n], refs[n:n + m])
        assert (len(remote), len(local)) == (n_remote, n_local)
        copies = [pltpu.make_async_copy(src, dst, local_sems.at[i]) for i, (src, dst) in enumerate(local)]
        copies += [pltpu.make_async_remote_copy(src_ref=src, dst_ref=dst, send_sem=send_sems.at[i], recv_sem=recv_sems.at[i],
                                                device_id=dev, device_id_type=MESH)
                   for i, (src, dst, dev) in enumerate(remote)]
        for cp in copies:
            cp.start()
        for cp in copies:
            cp.wait()

    hbm = pl.BlockSpec(memory_space=pl.ANY)
    return pl.pallas_call(
        body, name=name, in_specs=[hbm] * n, out_specs=[hbm] * m, out_shape=out_shapes,
        scratch_shapes=[pltpu.SemaphoreType.DMA((n_remote,)), pltpu.SemaphoreType.DMA((n_remote,)),
                        pltpu.SemaphoreType.DMA((max(n_local, 1),))],
        compiler_params=pltpu.CompilerParams(has_side_effects=True))(*arrays)


def _pair_send(slabs):
    n = len(slabs)
    pieces = [_row_chunks(g.shape[2], g.shape[3] * g.dtype.itemsize) for g in slabs]

    def plan(pos, ins, outs):
        return [(ins[a].at[j, 1 - pos[2], pl.ds(r0, nr)], outs[a].at[j, pl.ds(r0, nr)], _flip(pos, 1))
                for a in range(n) for j in range(N_CHIPS) for r0, nr in pieces[a]], []

    return _exchange("pair_send", slabs, [jax.ShapeDtypeStruct((N_CHIPS,) + g.shape[2:], g.dtype) for g in slabs], plan,
                     N_CHIPS * sum(len(p) for p in pieces), 0)


def _chip_scatter(sums):
    n = len(sums)
    pieces = [_row_chunks(g.shape[1], g.shape[2] * g.dtype.itemsize) for g in sums]

    def plan(pos, ins, outs):
        remote = []
        for a in range(n):
            for k in range(1, N_CHIPS):
                to = _flip(pos, 2 * k)
                remote += [(ins[a].at[2 * to[0] + to[1], pl.ds(r0, nr)], outs[a].at[k - 1, pl.ds(r0, nr)], to)
                           for r0, nr in pieces[a]]
        return remote, []

    return _exchange("chip_scatter", sums, [jax.ShapeDtypeStruct((N_CHIPS - 1,) + g.shape[1:], g.dtype) for g in sums],
                     plan, (N_CHIPS - 1) * sum(len(p) for p in pieces), 0)


def _pair_exchange(halves):
    n = len(halves)
    pieces = [_row_chunks(h.shape[0], h.shape[1] * h.dtype.itemsize) for h in halves]

    def plan(pos, ins, outs):
        return [(ins[a].at[pl.ds(r0, nr)], outs[a].at[pl.ds(r0, nr)], _flip(pos, 1))
                for a in range(n) for r0, nr in pieces[a]], []

    return _exchange("pair_exchange", halves, [jax.ShapeDtypeStruct(h.shape, h.dtype) for h in halves], plan,
                     sum(len(p) for p in pieces), 0)


def _pack(arrays):
    flat = jnp.concatenate([a.reshape(-1).astype(F32) for a in arrays])
    size = -(-flat.shape[0] // (8 * LANE)) * (8 * LANE)
    return jnp.pad(flat, (0, size - flat.shape[0])).reshape(size // LANE, LANE)


def _unpack(buf, shapes):
    flat = buf.reshape(-1)
    out, off = [], 0
    for shp in shapes:
        n = math.prod(shp)
        out.append(flat[off:off + n].reshape(shp))
        off += n
    return out


def _unpack_rows(bufs, shapes):
    flat = bufs.reshape(bufs.shape[0], -1)
    out, off = [], 0
    for shp in shapes:
        n = math.prod(shp)
        out.append(flat[:, off:off + n].reshape((bufs.shape[0],) + shp))
        off += n
    return out


def _taps8(w):
    return jnp.pad(w, ((0, 8 - CONV_K), (0, 0)))


def _local_step(xs, tgt, scale, shift, gate, norm_w, w_in_p, b_in_p, ml_conv_w, ml_conv_b, ml_norm_w, ssm_conv_w,
                ssm_conv_b, ssm_a_log, ssm_d, ssm_norm_w, wpm, wps, wo, final_w):
    s = xs.shape[0]
    ts = min(512, s)
    tm = min(2048, s)
    u = _prenorm_fwd(xs, norm_w, scale, shift, ts)
    proj = _matmul_bias(u, w_in_p, b_in_p, tm, 512)
    mlw8, ssw8 = _taps8(ml_conv_w), _taps8(ssm_conv_w)
    qk = _conv_fwd(proj, O_QK, 2048, mlw8, ml_conv_b, ts)
    xbc = _conv_fwd(proj, O_XBC, 3072, ssw8, ssm_conv_b, ts)
    gt = proj[:, O_IF:O_IF + LANE].T
    dtt = proj[:, O_DT:O_DT + LANE].T
    hm, cst, nm = _mlstm_fwd(qk, proj, gt)
    alog_row = jnp.pad(ssm_a_log, ((0, 0), (0, LANE - SSM_HEADS)))
    alog_col = alog_row.reshape(LANE, 1)
    dskip_x = jnp.repeat(ssm_d[0], SSM_HEADDIM)[None]
    expand = _head_expand()
    yssd, sst = _ssd_fwd(xbc, proj, dtt, alog_row, alog_col, dskip_x, expand)
    tp = min(128, s)
    ym, ys = _post_fwd(hm, yssd, proj, ml_norm_w, ssm_norm_w, tp)
    dxres, merged, dmo, dpm, dps, dym, dys, dproj, acc_m = _merge(xs, ym, ys, proj, tgt, gate, final_w, wpm, wps, wo, tp)
    dh, dyssd, dproj, acc_p = _post_bwd(dym, dys, hm, yssd, proj, ml_norm_w, ssm_norm_w, dproj, tp)
    dqk, dproj, dif = _mlstm_bwd(qk, proj, gt, hm, dh, cst, nm, dproj)
    dxbc, ddt, accd, acca = _ssd_bwd(xbc, proj, dtt, alog_row, alog_col, dskip_x, expand, expand.T, dyssd, sst)
    dproj, acc_cq = _conv_bwd(proj, O_QK, 2048, mlw8, ml_conv_b, dqk, dproj, ts)
    dproj, acc_cx = _conv_bwd(proj, O_XBC, 3072, ssw8, ssm_conv_b, dxbc, dproj, ts)
    dproj = dproj.at[:, O_IF:O_IF + SMALL_W].set(dif).at[:, O_DT:O_DT + SMALL_W].set(ddt)
    gw_in_p, gb_in_p = _matmul_tn(u.T, dproj, tm, 512, with_colsum=True, a_is_transposed=True)
    du = _matmul_nt(dproj, w_in_p, tm, 512)
    grad_x, acc_n = _prenorm_bwd(du, xs, dxres, norm_w, scale, ts)
    g_wpm = _matmul_tn(ym, dpm, tm, 512)
    g_wps = _matmul_tn(ys, dps, tm, 512)
    g_wo = _matmul_tn(merged, dmo, tm, 512)
    a_coef = -jnp.exp(ssm_a_log[0])
    small = dict(
        mod=jnp.concatenate([acc_n[2], acc_n[1], acc_m[2]]), norm_w=acc_n[0], b_in=_unpad_cols(gb_in_p[0]),
        ml_conv_w=acc_cq[0:CONV_K], ml_conv_b=acc_cq[CONV_K], ml_norm_w=acc_p[0], ssm_conv_w=acc_cx[0:CONV_K],
        ssm_conv_b=acc_cx[CONV_K], ssm_a_log=acca[0, :SSM_HEADS] * a_coef,
        ssm_d=accd[0].reshape(SSM_HEADS, SSM_HEADDIM).sum(axis=1), ssm_norm_w=acc_p[1], final_w=acc_m[1], loss=acc_m[0, 0:1])
    return grad_x, small, gw_in_p, g_wpm, g_wps, g_wo


WEIGHTS = ("norm_w", "ada_w", "ada_b", "w_in", "b_in", "ml_conv_w", "ml_conv_b", "ml_norm_w", "ssm_conv_w", "ssm_conv_b",
           "ssm_a_log", "ssm_d", "ssm_norm_w", "w_proj_m", "w_proj_s", "w_out", "final_w")
LARGE = ("ada_w", "w_in", "w_proj_m", "w_proj_s", "w_out")
SMALL_SUMS = (("mod", (3 * D_MODEL,)), ("norm_w", (D_MODEL,)), ("b_in", (IN_WIDTH,)), ("ml_conv_w", (CONV_K, 2048)),
              ("ml_conv_b", (2048,)), ("ml_norm_w", (2048,)), ("ssm_conv_w", (CONV_K, 3072)), ("ssm_conv_b", (3072,)),
              ("ssm_a_log", (SSM_HEADS,)), ("ssm_d", (SSM_HEADS,)), ("ssm_norm_w", (2048,)), ("final_w", (D_MODEL,)),
              ("loss", (1,)))


def kernel(x, c, norm_w, ada_w, ada_b, w_in, b_in, ml_conv_w, ml_conv_b, ml_norm_w, ssm_conv_w, ssm_conv_b, ssm_a_log, ssm_d, ssm_norm_w, w_proj_m, w_proj_s, w_out, final_w, loss_target, m_norm_w, m_ada_w, m_ada_b, m_w_in, m_b_in, m_ml_conv_w, m_ml_conv_b, m_ml_norm_w, m_ssm_conv_w, m_ssm_conv_b, m_ssm_a_log, m_ssm_d, m_ssm_norm_w, m_w_proj_m, m_w_proj_s, m_w_out, m_final_w, v_norm_w, v_ada_w, v_ada_b, v_w_in, v_b_in, v_ml_conv_w, v_ml_conv_b, v_ml_norm_w, v_ssm_conv_w, v_ssm_conv_b, v_ssm_a_log, v_ssm_d, v_ssm_norm_w, v_w_proj_m, v_w_proj_s, v_w_out, v_final_w):
    w = dict(norm_w=norm_w, ada_w=ada_w, ada_b=ada_b, w_in=w_in, b_in=b_in, ml_conv_w=ml_conv_w, ml_conv_b=ml_conv_b,
             ml_norm_w=ml_norm_w, ssm_conv_w=ssm_conv_w, ssm_conv_b=ssm_conv_b, ssm_a_log=ssm_a_log, ssm_d=ssm_d,
             ssm_norm_w=ssm_norm_w, w_proj_m=w_proj_m, w_proj_s=w_proj_s, w_out=w_out, final_w=final_w)
    m = dict(zip(WEIGHTS, (m_norm_w, m_ada_w, m_ada_b, m_w_in, m_b_in, m_ml_conv_w, m_ml_conv_b, m_ml_norm_w, m_ssm_conv_w,
                           m_ssm_conv_b, m_ssm_a_log, m_ssm_d, m_ssm_norm_w, m_w_proj_m, m_w_proj_s, m_w_out, m_final_w)))
    v = dict(zip(WEIGHTS, (v_norm_w, v_ada_w, v_ada_b, v_w_in, v_b_in, v_ml_conv_w, v_ml_conv_b, v_ml_norm_w, v_ssm_conv_w,
                           v_ssm_conv_b, v_ssm_a_log, v_ssm_d, v_ssm_norm_w, v_w_proj_m, v_w_proj_s, v_w_out, v_final_w)))
    pos = _position()
    chip = 2 * pos[0] + pos[1]
    dev = 2 * chip + pos[2]
    mlw_cols, ssw_cols, ada_cols = ml_conv_w.shape[2], ssm_conv_w.shape[2], ada_w.shape[2]

    g0 = _allgather8(_pack([c, ml_conv_w, ssm_conv_w]))
    c_all, mlw_all, ssw_all = _unpack_rows(g0, [(D_MODEL,), (CONV_K, mlw_cols), (CONV_K, ssw_cols)])
    ml_conv_full = mlw_all[0::2].transpose(1, 0, 2).reshape(CONV_K, N_CHIPS * mlw_cols)
    ssm_conv_full = ssw_all[0::2].transpose(1, 0, 2).reshape(CONV_K, N_CHIPS * ssw_cols)

    ada_b_mine = lax.dynamic_slice_in_dim(ada_b, chip * ada_cols, ada_cols, axis=1)
    g1 = _allgather8(_ada_fwd(c_all, ada_w[0], ada_b_mine))
    mod = lax.dynamic_index_in_dim(g1[0::2], dev, axis=1, keepdims=False).reshape(1, 3 * D_MODEL)
    shift, scale, gate = mod[:, :D_MODEL], mod[:, D_MODEL:2 * D_MODEL], mod[:, 2 * D_MODEL:]

    mine = [_bf(a[0]).reshape(2, a.shape[1] // 2, a.shape[2]) for a in (w_in, w_proj_m, w_proj_s, w_out)]
    gw = [lax.dynamic_update_index_in_dim(got, own, chip, 0).reshape(N_CHIPS, -1, own.shape[-1])
          for got, own in zip(_weight_gather(mine), mine)]
    w_in_p = _shards_to_padded(gw[0])
    wpm, wps, wo = (a.reshape(-1, D_MODEL) for a in gw[1:])

    grad_x, small, g_w_in, g_wpm, g_wps, g_wo = _local_step(
        x[0], loss_target[0], scale, shift, gate, norm_w, w_in_p, _pad_cols(b_in), ml_conv_full, ml_conv_b, ml_norm_w,
        ssm_conv_full, ssm_conv_b, ssm_a_log, ssm_d, ssm_norm_w, wpm, wps, wo, final_w[None])

    g2 = _allgather8(_pack([small[name] for name, _ in SMALL_SUMS]))
    total = dict(zip([name for name, _ in SMALL_SUMS], _unpack(_sum_parts(None, g2, g2.shape[1]), [s for _, s in SMALL_SUMS])))
    dmod_all = g2[:, :3 * D_MODEL // LANE].reshape(N_DEV, 3 * D_MODEL)
    grads = dict(total)
    grads["ada_b"] = total["mod"]
    grads["ml_conv_w"] = lax.dynamic_slice_in_dim(total["ml_conv_w"], chip * mlw_cols, mlw_cols, axis=1)
    grads["ssm_conv_w"] = lax.dynamic_slice_in_dim(total["ssm_conv_w"], chip * ssw_cols, ssw_cols, axis=1)
    grads["ada_w"] = _ada_bwd(c_all, lax.dynamic_slice_in_dim(dmod_all, chip * ada_cols, ada_cols, axis=1))

    split = lambda g, rows: _bf(g).reshape(N_CHIPS, 2, rows // (2 * N_CHIPS), g.shape[-1])
    slabs = [split(_padded_to_shards(_bf(g_w_in)), N_CHIPS * D_MODEL),
             split(g_wpm, g_wpm.shape[0]), split(g_wps, g_wps.shape[0]), split(g_wo, g_wo.shape[0])]
    pair_sums = []
    for slab, rec in zip(slabs, _pair_send(slabs)):
        kept = lax.dynamic_index_in_dim(slab, pos[2], 1, keepdims=False)
        rows = kept.shape[0] * kept.shape[1]
        both = _sum_parts(kept.reshape(rows, -1), rec.reshape(1, rows, -1), 32, BF16)
        pair_sums.append(both.reshape(kept.shape))
    halves = []
    for both, rec in zip(pair_sums, _chip_scatter(pair_sums)):
        halves.append(_sum_parts(lax.dynamic_index_in_dim(both, chip, 0, keepdims=False), rec, 32))
    for name, half, other in zip(("w_in", "w_proj_m", "w_proj_s", "w_out"), halves, _pair_exchange(halves)):
        grads[name] = jnp.where(pos[2] == 0, jnp.concatenate([half, other]), jnp.concatenate([other, half]))

    delta, new_m, new_v = {}, {}, {}
    for name in LARGE:
        if w[name].shape[-1] % LANE:
            flat = lambda a: a.reshape(a.shape[-2:]).T.reshape(-1, LANE)
            back = lambda a: a.reshape(w[name].shape[-1], w[name].shape[-2]).T.reshape(w[name].shape)
            rows = w[name].size // LANE
            tr = max(t for t in range(8, 4096, 8) if rows % t == 0)
            g_flat = flat(grads[name])
            delta[name], new_m[name], new_v[name] = (back(a) for a in _adamw(flat(w[name]), g_flat, flat(m[name]), flat(v[name]), tr))
            grads[name] = back(g_flat)
        else:
            delta[name], new_m[name], new_v[name] = _adamw(w[name], grads[name], m[name], v[name], 64)
    rest = [name for name in WEIGHTS if name not in LARGE]
    packed = [_pack([t[name] for name in rest]) for t in (w, grads, m, v)]
    for out, buf in zip((delta, new_m, new_v), _adamw(*packed, packed[0].shape[0])):
        out.update(zip(rest, _unpack(buf, [w[name].shape for name in rest])))
    loss = total["loss"][0]
    return (loss, grad_x[None], *[grads[name].reshape(w[name].shape) for name in WEIGHTS], *[delta[name] for name in WEIGHTS],
            *[new_m[name] for name in WEIGHTS], *[new_v[name] for name in WEIGHTS])
```

```python
import functools
import math

import jax
import jax.numpy as jnp
from jax import lax
from jax.experimental import pallas as pl
from jax.experimental.pallas import tpu as pltpu

F32 = jnp.float32
BF16 = jnp.bfloat16
HI = lax.Precision.HIGHEST
MESH = pl.DeviceIdType.MESH

D_MODEL = 1024
EPS = 1e-6
CONV_K = 4
ML_HEADS = 8
ML_DQK = 128
ML_DV = 256
SSM_HEADS = 32
SSM_HEADDIM = 64
SSM_GROUPS = 4
SSM_STATE = 128
IN_WIDTH = 15408
N_CHIPS = 4
N_DEV = 8
ADAM_LR, ADAM_B1, ADAM_B2, ADAM_EPS, ADAM_WD, ADAM_STEP = 0.001, 0.9, 0.999, 1e-08, 0.01, 10

O_O, O_ZM, O_ZS, O_MG, O_QK, O_V, O_XBC, O_IF, O_DT = 0, 2048, 4096, 6144, 8192, 10240, 12288, 15360, 15616
SMALL_W = 256
NP = 15872
LANE = 128
CHUNK = 128
NEG = -1e30
VMEM_LIMIT = 48 * 1024 * 1024


def _cparams(*sem):
    return pltpu.CompilerParams(dimension_semantics=sem, vmem_limit_bytes=VMEM_LIMIT)


def _pad_cols(w):
    z = lambda n: jnp.zeros(w.shape[:-1] + (n,), w.dtype)
    return jnp.concatenate([w[..., 4096:8192], w[..., 11280:13328], w[..., 13360:15408], w[..., :4096], w[..., 8208:11280],
                            w[..., 8192:8208], z(SMALL_W - 16), w[..., 13328:13360], z(SMALL_W - 32)], axis=-1)


def _unpad_cols(g):
    return jnp.concatenate([g[..., O_QK:O_QK + 4096], g[..., O_O:O_O + 4096], g[..., O_IF:O_IF + 16],
                            g[..., O_XBC:O_XBC + 3072], g[..., O_ZS:O_ZS + 2048], g[..., O_DT:O_DT + 32],
                            g[..., O_MG:O_MG + 2048]], axis=-1)


PADDED_SEGMENTS = ((4096, 8192, 0), (11280, 13328, 0), (13360, 15408, 0), (0, 4096, 0), (8208, 11280, 0),
                   (8192, 8208, SMALL_W - 16), (13328, 13360, SMALL_W - 32))
SHARD_W = IN_WIDTH // N_CHIPS


def _shards_to_padded(shards):
    parts = []
    for first, last, pad in PADDED_SEGMENTS:
        for j in range(N_CHIPS):
            lo, hi = max(first, j * SHARD_W), min(last, (j + 1) * SHARD_W)
            if lo < hi:
                parts.append(shards[j][:, lo - j * SHARD_W:hi - j * SHARD_W])
        if pad:
            parts.append(jnp.zeros((shards.shape[1], pad), shards.dtype))
    return jnp.concatenate(parts, axis=1)


def _padded_to_shards(g):
    offsets, off = {}, 0
    for first, last, pad in PADDED_SEGMENTS:
        offsets[first] = off
        off += last - first + pad
    shards = []
    for j in range(N_CHIPS):
        parts = []
        for first, last, _ in sorted(PADDED_SEGMENTS):
            lo, hi = max(first, j * SHARD_W), min(last, (j + 1) * SHARD_W)
            if lo < hi:
                parts.append(g[:, offsets[first] + lo - first:offsets[first] + hi - first])
        shards.append(jnp.concatenate(parts, axis=1))
    return jnp.stack(shards)


def _sigmoid(x):
    return 1.0 / (1.0 + jnp.exp(-x))


def _silu(x):
    return x * _sigmoid(x)


def _dsilu(x):
    s = _sigmoid(x)
    return s + x * s * (1.0 - s)


def _softplus(x):
    return jnp.maximum(x, 0.0) + jnp.log(1.0 + jnp.exp(-jnp.abs(x)))


def _logsigmoid(x):
    return jnp.minimum(x, 0.0) - jnp.log(1.0 + jnp.exp(-jnp.abs(x)))


def _dot(a, b, dims, precision=None):
    return lax.dot_general(a, b, (dims, ((), ())), preferred_element_type=F32, precision=precision)


def _nn(a, b, precision=None):
    return _dot(a, b, ((1,), (0,)), precision)


def _nt(a, b, precision=None):
    return _dot(a, b, ((1,), (1,)), precision)


def _tn(a, b, precision=None):
    return _dot(a, b, ((0,), (0,)), precision)


def _bf(x):
    return x.astype(BF16)


def _split(x, terms):
    parts = []
    for _ in range(terms):
        part = _bf(x)
        parts.append(part)
        x = x - part.astype(F32)
    return parts


def _pick_right(x, pick, terms):
    pick = _bf(pick)
    out = None
    for part in _split(x, terms):
        out = _nn(part, pick) if out is None else out + _nn(part, pick)
    return out


def _pick_left(pick, x, terms):
    pick = _bf(pick)
    out = None
    for part in _split(x, terms):
        out = _nn(pick, part) if out is None else out + _nn(pick, part)
    return out


def _lane_col(x, lane):
    idx = lax.broadcasted_iota(jnp.int32, x.shape, 1)
    return jnp.sum(jnp.where(idx == lane, x, 0.0), axis=1, keepdims=True)


def _tri(n, upper):
    r = lax.broadcasted_iota(jnp.int32, (n, n), 0)
    c = lax.broadcasted_iota(jnp.int32, (n, n), 1)
    return jnp.where((r <= c) if upper else (r >= c), 1.0, 0.0).astype(F32)


def _eye(n):
    return jnp.where(lax.broadcasted_iota(jnp.int32, (n, n), 0) == lax.broadcasted_iota(jnp.int32, (n, n), 1), 1.0, 0.0)


def _sum_all(x):
    return jnp.sum(jnp.sum(x, axis=1, keepdims=True), axis=0, keepdims=True)


def _crossing(p):
    L = p.shape[0]
    hi = _bf(p)
    lo = _bf(p - hi.astype(F32))
    upper = _bf(_tri(L, True))
    below = _nn(upper, hi) + _nn(upper, lo)
    strict = lax.broadcasted_iota(jnp.int32, (L, L), 0) > lax.broadcasted_iota(jnp.int32, (L, L), 1)
    return jnp.sum(jnp.where(strict, below, 0.0), axis=1, keepdims=True)


def _matmul_bias(a, w, bias, tm, tn):
    m, k = a.shape
    n = w.shape[1]

    def body(a_ref, w_ref, b_ref, o_ref):
        o_ref[...] = _nn(a_ref[...], w_ref[...]) + b_ref[...]

    return pl.pallas_call(
        body, name="matmul_bias", grid=(m // tm, n // tn),
        in_specs=[pl.BlockSpec((tm, k), lambda i, j: (i, 0)), pl.BlockSpec((k, tn), lambda i, j: (0, j)),
                  pl.BlockSpec((1, tn), lambda i, j: (0, j))],
        out_specs=pl.BlockSpec((tm, tn), lambda i, j: (i, j)),
        out_shape=jax.ShapeDtypeStruct((m, n), F32),
        compiler_params=_cparams("parallel", "arbitrary"))(a, w, bias)


def _matmul_nt(a, w, tm, tk):
    m, n = a.shape
    k = w.shape[0]

    def body(a_ref, w_ref, o_ref):
        @pl.when(pl.program_id(1) == 0)
        def _():
            o_ref[...] = jnp.zeros_like(o_ref)
        o_ref[...] += _nt(a_ref[...], w_ref[...])

    return pl.pallas_call(
        body, name="matmul_nt", grid=(m // tm, n // tk),
        in_specs=[pl.BlockSpec((tm, tk), lambda i, j: (i, j)), pl.BlockSpec((k, tk), lambda i, j: (0, j))],
        out_specs=pl.BlockSpec((tm, k), lambda i, j: (i, 0)),
        out_shape=jax.ShapeDtypeStruct((m, k), F32),
        compiler_params=_cparams("parallel", "arbitrary"))(a, w)


def _matmul_tn(a, b, tm, tn, with_colsum=False, a_is_transposed=False):
    k, m = a.shape if a_is_transposed else a.shape[::-1]
    n = b.shape[1]

    def body(a_ref, b_ref, o_ref, *rest):
        first = pl.program_id(1) == 0

        @pl.when(first)
        def _():
            o_ref[...] = jnp.zeros_like(o_ref)
        o_ref[...] += _nn(a_ref[...], b_ref[...]) if a_is_transposed else _tn(a_ref[...], b_ref[...])
        if with_colsum:
            s_ref = rest[0]

            @pl.when(first)
            def _():
                s_ref[...] = jnp.zeros_like(s_ref)
            s_ref[...] += jnp.sum(b_ref[...].astype(F32), axis=0, keepdims=True)

    out_specs = [pl.BlockSpec((k, tn), lambda j, i: (0, j))]
    out_shape = [jax.ShapeDtypeStruct((k, n), F32)]
    if with_colsum:
        out_specs.append(pl.BlockSpec((1, tn), lambda j, i: (0, j)))
        out_shape.append(jax.ShapeDtypeStruct((1, n), F32))
    out = pl.pallas_call(
        body, name="matmul_tn", grid=(n // tn, m // tm),
        in_specs=[pl.BlockSpec((k, tm), lambda j, i: (0, i)) if a_is_transposed else pl.BlockSpec((tm, k), lambda j, i: (i, 0)),
                  pl.BlockSpec((tm, tn), lambda j, i: (i, j))],
        out_specs=out_specs, out_shape=out_shape,
        compiler_params=_cparams("parallel", "arbitrary"))(a, b)
    return out if with_colsum else out[0]


def _ada_fwd(c_all, ada_w, ada_b):
    def body(c_ref, w_ref, b_ref, o_ref):
        o_ref[...] = _nn(_bf(_silu(c_ref[...])), _bf(w_ref[...])) + b_ref[...]

    return pl.pallas_call(body, name="ada_fwd", out_shape=jax.ShapeDtypeStruct((c_all.shape[0], ada_w.shape[1]), F32),
                          compiler_params=_cparams())(c_all, ada_w, ada_b)


def _ada_bwd(c_all, dmod):
    def body(c_ref, d_ref, o_ref):
        o_ref[...] = _tn(_bf(_silu(c_ref[...])), _bf(d_ref[...]))

    return pl.pallas_call(body, name="ada_bwd", out_shape=jax.ShapeDtypeStruct((c_all.shape[1], dmod.shape[1]), F32),
                          compiler_params=_cparams())(c_all, dmod)


def _prenorm_fwd(x, norm_w, scale, shift, ts):
    s, d = x.shape

    def body(x_ref, nw_ref, sc_ref, sh_ref, u_ref):
        xv = x_ref[...]
        r = lax.rsqrt(jnp.mean(xv * xv, axis=1, keepdims=True) + EPS)
        u_ref[...] = _bf(xv * r * nw_ref[...] * (1.0 + sc_ref[...]) + sh_ref[...])

    row = pl.BlockSpec((1, d), lambda i: (0, 0))
    return pl.pallas_call(
        body, name="prenorm_fwd", grid=(s // ts,),
        in_specs=[pl.BlockSpec((ts, d), lambda i: (i, 0)), row, row, row],
        out_specs=pl.BlockSpec((ts, d), lambda i: (i, 0)), out_shape=jax.ShapeDtypeStruct((s, d), BF16),
        compiler_params=_cparams("parallel"))(x, norm_w, scale, shift)


def _prenorm_bwd(du, x, dxres, norm_w, scale, ts):
    s, d = x.shape

    def body(du_ref, x_ref, dr_ref, nw_ref, sc_ref, gx_ref, acc_ref):
        @pl.when(pl.program_id(0) == 0)
        def _():
            acc_ref[...] = jnp.zeros_like(acc_ref)
        xv, duv = x_ref[...], du_ref[...]
        r = lax.rsqrt(jnp.mean(xv * xv, axis=1, keepdims=True) + EPS)
        xn = xv * r
        nw, sc1 = nw_ref[...], 1.0 + sc_ref[...]
        dxn = duv * (nw * sc1)
        gx_ref[...] = r * (dxn - xn * jnp.mean(dxn * xn, axis=1, keepdims=True)) + dr_ref[...]
        t = duv * xn
        acc_ref[0:1, :] += jnp.sum(t, axis=0, keepdims=True) * sc1
        acc_ref[1:2, :] += jnp.sum(t, axis=0, keepdims=True) * nw
        acc_ref[2:3, :] += jnp.sum(duv, axis=0, keepdims=True)

    tile = pl.BlockSpec((ts, d), lambda i: (i, 0))
    row = pl.BlockSpec((1, d), lambda i: (0, 0))
    return pl.pallas_call(
        body, name="prenorm_bwd", grid=(s // ts,),
        in_specs=[tile, tile, tile, row, row],
        out_specs=[tile, pl.BlockSpec((8, d), lambda i: (0, 0))],
        out_shape=[jax.ShapeDtypeStruct((s, d), F32), jax.ShapeDtypeStruct((8, d), F32)],
        compiler_params=_cparams("arbitrary"))(du, x, dxres, norm_w, scale)


CONV_CB = 512


def _conv_taps(buf_ref, ts):
    return [buf_ref[pl.ds(8 - (CONV_K - 1) + j, ts), :] for j in range(CONV_K)]


def _conv_fwd(proj, col0, width, w8, b, ts):
    s = proj.shape[0]
    cb = CONV_CB
    nt = s // ts

    def body(x_ref, w_ref, b_ref, o_ref, ds_ref, buf_ref):
        @pl.when(pl.program_id(1) == 0)
        def _():
            buf_ref[0:8, :] = jnp.zeros((8, cb), F32)
        buf_ref[pl.ds(8, ts), :] = x_ref[...]
        acc = b_ref[...] + jnp.zeros((ts, cb), F32)
        for j, tap in enumerate(_conv_taps(buf_ref, ts)):
            acc = acc + tap * w_ref[j:j + 1, :]
        sg = _sigmoid(acc)
        o_ref[...] = acc * sg
        ds_ref[...] = _bf(sg + acc * sg * (1.0 - sg))
        buf_ref[0:8, :] = x_ref[pl.ds(ts - 8, 8), :]

    c0 = col0 // cb
    tile = pl.BlockSpec((ts, cb), lambda c, i: (i, c))
    return pl.pallas_call(
        body, name="conv_fwd", grid=(width // cb, nt),
        in_specs=[pl.BlockSpec((ts, cb), lambda c, i: (i, c0 + c)), pl.BlockSpec((8, cb), lambda c, i: (0, c)),
                  pl.BlockSpec((1, cb), lambda c, i: (0, c))],
        out_specs=[tile, tile],
        out_shape=[jax.ShapeDtypeStruct((s, width), F32), jax.ShapeDtypeStruct((s, width), BF16)],
        scratch_shapes=[pltpu.VMEM((ts + 8, cb), F32)],
        compiler_params=_cparams("parallel", "arbitrary"))(proj, w8, b)


def _conv_bwd(proj, col0, width, w8, dact, dpost, dproj, ts):
    s = proj.shape[0]
    cb = CONV_CB
    nt = s // ts
    c0 = col0 // cb

    def body(x_ref, da_ref, dp_ref, w_ref, _, dx_ref, acc_ref, dbuf_ref):
        @pl.when(pl.program_id(1) == 0)
        def _():
            acc_ref[...] = jnp.zeros_like(acc_ref)
            dbuf_ref[pl.ds(ts, 8), :] = jnp.zeros((8, cb), F32)
        dconv = dp_ref[...] * da_ref[...].astype(F32)
        acc_ref[CONV_K:CONV_K + 1, :] += jnp.sum(dconv, axis=0, keepdims=True)
        dbuf_ref[pl.ds(0, ts), :] = dconv
        xv = x_ref[...]
        dx = jnp.zeros((ts, cb), F32)
        for j in range(CONV_K):
            shifted = dbuf_ref[pl.ds(CONV_K - 1 - j, ts), :]
            dx = dx + shifted * w_ref[j:j + 1, :]
            acc_ref[j:j + 1, :] += jnp.sum(xv * shifted, axis=0, keepdims=True)
        dx_ref[...] = _bf(dx)
        dbuf_ref[pl.ds(ts, 8), :] = dconv[0:8, :]

    tile = pl.BlockSpec((ts, cb), lambda c, i: (nt - 1 - i, c))
    wide = pl.BlockSpec((ts, cb), lambda c, i: (nt - 1 - i, c0 + c))
    return pl.pallas_call(
        body, name="conv_bwd", grid=(width // cb, nt),
        in_specs=[wide, tile, tile, pl.BlockSpec((8, cb), lambda c, i: (0, c)), pl.BlockSpec(memory_space=pl.ANY)],
        out_specs=[wide, pl.BlockSpec((8, cb), lambda c, i: (0, c))],
        out_shape=[jax.ShapeDtypeStruct(dproj.shape, dproj.dtype), jax.ShapeDtypeStruct((8, width), F32)],
        input_output_aliases={4: 0},
        scratch_shapes=[pltpu.VMEM((ts + 8, cb), F32)],
        compiler_params=_cparams("parallel", "arbitrary"))(proj, dact, dpost, w8, dproj)


def _mlstm_gates(gif_ref, gt_ref, a_scr, at_scr):
    L = gif_ref.shape[0]
    fb = _logsigmoid(gif_ref[...])
    a_scr[...] = _pick_left(_tri(L, False), fb, 3)
    at_scr[...] = _pick_right(_logsigmoid(gt_ref[...]), _tri(L, True), 3)
    return jnp.sum(fb, axis=0, keepdims=True)


def _mlstm_head(h, qk_ref, v_ref, gif, gt_ref, a, at_scr, a_last_row, c_mat, n_row, m_prev):
    L = gif.shape[0]
    q = qk_ref[:, h * ML_DQK:(h + 1) * ML_DQK] * (ML_DQK ** -0.5)
    k = qk_ref[:, (ML_HEADS + h) * ML_DQK:(ML_HEADS + h + 1) * ML_DQK]
    v = v_ref[:, h * ML_DV:(h + 1) * ML_DV]
    i_col, a_col = _lane_col(gif, h), _lane_col(a, ML_HEADS + h)
    i_row, a_row = gt_ref[h:h + 1, :], at_scr[ML_HEADS + h:ML_HEADS + h + 1, :]
    causal = lax.broadcasted_iota(jnp.int32, (L, L), 0) >= lax.broadcasted_iota(jnp.int32, (L, L), 1)
    dmat = jnp.where(causal, a_col - a_row + i_row, NEG)
    inter = a_col + m_prev
    m_t = jnp.maximum(inter, jnp.max(dmat, axis=1, keepdims=True))
    w_intra = jnp.exp(dmat - m_t)
    w_inter = jnp.exp(inter - m_t)
    sc = _nt(_bf(q), _bf(k)) * w_intra
    den = jnp.sum(sc, axis=1, keepdims=True) + w_inter * jnp.sum(q * n_row, axis=1, keepdims=True)
    floor = jnp.exp(-m_t)
    a_last = _lane_col(a_last_row, ML_HEADS + h)
    g = a_last - a_col + i_col
    m_new = jnp.maximum(a_last + m_prev, jnp.max(g, axis=0, keepdims=True))
    wk = jnp.exp(g - m_new)
    decay = jnp.exp(a_last + m_prev - m_new)
    return dict(q=q, k=k, v=v, w_intra=w_intra, w_inter=w_inter, sc=sc, den=den, floor=floor, m_new=m_new, wk=wk,
                decay=decay)


def _state_tile(n_row, m11):
    r = lax.broadcasted_iota(jnp.int32, (8, LANE), 0)
    return jnp.where(r == 0, n_row, jnp.where(r == 1, m11, 0.0))


def _mlstm_fwd(qk, proj, gt):
    s = qk.shape[0]
    L = CHUNK
    nc = s // L

    def body(qk_ref, v_ref, gif_ref, gt_ref, h_ref, cst_ref, nm_ref, c_scr, nm_scr, a_scr, at_scr):
        @pl.when(pl.program_id(0) == 0)
        def _():
            c_scr[...] = jnp.zeros_like(c_scr)
            nm_scr[...] = jnp.zeros_like(nm_scr)
        a_last_row = _mlstm_gates(gif_ref, gt_ref, a_scr, at_scr)
        gif, a = gif_ref[...], a_scr[...]
        for h in range(ML_HEADS):
            c_mat, n_row = c_scr[h], nm_scr[h, 0:1, :]
            m_prev = jnp.max(nm_scr[h, 1:2, :], axis=1, keepdims=True)
            cst_ref[0, h] = c_mat
            nm_ref[0, h] = nm_scr[h]
            t = _mlstm_head(h, qk_ref, v_ref, gif, gt_ref, a, at_scr, a_last_row, c_mat, n_row, m_prev)
            num = _nn(_bf(t["sc"]), _bf(t["v"])) + t["w_inter"] * _nn(_bf(t["q"]), _bf(c_mat))
            h_ref[:, h * ML_DV:(h + 1) * ML_DV] = num / jnp.maximum(jnp.abs(t["den"]), t["floor"])
            kw = t["k"] * t["wk"]
            c_scr[h] = t["decay"] * c_mat + _tn(_bf(kw), _bf(t["v"]))
            nm_scr[h] = _state_tile(t["decay"] * n_row + jnp.sum(kw, axis=0, keepdims=True), t["m_new"])

    return pl.pallas_call(
        body, name="mlstm_fwd", grid=(nc,),
        in_specs=[pl.BlockSpec((L, 2048), lambda c: (c, 0)), pl.BlockSpec((L, 2048), lambda c: (c, O_V // 2048)),
                  pl.BlockSpec((L, LANE), lambda c: (c, O_IF // LANE)), pl.BlockSpec((LANE, L), lambda c: (0, c))],
        out_specs=[pl.BlockSpec((L, 2048), lambda c: (c, 0)),
                   pl.BlockSpec((1, ML_HEADS, ML_DQK, ML_DV), lambda c: (c, 0, 0, 0)),
                   pl.BlockSpec((1, ML_HEADS, 8, LANE), lambda c: (c, 0, 0, 0))],
        out_shape=[jax.ShapeDtypeStruct((s, 2048), F32), jax.ShapeDtypeStruct((nc, ML_HEADS, ML_DQK, ML_DV), F32),
                   jax.ShapeDtypeStruct((nc, ML_HEADS, 8, LANE), F32)],
        scratch_shapes=[pltpu.VMEM((ML_HEADS, ML_DQK, ML_DV), F32), pltpu.VMEM((ML_HEADS, 8, LANE), F32),
                        pltpu.VMEM((L, LANE), F32), pltpu.VMEM((LANE, L), F32)],
        compiler_params=_cparams("arbitrary"))(qk, proj, proj, gt)


def _mlstm_bwd(qk, proj, gt, hout, dh, cst, nm, dproj):
    s = qk.shape[0]
    L = CHUNK
    nc = s // L

    def body(qk_ref, v_ref, gif_ref, gt_ref, h_ref, dh_ref, cst_ref, nm_ref, _, dqk_ref, dv_ref, dif_ref,
             dc_scr, dn_scr, a_scr, at_scr):
        @pl.when(pl.program_id(0) == 0)
        def _():
            dc_scr[...] = jnp.zeros_like(dc_scr)
            dn_scr[...] = jnp.zeros_like(dn_scr)
        a_last_row = _mlstm_gates(gif_ref, gt_ref, a_scr, at_scr)
        gif, a = gif_ref[...], a_scr[...]
        lane = lax.broadcasted_iota(jnp.int32, (L, LANE), 1)
        last = lax.broadcasted_iota(jnp.int32, (L, 1), 0) == L - 1
        di_tile = jnp.zeros((L, LANE), F32)
        cross = [jnp.zeros((L, LANE), F32)] * 3
        for h in range(ML_HEADS):
            c_mat, n_row = cst_ref[0, h], nm_ref[0, h, 0:1, :]
            m_prev = jnp.max(nm_ref[0, h, 1:2, :], axis=1, keepdims=True)
            t = _mlstm_head(h, qk_ref, v_ref, gif, gt_ref, a, at_scr, a_last_row, c_mat, n_row, m_prev)
            q, k, v, den = t["q"], t["k"], t["v"], t["den"]
            dhh = dh_ref[:, h * ML_DV:(h + 1) * ML_DV]
            hh = h_ref[:, h * ML_DV:(h + 1) * ML_DV]
            dnorm = jnp.maximum(jnp.abs(den), t["floor"])
            dnum = dhh / dnorm
            d_dn = -jnp.sum(dhh * hh, axis=1, keepdims=True) / dnorm
            dden = jnp.where(jnp.abs(den) >= t["floor"], jnp.where(den >= 0.0, d_dn, -d_dn), 0.0)
            dsc = _nt(_bf(dnum), _bf(v)) + dden
            ds = dsc * t["w_intra"]
            dq_inter = t["w_inter"] * (_nt(_bf(dnum), _bf(c_mat)) + dden * n_row)
            dq = _nn(_bf(ds), _bf(k)) + dq_inter
            dc, dn_row = dc_scr[h], dn_scr[h, 0:1, :]
            dk_state = t["wk"] * (_nt(_bf(v), _bf(dc)) + dn_row)
            dk = _tn(_bf(ds), _bf(q)) + dk_state
            dv = _tn(_bf(t["sc"]), _bf(dnum)) + t["wk"] * _nn(_bf(k), _bf(dc))
            qi = q * t["w_inter"]
            dc_scr[h] = t["decay"] * dc + _tn(_bf(qi), _bf(dnum))
            dn_scr[h] = jnp.broadcast_to(t["decay"] * dn_row + jnp.sum(qi * dden, axis=0, keepdims=True), (8, LANE))
            dqk_ref[:, h * ML_DQK:(h + 1) * ML_DQK] = dq * (ML_DQK ** -0.5)
            dqk_ref[:, (ML_HEADS + h) * ML_DQK:(ML_HEADS + h + 1) * ML_DQK] = dk
            dv_ref[:, h * ML_DV:(h + 1) * ML_DV] = _bf(dv)
            di_tile = di_tile + jnp.where(lane == h, jnp.sum(k * dk, axis=1, keepdims=True), 0.0)
            carried = t["decay"] * (_sum_all(dc * c_mat) + jnp.sum(dn_row * n_row, axis=1, keepdims=True))
            parts = (_crossing(dsc * t["sc"]),
                     jnp.sum(q * dq_inter, axis=1, keepdims=True) + jnp.where(last, carried, 0.0),
                     jnp.sum(k * dk_state, axis=1, keepdims=True))
            cross = [c + jnp.where(lane == ML_HEADS + h, p, 0.0) for c, p in zip(cross, parts)]
        dfb = cross[0] + _pick_left(_tri(L, True), cross[1], 2) + _pick_left(_tri(L, False) - _eye(L), cross[2], 2)
        dif_ref[:, 0:LANE] = _bf(di_tile + dfb * _sigmoid(-gif))
        dif_ref[:, LANE:SMALL_W] = jnp.zeros((L, SMALL_W - LANE), BF16)

    rev = lambda c: nc - 1 - c
    return pl.pallas_call(
        body, name="mlstm_bwd", grid=(nc,),
        in_specs=[pl.BlockSpec((L, 2048), lambda c: (rev(c), 0)), pl.BlockSpec((L, 2048), lambda c: (rev(c), O_V // 2048)),
                  pl.BlockSpec((L, LANE), lambda c: (rev(c), O_IF // LANE)), pl.BlockSpec((LANE, L), lambda c: (0, rev(c))),
                  pl.BlockSpec((L, 2048), lambda c: (rev(c), 0)), pl.BlockSpec((L, 2048), lambda c: (rev(c), 0)),
                  pl.BlockSpec((1, ML_HEADS, ML_DQK, ML_DV), lambda c: (rev(c), 0, 0, 0)),
                  pl.BlockSpec((1, ML_HEADS, 8, LANE), lambda c: (rev(c), 0, 0, 0)), pl.BlockSpec(memory_space=pl.ANY)],
        out_specs=[pl.BlockSpec((L, 2048), lambda c: (rev(c), 0)), pl.BlockSpec((L, 2048), lambda c: (rev(c), O_V // 2048)),
                   pl.BlockSpec((L, SMALL_W), lambda c: (rev(c), 0))],
        out_shape=[jax.ShapeDtypeStruct((s, 2048), F32), jax.ShapeDtypeStruct(dproj.shape, dproj.dtype),
                   jax.ShapeDtypeStruct((s, SMALL_W), BF16)],
        input_output_aliases={8: 1},
        scratch_shapes=[pltpu.VMEM((ML_HEADS, ML_DQK, ML_DV), F32), pltpu.VMEM((ML_HEADS, 8, LANE), F32),
                        pltpu.VMEM((L, LANE), F32), pltpu.VMEM((LANE, L), F32)],
        compiler_params=_cparams("arbitrary"))(qk, proj, proj, gt, hout, dh, cst, nm, dproj)


GROUP_W = SSM_HEADS // SSM_GROUPS * SSM_HEADDIM
O_B = SSM_HEADS * SSM_HEADDIM
O_C = O_B + SSM_GROUPS * SSM_STATE


def _head_expand():
    r = jnp.arange(LANE)[:, None]
    c = jnp.arange(SSM_HEADS * SSM_HEADDIM)[None, :] // SSM_HEADDIM
    return (r == c).astype(F32)


def _ssd_gates(dt_ref, dtt_ref, alog_row_ref, alog_col_ref, at_scr):
    L = dt_ref.shape[0]
    dt = _softplus(dt_ref[...])
    acoef = -jnp.exp(alog_row_ref[...])
    a = _pick_left(_tri(L, False), dt * acoef, 3)
    at_scr[...] = _pick_right(_softplus(dtt_ref[...]) * (-jnp.exp(alog_col_ref[...])), _tri(L, True), 3)
    return dt, acoef, a


def _ssd_group(g, xbc_ref, dt, a, e_ref, ax_scr):
    eg = e_ref[:, g * GROUP_W:(g + 1) * GROUP_W]
    ax_scr[...] = _pick_right(a, eg, 3)
    ax = ax_scr[...]
    alx = ax_scr[ax.shape[0] - 1:ax.shape[0], :]
    dtx = _pick_right(dt, eg, 2)
    xg = xbc_ref[:, g * GROUP_W:(g + 1) * GROUP_W]
    bg = xbc_ref[:, O_B + g * SSM_STATE:O_B + (g + 1) * SSM_STATE]
    cg = xbc_ref[:, O_C + g * SSM_STATE:O_C + (g + 1) * SSM_STATE]
    return dict(ax=ax, alx=alx, dtx=dtx, xg=xg, bg=bg, cg=cg, xdt=xg * dtx, gmat=_nt(_bf(cg), _bf(bg)))


def _ssd_decay(hh, a, at_scr):
    L = a.shape[0]
    causal = lax.broadcasted_iota(jnp.int32, (L, L), 0) >= lax.broadcasted_iota(jnp.int32, (L, L), 1)
    return jnp.exp(jnp.where(causal, _lane_col(a, hh) - at_scr[hh:hh + 1, :], NEG))


def _ssd_fwd(xbc, proj, dtt, alog_row, alog_col, dskip_x, expand):
    s = xbc.shape[0]
    L = CHUNK
    nc = s // L
    half = SSM_HEADDIM

    def body(xbc_ref, dt_ref, dtt_ref, ar_ref, ac_ref, dk_ref, e_ref, y_ref, st_ref, st_scr, at_scr, ax_scr):
        @pl.when(pl.program_id(0) == 0)
        def _():
            st_scr[...] = jnp.zeros_like(st_scr)
        dt, _, a = _ssd_gates(dt_ref, dtt_ref, ar_ref, ac_ref, at_scr)
        lane = lax.broadcasted_iota(jnp.int32, (L, LANE), 1)
        for g in range(SSM_GROUPS):
            t = _ssd_group(g, xbc_ref, dt, a, e_ref, ax_scr)
            st = st_scr[g]
            st_ref[0, g] = st
            pairs = []
            for j in range(GROUP_W // LANE):
                xp = _bf(t["xdt"][:, j * LANE:(j + 1) * LANE])
                hh = g * (SSM_HEADS // SSM_GROUPS) + 2 * j
                y0 = _nn(_bf(t["gmat"] * _ssd_decay(hh, a, at_scr)), xp)
                y1 = _nn(_bf(t["gmat"] * _ssd_decay(hh + 1, a, at_scr)), xp)
                pairs.append(jnp.where(lane < half, y0, y1))
            y = jnp.concatenate(pairs, axis=1) + _nn(_bf(t["cg"]), _bf(st)) * jnp.exp(t["ax"])
            y_ref[:, g * GROUP_W:(g + 1) * GROUP_W] = y + dk_ref[:, g * GROUP_W:(g + 1) * GROUP_W] * t["xg"]
            wts = jnp.exp(t["alx"] - t["ax"])
            st_scr[g] = jnp.exp(t["alx"]) * st + _tn(_bf(t["bg"]), _bf(t["xdt"] * wts))

    row = lambda w: pl.BlockSpec((1, w), lambda c: (0, 0))
    return pl.pallas_call(
        body, name="ssd_fwd", grid=(nc,),
        in_specs=[pl.BlockSpec((L, 3072), lambda c: (c, 0)), pl.BlockSpec((L, LANE), lambda c: (c, O_DT // LANE)),
                  pl.BlockSpec((LANE, L), lambda c: (0, c)), row(LANE), pl.BlockSpec((LANE, 1), lambda c: (0, 0)),
                  row(2048), pl.BlockSpec((LANE, 2048), lambda c: (0, 0))],
        out_specs=[pl.BlockSpec((L, 2048), lambda c: (c, 0)),
                   pl.BlockSpec((1, SSM_GROUPS, SSM_STATE, GROUP_W), lambda c: (c, 0, 0, 0))],
        out_shape=[jax.ShapeDtypeStruct((s, 2048), F32),
                   jax.ShapeDtypeStruct((nc, SSM_GROUPS, SSM_STATE, GROUP_W), F32)],
        scratch_shapes=[pltpu.VMEM((SSM_GROUPS, SSM_STATE, GROUP_W), F32), pltpu.VMEM((LANE, L), F32),
                        pltpu.VMEM((L, GROUP_W), F32)],
        compiler_params=_cparams("arbitrary"))(xbc, proj, dtt, alog_row, alog_col, dskip_x, expand)


def _ssd_bwd(xbc, proj, dtt, alog_row, alog_col, dskip_x, expand, expand_t, dy, states):
    s = xbc.shape[0]
    L = CHUNK
    nc = s // L
    half = SSM_HEADDIM

    def body(xbc_ref, dt_ref, dtt_ref, ar_ref, ac_ref, dk_ref, e_ref, et_ref, dy_ref, st_ref,
             dxbc_ref, ddt_ref, accd_ref, acca_ref, dst_scr, at_scr, ax_scr):
        @pl.when(pl.program_id(0) == 0)
        def _():
            dst_scr[...] = jnp.zeros_like(dst_scr)
            accd_ref[...] = jnp.zeros_like(accd_ref)
            acca_ref[...] = jnp.zeros_like(acca_ref)
        dt, acoef, a = _ssd_gates(dt_ref, dtt_ref, ar_ref, ac_ref, at_scr)
        lane = lax.broadcasted_iota(jnp.int32, (L, LANE), 1)
        low = lane < half
        last = lax.broadcasted_iota(jnp.int32, (L, 1), 0) == L - 1
        cross = [jnp.zeros((L, LANE), F32)] * 3
        ddt_tile = jnp.zeros((L, LANE), F32)
        for g in range(SSM_GROUPS):
            t = _ssd_group(g, xbc_ref, dt, a, e_ref, ax_scr)
            xg, bg, cg, xdt, gmat = t["xg"], t["bg"], t["cg"], t["xdt"], t["gmat"]
            st, dst = st_ref[0, g], dst_scr[g]
            dyg = dy_ref[:, g * GROUP_W:(g + 1) * GROUP_W]
            ea, eal = jnp.exp(t["ax"]), jnp.exp(t["alx"])
            wts = jnp.exp(t["alx"] - t["ax"])
            dyi = dyg * ea
            y_inter = _nn(_bf(cg), _bf(st)) * ea
            dc = _nt(_bf(dyi), _bf(st))
            d_xdt_state = _nn(_bf(bg), _bf(dst)) * wts
            db = _nt(_bf(xdt * wts), _bf(dst))
            dst_scr[g] = eal * dst + _tn(_bf(cg), _bf(dyi))
            dg = jnp.zeros((L, L), F32)
            dx_pairs = []
            for j in range(GROUP_W // LANE):
                xp = _bf(xdt[:, j * LANE:(j + 1) * LANE])
                dyp = dyg[:, j * LANE:(j + 1) * LANE]
                dxs = []
                for b in range(2):
                    hh = g * (SSM_HEADS // SSM_GROUPS) + 2 * j + b
                    dec = _ssd_decay(hh, a, at_scr)
                    w = gmat * dec
                    dxs.append(_tn(_bf(w), _bf(dyp)))
                    dw = _nt(_bf(jnp.where(low if b == 0 else ~low, dyp, 0.0)), xp)
                    dg = dg + dw * dec
                    cross[0] = cross[0] + jnp.where(lane == hh, _crossing(dw * w), 0.0)
                dx_pairs.append(jnp.where(low, dxs[0], dxs[1]))
            d_xdt = d_xdt_state + jnp.concatenate(dx_pairs, axis=1)
            dc = dc + _nn(_bf(dg), _bf(bg))
            db = db + _tn(_bf(dg), _bf(cg))
            etg = et_ref[g * GROUP_W:(g + 1) * GROUP_W, :]
            carried = jnp.sum(dst * st, axis=0, keepdims=True) * eal
            cross[1] = cross[1] + _pick_right(dyg * y_inter + jnp.where(last, carried, 0.0), etg, 2)
            cross[2] = cross[2] + _pick_right(xdt * d_xdt_state, etg, 2)
            ddt_tile = ddt_tile + _pick_right(d_xdt * xg, etg, 2)
            dxbc_ref[:, g * GROUP_W:(g + 1) * GROUP_W] = d_xdt * t["dtx"] + dk_ref[:, g * GROUP_W:(g + 1) * GROUP_W] * dyg
            dxbc_ref[:, O_B + g * SSM_STATE:O_B + (g + 1) * SSM_STATE] = db
            dxbc_ref[:, O_C + g * SSM_STATE:O_C + (g + 1) * SSM_STATE] = dc
            accd_ref[0:1, g * GROUP_W:(g + 1) * GROUP_W] += jnp.sum(dyg * xg, axis=0, keepdims=True)
        d_da = cross[0] + _pick_left(_tri(L, True), cross[1], 2) + _pick_left(_tri(L, False) - _eye(L), cross[2], 2)
        acca_ref[0:1, :] += jnp.sum(d_da * dt, axis=0, keepdims=True)
        ddt_ref[:, 0:LANE] = _bf((ddt_tile + d_da * acoef) * _sigmoid(dt_ref[...]))
        ddt_ref[:, LANE:SMALL_W] = jnp.zeros((L, SMALL_W - LANE), BF16)

    rev = lambda c: nc - 1 - c
    row = lambda w: pl.BlockSpec((1, w), lambda c: (0, 0))
    return pl.pallas_call(
        body, name="ssd_bwd", grid=(nc,),
        in_specs=[pl.BlockSpec((L, 3072), lambda c: (rev(c), 0)), pl.BlockSpec((L, LANE), lambda c: (rev(c), O_DT // LANE)),
                  pl.BlockSpec((LANE, L), lambda c: (0, rev(c))), row(LANE), pl.BlockSpec((LANE, 1), lambda c: (0, 0)),
                  row(2048), pl.BlockSpec((LANE, 2048), lambda c: (0, 0)), pl.BlockSpec((2048, LANE), lambda c: (0, 0)),
                  pl.BlockSpec((L, 2048), lambda c: (rev(c), 0)),
                  pl.BlockSpec((1, SSM_GROUPS, SSM_STATE, GROUP_W), lambda c: (rev(c), 0, 0, 0))],
        out_specs=[pl.BlockSpec((L, 3072), lambda c: (rev(c), 0)), pl.BlockSpec((L, SMALL_W), lambda c: (rev(c), 0)),
                   pl.BlockSpec((8, 2048), lambda c: (0, 0)), pl.BlockSpec((8, LANE), lambda c: (0, 0))],
        out_shape=[jax.ShapeDtypeStruct((s, 3072), F32), jax.ShapeDtypeStruct((s, SMALL_W), BF16),
                   jax.ShapeDtypeStruct((8, 2048), F32), jax.ShapeDtypeStruct((8, LANE), F32)],
        scratch_shapes=[pltpu.VMEM((SSM_GROUPS, SSM_STATE, GROUP_W), F32),
                        pltpu.VMEM((LANE, L), F32), pltpu.VMEM((L, GROUP_W), F32)],
        compiler_params=_cparams("arbitrary"))(xbc, proj, dtt, alog_row, alog_col, dskip_x, expand, expand_t, dy, states)


def _group_norm(v, width):
    outs, rs = [], []
    for k in range(v.shape[1] // width):
        blk = v[:, k * width:(k + 1) * width]
        r = lax.rsqrt(jnp.mean(blk * blk, axis=1, keepdims=True) + EPS)
        outs.append(blk * r)
        rs.append(jnp.broadcast_to(r, blk.shape))
    return jnp.concatenate(outs, axis=1), jnp.concatenate(rs, axis=1)


def _group_mean(v, width):
    return jnp.concatenate([jnp.broadcast_to(jnp.mean(v[:, k * width:(k + 1) * width], axis=1, keepdims=True),
                                             (v.shape[0], width)) for k in range(v.shape[1] // width)], axis=1)


def _post_fwd(hm, yssd, proj, ml_norm_w, ssm_norm_w, ts):
    s = hm.shape[0]

    def body(h_ref, ys_ref, o_ref, zm_ref, zs_ref, wm_ref, ws_ref, ym_ref, yso_ref):
        hn, _ = _group_norm(h_ref[...], ML_DV)
        ym_ref[...] = _bf(_sigmoid(o_ref[...]) * hn * wm_ref[...] * _silu(zm_ref[...]))
        pn, _ = _group_norm(ys_ref[...] * _silu(zs_ref[...]), GROUP_W)
        yso_ref[...] = _bf(pn * ws_ref[...])

    tile = pl.BlockSpec((ts, 2048), lambda i: (i, 0))
    col = lambda off: pl.BlockSpec((ts, 2048), lambda i: (i, off // 2048))
    row = pl.BlockSpec((1, 2048), lambda i: (0, 0))
    return pl.pallas_call(
        body, name="post_fwd", grid=(s // ts,),
        in_specs=[tile, tile, col(O_O), col(O_ZM), col(O_ZS), row, row],
        out_specs=[tile, tile],
        out_shape=[jax.ShapeDtypeStruct((s, 2048), BF16)] * 2,
        compiler_params=_cparams("parallel"))(hm, yssd, proj, proj, proj, ml_norm_w, ssm_norm_w)


def _post_bwd(dym, dys, hm, yssd, proj, ml_norm_w, ssm_norm_w, dproj, ts):
    s = hm.shape[0]

    def body(dym_ref, dys_ref, h_ref, ys_ref, o_ref, zm_ref, zs_ref, wm_ref, ws_ref, _,
             dh_ref, dyssd_ref, dp_ref, acc_ref):
        @pl.when(pl.program_id(0) == 0)
        def _():
            acc_ref[...] = jnp.zeros_like(acc_ref)
        hn, r = _group_norm(h_ref[...], ML_DV)
        so, zm, wm, d_ym = _sigmoid(o_ref[...]), zm_ref[...], wm_ref[...], dym_ref[...]
        sz = _silu(zm)
        hnw = hn * wm
        dp_ref[:, O_O:O_O + 2048] = _bf(d_ym * hnw * sz * so * (1.0 - so))
        dp_ref[:, O_ZM:O_ZM + 2048] = _bf(d_ym * so * hnw * _dsilu(zm))
        dhnw = d_ym * so * sz
        acc_ref[0:1, :] += jnp.sum(dhnw * hn, axis=0, keepdims=True)
        dhn = dhnw * wm
        dh_ref[...] = r * (dhn - hn * _group_mean(dhn * hn, ML_DV))
        ysv, zs, d_ys = ys_ref[...], zs_ref[...], dys_ref[...]
        szs = _silu(zs)
        pn, r2 = _group_norm(ysv * szs, GROUP_W)
        acc_ref[1:2, :] += jnp.sum(d_ys * pn, axis=0, keepdims=True)
        dpn = d_ys * ws_ref[...]
        dp = r2 * (dpn - pn * _group_mean(dpn * pn, GROUP_W))
        dyssd_ref[...] = dp * szs
        dp_ref[:, O_ZS:O_ZS + 2048] = _bf(dp * ysv * _dsilu(zs))

    tile = pl.BlockSpec((ts, 2048), lambda i: (i, 0))
    col = lambda off: pl.BlockSpec((ts, 2048), lambda i: (i, off // 2048))
    row = pl.BlockSpec((1, 2048), lambda i: (0, 0))
    sds = lambda dt: jax.ShapeDtypeStruct((s, 2048), dt)
    return pl.pallas_call(
        body, name="post_bwd", grid=(s // ts,),
        in_specs=[tile, tile, tile, tile, col(O_O), col(O_ZM), col(O_ZS), row, row, pl.BlockSpec(memory_space=pl.ANY)],
        out_specs=[tile, tile, pl.BlockSpec((ts, O_MG), lambda i: (i, 0)), pl.BlockSpec((8, 2048), lambda i: (0, 0))],
        out_shape=[sds(F32), sds(F32), jax.ShapeDtypeStruct(dproj.shape, dproj.dtype), jax.ShapeDtypeStruct((8, 2048), F32)],
        input_output_aliases={9: 2},
        compiler_params=_cparams("arbitrary"))(dym, dys, hm, yssd, proj, proj, proj, ml_norm_w, ssm_norm_w, dproj)


def _merge(x, ym, ys, proj, target, gate, final_w, wpm, wps, wo, ts):
    wpm_t, wps_t, wo_t = wpm.T, wps.T, wo.T
    s, d = x.shape

    def body(x_ref, ym_ref, ys_ref, mg_ref, t_ref, gate_ref, fw_ref, wpm_ref, wps_ref, wo_ref, wpmt_ref, wpst_ref, wot_ref,
             dres_ref, mer_ref, dmo_ref, dpm_ref, dps_ref, dym_ref, dys_ref, dmg_ref, acc_ref):
        @pl.when(pl.program_id(0) == 0)
        def _():
            acc_ref[...] = jnp.zeros_like(acc_ref)
        gm, gs = _sigmoid(mg_ref[:, 0:d]), _sigmoid(mg_ref[:, d:2 * d])
        pm = _nn(ym_ref[...], wpm_ref[...])
        ps = _nn(ys_ref[...], wps_ref[...])
        merged = _bf(gm * pm + gs * ps)
        mer_ref[...] = merged
        mo = _nn(merged, wo_ref[...])
        gate, fw = gate_ref[...], fw_ref[...]
        out = x_ref[...] + gate * mo
        r = lax.rsqrt(jnp.mean(out * out, axis=1, keepdims=True) + EPS)
        on = out * r
        diff = on * fw - t_ref[...]
        acc_ref[0:1, :] += jnp.sum(0.5 * jnp.sum(diff * diff, axis=1, keepdims=True) / d, axis=0, keepdims=True)
        dyv = diff * (1.0 / d)
        acc_ref[1:2, :] += jnp.sum(dyv * on, axis=0, keepdims=True)
        don = dyv * fw
        dout = r * (don - on * jnp.mean(don * on, axis=1, keepdims=True))
        dres_ref[...] = dout
        acc_ref[2:3, :] += jnp.sum(dout * mo, axis=0, keepdims=True)
        dmo = _bf(dout * gate)
        dmo_ref[...] = dmo
        dmer = _nn(dmo, wot_ref[...])
        dpm, dps = _bf(dmer * gm), _bf(dmer * gs)
        dpm_ref[...] = dpm
        dps_ref[...] = dps
        dmg_ref[:, 0:d] = _bf(dmer * pm * gm * (1.0 - gm))
        dmg_ref[:, d:2 * d] = _bf(dmer * ps * gs * (1.0 - gs))
        dym_ref[...] = _nn(dpm, wpmt_ref[...])
        dys_ref[...] = _nn(dps, wpst_ref[...])

    t1 = pl.BlockSpec((ts, d), lambda i: (i, 0))
    t2 = pl.BlockSpec((ts, 2 * d), lambda i: (i, 0))
    row = pl.BlockSpec((1, d), lambda i: (0, 0))
    whole = pl.BlockSpec(memory_space=pltpu.VMEM)
    sd = lambda w, dt: jax.ShapeDtypeStruct((s, w), dt)
    return pl.pallas_call(
        body, name="merge_fwd_bwd", grid=(s // ts,),
        in_specs=[t1, t2, t2, pl.BlockSpec((ts, 2 * d), lambda i: (i, O_MG // (2 * d))), t1, row, row] + [whole] * 6,
        out_specs=[t1, t1, t1, t1, t1, t2, t2, pl.BlockSpec((ts, 2 * d), lambda i: (i, O_MG // (2 * d))),
                   pl.BlockSpec((8, d), lambda i: (0, 0))],
        out_shape=[sd(d, F32), sd(d, BF16), sd(d, BF16), sd(d, BF16), sd(d, BF16), sd(2 * d, F32), sd(2 * d, F32),
                   sd(NP, BF16), jax.ShapeDtypeStruct((8, d), F32)],
        compiler_params=_cparams("arbitrary"))(x, ym, ys, proj, target, gate, final_w, wpm, wps, wo, wpm_t, wps_t, wo_t)


def _adamw(w, g, m, v, tr):
    rows, cols = w.shape[-2:]
    lead = (None,) * (w.ndim - 2)

    def body(w_ref, g_ref, m_ref, v_ref, d_ref, nm_ref, nv_ref):
        gv = g_ref[...]
        m2 = ADAM_B1 * m_ref[...] + (1.0 - ADAM_B1) * gv
        v2 = ADAM_B2 * v_ref[...] + (1.0 - ADAM_B2) * (gv * gv)
        m_hat = m2 / (1.0 - ADAM_B1 ** ADAM_STEP)
        v_hat = v2 / (1.0 - ADAM_B2 ** ADAM_STEP)
        d_ref[...] = -ADAM_LR * (m_hat / (jnp.sqrt(v_hat) + ADAM_EPS) + ADAM_WD * w_ref[...])
        nm_ref[...] = m2
        nv_ref[...] = v2

    tile = pl.BlockSpec(lead + (tr, cols), lambda i: (0,) * len(lead) + (i, 0))
    return pl.pallas_call(
        body, name="adamw", grid=(rows // tr,), in_specs=[tile] * 4, out_specs=[tile] * 3,
        out_shape=[jax.ShapeDtypeStruct(w.shape, F32)] * 3,
        compiler_params=_cparams("parallel"))(w, g.reshape(w.shape), m, v)


def _sum_parts(own, parts, tr, dtype=F32):
    p, rows, cols = parts.shape

    def body(*refs):
        p_ref, o_ref = refs[-2], refs[-1]
        acc = p_ref[0].astype(F32) if own is None else refs[0][...].astype(F32) + p_ref[0].astype(F32)
        for i in range(1, p):
            acc = acc + p_ref[i].astype(F32)
        o_ref[...] = acc.astype(dtype)

    tile = pl.BlockSpec((tr, cols), lambda i: (i, 0))
    ins = ([] if own is None else [tile]) + [pl.BlockSpec((p, tr, cols), lambda i: (0, i, 0))]
    args = ([] if own is None else [own]) + [parts]
    return pl.pallas_call(
        body, name="sum_parts", grid=(rows // tr,), in_specs=ins, out_specs=tile,
        out_shape=jax.ShapeDtypeStruct((rows, cols), dtype), compiler_params=_cparams("parallel"))(*args)


def _position():
    return lax.axis_index("x"), lax.axis_index("y"), lax.axis_index("c")


def _flip(pos, k):
    return tuple(1 - p if (k >> s) & 1 else p for p, s in zip(pos, (2, 1, 0)))


def _allgather8(block):
    rows, cols = block.shape

    def body(x_ref, o_ref, send_sems, recv_sems, local_sem):
        pos = _position()
        me = 4 * pos[0] + 2 * pos[1] + pos[2]
        mine = pltpu.make_async_copy(x_ref, o_ref.at[me], local_sem)
        mine.start()
        copies = [pltpu.make_async_remote_copy(src_ref=x_ref, dst_ref=o_ref.at[me], send_sem=send_sems.at[k - 1],
                                               recv_sem=recv_sems.at[k - 1], device_id=_flip(pos, k), device_id_type=MESH)
                  for k in range(1, N_DEV)]
        for cp in copies:
            cp.start()
        for cp in copies:
            cp.wait()
        mine.wait()

    vmem = pl.BlockSpec(memory_space=pltpu.VMEM)
    return pl.pallas_call(
        body, name="allgather8", in_specs=[vmem], out_specs=vmem,
        out_shape=jax.ShapeDtypeStruct((N_DEV, rows, cols), block.dtype),
        scratch_shapes=[pltpu.SemaphoreType.DMA((N_DEV - 1,)), pltpu.SemaphoreType.DMA((N_DEV - 1,)),
                        pltpu.SemaphoreType.DMA],
        compiler_params=pltpu.CompilerParams(vmem_limit_bytes=VMEM_LIMIT))(block)


COPY_BYTES = 1 << 20


def _row_chunks(rows, row_bytes):
    n = max(1, min(rows // 16, -(-rows * row_bytes // COPY_BYTES)))
    while rows % (16 * n):
        n -= 1
    return [(i * (rows // n), rows // n) for i in range(n)]


def _weight_gather(shards):
    n = len(shards)
    pieces = [_row_chunks(a.shape[1], a.shape[2] * a.dtype.itemsize) for a in shards]
    plan = [(a, k, r0, nr) for a in range(n) for k in range(1, N_CHIPS) for r0, nr in pieces[a]]

    def body(*refs):
        ins, outs = refs[:n], refs[n:2 * n]
        ici_send, ici_recv, d2d_send, d2d_recv = refs[2 * n:]
        pos = _position()
        chip, core = 2 * pos[0] + pos[1], pos[2]
        sibling = _flip(pos, 1)
        sent = []
        for i, (a, k, r0, nr) in enumerate(plan):
            cp = pltpu.make_async_remote_copy(
                src_ref=ins[a].at[core, pl.ds(r0, nr)], dst_ref=outs[a].at[chip, core, pl.ds(r0, nr)],
                send_sem=ici_send.at[i], recv_sem=ici_recv.at[i], device_id=_flip(pos, 2 * k), device_id_type=MESH)
            cp.start()
            sent.append(cp)
        passed = []
        for i, (a, k, r0, nr) in enumerate(plan):
            there = _flip(pos, 2 * k)
            landed = outs[a].at[2 * there[0] + there[1], core, pl.ds(r0, nr)]
            sent[i].wait_recv()
            cp = pltpu.make_async_remote_copy(src_ref=landed, dst_ref=landed, send_sem=d2d_send.at[i],
                                              recv_sem=d2d_recv.at[i], device_id=sibling, device_id_type=MESH)
            cp.start()
            passed.append(cp)
        for cp in passed:
            cp.wait()
        for cp in sent:
            cp.wait_send()

    hbm = pl.BlockSpec(memory_space=pl.ANY)
    sems = pltpu.SemaphoreType.DMA((len(plan),))
    return pl.pallas_call(
        body, name="weight_gather", in_specs=[hbm] * n, out_specs=[hbm] * n,
        out_shape=[jax.ShapeDtypeStruct((N_CHIPS,) + a.shape, a.dtype) for a in shards],
        scratch_shapes=[sems, sems, sems, sems],
        compiler_params=pltpu.CompilerParams(has_side_effects=True))(*shards)


def _exchange(name, arrays, out_shapes, plan, n_remote, n_local):
    n, m = len(arrays), len(out_shapes)

    def body(*refs):
        send_sems, recv_sems, local_sems = refs[n + m:]
        remote, local = plan(_position(), refs[:n], refs[n:n + m])
        assert (len(remote), len(local)) == (n_remote, n_local)
        copies = [pltpu.make_async_copy(src, dst, local_sems.at[i]) for i, (src, dst) in enumerate(local)]
        copies += [pltpu.make_async_remote_copy(src_ref=src, dst_ref=dst, send_sem=send_sems.at[i], recv_sem=recv_sems.at[i],
                                                device_id=dev, device_id_type=MESH)
                   for i, (src, dst, dev) in enumerate(remote)]
        for cp in copies:
            cp.start()
        for cp in copies:
            cp.wait()

    hbm = pl.BlockSpec(memory_space=pl.ANY)
    return pl.pallas_call(
        body, name=name, in_specs=[hbm] * n, out_specs=[hbm] * m, out_shape=out_shapes,
        scratch_shapes=[pltpu.SemaphoreType.DMA((n_remote,)), pltpu.SemaphoreType.DMA((n_remote,)),
                        pltpu.SemaphoreType.DMA((max(n_local, 1),))],
        compiler_params=pltpu.CompilerParams(has_side_effects=True))(*arrays)


def _pair_send(slabs):
    n = len(slabs)
    pieces = [_row_chunks(g.shape[2], g.shape[3] * g.dtype.itemsize) for g in slabs]

    def plan(pos, ins, outs):
        return [(ins[a].at[j, 1 - pos[2], pl.ds(r0, nr)], outs[a].at[j, pl.ds(r0, nr)], _flip(pos, 1))
                for a in range(n) for j in range(N_CHIPS) for r0, nr in pieces[a]], []

    return _exchange("pair_send", slabs, [jax.ShapeDtypeStruct((N_CHIPS,) + g.shape[2:], g.dtype) for g in slabs], plan,
                     N_CHIPS * sum(len(p) for p in pieces), 0)


def _chip_scatter(sums):
    n = len(sums)
    pieces = [_row_chunks(g.shape[1], g.shape[2] * g.dtype.itemsize) for g in sums]

    def plan(pos, ins, outs):
        remote = []
        for a in range(n):
            for k in range(1, N_CHIPS):
                to = _flip(pos, 2 * k)
                remote += [(ins[a].at[2 * to[0] + to[1], pl.ds(r0, nr)], outs[a].at[k - 1, pl.ds(r0, nr)], to)
                           for r0, nr in pieces[a]]
        return remote, []

    return _exchange("chip_scatter", sums, [jax.ShapeDtypeStruct((N_CHIPS - 1,) + g.shape[1:], g.dtype) for g in sums],
                     plan, (N_CHIPS - 1) * sum(len(p) for p in pieces), 0)


def _pair_exchange(halves):
    n = len(halves)
    pieces = [_row_chunks(h.shape[0], h.shape[1] * h.dtype.itemsize) for h in halves]

    def plan(pos, ins, outs):
        return [(ins[a].at[pl.ds(r0, nr)], outs[a].at[pl.ds(r0, nr)], _flip(pos, 1))
                for a in range(n) for r0, nr in pieces[a]], []

    return _exchange("pair_exchange", halves, [jax.ShapeDtypeStruct(h.shape, h.dtype) for h in halves], plan,
                     sum(len(p) for p in pieces), 0)


def _pack(arrays):
    flat = jnp.concatenate([a.reshape(-1).astype(F32) for a in arrays])
    size = -(-flat.shape[0] // (8 * LANE)) * (8 * LANE)
    return jnp.pad(flat, (0, size - flat.shape[0])).reshape(size // LANE, LANE)


def _unpack(buf, shapes):
    flat = buf.reshape(-1)
    out, off = [], 0
    for shp in shapes:
        n = math.prod(shp)
        out.append(flat[off:off + n].reshape(shp))
        off += n
    return out


def _unpack_rows(bufs, shapes):
    flat = bufs.reshape(bufs.shape[0], -1)
    out, off = [], 0
    for shp in shapes:
        n = math.prod(shp)
        out.append(flat[:, off:off + n].reshape((bufs.shape[0],) + shp))
        off += n
    return out


def _taps8(w):
    return jnp.pad(w, ((0, 8 - CONV_K), (0, 0)))


def _local_step(xs, tgt, scale, shift, gate, norm_w, w_in_p, b_in_p, ml_conv_w, ml_conv_b, ml_norm_w, ssm_conv_w,
                ssm_conv_b, ssm_a_log, ssm_d, ssm_norm_w, wpm, wps, wo, final_w):
    s = xs.shape[0]
    ts = min(512, s)
    tm = min(2048, s)
    u = _prenorm_fwd(xs, norm_w, scale, shift, ts)
    proj = _matmul_bias(u, w_in_p, b_in_p, tm, 512)
    mlw8, ssw8 = _taps8(ml_conv_w), _taps8(ssm_conv_w)
    qk, qk_dact = _conv_fwd(proj, O_QK, 2048, mlw8, ml_conv_b, ts)
    xbc, xbc_dact = _conv_fwd(proj, O_XBC, 3072, ssw8, ssm_conv_b, ts)
    gt = proj[:, O_IF:O_IF + LANE].T
    dtt = proj[:, O_DT:O_DT + LANE].T
    hm, cst, nm = _mlstm_fwd(qk, proj, gt)
    alog_row = jnp.pad(ssm_a_log, ((0, 0), (0, LANE - SSM_HEADS)))
    alog_col = alog_row.reshape(LANE, 1)
    dskip_x = jnp.repeat(ssm_d[0], SSM_HEADDIM)[None]
    expand = _head_expand()
    yssd, sst = _ssd_fwd(xbc, proj, dtt, alog_row, alog_col, dskip_x, expand)
    tp = min(128, s)
    ym, ys = _post_fwd(hm, yssd, proj, ml_norm_w, ssm_norm_w, tp)
    dxres, merged, dmo, dpm, dps, dym, dys, dproj, acc_m = _merge(xs, ym, ys, proj, tgt, gate, final_w, wpm, wps, wo, tp)
    dh, dyssd, dproj, acc_p = _post_bwd(dym, dys, hm, yssd, proj, ml_norm_w, ssm_norm_w, dproj, tp)
    dqk, dproj, dif = _mlstm_bwd(qk, proj, gt, hm, dh, cst, nm, dproj)
    dxbc, ddt, accd, acca = _ssd_bwd(xbc, proj, dtt, alog_row, alog_col, dskip_x, expand, expand.T, dyssd, sst)
    dproj, acc_cq = _conv_bwd(proj, O_QK, 2048, mlw8, qk_dact, dqk, dproj, ts)
    dproj, acc_cx = _conv_bwd(proj, O_XBC, 3072, ssw8, xbc_dact, dxbc, dproj, ts)
    dproj = dproj.at[:, O_IF:O_IF + SMALL_W].set(dif).at[:, O_DT:O_DT + SMALL_W].set(ddt)
    gw_in_p, gb_in_p = _matmul_tn(u.T, dproj, tm, 512, with_colsum=True, a_is_transposed=True)
    du = _matmul_nt(dproj, w_in_p, tm, 512)
    grad_x, acc_n = _prenorm_bwd(du, xs, dxres, norm_w, scale, ts)
    g_wpm = _matmul_tn(ym, dpm, tm, 512)
    g_wps = _matmul_tn(ys, dps, tm, 512)
    g_wo = _matmul_tn(merged, dmo, tm, 512)
    a_coef = -jnp.exp(ssm_a_log[0])
    small = dict(
        mod=jnp.concatenate([acc_n[2], acc_n[1], acc_m[2]]), norm_w=acc_n[0], b_in=_unpad_cols(gb_in_p[0]),
        ml_conv_w=acc_cq[0:CONV_K], ml_conv_b=acc_cq[CONV_K], ml_norm_w=acc_p[0], ssm_conv_w=acc_cx[0:CONV_K],
        ssm_conv_b=acc_cx[CONV_K], ssm_a_log=acca[0, :SSM_HEADS] * a_coef,
        ssm_d=accd[0].reshape(SSM_HEADS, SSM_HEADDIM).sum(axis=1), ssm_norm_w=acc_p[1], final_w=acc_m[1], loss=acc_m[0, 0:1])
    return grad_x, small, gw_in_p, g_wpm, g_wps, g_wo


WEIGHTS = ("norm_w", "ada_w", "ada_b", "w_in", "b_in", "ml_conv_w", "ml_conv_b", "ml_norm_w", "ssm_conv_w", "ssm_conv_b",
           "ssm_a_log", "ssm_d", "ssm_norm_w", "w_proj_m", "w_proj_s", "w_out", "final_w")
LARGE = ("ada_w", "w_in", "w_proj_m", "w_proj_s", "w_out")
SMALL_SUMS = (("mod", (3 * D_MODEL,)), ("norm_w", (D_MODEL,)), ("b_in", (IN_WIDTH,)), ("ml_conv_w", (CONV_K, 2048)),
              ("ml_conv_b", (2048,)), ("ml_norm_w", (2048,)), ("ssm_conv_w", (CONV_K, 3072)), ("ssm_conv_b", (3072,)),
              ("ssm_a_log", (SSM_HEADS,)), ("ssm_d", (SSM_HEADS,)), ("ssm_norm_w", (2048,)), ("final_w", (D_MODEL,)),
              ("loss", (1,)))


def kernel(x, c, norm_w, ada_w, ada_b, w_in, b_in, ml_conv_w, ml_conv_b, ml_norm_w, ssm_conv_w, ssm_conv_b, ssm_a_log, ssm_d, ssm_norm_w, w_proj_m, w_proj_s, w_out, final_w, loss_target, m_norm_w, m_ada_w, m_ada_b, m_w_in, m_b_in, m_ml_conv_w, m_ml_conv_b, m_ml_norm_w, m_ssm_conv_w, m_ssm_conv_b, m_ssm_a_log, m_ssm_d, m_ssm_norm_w, m_w_proj_m, m_w_proj_s, m_w_out, m_final_w, v_norm_w, v_ada_w, v_ada_b, v_w_in, v_b_in, v_ml_conv_w, v_ml_conv_b, v_ml_norm_w, v_ssm_conv_w, v_ssm_conv_b, v_ssm_a_log, v_ssm_d, v_ssm_norm_w, v_w_proj_m, v_w_proj_s, v_w_out, v_final_w):
    w = dict(norm_w=norm_w, ada_w=ada_w, ada_b=ada_b, w_in=w_in, b_in=b_in, ml_conv_w=ml_conv_w, ml_conv_b=ml_conv_b,
             ml_norm_w=ml_norm_w, ssm_conv_w=ssm_conv_w, ssm_conv_b=ssm_conv_b, ssm_a_log=ssm_a_log, ssm_d=ssm_d,
             ssm_norm_w=ssm_norm_w, w_proj_m=w_proj_m, w_proj_s=w_proj_s, w_out=w_out, final_w=final_w)
    m = dict(zip(WEIGHTS, (m_norm_w, m_ada_w, m_ada_b, m_w_in, m_b_in, m_ml_conv_w, m_ml_conv_b, m_ml_norm_w, m_ssm_conv_w,
                           m_ssm_conv_b, m_ssm_a_log, m_ssm_d, m_ssm_norm_w, m_w_proj_m, m_w_proj_s, m_w_out, m_final_w)))
    v = dict(zip(WEIGHTS, (v_norm_w, v_ada_w, v_ada_b, v_w_in, v_b_in, v_ml_conv_w, v_ml_conv_b, v_ml_norm_w, v_ssm_conv_w,
                           v_ssm_conv_b, v_ssm_a_log, v_ssm_d, v_ssm_norm_w, v_w_proj_m, v_w_proj_s, v_w_out, v_final_w)))
    pos = _position()
    chip = 2 * pos[0] + pos[1]
    dev = 2 * chip + pos[2]
    mlw_cols, ssw_cols, ada_cols = ml_conv_w.shape[2], ssm_conv_w.shape[2], ada_w.shape[2]

    g0 = _allgather8(_pack([c, ml_conv_w, ssm_conv_w]))
    c_all, mlw_all, ssw_all = _unpack_rows(g0, [(D_MODEL,), (CONV_K, mlw_cols), (CONV_K, ssw_cols)])
    ml_conv_full = mlw_all[0::2].transpose(1, 0, 2).reshape(CONV_K, N_CHIPS * mlw_cols)
    ssm_conv_full = ssw_all[0::2].transpose(1, 0, 2).reshape(CONV_K, N_CHIPS * ssw_cols)

    ada_b_mine = lax.dynamic_slice_in_dim(ada_b, chip * ada_cols, ada_cols, axis=1)
    g1 = _allgather8(_ada_fwd(c_all, ada_w[0], ada_b_mine))
    mod = lax.dynamic_index_in_dim(g1[0::2], dev, axis=1, keepdims=False).reshape(1, 3 * D_MODEL)
    shift, scale, gate = mod[:, :D_MODEL], mod[:, D_MODEL:2 * D_MODEL], mod[:, 2 * D_MODEL:]

    mine = [_bf(a[0]).reshape(2, a.shape[1] // 2, a.shape[2]) for a in (w_in, w_proj_m, w_proj_s, w_out)]
    gw = [lax.dynamic_update_index_in_dim(got, own, chip, 0).reshape(N_CHIPS, -1, own.shape[-1])
          for got, own in zip(_weight_gather(mine), mine)]
    w_in_p = _shards_to_padded(gw[0])
    wpm, wps, wo = (a.reshape(-1, D_MODEL) for a in gw[1:])

    grad_x, small, g_w_in, g_wpm, g_wps, g_wo = _local_step(
        x[0], loss_target[0], scale, shift, gate, norm_w, w_in_p, _pad_cols(b_in), ml_conv_full, ml_conv_b, ml_norm_w,
        ssm_conv_full, ssm_conv_b, ssm_a_log, ssm_d, ssm_norm_w, wpm, wps, wo, final_w[None])

    g2 = _allgather8(_pack([small[name] for name, _ in SMALL_SUMS]))
    total = dict(zip([name for name, _ in SMALL_SUMS], _unpack(_sum_parts(None, g2, g2.shape[1]), [s for _, s in SMALL_SUMS])))
    dmod_all = g2[:, :3 * D_MODEL // LANE].reshape(N_DEV, 3 * D_MODEL)
    grads = dict(total)
    grads["ada_b"] = total["mod"]
    grads["ml_conv_w"] = lax.dynamic_slice_in_dim(total["ml_conv_w"], chip * mlw_cols, mlw_cols, axis=1)
    grads["ssm_conv_w"] = lax.dynamic_slice_in_dim(total["ssm_conv_w"], chip * ssw_cols, ssw_cols, axis=1)
    grads["ada_w"] = _ada_bwd(c_all, lax.dynamic_slice_in_dim(dmod_all, chip * ada_cols, ada_cols, axis=1))

    split = lambda g, rows: _bf(g).reshape(N_CHIPS, 2, rows // (2 * N_CHIPS), g.shape[-1])
    slabs = [split(_padded_to_shards(_bf(g_w_in)), N_CHIPS * D_MODEL),
             split(g_wpm, g_wpm.shape[0]), split(g_wps, g_wps.shape[0]), split(g_wo, g_wo.shape[0])]
    pair_sums = []
    for slab, rec in zip(slabs, _pair_send(slabs)):
        kept = lax.dynamic_index_in_dim(slab, pos[2], 1, keepdims=False)
        rows = kept.shape[0] * kept.shape[1]
        both = _sum_parts(kept.reshape(rows, -1), rec.reshape(1, rows, -1), 32, BF16)
        pair_sums.append(both.reshape(kept.shape))
    halves = []
    for both, rec in zip(pair_sums, _chip_scatter(pair_sums)):
        halves.append(_sum_parts(lax.dynamic_index_in_dim(both, chip, 0, keepdims=False), rec, 32))
    for name, half, other in zip(("w_in", "w_proj_m", "w_proj_s", "w_out"), halves, _pair_exchange(halves)):
        grads[name] = jnp.where(pos[2] == 0, jnp.concatenate([half, other]), jnp.concatenate([other, half]))

    delta, new_m, new_v = {}, {}, {}
    for name in LARGE:
        if w[name].shape[-1] % LANE:
            flat = lambda a: a.reshape(a.shape[-2:]).T.reshape(-1, LANE)
            back = lambda a: a.reshape(w[name].shape[-1], w[name].shape[-2]).T.reshape(w[name].shape)
            rows = w[name].size // LANE
            tr = max(t for t in range(8, 4096, 8) if rows % t == 0)
            g_flat = flat(grads[name])
            delta[name], new_m[name], new_v[name] = (back(a) for a in _adamw(flat(w[name]), g_flat, flat(m[name]), flat(v[name]), tr))
            grads[name] = back(g_flat)
        else:
            delta[name], new_m[name], new_v[name] = _adamw(w[name], grads[name], m[name], v[name], 64)
    rest = [name for name in WEIGHTS if name not in LARGE]
    packed = [_pack([t[name] for name in rest]) for t in (w, grads, m, v)]
    for out, buf in zip((delta, new_m, new_v), _adamw(*packed, packed[0].shape[0])):
        out.update(zip(rest, _unpack(buf, [w[name].shape for name in rest])))
    loss = total["loss"][0]
    return (loss, grad_x[None], *[grads[name].reshape(w[name].shape) for name in WEIGHTS], *[delta[name] for name in WEIGHTS],
            *[new_m[name] for name in WEIGHTS], *[new_v[name] for name in WEIGHTS])
```

```python
import functools
import math

import jax
import jax.numpy as jnp
from jax import lax
from jax.experimental import pallas as pl
from jax.experimental.pallas import tpu as pltpu

F32 = jnp.float32
BF16 = jnp.bfloat16
HI = lax.Precision.HIGHEST
MESH = pl.DeviceIdType.MESH

D_MODEL = 1024
EPS = 1e-6
CONV_K = 4
ML_HEADS = 8
ML_DQK = 128
ML_DV = 256
SSM_HEADS = 32
SSM_HEADDIM = 64
SSM_GROUPS = 4
SSM_STATE = 128
IN_WIDTH = 15408
N_CHIPS = 4
N_DEV = 8
ADAM_LR, ADAM_B1, ADAM_B2, ADAM_EPS, ADAM_WD, ADAM_STEP = 0.001, 0.9, 0.999, 1e-08, 0.01, 10

O_O, O_ZM, O_ZS, O_MG, O_QK, O_V, O_XBC, O_IF, O_DT = 0, 2048, 4096, 6144, 8192, 10240, 12288, 15360, 15616
SMALL_W = 256
NP = 15872
LANE = 128
CHUNK = 128
NEG = -1e30
VMEM_LIMIT = 48 * 1024 * 1024


def _cparams(*sem):
    return pltpu.CompilerParams(dimension_semantics=sem, vmem_limit_bytes=VMEM_LIMIT)


def _pad_cols(w):
    z = lambda n: jnp.zeros(w.shape[:-1] + (n,), w.dtype)
    return jnp.concatenate([w[..., 4096:8192], w[..., 11280:13328], w[..., 13360:15408], w[..., :4096], w[..., 8208:11280],
                            w[..., 8192:8208], z(SMALL_W - 16), w[..., 13328:13360], z(SMALL_W - 32)], axis=-1)


def _unpad_cols(g):
    return jnp.concatenate([g[..., O_QK:O_QK + 4096], g[..., O_O:O_O + 4096], g[..., O_IF:O_IF + 16],
                            g[..., O_XBC:O_XBC + 3072], g[..., O_ZS:O_ZS + 2048], g[..., O_DT:O_DT + 32],
                            g[..., O_MG:O_MG + 2048]], axis=-1)


PADDED_SEGMENTS = ((4096, 8192, 0), (11280, 13328, 0), (13360, 15408, 0), (0, 4096, 0), (8208, 11280, 0),
                   (8192, 8208, SMALL_W - 16), (13328, 13360, SMALL_W - 32))
SHARD_W = IN_WIDTH // N_CHIPS


def _shards_to_padded(shards):
    parts = []
    for first, last, pad in PADDED_SEGMENTS:
        for j in range(N_CHIPS):
            lo, hi = max(first, j * SHARD_W), min(last, (j + 1) * SHARD_W)
            if lo < hi:
                parts.append(shards[j][:, lo - j * SHARD_W:hi - j * SHARD_W])
        if pad:
            parts.append(jnp.zeros((shards.shape[1], pad), shards.dtype))
    return jnp.concatenate(parts, axis=1)


def _padded_to_shards(g):
    offsets, off = {}, 0
    for first, last, pad in PADDED_SEGMENTS:
        offsets[first] = off
        off += last - first + pad
    shards = []
    for j in range(N_CHIPS):
        parts = []
        for first, last, _ in sorted(PADDED_SEGMENTS):
            lo, hi = max(first, j * SHARD_W), min(last, (j + 1) * SHARD_W)
            if lo < hi:
                parts.append(g[:, offsets[first] + lo - first:offsets[first] + hi - first])
        shards.append(jnp.concatenate(parts, axis=1))
    return jnp.stack(shards)


def _sigmoid(x):
    return 1.0 / (1.0 + jnp.exp(-x))


def _silu(x):
    return x * _sigmoid(x)


def _dsilu(x):
    s = _sigmoid(x)
    return s + x * s * (1.0 - s)


def _softplus(x):
    return jnp.maximum(x, 0.0) + jnp.log(1.0 + jnp.exp(-jnp.abs(x)))


def _logsigmoid(x):
    return jnp.minimum(x, 0.0) - jnp.log(1.0 + jnp.exp(-jnp.abs(x)))


def _dot(a, b, dims, precision=None):
    return lax.dot_general(a, b, (dims, ((), ())), preferred_element_type=F32, precision=precision)


def _nn(a, b, precision=None):
    return _dot(a, b, ((1,), (0,)), precision)


def _nt(a, b, precision=None):
    return _dot(a, b, ((1,), (1,)), precision)


def _tn(a, b, precision=None):
    return _dot(a, b, ((0,), (0,)), precision)


def _bf(x):
    return x.astype(BF16)


def _split(x, terms):
    parts = []
    for _ in range(terms):
        part = _bf(x)
        parts.append(part)
        x = x - part.astype(F32)
    return parts


def _pick_right(x, pick, terms):
    pick = _bf(pick)
    out = None
    for part in _split(x, terms):
        out = _nn(part, pick) if out is None else out + _nn(part, pick)
    return out


def _pick_left(pick, x, terms):
    pick = _bf(pick)
    out = None
    for part in _split(x, terms):
        out = _nn(pick, part) if out is None else out + _nn(pick, part)
    return out


def _lane_col(x, lane):
    idx = lax.broadcasted_iota(jnp.int32, x.shape, 1)
    return jnp.sum(jnp.where(idx == lane, x, 0.0), axis=1, keepdims=True)


def _tri(n, upper):
    r = lax.broadcasted_iota(jnp.int32, (n, n), 0)
    c = lax.broadcasted_iota(jnp.int32, (n, n), 1)
    return jnp.where((r <= c) if upper else (r >= c), 1.0, 0.0).astype(F32)


def _eye(n):
    return jnp.where(lax.broadcasted_iota(jnp.int32, (n, n), 0) == lax.broadcasted_iota(jnp.int32, (n, n), 1), 1.0, 0.0)


def _sum_all(x):
    return jnp.sum(jnp.sum(x, axis=1, keepdims=True), axis=0, keepdims=True)


def _crossing(p):
    L = p.shape[0]
    hi = _bf(p)
    lo = _bf(p - hi.astype(F32))
    upper = _bf(_tri(L, True))
    below = _nn(upper, hi) + _nn(upper, lo)
    strict = lax.broadcasted_iota(jnp.int32, (L, L), 0) > lax.broadcasted_iota(jnp.int32, (L, L), 1)
    return jnp.sum(jnp.where(strict, below, 0.0), axis=1, keepdims=True)


def _matmul_bias(a, w, bias, tm, tn):
    m, k = a.shape
    n = w.shape[1]

    def body(a_ref, w_ref, b_ref, o_ref):
        o_ref[...] = _nn(a_ref[...], w_ref[...]) + b_ref[...]

    return pl.pallas_call(
        body, name="matmul_bias", grid=(m // tm, n // tn),
        in_specs=[pl.BlockSpec((tm, k), lambda i, j: (i, 0)), pl.BlockSpec((k, tn), lambda i, j: (0, j)),
                  pl.BlockSpec((1, tn), lambda i, j: (0, j))],
        out_specs=pl.BlockSpec((tm, tn), lambda i, j: (i, j)),
        out_shape=jax.ShapeDtypeStruct((m, n), F32),
        compiler_params=_cparams("parallel", "arbitrary"))(a, w, bias)


def _matmul_nt(a, w, tm, tk, after=None):
    m, n = a.shape
    k = w.shape[0]

    def body(a_ref, w_ref, *rest):
        o_ref = rest[-1]

        @pl.when(pl.program_id(1) == 0)
        def _():
            o_ref[...] = jnp.zeros_like(o_ref)
        o_ref[...] += _nt(a_ref[...], w_ref[...])

    extra = [] if after is None else [after]
    return pl.pallas_call(
        body, name="matmul_nt", grid=(m // tm, n // tk),
        in_specs=[pl.BlockSpec((tm, tk), lambda i, j: (i, j)), pl.BlockSpec((k, tk), lambda i, j: (0, j))]
        + [pl.BlockSpec(memory_space=pl.ANY)] * len(extra),
        out_specs=pl.BlockSpec((tm, k), lambda i, j: (i, 0)),
        out_shape=jax.ShapeDtypeStruct((m, k), F32),
        compiler_params=_cparams("parallel", "arbitrary"))(a, w, *extra)


def _matmul_tn(a, b, tm, tn, with_colsum=False, a_is_transposed=False):
    k, m = a.shape if a_is_transposed else a.shape[::-1]
    n = b.shape[1]

    def body(a_ref, b_ref, o_ref, *rest):
        first = pl.program_id(1) == 0

        @pl.when(first)
        def _():
            o_ref[...] = jnp.zeros_like(o_ref)
        o_ref[...] += _nn(a_ref[...], b_ref[...]) if a_is_transposed else _tn(a_ref[...], b_ref[...])
        if with_colsum:
            s_ref = rest[0]

            @pl.when(first)
            def _():
                s_ref[...] = jnp.zeros_like(s_ref)
            s_ref[...] += jnp.sum(b_ref[...].astype(F32), axis=0, keepdims=True)

    out_specs = [pl.BlockSpec((k, tn), lambda j, i: (0, j))]
    out_shape = [jax.ShapeDtypeStruct((k, n), F32)]
    if with_colsum:
        out_specs.append(pl.BlockSpec((1, tn), lambda j, i: (0, j)))
        out_shape.append(jax.ShapeDtypeStruct((1, n), F32))
    out = pl.pallas_call(
        body, name="matmul_tn", grid=(n // tn, m // tm),
        in_specs=[pl.BlockSpec((k, tm), lambda j, i: (0, i)) if a_is_transposed else pl.BlockSpec((tm, k), lambda j, i: (i, 0)),
                  pl.BlockSpec((tm, tn), lambda j, i: (i, j))],
        out_specs=out_specs, out_shape=out_shape,
        compiler_params=_cparams("parallel", "arbitrary"))(a, b)
    return out if with_colsum else out[0]


def _ada_fwd(c_all, ada_w, ada_b):
    def body(c_ref, w_ref, b_ref, o_ref):
        o_ref[...] = _nn(_bf(_silu(c_ref[...])), _bf(w_ref[...])) + b_ref[...]

    return pl.pallas_call(body, name="ada_fwd", out_shape=jax.ShapeDtypeStruct((c_all.shape[0], ada_w.shape[1]), F32),
                          compiler_params=_cparams())(c_all, ada_w, ada_b)


def _ada_bwd(c_all, dmod):
    def body(c_ref, d_ref, o_ref):
        o_ref[...] = _tn(_bf(_silu(c_ref[...])), _bf(d_ref[...]))

    return pl.pallas_call(body, name="ada_bwd", out_shape=jax.ShapeDtypeStruct((c_all.shape[1], dmod.shape[1]), F32),
                          compiler_params=_cparams())(c_all, dmod)


def _prenorm_fwd(x, norm_w, scale, shift, ts):
    s, d = x.shape

    def body(x_ref, nw_ref, sc_ref, sh_ref, u_ref):
        xv = x_ref[...]
        r = lax.rsqrt(jnp.mean(xv * xv, axis=1, keepdims=True) + EPS)
        u_ref[...] = _bf(xv * r * nw_ref[...] * (1.0 + sc_ref[...]) + sh_ref[...])

    row = pl.BlockSpec((1, d), lambda i: (0, 0))
    return pl.pallas_call(
        body, name="prenorm_fwd", grid=(s // ts,),
        in_specs=[pl.BlockSpec((ts, d), lambda i: (i, 0)), row, row, row],
        out_specs=pl.BlockSpec((ts, d), lambda i: (i, 0)), out_shape=jax.ShapeDtypeStruct((s, d), BF16),
        compiler_params=_cparams("parallel"))(x, norm_w, scale, shift)


def _prenorm_bwd(du, x, dxres, norm_w, scale, ts):
    s, d = x.shape

    def body(du_ref, x_ref, dr_ref, nw_ref, sc_ref, gx_ref, acc_ref):
        @pl.when(pl.program_id(0) == 0)
        def _():
            acc_ref[...] = jnp.zeros_like(acc_ref)
        xv, duv = x_ref[...], du_ref[...]
        r = lax.rsqrt(jnp.mean(xv * xv, axis=1, keepdims=True) + EPS)
        xn = xv * r
        nw, sc1 = nw_ref[...], 1.0 + sc_ref[...]
        dxn = duv * (nw * sc1)
        gx_ref[...] = r * (dxn - xn * jnp.mean(dxn * xn, axis=1, keepdims=True)) + dr_ref[...]
        t = duv * xn
        acc_ref[0:1, :] += jnp.sum(t, axis=0, keepdims=True) * sc1
        acc_ref[1:2, :] += jnp.sum(t, axis=0, keepdims=True) * nw
        acc_ref[2:3, :] += jnp.sum(duv, axis=0, keepdims=True)

    tile = pl.BlockSpec((ts, d), lambda i: (i, 0))
    row = pl.BlockSpec((1, d), lambda i: (0, 0))
    return pl.pallas_call(
        body, name="prenorm_bwd", grid=(s // ts,),
        in_specs=[tile, tile, tile, row, row],
        out_specs=[tile, pl.BlockSpec((8, d), lambda i: (0, 0))],
        out_shape=[jax.ShapeDtypeStruct((s, d), F32), jax.ShapeDtypeStruct((8, d), F32)],
        compiler_params=_cparams("arbitrary"))(du, x, dxres, norm_w, scale)


CONV_CB = 512


def _conv_taps(buf_ref, ts):
    return [buf_ref[pl.ds(8 - (CONV_K - 1) + j, ts), :] for j in range(CONV_K)]


def _conv_fwd(proj, col0, width, w8, b, ts):
    s = proj.shape[0]
    cb = CONV_CB
    nt = s // ts

    def body(x_ref, w_ref, b_ref, o_ref, ds_ref, buf_ref):
        @pl.when(pl.program_id(1) == 0)
        def _():
            buf_ref[0:8, :] = jnp.zeros((8, cb), F32)
        buf_ref[pl.ds(8, ts), :] = x_ref[...]
        acc = b_ref[...] + jnp.zeros((ts, cb), F32)
        for j, tap in enumerate(_conv_taps(buf_ref, ts)):
            acc = acc + tap * w_ref[j:j + 1, :]
        sg = _sigmoid(acc)
        o_ref[...] = acc * sg
        ds_ref[...] = _bf(sg + acc * sg * (1.0 - sg))
        buf_ref[0:8, :] = x_ref[pl.ds(ts - 8, 8), :]

    c0 = col0 // cb
    tile = pl.BlockSpec((ts, cb), lambda c, i: (i, c))
    return pl.pallas_call(
        body, name="conv_fwd", grid=(width // cb, nt),
        in_specs=[pl.BlockSpec((ts, cb), lambda c, i: (i, c0 + c)), pl.BlockSpec((8, cb), lambda c, i: (0, c)),
                  pl.BlockSpec((1, cb), lambda c, i: (0, c))],
        out_specs=[tile, tile],
        out_shape=[jax.ShapeDtypeStruct((s, width), F32), jax.ShapeDtypeStruct((s, width), BF16)],
        scratch_shapes=[pltpu.VMEM((ts + 8, cb), F32)],
        compiler_params=_cparams("parallel", "arbitrary"))(proj, w8, b)


def _conv_bwd(proj, col0, width, w8, dact, dpost, dproj, ts):
    s = proj.shape[0]
    cb = CONV_CB
    nt = s // ts
    c0 = col0 // cb

    def body(x_ref, da_ref, dp_ref, w_ref, _, dx_ref, acc_ref, dbuf_ref):
        @pl.when(pl.program_id(1) == 0)
        def _():
            acc_ref[...] = jnp.zeros_like(acc_ref)
            dbuf_ref[pl.ds(ts, 8), :] = jnp.zeros((8, cb), F32)
        dconv = dp_ref[...] * da_ref[...].astype(F32)
        acc_ref[CONV_K:CONV_K + 1, :] += jnp.sum(dconv, axis=0, keepdims=True)
        dbuf_ref[pl.ds(0, ts), :] = dconv
        xv = x_ref[...]
        dx = jnp.zeros((ts, cb), F32)
        for j in range(CONV_K):
            shifted = dbuf_ref[pl.ds(CONV_K - 1 - j, ts), :]
            dx = dx + shifted * w_ref[j:j + 1, :]
            acc_ref[j:j + 1, :] += jnp.sum(xv * shifted, axis=0, keepdims=True)
        dx_ref[...] = _bf(dx)
        dbuf_ref[pl.ds(ts, 8), :] = dconv[0:8, :]

    tile = pl.BlockSpec((ts, cb), lambda c, i: (nt - 1 - i, c))
    wide = pl.BlockSpec((ts, cb), lambda c, i: (nt - 1 - i, c0 + c))
    return pl.pallas_call(
        body, name="conv_bwd", grid=(width // cb, nt),
        in_specs=[wide, tile, tile, pl.BlockSpec((8, cb), lambda c, i: (0, c)), pl.BlockSpec(memory_space=pl.ANY)],
        out_specs=[wide, pl.BlockSpec((8, cb), lambda c, i: (0, c))],
        out_shape=[jax.ShapeDtypeStruct(dproj.shape, dproj.dtype), jax.ShapeDtypeStruct((8, width), F32)],
        input_output_aliases={4: 0},
        scratch_shapes=[pltpu.VMEM((ts + 8, cb), F32)],
        compiler_params=_cparams("parallel", "arbitrary"))(proj, dact, dpost, w8, dproj)


def _mlstm_gates(gif_ref, gt_ref, a_scr, at_scr):
    L = gif_ref.shape[0]
    fb = _logsigmoid(gif_ref[...])
    a_scr[...] = _pick_left(_tri(L, False), fb, 3)
    at_scr[...] = _pick_right(_logsigmoid(gt_ref[...]), _tri(L, True), 3)
    return jnp.sum(fb, axis=0, keepdims=True)


def _mlstm_head(h, qk_ref, v_ref, gif, gt_ref, a, at_scr, a_last_row, c_mat, n_row, m_prev):
    L = gif.shape[0]
    q = qk_ref[:, h * ML_DQK:(h + 1) * ML_DQK] * (ML_DQK ** -0.5)
    k = qk_ref[:, (ML_HEADS + h) * ML_DQK:(ML_HEADS + h + 1) * ML_DQK]
    v = v_ref[:, h * ML_DV:(h + 1) * ML_DV]
    i_col, a_col = _lane_col(gif, h), _lane_col(a, ML_HEADS + h)
    i_row, a_row = gt_ref[h:h + 1, :], at_scr[ML_HEADS + h:ML_HEADS + h + 1, :]
    causal = lax.broadcasted_iota(jnp.int32, (L, L), 0) >= lax.broadcasted_iota(jnp.int32, (L, L), 1)
    dmat = jnp.where(causal, a_col - a_row + i_row, NEG)
    inter = a_col + m_prev
    m_t = jnp.maximum(inter, jnp.max(dmat, axis=1, keepdims=True))
    w_intra = jnp.exp(dmat - m_t)
    w_inter = jnp.exp(inter - m_t)
    sc = _nt(_bf(q), _bf(k)) * w_intra
    den = jnp.sum(sc, axis=1, keepdims=True) + w_inter * jnp.sum(q * n_row, axis=1, keepdims=True)
    floor = jnp.exp(-m_t)
    a_last = _lane_col(a_last_row, ML_HEADS + h)
    g = a_last - a_col + i_col
    m_new = jnp.maximum(a_last + m_prev, jnp.max(g, axis=0, keepdims=True))
    wk = jnp.exp(g - m_new)
    decay = jnp.exp(a_last + m_prev - m_new)
    return dict(q=q, k=k, v=v, w_intra=w_intra, w_inter=w_inter, sc=sc, den=den, floor=floor, m_new=m_new, wk=wk,
                decay=decay)


def _state_tile(n_row, m11):
    r = lax.broadcasted_iota(jnp.int32, (8, LANE), 0)
    return jnp.where(r == 0, n_row, jnp.where(r == 1, m11, 0.0))


def _mlstm_fwd(qk, proj, gt):
    s = qk.shape[0]
    L = CHUNK
    nc = s // L

    def body(qk_ref, v_ref, gif_ref, gt_ref, h_ref, cst_ref, nm_ref, c_scr, nm_scr, a_scr, at_scr):
        @pl.when(pl.program_id(0) == 0)
        def _():
            c_scr[...] = jnp.zeros_like(c_scr)
            nm_scr[...] = jnp.zeros_like(nm_scr)
        a_last_row = _mlstm_gates(gif_ref, gt_ref, a_scr, at_scr)
        gif, a = gif_ref[...], a_scr[...]
        for h in range(ML_HEADS):
            c_mat, n_row = c_scr[h], nm_scr[h, 0:1, :]
            m_prev = jnp.max(nm_scr[h, 1:2, :], axis=1, keepdims=True)
            cst_ref[0, h] = c_mat
            nm_ref[0, h] = nm_scr[h]
            t = _mlstm_head(h, qk_ref, v_ref, gif, gt_ref, a, at_scr, a_last_row, c_mat, n_row, m_prev)
            num = _nn(_bf(t["sc"]), _bf(t["v"])) + t["w_inter"] * _nn(_bf(t["q"]), _bf(c_mat))
            h_ref[:, h * ML_DV:(h + 1) * ML_DV] = num / jnp.maximum(jnp.abs(t["den"]), t["floor"])
            kw = t["k"] * t["wk"]
            c_scr[h] = t["decay"] * c_mat + _tn(_bf(kw), _bf(t["v"]))
            nm_scr[h] = _state_tile(t["decay"] * n_row + jnp.sum(kw, axis=0, keepdims=True), t["m_new"])

    return pl.pallas_call(
        body, name="mlstm_fwd", grid=(nc,),
        in_specs=[pl.BlockSpec((L, 2048), lambda c: (c, 0)), pl.BlockSpec((L, 2048), lambda c: (c, O_V // 2048)),
                  pl.BlockSpec((L, LANE), lambda c: (c, O_IF // LANE)), pl.BlockSpec((LANE, L), lambda c: (0, c))],
        out_specs=[pl.BlockSpec((L, 2048), lambda c: (c, 0)),
                   pl.BlockSpec((1, ML_HEADS, ML_DQK, ML_DV), lambda c: (c, 0, 0, 0)),
                   pl.BlockSpec((1, ML_HEADS, 8, LANE), lambda c: (c, 0, 0, 0))],
        out_shape=[jax.ShapeDtypeStruct((s, 2048), F32), jax.ShapeDtypeStruct((nc, ML_HEADS, ML_DQK, ML_DV), F32),
                   jax.ShapeDtypeStruct((nc, ML_HEADS, 8, LANE), F32)],
        scratch_shapes=[pltpu.VMEM((ML_HEADS, ML_DQK, ML_DV), F32), pltpu.VMEM((ML_HEADS, 8, LANE), F32),
                        pltpu.VMEM((L, LANE), F32), pltpu.VMEM((LANE, L), F32)],
        compiler_params=_cparams("arbitrary"))(qk, proj, proj, gt)


def _mlstm_bwd(qk, proj, gt, hout, dh, cst, nm, dproj):
    s = qk.shape[0]
    L = CHUNK
    nc = s // L

    def body(qk_ref, v_ref, gif_ref, gt_ref, h_ref, dh_ref, cst_ref, nm_ref, _, dqk_ref, dv_ref, dif_ref,
             dc_scr, dn_scr, a_scr, at_scr):
        @pl.when(pl.program_id(0) == 0)
        def _():
            dc_scr[...] = jnp.zeros_like(dc_scr)
            dn_scr[...] = jnp.zeros_like(dn_scr)
        a_last_row = _mlstm_gates(gif_ref, gt_ref, a_scr, at_scr)
        gif, a = gif_ref[...], a_scr[...]
        lane = lax.broadcasted_iota(jnp.int32, (L, LANE), 1)
        last = lax.broadcasted_iota(jnp.int32, (L, 1), 0) == L - 1
        di_tile = jnp.zeros((L, LANE), F32)
        cross = [jnp.zeros((L, LANE), F32)] * 3
        for h in range(ML_HEADS):
            c_mat, n_row = cst_ref[0, h], nm_ref[0, h, 0:1, :]
            m_prev = jnp.max(nm_ref[0, h, 1:2, :], axis=1, keepdims=True)
            t = _mlstm_head(h, qk_ref, v_ref, gif, gt_ref, a, at_scr, a_last_row, c_mat, n_row, m_prev)
            q, k, v, den = t["q"], t["k"], t["v"], t["den"]
            dhh = dh_ref[:, h * ML_DV:(h + 1) * ML_DV]
            hh = h_ref[:, h * ML_DV:(h + 1) * ML_DV]
            dnorm = jnp.maximum(jnp.abs(den), t["floor"])
            dnum = dhh / dnorm
            d_dn = -jnp.sum(dhh * hh, axis=1, keepdims=True) / dnorm
            dden = jnp.where(jnp.abs(den) >= t["floor"], jnp.where(den >= 0.0, d_dn, -d_dn), 0.0)
            dsc = _nt(_bf(dnum), _bf(v)) + dden
            ds = dsc * t["w_intra"]
            dq_inter = t["w_inter"] * (_nt(_bf(dnum), _bf(c_mat)) + dden * n_row)
            dq = _nn(_bf(ds), _bf(k)) + dq_inter
            dc, dn_row = dc_scr[h], dn_scr[h, 0:1, :]
            dk_state = t["wk"] * (_nt(_bf(v), _bf(dc)) + dn_row)
            dk = _tn(_bf(ds), _bf(q)) + dk_state
            dv = _tn(_bf(t["sc"]), _bf(dnum)) + t["wk"] * _nn(_bf(k), _bf(dc))
            qi = q * t["w_inter"]
            dc_scr[h] = t["decay"] * dc + _tn(_bf(qi), _bf(dnum))
            dn_scr[h] = jnp.broadcast_to(t["decay"] * dn_row + jnp.sum(qi * dden, axis=0, keepdims=True), (8, LANE))
            dqk_ref[:, h * ML_DQK:(h + 1) * ML_DQK] = dq * (ML_DQK ** -0.5)
            dqk_ref[:, (ML_HEADS + h) * ML_DQK:(ML_HEADS + h + 1) * ML_DQK] = dk
            dv_ref[:, h * ML_DV:(h + 1) * ML_DV] = _bf(dv)
            di_tile = di_tile + jnp.where(lane == h, jnp.sum(k * dk, axis=1, keepdims=True), 0.0)
            carried = t["decay"] * (_sum_all(dc * c_mat) + jnp.sum(dn_row * n_row, axis=1, keepdims=True))
            parts = (_crossing(dsc * t["sc"]),
                     jnp.sum(q * dq_inter, axis=1, keepdims=True) + jnp.where(last, carried, 0.0),
                     jnp.sum(k * dk_state, axis=1, keepdims=True))
            cross = [c + jnp.where(lane == ML_HEADS + h, p, 0.0) for c, p in zip(cross, parts)]
        dfb = cross[0] + _pick_left(_tri(L, True), cross[1], 2) + _pick_left(_tri(L, False) - _eye(L), cross[2], 2)
        dif_ref[:, 0:LANE] = _bf(di_tile + dfb * _sigmoid(-gif))
        dif_ref[:, LANE:SMALL_W] = jnp.zeros((L, SMALL_W - LANE), BF16)

    rev = lambda c: nc - 1 - c
    return pl.pallas_call(
        body, name="mlstm_bwd", grid=(nc,),
        in_specs=[pl.BlockSpec((L, 2048), lambda c: (rev(c), 0)), pl.BlockSpec((L, 2048), lambda c: (rev(c), O_V // 2048)),
                  pl.BlockSpec((L, LANE), lambda c: (rev(c), O_IF // LANE)), pl.BlockSpec((LANE, L), lambda c: (0, rev(c))),
                  pl.BlockSpec((L, 2048), lambda c: (rev(c), 0)), pl.BlockSpec((L, 2048), lambda c: (rev(c), 0)),
                  pl.BlockSpec((1, ML_HEADS, ML_DQK, ML_DV), lambda c: (rev(c), 0, 0, 0)),
                  pl.BlockSpec((1, ML_HEADS, 8, LANE), lambda c: (rev(c), 0, 0, 0)), pl.BlockSpec(memory_space=pl.ANY)],
        out_specs=[pl.BlockSpec((L, 2048), lambda c: (rev(c), 0)), pl.BlockSpec((L, 2048), lambda c: (rev(c), O_V // 2048)),
                   pl.BlockSpec((L, SMALL_W), lambda c: (rev(c), 0))],
        out_shape=[jax.ShapeDtypeStruct((s, 2048), F32), jax.ShapeDtypeStruct(dproj.shape, dproj.dtype),
                   jax.ShapeDtypeStruct((s, SMALL_W), BF16)],
        input_output_aliases={8: 1},
        scratch_shapes=[pltpu.VMEM((ML_HEADS, ML_DQK, ML_DV), F32), pltpu.VMEM((ML_HEADS, 8, LANE), F32),
                        pltpu.VMEM((L, LANE), F32), pltpu.VMEM((LANE, L), F32)],
        compiler_params=_cparams("arbitrary"))(qk, proj, proj, gt, hout, dh, cst, nm, dproj)


GROUP_W = SSM_HEADS // SSM_GROUPS * SSM_HEADDIM
O_B = SSM_HEADS * SSM_HEADDIM
O_C = O_B + SSM_GROUPS * SSM_STATE


def _head_expand():
    r = jnp.arange(LANE)[:, None]
    c = jnp.arange(SSM_HEADS * SSM_HEADDIM)[None, :] // SSM_HEADDIM
    return (r == c).astype(F32)


def _ssd_gates(dt_ref, dtt_ref, alog_row_ref, alog_col_ref, at_scr):
    L = dt_ref.shape[0]
    dt = _softplus(dt_ref[...])
    acoef = -jnp.exp(alog_row_ref[...])
    a = _pick_left(_tri(L, False), dt * acoef, 3)
    at_scr[...] = _pick_right(_softplus(dtt_ref[...]) * (-jnp.exp(alog_col_ref[...])), _tri(L, True), 3)
    return dt, acoef, a


def _ssd_group(g, xbc_ref, dt, a, e_ref, ax_scr):
    eg = e_ref[:, g * GROUP_W:(g + 1) * GROUP_W]
    ax_scr[...] = _pick_right(a, eg, 3)
    ax = ax_scr[...]
    alx = ax_scr[ax.shape[0] - 1:ax.shape[0], :]
    dtx = _pick_right(dt, eg, 2)
    xg = xbc_ref[:, g * GROUP_W:(g + 1) * GROUP_W]
    bg = xbc_ref[:, O_B + g * SSM_STATE:O_B + (g + 1) * SSM_STATE]
    cg = xbc_ref[:, O_C + g * SSM_STATE:O_C + (g + 1) * SSM_STATE]
    return dict(ax=ax, alx=alx, dtx=dtx, xg=xg, bg=bg, cg=cg, xdt=xg * dtx, gmat=_nt(_bf(cg), _bf(bg)))


def _ssd_decay(hh, a, at_scr):
    L = a.shape[0]
    causal = lax.broadcasted_iota(jnp.int32, (L, L), 0) >= lax.broadcasted_iota(jnp.int32, (L, L), 1)
    return jnp.exp(jnp.where(causal, _lane_col(a, hh) - at_scr[hh:hh + 1, :], NEG))


def _ssd_fwd(xbc, proj, dtt, alog_row, alog_col, dskip_x, expand):
    s = xbc.shape[0]
    L = CHUNK
    nc = s // L
    half = SSM_HEADDIM

    def body(xbc_ref, dt_ref, dtt_ref, ar_ref, ac_ref, dk_ref, e_ref, y_ref, st_ref, st_scr, at_scr, ax_scr):
        @pl.when(pl.program_id(0) == 0)
        def _():
            st_scr[...] = jnp.zeros_like(st_scr)
        dt, _, a = _ssd_gates(dt_ref, dtt_ref, ar_ref, ac_ref, at_scr)
        lane = lax.broadcasted_iota(jnp.int32, (L, LANE), 1)
        for g in range(SSM_GROUPS):
            t = _ssd_group(g, xbc_ref, dt, a, e_ref, ax_scr)
            st = st_scr[g]
            st_ref[0, g] = st
            pairs = []
            for j in range(GROUP_W // LANE):
                xp = _bf(t["xdt"][:, j * LANE:(j + 1) * LANE])
                hh = g * (SSM_HEADS // SSM_GROUPS) + 2 * j
                y0 = _nn(_bf(t["gmat"] * _ssd_decay(hh, a, at_scr)), xp)
                y1 = _nn(_bf(t["gmat"] * _ssd_decay(hh + 1, a, at_scr)), xp)
                pairs.append(jnp.where(lane < half, y0, y1))
            y = jnp.concatenate(pairs, axis=1) + _nn(_bf(t["cg"]), _bf(st)) * jnp.exp(t["ax"])
            y_ref[:, g * GROUP_W:(g + 1) * GROUP_W] = y + dk_ref[:, g * GROUP_W:(g + 1) * GROUP_W] * t["xg"]
            wts = jnp.exp(t["alx"] - t["ax"])
            st_scr[g] = jnp.exp(t["alx"]) * st + _tn(_bf(t["bg"]), _bf(t["xdt"] * wts))

    row = lambda w: pl.BlockSpec((1, w), lambda c: (0, 0))
    return pl.pallas_call(
        body, name="ssd_fwd", grid=(nc,),
        in_specs=[pl.BlockSpec((L, 3072), lambda c: (c, 0)), pl.BlockSpec((L, LANE), lambda c: (c, O_DT // LANE)),
                  pl.BlockSpec((LANE, L), lambda c: (0, c)), row(LANE), pl.BlockSpec((LANE, 1), lambda c: (0, 0)),
                  row(2048), pl.BlockSpec((LANE, 2048), lambda c: (0, 0))],
        out_specs=[pl.BlockSpec((L, 2048), lambda c: (c, 0)),
                   pl.BlockSpec((1, SSM_GROUPS, SSM_STATE, GROUP_W), lambda c: (c, 0, 0, 0))],
        out_shape=[jax.ShapeDtypeStruct((s, 2048), F32),
                   jax.ShapeDtypeStruct((nc, SSM_GROUPS, SSM_STATE, GROUP_W), F32)],
        scratch_shapes=[pltpu.VMEM((SSM_GROUPS, SSM_STATE, GROUP_W), F32), pltpu.VMEM((LANE, L), F32),
                        pltpu.VMEM((L, GROUP_W), F32)],
        compiler_params=_cparams("arbitrary"))(xbc, proj, dtt, alog_row, alog_col, dskip_x, expand)


def _ssd_bwd(xbc, proj, dtt, alog_row, alog_col, dskip_x, expand, expand_t, dy, states):
    s = xbc.shape[0]
    L = CHUNK
    nc = s // L
    half = SSM_HEADDIM

    def body(xbc_ref, dt_ref, dtt_ref, ar_ref, ac_ref, dk_ref, e_ref, et_ref, dy_ref, st_ref,
             dxbc_ref, ddt_ref, accd_ref, acca_ref, dst_scr, at_scr, ax_scr):
        @pl.when(pl.program_id(0) == 0)
        def _():
            dst_scr[...] = jnp.zeros_like(dst_scr)
            accd_ref[...] = jnp.zeros_like(accd_ref)
            acca_ref[...] = jnp.zeros_like(acca_ref)
        dt, acoef, a = _ssd_gates(dt_ref, dtt_ref, ar_ref, ac_ref, at_scr)
        lane = lax.broadcasted_iota(jnp.int32, (L, LANE), 1)
        low = lane < half
        last = lax.broadcasted_iota(jnp.int32, (L, 1), 0) == L - 1
        cross = [jnp.zeros((L, LANE), F32)] * 3
        ddt_tile = jnp.zeros((L, LANE), F32)
        for g in range(SSM_GROUPS):
            t = _ssd_group(g, xbc_ref, dt, a, e_ref, ax_scr)
            xg, bg, cg, xdt, gmat = t["xg"], t["bg"], t["cg"], t["xdt"], t["gmat"]
            st, dst = st_ref[0, g], dst_scr[g]
            dyg = dy_ref[:, g * GROUP_W:(g + 1) * GROUP_W]
            ea, eal = jnp.exp(t["ax"]), jnp.exp(t["alx"])
            wts = jnp.exp(t["alx"] - t["ax"])
            dyi = dyg * ea
            y_inter = _nn(_bf(cg), _bf(st)) * ea
            dc = _nt(_bf(dyi), _bf(st))
            d_xdt_state = _nn(_bf(bg), _bf(dst)) * wts
            db = _nt(_bf(xdt * wts), _bf(dst))
            dst_scr[g] = eal * dst + _tn(_bf(cg), _bf(dyi))
            dg = jnp.zeros((L, L), F32)
            dx_pairs = []
            for j in range(GROUP_W // LANE):
                xp = _bf(xdt[:, j * LANE:(j + 1) * LANE])
                dyp = dyg[:, j * LANE:(j + 1) * LANE]
                dxs = []
                for b in range(2):
                    hh = g * (SSM_HEADS // SSM_GROUPS) + 2 * j + b
                    dec = _ssd_decay(hh, a, at_scr)
                    w = gmat * dec
                    dxs.append(_tn(_bf(w), _bf(dyp)))
                    dw = _nt(_bf(jnp.where(low if b == 0 else ~low, dyp, 0.0)), xp)
                    dg = dg + dw * dec
                    cross[0] = cross[0] + jnp.where(lane == hh, _crossing(dw * w), 0.0)
                dx_pairs.append(jnp.where(low, dxs[0], dxs[1]))
            d_xdt = d_xdt_state + jnp.concatenate(dx_pairs, axis=1)
            dc = dc + _nn(_bf(dg), _bf(bg))
            db = db + _tn(_bf(dg), _bf(cg))
            etg = et_ref[g * GROUP_W:(g + 1) * GROUP_W, :]
            carried = jnp.sum(dst * st, axis=0, keepdims=True) * eal
            cross[1] = cross[1] + _pick_right(dyg * y_inter + jnp.where(last, carried, 0.0), etg, 2)
            cross[2] = cross[2] + _pick_right(xdt * d_xdt_state, etg, 2)
            ddt_tile = ddt_tile + _pick_right(d_xdt * xg, etg, 2)
            dxbc_ref[:, g * GROUP_W:(g + 1) * GROUP_W] = d_xdt * t["dtx"] + dk_ref[:, g * GROUP_W:(g + 1) * GROUP_W] * dyg
            dxbc_ref[:, O_B + g * SSM_STATE:O_B + (g + 1) * SSM_STATE] = db
            dxbc_ref[:, O_C + g * SSM_STATE:O_C + (g + 1) * SSM_STATE] = dc
            accd_ref[0:1, g * GROUP_W:(g + 1) * GROUP_W] += jnp.sum(dyg * xg, axis=0, keepdims=True)
        d_da = cross[0] + _pick_left(_tri(L, True), cross[1], 2) + _pick_left(_tri(L, False) - _eye(L), cross[2], 2)
        acca_ref[0:1, :] += jnp.sum(d_da * dt, axis=0, keepdims=True)
        ddt_ref[:, 0:LANE] = _bf((ddt_tile + d_da * acoef) * _sigmoid(dt_ref[...]))
        ddt_ref[:, LANE:SMALL_W] = jnp.zeros((L, SMALL_W - LANE), BF16)

    rev = lambda c: nc - 1 - c
    row = lambda w: pl.BlockSpec((1, w), lambda c: (0, 0))
    return pl.pallas_call(
        body, name="ssd_bwd", grid=(nc,),
        in_specs=[pl.BlockSpec((L, 3072), lambda c: (rev(c), 0)), pl.BlockSpec((L, LANE), lambda c: (rev(c), O_DT // LANE)),
                  pl.BlockSpec((LANE, L), lambda c: (0, rev(c))), row(LANE), pl.BlockSpec((LANE, 1), lambda c: (0, 0)),
                  row(2048), pl.BlockSpec((LANE, 2048), lambda c: (0, 0)), pl.BlockSpec((2048, LANE), lambda c: (0, 0)),
                  pl.BlockSpec((L, 2048), lambda c: (rev(c), 0)),
                  pl.BlockSpec((1, SSM_GROUPS, SSM_STATE, GROUP_W), lambda c: (rev(c), 0, 0, 0))],
        out_specs=[pl.BlockSpec((L, 3072), lambda c: (rev(c), 0)), pl.BlockSpec((L, SMALL_W), lambda c: (rev(c), 0)),
                   pl.BlockSpec((8, 2048), lambda c: (0, 0)), pl.BlockSpec((8, LANE), lambda c: (0, 0))],
        out_shape=[jax.ShapeDtypeStruct((s, 3072), F32), jax.ShapeDtypeStruct((s, SMALL_W), BF16),
                   jax.ShapeDtypeStruct((8, 2048), F32), jax.ShapeDtypeStruct((8, LANE), F32)],
        scratch_shapes=[pltpu.VMEM((SSM_GROUPS, SSM_STATE, GROUP_W), F32),
                        pltpu.VMEM((LANE, L), F32), pltpu.VMEM((L, GROUP_W), F32)],
        compiler_params=_cparams("arbitrary"))(xbc, proj, dtt, alog_row, alog_col, dskip_x, expand, expand_t, dy, states)


def _group_norm(v, width):
    outs, rs = [], []
    for k in range(v.shape[1] // width):
        blk = v[:, k * width:(k + 1) * width]
        r = lax.rsqrt(jnp.mean(blk * blk, axis=1, keepdims=True) + EPS)
        outs.append(blk * r)
        rs.append(jnp.broadcast_to(r, blk.shape))
    return jnp.concatenate(outs, axis=1), jnp.concatenate(rs, axis=1)


def _group_mean(v, width):
    return jnp.concatenate([jnp.broadcast_to(jnp.mean(v[:, k * width:(k + 1) * width], axis=1, keepdims=True),
                                             (v.shape[0], width)) for k in range(v.shape[1] // width)], axis=1)


def _post_fwd(hm, yssd, proj, ml_norm_w, ssm_norm_w, ts):
    s = hm.shape[0]

    def body(h_ref, ys_ref, o_ref, zm_ref, zs_ref, wm_ref, ws_ref, ym_ref, yso_ref):
        hn, _ = _group_norm(h_ref[...], ML_DV)
        ym_ref[...] = _bf(_sigmoid(o_ref[...]) * hn * wm_ref[...] * _silu(zm_ref[...]))
        pn, _ = _group_norm(ys_ref[...] * _silu(zs_ref[...]), GROUP_W)
        yso_ref[...] = _bf(pn * ws_ref[...])

    tile = pl.BlockSpec((ts, 2048), lambda i: (i, 0))
    col = lambda off: pl.BlockSpec((ts, 2048), lambda i: (i, off // 2048))
    row = pl.BlockSpec((1, 2048), lambda i: (0, 0))
    return pl.pallas_call(
        body, name="post_fwd", grid=(s // ts,),
        in_specs=[tile, tile, col(O_O), col(O_ZM), col(O_ZS), row, row],
        out_specs=[tile, tile],
        out_shape=[jax.ShapeDtypeStruct((s, 2048), BF16)] * 2,
        compiler_params=_cparams("parallel"))(hm, yssd, proj, proj, proj, ml_norm_w, ssm_norm_w)


def _post_bwd(dym, dys, hm, yssd, proj, ml_norm_w, ssm_norm_w, dproj, ts):
    s = hm.shape[0]

    def body(dym_ref, dys_ref, h_ref, ys_ref, o_ref, zm_ref, zs_ref, wm_ref, ws_ref, _,
             dh_ref, dyssd_ref, dp_ref, acc_ref):
        @pl.when(pl.program_id(0) == 0)
        def _():
            acc_ref[...] = jnp.zeros_like(acc_ref)
        hn, r = _group_norm(h_ref[...], ML_DV)
        so, zm, wm, d_ym = _sigmoid(o_ref[...]), zm_ref[...], wm_ref[...], dym_ref[...]
        sz = _silu(zm)
        hnw = hn * wm
        dp_ref[:, O_O:O_O + 2048] = _bf(d_ym * hnw * sz * so * (1.0 - so))
        dp_ref[:, O_ZM:O_ZM + 2048] = _bf(d_ym * so * hnw * _dsilu(zm))
        dhnw = d_ym * so * sz
        acc_ref[0:1, :] += jnp.sum(dhnw * hn, axis=0, keepdims=True)
        dhn = dhnw * wm
        dh_ref[...] = r * (dhn - hn * _group_mean(dhn * hn, ML_DV))
        ysv, zs, d_ys = ys_ref[...], zs_ref[...], dys_ref[...]
        szs = _silu(zs)
        pn, r2 = _group_norm(ysv * szs, GROUP_W)
        acc_ref[1:2, :] += jnp.sum(d_ys * pn, axis=0, keepdims=True)
        dpn = d_ys * ws_ref[...]
        dp = r2 * (dpn - pn * _group_mean(dpn * pn, GROUP_W))
        dyssd_ref[...] = dp * szs
        dp_ref[:, O_ZS:O_ZS + 2048] = _bf(dp * ysv * _dsilu(zs))

    tile = pl.BlockSpec((ts, 2048), lambda i: (i, 0))
    col = lambda off: pl.BlockSpec((ts, 2048), lambda i: (i, off // 2048))
    row = pl.BlockSpec((1, 2048), lambda i: (0, 0))
    sds = lambda dt: jax.ShapeDtypeStruct((s, 2048), dt)
    return pl.pallas_call(
        body, name="post_bwd", grid=(s // ts,),
        in_specs=[tile, tile, tile, tile, col(O_O), col(O_ZM), col(O_ZS), row, row, pl.BlockSpec(memory_space=pl.ANY)],
        out_specs=[tile, tile, pl.BlockSpec((ts, O_MG), lambda i: (i, 0)), pl.BlockSpec((8, 2048), lambda i: (0, 0))],
        out_shape=[sds(F32), sds(F32), jax.ShapeDtypeStruct(dproj.shape, dproj.dtype), jax.ShapeDtypeStruct((8, 2048), F32)],
        input_output_aliases={9: 2},
        compiler_params=_cparams("arbitrary"))(dym, dys, hm, yssd, proj, proj, proj, ml_norm_w, ssm_norm_w, dproj)


def _merge(x, ym, ys, proj, target, gate, final_w, wpm, wps, wo, ts):
    wpm_t, wps_t, wo_t = wpm.T, wps.T, wo.T
    s, d = x.shape

    def body(x_ref, ym_ref, ys_ref, mg_ref, t_ref, gate_ref, fw_ref, wpm_ref, wps_ref, wo_ref, wpmt_ref, wpst_ref, wot_ref,
             dres_ref, mer_ref, dmo_ref, dpm_ref, dps_ref, dym_ref, dys_ref, dmg_ref, acc_ref):
        @pl.when(pl.program_id(0) == 0)
        def _():
            acc_ref[...] = jnp.zeros_like(acc_ref)
        gm, gs = _sigmoid(mg_ref[:, 0:d]), _sigmoid(mg_ref[:, d:2 * d])
        pm = _nn(ym_ref[...], wpm_ref[...])
        ps = _nn(ys_ref[...], wps_ref[...])
        merged = _bf(gm * pm + gs * ps)
        mer_ref[...] = merged
        mo = _nn(merged, wo_ref[...])
        gate, fw = gate_ref[...], fw_ref[...]
        out = x_ref[...] + gate * mo
        r = lax.rsqrt(jnp.mean(out * out, axis=1, keepdims=True) + EPS)
        on = out * r
        diff = on * fw - t_ref[...]
        acc_ref[0:1, :] += jnp.sum(0.5 * jnp.sum(diff * diff, axis=1, keepdims=True) / d, axis=0, keepdims=True)
        dyv = diff * (1.0 / d)
        acc_ref[1:2, :] += jnp.sum(dyv * on, axis=0, keepdims=True)
        don = dyv * fw
        dout = r * (don - on * jnp.mean(don * on, axis=1, keepdims=True))
        dres_ref[...] = dout
        acc_ref[2:3, :] += jnp.sum(dout * mo, axis=0, keepdims=True)
        dmo = _bf(dout * gate)
        dmo_ref[...] = dmo
        dmer = _nn(dmo, wot_ref[...])
        dpm, dps = _bf(dmer * gm), _bf(dmer * gs)
        dpm_ref[...] = dpm
        dps_ref[...] = dps
        dmg_ref[:, 0:d] = _bf(dmer * pm * gm * (1.0 - gm))
        dmg_ref[:, d:2 * d] = _bf(dmer * ps * gs * (1.0 - gs))
        dym_ref[...] = _nn(dpm, wpmt_ref[...])
        dys_ref[...] = _nn(dps, wpst_ref[...])

    t1 = pl.BlockSpec((ts, d), lambda i: (i, 0))
    t2 = pl.BlockSpec((ts, 2 * d), lambda i: (i, 0))
    row = pl.BlockSpec((1, d), lambda i: (0, 0))
    whole = pl.BlockSpec(memory_space=pltpu.VMEM)
    sd = lambda w, dt: jax.ShapeDtypeStruct((s, w), dt)
    return pl.pallas_call(
        body, name="merge_fwd_bwd", grid=(s // ts,),
        in_specs=[t1, t2, t2, pl.BlockSpec((ts, 2 * d), lambda i: (i, O_MG // (2 * d))), t1, row, row] + [whole] * 6,
        out_specs=[t1, t1, t1, t1, t1, t2, t2, pl.BlockSpec((ts, 2 * d), lambda i: (i, O_MG // (2 * d))),
                   pl.BlockSpec((8, d), lambda i: (0, 0))],
        out_shape=[sd(d, F32), sd(d, BF16), sd(d, BF16), sd(d, BF16), sd(d, BF16), sd(2 * d, F32), sd(2 * d, F32),
                   sd(NP, BF16), jax.ShapeDtypeStruct((8, d), F32)],
        compiler_params=_cparams("arbitrary"))(x, ym, ys, proj, target, gate, final_w, wpm, wps, wo, wpm_t, wps_t, wo_t)


def _adamw(w, g, m, v, tr):
    rows, cols = w.shape[-2:]
    lead = (None,) * (w.ndim - 2)

    def body(w_ref, g_ref, m_ref, v_ref, d_ref, nm_ref, nv_ref):
        gv = g_ref[...]
        m2 = ADAM_B1 * m_ref[...] + (1.0 - ADAM_B1) * gv
        v2 = ADAM_B2 * v_ref[...] + (1.0 - ADAM_B2) * (gv * gv)
        m_hat = m2 / (1.0 - ADAM_B1 ** ADAM_STEP)
        v_hat = v2 / (1.0 - ADAM_B2 ** ADAM_STEP)
        d_ref[...] = -ADAM_LR * (m_hat / (jnp.sqrt(v_hat) + ADAM_EPS) + ADAM_WD * w_ref[...])
        nm_ref[...] = m2
        nv_ref[...] = v2

    tile = pl.BlockSpec(lead + (tr, cols), lambda i: (0,) * len(lead) + (i, 0))
    return pl.pallas_call(
        body, name="adamw", grid=(rows // tr,), in_specs=[tile] * 4, out_specs=[tile] * 3,
        out_shape=[jax.ShapeDtypeStruct(w.shape, F32)] * 3,
        compiler_params=_cparams("parallel"))(w, g.reshape(w.shape), m, v)


def _sum_parts(own, parts, tr, dtype=F32):
    p, rows, cols = parts.shape

    def body(*refs):
        p_ref, o_ref = refs[-2], refs[-1]
        acc = p_ref[0].astype(F32) if own is None else refs[0][...].astype(F32) + p_ref[0].astype(F32)
        for i in range(1, p):
            acc = acc + p_ref[i].astype(F32)
        o_ref[...] = acc.astype(dtype)

    tile = pl.BlockSpec((tr, cols), lambda i: (i, 0))
    ins = ([] if own is None else [tile]) + [pl.BlockSpec((p, tr, cols), lambda i: (0, i, 0))]
    args = ([] if own is None else [own]) + [parts]
    return pl.pallas_call(
        body, name="sum_parts", grid=(rows // tr,), in_specs=ins, out_specs=tile,
        out_shape=jax.ShapeDtypeStruct((rows, cols), dtype), compiler_params=_cparams("parallel"))(*args)


def _position():
    return lax.axis_index("x"), lax.axis_index("y"), lax.axis_index("c")


def _flip(pos, k):
    return tuple(1 - p if (k >> s) & 1 else p for p, s in zip(pos, (2, 1, 0)))


def _allgather8(block):
    rows, cols = block.shape

    def body(x_ref, o_ref, send_sems, recv_sems, local_sem):
        pos = _position()
        me = 4 * pos[0] + 2 * pos[1] + pos[2]
        mine = pltpu.make_async_copy(x_ref, o_ref.at[me], local_sem)
        mine.start()
        copies = [pltpu.make_async_remote_copy(src_ref=x_ref, dst_ref=o_ref.at[me], send_sem=send_sems.at[k - 1],
                                               recv_sem=recv_sems.at[k - 1], device_id=_flip(pos, k), device_id_type=MESH)
                  for k in range(1, N_DEV)]
        for cp in copies:
            cp.start()
        for cp in copies:
            cp.wait()
        mine.wait()

    vmem = pl.BlockSpec(memory_space=pltpu.VMEM)
    return pl.pallas_call(
        body, name="allgather8", in_specs=[vmem], out_specs=vmem,
        out_shape=jax.ShapeDtypeStruct((N_DEV, rows, cols), block.dtype),
        scratch_shapes=[pltpu.SemaphoreType.DMA((N_DEV - 1,)), pltpu.SemaphoreType.DMA((N_DEV - 1,)),
                        pltpu.SemaphoreType.DMA],
        compiler_params=pltpu.CompilerParams(vmem_limit_bytes=VMEM_LIMIT))(block)


COPY_BYTES = 1 << 20


def _row_chunks(rows, row_bytes):
    n = max(1, min(rows // 16, -(-rows * row_bytes // COPY_BYTES)))
    while rows % (16 * n):
        n -= 1
    return [(i * (rows // n), rows // n) for i in range(n)]


def _weight_gather(shards):
    n = len(shards)
    pieces = [_row_chunks(a.shape[1], a.shape[2] * a.dtype.itemsize) for a in shards]
    plan = [(a, k, r0, nr) for a in range(n) for k in range(1, N_CHIPS) for r0, nr in pieces[a]]

    def body(*refs):
        ins, outs = refs[:n], refs[n:2 * n]
        ici_send, ici_recv, d2d_send, d2d_recv = refs[2 * n:]
        pos = _position()
        chip, core = 2 * pos[0] + pos[1], pos[2]
        sibling = _flip(pos, 1)
        sent = []
        for i, (a, k, r0, nr) in enumerate(plan):
            cp = pltpu.make_async_remote_copy(
                src_ref=ins[a].at[core, pl.ds(r0, nr)], dst_ref=outs[a].at[chip, core, pl.ds(r0, nr)],
                send_sem=ici_send.at[i], recv_sem=ici_recv.at[i], device_id=_flip(pos, 2 * k), device_id_type=MESH)
            cp.start()
            sent.append(cp)
        passed = []
        for i, (a, k, r0, nr) in enumerate(plan):
            there = _flip(pos, 2 * k)
            landed = outs[a].at[2 * there[0] + there[1], core, pl.ds(r0, nr)]
            sent[i].wait_recv()
            cp = pltpu.make_async_remote_copy(src_ref=landed, dst_ref=landed, send_sem=d2d_send.at[i],
                                              recv_sem=d2d_recv.at[i], device_id=sibling, device_id_type=MESH)
            cp.start()
            passed.append(cp)
        for cp in passed:
            cp.wait()
        for cp in sent:
            cp.wait_send()

    hbm = pl.BlockSpec(memory_space=pl.ANY)
    sems = pltpu.SemaphoreType.DMA((len(plan),))
    return pl.pallas_call(
        body, name="weight_gather", in_specs=[hbm] * n, out_specs=[hbm] * n,
        out_shape=[jax.ShapeDtypeStruct((N_CHIPS,) + a.shape, a.dtype) for a in shards],
        scratch_shapes=[sems, sems, sems, sems],
        compiler_params=pltpu.CompilerParams(has_side_effects=True))(*shards)


def _exchange(name, arrays, out_shapes, plan, n_remote, n_local):
    n, m = len(arrays), len(out_shapes)

    def body(*refs):
        send_sems, recv_sems, local_sems = refs[n + m:]
        remote, local = plan(_position(), refs[:n], refs[n:n + m])
        assert (len(remote), len(local)) == (n_remote, n_local)
        copies = [pltpu.make_async_copy(src, dst, local_sems.at[i]) for i, (src, dst) in enumerate(local)]
        copies += [pltpu.make_async_remote_copy(src_ref=src, dst_ref=dst, send_sem=send_sems.at[i], recv_sem=recv_sems.at[i],
                                                device_id=dev, device_id_type=MESH)
                   for i, (src, dst, dev) in enumerate(remote)]
        for cp in copies:
            cp.start()
        for cp in copies:
            cp.wait()

    hbm = pl.BlockSpec(memory_space=pl.ANY)
    return pl.pallas_call(
        body, name=name, in_specs=[hbm] * n, out_specs=[hbm] * m, out_shape=out_shapes,
        scratch_shapes=[pltpu.SemaphoreType.DMA((n_remote,)), pltpu.SemaphoreType.DMA((n_remote,)),
                        pltpu.SemaphoreType.DMA((max(n_local, 1),))],
        compiler_params=pltpu.CompilerParams(has_side_effects=True))(*arrays)


def _pair_send(slabs):
    n = len(slabs)
    pieces = [_row_chunks(g.shape[2], g.shape[3] * g.dtype.itemsize) for g in slabs]

    def plan(pos, ins, outs):
        return [(ins[a].at[j, 1 - pos[2], pl.ds(r0, nr)], outs[a].at[j, pl.ds(r0, nr)], _flip(pos, 1))
                for a in range(n) for j in range(N_CHIPS) for r0, nr in pieces[a]], []

    return _exchange("pair_send", slabs, [jax.ShapeDtypeStruct((N_CHIPS,) + g.shape[2:], g.dtype) for g in slabs], plan,
                     N_CHIPS * sum(len(p) for p in pieces), 0)


def _chip_scatter_copies(pos, sums, lands, send_sems, recv_sems):
    copies = []
    for a in range(len(sums)):
        for k in range(1, N_CHIPS):
            to = _flip(pos, 2 * k)
            for r0, nr in _row_chunks(sums[a].shape[1], sums[a].shape[2] * sums[a].dtype.itemsize):
                i = len(copies)
                copies.append(pltpu.make_async_remote_copy(
                    src_ref=sums[a].at[2 * to[0] + to[1], pl.ds(r0, nr)], dst_ref=lands[a].at[k - 1, pl.ds(r0, nr)],
                    send_sem=send_sems.at[i], recv_sem=recv_sems.at[i], device_id=to, device_id_type=MESH))
    return copies


def _chip_scatter_start(sums):
    n = len(sums)
    n_copies = (N_CHIPS - 1) * sum(len(_row_chunks(g.shape[1], g.shape[2] * g.dtype.itemsize)) for g in sums)
    lands = [lax.empty((N_CHIPS - 1,) + g.shape[1:], g.dtype) for g in sums]

    def body(*refs):
        send_sems, recv_sems = refs[2 * n], refs[2 * n + 1]
        for cp in _chip_scatter_copies(_position(), refs[:n], refs[n:2 * n], send_sems, recv_sems):
            cp.start()
        refs[-1][...] = jnp.zeros((8, LANE), F32)

    hbm = pl.BlockSpec(memory_space=pltpu.HBM)
    sem = pl.BlockSpec(memory_space=pltpu.SEMAPHORE)
    operands = [pltpu.with_memory_space_constraint(t, pltpu.HBM) for t in list(sums) + lands]
    out = pl.pallas_call(
        body, name="chip_scatter_start", in_specs=[hbm] * (2 * n),
        out_specs=[sem, sem] + [hbm] * (2 * n) + [pl.BlockSpec(memory_space=pltpu.VMEM)],
        out_shape=[pltpu.SemaphoreType.DMA((n_copies,)), pltpu.SemaphoreType.DMA((n_copies,))]
        + [pltpu.HBM(t.shape, t.dtype) for t in operands] + [jax.ShapeDtypeStruct((8, LANE), F32)],
        input_output_aliases={i: 2 + i for i in range(2 * n)},
        compiler_params=pltpu.CompilerParams(has_side_effects=pltpu.SideEffectType.DATAFLOW_SIDE_EFFECTING))(*operands)
    return out[0], out[1], out[2:2 + n], out[2 + n:2 + 2 * n], out[-1]


def _chip_scatter_wait(send_sems, recv_sems, sums, lands, after):
    n = len(sums)

    def body(*refs):
        for cp in _chip_scatter_copies(_position(), refs[:n], refs[n:2 * n], refs[2 * n], refs[2 * n + 1]):
            cp.wait_send()
            cp.wait_recv()

    hbm = pl.BlockSpec(memory_space=pltpu.HBM)
    sem = pl.BlockSpec(memory_space=pltpu.SEMAPHORE)
    out = pl.pallas_call(
        body, name="chip_scatter_wait", in_specs=[hbm] * (2 * n) + [sem, sem, pl.BlockSpec(memory_space=pl.ANY)],
        out_specs=[hbm] * (2 * n), out_shape=[pltpu.HBM(t.shape, t.dtype) for t in list(sums) + list(lands)],
        input_output_aliases={i: i for i in range(2 * n)},
        compiler_params=pltpu.CompilerParams(has_side_effects=pltpu.SideEffectType.DATAFLOW_SIDE_EFFECTING))(
            *sums, *lands, send_sems, recv_sems, after)
    return out[:n], out[n:]


def _pair_exchange(halves):
    n = len(halves)
    pieces = [_row_chunks(h.shape[0], h.shape[1] * h.dtype.itemsize) for h in halves]

    def plan(pos, ins, outs):
        return [(ins[a].at[pl.ds(r0, nr)], outs[a].at[pl.ds(r0, nr)], _flip(pos, 1))
                for a in range(n) for r0, nr in pieces[a]], []

    return _exchange("pair_exchange", halves, [jax.ShapeDtypeStruct(h.shape, h.dtype) for h in halves], plan,
                     sum(len(p) for p in pieces), 0)


def _pack(arrays):
    flat = jnp.concatenate([a.reshape(-1).astype(F32) for a in arrays])
    size = -(-flat.shape[0] // (8 * LANE)) * (8 * LANE)
    return jnp.pad(flat, (0, size - flat.shape[0])).reshape(size // LANE, LANE)


def _unpack(buf, shapes):
    flat = buf.reshape(-1)
    out, off = [], 0
    for shp in shapes:
        n = math.prod(shp)
        out.append(flat[off:off + n].reshape(shp))
        off += n
    return out


def _unpack_rows(bufs, shapes):
    flat = bufs.reshape(bufs.shape[0], -1)
    out, off = [], 0
    for shp in shapes:
        n = math.prod(shp)
        out.append(flat[:, off:off + n].reshape((bufs.shape[0],) + shp))
        off += n
    return out


def _taps8(w):
    return jnp.pad(w, ((0, 8 - CONV_K), (0, 0)))


def _local_step(xs, tgt, scale, shift, gate, norm_w, w_in_p, b_in_p, ml_conv_w, ml_conv_b, ml_norm_w, ssm_conv_w,
                ssm_conv_b, ssm_a_log, ssm_d, ssm_norm_w, wpm, wps, wo, final_w, start_exchange=None):
    s = xs.shape[0]
    ts = min(512, s)
    tm = min(2048, s)
    u = _prenorm_fwd(xs, norm_w, scale, shift, ts)
    proj = _matmul_bias(u, w_in_p, b_in_p, tm, 512)
    mlw8, ssw8 = _taps8(ml_conv_w), _taps8(ssm_conv_w)
    qk, qk_dact = _conv_fwd(proj, O_QK, 2048, mlw8, ml_conv_b, ts)
    xbc, xbc_dact = _conv_fwd(proj, O_XBC, 3072, ssw8, ssm_conv_b, ts)
    gt = proj[:, O_IF:O_IF + LANE].T
    dtt = proj[:, O_DT:O_DT + LANE].T
    hm, cst, nm = _mlstm_fwd(qk, proj, gt)
    alog_row = jnp.pad(ssm_a_log, ((0, 0), (0, LANE - SSM_HEADS)))
    alog_col = alog_row.reshape(LANE, 1)
    dskip_x = jnp.repeat(ssm_d[0], SSM_HEADDIM)[None]
    expand = _head_expand()
    yssd, sst = _ssd_fwd(xbc, proj, dtt, alog_row, alog_col, dskip_x, expand)
    tp = min(128, s)
    ym, ys = _post_fwd(hm, yssd, proj, ml_norm_w, ssm_norm_w, tp)
    dxres, merged, dmo, dpm, dps, dym, dys, dproj, acc_m = _merge(xs, ym, ys, proj, tgt, gate, final_w, wpm, wps, wo, tp)
    dh, dyssd, dproj, acc_p = _post_bwd(dym, dys, hm, yssd, proj, ml_norm_w, ssm_norm_w, dproj, tp)
    dqk, dproj, dif = _mlstm_bwd(qk, proj, gt, hm, dh, cst, nm, dproj)
    dxbc, ddt, accd, acca = _ssd_bwd(xbc, proj, dtt, alog_row, alog_col, dskip_x, expand, expand.T, dyssd, sst)
    dproj, acc_cq = _conv_bwd(proj, O_QK, 2048, mlw8, qk_dact, dqk, dproj, ts)
    dproj, acc_cx = _conv_bwd(proj, O_XBC, 3072, ssw8, xbc_dact, dxbc, dproj, ts)
    dproj = dproj.at[:, O_IF:O_IF + SMALL_W].set(dif).at[:, O_DT:O_DT + SMALL_W].set(ddt)
    gw_in_p, gb_in_p = _matmul_tn(u.T, dproj, tm, 512, with_colsum=True, a_is_transposed=True)
    g_wpm = _matmul_tn(ym, dpm, tm, 512)
    g_wps = _matmul_tn(ys, dps, tm, 512)
    g_wo = _matmul_tn(merged, dmo, tm, 512)
    token, in_flight = (None, None) if start_exchange is None else start_exchange(gw_in_p, g_wpm, g_wps, g_wo)
    du = _matmul_nt(dproj, w_in_p, tm, 512, after=token)
    grad_x, acc_n = _prenorm_bwd(du, xs, dxres, norm_w, scale, ts)
    a_coef = -jnp.exp(ssm_a_log[0])
    small = dict(
        mod=jnp.concatenate([acc_n[2], acc_n[1], acc_m[2]]), norm_w=acc_n[0], b_in=_unpad_cols(gb_in_p[0]),
        ml_conv_w=acc_cq[0:CONV_K], ml_conv_b=acc_cq[CONV_K], ml_norm_w=acc_p[0], ssm_conv_w=acc_cx[0:CONV_K],
        ssm_conv_b=acc_cx[CONV_K], ssm_a_log=acca[0, :SSM_HEADS] * a_coef,
        ssm_d=accd[0].reshape(SSM_HEADS, SSM_HEADDIM).sum(axis=1), ssm_norm_w=acc_p[1], final_w=acc_m[1], loss=acc_m[0, 0:1])
    return grad_x, small, gw_in_p, g_wpm, g_wps, g_wo, in_flight


WEIGHTS = ("norm_w", "ada_w", "ada_b", "w_in", "b_in", "ml_conv_w", "ml_conv_b", "ml_norm_w", "ssm_conv_w", "ssm_conv_b",
           "ssm_a_log", "ssm_d", "ssm_norm_w", "w_proj_m", "w_proj_s", "w_out", "final_w")
LARGE = ("ada_w", "w_in", "w_proj_m", "w_proj_s", "w_out")
SMALL_SUMS = (("mod", (3 * D_MODEL,)), ("norm_w", (D_MODEL,)), ("b_in", (IN_WIDTH,)), ("ml_conv_w", (CONV_K, 2048)),
              ("ml_conv_b", (2048,)), ("ml_norm_w", (2048,)), ("ssm_conv_w", (CONV_K, 3072)), ("ssm_conv_b", (3072,)),
              ("ssm_a_log", (SSM_HEADS,)), ("ssm_d", (SSM_HEADS,)), ("ssm_norm_w", (2048,)), ("final_w", (D_MODEL,)),
              ("loss", (1,)))


def kernel(x, c, norm_w, ada_w, ada_b, w_in, b_in, ml_conv_w, ml_conv_b, ml_norm_w, ssm_conv_w, ssm_conv_b, ssm_a_log, ssm_d, ssm_norm_w, w_proj_m, w_proj_s, w_out, final_w, loss_target, m_norm_w, m_ada_w, m_ada_b, m_w_in, m_b_in, m_ml_conv_w, m_ml_conv_b, m_ml_norm_w, m_ssm_conv_w, m_ssm_conv_b, m_ssm_a_log, m_ssm_d, m_ssm_norm_w, m_w_proj_m, m_w_proj_s, m_w_out, m_final_w, v_norm_w, v_ada_w, v_ada_b, v_w_in, v_b_in, v_ml_conv_w, v_ml_conv_b, v_ml_norm_w, v_ssm_conv_w, v_ssm_conv_b, v_ssm_a_log, v_ssm_d, v_ssm_norm_w, v_w_proj_m, v_w_proj_s, v_w_out, v_final_w):
    w = dict(norm_w=norm_w, ada_w=ada_w, ada_b=ada_b, w_in=w_in, b_in=b_in, ml_conv_w=ml_conv_w, ml_conv_b=ml_conv_b,
             ml_norm_w=ml_norm_w, ssm_conv_w=ssm_conv_w, ssm_conv_b=ssm_conv_b, ssm_a_log=ssm_a_log, ssm_d=ssm_d,
             ssm_norm_w=ssm_norm_w, w_proj_m=w_proj_m, w_proj_s=w_proj_s, w_out=w_out, final_w=final_w)
    m = dict(zip(WEIGHTS, (m_norm_w, m_ada_w, m_ada_b, m_w_in, m_b_in, m_ml_conv_w, m_ml_conv_b, m_ml_norm_w, m_ssm_conv_w,
                           m_ssm_conv_b, m_ssm_a_log, m_ssm_d, m_ssm_norm_w, m_w_proj_m, m_w_proj_s, m_w_out, m_final_w)))
    v = dict(zip(WEIGHTS, (v_norm_w, v_ada_w, v_ada_b, v_w_in, v_b_in, v_ml_conv_w, v_ml_conv_b, v_ml_norm_w, v_ssm_conv_w,
                           v_ssm_conv_b, v_ssm_a_log, v_ssm_d, v_ssm_norm_w, v_w_proj_m, v_w_proj_s, v_w_out, v_final_w)))
    pos = _position()
    chip = 2 * pos[0] + pos[1]
    dev = 2 * chip + pos[2]
    mlw_cols, ssw_cols, ada_cols = ml_conv_w.shape[2], ssm_conv_w.shape[2], ada_w.shape[2]

    g0 = _allgather8(_pack([c, ml_conv_w, ssm_conv_w]))
    c_all, mlw_all, ssw_all = _unpack_rows(g0, [(D_MODEL,), (CONV_K, mlw_cols), (CONV_K, ssw_cols)])
    ml_conv_full = mlw_all[0::2].transpose(1, 0, 2).reshape(CONV_K, N_CHIPS * mlw_cols)
    ssm_conv_full = ssw_all[0::2].transpose(1, 0, 2).reshape(CONV_K, N_CHIPS * ssw_cols)

    ada_b_mine = lax.dynamic_slice_in_dim(ada_b, chip * ada_cols, ada_cols, axis=1)
    g1 = _allgather8(_ada_fwd(c_all, ada_w[0], ada_b_mine))
    mod = lax.dynamic_index_in_dim(g1[0::2], dev, axis=1, keepdims=False).reshape(1, 3 * D_MODEL)
    shift, scale, gate = mod[:, :D_MODEL], mod[:, D_MODEL:2 * D_MODEL], mod[:, 2 * D_MODEL:]

    mine = [_bf(a[0]).reshape(2, a.shape[1] // 2, a.shape[2]) for a in (w_in, w_proj_m, w_proj_s, w_out)]
    gw = [lax.dynamic_update_index_in_dim(got, own, chip, 0).reshape(N_CHIPS, -1, own.shape[-1])
          for got, own in zip(_weight_gather(mine), mine)]
    w_in_p = _shards_to_padded(gw[0])
    wpm, wps, wo = (a.reshape(-1, D_MODEL) for a in gw[1:])

    def start_exchange(g_w_in, g_wpm, g_wps, g_wo):
        split = lambda g, rows: _bf(g).reshape(N_CHIPS, 2, rows // (2 * N_CHIPS), g.shape[-1])
        slabs = [split(_padded_to_shards(_bf(g_w_in)), N_CHIPS * D_MODEL),
                 split(g_wpm, g_wpm.shape[0]), split(g_wps, g_wps.shape[0]), split(g_wo, g_wo.shape[0])]
        pair_sums = []
        for slab, rec in zip(slabs, _pair_send(slabs)):
            kept = lax.dynamic_index_in_dim(slab, pos[2], 1, keepdims=False)
            rows = kept.shape[0] * kept.shape[1]
            both = _sum_parts(kept.reshape(rows, -1), rec.reshape(1, rows, -1), 32, BF16)
            pair_sums.append(both.reshape(kept.shape))
        send_sems, recv_sems, sums, lands, token = _chip_scatter_start(pair_sums)
        return token, (send_sems, recv_sems, sums, lands)

    grad_x, small, _, _, _, _, in_flight = _local_step(
        x[0], loss_target[0], scale, shift, gate, norm_w, w_in_p, _pad_cols(b_in), ml_conv_full, ml_conv_b, ml_norm_w,
        ssm_conv_full, ssm_conv_b, ssm_a_log, ssm_d, ssm_norm_w, wpm, wps, wo, final_w[None], start_exchange)

    g2 = _allgather8(_pack([small[name] for name, _ in SMALL_SUMS]))
    total = dict(zip([name for name, _ in SMALL_SUMS], _unpack(_sum_parts(None, g2, g2.shape[1]), [s for _, s in SMALL_SUMS])))
    dmod_all = g2[:, :3 * D_MODEL // LANE].reshape(N_DEV, 3 * D_MODEL)
    grads = dict(total)
    grads["ada_b"] = total["mod"]
    grads["ml_conv_w"] = lax.dynamic_slice_in_dim(total["ml_conv_w"], chip * mlw_cols, mlw_cols, axis=1)
    grads["ssm_conv_w"] = lax.dynamic_slice_in_dim(total["ssm_conv_w"], chip * ssw_cols, ssw_cols, axis=1)
    grads["ada_w"] = _ada_bwd(c_all, lax.dynamic_slice_in_dim(dmod_all, chip * ada_cols, ada_cols, axis=1))

    halves = []
    for both, rec in zip(*_chip_scatter_wait(*in_flight, grad_x)):
        halves.append(_sum_parts(lax.dynamic_index_in_dim(both, chip, 0, keepdims=False), rec, 32))
    for name, half, other in zip(("w_in", "w_proj_m", "w_proj_s", "w_out"), halves, _pair_exchange(halves)):
        grads[name] = jnp.where(pos[2] == 0, jnp.concatenate([half, other]), jnp.concatenate([other, half]))

    delta, new_m, new_v = {}, {}, {}
    for name in LARGE:
        if w[name].shape[-1] % LANE:
            flat = lambda a: a.reshape(a.shape[-2:]).T.reshape(-1, LANE)
            back = lambda a: a.reshape(w[name].shape[-1], w[name].shape[-2]).T.reshape(w[name].shape)
            rows = w[name].size // LANE
            tr = max(t for t in range(8, 4096, 8) if rows % t == 0)
            g_flat = flat(grads[name])
            delta[name], new_m[name], new_v[name] = (back(a) for a in _adamw(flat(w[name]), g_flat, flat(m[name]), flat(v[name]), tr))
            grads[name] = back(g_flat)
        else:
            delta[name], new_m[name], new_v[name] = _adamw(w[name], grads[name], m[name], v[name], 64)
    rest = [name for name in WEIGHTS if name not in LARGE]
    packed = [_pack([t[name] for name in rest]) for t in (w, grads, m, v)]
    for out, buf in zip((delta, new_m, new_v), _adamw(*packed, packed[0].shape[0])):
        out.update(zip(rest, _unpack(buf, [w[name].shape for name in rest])))
    loss = total["loss"][0]
    return (loss, grad_x[None], *[grads[name].reshape(w[name].shape) for name in WEIGHTS], *[delta[name] for name in WEIGHTS],
            *[new_m[name] for name in WEIGHTS], *[new_v[name] for name in WEIGHTS])
```

```python
import functools
import math

import jax
import jax.numpy as jnp
from jax import lax
from jax.experimental import pallas as pl
from jax.experimental.pallas import tpu as pltpu

F32 = jnp.float32
BF16 = jnp.bfloat16
HI = lax.Precision.HIGHEST
MESH = pl.DeviceIdType.MESH

D_MODEL = 1024
EPS = 1e-6
CONV_K = 4
ML_HEADS = 8
ML_DQK = 128
ML_DV = 256
SSM_HEADS = 32
SSM_HEADDIM = 64
SSM_GROUPS = 4
SSM_STATE = 128
IN_WIDTH = 15408
N_CHIPS = 4
N_DEV = 8
ADAM_LR, ADAM_B1, ADAM_B2, ADAM_EPS, ADAM_WD, ADAM_STEP = 0.001, 0.9, 0.999, 1e-08, 0.01, 10

O_O, O_ZM, O_ZS, O_MG, O_QK, O_V, O_XBC, O_IF, O_DT = 0, 2048, 4096, 6144, 8192, 10240, 12288, 15360, 15616
SMALL_W = 256
NP = 15872
LANE = 128
CHUNK = 128
NEG = -1e30
VMEM_LIMIT = 48 * 1024 * 1024


def _cparams(*sem):
    return pltpu.CompilerParams(dimension_semantics=sem, vmem_limit_bytes=VMEM_LIMIT)


def _pad_cols(w):
    z = lambda n: jnp.zeros(w.shape[:-1] + (n,), w.dtype)
    return jnp.concatenate([w[..., 4096:8192], w[..., 11280:13328], w[..., 13360:15408], w[..., :4096], w[..., 8208:11280],
                            w[..., 8192:8208], z(SMALL_W - 16), w[..., 13328:13360], z(SMALL_W - 32)], axis=-1)


def _unpad_cols(g):
    return jnp.concatenate([g[..., O_QK:O_QK + 4096], g[..., O_O:O_O + 4096], g[..., O_IF:O_IF + 16],
                            g[..., O_XBC:O_XBC + 3072], g[..., O_ZS:O_ZS + 2048], g[..., O_DT:O_DT + 32],
                            g[..., O_MG:O_MG + 2048]], axis=-1)


PADDED_SEGMENTS = ((4096, 8192, 0), (11280, 13328, 0), (13360, 15408, 0), (0, 4096, 0), (8208, 11280, 0),
                   (8192, 8208, SMALL_W - 16), (13328, 13360, SMALL_W - 32))
SHARD_W = IN_WIDTH // N_CHIPS


def _shards_to_padded(shards):
    parts = []
    for first, last, pad in PADDED_SEGMENTS:
        for j in range(N_CHIPS):
            lo, hi = max(first, j * SHARD_W), min(last, (j + 1) * SHARD_W)
            if lo < hi:
                parts.append(shards[j][:, lo - j * SHARD_W:hi - j * SHARD_W])
        if pad:
            parts.append(jnp.zeros((shards.shape[1], pad), shards.dtype))
    return jnp.concatenate(parts, axis=1)


def _padded_to_shards(g):
    offsets, off = {}, 0
    for first, last, pad in PADDED_SEGMENTS:
        offsets[first] = off
        off += last - first + pad
    shards = []
    for j in range(N_CHIPS):
        parts = []
        for first, last, _ in sorted(PADDED_SEGMENTS):
            lo, hi = max(first, j * SHARD_W), min(last, (j + 1) * SHARD_W)
            if lo < hi:
                parts.append(g[:, offsets[first] + lo - first:offsets[first] + hi - first])
        shards.append(jnp.concatenate(parts, axis=1))
    return jnp.stack(shards)


def _sigmoid(x):
    return 1.0 / (1.0 + jnp.exp(-x))


def _silu(x):
    return x * _sigmoid(x)


def _dsilu(x):
    s = _sigmoid(x)
    return s + x * s * (1.0 - s)


def _softplus(x):
    return jnp.maximum(x, 0.0) + jnp.log(1.0 + jnp.exp(-jnp.abs(x)))


def _logsigmoid(x):
    return jnp.minimum(x, 0.0) - jnp.log(1.0 + jnp.exp(-jnp.abs(x)))


def _dot(a, b, dims, precision=None):
    return lax.dot_general(a, b, (dims, ((), ())), preferred_element_type=F32, precision=precision)


def _nn(a, b, precision=None):
    return _dot(a, b, ((1,), (0,)), precision)


def _nt(a, b, precision=None):
    return _dot(a, b, ((1,), (1,)), precision)


def _tn(a, b, precision=None):
    return _dot(a, b, ((0,), (0,)), precision)


def _bf(x):
    return x.astype(BF16)


def _split(x, terms):
    parts = []
    for _ in range(terms):
        part = _bf(x)
        parts.append(part)
        x = x - part.astype(F32)
    return parts


def _pick_right(x, pick, terms):
    pick = _bf(pick)
    out = None
    for part in _split(x, terms):
        out = _nn(part, pick) if out is None else out + _nn(part, pick)
    return out


def _pick_left(pick, x, terms):
    pick = _bf(pick)
    out = None
    for part in _split(x, terms):
        out = _nn(pick, part) if out is None else out + _nn(pick, part)
    return out


def _lane_col(x, lane):
    idx = lax.broadcasted_iota(jnp.int32, x.shape, 1)
    return jnp.sum(jnp.where(idx == lane, x, 0.0), axis=1, keepdims=True)


def _tri(n, upper):
    r = lax.broadcasted_iota(jnp.int32, (n, n), 0)
    c = lax.broadcasted_iota(jnp.int32, (n, n), 1)
    return jnp.where((r <= c) if upper else (r >= c), 1.0, 0.0).astype(F32)


def _eye(n):
    return jnp.where(lax.broadcasted_iota(jnp.int32, (n, n), 0) == lax.broadcasted_iota(jnp.int32, (n, n), 1), 1.0, 0.0)


def _sum_all(x):
    return jnp.sum(jnp.sum(x, axis=1, keepdims=True), axis=0, keepdims=True)


def _crossing(p):
    L = p.shape[0]
    hi = _bf(p)
    lo = _bf(p - hi.astype(F32))
    upper = _bf(_tri(L, True))
    below = _nn(upper, hi) + _nn(upper, lo)
    strict = lax.broadcasted_iota(jnp.int32, (L, L), 0) > lax.broadcasted_iota(jnp.int32, (L, L), 1)
    return jnp.sum(jnp.where(strict, below, 0.0), axis=1, keepdims=True)


def _matmul_bias(a, w, bias, tm, tn):
    m, k = a.shape
    n = w.shape[1]

    def body(a_ref, w_ref, b_ref, o_ref):
        o_ref[...] = _nn(a_ref[...], w_ref[...]) + b_ref[...]

    return pl.pallas_call(
        body, name="matmul_bias", grid=(m // tm, n // tn),
        in_specs=[pl.BlockSpec((tm, k), lambda i, j: (i, 0)), pl.BlockSpec((k, tn), lambda i, j: (0, j)),
                  pl.BlockSpec((1, tn), lambda i, j: (0, j))],
        out_specs=pl.BlockSpec((tm, tn), lambda i, j: (i, j)),
        out_shape=jax.ShapeDtypeStruct((m, n), F32),
        compiler_params=_cparams("parallel", "arbitrary"))(a, w, bias)


def _matmul_nt(a, w, tm, tk, after=None):
    m, n = a.shape
    k = w.shape[0]

    def body(a_ref, w_ref, *rest):
        o_ref = rest[-1]

        @pl.when(pl.program_id(1) == 0)
        def _():
            o_ref[...] = jnp.zeros_like(o_ref)
        o_ref[...] += _nt(a_ref[...], w_ref[...])

    extra = [] if after is None else [after]
    return pl.pallas_call(
        body, name="matmul_nt", grid=(m // tm, n // tk),
        in_specs=[pl.BlockSpec((tm, tk), lambda i, j: (i, j)), pl.BlockSpec((k, tk), lambda i, j: (0, j))]
        + [pl.BlockSpec(memory_space=pl.ANY)] * len(extra),
        out_specs=pl.BlockSpec((tm, k), lambda i, j: (i, 0)),
        out_shape=jax.ShapeDtypeStruct((m, k), F32),
        compiler_params=_cparams("parallel", "arbitrary"))(a, w, *extra)


def _matmul_tn(a, b, tm, tn, with_colsum=False, a_is_transposed=False):
    k, m = a.shape if a_is_transposed else a.shape[::-1]
    n = b.shape[1]

    def body(a_ref, b_ref, o_ref, *rest):
        first = pl.program_id(1) == 0

        @pl.when(first)
        def _():
            o_ref[...] = jnp.zeros_like(o_ref)
        o_ref[...] += _nn(a_ref[...], b_ref[...]) if a_is_transposed else _tn(a_ref[...], b_ref[...])
        if with_colsum:
            s_ref = rest[0]

            @pl.when(first)
            def _():
                s_ref[...] = jnp.zeros_like(s_ref)
            s_ref[...] += jnp.sum(b_ref[...].astype(F32), axis=0, keepdims=True)

    out_specs = [pl.BlockSpec((k, tn), lambda j, i: (0, j))]
    out_shape = [jax.ShapeDtypeStruct((k, n), F32)]
    if with_colsum:
        out_specs.append(pl.BlockSpec((1, tn), lambda j, i: (0, j)))
        out_shape.append(jax.ShapeDtypeStruct((1, n), F32))
    out = pl.pallas_call(
        body, name="matmul_tn", grid=(n // tn, m // tm),
        in_specs=[pl.BlockSpec((k, tm), lambda j, i: (0, i)) if a_is_transposed else pl.BlockSpec((tm, k), lambda j, i: (i, 0)),
                  pl.BlockSpec((tm, tn), lambda j, i: (i, j))],
        out_specs=out_specs, out_shape=out_shape,
        compiler_params=_cparams("parallel", "arbitrary"))(a, b)
    return out if with_colsum else out[0]


def _ada_fwd(c_all, ada_w, ada_b):
    def body(c_ref, w_ref, b_ref, o_ref):
        o_ref[...] = _nn(_bf(_silu(c_ref[...])), _bf(w_ref[...])) + b_ref[...]

    return pl.pallas_call(body, name="ada_fwd", out_shape=jax.ShapeDtypeStruct((c_all.shape[0], ada_w.shape[1]), F32),
                          compiler_params=_cparams())(c_all, ada_w, ada_b)


def _ada_bwd(c_all, dmod):
    def body(c_ref, d_ref, o_ref):
        o_ref[...] = _tn(_bf(_silu(c_ref[...])), _bf(d_ref[...]))

    return pl.pallas_call(body, name="ada_bwd", out_shape=jax.ShapeDtypeStruct((c_all.shape[1], dmod.shape[1]), F32),
                          compiler_params=_cparams())(c_all, dmod)


def _prenorm_fwd(x, norm_w, scale, shift, ts):
    s, d = x.shape

    def body(x_ref, nw_ref, sc_ref, sh_ref, u_ref):
        xv = x_ref[...]
        r = lax.rsqrt(jnp.mean(xv * xv, axis=1, keepdims=True) + EPS)
        u_ref[...] = _bf(xv * r * nw_ref[...] * (1.0 + sc_ref[...]) + sh_ref[...])

    row = pl.BlockSpec((1, d), lambda i: (0, 0))
    return pl.pallas_call(
        body, name="prenorm_fwd", grid=(s // ts,),
        in_specs=[pl.BlockSpec((ts, d), lambda i: (i, 0)), row, row, row],
        out_specs=pl.BlockSpec((ts, d), lambda i: (i, 0)), out_shape=jax.ShapeDtypeStruct((s, d), BF16),
        compiler_params=_cparams("parallel"))(x, norm_w, scale, shift)


def _prenorm_bwd(du, x, dxres, norm_w, scale, ts):
    s, d = x.shape

    def body(du_ref, x_ref, dr_ref, nw_ref, sc_ref, gx_ref, acc_ref):
        @pl.when(pl.program_id(0) == 0)
        def _():
            acc_ref[...] = jnp.zeros_like(acc_ref)
        xv, duv = x_ref[...], du_ref[...]
        r = lax.rsqrt(jnp.mean(xv * xv, axis=1, keepdims=True) + EPS)
        xn = xv * r
        nw, sc1 = nw_ref[...], 1.0 + sc_ref[...]
        dxn = duv * (nw * sc1)
        gx_ref[...] = r * (dxn - xn * jnp.mean(dxn * xn, axis=1, keepdims=True)) + dr_ref[...]
        t = duv * xn
        acc_ref[0:1, :] += jnp.sum(t, axis=0, keepdims=True) * sc1
        acc_ref[1:2, :] += jnp.sum(t, axis=0, keepdims=True) * nw
        acc_ref[2:3, :] += jnp.sum(duv, axis=0, keepdims=True)

    tile = pl.BlockSpec((ts, d), lambda i: (i, 0))
    row = pl.BlockSpec((1, d), lambda i: (0, 0))
    return pl.pallas_call(
        body, name="prenorm_bwd", grid=(s // ts,),
        in_specs=[tile, tile, tile, row, row],
        out_specs=[tile, pl.BlockSpec((8, d), lambda i: (0, 0))],
        out_shape=[jax.ShapeDtypeStruct((s, d), F32), jax.ShapeDtypeStruct((8, d), F32)],
        compiler_params=_cparams("arbitrary"))(du, x, dxres, norm_w, scale)


CONV_CB = 512


def _conv_taps(buf_ref, ts):
    return [buf_ref[pl.ds(8 - (CONV_K - 1) + j, ts), :] for j in range(CONV_K)]


def _conv_fwd(proj, col0, width, w8, b, ts):
    s = proj.shape[0]
    cb = CONV_CB
    nt = s // ts

    def body(x_ref, w_ref, b_ref, o_ref, ds_ref, buf_ref):
        @pl.when(pl.program_id(1) == 0)
        def _():
            buf_ref[0:8, :] = jnp.zeros((8, cb), F32)
        buf_ref[pl.ds(8, ts), :] = x_ref[...]
        acc = b_ref[...] + jnp.zeros((ts, cb), F32)
        for j, tap in enumerate(_conv_taps(buf_ref, ts)):
            acc = acc + tap * w_ref[j:j + 1, :]
        sg = _sigmoid(acc)
        o_ref[...] = acc * sg
        ds_ref[...] = _bf(sg + acc * sg * (1.0 - sg))
        buf_ref[0:8, :] = x_ref[pl.ds(ts - 8, 8), :]

    c0 = col0 // cb
    tile = pl.BlockSpec((ts, cb), lambda c, i: (i, c))
    return pl.pallas_call(
        body, name="conv_fwd", grid=(width // cb, nt),
        in_specs=[pl.BlockSpec((ts, cb), lambda c, i: (i, c0 + c)), pl.BlockSpec((8, cb), lambda c, i: (0, c)),
                  pl.BlockSpec((1, cb), lambda c, i: (0, c))],
        out_specs=[tile, tile],
        out_shape=[jax.ShapeDtypeStruct((s, width), F32), jax.ShapeDtypeStruct((s, width), BF16)],
        scratch_shapes=[pltpu.VMEM((ts + 8, cb), F32)],
        compiler_params=_cparams("parallel", "arbitrary"))(proj, w8, b)


def _conv_bwd(proj, col0, width, w8, dact, dpost, dproj, ts):
    s = proj.shape[0]
    cb = CONV_CB
    nt = s // ts
    c0 = col0 // cb

    def body(x_ref, da_ref, dp_ref, w_ref, _, dx_ref, acc_ref, dbuf_ref):
        @pl.when(pl.program_id(1) == 0)
        def _():
            acc_ref[...] = jnp.zeros_like(acc_ref)
            dbuf_ref[pl.ds(ts, 8), :] = jnp.zeros((8, cb), F32)
        dconv = dp_ref[...] * da_ref[...].astype(F32)
        acc_ref[CONV_K:CONV_K + 1, :] += jnp.sum(dconv, axis=0, keepdims=True)
        dbuf_ref[pl.ds(0, ts), :] = dconv
        xv = x_ref[...]
        dx = jnp.zeros((ts, cb), F32)
        for j in range(CONV_K):
            shifted = dbuf_ref[pl.ds(CONV_K - 1 - j, ts), :]
            dx = dx + shifted * w_ref[j:j + 1, :]
            acc_ref[j:j + 1, :] += jnp.sum(xv * shifted, axis=0, keepdims=True)
        dx_ref[...] = _bf(dx)
        dbuf_ref[pl.ds(ts, 8), :] = dconv[0:8, :]

    tile = pl.BlockSpec((ts, cb), lambda c, i: (nt - 1 - i, c))
    wide = pl.BlockSpec((ts, cb), lambda c, i: (nt - 1 - i, c0 + c))
    return pl.pallas_call(
        body, name="conv_bwd", grid=(width // cb, nt),
        in_specs=[wide, tile, tile, pl.BlockSpec((8, cb), lambda c, i: (0, c)), pl.BlockSpec(memory_space=pl.ANY)],
        out_specs=[wide, pl.BlockSpec((8, cb), lambda c, i: (0, c))],
        out_shape=[jax.ShapeDtypeStruct(dproj.shape, dproj.dtype), jax.ShapeDtypeStruct((8, width), F32)],
        input_output_aliases={4: 0},
        scratch_shapes=[pltpu.VMEM((ts + 8, cb), F32)],
        compiler_params=_cparams("parallel", "arbitrary"))(proj, dact, dpost, w8, dproj)


def _mlstm_gates(gif_ref, gt_ref, a_scr, at_scr):
    L = gif_ref.shape[0]
    fb = _logsigmoid(gif_ref[...])
    a_scr[...] = _pick_left(_tri(L, False), fb, 3)
    at_scr[...] = _pick_right(_logsigmoid(gt_ref[...]), _tri(L, True), 3)
    return jnp.sum(fb, axis=0, keepdims=True)


def _mlstm_head(h, qk_ref, v_ref, gif, gt_ref, a, at_scr, a_last_row, c_mat, n_row, m_prev):
    L = gif.shape[0]
    q = qk_ref[:, h * ML_DQK:(h + 1) * ML_DQK] * (ML_DQK ** -0.5)
    k = qk_ref[:, (ML_HEADS + h) * ML_DQK:(ML_HEADS + h + 1) * ML_DQK]
    v = v_ref[:, h * ML_DV:(h + 1) * ML_DV]
    i_col, a_col = _lane_col(gif, h), _lane_col(a, ML_HEADS + h)
    i_row, a_row = gt_ref[h:h + 1, :], at_scr[ML_HEADS + h:ML_HEADS + h + 1, :]
    causal = lax.broadcasted_iota(jnp.int32, (L, L), 0) >= lax.broadcasted_iota(jnp.int32, (L, L), 1)
    dmat = jnp.where(causal, a_col - a_row + i_row, NEG)
    inter = a_col + m_prev
    m_t = jnp.maximum(inter, jnp.max(dmat, axis=1, keepdims=True))
    w_intra = jnp.exp(dmat - m_t)
    w_inter = jnp.exp(inter - m_t)
    sc = _nt(_bf(q), _bf(k)) * w_intra
    den = jnp.sum(sc, axis=1, keepdims=True) + w_inter * jnp.sum(q * n_row, axis=1, keepdims=True)
    floor = jnp.exp(-m_t)
    a_last = _lane_col(a_last_row, ML_HEADS + h)
    g = a_last - a_col + i_col
    m_new = jnp.maximum(a_last + m_prev, jnp.max(g, axis=0, keepdims=True))
    wk = jnp.exp(g - m_new)
    decay = jnp.exp(a_last + m_prev - m_new)
    return dict(q=q, k=k, v=v, w_intra=w_intra, w_inter=w_inter, sc=sc, den=den, floor=floor, m_new=m_new, wk=wk,
                decay=decay)


def _state_tile(n_row, m11):
    r = lax.broadcasted_iota(jnp.int32, (8, LANE), 0)
    return jnp.where(r == 0, n_row, jnp.where(r == 1, m11, 0.0))


def _mlstm_fwd(qk, proj, gt):
    s = qk.shape[0]
    L = CHUNK
    nc = s // L

    def body(qk_ref, v_ref, gif_ref, gt_ref, h_ref, cst_ref, nm_ref, c_scr, nm_scr, a_scr, at_scr):
        @pl.when(pl.program_id(0) == 0)
        def _():
            c_scr[...] = jnp.zeros_like(c_scr)
            nm_scr[...] = jnp.zeros_like(nm_scr)
        a_last_row = _mlstm_gates(gif_ref, gt_ref, a_scr, at_scr)
        gif, a = gif_ref[...], a_scr[...]
        for h in range(ML_HEADS):
            c_mat, n_row = c_scr[h], nm_scr[h, 0:1, :]
            m_prev = jnp.max(nm_scr[h, 1:2, :], axis=1, keepdims=True)
            cst_ref[0, h] = c_mat
            nm_ref[0, h] = nm_scr[h]
            t = _mlstm_head(h, qk_ref, v_ref, gif, gt_ref, a, at_scr, a_last_row, c_mat, n_row, m_prev)
            num = _nn(_bf(t["sc"]), _bf(t["v"])) + t["w_inter"] * _nn(_bf(t["q"]), _bf(c_mat))
            h_ref[:, h * ML_DV:(h + 1) * ML_DV] = num / jnp.maximum(jnp.abs(t["den"]), t["floor"])
            kw = t["k"] * t["wk"]
            c_scr[h] = t["decay"] * c_mat + _tn(_bf(kw), _bf(t["v"]))
            nm_scr[h] = _state_tile(t["decay"] * n_row + jnp.sum(kw, axis=0, keepdims=True), t["m_new"])

    return pl.pallas_call(
        body, name="mlstm_fwd", grid=(nc,),
        in_specs=[pl.BlockSpec((L, 2048), lambda c: (c, 0)), pl.BlockSpec((L, 2048), lambda c: (c, O_V // 2048)),
                  pl.BlockSpec((L, LANE), lambda c: (c, O_IF // LANE)), pl.BlockSpec((LANE, L), lambda c: (0, c))],
        out_specs=[pl.BlockSpec((L, 2048), lambda c: (c, 0)),
                   pl.BlockSpec((1, ML_HEADS, ML_DQK, ML_DV), lambda c: (c, 0, 0, 0)),
                   pl.BlockSpec((1, ML_HEADS, 8, LANE), lambda c: (c, 0, 0, 0))],
        out_shape=[jax.ShapeDtypeStruct((s, 2048), F32), jax.ShapeDtypeStruct((nc, ML_HEADS, ML_DQK, ML_DV), F32),
                   jax.ShapeDtypeStruct((nc, ML_HEADS, 8, LANE), F32)],
        scratch_shapes=[pltpu.VMEM((ML_HEADS, ML_DQK, ML_DV), F32), pltpu.VMEM((ML_HEADS, 8, LANE), F32),
                        pltpu.VMEM((L, LANE), F32), pltpu.VMEM((LANE, L), F32)],
        compiler_params=_cparams("arbitrary"))(qk, proj, proj, gt)


def _mlstm_bwd(qk, proj, gt, hout, dh, cst, nm, dproj):
    s = qk.shape[0]
    L = CHUNK
    nc = s // L

    def body(qk_ref, v_ref, gif_ref, gt_ref, h_ref, dh_ref, cst_ref, nm_ref, _, dqk_ref, dv_ref, dif_ref,
             dc_scr, dn_scr, a_scr, at_scr):
        @pl.when(pl.program_id(0) == 0)
        def _():
            dc_scr[...] = jnp.zeros_like(dc_scr)
            dn_scr[...] = jnp.zeros_like(dn_scr)
        a_last_row = _mlstm_gates(gif_ref, gt_ref, a_scr, at_scr)
        gif, a = gif_ref[...], a_scr[...]
        lane = lax.broadcasted_iota(jnp.int32, (L, LANE), 1)
        last = lax.broadcasted_iota(jnp.int32, (L, 1), 0) == L - 1
        di_tile = jnp.zeros((L, LANE), F32)
        cross = [jnp.zeros((L, LANE), F32)] * 3
        for h in range(ML_HEADS):
            c_mat, n_row = cst_ref[0, h], nm_ref[0, h, 0:1, :]
            m_prev = jnp.max(nm_ref[0, h, 1:2, :], axis=1, keepdims=True)
            t = _mlstm_head(h, qk_ref, v_ref, gif, gt_ref, a, at_scr, a_last_row, c_mat, n_row, m_prev)
            q, k, v, den = t["q"], t["k"], t["v"], t["den"]
            dhh = dh_ref[:, h * ML_DV:(h + 1) * ML_DV].astype(F32)
            hh = h_ref[:, h * ML_DV:(h + 1) * ML_DV]
            dnorm = jnp.maximum(jnp.abs(den), t["floor"])
            dnum = dhh / dnorm
            d_dn = -jnp.sum(dhh * hh, axis=1, keepdims=True) / dnorm
            dden = jnp.where(jnp.abs(den) >= t["floor"], jnp.where(den >= 0.0, d_dn, -d_dn), 0.0)
            dsc = _nt(_bf(dnum), _bf(v)) + dden
            ds = dsc * t["w_intra"]
            dq_inter = t["w_inter"] * (_nt(_bf(dnum), _bf(c_mat)) + dden * n_row)
            dq = _nn(_bf(ds), _bf(k)) + dq_inter
            dc, dn_row = dc_scr[h], dn_scr[h, 0:1, :]
            dk_state = t["wk"] * (_nt(_bf(v), _bf(dc)) + dn_row)
            dk = _tn(_bf(ds), _bf(q)) + dk_state
            dv = _tn(_bf(t["sc"]), _bf(dnum)) + t["wk"] * _nn(_bf(k), _bf(dc))
            qi = q * t["w_inter"]
            dc_scr[h] = t["decay"] * dc + _tn(_bf(qi), _bf(dnum))
            dn_scr[h] = jnp.broadcast_to(t["decay"] * dn_row + jnp.sum(qi * dden, axis=0, keepdims=True), (8, LANE))
            dqk_ref[:, h * ML_DQK:(h + 1) * ML_DQK] = dq * (ML_DQK ** -0.5)
            dqk_ref[:, (ML_HEADS + h) * ML_DQK:(ML_HEADS + h + 1) * ML_DQK] = dk
            dv_ref[:, h * ML_DV:(h + 1) * ML_DV] = _bf(dv)
            di_tile = di_tile + jnp.where(lane == h, jnp.sum(k * dk, axis=1, keepdims=True), 0.0)
            carried = t["decay"] * (_sum_all(dc * c_mat) + jnp.sum(dn_row * n_row, axis=1, keepdims=True))
            parts = (_crossing(dsc * t["sc"]),
                     jnp.sum(q * dq_inter, axis=1, keepdims=True) + jnp.where(last, carried, 0.0),
                     jnp.sum(k * dk_state, axis=1, keepdims=True))
            cross = [c + jnp.where(lane == ML_HEADS + h, p, 0.0) for c, p in zip(cross, parts)]
        dfb = cross[0] + _pick_left(_tri(L, True), cross[1], 2) + _pick_left(_tri(L, False) - _eye(L), cross[2], 2)
        dif_ref[:, 0:LANE] = _bf(di_tile + dfb * _sigmoid(-gif))
        dif_ref[:, LANE:SMALL_W] = jnp.zeros((L, SMALL_W - LANE), BF16)

    rev = lambda c: nc - 1 - c
    return pl.pallas_call(
        body, name="mlstm_bwd", grid=(nc,),
        in_specs=[pl.BlockSpec((L, 2048), lambda c: (rev(c), 0)), pl.BlockSpec((L, 2048), lambda c: (rev(c), O_V // 2048)),
                  pl.BlockSpec((L, LANE), lambda c: (rev(c), O_IF // LANE)), pl.BlockSpec((LANE, L), lambda c: (0, rev(c))),
                  pl.BlockSpec((L, 2048), lambda c: (rev(c), 0)), pl.BlockSpec((L, 2048), lambda c: (rev(c), 0)),
                  pl.BlockSpec((1, ML_HEADS, ML_DQK, ML_DV), lambda c: (rev(c), 0, 0, 0)),
                  pl.BlockSpec((1, ML_HEADS, 8, LANE), lambda c: (rev(c), 0, 0, 0)), pl.BlockSpec(memory_space=pl.ANY)],
        out_specs=[pl.BlockSpec((L, 2048), lambda c: (rev(c), 0)), pl.BlockSpec((L, 2048), lambda c: (rev(c), O_V // 2048)),
                   pl.BlockSpec((L, SMALL_W), lambda c: (rev(c), 0))],
        out_shape=[jax.ShapeDtypeStruct((s, 2048), F32), jax.ShapeDtypeStruct(dproj.shape, dproj.dtype),
                   jax.ShapeDtypeStruct((s, SMALL_W), BF16)],
        input_output_aliases={8: 1},
        scratch_shapes=[pltpu.VMEM((ML_HEADS, ML_DQK, ML_DV), F32), pltpu.VMEM((ML_HEADS, 8, LANE), F32),
                        pltpu.VMEM((L, LANE), F32), pltpu.VMEM((LANE, L), F32)],
        compiler_params=_cparams("arbitrary"))(qk, proj, proj, gt, hout, dh, cst, nm, dproj)


GROUP_W = SSM_HEADS // SSM_GROUPS * SSM_HEADDIM
O_B = SSM_HEADS * SSM_HEADDIM
O_C = O_B + SSM_GROUPS * SSM_STATE


def _head_expand():
    r = jnp.arange(LANE)[:, None]
    c = jnp.arange(SSM_HEADS * SSM_HEADDIM)[None, :] // SSM_HEADDIM
    return (r == c).astype(F32)


def _ssd_gates(dt_ref, dtt_ref, alog_row_ref, alog_col_ref, at_scr):
    L = dt_ref.shape[0]
    dt = _softplus(dt_ref[...])
    acoef = -jnp.exp(alog_row_ref[...])
    a = _pick_left(_tri(L, False), dt * acoef, 3)
    at_scr[...] = _pick_right(_softplus(dtt_ref[...]) * (-jnp.exp(alog_col_ref[...])), _tri(L, True), 3)
    return dt, acoef, a


def _ssd_group(g, xbc_ref, dt, a, e_ref, ax_scr):
    eg = e_ref[:, g * GROUP_W:(g + 1) * GROUP_W]
    ax_scr[...] = _pick_right(a, eg, 3)
    ax = ax_scr[...]
    alx = ax_scr[ax.shape[0] - 1:ax.shape[0], :]
    dtx = _pick_right(dt, eg, 2)
    xg = xbc_ref[:, g * GROUP_W:(g + 1) * GROUP_W]
    bg = xbc_ref[:, O_B + g * SSM_STATE:O_B + (g + 1) * SSM_STATE]
    cg = xbc_ref[:, O_C + g * SSM_STATE:O_C + (g + 1) * SSM_STATE]
    return dict(ax=ax, alx=alx, dtx=dtx, xg=xg, bg=bg, cg=cg, xdt=xg * dtx, gmat=_nt(_bf(cg), _bf(bg)))


def _ssd_decay(hh, a, at_scr):
    L = a.shape[0]
    causal = lax.broadcasted_iota(jnp.int32, (L, L), 0) >= lax.broadcasted_iota(jnp.int32, (L, L), 1)
    return jnp.exp(jnp.where(causal, _lane_col(a, hh) - at_scr[hh:hh + 1, :], NEG))


def _ssd_fwd(xbc, proj, dtt, alog_row, alog_col, dskip_x, expand):
    s = xbc.shape[0]
    L = CHUNK
    nc = s // L
    half = SSM_HEADDIM

    def body(xbc_ref, dt_ref, dtt_ref, ar_ref, ac_ref, dk_ref, e_ref, y_ref, st_ref, st_scr, at_scr, ax_scr):
        @pl.when(pl.program_id(0) == 0)
        def _():
            st_scr[...] = jnp.zeros_like(st_scr)
        dt, _, a = _ssd_gates(dt_ref, dtt_ref, ar_ref, ac_ref, at_scr)
        lane = lax.broadcasted_iota(jnp.int32, (L, LANE), 1)
        for g in range(SSM_GROUPS):
            t = _ssd_group(g, xbc_ref, dt, a, e_ref, ax_scr)
            st = st_scr[g]
            st_ref[0, g] = st
            pairs = []
            for j in range(GROUP_W // LANE):
                xp = _bf(t["xdt"][:, j * LANE:(j + 1) * LANE])
                hh = g * (SSM_HEADS // SSM_GROUPS) + 2 * j
                y0 = _nn(_bf(t["gmat"] * _ssd_decay(hh, a, at_scr)), xp)
                y1 = _nn(_bf(t["gmat"] * _ssd_decay(hh + 1, a, at_scr)), xp)
                pairs.append(jnp.where(lane < half, y0, y1))
            y = jnp.concatenate(pairs, axis=1) + _nn(_bf(t["cg"]), _bf(st)) * jnp.exp(t["ax"])
            y_ref[:, g * GROUP_W:(g + 1) * GROUP_W] = y + dk_ref[:, g * GROUP_W:(g + 1) * GROUP_W] * t["xg"]
            wts = jnp.exp(t["alx"] - t["ax"])
            st_scr[g] = jnp.exp(t["alx"]) * st + _tn(_bf(t["bg"]), _bf(t["xdt"] * wts))

    row = lambda w: pl.BlockSpec((1, w), lambda c: (0, 0))
    return pl.pallas_call(
        body, name="ssd_fwd", grid=(nc,),
        in_specs=[pl.BlockSpec((L, 3072), lambda c: (c, 0)), pl.BlockSpec((L, LANE), lambda c: (c, O_DT // LANE)),
                  pl.BlockSpec((LANE, L), lambda c: (0, c)), row(LANE), pl.BlockSpec((LANE, 1), lambda c: (0, 0)),
                  row(2048), pl.BlockSpec((LANE, 2048), lambda c: (0, 0))],
        out_specs=[pl.BlockSpec((L, 2048), lambda c: (c, 0)),
                   pl.BlockSpec((1, SSM_GROUPS, SSM_STATE, GROUP_W), lambda c: (c, 0, 0, 0))],
        out_shape=[jax.ShapeDtypeStruct((s, 2048), F32),
                   jax.ShapeDtypeStruct((nc, SSM_GROUPS, SSM_STATE, GROUP_W), F32)],
        scratch_shapes=[pltpu.VMEM((SSM_GROUPS, SSM_STATE, GROUP_W), F32), pltpu.VMEM((LANE, L), F32),
                        pltpu.VMEM((L, GROUP_W), F32)],
        compiler_params=_cparams("arbitrary"))(xbc, proj, dtt, alog_row, alog_col, dskip_x, expand)


def _ssd_bwd(xbc, proj, dtt, alog_row, alog_col, dskip_x, expand, expand_t, dy, states):
    s = xbc.shape[0]
    L = CHUNK
    nc = s // L
    half = SSM_HEADDIM

    def body(xbc_ref, dt_ref, dtt_ref, ar_ref, ac_ref, dk_ref, e_ref, et_ref, dy_ref, st_ref,
             dxbc_ref, ddt_ref, accd_ref, acca_ref, dst_scr, at_scr, ax_scr):
        @pl.when(pl.program_id(0) == 0)
        def _():
            dst_scr[...] = jnp.zeros_like(dst_scr)
            accd_ref[...] = jnp.zeros_like(accd_ref)
            acca_ref[...] = jnp.zeros_like(acca_ref)
        dt, acoef, a = _ssd_gates(dt_ref, dtt_ref, ar_ref, ac_ref, at_scr)
        lane = lax.broadcasted_iota(jnp.int32, (L, LANE), 1)
        low = lane < half
        last = lax.broadcasted_iota(jnp.int32, (L, 1), 0) == L - 1
        cross = [jnp.zeros((L, LANE), F32)] * 3
        ddt_tile = jnp.zeros((L, LANE), F32)
        for g in range(SSM_GROUPS):
            t = _ssd_group(g, xbc_ref, dt, a, e_ref, ax_scr)
            xg, bg, cg, xdt, gmat = t["xg"], t["bg"], t["cg"], t["xdt"], t["gmat"]
            st, dst = st_ref[0, g], dst_scr[g]
            dyg = dy_ref[:, g * GROUP_W:(g + 1) * GROUP_W].astype(F32)
            ea, eal = jnp.exp(t["ax"]), jnp.exp(t["alx"])
            wts = jnp.exp(t["alx"] - t["ax"])
            dyi = dyg * ea
            y_inter = _nn(_bf(cg), _bf(st)) * ea
            dc = _nt(_bf(dyi), _bf(st))
            d_xdt_state = _nn(_bf(bg), _bf(dst)) * wts
            db = _nt(_bf(xdt * wts), _bf(dst))
            dst_scr[g] = eal * dst + _tn(_bf(cg), _bf(dyi))
            dg = jnp.zeros((L, L), F32)
            dx_pairs = []
            for j in range(GROUP_W // LANE):
                xp = _bf(xdt[:, j * LANE:(j + 1) * LANE])
                dyp = dyg[:, j * LANE:(j + 1) * LANE]
                dxs = []
                for b in range(2):
                    hh = g * (SSM_HEADS // SSM_GROUPS) + 2 * j + b
                    dec = _ssd_decay(hh, a, at_scr)
                    w = gmat * dec
                    dxs.append(_tn(_bf(w), _bf(dyp)))
                    dw = _nt(_bf(jnp.where(low if b == 0 else ~low, dyp, 0.0)), xp)
                    dg = dg + dw * dec
                    cross[0] = cross[0] + jnp.where(lane == hh, _crossing(dw * w), 0.0)
                dx_pairs.append(jnp.where(low, dxs[0], dxs[1]))
            d_xdt = d_xdt_state + jnp.concatenate(dx_pairs, axis=1)
            dc = dc + _nn(_bf(dg), _bf(bg))
            db = db + _tn(_bf(dg), _bf(cg))
            etg = et_ref[g * GROUP_W:(g + 1) * GROUP_W, :]
            carried = jnp.sum(dst * st, axis=0, keepdims=True) * eal
            cross[1] = cross[1] + _pick_right(dyg * y_inter + jnp.where(last, carried, 0.0), etg, 2)
            cross[2] = cross[2] + _pick_right(xdt * d_xdt_state, etg, 2)
            ddt_tile = ddt_tile + _pick_right(d_xdt * xg, etg, 2)
            dxbc_ref[:, g * GROUP_W:(g + 1) * GROUP_W] = d_xdt * t["dtx"] + dk_ref[:, g * GROUP_W:(g + 1) * GROUP_W] * dyg
            dxbc_ref[:, O_B + g * SSM_STATE:O_B + (g + 1) * SSM_STATE] = db
            dxbc_ref[:, O_C + g * SSM_STATE:O_C + (g + 1) * SSM_STATE] = dc
            accd_ref[0:1, g * GROUP_W:(g + 1) * GROUP_W] += jnp.sum(dyg * xg, axis=0, keepdims=True)
        d_da = cross[0] + _pick_left(_tri(L, True), cross[1], 2) + _pick_left(_tri(L, False) - _eye(L), cross[2], 2)
        acca_ref[0:1, :] += jnp.sum(d_da * dt, axis=0, keepdims=True)
        ddt_ref[:, 0:LANE] = _bf((ddt_tile + d_da * acoef) * _sigmoid(dt_ref[...]))
        ddt_ref[:, LANE:SMALL_W] = jnp.zeros((L, SMALL_W - LANE), BF16)

    rev = lambda c: nc - 1 - c
    row = lambda w: pl.BlockSpec((1, w), lambda c: (0, 0))
    return pl.pallas_call(
        body, name="ssd_bwd", grid=(nc,),
        in_specs=[pl.BlockSpec((L, 3072), lambda c: (rev(c), 0)), pl.BlockSpec((L, LANE), lambda c: (rev(c), O_DT // LANE)),
                  pl.BlockSpec((LANE, L), lambda c: (0, rev(c))), row(LANE), pl.BlockSpec((LANE, 1), lambda c: (0, 0)),
                  row(2048), pl.BlockSpec((LANE, 2048), lambda c: (0, 0)), pl.BlockSpec((2048, LANE), lambda c: (0, 0)),
                  pl.BlockSpec((L, 2048), lambda c: (rev(c), 0)),
                  pl.BlockSpec((1, SSM_GROUPS, SSM_STATE, GROUP_W), lambda c: (rev(c), 0, 0, 0))],
        out_specs=[pl.BlockSpec((L, 3072), lambda c: (rev(c), 0)), pl.BlockSpec((L, SMALL_W), lambda c: (rev(c), 0)),
                   pl.BlockSpec((8, 2048), lambda c: (0, 0)), pl.BlockSpec((8, LANE), lambda c: (0, 0))],
        out_shape=[jax.ShapeDtypeStruct((s, 3072), F32), jax.ShapeDtypeStruct((s, SMALL_W), BF16),
                   jax.ShapeDtypeStruct((8, 2048), F32), jax.ShapeDtypeStruct((8, LANE), F32)],
        scratch_shapes=[pltpu.VMEM((SSM_GROUPS, SSM_STATE, GROUP_W), F32),
                        pltpu.VMEM((LANE, L), F32), pltpu.VMEM((L, GROUP_W), F32)],
        compiler_params=_cparams("arbitrary"))(xbc, proj, dtt, alog_row, alog_col, dskip_x, expand, expand_t, dy, states)


def _group_norm(v, width):
    outs, rs = [], []
    for k in range(v.shape[1] // width):
        blk = v[:, k * width:(k + 1) * width]
        r = lax.rsqrt(jnp.mean(blk * blk, axis=1, keepdims=True) + EPS)
        outs.append(blk * r)
        rs.append(jnp.broadcast_to(r, blk.shape))
    return jnp.concatenate(outs, axis=1), jnp.concatenate(rs, axis=1)


def _group_mean(v, width):
    return jnp.concatenate([jnp.broadcast_to(jnp.mean(v[:, k * width:(k + 1) * width], axis=1, keepdims=True),
                                             (v.shape[0], width)) for k in range(v.shape[1] // width)], axis=1)


def _post_fwd(hm, yssd, proj, ml_norm_w, ssm_norm_w, ts):
    s = hm.shape[0]

    def body(h_ref, ys_ref, o_ref, zm_ref, zs_ref, wm_ref, ws_ref, ym_ref, yso_ref):
        hn, _ = _group_norm(h_ref[...], ML_DV)
        ym_ref[...] = _bf(_sigmoid(o_ref[...]) * hn * wm_ref[...] * _silu(zm_ref[...]))
        pn, _ = _group_norm(ys_ref[...] * _silu(zs_ref[...]), GROUP_W)
        yso_ref[...] = _bf(pn * ws_ref[...])

    tile = pl.BlockSpec((ts, 2048), lambda i: (i, 0))
    col = lambda off: pl.BlockSpec((ts, 2048), lambda i: (i, off // 2048))
    row = pl.BlockSpec((1, 2048), lambda i: (0, 0))
    return pl.pallas_call(
        body, name="post_fwd", grid=(s // ts,),
        in_specs=[tile, tile, col(O_O), col(O_ZM), col(O_ZS), row, row],
        out_specs=[tile, tile],
        out_shape=[jax.ShapeDtypeStruct((s, 2048), BF16)] * 2,
        compiler_params=_cparams("parallel"))(hm, yssd, proj, proj, proj, ml_norm_w, ssm_norm_w)


def _post_bwd(dym, dys, hm, yssd, proj, ml_norm_w, ssm_norm_w, dproj, ts):
    s = hm.shape[0]

    def body(dym_ref, dys_ref, h_ref, ys_ref, o_ref, zm_ref, zs_ref, wm_ref, ws_ref, _,
             dh_ref, dyssd_ref, dp_ref, acc_ref):
        @pl.when(pl.program_id(0) == 0)
        def _():
            acc_ref[...] = jnp.zeros_like(acc_ref)
        hn, r = _group_norm(h_ref[...], ML_DV)
        so, zm, wm, d_ym = _sigmoid(o_ref[...]), zm_ref[...], wm_ref[...], dym_ref[...].astype(F32)
        sz = _silu(zm)
        hnw = hn * wm
        dp_ref[:, O_O:O_O + 2048] = _bf(d_ym * hnw * sz * so * (1.0 - so))
        dp_ref[:, O_ZM:O_ZM + 2048] = _bf(d_ym * so * hnw * _dsilu(zm))
        dhnw = d_ym * so * sz
        acc_ref[0:1, :] += jnp.sum(dhnw * hn, axis=0, keepdims=True)
        dhn = dhnw * wm
        dh_ref[...] = _bf(r * (dhn - hn * _group_mean(dhn * hn, ML_DV)))
        ysv, zs, d_ys = ys_ref[...], zs_ref[...], dys_ref[...].astype(F32)
        szs = _silu(zs)
        pn, r2 = _group_norm(ysv * szs, GROUP_W)
        acc_ref[1:2, :] += jnp.sum(d_ys * pn, axis=0, keepdims=True)
        dpn = d_ys * ws_ref[...]
        dp = r2 * (dpn - pn * _group_mean(dpn * pn, GROUP_W))
        dyssd_ref[...] = _bf(dp * szs)
        dp_ref[:, O_ZS:O_ZS + 2048] = _bf(dp * ysv * _dsilu(zs))

    tile = pl.BlockSpec((ts, 2048), lambda i: (i, 0))
    col = lambda off: pl.BlockSpec((ts, 2048), lambda i: (i, off // 2048))
    row = pl.BlockSpec((1, 2048), lambda i: (0, 0))
    sds = lambda dt: jax.ShapeDtypeStruct((s, 2048), dt)
    return pl.pallas_call(
        body, name="post_bwd", grid=(s // ts,),
        in_specs=[tile, tile, tile, tile, col(O_O), col(O_ZM), col(O_ZS), row, row, pl.BlockSpec(memory_space=pl.ANY)],
        out_specs=[tile, tile, pl.BlockSpec((ts, O_MG), lambda i: (i, 0)), pl.BlockSpec((8, 2048), lambda i: (0, 0))],
        out_shape=[sds(BF16), sds(BF16), jax.ShapeDtypeStruct(dproj.shape, dproj.dtype), jax.ShapeDtypeStruct((8, 2048), F32)],
        input_output_aliases={9: 2},
        compiler_params=_cparams("arbitrary"))(dym, dys, hm, yssd, proj, proj, proj, ml_norm_w, ssm_norm_w, dproj)


def _merge(x, ym, ys, proj, target, gate, final_w, wpm, wps, wo, ts):
    wpm_t, wps_t, wo_t = wpm.T, wps.T, wo.T
    s, d = x.shape

    def body(x_ref, ym_ref, ys_ref, mg_ref, t_ref, gate_ref, fw_ref, wpm_ref, wps_ref, wo_ref, wpmt_ref, wpst_ref, wot_ref,
             dres_ref, mer_ref, dmo_ref, dpm_ref, dps_ref, dym_ref, dys_ref, dmg_ref, acc_ref):
        @pl.when(pl.program_id(0) == 0)
        def _():
            acc_ref[...] = jnp.zeros_like(acc_ref)
        gm, gs = _sigmoid(mg_ref[:, 0:d]), _sigmoid(mg_ref[:, d:2 * d])
        pm = _nn(ym_ref[...], wpm_ref[...])
        ps = _nn(ys_ref[...], wps_ref[...])
        merged = _bf(gm * pm + gs * ps)
        mer_ref[...] = merged
        mo = _nn(merged, wo_ref[...])
        gate, fw = gate_ref[...], fw_ref[...]
        out = x_ref[...] + gate * mo
        r = lax.rsqrt(jnp.mean(out * out, axis=1, keepdims=True) + EPS)
        on = out * r
        diff = on * fw - t_ref[...]
        acc_ref[0:1, :] += jnp.sum(0.5 * jnp.sum(diff * diff, axis=1, keepdims=True) / d, axis=0, keepdims=True)
        dyv = diff * (1.0 / d)
        acc_ref[1:2, :] += jnp.sum(dyv * on, axis=0, keepdims=True)
        don = dyv * fw
        dout = r * (don - on * jnp.mean(don * on, axis=1, keepdims=True))
        dres_ref[...] = dout
        acc_ref[2:3, :] += jnp.sum(dout * mo, axis=0, keepdims=True)
        dmo = _bf(dout * gate)
        dmo_ref[...] = dmo
        dmer = _nn(dmo, wot_ref[...])
        dpm, dps = _bf(dmer * gm), _bf(dmer * gs)
        dpm_ref[...] = dpm
        dps_ref[...] = dps
        dmg_ref[:, 0:d] = _bf(dmer * pm * gm * (1.0 - gm))
        dmg_ref[:, d:2 * d] = _bf(dmer * ps * gs * (1.0 - gs))
        dym_ref[...] = _bf(_nn(dpm, wpmt_ref[...]))
        dys_ref[...] = _bf(_nn(dps, wpst_ref[...]))

    t1 = pl.BlockSpec((ts, d), lambda i: (i, 0))
    t2 = pl.BlockSpec((ts, 2 * d), lambda i: (i, 0))
    row = pl.BlockSpec((1, d), lambda i: (0, 0))
    whole = pl.BlockSpec(memory_space=pltpu.VMEM)
    sd = lambda w, dt: jax.ShapeDtypeStruct((s, w), dt)
    return pl.pallas_call(
        body, name="merge_fwd_bwd", grid=(s // ts,),
        in_specs=[t1, t2, t2, pl.BlockSpec((ts, 2 * d), lambda i: (i, O_MG // (2 * d))), t1, row, row] + [whole] * 6,
        out_specs=[t1, t1, t1, t1, t1, t2, t2, pl.BlockSpec((ts, 2 * d), lambda i: (i, O_MG // (2 * d))),
                   pl.BlockSpec((8, d), lambda i: (0, 0))],
        out_shape=[sd(d, F32), sd(d, BF16), sd(d, BF16), sd(d, BF16), sd(d, BF16), sd(2 * d, BF16), sd(2 * d, BF16),
                   sd(NP, BF16), jax.ShapeDtypeStruct((8, d), F32)],
        compiler_params=_cparams("arbitrary"))(x, ym, ys, proj, target, gate, final_w, wpm, wps, wo, wpm_t, wps_t, wo_t)


def _adamw(w, g, m, v, tr):
    if w.ndim == 3 and w.shape[0] > 1:
        tile, steps = pl.BlockSpec((tr,) + w.shape[1:], lambda i: (i, 0, 0)), w.shape[0] // tr
    else:
        lead = (None,) * (w.ndim - 2)
        tile, steps = pl.BlockSpec(lead + (tr, w.shape[-1]), lambda i: (0,) * len(lead) + (i, 0)), w.shape[-2] // tr

    def body(w_ref, g_ref, m_ref, v_ref, d_ref, nm_ref, nv_ref):
        gv = g_ref[...]
        m2 = ADAM_B1 * m_ref[...] + (1.0 - ADAM_B1) * gv
        v2 = ADAM_B2 * v_ref[...] + (1.0 - ADAM_B2) * (gv * gv)
        m_hat = m2 / (1.0 - ADAM_B1 ** ADAM_STEP)
        v_hat = v2 / (1.0 - ADAM_B2 ** ADAM_STEP)
        d_ref[...] = -ADAM_LR * (m_hat / (jnp.sqrt(v_hat) + ADAM_EPS) + ADAM_WD * w_ref[...])
        nm_ref[...] = m2
        nv_ref[...] = v2

    return pl.pallas_call(
        body, name="adamw", grid=(steps,), in_specs=[tile] * 4, out_specs=[tile] * 3,
        out_shape=[jax.ShapeDtypeStruct(w.shape, F32)] * 3,
        compiler_params=_cparams("parallel"))(w, g.reshape(w.shape), m, v)


def _sum_parts(own, parts, tr, dtype=F32):
    p, rows, cols = parts.shape

    def body(*refs):
        p_ref, o_ref = refs[-2], refs[-1]
        acc = p_ref[0].astype(F32) if own is None else refs[0][...].astype(F32) + p_ref[0].astype(F32)
        for i in range(1, p):
            acc = acc + p_ref[i].astype(F32)
        o_ref[...] = acc.astype(dtype)

    tile = pl.BlockSpec((tr, cols), lambda i: (i, 0))
    ins = ([] if own is None else [tile]) + [pl.BlockSpec((p, tr, cols), lambda i: (0, i, 0))]
    args = ([] if own is None else [own]) + [parts]
    return pl.pallas_call(
        body, name="sum_parts", grid=(rows // tr,), in_specs=ins, out_specs=tile,
        out_shape=jax.ShapeDtypeStruct((rows, cols), dtype), compiler_params=_cparams("parallel"))(*args)


def _position():
    return lax.axis_index("x"), lax.axis_index("y"), lax.axis_index("c")


def _flip(pos, k):
    return tuple(1 - p if (k >> s) & 1 else p for p, s in zip(pos, (2, 1, 0)))


def _allgather8(block):
    rows, cols = block.shape

    def body(x_ref, o_ref, send_sems, recv_sems, local_sem):
        pos = _position()
        me = 4 * pos[0] + 2 * pos[1] + pos[2]
        mine = pltpu.make_async_copy(x_ref, o_ref.at[me], local_sem)
        mine.start()
        copies = [pltpu.make_async_remote_copy(src_ref=x_ref, dst_ref=o_ref.at[me], send_sem=send_sems.at[k - 1],
                                               recv_sem=recv_sems.at[k - 1], device_id=_flip(pos, k), device_id_type=MESH)
                  for k in range(1, N_DEV)]
        for cp in copies:
            cp.start()
        for cp in copies:
            cp.wait()
        mine.wait()

    vmem = pl.BlockSpec(memory_space=pltpu.VMEM)
    return pl.pallas_call(
        body, name="allgather8", in_specs=[vmem], out_specs=vmem,
        out_shape=jax.ShapeDtypeStruct((N_DEV, rows, cols), block.dtype),
        scratch_shapes=[pltpu.SemaphoreType.DMA((N_DEV - 1,)), pltpu.SemaphoreType.DMA((N_DEV - 1,)),
                        pltpu.SemaphoreType.DMA],
        compiler_params=pltpu.CompilerParams(vmem_limit_bytes=VMEM_LIMIT))(block)


COPY_BYTES = 1 << 20


def _row_chunks(rows, row_bytes):
    n = max(1, min(rows // 16, -(-rows * row_bytes // COPY_BYTES)))
    while rows % (16 * n):
        n -= 1
    return [(i * (rows // n), rows // n) for i in range(n)]


def _weight_gather(shards):
    n = len(shards)
    pieces = [_row_chunks(a.shape[1], a.shape[2] * a.dtype.itemsize) for a in shards]
    plan = [(a, k, r0, nr) for a in range(n) for k in range(1, N_CHIPS) for r0, nr in pieces[a]]

    def body(*refs):
        ins, outs = refs[:n], refs[n:2 * n]
        ici_send, ici_recv, d2d_send, d2d_recv = refs[2 * n:]
        pos = _position()
        chip, core = 2 * pos[0] + pos[1], pos[2]
        sibling = _flip(pos, 1)
        sent = []
        for i, (a, k, r0, nr) in enumerate(plan):
            cp = pltpu.make_async_remote_copy(
                src_ref=ins[a].at[core, pl.ds(r0, nr)], dst_ref=outs[a].at[chip, core, pl.ds(r0, nr)],
                send_sem=ici_send.at[i], recv_sem=ici_recv.at[i], device_id=_flip(pos, 2 * k), device_id_type=MESH)
            cp.start()
            sent.append(cp)
        passed = []
        for i, (a, k, r0, nr) in enumerate(plan):
            there = _flip(pos, 2 * k)
            landed = outs[a].at[2 * there[0] + there[1], core, pl.ds(r0, nr)]
            sent[i].wait_recv()
            cp = pltpu.make_async_remote_copy(src_ref=landed, dst_ref=landed, send_sem=d2d_send.at[i],
                                              recv_sem=d2d_recv.at[i], device_id=sibling, device_id_type=MESH)
            cp.start()
            passed.append(cp)
        for cp in passed:
            cp.wait()
        for cp in sent:
            cp.wait_send()

    hbm = pl.BlockSpec(memory_space=pl.ANY)
    sems = pltpu.SemaphoreType.DMA((len(plan),))
    return pl.pallas_call(
        body, name="weight_gather", in_specs=[hbm] * n, out_specs=[hbm] * n,
        out_shape=[jax.ShapeDtypeStruct((N_CHIPS,) + a.shape, a.dtype) for a in shards],
        scratch_shapes=[sems, sems, sems, sems],
        compiler_params=pltpu.CompilerParams(has_side_effects=True))(*shards)


def _exchange(name, arrays, out_shapes, plan, n_remote, n_local):
    n, m = len(arrays), len(out_shapes)

    def body(*refs):
        send_sems, recv_sems, local_sems = refs[n + m:]
        remote, local = plan(_position(), refs[:n], refs[n:n + m])
        assert (len(remote), len(local)) == (n_remote, n_local)
        copies = [pltpu.make_async_copy(src, dst, local_sems.at[i]) for i, (src, dst) in enumerate(local)]
        copies += [pltpu.make_async_remote_copy(src_ref=src, dst_ref=dst, send_sem=send_sems.at[i], recv_sem=recv_sems.at[i],
                                                device_id=dev, device_id_type=MESH)
                   for i, (src, dst, dev) in enumerate(remote)]
        for cp in copies:
            cp.start()
        for cp in copies:
            cp.wait()

    hbm = pl.BlockSpec(memory_space=pl.ANY)
    return pl.pallas_call(
        body, name=name, in_specs=[hbm] * n, out_specs=[hbm] * m, out_shape=out_shapes,
        scratch_shapes=[pltpu.SemaphoreType.DMA((n_remote,)), pltpu.SemaphoreType.DMA((n_remote,)),
                        pltpu.SemaphoreType.DMA((max(n_local, 1),))],
        compiler_params=pltpu.CompilerParams(has_side_effects=True))(*arrays)


def _pair_send(slabs):
    n = len(slabs)
    pieces = [_row_chunks(g.shape[2], g.shape[3] * g.dtype.itemsize) for g in slabs]

    def plan(pos, ins, outs):
        return [(ins[a].at[j, 1 - pos[2], pl.ds(r0, nr)], outs[a].at[j, pl.ds(r0, nr)], _flip(pos, 1))
                for a in range(n) for j in range(N_CHIPS) for r0, nr in pieces[a]], []

    return _exchange("pair_send", slabs, [jax.ShapeDtypeStruct((N_CHIPS,) + g.shape[2:], g.dtype) for g in slabs], plan,
                     N_CHIPS * sum(len(p) for p in pieces), 0)


def _chip_scatter_copies(pos, sums, lands, send_sems, recv_sems):
    copies = []
    for a in range(len(sums)):
        for k in range(1, N_CHIPS):
            to = _flip(pos, 2 * k)
            for r0, nr in _row_chunks(sums[a].shape[1], sums[a].shape[2] * sums[a].dtype.itemsize):
                i = len(copies)
                copies.append(pltpu.make_async_remote_copy(
                    src_ref=sums[a].at[2 * to[0] + to[1], pl.ds(r0, nr)], dst_ref=lands[a].at[k - 1, pl.ds(r0, nr)],
                    send_sem=send_sems.at[i], recv_sem=recv_sems.at[i], device_id=to, device_id_type=MESH))
    return copies


def _chip_scatter_start(sums):
    n = len(sums)
    n_copies = (N_CHIPS - 1) * sum(len(_row_chunks(g.shape[1], g.shape[2] * g.dtype.itemsize)) for g in sums)
    lands = [lax.empty((N_CHIPS - 1,) + g.shape[1:], g.dtype) for g in sums]

    def body(*refs):
        send_sems, recv_sems = refs[2 * n], refs[2 * n + 1]
        for cp in _chip_scatter_copies(_position(), refs[:n], refs[n:2 * n], send_sems, recv_sems):
            cp.start()
        refs[-1][...] = jnp.zeros((8, LANE), F32)

    hbm = pl.BlockSpec(memory_space=pltpu.HBM)
    sem = pl.BlockSpec(memory_space=pltpu.SEMAPHORE)
    operands = [pltpu.with_memory_space_constraint(t, pltpu.HBM) for t in list(sums) + lands]
    out = pl.pallas_call(
        body, name="chip_scatter_start", in_specs=[hbm] * (2 * n),
        out_specs=[sem, sem] + [hbm] * (2 * n) + [pl.BlockSpec(memory_space=pltpu.VMEM)],
        out_shape=[pltpu.SemaphoreType.DMA((n_copies,)), pltpu.SemaphoreType.DMA((n_copies,))]
        + [pltpu.HBM(t.shape, t.dtype) for t in operands] + [jax.ShapeDtypeStruct((8, LANE), F32)],
        input_output_aliases={i: 2 + i for i in range(2 * n)},
        compiler_params=pltpu.CompilerParams(has_side_effects=pltpu.SideEffectType.DATAFLOW_SIDE_EFFECTING))(*operands)
    return out[0], out[1], out[2:2 + n], out[2 + n:2 + 2 * n], out[-1]


def _chip_scatter_wait(send_sems, recv_sems, sums, lands, after):
    n = len(sums)

    def body(*refs):
        for cp in _chip_scatter_copies(_position(), refs[:n], refs[n:2 * n], refs[2 * n], refs[2 * n + 1]):
            cp.wait_send()
            cp.wait_recv()

    hbm = pl.BlockSpec(memory_space=pltpu.HBM)
    sem = pl.BlockSpec(memory_space=pltpu.SEMAPHORE)
    out = pl.pallas_call(
        body, name="chip_scatter_wait", in_specs=[hbm] * (2 * n) + [sem, sem, pl.BlockSpec(memory_space=pl.ANY)],
        out_specs=[hbm] * (2 * n), out_shape=[pltpu.HBM(t.shape, t.dtype) for t in list(sums) + list(lands)],
        input_output_aliases={i: i for i in range(2 * n)},
        compiler_params=pltpu.CompilerParams(has_side_effects=pltpu.SideEffectType.DATAFLOW_SIDE_EFFECTING))(
            *sums, *lands, send_sems, recv_sems, after)
    return out[:n], out[n:]


def _pair_exchange(halves):
    n = len(halves)
    pieces = [_row_chunks(h.shape[0], h.shape[1] * h.dtype.itemsize) for h in halves]

    def plan(pos, ins, outs):
        return [(ins[a].at[pl.ds(r0, nr)], outs[a].at[pl.ds(r0, nr)], _flip(pos, 1))
                for a in range(n) for r0, nr in pieces[a]], []

    return _exchange("pair_exchange", halves, [jax.ShapeDtypeStruct(h.shape, h.dtype) for h in halves], plan,
                     sum(len(p) for p in pieces), 0)


def _pack(arrays):
    flat = jnp.concatenate([a.reshape(-1).astype(F32) for a in arrays])
    size = -(-flat.shape[0] // (8 * LANE)) * (8 * LANE)
    return jnp.pad(flat, (0, size - flat.shape[0])).reshape(size // LANE, LANE)


def _unpack(buf, shapes):
    flat = buf.reshape(-1)
    out, off = [], 0
    for shp in shapes:
        n = math.prod(shp)
        out.append(flat[off:off + n].reshape(shp))
        off += n
    return out


def _unpack_rows(bufs, shapes):
    flat = bufs.reshape(bufs.shape[0], -1)
    out, off = [], 0
    for shp in shapes:
        n = math.prod(shp)
        out.append(flat[:, off:off + n].reshape((bufs.shape[0],) + shp))
        off += n
    return out


def _taps8(w):
    return jnp.pad(w, ((0, 8 - CONV_K), (0, 0)))


def _local_step(xs, tgt, scale, shift, gate, norm_w, w_in_p, b_in_p, ml_conv_w, ml_conv_b, ml_norm_w, ssm_conv_w,
                ssm_conv_b, ssm_a_log, ssm_d, ssm_norm_w, wpm, wps, wo, final_w, start_exchange=None):
    s = xs.shape[0]
    ts = min(512, s)
    tm = min(2048, s)
    u = _prenorm_fwd(xs, norm_w, scale, shift, ts)
    proj = _matmul_bias(u, w_in_p, b_in_p, tm, 512)
    mlw8, ssw8 = _taps8(ml_conv_w), _taps8(ssm_conv_w)
    qk, qk_dact = _conv_fwd(proj, O_QK, 2048, mlw8, ml_conv_b, ts)
    xbc, xbc_dact = _conv_fwd(proj, O_XBC, 3072, ssw8, ssm_conv_b, ts)
    gt = proj[:, O_IF:O_IF + LANE].T
    dtt = proj[:, O_DT:O_DT + LANE].T
    hm, cst, nm = _mlstm_fwd(qk, proj, gt)
    alog_row = jnp.pad(ssm_a_log, ((0, 0), (0, LANE - SSM_HEADS)))
    alog_col = alog_row.reshape(LANE, 1)
    dskip_x = jnp.repeat(ssm_d[0], SSM_HEADDIM)[None]
    expand = _head_expand()
    yssd, sst = _ssd_fwd(xbc, proj, dtt, alog_row, alog_col, dskip_x, expand)
    tp = min(128, s)
    ym, ys = _post_fwd(hm, yssd, proj, ml_norm_w, ssm_norm_w, tp)
    dxres, merged, dmo, dpm, dps, dym, dys, dproj, acc_m = _merge(xs, ym, ys, proj, tgt, gate, final_w, wpm, wps, wo, tp)
    dh, dyssd, dproj, acc_p = _post_bwd(dym, dys, hm, yssd, proj, ml_norm_w, ssm_norm_w, dproj, tp)
    dqk, dproj, dif = _mlstm_bwd(qk, proj, gt, hm, dh, cst, nm, dproj)
    dxbc, ddt, accd, acca = _ssd_bwd(xbc, proj, dtt, alog_row, alog_col, dskip_x, expand, expand.T, dyssd, sst)
    dproj, acc_cq = _conv_bwd(proj, O_QK, 2048, mlw8, qk_dact, dqk, dproj, ts)
    dproj, acc_cx = _conv_bwd(proj, O_XBC, 3072, ssw8, xbc_dact, dxbc, dproj, ts)
    dproj = dproj.at[:, O_IF:O_IF + SMALL_W].set(dif).at[:, O_DT:O_DT + SMALL_W].set(ddt)
    gw_in_p, gb_in_p = _matmul_tn(u.T, dproj, tm, 512, with_colsum=True, a_is_transposed=True)
    g_wpm = _matmul_tn(ym, dpm, tm, 512)
    g_wps = _matmul_tn(ys, dps, tm, 512)
    g_wo = _matmul_tn(merged, dmo, tm, 512)
    token, in_flight = (None, None) if start_exchange is None else start_exchange(gw_in_p, g_wpm, g_wps, g_wo)
    du = _matmul_nt(dproj, w_in_p, tm, 512, after=token)
    grad_x, acc_n = _prenorm_bwd(du, xs, dxres, norm_w, scale, ts)
    a_coef = -jnp.exp(ssm_a_log[0])
    small = dict(
        mod=jnp.concatenate([acc_n[2], acc_n[1], acc_m[2]]), norm_w=acc_n[0], b_in=_unpad_cols(gb_in_p[0]),
        ml_conv_w=acc_cq[0:CONV_K], ml_conv_b=acc_cq[CONV_K], ml_norm_w=acc_p[0], ssm_conv_w=acc_cx[0:CONV_K],
        ssm_conv_b=acc_cx[CONV_K], ssm_a_log=acca[0, :SSM_HEADS] * a_coef,
        ssm_d=accd[0].reshape(SSM_HEADS, SSM_HEADDIM).sum(axis=1), ssm_norm_w=acc_p[1], final_w=acc_m[1], loss=acc_m[0, 0:1])
    return grad_x, small, gw_in_p, g_wpm, g_wps, g_wo, in_flight


WEIGHTS = ("norm_w", "ada_w", "ada_b", "w_in", "b_in", "ml_conv_w", "ml_conv_b", "ml_norm_w", "ssm_conv_w", "ssm_conv_b",
           "ssm_a_log", "ssm_d", "ssm_norm_w", "w_proj_m", "w_proj_s", "w_out", "final_w")
LARGE = ("ada_w", "w_in", "w_proj_m", "w_proj_s", "w_out")
SMALL_SUMS = (("mod", (3 * D_MODEL,)), ("norm_w", (D_MODEL,)), ("b_in", (IN_WIDTH,)), ("ml_conv_w", (CONV_K, 2048)),
              ("ml_conv_b", (2048,)), ("ml_norm_w", (2048,)), ("ssm_conv_w", (CONV_K, 3072)), ("ssm_conv_b", (3072,)),
              ("ssm_a_log", (SSM_HEADS,)), ("ssm_d", (SSM_HEADS,)), ("ssm_norm_w", (2048,)), ("final_w", (D_MODEL,)),
              ("loss", (1,)))


def kernel(x, c, norm_w, ada_w, ada_b, w_in, b_in, ml_conv_w, ml_conv_b, ml_norm_w, ssm_conv_w, ssm_conv_b, ssm_a_log, ssm_d, ssm_norm_w, w_proj_m, w_proj_s, w_out, final_w, loss_target, m_norm_w, m_ada_w, m_ada_b, m_w_in, m_b_in, m_ml_conv_w, m_ml_conv_b, m_ml_norm_w, m_ssm_conv_w, m_ssm_conv_b, m_ssm_a_log, m_ssm_d, m_ssm_norm_w, m_w_proj_m, m_w_proj_s, m_w_out, m_final_w, v_norm_w, v_ada_w, v_ada_b, v_w_in, v_b_in, v_ml_conv_w, v_ml_conv_b, v_ml_norm_w, v_ssm_conv_w, v_ssm_conv_b, v_ssm_a_log, v_ssm_d, v_ssm_norm_w, v_w_proj_m, v_w_proj_s, v_w_out, v_final_w):
    w = dict(norm_w=norm_w, ada_w=ada_w, ada_b=ada_b, w_in=w_in, b_in=b_in, ml_conv_w=ml_conv_w, ml_conv_b=ml_conv_b,
             ml_norm_w=ml_norm_w, ssm_conv_w=ssm_conv_w, ssm_conv_b=ssm_conv_b, ssm_a_log=ssm_a_log, ssm_d=ssm_d,
             ssm_norm_w=ssm_norm_w, w_proj_m=w_proj_m, w_proj_s=w_proj_s, w_out=w_out, final_w=final_w)
    m = dict(zip(WEIGHTS, (m_norm_w, m_ada_w, m_ada_b, m_w_in, m_b_in, m_ml_conv_w, m_ml_conv_b, m_ml_norm_w, m_ssm_conv_w,
                           m_ssm_conv_b, m_ssm_a_log, m_ssm_d, m_ssm_norm_w, m_w_proj_m, m_w_proj_s, m_w_out, m_final_w)))
    v = dict(zip(WEIGHTS, (v_norm_w, v_ada_w, v_ada_b, v_w_in, v_b_in, v_ml_conv_w, v_ml_conv_b, v_ml_norm_w, v_ssm_conv_w,
                           v_ssm_conv_b, v_ssm_a_log, v_ssm_d, v_ssm_norm_w, v_w_proj_m, v_w_proj_s, v_w_out, v_final_w)))
    pos = _position()
    chip = 2 * pos[0] + pos[1]
    dev = 2 * chip + pos[2]
    mlw_cols, ssw_cols, ada_cols = ml_conv_w.shape[2], ssm_conv_w.shape[2], ada_w.shape[2]

    g0 = _allgather8(_pack([c, ml_conv_w, ssm_conv_w]))
    c_all, mlw_all, ssw_all = _unpack_rows(g0, [(D_MODEL,), (CONV_K, mlw_cols), (CONV_K, ssw_cols)])
    ml_conv_full = mlw_all[0::2].transpose(1, 0, 2).reshape(CONV_K, N_CHIPS * mlw_cols)
    ssm_conv_full = ssw_all[0::2].transpose(1, 0, 2).reshape(CONV_K, N_CHIPS * ssw_cols)

    ada_b_mine = lax.dynamic_slice_in_dim(ada_b, chip * ada_cols, ada_cols, axis=1)
    g1 = _allgather8(_ada_fwd(c_all, ada_w[0], ada_b_mine))
    mod = lax.dynamic_index_in_dim(g1[0::2], dev, axis=1, keepdims=False).reshape(1, 3 * D_MODEL)
    shift, scale, gate = mod[:, :D_MODEL], mod[:, D_MODEL:2 * D_MODEL], mod[:, 2 * D_MODEL:]

    mine = [_bf(a[0]).reshape(2, a.shape[1] // 2, a.shape[2]) for a in (w_in, w_proj_m, w_proj_s, w_out)]
    gw = [lax.dynamic_update_index_in_dim(got, own, chip, 0).reshape(N_CHIPS, -1, own.shape[-1])
          for got, own in zip(_weight_gather(mine), mine)]
    w_in_p = _shards_to_padded(gw[0])
    wpm, wps, wo = (a.reshape(-1, D_MODEL) for a in gw[1:])

    def start_exchange(g_w_in, g_wpm, g_wps, g_wo):
        split = lambda g, rows: _bf(g).reshape(N_CHIPS, 2, rows // (2 * N_CHIPS), g.shape[-1])
        slabs = [split(_padded_to_shards(_bf(g_w_in)), N_CHIPS * D_MODEL),
                 split(g_wpm, g_wpm.shape[0]), split(g_wps, g_wps.shape[0]), split(g_wo, g_wo.shape[0])]
        pair_sums = []
        for slab, rec in zip(slabs, _pair_send(slabs)):
            kept = lax.dynamic_index_in_dim(slab, pos[2], 1, keepdims=False)
            rows = kept.shape[0] * kept.shape[1]
            both = _sum_parts(kept.reshape(rows, -1), rec.reshape(1, rows, -1), 32, BF16)
            pair_sums.append(both.reshape(kept.shape))
        send_sems, recv_sems, sums, lands, token = _chip_scatter_start(pair_sums)
        return token, (send_sems, recv_sems, sums, lands)

    grad_x, small, _, _, _, _, in_flight = _local_step(
        x[0], loss_target[0], scale, shift, gate, norm_w, w_in_p, _pad_cols(b_in), ml_conv_full, ml_conv_b, ml_norm_w,
        ssm_conv_full, ssm_conv_b, ssm_a_log, ssm_d, ssm_norm_w, wpm, wps, wo, final_w[None], start_exchange)

    g2 = _allgather8(_pack([small[name] for name, _ in SMALL_SUMS]))
    total = dict(zip([name for name, _ in SMALL_SUMS], _unpack(_sum_parts(None, g2, g2.shape[1]), [s for _, s in SMALL_SUMS])))
    dmod_all = g2[:, :3 * D_MODEL // LANE].reshape(N_DEV, 3 * D_MODEL)
    grads = dict(total)
    grads["ada_b"] = total["mod"]
    grads["ml_conv_w"] = lax.dynamic_slice_in_dim(total["ml_conv_w"], chip * mlw_cols, mlw_cols, axis=1)
    grads["ssm_conv_w"] = lax.dynamic_slice_in_dim(total["ssm_conv_w"], chip * ssw_cols, ssw_cols, axis=1)
    grads["ada_w"] = _ada_bwd(c_all, lax.dynamic_slice_in_dim(dmod_all, chip * ada_cols, ada_cols, axis=1))

    halves = []
    for both, rec in zip(*_chip_scatter_wait(*in_flight, grad_x)):
        halves.append(_sum_parts(lax.dynamic_index_in_dim(both, chip, 0, keepdims=False), rec, 32))
    for name, half, other in zip(("w_in", "w_proj_m", "w_proj_s", "w_out"), halves, _pair_exchange(halves)):
        grads[name] = jnp.where(pos[2] == 0, jnp.concatenate([half, other]), jnp.concatenate([other, half]))

    delta, new_m, new_v = {}, {}, {}
    for name in LARGE:
        if w[name].shape[-1] % LANE:
            flat = lambda a: a.reshape(a.shape[-2:]).T.reshape(-1, 8, LANE)
            back = lambda a: a.reshape(w[name].shape[-1], w[name].shape[-2]).T.reshape(w[name].shape)
            tr = max(t for t in range(1, 512) if w[name].shape[-1] % t == 0)
            g_flat = flat(grads[name])
            delta[name], new_m[name], new_v[name] = (back(a) for a in _adamw(flat(w[name]), g_flat, flat(m[name]), flat(v[name]), tr))
            grads[name] = back(g_flat)
        else:
            delta[name], new_m[name], new_v[name] = _adamw(w[name], grads[name], m[name], v[name], 64)
    rest = [name for name in WEIGHTS if name not in LARGE]
    packed = [_pack([t[name] for name in rest]) for t in (w, grads, m, v)]
    for out, buf in zip((delta, new_m, new_v), _adamw(*packed, packed[0].shape[0])):
        out.update(zip(rest, _unpack(buf, [w[name].shape for name in rest])))
    loss = total["loss"][0]
    return (loss, grad_x[None], *[grads[name].reshape(w[name].shape) for name in WEIGHTS], *[delta[name] for name in WEIGHTS],
            *[new_m[name] for name in WEIGHTS], *[new_v[name] for name in WEIGHTS])
```

```python
import functools
import math

import jax
import jax.numpy as jnp
from jax import lax
from jax.experimental import pallas as pl
from jax.experimental.pallas import tpu as pltpu

F32 = jnp.float32
BF16 = jnp.bfloat16
HI = lax.Precision.HIGHEST
MESH = pl.DeviceIdType.MESH

D_MODEL = 1024
EPS = 1e-6
CONV_K = 4
ML_HEADS = 8
ML_DQK = 128
ML_DV = 256
SSM_HEADS = 32
SSM_HEADDIM = 64
SSM_GROUPS = 4
SSM_STATE = 128
IN_WIDTH = 15408
N_CHIPS = 4
N_DEV = 8
ADAM_LR, ADAM_B1, ADAM_B2, ADAM_EPS, ADAM_WD, ADAM_STEP = 0.001, 0.9, 0.999, 1e-08, 0.01, 10

O_O, O_ZM, O_ZS, O_MG, O_QK, O_V, O_XBC, O_IF, O_DT = 0, 2048, 4096, 6144, 8192, 10240, 12288, 15360, 15616
SMALL_W = 256
NP = 15872
LANE = 128
CHUNK = 128
NEG = -1e30
VMEM_LIMIT = 48 * 1024 * 1024


def _cparams(*sem):
    return pltpu.CompilerParams(dimension_semantics=sem, vmem_limit_bytes=VMEM_LIMIT)


def _pad_cols(w):
    z = lambda n: jnp.zeros(w.shape[:-1] + (n,), w.dtype)
    return jnp.concatenate([w[..., 4096:8192], w[..., 11280:13328], w[..., 13360:15408], w[..., :4096], w[..., 8208:11280],
                            w[..., 8192:8208], z(SMALL_W - 16), w[..., 13328:13360], z(SMALL_W - 32)], axis=-1)


def _unpad_cols(g):
    return jnp.concatenate([g[..., O_QK:O_QK + 4096], g[..., O_O:O_O + 4096], g[..., O_IF:O_IF + 16],
                            g[..., O_XBC:O_XBC + 3072], g[..., O_ZS:O_ZS + 2048], g[..., O_DT:O_DT + 32],
                            g[..., O_MG:O_MG + 2048]], axis=-1)


PADDED_SEGMENTS = ((4096, 8192, 0), (11280, 13328, 0), (13360, 15408, 0), (0, 4096, 0), (8208, 11280, 0),
                   (8192, 8208, SMALL_W - 16), (13328, 13360, SMALL_W - 32))
SHARD_W = IN_WIDTH // N_CHIPS


def _shards_to_padded(shards):
    parts = []
    for first, last, pad in PADDED_SEGMENTS:
        for j in range(N_CHIPS):
            lo, hi = max(first, j * SHARD_W), min(last, (j + 1) * SHARD_W)
            if lo < hi:
                parts.append(shards[j][:, lo - j * SHARD_W:hi - j * SHARD_W])
        if pad:
            parts.append(jnp.zeros((shards.shape[1], pad), shards.dtype))
    return jnp.concatenate(parts, axis=1)


def _padded_to_shards(g):
    offsets, off = {}, 0
    for first, last, pad in PADDED_SEGMENTS:
        offsets[first] = off
        off += last - first + pad
    shards = []
    for j in range(N_CHIPS):
        parts = []
        for first, last, _ in sorted(PADDED_SEGMENTS):
            lo, hi = max(first, j * SHARD_W), min(last, (j + 1) * SHARD_W)
            if lo < hi:
                parts.append(g[:, offsets[first] + lo - first:offsets[first] + hi - first])
        shards.append(jnp.concatenate(parts, axis=1))
    return jnp.stack(shards)


def _sigmoid(x):
    return 1.0 / (1.0 + jnp.exp(-x))


def _silu(x):
    return x * _sigmoid(x)


def _dsilu(x):
    s = _sigmoid(x)
    return s + x * s * (1.0 - s)


def _softplus(x):
    return jnp.maximum(x, 0.0) + jnp.log(1.0 + jnp.exp(-jnp.abs(x)))


def _logsigmoid(x):
    return jnp.minimum(x, 0.0) - jnp.log(1.0 + jnp.exp(-jnp.abs(x)))


def _dot(a, b, dims, precision=None):
    return lax.dot_general(a, b, (dims, ((), ())), preferred_element_type=F32, precision=precision)


def _nn(a, b, precision=None):
    return _dot(a, b, ((1,), (0,)), precision)


def _nt(a, b, precision=None):
    return _dot(a, b, ((1,), (1,)), precision)


def _tn(a, b, precision=None):
    return _dot(a, b, ((0,), (0,)), precision)


def _bf(x):
    return x.astype(BF16)


def _split(x, terms):
    parts = []
    for _ in range(terms):
        part = _bf(x)
        parts.append(part)
        x = x - part.astype(F32)
    return parts


def _pick_right(x, pick, terms):
    pick = _bf(pick)
    out = None
    for part in _split(x, terms):
        out = _nn(part, pick) if out is None else out + _nn(part, pick)
    return out


def _pick_left(pick, x, terms):
    pick = _bf(pick)
    out = None
    for part in _split(x, terms):
        out = _nn(pick, part) if out is None else out + _nn(pick, part)
    return out


def _lane_col(x, lane):
    idx = lax.broadcasted_iota(jnp.int32, x.shape, 1)
    return jnp.sum(jnp.where(idx == lane, x, 0.0), axis=1, keepdims=True)


def _tri(n, upper):
    r = lax.broadcasted_iota(jnp.int32, (n, n), 0)
    c = lax.broadcasted_iota(jnp.int32, (n, n), 1)
    return jnp.where((r <= c) if upper else (r >= c), 1.0, 0.0).astype(F32)


def _eye(n):
    return jnp.where(lax.broadcasted_iota(jnp.int32, (n, n), 0) == lax.broadcasted_iota(jnp.int32, (n, n), 1), 1.0, 0.0)


def _sum_all(x):
    return jnp.sum(jnp.sum(x, axis=1, keepdims=True), axis=0, keepdims=True)


def _crossing(p):
    L = p.shape[0]
    hi = _bf(p)
    lo = _bf(p - hi.astype(F32))
    upper = _bf(_tri(L, True))
    below = _nn(upper, hi) + _nn(upper, lo)
    strict = lax.broadcasted_iota(jnp.int32, (L, L), 0) > lax.broadcasted_iota(jnp.int32, (L, L), 1)
    return jnp.sum(jnp.where(strict, below, 0.0), axis=1, keepdims=True)


def _matmul_bias(a, w, bias, tm, tn, col0, ncols, dtype):
    m, k = a.shape
    j0 = col0 // tn

    def body(a_ref, w_ref, b_ref, o_ref):
        o_ref[...] = (_nn(a_ref[...], w_ref[...]) + b_ref[...]).astype(dtype)

    return pl.pallas_call(
        body, name="matmul_bias", grid=(m // tm, ncols // tn),
        in_specs=[pl.BlockSpec((tm, k), lambda i, j: (i, 0)), pl.BlockSpec((k, tn), lambda i, j: (0, j0 + j)),
                  pl.BlockSpec((1, tn), lambda i, j: (0, j0 + j))],
        out_specs=pl.BlockSpec((tm, tn), lambda i, j: (i, j)),
        out_shape=jax.ShapeDtypeStruct((m, ncols), dtype),
        compiler_params=_cparams("parallel", "arbitrary"))(a, w, bias)


def _matmul_nt(a, w, tm, tk, after=None):
    m, n = a.shape
    k = w.shape[0]

    def body(a_ref, w_ref, *rest):
        o_ref = rest[-1]

        @pl.when(pl.program_id(1) == 0)
        def _():
            o_ref[...] = jnp.zeros_like(o_ref)
        o_ref[...] += _nt(a_ref[...], w_ref[...])

    extra = [] if after is None else [after]
    return pl.pallas_call(
        body, name="matmul_nt", grid=(m // tm, n // tk),
        in_specs=[pl.BlockSpec((tm, tk), lambda i, j: (i, j)), pl.BlockSpec((k, tk), lambda i, j: (0, j))]
        + [pl.BlockSpec(memory_space=pl.ANY)] * len(extra),
        out_specs=pl.BlockSpec((tm, k), lambda i, j: (i, 0)),
        out_shape=jax.ShapeDtypeStruct((m, k), F32),
        compiler_params=_cparams("parallel", "arbitrary"))(a, w, *extra)


def _matmul_tn(a, b, tm, tn, with_colsum=False, a_is_transposed=False):
    k, m = a.shape if a_is_transposed else a.shape[::-1]
    n = b.shape[1]

    def body(a_ref, b_ref, o_ref, *rest):
        first = pl.program_id(1) == 0

        @pl.when(first)
        def _():
            o_ref[...] = jnp.zeros_like(o_ref)
        o_ref[...] += _nn(a_ref[...], b_ref[...]) if a_is_transposed else _tn(a_ref[...], b_ref[...])
        if with_colsum:
            s_ref = rest[0]

            @pl.when(first)
            def _():
                s_ref[...] = jnp.zeros_like(s_ref)
            s_ref[...] += jnp.sum(b_ref[...].astype(F32), axis=0, keepdims=True)

    out_specs = [pl.BlockSpec((k, tn), lambda j, i: (0, j))]
    out_shape = [jax.ShapeDtypeStruct((k, n), F32)]
    if with_colsum:
        out_specs.append(pl.BlockSpec((1, tn), lambda j, i: (0, j)))
        out_shape.append(jax.ShapeDtypeStruct((1, n), F32))
    out = pl.pallas_call(
        body, name="matmul_tn", grid=(n // tn, m // tm),
        in_specs=[pl.BlockSpec((k, tm), lambda j, i: (0, i)) if a_is_transposed else pl.BlockSpec((tm, k), lambda j, i: (i, 0)),
                  pl.BlockSpec((tm, tn), lambda j, i: (i, j))],
        out_specs=out_specs, out_shape=out_shape,
        compiler_params=_cparams("parallel", "arbitrary"))(a, b)
    return out if with_colsum else out[0]


def _ada_fwd(c_all, ada_w, ada_b):
    def body(c_ref, w_ref, b_ref, o_ref):
        o_ref[...] = _nn(_bf(_silu(c_ref[...])), _bf(w_ref[...])) + b_ref[...]

    return pl.pallas_call(body, name="ada_fwd", out_shape=jax.ShapeDtypeStruct((c_all.shape[0], ada_w.shape[1]), F32),
                          compiler_params=_cparams())(c_all, ada_w, ada_b)


def _ada_bwd(c_all, dmod):
    def body(c_ref, d_ref, o_ref):
        o_ref[...] = _tn(_bf(_silu(c_ref[...])), _bf(d_ref[...]))

    return pl.pallas_call(body, name="ada_bwd", out_shape=jax.ShapeDtypeStruct((c_all.shape[1], dmod.shape[1]), F32),
                          compiler_params=_cparams())(c_all, dmod)


def _prenorm_fwd(x, norm_w, scale, shift, ts):
    s, d = x.shape

    def body(x_ref, nw_ref, sc_ref, sh_ref, u_ref):
        xv = x_ref[...]
        r = lax.rsqrt(jnp.mean(xv * xv, axis=1, keepdims=True) + EPS)
        u_ref[...] = _bf(xv * r * nw_ref[...] * (1.0 + sc_ref[...]) + sh_ref[...])

    row = pl.BlockSpec((1, d), lambda i: (0, 0))
    return pl.pallas_call(
        body, name="prenorm_fwd", grid=(s // ts,),
        in_specs=[pl.BlockSpec((ts, d), lambda i: (i, 0)), row, row, row],
        out_specs=pl.BlockSpec((ts, d), lambda i: (i, 0)), out_shape=jax.ShapeDtypeStruct((s, d), BF16),
        compiler_params=_cparams("parallel"))(x, norm_w, scale, shift)


def _prenorm_bwd(du, x, dxres, norm_w, scale, ts):
    s, d = x.shape

    def body(du_ref, x_ref, dr_ref, nw_ref, sc_ref, gx_ref, acc_ref):
        @pl.when(pl.program_id(0) == 0)
        def _():
            acc_ref[...] = jnp.zeros_like(acc_ref)
        xv, duv = x_ref[...], du_ref[...]
        r = lax.rsqrt(jnp.mean(xv * xv, axis=1, keepdims=True) + EPS)
        xn = xv * r
        nw, sc1 = nw_ref[...], 1.0 + sc_ref[...]
        dxn = duv * (nw * sc1)
        gx_ref[...] = r * (dxn - xn * jnp.mean(dxn * xn, axis=1, keepdims=True)) + dr_ref[...]
        t = duv * xn
        acc_ref[0:1, :] += jnp.sum(t, axis=0, keepdims=True) * sc1
        acc_ref[1:2, :] += jnp.sum(t, axis=0, keepdims=True) * nw
        acc_ref[2:3, :] += jnp.sum(duv, axis=0, keepdims=True)

    tile = pl.BlockSpec((ts, d), lambda i: (i, 0))
    row = pl.BlockSpec((1, d), lambda i: (0, 0))
    return pl.pallas_call(
        body, name="prenorm_bwd", grid=(s // ts,),
        in_specs=[tile, tile, tile, row, row],
        out_specs=[tile, pl.BlockSpec((8, d), lambda i: (0, 0))],
        out_shape=[jax.ShapeDtypeStruct((s, d), F32), jax.ShapeDtypeStruct((8, d), F32)],
        compiler_params=_cparams("arbitrary"))(du, x, dxres, norm_w, scale)


CONV_CB = 512


def _conv_taps(buf_ref, ts):
    return [buf_ref[pl.ds(8 - (CONV_K - 1) + j, ts), :] for j in range(CONV_K)]


def _conv_fwd(proj, col0, width, w8, b, ts):
    s = proj.shape[0]
    cb = CONV_CB
    nt = s // ts

    def body(x_ref, w_ref, b_ref, o_ref, ds_ref, buf_ref):
        @pl.when(pl.program_id(1) == 0)
        def _():
            buf_ref[0:8, :] = jnp.zeros((8, cb), F32)
        buf_ref[pl.ds(8, ts), :] = x_ref[...].astype(F32)
        acc = b_ref[...] + jnp.zeros((ts, cb), F32)
        for j, tap in enumerate(_conv_taps(buf_ref, ts)):
            acc = acc + tap * w_ref[j:j + 1, :]
        sg = _sigmoid(acc)
        o_ref[...] = acc * sg
        ds_ref[...] = _bf(sg + acc * sg * (1.0 - sg))
        buf_ref[0:8, :] = buf_ref[pl.ds(ts, 8), :]

    c0 = col0 // cb
    tile = pl.BlockSpec((ts, cb), lambda c, i: (i, c))
    return pl.pallas_call(
        body, name="conv_fwd", grid=(width // cb, nt),
        in_specs=[pl.BlockSpec((ts, cb), lambda c, i: (i, c0 + c)), pl.BlockSpec((8, cb), lambda c, i: (0, c)),
                  pl.BlockSpec((1, cb), lambda c, i: (0, c))],
        out_specs=[tile, tile],
        out_shape=[jax.ShapeDtypeStruct((s, width), F32), jax.ShapeDtypeStruct((s, width), BF16)],
        scratch_shapes=[pltpu.VMEM((ts + 8, cb), F32)],
        compiler_params=_cparams("parallel", "arbitrary"))(proj, w8, b)


def _conv_bwd(proj, col0, width, w8, dact, dpost, dproj, ts):
    s = proj.shape[0]
    cb = CONV_CB
    nt = s // ts
    c0 = col0 // cb

    def body(x_ref, da_ref, dp_ref, w_ref, _, dx_ref, acc_ref, dbuf_ref):
        @pl.when(pl.program_id(1) == 0)
        def _():
            acc_ref[...] = jnp.zeros_like(acc_ref)
            dbuf_ref[pl.ds(ts, 8), :] = jnp.zeros((8, cb), F32)
        dconv = dp_ref[...] * da_ref[...].astype(F32)
        acc_ref[CONV_K:CONV_K + 1, :] += jnp.sum(dconv, axis=0, keepdims=True)
        dbuf_ref[pl.ds(0, ts), :] = dconv
        xv = x_ref[...].astype(F32)
        dx = jnp.zeros((ts, cb), F32)
        for j in range(CONV_K):
            shifted = dbuf_ref[pl.ds(CONV_K - 1 - j, ts), :]
            dx = dx + shifted * w_ref[j:j + 1, :]
            acc_ref[j:j + 1, :] += jnp.sum(xv * shifted, axis=0, keepdims=True)
        dx_ref[...] = _bf(dx)
        dbuf_ref[pl.ds(ts, 8), :] = dconv[0:8, :]

    tile = pl.BlockSpec((ts, cb), lambda c, i: (nt - 1 - i, c))
    wide = pl.BlockSpec((ts, cb), lambda c, i: (nt - 1 - i, c0 + c))
    return pl.pallas_call(
        body, name="conv_bwd", grid=(width // cb, nt),
        in_specs=[wide, tile, tile, pl.BlockSpec((8, cb), lambda c, i: (0, c)), pl.BlockSpec(memory_space=pl.ANY)],
        out_specs=[wide, pl.BlockSpec((8, cb), lambda c, i: (0, c))],
        out_shape=[jax.ShapeDtypeStruct(dproj.shape, dproj.dtype), jax.ShapeDtypeStruct((8, width), F32)],
        input_output_aliases={4: 0},
        scratch_shapes=[pltpu.VMEM((ts + 8, cb), F32)],
        compiler_params=_cparams("parallel", "arbitrary"))(proj, dact, dpost, w8, dproj)


def _mlstm_gates(gif_ref, gt_ref, a_scr, at_scr):
    L = gif_ref.shape[0]
    fb = _logsigmoid(gif_ref[...])
    a_scr[...] = _pick_left(_tri(L, False), fb, 3)
    at_scr[...] = _pick_right(_logsigmoid(gt_ref[...]), _tri(L, True), 3)
    return jnp.sum(fb, axis=0, keepdims=True)


def _mlstm_head(h, qk_ref, v_ref, gif, gt_ref, a, at_scr, a_last_row, c_mat, n_row, m_prev):
    L = gif.shape[0]
    q = qk_ref[:, h * ML_DQK:(h + 1) * ML_DQK] * (ML_DQK ** -0.5)
    k = qk_ref[:, (ML_HEADS + h) * ML_DQK:(ML_HEADS + h + 1) * ML_DQK]
    v = v_ref[:, h * ML_DV:(h + 1) * ML_DV]
    i_col, a_col = _lane_col(gif, h), _lane_col(a, ML_HEADS + h)
    i_row, a_row = gt_ref[h:h + 1, :], at_scr[ML_HEADS + h:ML_HEADS + h + 1, :]
    causal = lax.broadcasted_iota(jnp.int32, (L, L), 0) >= lax.broadcasted_iota(jnp.int32, (L, L), 1)
    dmat = jnp.where(causal, a_col - a_row + i_row, NEG)
    inter = a_col + m_prev
    m_t = jnp.maximum(inter, jnp.max(dmat, axis=1, keepdims=True))
    w_intra = jnp.exp(dmat - m_t)
    w_inter = jnp.exp(inter - m_t)
    sc = _nt(_bf(q), _bf(k)) * w_intra
    den = jnp.sum(sc, axis=1, keepdims=True) + w_inter * jnp.sum(q * n_row, axis=1, keepdims=True)
    floor = jnp.exp(-m_t)
    a_last = _lane_col(a_last_row, ML_HEADS + h)
    g = a_last - a_col + i_col
    m_new = jnp.maximum(a_last + m_prev, jnp.max(g, axis=0, keepdims=True))
    wk = jnp.exp(g - m_new)
    decay = jnp.exp(a_last + m_prev - m_new)
    return dict(q=q, k=k, v=v, w_intra=w_intra, w_inter=w_inter, sc=sc, den=den, floor=floor, m_new=m_new, wk=wk,
                decay=decay)


def _state_tile(n_row, m11):
    r = lax.broadcasted_iota(jnp.int32, (8, LANE), 0)
    return jnp.where(r == 0, n_row, jnp.where(r == 1, m11, 0.0))


def _mlstm_fwd(qk, proj, gates, gt):
    s = qk.shape[0]
    L = CHUNK
    nc = s // L

    def body(qk_ref, v_ref, gif_ref, gt_ref, h_ref, cst_ref, nm_ref, c_scr, nm_scr, a_scr, at_scr):
        @pl.when(pl.program_id(0) == 0)
        def _():
            c_scr[...] = jnp.zeros_like(c_scr)
            nm_scr[...] = jnp.zeros_like(nm_scr)
        a_last_row = _mlstm_gates(gif_ref, gt_ref, a_scr, at_scr)
        gif, a = gif_ref[...], a_scr[...]
        for h in range(ML_HEADS):
            c_mat, n_row = c_scr[h], nm_scr[h, 0:1, :]
            m_prev = jnp.max(nm_scr[h, 1:2, :], axis=1, keepdims=True)
            cst_ref[0, h] = c_mat
            nm_ref[0, h] = nm_scr[h]
            t = _mlstm_head(h, qk_ref, v_ref, gif, gt_ref, a, at_scr, a_last_row, c_mat, n_row, m_prev)
            num = _nn(_bf(t["sc"]), _bf(t["v"])) + t["w_inter"] * _nn(_bf(t["q"]), _bf(c_mat))
            h_ref[:, h * ML_DV:(h + 1) * ML_DV] = num / jnp.maximum(jnp.abs(t["den"]), t["floor"])
            kw = t["k"] * t["wk"]
            c_scr[h] = t["decay"] * c_mat + _tn(_bf(kw), _bf(t["v"]))
            nm_scr[h] = _state_tile(t["decay"] * n_row + jnp.sum(kw, axis=0, keepdims=True), t["m_new"])

    return pl.pallas_call(
        body, name="mlstm_fwd", grid=(nc,),
        in_specs=[pl.BlockSpec((L, 2048), lambda c: (c, 0)), pl.BlockSpec((L, 2048), lambda c: (c, O_V // 2048)),
                  pl.BlockSpec((L, LANE), lambda c: (c, 0)), pl.BlockSpec((LANE, L), lambda c: (0, c))],
        out_specs=[pl.BlockSpec((L, 2048), lambda c: (c, 0)),
                   pl.BlockSpec((1, ML_HEADS, ML_DQK, ML_DV), lambda c: (c, 0, 0, 0)),
                   pl.BlockSpec((1, ML_HEADS, 8, LANE), lambda c: (c, 0, 0, 0))],
        out_shape=[jax.ShapeDtypeStruct((s, 2048), F32), jax.ShapeDtypeStruct((nc, ML_HEADS, ML_DQK, ML_DV), F32),
                   jax.ShapeDtypeStruct((nc, ML_HEADS, 8, LANE), F32)],
        scratch_shapes=[pltpu.VMEM((ML_HEADS, ML_DQK, ML_DV), F32), pltpu.VMEM((ML_HEADS, 8, LANE), F32),
                        pltpu.VMEM((L, LANE), F32), pltpu.VMEM((LANE, L), F32)],
        compiler_params=_cparams("arbitrary"))(qk, proj, gates, gt)


def _mlstm_bwd(qk, proj, gates, gt, hout, dh, cst, nm, dproj):
    s = qk.shape[0]
    L = CHUNK
    nc = s // L

    def body(qk_ref, v_ref, gif_ref, gt_ref, h_ref, dh_ref, cst_ref, nm_ref, _, dqk_ref, dv_ref, dif_ref,
             dc_scr, dn_scr, a_scr, at_scr):
        @pl.when(pl.program_id(0) == 0)
        def _():
            dc_scr[...] = jnp.zeros_like(dc_scr)
            dn_scr[...] = jnp.zeros_like(dn_scr)
        a_last_row = _mlstm_gates(gif_ref, gt_ref, a_scr, at_scr)
        gif, a = gif_ref[...], a_scr[...]
        lane = lax.broadcasted_iota(jnp.int32, (L, LANE), 1)
        last = lax.broadcasted_iota(jnp.int32, (L, 1), 0) == L - 1
        di_tile = jnp.zeros((L, LANE), F32)
        cross = [jnp.zeros((L, LANE), F32)] * 3
        for h in range(ML_HEADS):
            c_mat, n_row = cst_ref[0, h], nm_ref[0, h, 0:1, :]
            m_prev = jnp.max(nm_ref[0, h, 1:2, :], axis=1, keepdims=True)
            t = _mlstm_head(h, qk_ref, v_ref, gif, gt_ref, a, at_scr, a_last_row, c_mat, n_row, m_prev)
            q, k, v, den = t["q"], t["k"], t["v"], t["den"]
            dhh = dh_ref[:, h * ML_DV:(h + 1) * ML_DV].astype(F32)
            hh = h_ref[:, h * ML_DV:(h + 1) * ML_DV]
            dnorm = jnp.maximum(jnp.abs(den), t["floor"])
            dnum = dhh / dnorm
            d_dn = -jnp.sum(dhh * hh, axis=1, keepdims=True) / dnorm
            dden = jnp.where(jnp.abs(den) >= t["floor"], jnp.where(den >= 0.0, d_dn, -d_dn), 0.0)
            dsc = _nt(_bf(dnum), _bf(v)) + dden
            ds = dsc * t["w_intra"]
            dq_inter = t["w_inter"] * (_nt(_bf(dnum), _bf(c_mat)) + dden * n_row)
            dq = _nn(_bf(ds), _bf(k)) + dq_inter
            dc, dn_row = dc_scr[h], dn_scr[h, 0:1, :]
            dk_state = t["wk"] * (_nt(_bf(v), _bf(dc)) + dn_row)
            dk = _tn(_bf(ds), _bf(q)) + dk_state
            dv = _tn(_bf(t["sc"]), _bf(dnum)) + t["wk"] * _nn(_bf(k), _bf(dc))
            qi = q * t["w_inter"]
            dc_scr[h] = t["decay"] * dc + _tn(_bf(qi), _bf(dnum))
            dn_scr[h] = jnp.broadcast_to(t["decay"] * dn_row + jnp.sum(qi * dden, axis=0, keepdims=True), (8, LANE))
            dqk_ref[:, h * ML_DQK:(h + 1) * ML_DQK] = dq * (ML_DQK ** -0.5)
            dqk_ref[:, (ML_HEADS + h) * ML_DQK:(ML_HEADS + h + 1) * ML_DQK] = dk
            dv_ref[:, h * ML_DV:(h + 1) * ML_DV] = _bf(dv)
            di_tile = di_tile + jnp.where(lane == h, jnp.sum(k * dk, axis=1, keepdims=True), 0.0)
            carried = t["decay"] * (_sum_all(dc * c_mat) + jnp.sum(dn_row * n_row, axis=1, keepdims=True))
            parts = (_crossing(dsc * t["sc"]),
                     jnp.sum(q * dq_inter, axis=1, keepdims=True) + jnp.where(last, carried, 0.0),
                     jnp.sum(k * dk_state, axis=1, keepdims=True))
            cross = [c + jnp.where(lane == ML_HEADS + h, p, 0.0) for c, p in zip(cross, parts)]
        dfb = cross[0] + _pick_left(_tri(L, True), cross[1], 2) + _pick_left(_tri(L, False) - _eye(L), cross[2], 2)
        dif_ref[:, 0:LANE] = _bf(di_tile + dfb * _sigmoid(-gif))
        dif_ref[:, LANE:SMALL_W] = jnp.zeros((L, SMALL_W - LANE), BF16)

    rev = lambda c: nc - 1 - c
    return pl.pallas_call(
        body, name="mlstm_bwd", grid=(nc,),
        in_specs=[pl.BlockSpec((L, 2048), lambda c: (rev(c), 0)), pl.BlockSpec((L, 2048), lambda c: (rev(c), O_V // 2048)),
                  pl.BlockSpec((L, LANE), lambda c: (rev(c), 0)), pl.BlockSpec((LANE, L), lambda c: (0, rev(c))),
                  pl.BlockSpec((L, 2048), lambda c: (rev(c), 0)), pl.BlockSpec((L, 2048), lambda c: (rev(c), 0)),
                  pl.BlockSpec((1, ML_HEADS, ML_DQK, ML_DV), lambda c: (rev(c), 0, 0, 0)),
                  pl.BlockSpec((1, ML_HEADS, 8, LANE), lambda c: (rev(c), 0, 0, 0)), pl.BlockSpec(memory_space=pl.ANY)],
        out_specs=[pl.BlockSpec((L, 2048), lambda c: (rev(c), 0)), pl.BlockSpec((L, 2048), lambda c: (rev(c), O_V // 2048)),
                   pl.BlockSpec((L, SMALL_W), lambda c: (rev(c), 0))],
        out_shape=[jax.ShapeDtypeStruct((s, 2048), F32), jax.ShapeDtypeStruct(dproj.shape, dproj.dtype),
                   jax.ShapeDtypeStruct((s, SMALL_W), BF16)],
        input_output_aliases={8: 1},
        scratch_shapes=[pltpu.VMEM((ML_HEADS, ML_DQK, ML_DV), F32), pltpu.VMEM((ML_HEADS, 8, LANE), F32),
                        pltpu.VMEM((L, LANE), F32), pltpu.VMEM((LANE, L), F32)],
        compiler_params=_cparams("arbitrary"))(qk, proj, gates, gt, hout, dh, cst, nm, dproj)


GROUP_W = SSM_HEADS // SSM_GROUPS * SSM_HEADDIM
O_B = SSM_HEADS * SSM_HEADDIM
O_C = O_B + SSM_GROUPS * SSM_STATE


def _head_expand():
    r = jnp.arange(LANE)[:, None]
    c = jnp.arange(SSM_HEADS * SSM_HEADDIM)[None, :] // SSM_HEADDIM
    return (r == c).astype(F32)


def _ssd_gates(dt_ref, dtt_ref, alog_row_ref, alog_col_ref, at_scr):
    L = dt_ref.shape[0]
    dt = _softplus(dt_ref[...])
    acoef = -jnp.exp(alog_row_ref[...])
    a = _pick_left(_tri(L, False), dt * acoef, 3)
    at_scr[...] = _pick_right(_softplus(dtt_ref[...]) * (-jnp.exp(alog_col_ref[...])), _tri(L, True), 3)
    return dt, acoef, a


def _ssd_group(g, xbc_ref, dt, a, e_ref, ax_scr):
    eg = e_ref[:, g * GROUP_W:(g + 1) * GROUP_W]
    ax_scr[...] = _pick_right(a, eg, 3)
    ax = ax_scr[...]
    alx = ax_scr[ax.shape[0] - 1:ax.shape[0], :]
    dtx = _pick_right(dt, eg, 2)
    xg = xbc_ref[:, g * GROUP_W:(g + 1) * GROUP_W]
    bg = xbc_ref[:, O_B + g * SSM_STATE:O_B + (g + 1) * SSM_STATE]
    cg = xbc_ref[:, O_C + g * SSM_STATE:O_C + (g + 1) * SSM_STATE]
    return dict(ax=ax, alx=alx, dtx=dtx, xg=xg, bg=bg, cg=cg, xdt=xg * dtx, gmat=_nt(_bf(cg), _bf(bg)))


def _ssd_decay(hh, a, at_scr):
    L = a.shape[0]
    causal = lax.broadcasted_iota(jnp.int32, (L, L), 0) >= lax.broadcasted_iota(jnp.int32, (L, L), 1)
    return jnp.exp(jnp.where(causal, _lane_col(a, hh) - at_scr[hh:hh + 1, :], NEG))


def _ssd_fwd(xbc, gates, dtt, alog_row, alog_col, dskip_x, expand):
    s = xbc.shape[0]
    L = CHUNK
    nc = s // L
    half = SSM_HEADDIM

    def body(xbc_ref, dt_ref, dtt_ref, ar_ref, ac_ref, dk_ref, e_ref, y_ref, st_ref, st_scr, at_scr, ax_scr):
        @pl.when(pl.program_id(0) == 0)
        def _():
            st_scr[...] = jnp.zeros_like(st_scr)
        dt, _, a = _ssd_gates(dt_ref, dtt_ref, ar_ref, ac_ref, at_scr)
        lane = lax.broadcasted_iota(jnp.int32, (L, LANE), 1)
        for g in range(SSM_GROUPS):
            t = _ssd_group(g, xbc_ref, dt, a, e_ref, ax_scr)
            st = st_scr[g]
            st_ref[0, g] = st
            pairs = []
            for j in range(GROUP_W // LANE):
                xp = _bf(t["xdt"][:, j * LANE:(j + 1) * LANE])
                hh = g * (SSM_HEADS // SSM_GROUPS) + 2 * j
                y0 = _nn(_bf(t["gmat"] * _ssd_decay(hh, a, at_scr)), xp)
                y1 = _nn(_bf(t["gmat"] * _ssd_decay(hh + 1, a, at_scr)), xp)
                pairs.append(jnp.where(lane < half, y0, y1))
            y = jnp.concatenate(pairs, axis=1) + _nn(_bf(t["cg"]), _bf(st)) * jnp.exp(t["ax"])
            y_ref[:, g * GROUP_W:(g + 1) * GROUP_W] = y + dk_ref[:, g * GROUP_W:(g + 1) * GROUP_W] * t["xg"]
            wts = jnp.exp(t["alx"] - t["ax"])
            st_scr[g] = jnp.exp(t["alx"]) * st + _tn(_bf(t["bg"]), _bf(t["xdt"] * wts))

    row = lambda w: pl.BlockSpec((1, w), lambda c: (0, 0))
    return pl.pallas_call(
        body, name="ssd_fwd", grid=(nc,),
        in_specs=[pl.BlockSpec((L, 3072), lambda c: (c, 0)), pl.BlockSpec((L, LANE), lambda c: (c, (O_DT - O_IF) // LANE)),
                  pl.BlockSpec((LANE, L), lambda c: (0, c)), row(LANE), pl.BlockSpec((LANE, 1), lambda c: (0, 0)),
                  row(2048), pl.BlockSpec((LANE, 2048), lambda c: (0, 0))],
        out_specs=[pl.BlockSpec((L, 2048), lambda c: (c, 0)),
                   pl.BlockSpec((1, SSM_GROUPS, SSM_STATE, GROUP_W), lambda c: (c, 0, 0, 0))],
        out_shape=[jax.ShapeDtypeStruct((s, 2048), F32),
                   jax.ShapeDtypeStruct((nc, SSM_GROUPS, SSM_STATE, GROUP_W), F32)],
        scratch_shapes=[pltpu.VMEM((SSM_GROUPS, SSM_STATE, GROUP_W), F32), pltpu.VMEM((LANE, L), F32),
                        pltpu.VMEM((L, GROUP_W), F32)],
        compiler_params=_cparams("arbitrary"))(xbc, gates, dtt, alog_row, alog_col, dskip_x, expand)


def _ssd_bwd(xbc, gates, dtt, alog_row, alog_col, dskip_x, expand, expand_t, dy, states):
    s = xbc.shape[0]
    L = CHUNK
    nc = s // L
    half = SSM_HEADDIM

    def body(xbc_ref, dt_ref, dtt_ref, ar_ref, ac_ref, dk_ref, e_ref, et_ref, dy_ref, st_ref,
             dxbc_ref, ddt_ref, accd_ref, acca_ref, dst_scr, at_scr, ax_scr):
        @pl.when(pl.program_id(0) == 0)
        def _():
            dst_scr[...] = jnp.zeros_like(dst_scr)
            accd_ref[...] = jnp.zeros_like(accd_ref)
            acca_ref[...] = jnp.zeros_like(acca_ref)
        dt, acoef, a = _ssd_gates(dt_ref, dtt_ref, ar_ref, ac_ref, at_scr)
        lane = lax.broadcasted_iota(jnp.int32, (L, LANE), 1)
        low = lane < half
        last = lax.broadcasted_iota(jnp.int32, (L, 1), 0) == L - 1
        cross = [jnp.zeros((L, LANE), F32)] * 3
        ddt_tile = jnp.zeros((L, LANE), F32)
        for g in range(SSM_GROUPS):
            t = _ssd_group(g, xbc_ref, dt, a, e_ref, ax_scr)
            xg, bg, cg, xdt, gmat = t["xg"], t["bg"], t["cg"], t["xdt"], t["gmat"]
            st, dst = st_ref[0, g], dst_scr[g]
            dyg = dy_ref[:, g * GROUP_W:(g + 1) * GROUP_W].astype(F32)
            ea, eal = jnp.exp(t["ax"]), jnp.exp(t["alx"])
            wts = jnp.exp(t["alx"] - t["ax"])
            dyi = dyg * ea
            y_inter = _nn(_bf(cg), _bf(st)) * ea
            dc = _nt(_bf(dyi), _bf(st))
            d_xdt_state = _nn(_bf(bg), _bf(dst)) * wts
            db = _nt(_bf(xdt * wts), _bf(dst))
            dst_scr[g] = eal * dst + _tn(_bf(cg), _bf(dyi))
            dg = jnp.zeros((L, L), F32)
            dx_pairs = []
            for j in range(GROUP_W // LANE):
                xp = _bf(xdt[:, j * LANE:(j + 1) * LANE])
                dyp = dyg[:, j * LANE:(j + 1) * LANE]
                dxs = []
                for b in range(2):
                    hh = g * (SSM_HEADS // SSM_GROUPS) + 2 * j + b
                    dec = _ssd_decay(hh, a, at_scr)
                    w = gmat * dec
                    dxs.append(_tn(_bf(w), _bf(dyp)))
                    dw = _nt(_bf(jnp.where(low if b == 0 else ~low, dyp, 0.0)), xp)
                    dg = dg + dw * dec
                    cross[0] = cross[0] + jnp.where(lane == hh, _crossing(dw * w), 0.0)
                dx_pairs.append(jnp.where(low, dxs[0], dxs[1]))
            d_xdt = d_xdt_state + jnp.concatenate(dx_pairs, axis=1)
            dc = dc + _nn(_bf(dg), _bf(bg))
            db = db + _tn(_bf(dg), _bf(cg))
            etg = et_ref[g * GROUP_W:(g + 1) * GROUP_W, :]
            carried = jnp.sum(dst * st, axis=0, keepdims=True) * eal
            cross[1] = cross[1] + _pick_right(dyg * y_inter + jnp.where(last, carried, 0.0), etg, 2)
            cross[2] = cross[2] + _pick_right(xdt * d_xdt_state, etg, 2)
            ddt_tile = ddt_tile + _pick_right(d_xdt * xg, etg, 2)
            dxbc_ref[:, g * GROUP_W:(g + 1) * GROUP_W] = d_xdt * t["dtx"] + dk_ref[:, g * GROUP_W:(g + 1) * GROUP_W] * dyg
            dxbc_ref[:, O_B + g * SSM_STATE:O_B + (g + 1) * SSM_STATE] = db
            dxbc_ref[:, O_C + g * SSM_STATE:O_C + (g + 1) * SSM_STATE] = dc
            accd_ref[0:1, g * GROUP_W:(g + 1) * GROUP_W] += jnp.sum(dyg * xg, axis=0, keepdims=True)
        d_da = cross[0] + _pick_left(_tri(L, True), cross[1], 2) + _pick_left(_tri(L, False) - _eye(L), cross[2], 2)
        acca_ref[0:1, :] += jnp.sum(d_da * dt, axis=0, keepdims=True)
        ddt_ref[:, 0:LANE] = _bf((ddt_tile + d_da * acoef) * _sigmoid(dt_ref[...]))
        ddt_ref[:, LANE:SMALL_W] = jnp.zeros((L, SMALL_W - LANE), BF16)

    rev = lambda c: nc - 1 - c
    row = lambda w: pl.BlockSpec((1, w), lambda c: (0, 0))
    return pl.pallas_call(
        body, name="ssd_bwd", grid=(nc,),
        in_specs=[pl.BlockSpec((L, 3072), lambda c: (rev(c), 0)), pl.BlockSpec((L, LANE), lambda c: (rev(c), (O_DT - O_IF) // LANE)),
                  pl.BlockSpec((LANE, L), lambda c: (0, rev(c))), row(LANE), pl.BlockSpec((LANE, 1), lambda c: (0, 0)),
                  row(2048), pl.BlockSpec((LANE, 2048), lambda c: (0, 0)), pl.BlockSpec((2048, LANE), lambda c: (0, 0)),
                  pl.BlockSpec((L, 2048), lambda c: (rev(c), 0)),
                  pl.BlockSpec((1, SSM_GROUPS, SSM_STATE, GROUP_W), lambda c: (rev(c), 0, 0, 0))],
        out_specs=[pl.BlockSpec((L, 3072), lambda c: (rev(c), 0)), pl.BlockSpec((L, SMALL_W), lambda c: (rev(c), 0)),
                   pl.BlockSpec((8, 2048), lambda c: (0, 0)), pl.BlockSpec((8, LANE), lambda c: (0, 0))],
        out_shape=[jax.ShapeDtypeStruct((s, 3072), F32), jax.ShapeDtypeStruct((s, SMALL_W), BF16),
                   jax.ShapeDtypeStruct((8, 2048), F32), jax.ShapeDtypeStruct((8, LANE), F32)],
        scratch_shapes=[pltpu.VMEM((SSM_GROUPS, SSM_STATE, GROUP_W), F32),
                        pltpu.VMEM((LANE, L), F32), pltpu.VMEM((L, GROUP_W), F32)],
        compiler_params=_cparams("arbitrary"))(xbc, gates, dtt, alog_row, alog_col, dskip_x, expand, expand_t, dy, states)


def _group_norm(v, width):
    outs, rs = [], []
    for k in range(v.shape[1] // width):
        blk = v[:, k * width:(k + 1) * width]
        r = lax.rsqrt(jnp.mean(blk * blk, axis=1, keepdims=True) + EPS)
        outs.append(blk * r)
        rs.append(jnp.broadcast_to(r, blk.shape))
    return jnp.concatenate(outs, axis=1), jnp.concatenate(rs, axis=1)


def _group_mean(v, width):
    return jnp.concatenate([jnp.broadcast_to(jnp.mean(v[:, k * width:(k + 1) * width], axis=1, keepdims=True),
                                             (v.shape[0], width)) for k in range(v.shape[1] // width)], axis=1)


def _post_fwd(hm, yssd, proj, ml_norm_w, ssm_norm_w, ts):
    s = hm.shape[0]

    def body(h_ref, ys_ref, o_ref, zm_ref, zs_ref, wm_ref, ws_ref, ym_ref, yso_ref):
        hn, _ = _group_norm(h_ref[...], ML_DV)
        ym_ref[...] = _bf(_sigmoid(o_ref[...].astype(F32)) * hn * wm_ref[...] * _silu(zm_ref[...].astype(F32)))
        pn, _ = _group_norm(ys_ref[...] * _silu(zs_ref[...].astype(F32)), GROUP_W)
        yso_ref[...] = _bf(pn * ws_ref[...])

    tile = pl.BlockSpec((ts, 2048), lambda i: (i, 0))
    col = lambda off: pl.BlockSpec((ts, 2048), lambda i: (i, off // 2048))
    row = pl.BlockSpec((1, 2048), lambda i: (0, 0))
    return pl.pallas_call(
        body, name="post_fwd", grid=(s // ts,),
        in_specs=[tile, tile, col(O_O), col(O_ZM), col(O_ZS), row, row],
        out_specs=[tile, tile],
        out_shape=[jax.ShapeDtypeStruct((s, 2048), BF16)] * 2,
        compiler_params=_cparams("parallel"))(hm, yssd, proj, proj, proj, ml_norm_w, ssm_norm_w)


def _post_bwd(dym, dys, hm, yssd, proj, ml_norm_w, ssm_norm_w, dproj, ts):
    s = hm.shape[0]

    def body(dym_ref, dys_ref, h_ref, ys_ref, o_ref, zm_ref, zs_ref, wm_ref, ws_ref, _,
             dh_ref, dyssd_ref, dp_ref, acc_ref):
        @pl.when(pl.program_id(0) == 0)
        def _():
            acc_ref[...] = jnp.zeros_like(acc_ref)
        hn, r = _group_norm(h_ref[...], ML_DV)
        so, zm, wm, d_ym = _sigmoid(o_ref[...].astype(F32)), zm_ref[...].astype(F32), wm_ref[...], dym_ref[...].astype(F32)
        sz = _silu(zm)
        hnw = hn * wm
        dp_ref[:, O_O:O_O + 2048] = _bf(d_ym * hnw * sz * so * (1.0 - so))
        dp_ref[:, O_ZM:O_ZM + 2048] = _bf(d_ym * so * hnw * _dsilu(zm))
        dhnw = d_ym * so * sz
        acc_ref[0:1, :] += jnp.sum(dhnw * hn, axis=0, keepdims=True)
        dhn = dhnw * wm
        dh_ref[...] = _bf(r * (dhn - hn * _group_mean(dhn * hn, ML_DV)))
        ysv, zs, d_ys = ys_ref[...], zs_ref[...].astype(F32), dys_ref[...].astype(F32)
        szs = _silu(zs)
        pn, r2 = _group_norm(ysv * szs, GROUP_W)
        acc_ref[1:2, :] += jnp.sum(d_ys * pn, axis=0, keepdims=True)
        dpn = d_ys * ws_ref[...]
        dp = r2 * (dpn - pn * _group_mean(dpn * pn, GROUP_W))
        dyssd_ref[...] = _bf(dp * szs)
        dp_ref[:, O_ZS:O_ZS + 2048] = _bf(dp * ysv * _dsilu(zs))

    tile = pl.BlockSpec((ts, 2048), lambda i: (i, 0))
    col = lambda off: pl.BlockSpec((ts, 2048), lambda i: (i, off // 2048))
    row = pl.BlockSpec((1, 2048), lambda i: (0, 0))
    sds = lambda dt: jax.ShapeDtypeStruct((s, 2048), dt)
    return pl.pallas_call(
        body, name="post_bwd", grid=(s // ts,),
        in_specs=[tile, tile, tile, tile, col(O_O), col(O_ZM), col(O_ZS), row, row, pl.BlockSpec(memory_space=pl.ANY)],
        out_specs=[tile, tile, pl.BlockSpec((ts, O_MG), lambda i: (i, 0)), pl.BlockSpec((8, 2048), lambda i: (0, 0))],
        out_shape=[sds(BF16), sds(BF16), jax.ShapeDtypeStruct(dproj.shape, dproj.dtype), jax.ShapeDtypeStruct((8, 2048), F32)],
        input_output_aliases={9: 2},
        compiler_params=_cparams("arbitrary"))(dym, dys, hm, yssd, proj, proj, proj, ml_norm_w, ssm_norm_w, dproj)


def _merge(x, ym, ys, proj, target, gate, final_w, wpm, wps, wo, ts):
    wpm_t, wps_t, wo_t = wpm.T, wps.T, wo.T
    s, d = x.shape

    def body(x_ref, ym_ref, ys_ref, mg_ref, t_ref, gate_ref, fw_ref, wpm_ref, wps_ref, wo_ref, wpmt_ref, wpst_ref, wot_ref,
             dres_ref, mer_ref, dmo_ref, dpm_ref, dps_ref, dym_ref, dys_ref, dmg_ref, acc_ref):
        @pl.when(pl.program_id(0) == 0)
        def _():
            acc_ref[...] = jnp.zeros_like(acc_ref)
        gm, gs = _sigmoid(mg_ref[:, 0:d].astype(F32)), _sigmoid(mg_ref[:, d:2 * d].astype(F32))
        pm = _nn(ym_ref[...], wpm_ref[...])
        ps = _nn(ys_ref[...], wps_ref[...])
        merged = _bf(gm * pm + gs * ps)
        mer_ref[...] = merged
        mo = _nn(merged, wo_ref[...])
        gate, fw = gate_ref[...], fw_ref[...]
        out = x_ref[...] + gate * mo
        r = lax.rsqrt(jnp.mean(out * out, axis=1, keepdims=True) + EPS)
        on = out * r
        diff = on * fw - t_ref[...]
        acc_ref[0:1, :] += jnp.sum(0.5 * jnp.sum(diff * diff, axis=1, keepdims=True) / d, axis=0, keepdims=True)
        dyv = diff * (1.0 / d)
        acc_ref[1:2, :] += jnp.sum(dyv * on, axis=0, keepdims=True)
        don = dyv * fw
        dout = r * (don - on * jnp.mean(don * on, axis=1, keepdims=True))
        dres_ref[...] = dout
        acc_ref[2:3, :] += jnp.sum(dout * mo, axis=0, keepdims=True)
        dmo = _bf(dout * gate)
        dmo_ref[...] = dmo
        dmer = _nn(dmo, wot_ref[...])
        dpm, dps = _bf(dmer * gm), _bf(dmer * gs)
        dpm_ref[...] = dpm
        dps_ref[...] = dps
        dmg_ref[:, 0:d] = _bf(dmer * pm * gm * (1.0 - gm))
        dmg_ref[:, d:2 * d] = _bf(dmer * ps * gs * (1.0 - gs))
        dym_ref[...] = _bf(_nn(dpm, wpmt_ref[...]))
        dys_ref[...] = _bf(_nn(dps, wpst_ref[...]))

    t1 = pl.BlockSpec((ts, d), lambda i: (i, 0))
    t2 = pl.BlockSpec((ts, 2 * d), lambda i: (i, 0))
    row = pl.BlockSpec((1, d), lambda i: (0, 0))
    whole = pl.BlockSpec(memory_space=pltpu.VMEM)
    sd = lambda w, dt: jax.ShapeDtypeStruct((s, w), dt)
    return pl.pallas_call(
        body, name="merge_fwd_bwd", grid=(s // ts,),
        in_specs=[t1, t2, t2, pl.BlockSpec((ts, 2 * d), lambda i: (i, O_MG // (2 * d))), t1, row, row] + [whole] * 6,
        out_specs=[t1, t1, t1, t1, t1, t2, t2, pl.BlockSpec((ts, 2 * d), lambda i: (i, O_MG // (2 * d))),
                   pl.BlockSpec((8, d), lambda i: (0, 0))],
        out_shape=[sd(d, F32), sd(d, BF16), sd(d, BF16), sd(d, BF16), sd(d, BF16), sd(2 * d, BF16), sd(2 * d, BF16),
                   sd(NP, BF16), jax.ShapeDtypeStruct((8, d), F32)],
        compiler_params=_cparams("arbitrary"))(x, ym, ys, proj, target, gate, final_w, wpm, wps, wo, wpm_t, wps_t, wo_t)


def _adamw(w, g, m, v, tr):
    if w.ndim == 2 and w.shape[0] % 8:
        tile, steps = pl.BlockSpec((w.shape[0], tr), lambda i: (0, i)), w.shape[1] // tr
    else:
        lead = (None,) * (w.ndim - 2)
        tile, steps = pl.BlockSpec(lead + (tr, w.shape[-1]), lambda i: (0,) * len(lead) + (i, 0)), w.shape[-2] // tr

    def body(w_ref, g_ref, m_ref, v_ref, d_ref, nm_ref, nv_ref):
        gv = g_ref[...]
        m2 = ADAM_B1 * m_ref[...] + (1.0 - ADAM_B1) * gv
        v2 = ADAM_B2 * v_ref[...] + (1.0 - ADAM_B2) * (gv * gv)
        m_hat = m2 / (1.0 - ADAM_B1 ** ADAM_STEP)
        v_hat = v2 / (1.0 - ADAM_B2 ** ADAM_STEP)
        d_ref[...] = -ADAM_LR * (m_hat / (jnp.sqrt(v_hat) + ADAM_EPS) + ADAM_WD * w_ref[...])
        nm_ref[...] = m2
        nv_ref[...] = v2

    return pl.pallas_call(
        body, name="adamw", grid=(steps,), in_specs=[tile] * 4, out_specs=[tile] * 3,
        out_shape=[jax.ShapeDtypeStruct(w.shape, F32)] * 3,
        compiler_params=_cparams("parallel"))(w, g.reshape(w.shape), m, v)


def _sum_parts(own, parts, tr, dtype=F32):
    p, rows, cols = parts.shape

    def body(*refs):
        p_ref, o_ref = refs[-2], refs[-1]
        acc = p_ref[0].astype(F32) if own is None else refs[0][...].astype(F32) + p_ref[0].astype(F32)
        for i in range(1, p):
            acc = acc + p_ref[i].astype(F32)
        o_ref[...] = acc.astype(dtype)

    tile = pl.BlockSpec((tr, cols), lambda i: (i, 0))
    ins = ([] if own is None else [tile]) + [pl.BlockSpec((p, tr, cols), lambda i: (0, i, 0))]
    args = ([] if own is None else [own]) + [parts]
    return pl.pallas_call(
        body, name="sum_parts", grid=(rows // tr,), in_specs=ins, out_specs=tile,
        out_shape=jax.ShapeDtypeStruct((rows, cols), dtype), compiler_params=_cparams("parallel"))(*args)


def _position():
    return lax.axis_index("x"), lax.axis_index("y"), lax.axis_index("c")


def _flip(pos, k):
    return tuple(1 - p if (k >> s) & 1 else p for p, s in zip(pos, (2, 1, 0)))


def _allgather8(block):
    rows, cols = block.shape

    def body(x_ref, o_ref, send_sems, recv_sems, local_sem):
        pos = _position()
        me = 4 * pos[0] + 2 * pos[1] + pos[2]
        mine = pltpu.make_async_copy(x_ref, o_ref.at[me], local_sem)
        mine.start()
        copies = [pltpu.make_async_remote_copy(src_ref=x_ref, dst_ref=o_ref.at[me], send_sem=send_sems.at[k - 1],
                                               recv_sem=recv_sems.at[k - 1], device_id=_flip(pos, k), device_id_type=MESH)
                  for k in range(1, N_DEV)]
        for cp in copies:
            cp.start()
        for cp in copies:
            cp.wait()
        mine.wait()

    vmem = pl.BlockSpec(memory_space=pltpu.VMEM)
    return pl.pallas_call(
        body, name="allgather8", in_specs=[vmem], out_specs=vmem,
        out_shape=jax.ShapeDtypeStruct((N_DEV, rows, cols), block.dtype),
        scratch_shapes=[pltpu.SemaphoreType.DMA((N_DEV - 1,)), pltpu.SemaphoreType.DMA((N_DEV - 1,)),
                        pltpu.SemaphoreType.DMA],
        compiler_params=pltpu.CompilerParams(vmem_limit_bytes=VMEM_LIMIT))(block)


COPY_BYTES = 1 << 20


def _row_chunks(rows, row_bytes):
    n = max(1, min(rows // 16, -(-rows * row_bytes // COPY_BYTES)))
    while rows % (16 * n):
        n -= 1
    return [(i * (rows // n), rows // n) for i in range(n)]


def _weight_gather(shards):
    n = len(shards)
    pieces = [_row_chunks(a.shape[1], a.shape[2] * a.dtype.itemsize) for a in shards]
    plan = [(a, k, r0, nr) for a in range(n) for k in range(1, N_CHIPS) for r0, nr in pieces[a]]

    def body(*refs):
        ins, outs = refs[:n], refs[n:2 * n]
        ici_send, ici_recv, d2d_send, d2d_recv = refs[2 * n:]
        pos = _position()
        chip, core = 2 * pos[0] + pos[1], pos[2]
        sibling = _flip(pos, 1)
        sent = []
        for i, (a, k, r0, nr) in enumerate(plan):
            cp = pltpu.make_async_remote_copy(
                src_ref=ins[a].at[core, pl.ds(r0, nr)], dst_ref=outs[a].at[chip, core, pl.ds(r0, nr)],
                send_sem=ici_send.at[i], recv_sem=ici_recv.at[i], device_id=_flip(pos, 2 * k), device_id_type=MESH)
            cp.start()
            sent.append(cp)
        passed = []
        for i, (a, k, r0, nr) in enumerate(plan):
            there = _flip(pos, 2 * k)
            landed = outs[a].at[2 * there[0] + there[1], core, pl.ds(r0, nr)]
            sent[i].wait_recv()
            cp = pltpu.make_async_remote_copy(src_ref=landed, dst_ref=landed, send_sem=d2d_send.at[i],
                                              recv_sem=d2d_recv.at[i], device_id=sibling, device_id_type=MESH)
            cp.start()
            passed.append(cp)
        for cp in passed:
            cp.wait()
        for cp in sent:
            cp.wait_send()

    hbm = pl.BlockSpec(memory_space=pl.ANY)
    sems = pltpu.SemaphoreType.DMA((len(plan),))
    return pl.pallas_call(
        body, name="weight_gather", in_specs=[hbm] * n, out_specs=[hbm] * n,
        out_shape=[jax.ShapeDtypeStruct((N_CHIPS,) + a.shape, a.dtype) for a in shards],
        scratch_shapes=[sems, sems, sems, sems],
        compiler_params=pltpu.CompilerParams(has_side_effects=True))(*shards)


def _exchange(name, arrays, out_shapes, plan, n_remote, n_local):
    n, m = len(arrays), len(out_shapes)

    def body(*refs):
        send_sems, recv_sems, local_sems = refs[n + m:]
        remote, local = plan(_position(), refs[:n], refs[n:n + m])
        assert (len(remote), len(local)) == (n_remote, n_local)
        copies = [pltpu.make_async_copy(src, dst, local_sems.at[i]) for i, (src, dst) in enumerate(local)]
        copies += [pltpu.make_async_remote_copy(src_ref=src, dst_ref=dst, send_sem=send_sems.at[i], recv_sem=recv_sems.at[i],
                                                device_id=dev, device_id_type=MESH)
                   for i, (src, dst, dev) in enumerate(remote)]
        for cp in copies:
            cp.start()
        for cp in copies:
            cp.wait()

    hbm = pl.BlockSpec(memory_space=pl.ANY)
    return pl.pallas_call(
        body, name=name, in_specs=[hbm] * n, out_specs=[hbm] * m, out_shape=out_shapes,
        scratch_shapes=[pltpu.SemaphoreType.DMA((n_remote,)), pltpu.SemaphoreType.DMA((n_remote,)),
                        pltpu.SemaphoreType.DMA((max(n_local, 1),))],
        compiler_params=pltpu.CompilerParams(has_side_effects=True))(*arrays)


def _pair_send(slabs):
    n = len(slabs)
    pieces = [_row_chunks(g.shape[2], g.shape[3] * g.dtype.itemsize) for g in slabs]

    def plan(pos, ins, outs):
        return [(ins[a].at[j, 1 - pos[2], pl.ds(r0, nr)], outs[a].at[j, pl.ds(r0, nr)], _flip(pos, 1))
                for a in range(n) for j in range(N_CHIPS) for r0, nr in pieces[a]], []

    return _exchange("pair_send", slabs, [jax.ShapeDtypeStruct((N_CHIPS,) + g.shape[2:], g.dtype) for g in slabs], plan,
                     N_CHIPS * sum(len(p) for p in pieces), 0)


def _chip_scatter_copies(pos, sums, lands, send_sems, recv_sems):
    copies = []
    for a in range(len(sums)):
        for k in range(1, N_CHIPS):
            to = _flip(pos, 2 * k)
            for r0, nr in _row_chunks(sums[a].shape[1], sums[a].shape[2] * sums[a].dtype.itemsize):
                i = len(copies)
                copies.append(pltpu.make_async_remote_copy(
                    src_ref=sums[a].at[2 * to[0] + to[1], pl.ds(r0, nr)], dst_ref=lands[a].at[k - 1, pl.ds(r0, nr)],
                    send_sem=send_sems.at[i], recv_sem=recv_sems.at[i], device_id=to, device_id_type=MESH))
    return copies


def _chip_scatter_start(sums):
    n = len(sums)
    n_copies = (N_CHIPS - 1) * sum(len(_row_chunks(g.shape[1], g.shape[2] * g.dtype.itemsize)) for g in sums)
    lands = [lax.empty((N_CHIPS - 1,) + g.shape[1:], g.dtype) for g in sums]

    def body(*refs):
        send_sems, recv_sems = refs[2 * n], refs[2 * n + 1]
        for cp in _chip_scatter_copies(_position(), refs[:n], refs[n:2 * n], send_sems, recv_sems):
            cp.start()
        refs[-1][...] = jnp.zeros((8, LANE), F32)

    hbm = pl.BlockSpec(memory_space=pltpu.HBM)
    sem = pl.BlockSpec(memory_space=pltpu.SEMAPHORE)
    operands = [pltpu.with_memory_space_constraint(t, pltpu.HBM) for t in list(sums) + lands]
    out = pl.pallas_call(
        body, name="chip_scatter_start", in_specs=[hbm] * (2 * n),
        out_specs=[sem, sem] + [hbm] * (2 * n) + [pl.BlockSpec(memory_space=pltpu.VMEM)],
        out_shape=[pltpu.SemaphoreType.DMA((n_copies,)), pltpu.SemaphoreType.DMA((n_copies,))]
        + [pltpu.HBM(t.shape, t.dtype) for t in operands] + [jax.ShapeDtypeStruct((8, LANE), F32)],
        input_output_aliases={i: 2 + i for i in range(2 * n)},
        compiler_params=pltpu.CompilerParams(has_side_effects=pltpu.SideEffectType.DATAFLOW_SIDE_EFFECTING))(*operands)
    return out[0], out[1], out[2:2 + n], out[2 + n:2 + 2 * n], out[-1]


def _chip_scatter_wait(send_sems, recv_sems, sums, lands, after):
    n = len(sums)

    def body(*refs):
        for cp in _chip_scatter_copies(_position(), refs[:n], refs[n:2 * n], refs[2 * n], refs[2 * n + 1]):
            cp.wait_send()
            cp.wait_recv()

    hbm = pl.BlockSpec(memory_space=pltpu.HBM)
    sem = pl.BlockSpec(memory_space=pltpu.SEMAPHORE)
    out = pl.pallas_call(
        body, name="chip_scatter_wait", in_specs=[hbm] * (2 * n) + [sem, sem, pl.BlockSpec(memory_space=pl.ANY)],
        out_specs=[hbm] * (2 * n), out_shape=[pltpu.HBM(t.shape, t.dtype) for t in list(sums) + list(lands)],
        input_output_aliases={i: i for i in range(2 * n)},
        compiler_params=pltpu.CompilerParams(has_side_effects=pltpu.SideEffectType.DATAFLOW_SIDE_EFFECTING))(
            *sums, *lands, send_sems, recv_sems, after)
    return out[:n], out[n:]


def _pair_exchange(halves):
    n = len(halves)
    pieces = [_row_chunks(h.shape[0], h.shape[1] * h.dtype.itemsize) for h in halves]

    def plan(pos, ins, outs):
        return [(ins[a].at[pl.ds(r0, nr)], outs[a].at[pl.ds(r0, nr)], _flip(pos, 1))
                for a in range(n) for r0, nr in pieces[a]], []

    return _exchange("pair_exchange", halves, [jax.ShapeDtypeStruct(h.shape, h.dtype) for h in halves], plan,
                     sum(len(p) for p in pieces), 0)


def _pack(arrays):
    flat = jnp.concatenate([a.reshape(-1).astype(F32) for a in arrays])
    size = -(-flat.shape[0] // (8 * LANE)) * (8 * LANE)
    return jnp.pad(flat, (0, size - flat.shape[0])).reshape(size // LANE, LANE)


def _unpack(buf, shapes):
    flat = buf.reshape(-1)
    out, off = [], 0
    for shp in shapes:
        n = math.prod(shp)
        out.append(flat[off:off + n].reshape(shp))
        off += n
    return out


def _unpack_rows(bufs, shapes):
    flat = bufs.reshape(bufs.shape[0], -1)
    out, off = [], 0
    for shp in shapes:
        n = math.prod(shp)
        out.append(flat[:, off:off + n].reshape((bufs.shape[0],) + shp))
        off += n
    return out


def _taps8(w):
    return jnp.pad(w, ((0, 8 - CONV_K), (0, 0)))


def _local_step(xs, tgt, scale, shift, gate, norm_w, w_in_p, b_in_p, ml_conv_w, ml_conv_b, ml_norm_w, ssm_conv_w,
                ssm_conv_b, ssm_a_log, ssm_d, ssm_norm_w, wpm, wps, wo, final_w, start_exchange=None):
    s = xs.shape[0]
    ts = min(512, s)
    tm = min(2048, s)
    u = _prenorm_fwd(xs, norm_w, scale, shift, ts)
    proj = _matmul_bias(u, w_in_p, b_in_p, tm, 512, 0, O_IF, BF16)
    gates = _matmul_bias(u, w_in_p, b_in_p, tm, 512, O_IF, NP - O_IF, F32)
    mlw8, ssw8 = _taps8(ml_conv_w), _taps8(ssm_conv_w)
    qk, qk_dact = _conv_fwd(proj, O_QK, 2048, mlw8, ml_conv_b, ts)
    xbc, xbc_dact = _conv_fwd(proj, O_XBC, 3072, ssw8, ssm_conv_b, ts)
    gt = gates[:, :LANE].T
    dtt = gates[:, O_DT - O_IF:O_DT - O_IF + LANE].T
    hm, cst, nm = _mlstm_fwd(qk, proj, gates, gt)
    alog_row = jnp.pad(ssm_a_log, ((0, 0), (0, LANE - SSM_HEADS)))
    alog_col = alog_row.reshape(LANE, 1)
    dskip_x = jnp.repeat(ssm_d[0], SSM_HEADDIM)[None]
    expand = _head_expand()
    yssd, sst = _ssd_fwd(xbc, gates, dtt, alog_row, alog_col, dskip_x, expand)
    tp = min(128, s)
    ym, ys = _post_fwd(hm, yssd, proj, ml_norm_w, ssm_norm_w, tp)
    dxres, merged, dmo, dpm, dps, dym, dys, dproj, acc_m = _merge(xs, ym, ys, proj, tgt, gate, final_w, wpm, wps, wo, tp)
    dh, dyssd, dproj, acc_p = _post_bwd(dym, dys, hm, yssd, proj, ml_norm_w, ssm_norm_w, dproj, tp)
    dqk, dproj, dif = _mlstm_bwd(qk, proj, gates, gt, hm, dh, cst, nm, dproj)
    dxbc, ddt, accd, acca = _ssd_bwd(xbc, gates, dtt, alog_row, alog_col, dskip_x, expand, expand.T, dyssd, sst)
    dproj, acc_cq = _conv_bwd(proj, O_QK, 2048, mlw8, qk_dact, dqk, dproj, ts)
    dproj, acc_cx = _conv_bwd(proj, O_XBC, 3072, ssw8, xbc_dact, dxbc, dproj, ts)
    dproj = dproj.at[:, O_IF:O_IF + SMALL_W].set(dif).at[:, O_DT:O_DT + SMALL_W].set(ddt)
    gw_in_p, gb_in_p = _matmul_tn(u.T, dproj, tm, 512, with_colsum=True, a_is_transposed=True)
    g_wpm = _matmul_tn(ym, dpm, tm, 512)
    g_wps = _matmul_tn(ys, dps, tm, 512)
    g_wo = _matmul_tn(merged, dmo, tm, 512)
    token, in_flight = (None, None) if start_exchange is None else start_exchange(gw_in_p, g_wpm, g_wps, g_wo)
    du = _matmul_nt(dproj, w_in_p, tm, 512, after=token)
    grad_x, acc_n = _prenorm_bwd(du, xs, dxres, norm_w, scale, ts)
    a_coef = -jnp.exp(ssm_a_log[0])
    small = dict(
        mod=jnp.concatenate([acc_n[2], acc_n[1], acc_m[2]]), norm_w=acc_n[0], b_in=_unpad_cols(gb_in_p[0]),
        ml_conv_w=acc_cq[0:CONV_K], ml_conv_b=acc_cq[CONV_K], ml_norm_w=acc_p[0], ssm_conv_w=acc_cx[0:CONV_K],
        ssm_conv_b=acc_cx[CONV_K], ssm_a_log=acca[0, :SSM_HEADS] * a_coef,
        ssm_d=accd[0].reshape(SSM_HEADS, SSM_HEADDIM).sum(axis=1), ssm_norm_w=acc_p[1], final_w=acc_m[1], loss=acc_m[0, 0:1])
    return grad_x, small, gw_in_p, g_wpm, g_wps, g_wo, in_flight


WEIGHTS = ("norm_w", "ada_w", "ada_b", "w_in", "b_in", "ml_conv_w", "ml_conv_b", "ml_norm_w", "ssm_conv_w", "ssm_conv_b",
           "ssm_a_log", "ssm_d", "ssm_norm_w", "w_proj_m", "w_proj_s", "w_out", "final_w")
LARGE = ("ada_w", "w_in", "w_proj_m", "w_proj_s", "w_out")
SMALL_SUMS = (("mod", (3 * D_MODEL,)), ("norm_w", (D_MODEL,)), ("b_in", (IN_WIDTH,)), ("ml_conv_w", (CONV_K, 2048)),
              ("ml_conv_b", (2048,)), ("ml_norm_w", (2048,)), ("ssm_conv_w", (CONV_K, 3072)), ("ssm_conv_b", (3072,)),
              ("ssm_a_log", (SSM_HEADS,)), ("ssm_d", (SSM_HEADS,)), ("ssm_norm_w", (2048,)), ("final_w", (D_MODEL,)),
              ("loss", (1,)))


def kernel(x, c, norm_w, ada_w, ada_b, w_in, b_in, ml_conv_w, ml_conv_b, ml_norm_w, ssm_conv_w, ssm_conv_b, ssm_a_log, ssm_d, ssm_norm_w, w_proj_m, w_proj_s, w_out, final_w, loss_target, m_norm_w, m_ada_w, m_ada_b, m_w_in, m_b_in, m_ml_conv_w, m_ml_conv_b, m_ml_norm_w, m_ssm_conv_w, m_ssm_conv_b, m_ssm_a_log, m_ssm_d, m_ssm_norm_w, m_w_proj_m, m_w_proj_s, m_w_out, m_final_w, v_norm_w, v_ada_w, v_ada_b, v_w_in, v_b_in, v_ml_conv_w, v_ml_conv_b, v_ml_norm_w, v_ssm_conv_w, v_ssm_conv_b, v_ssm_a_log, v_ssm_d, v_ssm_norm_w, v_w_proj_m, v_w_proj_s, v_w_out, v_final_w):
    w = dict(norm_w=norm_w, ada_w=ada_w, ada_b=ada_b, w_in=w_in, b_in=b_in, ml_conv_w=ml_conv_w, ml_conv_b=ml_conv_b,
             ml_norm_w=ml_norm_w, ssm_conv_w=ssm_conv_w, ssm_conv_b=ssm_conv_b, ssm_a_log=ssm_a_log, ssm_d=ssm_d,
             ssm_norm_w=ssm_norm_w, w_proj_m=w_proj_m, w_proj_s=w_proj_s, w_out=w_out, final_w=final_w)
    m = dict(zip(WEIGHTS, (m_norm_w, m_ada_w, m_ada_b, m_w_in, m_b_in, m_ml_conv_w, m_ml_conv_b, m_ml_norm_w, m_ssm_conv_w,
                           m_ssm_conv_b, m_ssm_a_log, m_ssm_d, m_ssm_norm_w, m_w_proj_m, m_w_proj_s, m_w_out, m_final_w)))
    v = dict(zip(WEIGHTS, (v_norm_w, v_ada_w, v_ada_b, v_w_in, v_b_in, v_ml_conv_w, v_ml_conv_b, v_ml_norm_w, v_ssm_conv_w,
                           v_ssm_conv_b, v_ssm_a_log, v_ssm_d, v_ssm_norm_w, v_w_proj_m, v_w_proj_s, v_w_out, v_final_w)))
    pos = _position()
    chip = 2 * pos[0] + pos[1]
    dev = 2 * chip + pos[2]
    mlw_cols, ssw_cols, ada_cols = ml_conv_w.shape[2], ssm_conv_w.shape[2], ada_w.shape[2]

    g0 = _allgather8(_pack([c, ml_conv_w, ssm_conv_w]))
    c_all, mlw_all, ssw_all = _unpack_rows(g0, [(D_MODEL,), (CONV_K, mlw_cols), (CONV_K, ssw_cols)])
    ml_conv_full = mlw_all[0::2].transpose(1, 0, 2).reshape(CONV_K, N_CHIPS * mlw_cols)
    ssm_conv_full = ssw_all[0::2].transpose(1, 0, 2).reshape(CONV_K, N_CHIPS * ssw_cols)

    ada_b_mine = lax.dynamic_slice_in_dim(ada_b, chip * ada_cols, ada_cols, axis=1)
    g1 = _allgather8(_ada_fwd(c_all, ada_w[0], ada_b_mine))
    mod = lax.dynamic_index_in_dim(g1[0::2], dev, axis=1, keepdims=False).reshape(1, 3 * D_MODEL)
    shift, scale, gate = mod[:, :D_MODEL], mod[:, D_MODEL:2 * D_MODEL], mod[:, 2 * D_MODEL:]

    mine = [_bf(a[0]).reshape(2, a.shape[1] // 2, a.shape[2]) for a in (w_in, w_proj_m, w_proj_s, w_out)]
    gw = [lax.dynamic_update_index_in_dim(got, own, chip, 0).reshape(N_CHIPS, -1, own.shape[-1])
          for got, own in zip(_weight_gather(mine), mine)]
    w_in_p = _shards_to_padded(gw[0])
    wpm, wps, wo = (a.reshape(-1, D_MODEL) for a in gw[1:])

    def start_exchange(g_w_in, g_wpm, g_wps, g_wo):
        split = lambda g, rows: _bf(g).reshape(N_CHIPS, 2, rows // (2 * N_CHIPS), g.shape[-1])
        slabs = [split(_padded_to_shards(_bf(g_w_in)), N_CHIPS * D_MODEL),
                 split(g_wpm, g_wpm.shape[0]), split(g_wps, g_wps.shape[0]), split(g_wo, g_wo.shape[0])]
        pair_sums = []
        for slab, rec in zip(slabs, _pair_send(slabs)):
            kept = lax.dynamic_index_in_dim(slab, pos[2], 1, keepdims=False)
            rows = kept.shape[0] * kept.shape[1]
            both = _sum_parts(kept.reshape(rows, -1), rec.reshape(1, rows, -1), 32, BF16)
            pair_sums.append(both.reshape(kept.shape))
        send_sems, recv_sems, sums, lands, token = _chip_scatter_start(pair_sums)
        return token, (send_sems, recv_sems, sums, lands)

    grad_x, small, _, _, _, _, in_flight = _local_step(
        x[0], loss_target[0], scale, shift, gate, norm_w, w_in_p, _pad_cols(b_in), ml_conv_full, ml_conv_b, ml_norm_w,
        ssm_conv_full, ssm_conv_b, ssm_a_log, ssm_d, ssm_norm_w, wpm, wps, wo, final_w[None], start_exchange)

    g2 = _allgather8(_pack([small[name] for name, _ in SMALL_SUMS]))
    total = dict(zip([name for name, _ in SMALL_SUMS], _unpack(_sum_parts(None, g2, g2.shape[1]), [s for _, s in SMALL_SUMS])))
    dmod_all = g2[:, :3 * D_MODEL // LANE].reshape(N_DEV, 3 * D_MODEL)
    grads = dict(total)
    grads["ada_b"] = total["mod"]
    grads["ml_conv_w"] = lax.dynamic_slice_in_dim(total["ml_conv_w"], chip * mlw_cols, mlw_cols, axis=1)
    grads["ssm_conv_w"] = lax.dynamic_slice_in_dim(total["ssm_conv_w"], chip * ssw_cols, ssw_cols, axis=1)
    grads["ada_w"] = _ada_bwd(c_all, lax.dynamic_slice_in_dim(dmod_all, chip * ada_cols, ada_cols, axis=1))

    halves = []
    for both, rec in zip(*_chip_scatter_wait(*in_flight, grad_x)):
        halves.append(_sum_parts(lax.dynamic_index_in_dim(both, chip, 0, keepdims=False), rec, 32))
    for name, half, other in zip(("w_in", "w_proj_m", "w_proj_s", "w_out"), halves, _pair_exchange(halves)):
        grads[name] = lax.cond(pos[2] == 0, lambda mine, theirs: jnp.concatenate([mine, theirs]),
                               lambda mine, theirs: jnp.concatenate([theirs, mine]), half, other)

    delta, new_m, new_v = {}, {}, {}
    for name in LARGE:
        if w[name].shape[-1] % LANE:
            flat = lambda a: a.reshape(a.shape[-2:]).T
            back = lambda a: a.T.reshape(w[name].shape)
            g_flat = flat(grads[name])
            delta[name], new_m[name], new_v[name] = (back(a) for a in _adamw(flat(w[name]), g_flat, flat(m[name]), flat(v[name]), LANE))
            grads[name] = back(g_flat)
        else:
            delta[name], new_m[name], new_v[name] = _adamw(w[name], grads[name], m[name], v[name], 64)
    rest = [name for name in WEIGHTS if name not in LARGE]
    packed = [_pack([t[name] for name in rest]) for t in (w, grads, m, v)]
    for out, buf in zip((delta, new_m, new_v), _adamw(*packed, packed[0].shape[0])):
        out.update(zip(rest, _unpack(buf, [w[name].shape for name in rest])))
    loss = total["loss"][0]
    return (loss, grad_x[None], *[grads[name].reshape(w[name].shape) for name in WEIGHTS], *[delta[name] for name in WEIGHTS],
            *[new_m[name] for name in WEIGHTS], *[new_v[name] for name in WEIGHTS])
```

```python
import functools
import math

import jax
import jax.numpy as jnp
from jax import lax
from jax.experimental import pallas as pl
from jax.experimental.pallas import tpu as pltpu

F32 = jnp.float32
BF16 = jnp.bfloat16
HI = lax.Precision.HIGHEST
MESH = pl.DeviceIdType.MESH

D_MODEL = 1024
EPS = 1e-6
CONV_K = 4
ML_HEADS = 8
ML_DQK = 128
ML_DV = 256
SSM_HEADS = 32
SSM_HEADDIM = 64
SSM_GROUPS = 4
SSM_STATE = 128
IN_WIDTH = 15408
N_CHIPS = 4
N_DEV = 8
ADAM_LR, ADAM_B1, ADAM_B2, ADAM_EPS, ADAM_WD, ADAM_STEP = 0.001, 0.9, 0.999, 1e-08, 0.01, 10

O_O, O_ZM, O_ZS, O_MG, O_QK, O_V, O_XBC, O_IF, O_DT = 0, 2048, 4096, 6144, 8192, 10240, 12288, 15360, 15616
SMALL_W = 256
NP = 15872
LANE = 128
CHUNK = 128
NEG = -1e30
VMEM_LIMIT = 48 * 1024 * 1024


def _cparams(*sem):
    return pltpu.CompilerParams(dimension_semantics=sem, vmem_limit_bytes=VMEM_LIMIT)


def _pad_cols(w):
    z = lambda n: jnp.zeros(w.shape[:-1] + (n,), w.dtype)
    return jnp.concatenate([w[..., 4096:8192], w[..., 11280:13328], w[..., 13360:15408], w[..., :4096], w[..., 8208:11280],
                            w[..., 8192:8208], z(SMALL_W - 16), w[..., 13328:13360], z(SMALL_W - 32)], axis=-1)


def _unpad_cols(g):
    return jnp.concatenate([g[..., O_QK:O_QK + 4096], g[..., O_O:O_O + 4096], g[..., O_IF:O_IF + 16],
                            g[..., O_XBC:O_XBC + 3072], g[..., O_ZS:O_ZS + 2048], g[..., O_DT:O_DT + 32],
                            g[..., O_MG:O_MG + 2048]], axis=-1)


PADDED_SEGMENTS = ((4096, 8192, 0), (11280, 13328, 0), (13360, 15408, 0), (0, 4096, 0), (8208, 11280, 0),
                   (8192, 8208, SMALL_W - 16), (13328, 13360, SMALL_W - 32))
SHARD_W = IN_WIDTH // N_CHIPS


def _shards_to_padded(shards):
    parts = []
    for first, last, pad in PADDED_SEGMENTS:
        for j in range(N_CHIPS):
            lo, hi = max(first, j * SHARD_W), min(last, (j + 1) * SHARD_W)
            if lo < hi:
                parts.append(shards[j][:, lo - j * SHARD_W:hi - j * SHARD_W])
        if pad:
            parts.append(jnp.zeros((shards.shape[1], pad), shards.dtype))
    return jnp.concatenate(parts, axis=1)


def _padded_to_shards(g):
    offsets, off = {}, 0
    for first, last, pad in PADDED_SEGMENTS:
        offsets[first] = off
        off += last - first + pad
    shards = []
    for j in range(N_CHIPS):
        parts = []
        for first, last, _ in sorted(PADDED_SEGMENTS):
            lo, hi = max(first, j * SHARD_W), min(last, (j + 1) * SHARD_W)
            if lo < hi:
                parts.append(g[:, offsets[first] + lo - first:offsets[first] + hi - first])
        shards.append(jnp.concatenate(parts, axis=1))
    return jnp.stack(shards)


def _sigmoid(x):
    return 1.0 / (1.0 + jnp.exp(-x))


def _silu(x):
    return x * _sigmoid(x)


def _dsilu(x):
    s = _sigmoid(x)
    return s + x * s * (1.0 - s)


def _softplus(x):
    return jnp.maximum(x, 0.0) + jnp.log(1.0 + jnp.exp(-jnp.abs(x)))


def _logsigmoid(x):
    return jnp.minimum(x, 0.0) - jnp.log(1.0 + jnp.exp(-jnp.abs(x)))


def _dot(a, b, dims, precision=None):
    return lax.dot_general(a, b, (dims, ((), ())), preferred_element_type=F32, precision=precision)


def _nn(a, b, precision=None):
    return _dot(a, b, ((1,), (0,)), precision)


def _nt(a, b, precision=None):
    return _dot(a, b, ((1,), (1,)), precision)


def _tn(a, b, precision=None):
    return _dot(a, b, ((0,), (0,)), precision)


def _bf(x):
    return x.astype(BF16)


def _split(x, terms):
    parts = []
    for _ in range(terms):
        part = _bf(x)
        parts.append(part)
        x = x - part.astype(F32)
    return parts


def _pick_right(x, pick, terms):
    pick = _bf(pick)
    out = None
    for part in _split(x, terms):
        out = _nn(part, pick) if out is None else out + _nn(part, pick)
    return out


def _pick_left(pick, x, terms):
    pick = _bf(pick)
    out = None
    for part in _split(x, terms):
        out = _nn(pick, part) if out is None else out + _nn(pick, part)
    return out


def _lane_col(x, lane):
    idx = lax.broadcasted_iota(jnp.int32, x.shape, 1)
    return jnp.sum(jnp.where(idx == lane, x, 0.0), axis=1, keepdims=True)


def _tri(n, upper):
    r = lax.broadcasted_iota(jnp.int32, (n, n), 0)
    c = lax.broadcasted_iota(jnp.int32, (n, n), 1)
    return jnp.where((r <= c) if upper else (r >= c), 1.0, 0.0).astype(F32)


def _eye(n):
    return jnp.where(lax.broadcasted_iota(jnp.int32, (n, n), 0) == lax.broadcasted_iota(jnp.int32, (n, n), 1), 1.0, 0.0)


def _sum_all(x):
    return jnp.sum(jnp.sum(x, axis=1, keepdims=True), axis=0, keepdims=True)


def _crossing(p):
    L = p.shape[0]
    below = _nn(_bf(_tri(L, True)), _bf(p))
    strict = lax.broadcasted_iota(jnp.int32, (L, L), 0) > lax.broadcasted_iota(jnp.int32, (L, L), 1)
    return [jnp.sum(jnp.where(strict, below[:, b * L:(b + 1) * L], 0.0), axis=1, keepdims=True)
            for b in range(p.shape[1] // L)]


def _matmul_bias(a, w, bias, tm, tn, col0, ncols, dtype):
    m, k = a.shape
    j0 = col0 // tn

    def body(a_ref, w_ref, b_ref, o_ref):
        o_ref[...] = (_nn(a_ref[...], w_ref[...]) + b_ref[...]).astype(dtype)

    return pl.pallas_call(
        body, name="matmul_bias", grid=(m // tm, ncols // tn),
        in_specs=[pl.BlockSpec((tm, k), lambda i, j: (i, 0)), pl.BlockSpec((k, tn), lambda i, j: (0, j0 + j)),
                  pl.BlockSpec((1, tn), lambda i, j: (0, j0 + j))],
        out_specs=pl.BlockSpec((tm, tn), lambda i, j: (i, j)),
        out_shape=jax.ShapeDtypeStruct((m, ncols), dtype),
        compiler_params=_cparams("parallel", "arbitrary"))(a, w, bias)


def _matmul_nt(a, w, tm, tk, after=None):
    m, n = a.shape
    k = w.shape[0]

    def body(a_ref, w_ref, *rest):
        o_ref = rest[-1]

        @pl.when(pl.program_id(1) == 0)
        def _():
            o_ref[...] = jnp.zeros_like(o_ref)
        o_ref[...] += _nt(a_ref[...], w_ref[...])

    extra = [] if after is None else [after]
    return pl.pallas_call(
        body, name="matmul_nt", grid=(m // tm, n // tk),
        in_specs=[pl.BlockSpec((tm, tk), lambda i, j: (i, j)), pl.BlockSpec((k, tk), lambda i, j: (0, j))]
        + [pl.BlockSpec(memory_space=pl.ANY)] * len(extra),
        out_specs=pl.BlockSpec((tm, k), lambda i, j: (i, 0)),
        out_shape=jax.ShapeDtypeStruct((m, k), F32),
        compiler_params=_cparams("parallel", "arbitrary"))(a, w, *extra)


def _matmul_tn(a, b, tm, tn, with_colsum=False, a_is_transposed=False):
    k, m = a.shape if a_is_transposed else a.shape[::-1]
    n = b.shape[1]

    def body(a_ref, b_ref, o_ref, *rest):
        first = pl.program_id(1) == 0

        @pl.when(first)
        def _():
            o_ref[...] = jnp.zeros_like(o_ref)
        o_ref[...] += _nn(a_ref[...], b_ref[...]) if a_is_transposed else _tn(a_ref[...], b_ref[...])
        if with_colsum:
            s_ref = rest[0]

            @pl.when(first)
            def _():
                s_ref[...] = jnp.zeros_like(s_ref)
            s_ref[...] += jnp.sum(b_ref[...].astype(F32), axis=0, keepdims=True)

    out_specs = [pl.BlockSpec((k, tn), lambda j, i: (0, j))]
    out_shape = [jax.ShapeDtypeStruct((k, n), F32)]
    if with_colsum:
        out_specs.append(pl.BlockSpec((1, tn), lambda j, i: (0, j)))
        out_shape.append(jax.ShapeDtypeStruct((1, n), F32))
    out = pl.pallas_call(
        body, name="matmul_tn", grid=(n // tn, m // tm),
        in_specs=[pl.BlockSpec((k, tm), lambda j, i: (0, i)) if a_is_transposed else pl.BlockSpec((tm, k), lambda j, i: (i, 0)),
                  pl.BlockSpec((tm, tn), lambda j, i: (i, j))],
        out_specs=out_specs, out_shape=out_shape,
        compiler_params=_cparams("parallel", "arbitrary"))(a, b)
    return out if with_colsum else out[0]


def _ada_fwd(c_all, ada_w, ada_b):
    def body(c_ref, w_ref, b_ref, o_ref):
        o_ref[...] = _nn(_bf(_silu(c_ref[...])), _bf(w_ref[...])) + b_ref[...]

    return pl.pallas_call(body, name="ada_fwd", out_shape=jax.ShapeDtypeStruct((c_all.shape[0], ada_w.shape[1]), F32),
                          compiler_params=_cparams())(c_all, ada_w, ada_b)


def _ada_bwd(c_all, dmod):
    def body(c_ref, d_ref, o_ref):
        o_ref[...] = _tn(_bf(_silu(c_ref[...])), _bf(d_ref[...]))

    return pl.pallas_call(body, name="ada_bwd", out_shape=jax.ShapeDtypeStruct((c_all.shape[1], dmod.shape[1]), F32),
                          compiler_params=_cparams())(c_all, dmod)


def _prenorm_fwd(x, norm_w, scale, shift, ts):
    s, d = x.shape

    def body(x_ref, nw_ref, sc_ref, sh_ref, u_ref):
        xv = x_ref[...]
        r = lax.rsqrt(jnp.mean(xv * xv, axis=1, keepdims=True) + EPS)
        u_ref[...] = _bf(xv * r * nw_ref[...] * (1.0 + sc_ref[...]) + sh_ref[...])

    row = pl.BlockSpec((1, d), lambda i: (0, 0))
    return pl.pallas_call(
        body, name="prenorm_fwd", grid=(s // ts,),
        in_specs=[pl.BlockSpec((ts, d), lambda i: (i, 0)), row, row, row],
        out_specs=pl.BlockSpec((ts, d), lambda i: (i, 0)), out_shape=jax.ShapeDtypeStruct((s, d), BF16),
        compiler_params=_cparams("parallel"))(x, norm_w, scale, shift)


def _prenorm_bwd(du, x, dxres, norm_w, scale, ts):
    s, d = x.shape

    def body(du_ref, x_ref, dr_ref, nw_ref, sc_ref, gx_ref, acc_ref):
        @pl.when(pl.program_id(0) == 0)
        def _():
            acc_ref[...] = jnp.zeros_like(acc_ref)
        xv, duv = x_ref[...], du_ref[...]
        r = lax.rsqrt(jnp.mean(xv * xv, axis=1, keepdims=True) + EPS)
        xn = xv * r
        nw, sc1 = nw_ref[...], 1.0 + sc_ref[...]
        dxn = duv * (nw * sc1)
        gx_ref[...] = r * (dxn - xn * jnp.mean(dxn * xn, axis=1, keepdims=True)) + dr_ref[...]
        t = duv * xn
        acc_ref[0:1, :] += jnp.sum(t, axis=0, keepdims=True) * sc1
        acc_ref[1:2, :] += jnp.sum(t, axis=0, keepdims=True) * nw
        acc_ref[2:3, :] += jnp.sum(duv, axis=0, keepdims=True)

    tile = pl.BlockSpec((ts, d), lambda i: (i, 0))
    row = pl.BlockSpec((1, d), lambda i: (0, 0))
    return pl.pallas_call(
        body, name="prenorm_bwd", grid=(s // ts,),
        in_specs=[tile, tile, tile, row, row],
        out_specs=[tile, pl.BlockSpec((8, d), lambda i: (0, 0))],
        out_shape=[jax.ShapeDtypeStruct((s, d), F32), jax.ShapeDtypeStruct((8, d), F32)],
        compiler_params=_cparams("arbitrary"))(du, x, dxres, norm_w, scale)


CONV_CB = 512


def _conv_taps(buf_ref, ts):
    return [buf_ref[pl.ds(8 - (CONV_K - 1) + j, ts), :] for j in range(CONV_K)]


def _conv_fwd(proj, col0, width, w8, b, ts):
    s = proj.shape[0]
    cb = CONV_CB
    nt = s // ts

    def body(x_ref, w_ref, b_ref, o_ref, ds_ref, buf_ref):
        @pl.when(pl.program_id(1) == 0)
        def _():
            buf_ref[0:8, :] = jnp.zeros((8, cb), F32)
        buf_ref[pl.ds(8, ts), :] = x_ref[...].astype(F32)
        acc = b_ref[...] + jnp.zeros((ts, cb), F32)
        for j, tap in enumerate(_conv_taps(buf_ref, ts)):
            acc = acc + tap * w_ref[j:j + 1, :]
        sg = _sigmoid(acc)
        o_ref[...] = acc * sg
        ds_ref[...] = _bf(sg + acc * sg * (1.0 - sg))
        buf_ref[0:8, :] = buf_ref[pl.ds(ts, 8), :]

    c0 = col0 // cb
    tile = pl.BlockSpec((ts, cb), lambda c, i: (i, c))
    return pl.pallas_call(
        body, name="conv_fwd", grid=(width // cb, nt),
        in_specs=[pl.BlockSpec((ts, cb), lambda c, i: (i, c0 + c)), pl.BlockSpec((8, cb), lambda c, i: (0, c)),
                  pl.BlockSpec((1, cb), lambda c, i: (0, c))],
        out_specs=[tile, tile],
        out_shape=[jax.ShapeDtypeStruct((s, width), F32), jax.ShapeDtypeStruct((s, width), BF16)],
        scratch_shapes=[pltpu.VMEM((ts + 8, cb), F32)],
        compiler_params=_cparams("parallel", "arbitrary"))(proj, w8, b)


def _conv_bwd(proj, col0, width, w8, dact, dpost, dproj, ts):
    s = proj.shape[0]
    cb = CONV_CB
    nt = s // ts
    c0 = col0 // cb

    def body(x_ref, da_ref, dp_ref, w_ref, _, dx_ref, acc_ref, dbuf_ref):
        @pl.when(pl.program_id(1) == 0)
        def _():
            acc_ref[...] = jnp.zeros_like(acc_ref)
            dbuf_ref[pl.ds(ts, 8), :] = jnp.zeros((8, cb), F32)
        dconv = dp_ref[...].astype(F32) * da_ref[...].astype(F32)
        acc_ref[CONV_K:CONV_K + 1, :] += jnp.sum(dconv, axis=0, keepdims=True)
        dbuf_ref[pl.ds(0, ts), :] = dconv
        xv = x_ref[...].astype(F32)
        dx = jnp.zeros((ts, cb), F32)
        for j in range(CONV_K):
            shifted = dbuf_ref[pl.ds(CONV_K - 1 - j, ts), :]
            dx = dx + shifted * w_ref[j:j + 1, :]
            acc_ref[j:j + 1, :] += jnp.sum(xv * shifted, axis=0, keepdims=True)
        dx_ref[...] = _bf(dx)
        dbuf_ref[pl.ds(ts, 8), :] = dconv[0:8, :]

    tile = pl.BlockSpec((ts, cb), lambda c, i: (nt - 1 - i, c))
    wide = pl.BlockSpec((ts, cb), lambda c, i: (nt - 1 - i, c0 + c))
    return pl.pallas_call(
        body, name="conv_bwd", grid=(width // cb, nt),
        in_specs=[wide, tile, tile, pl.BlockSpec((8, cb), lambda c, i: (0, c)), pl.BlockSpec(memory_space=pl.ANY)],
        out_specs=[wide, pl.BlockSpec((8, cb), lambda c, i: (0, c))],
        out_shape=[jax.ShapeDtypeStruct(dproj.shape, dproj.dtype), jax.ShapeDtypeStruct((8, width), F32)],
        input_output_aliases={4: 0},
        scratch_shapes=[pltpu.VMEM((ts + 8, cb), F32)],
        compiler_params=_cparams("parallel", "arbitrary"))(proj, dact, dpost, w8, dproj)


def _mlstm_gates(gif_ref, gt_ref, a_scr, at_scr):
    L = gif_ref.shape[0]
    fb = _logsigmoid(gif_ref[...])
    a_scr[...] = _pick_left(_tri(L, False), fb, 3)
    at_scr[...] = _pick_right(_logsigmoid(gt_ref[...]), _tri(L, True), 3)
    return jnp.sum(fb, axis=0, keepdims=True)


def _mlstm_head(h, qk_ref, v_ref, gif, gt_ref, a, at_scr, a_last_row, c_mat, n_row, m_prev):
    L = gif.shape[0]
    q = qk_ref[:, h * ML_DQK:(h + 1) * ML_DQK] * (ML_DQK ** -0.5)
    k = qk_ref[:, (ML_HEADS + h) * ML_DQK:(ML_HEADS + h + 1) * ML_DQK]
    v = v_ref[:, h * ML_DV:(h + 1) * ML_DV]
    i_col, a_col = _lane_col(gif, h), _lane_col(a, ML_HEADS + h)
    i_row, a_row = gt_ref[h:h + 1, :], at_scr[ML_HEADS + h:ML_HEADS + h + 1, :]
    causal = lax.broadcasted_iota(jnp.int32, (L, L), 0) >= lax.broadcasted_iota(jnp.int32, (L, L), 1)
    dmat = jnp.where(causal, a_col - a_row + i_row, NEG)
    inter = a_col + m_prev
    m_t = jnp.maximum(inter, jnp.max(dmat, axis=1, keepdims=True))
    w_intra = jnp.exp(dmat - m_t)
    w_inter = jnp.exp(inter - m_t)
    sc = _nt(_bf(q), _bf(k)) * w_intra
    den = jnp.sum(sc, axis=1, keepdims=True) + w_inter * jnp.sum(q * n_row, axis=1, keepdims=True)
    floor = jnp.exp(-m_t)
    a_last = _lane_col(a_last_row, ML_HEADS + h)
    g = a_last - a_col + i_col
    m_new = jnp.maximum(a_last + m_prev, jnp.max(g, axis=0, keepdims=True))
    wk = jnp.exp(g - m_new)
    decay = jnp.exp(a_last + m_prev - m_new)
    return dict(q=q, k=k, v=v, w_intra=w_intra, w_inter=w_inter, sc=sc, den=den, floor=floor, m_new=m_new, wk=wk,
                decay=decay)


def _state_tile(n_row, m11):
    r = lax.broadcasted_iota(jnp.int32, (8, LANE), 0)
    return jnp.where(r == 0, n_row, jnp.where(r == 1, m11, 0.0))


def _mlstm_fwd(qk, proj, gates, gt):
    s = qk.shape[0]
    L = CHUNK
    nc = s // L

    def body(qk_ref, v_ref, gif_ref, gt_ref, h_ref, cst_ref, nm_ref, c_scr, nm_scr, a_scr, at_scr):
        @pl.when(pl.program_id(0) == 0)
        def _():
            c_scr[...] = jnp.zeros_like(c_scr)
            nm_scr[...] = jnp.zeros_like(nm_scr)
        a_last_row = _mlstm_gates(gif_ref, gt_ref, a_scr, at_scr)
        gif, a = gif_ref[...], a_scr[...]
        for h in range(ML_HEADS):
            c_mat, n_row = c_scr[h], nm_scr[h, 0:1, :]
            m_prev = jnp.max(nm_scr[h, 1:2, :], axis=1, keepdims=True)
            cst_ref[0, h] = c_mat
            nm_ref[0, h] = nm_scr[h]
            t = _mlstm_head(h, qk_ref, v_ref, gif, gt_ref, a, at_scr, a_last_row, c_mat, n_row, m_prev)
            num = _nn(_bf(t["sc"]), _bf(t["v"])) + t["w_inter"] * _nn(_bf(t["q"]), _bf(c_mat))
            h_ref[:, h * ML_DV:(h + 1) * ML_DV] = num / jnp.maximum(jnp.abs(t["den"]), t["floor"])
            kw = t["k"] * t["wk"]
            c_scr[h] = t["decay"] * c_mat + _tn(_bf(kw), _bf(t["v"]))
            nm_scr[h] = _state_tile(t["decay"] * n_row + jnp.sum(kw, axis=0, keepdims=True), t["m_new"])

    return pl.pallas_call(
        body, name="mlstm_fwd", grid=(nc,),
        in_specs=[pl.BlockSpec((L, 2048), lambda c: (c, 0)), pl.BlockSpec((L, 2048), lambda c: (c, O_V // 2048)),
                  pl.BlockSpec((L, LANE), lambda c: (c, 0)), pl.BlockSpec((LANE, L), lambda c: (0, c))],
        out_specs=[pl.BlockSpec((L, 2048), lambda c: (c, 0)),
                   pl.BlockSpec((1, ML_HEADS, ML_DQK, ML_DV), lambda c: (c, 0, 0, 0)),
                   pl.BlockSpec((1, ML_HEADS, 8, LANE), lambda c: (c, 0, 0, 0))],
        out_shape=[jax.ShapeDtypeStruct((s, 2048), F32), jax.ShapeDtypeStruct((nc, ML_HEADS, ML_DQK, ML_DV), F32),
                   jax.ShapeDtypeStruct((nc, ML_HEADS, 8, LANE), F32)],
        scratch_shapes=[pltpu.VMEM((ML_HEADS, ML_DQK, ML_DV), F32), pltpu.VMEM((ML_HEADS, 8, LANE), F32),
                        pltpu.VMEM((L, LANE), F32), pltpu.VMEM((LANE, L), F32)],
        compiler_params=_cparams("arbitrary"))(qk, proj, gates, gt)


def _mlstm_bwd(qk, proj, gates, gt, hout, dh, cst, nm, dproj):
    s = qk.shape[0]
    L = CHUNK
    nc = s // L

    def body(qk_ref, v_ref, gif_ref, gt_ref, h_ref, dh_ref, cst_ref, nm_ref, _, dqk_ref, dv_ref, dif_ref,
             dc_scr, dn_scr, a_scr, at_scr):
        @pl.when(pl.program_id(0) == 0)
        def _():
            dc_scr[...] = jnp.zeros_like(dc_scr)
            dn_scr[...] = jnp.zeros_like(dn_scr)
        a_last_row = _mlstm_gates(gif_ref, gt_ref, a_scr, at_scr)
        gif, a = gif_ref[...], a_scr[...]
        lane = lax.broadcasted_iota(jnp.int32, (L, LANE), 1)
        last = lax.broadcasted_iota(jnp.int32, (L, 1), 0) == L - 1
        di_tile = jnp.zeros((L, LANE), F32)
        cross = [jnp.zeros((L, LANE), F32)] * 3
        for h in range(ML_HEADS):
            c_mat, n_row = cst_ref[0, h], nm_ref[0, h, 0:1, :]
            m_prev = jnp.max(nm_ref[0, h, 1:2, :], axis=1, keepdims=True)
            t = _mlstm_head(h, qk_ref, v_ref, gif, gt_ref, a, at_scr, a_last_row, c_mat, n_row, m_prev)
            q, k, v, den = t["q"], t["k"], t["v"], t["den"]
            dhh = dh_ref[:, h * ML_DV:(h + 1) * ML_DV].astype(F32)
            hh = h_ref[:, h * ML_DV:(h + 1) * ML_DV]
            dnorm = jnp.maximum(jnp.abs(den), t["floor"])
            dnum = dhh / dnorm
            d_dn = -jnp.sum(dhh * hh, axis=1, keepdims=True) / dnorm
            dden = jnp.where(jnp.abs(den) >= t["floor"], jnp.where(den >= 0.0, d_dn, -d_dn), 0.0)
            dsc = _nt(_bf(dnum), _bf(v)) + dden
            ds = dsc * t["w_intra"]
            dq_inter = t["w_inter"] * (_nt(_bf(dnum), _bf(c_mat)) + dden * n_row)
            dq = _nn(_bf(ds), _bf(k)) + dq_inter
            dc, dn_row = dc_scr[h], dn_scr[h, 0:1, :]
            dk_state = t["wk"] * (_nt(_bf(v), _bf(dc)) + dn_row)
            dk = _tn(_bf(ds), _bf(q)) + dk_state
            dv = _tn(_bf(t["sc"]), _bf(dnum)) + t["wk"] * _nn(_bf(k), _bf(dc))
            qi = q * t["w_inter"]
            dc_scr[h] = t["decay"] * dc + _tn(_bf(qi), _bf(dnum))
            dn_scr[h] = jnp.broadcast_to(t["decay"] * dn_row + jnp.sum(qi * dden, axis=0, keepdims=True), (8, LANE))
            dqk_ref[:, h * ML_DQK:(h + 1) * ML_DQK] = _bf(dq * (ML_DQK ** -0.5))
            dqk_ref[:, (ML_HEADS + h) * ML_DQK:(ML_HEADS + h + 1) * ML_DQK] = _bf(dk)
            dv_ref[:, h * ML_DV:(h + 1) * ML_DV] = _bf(dv)
            di_tile = di_tile + jnp.where(lane == h, jnp.sum(k * dk, axis=1, keepdims=True), 0.0)
            carried = t["decay"] * (_sum_all(dc * c_mat) + jnp.sum(dn_row * n_row, axis=1, keepdims=True))
            parts = (_crossing(dsc * t["sc"])[0],
                     jnp.sum(q * dq_inter, axis=1, keepdims=True) + jnp.where(last, carried, 0.0),
                     jnp.sum(k * dk_state, axis=1, keepdims=True))
            cross = [c + jnp.where(lane == ML_HEADS + h, p, 0.0) for c, p in zip(cross, parts)]
        dfb = cross[0] + _pick_left(_tri(L, True), cross[1], 2) + _pick_left(_tri(L, False) - _eye(L), cross[2], 2)
        dif_ref[:, 0:LANE] = _bf(di_tile + dfb * _sigmoid(-gif))
        dif_ref[:, LANE:SMALL_W] = jnp.zeros((L, SMALL_W - LANE), BF16)

    rev = lambda c: nc - 1 - c
    return pl.pallas_call(
        body, name="mlstm_bwd", grid=(nc,),
        in_specs=[pl.BlockSpec((L, 2048), lambda c: (rev(c), 0)), pl.BlockSpec((L, 2048), lambda c: (rev(c), O_V // 2048)),
                  pl.BlockSpec((L, LANE), lambda c: (rev(c), 0)), pl.BlockSpec((LANE, L), lambda c: (0, rev(c))),
                  pl.BlockSpec((L, 2048), lambda c: (rev(c), 0)), pl.BlockSpec((L, 2048), lambda c: (rev(c), 0)),
                  pl.BlockSpec((1, ML_HEADS, ML_DQK, ML_DV), lambda c: (rev(c), 0, 0, 0)),
                  pl.BlockSpec((1, ML_HEADS, 8, LANE), lambda c: (rev(c), 0, 0, 0)), pl.BlockSpec(memory_space=pl.ANY)],
        out_specs=[pl.BlockSpec((L, 2048), lambda c: (rev(c), 0)), pl.BlockSpec((L, 2048), lambda c: (rev(c), O_V // 2048)),
                   pl.BlockSpec((L, SMALL_W), lambda c: (rev(c), 0))],
        out_shape=[jax.ShapeDtypeStruct((s, 2048), BF16), jax.ShapeDtypeStruct(dproj.shape, dproj.dtype),
                   jax.ShapeDtypeStruct((s, SMALL_W), BF16)],
        input_output_aliases={8: 1},
        scratch_shapes=[pltpu.VMEM((ML_HEADS, ML_DQK, ML_DV), F32), pltpu.VMEM((ML_HEADS, 8, LANE), F32),
                        pltpu.VMEM((L, LANE), F32), pltpu.VMEM((LANE, L), F32)],
        compiler_params=_cparams("arbitrary"))(qk, proj, gates, gt, hout, dh, cst, nm, dproj)


GROUP_W = SSM_HEADS // SSM_GROUPS * SSM_HEADDIM
O_B = SSM_HEADS * SSM_HEADDIM
O_C = O_B + SSM_GROUPS * SSM_STATE


def _head_expand():
    r = jnp.arange(LANE)[:, None]
    c = jnp.arange(SSM_HEADS * SSM_HEADDIM)[None, :] // SSM_HEADDIM
    return (r == c).astype(F32)


def _ssd_gates(dt_ref, dtt_ref, alog_row_ref, alog_col_ref, at_scr):
    L = dt_ref.shape[0]
    dt = _softplus(dt_ref[...])
    acoef = -jnp.exp(alog_row_ref[...])
    a = _pick_left(_tri(L, False), dt * acoef, 3)
    at_scr[...] = _pick_right(_softplus(dtt_ref[...]) * (-jnp.exp(alog_col_ref[...])), _tri(L, True), 3)
    return dt, acoef, a


def _ssd_group(g, xbc_ref, dt, a, e_ref, ax_scr):
    eg = e_ref[:, g * GROUP_W:(g + 1) * GROUP_W]
    ax_scr[...] = _pick_right(a, eg, 3)
    ax = ax_scr[...]
    alx = ax_scr[ax.shape[0] - 1:ax.shape[0], :]
    dtx = _pick_right(dt, eg, 2)
    xg = xbc_ref[:, g * GROUP_W:(g + 1) * GROUP_W]
    bg = xbc_ref[:, O_B + g * SSM_STATE:O_B + (g + 1) * SSM_STATE]
    cg = xbc_ref[:, O_C + g * SSM_STATE:O_C + (g + 1) * SSM_STATE]
    return dict(ax=ax, alx=alx, dtx=dtx, xg=xg, bg=bg, cg=cg, xdt=xg * dtx, gmat=_nt(_bf(cg), _bf(bg)))


def _ssd_decay(hh, a, at_scr):
    L = a.shape[0]
    causal = lax.broadcasted_iota(jnp.int32, (L, L), 0) >= lax.broadcasted_iota(jnp.int32, (L, L), 1)
    return jnp.exp(jnp.where(causal, _lane_col(a, hh) - at_scr[hh:hh + 1, :], NEG))


def _ssd_fwd(xbc, gates, dtt, alog_row, alog_col, dskip_x, expand):
    s = xbc.shape[0]
    L = CHUNK
    nc = s // L
    half = SSM_HEADDIM

    def body(xbc_ref, dt_ref, dtt_ref, ar_ref, ac_ref, dk_ref, e_ref, y_ref, st_ref, st_scr, at_scr, ax_scr):
        @pl.when(pl.program_id(0) == 0)
        def _():
            st_scr[...] = jnp.zeros_like(st_scr)
        dt, _, a = _ssd_gates(dt_ref, dtt_ref, ar_ref, ac_ref, at_scr)
        lane = lax.broadcasted_iota(jnp.int32, (L, LANE), 1)
        for g in range(SSM_GROUPS):
            t = _ssd_group(g, xbc_ref, dt, a, e_ref, ax_scr)
            st = st_scr[g]
            st_ref[0, g] = st
            pairs = []
            for j in range(GROUP_W // LANE):
                xp = _bf(t["xdt"][:, j * LANE:(j + 1) * LANE])
                hh = g * (SSM_HEADS // SSM_GROUPS) + 2 * j
                both = jnp.concatenate([_bf(t["gmat"] * _ssd_decay(hh, a, at_scr)),
                                        _bf(t["gmat"] * _ssd_decay(hh + 1, a, at_scr))], axis=0)
                ys = _nn(both, xp)
                pairs.append(jnp.where(lane < half, ys[0:L], ys[L:2 * L]))
            y = jnp.concatenate(pairs, axis=1) + _nn(_bf(t["cg"]), _bf(st)) * jnp.exp(t["ax"])
            y_ref[:, g * GROUP_W:(g + 1) * GROUP_W] = y + dk_ref[:, g * GROUP_W:(g + 1) * GROUP_W] * t["xg"]
            wts = jnp.exp(t["alx"] - t["ax"])
            st_scr[g] = jnp.exp(t["alx"]) * st + _tn(_bf(t["bg"]), _bf(t["xdt"] * wts))

    row = lambda w: pl.BlockSpec((1, w), lambda c: (0, 0))
    return pl.pallas_call(
        body, name="ssd_fwd", grid=(nc,),
        in_specs=[pl.BlockSpec((L, 3072), lambda c: (c, 0)), pl.BlockSpec((L, LANE), lambda c: (c, (O_DT - O_IF) // LANE)),
                  pl.BlockSpec((LANE, L), lambda c: (0, c)), row(LANE), pl.BlockSpec((LANE, 1), lambda c: (0, 0)),
                  row(2048), pl.BlockSpec((LANE, 2048), lambda c: (0, 0))],
        out_specs=[pl.BlockSpec((L, 2048), lambda c: (c, 0)),
                   pl.BlockSpec((1, SSM_GROUPS, SSM_STATE, GROUP_W), lambda c: (c, 0, 0, 0))],
        out_shape=[jax.ShapeDtypeStruct((s, 2048), F32),
                   jax.ShapeDtypeStruct((nc, SSM_GROUPS, SSM_STATE, GROUP_W), F32)],
        scratch_shapes=[pltpu.VMEM((SSM_GROUPS, SSM_STATE, GROUP_W), F32), pltpu.VMEM((LANE, L), F32),
                        pltpu.VMEM((L, GROUP_W), F32)],
        compiler_params=_cparams("arbitrary"))(xbc, gates, dtt, alog_row, alog_col, dskip_x, expand)


def _ssd_bwd(xbc, gates, dtt, alog_row, alog_col, dskip_x, expand, expand_t, dy, states):
    s = xbc.shape[0]
    L = CHUNK
    nc = s // L
    half = SSM_HEADDIM

    def body(xbc_ref, dt_ref, dtt_ref, ar_ref, ac_ref, dk_ref, e_ref, et_ref, dy_ref, st_ref,
             dxbc_ref, ddt_ref, accd_ref, acca_ref, dst_scr, at_scr, ax_scr):
        @pl.when(pl.program_id(0) == 0)
        def _():
            dst_scr[...] = jnp.zeros_like(dst_scr)
            accd_ref[...] = jnp.zeros_like(accd_ref)
            acca_ref[...] = jnp.zeros_like(acca_ref)
        dt, acoef, a = _ssd_gates(dt_ref, dtt_ref, ar_ref, ac_ref, at_scr)
        lane = lax.broadcasted_iota(jnp.int32, (L, LANE), 1)
        low = lane < half
        last = lax.broadcasted_iota(jnp.int32, (L, 1), 0) == L - 1
        cross = [jnp.zeros((L, LANE), F32)] * 3
        ddt_tile = jnp.zeros((L, LANE), F32)
        for g in range(SSM_GROUPS):
            t = _ssd_group(g, xbc_ref, dt, a, e_ref, ax_scr)
            xg, bg, cg, xdt, gmat = t["xg"], t["bg"], t["cg"], t["xdt"], t["gmat"]
            st, dst = st_ref[0, g], dst_scr[g]
            dyg = dy_ref[:, g * GROUP_W:(g + 1) * GROUP_W].astype(F32)
            ea, eal = jnp.exp(t["ax"]), jnp.exp(t["alx"])
            wts = jnp.exp(t["alx"] - t["ax"])
            dyi = dyg * ea
            y_inter = _nn(_bf(cg), _bf(st)) * ea
            dc = _nt(_bf(dyi), _bf(st))
            d_xdt_state = _nn(_bf(bg), _bf(dst)) * wts
            db = _nt(_bf(xdt * wts), _bf(dst))
            dst_scr[g] = eal * dst + _tn(_bf(cg), _bf(dyi))
            dg = jnp.zeros((L, L), F32)
            dx_pairs = []
            for j in range(GROUP_W // LANE):
                xp = _bf(xdt[:, j * LANE:(j + 1) * LANE])
                dyp = dyg[:, j * LANE:(j + 1) * LANE]
                hh = g * (SSM_HEADS // SSM_GROUPS) + 2 * j
                decs = [_ssd_decay(hh, a, at_scr), _ssd_decay(hh + 1, a, at_scr)]
                ws = [gmat * decs[0], gmat * decs[1]]
                dxs = _tn(_bf(jnp.concatenate(ws, axis=1)), _bf(dyp))
                dws = _nt(_bf(jnp.concatenate([jnp.where(low, dyp, 0.0), jnp.where(low, 0.0, dyp)], axis=0)), xp)
                dw0, dw1 = dws[0:L], dws[L:2 * L]
                dg = dg + dw0 * decs[0] + dw1 * decs[1]
                c0, c1 = _crossing(jnp.concatenate([dw0 * ws[0], dw1 * ws[1]], axis=1))
                cross[0] = cross[0] + jnp.where(lane == hh, c0, 0.0) + jnp.where(lane == hh + 1, c1, 0.0)
                dx_pairs.append(jnp.where(low, dxs[0:L], dxs[L:2 * L]))
            d_xdt = d_xdt_state + jnp.concatenate(dx_pairs, axis=1)
            dc = dc + _nn(_bf(dg), _bf(bg))
            db = db + _tn(_bf(dg), _bf(cg))
            etg = et_ref[g * GROUP_W:(g + 1) * GROUP_W, :]
            carried = jnp.sum(dst * st, axis=0, keepdims=True) * eal
            cross[1] = cross[1] + _pick_right(dyg * y_inter + jnp.where(last, carried, 0.0), etg, 2)
            cross[2] = cross[2] + _pick_right(xdt * d_xdt_state, etg, 2)
            ddt_tile = ddt_tile + _pick_right(d_xdt * xg, etg, 2)
            dxbc_ref[:, g * GROUP_W:(g + 1) * GROUP_W] = _bf(d_xdt * t["dtx"] + dk_ref[:, g * GROUP_W:(g + 1) * GROUP_W] * dyg)
            dxbc_ref[:, O_B + g * SSM_STATE:O_B + (g + 1) * SSM_STATE] = _bf(db)
            dxbc_ref[:, O_C + g * SSM_STATE:O_C + (g + 1) * SSM_STATE] = _bf(dc)
            accd_ref[0:1, g * GROUP_W:(g + 1) * GROUP_W] += jnp.sum(dyg * xg, axis=0, keepdims=True)
        d_da = cross[0] + _pick_left(_tri(L, True), cross[1], 2) + _pick_left(_tri(L, False) - _eye(L), cross[2], 2)
        acca_ref[0:1, :] += jnp.sum(d_da * dt, axis=0, keepdims=True)
        ddt_ref[:, 0:LANE] = _bf((ddt_tile + d_da * acoef) * _sigmoid(dt_ref[...]))
        ddt_ref[:, LANE:SMALL_W] = jnp.zeros((L, SMALL_W - LANE), BF16)

    rev = lambda c: nc - 1 - c
    row = lambda w: pl.BlockSpec((1, w), lambda c: (0, 0))
    return pl.pallas_call(
        body, name="ssd_bwd", grid=(nc,),
        in_specs=[pl.BlockSpec((L, 3072), lambda c: (rev(c), 0)), pl.BlockSpec((L, LANE), lambda c: (rev(c), (O_DT - O_IF) // LANE)),
                  pl.BlockSpec((LANE, L), lambda c: (0, rev(c))), row(LANE), pl.BlockSpec((LANE, 1), lambda c: (0, 0)),
                  row(2048), pl.BlockSpec((LANE, 2048), lambda c: (0, 0)), pl.BlockSpec((2048, LANE), lambda c: (0, 0)),
                  pl.BlockSpec((L, 2048), lambda c: (rev(c), 0)),
                  pl.BlockSpec((1, SSM_GROUPS, SSM_STATE, GROUP_W), lambda c: (rev(c), 0, 0, 0))],
        out_specs=[pl.BlockSpec((L, 3072), lambda c: (rev(c), 0)), pl.BlockSpec((L, SMALL_W), lambda c: (rev(c), 0)),
                   pl.BlockSpec((8, 2048), lambda c: (0, 0)), pl.BlockSpec((8, LANE), lambda c: (0, 0))],
        out_shape=[jax.ShapeDtypeStruct((s, 3072), BF16), jax.ShapeDtypeStruct((s, SMALL_W), BF16),
                   jax.ShapeDtypeStruct((8, 2048), F32), jax.ShapeDtypeStruct((8, LANE), F32)],
        scratch_shapes=[pltpu.VMEM((SSM_GROUPS, SSM_STATE, GROUP_W), F32),
                        pltpu.VMEM((LANE, L), F32), pltpu.VMEM((L, GROUP_W), F32)],
        compiler_params=_cparams("arbitrary"))(xbc, gates, dtt, alog_row, alog_col, dskip_x, expand, expand_t, dy, states)


def _group_norm(v, width):
    outs, rs = [], []
    for k in range(v.shape[1] // width):
        blk = v[:, k * width:(k + 1) * width]
        r = lax.rsqrt(jnp.mean(blk * blk, axis=1, keepdims=True) + EPS)
        outs.append(blk * r)
        rs.append(jnp.broadcast_to(r, blk.shape))
    return jnp.concatenate(outs, axis=1), jnp.concatenate(rs, axis=1)


def _group_mean(v, width):
    return jnp.concatenate([jnp.broadcast_to(jnp.mean(v[:, k * width:(k + 1) * width], axis=1, keepdims=True),
                                             (v.shape[0], width)) for k in range(v.shape[1] // width)], axis=1)


def _post_fwd(hm, yssd, proj, ml_norm_w, ssm_norm_w, ts):
    s = hm.shape[0]

    def body(h_ref, ys_ref, o_ref, zm_ref, zs_ref, wm_ref, ws_ref, ym_ref, yso_ref):
        hn, _ = _group_norm(h_ref[...], ML_DV)
        ym_ref[...] = _bf(_sigmoid(o_ref[...].astype(F32)) * hn * wm_ref[...] * _silu(zm_ref[...].astype(F32)))
        pn, _ = _group_norm(ys_ref[...] * _silu(zs_ref[...].astype(F32)), GROUP_W)
        yso_ref[...] = _bf(pn * ws_ref[...])

    tile = pl.BlockSpec((ts, 2048), lambda i: (i, 0))
    col = lambda off: pl.BlockSpec((ts, 2048), lambda i: (i, off // 2048))
    row = pl.BlockSpec((1, 2048), lambda i: (0, 0))
    return pl.pallas_call(
        body, name="post_fwd", grid=(s // ts,),
        in_specs=[tile, tile, col(O_O), col(O_ZM), col(O_ZS), row, row],
        out_specs=[tile, tile],
        out_shape=[jax.ShapeDtypeStruct((s, 2048), BF16)] * 2,
        compiler_params=_cparams("parallel"))(hm, yssd, proj, proj, proj, ml_norm_w, ssm_norm_w)


def _post_bwd(dym, dys, hm, yssd, proj, ml_norm_w, ssm_norm_w, dproj, ts):
    s = hm.shape[0]

    def body(dym_ref, dys_ref, h_ref, ys_ref, o_ref, zm_ref, zs_ref, wm_ref, ws_ref, _,
             dh_ref, dyssd_ref, dp_ref, acc_ref):
        @pl.when(pl.program_id(0) == 0)
        def _():
            acc_ref[...] = jnp.zeros_like(acc_ref)
        hn, r = _group_norm(h_ref[...], ML_DV)
        so, zm, wm, d_ym = _sigmoid(o_ref[...].astype(F32)), zm_ref[...].astype(F32), wm_ref[...], dym_ref[...].astype(F32)
        sz = _silu(zm)
        hnw = hn * wm
        dp_ref[:, O_O:O_O + 2048] = _bf(d_ym * hnw * sz * so * (1.0 - so))
        dp_ref[:, O_ZM:O_ZM + 2048] = _bf(d_ym * so * hnw * _dsilu(zm))
        dhnw = d_ym * so * sz
        acc_ref[0:1, :] += jnp.sum(dhnw * hn, axis=0, keepdims=True)
        dhn = dhnw * wm
        dh_ref[...] = _bf(r * (dhn - hn * _group_mean(dhn * hn, ML_DV)))
        ysv, zs, d_ys = ys_ref[...], zs_ref[...].astype(F32), dys_ref[...].astype(F32)
        szs = _silu(zs)
        pn, r2 = _group_norm(ysv * szs, GROUP_W)
        acc_ref[1:2, :] += jnp.sum(d_ys * pn, axis=0, keepdims=True)
        dpn = d_ys * ws_ref[...]
        dp = r2 * (dpn - pn * _group_mean(dpn * pn, GROUP_W))
        dyssd_ref[...] = _bf(dp * szs)
        dp_ref[:, O_ZS:O_ZS + 2048] = _bf(dp * ysv * _dsilu(zs))

    tile = pl.BlockSpec((ts, 2048), lambda i: (i, 0))
    col = lambda off: pl.BlockSpec((ts, 2048), lambda i: (i, off // 2048))
    row = pl.BlockSpec((1, 2048), lambda i: (0, 0))
    sds = lambda dt: jax.ShapeDtypeStruct((s, 2048), dt)
    return pl.pallas_call(
        body, name="post_bwd", grid=(s // ts,),
        in_specs=[tile, tile, tile, tile, col(O_O), col(O_ZM), col(O_ZS), row, row, pl.BlockSpec(memory_space=pl.ANY)],
        out_specs=[tile, tile, pl.BlockSpec((ts, O_MG), lambda i: (i, 0)), pl.BlockSpec((8, 2048), lambda i: (0, 0))],
        out_shape=[sds(BF16), sds(BF16), jax.ShapeDtypeStruct(dproj.shape, dproj.dtype), jax.ShapeDtypeStruct((8, 2048), F32)],
        input_output_aliases={9: 2},
        compiler_params=_cparams("arbitrary"))(dym, dys, hm, yssd, proj, proj, proj, ml_norm_w, ssm_norm_w, dproj)


def _merge(x, ym, ys, proj, target, gate, final_w, wpm, wps, wo, ts):
    wpm_t, wps_t, wo_t = wpm.T, wps.T, wo.T
    s, d = x.shape

    def body(x_ref, ym_ref, ys_ref, mg_ref, t_ref, gate_ref, fw_ref, wpm_ref, wps_ref, wo_ref, wpmt_ref, wpst_ref, wot_ref,
             dres_ref, mer_ref, dmo_ref, dpm_ref, dps_ref, dym_ref, dys_ref, dmg_ref, acc_ref):
        @pl.when(pl.program_id(0) == 0)
        def _():
            acc_ref[...] = jnp.zeros_like(acc_ref)
        gm, gs = _sigmoid(mg_ref[:, 0:d].astype(F32)), _sigmoid(mg_ref[:, d:2 * d].astype(F32))
        pm = _nn(ym_ref[...], wpm_ref[...])
        ps = _nn(ys_ref[...], wps_ref[...])
        merged = _bf(gm * pm + gs * ps)
        mer_ref[...] = merged
        mo = _nn(merged, wo_ref[...])
        gate, fw = gate_ref[...], fw_ref[...]
        out = x_ref[...] + gate * mo
        r = lax.rsqrt(jnp.mean(out * out, axis=1, keepdims=True) + EPS)
        on = out * r
        diff = on * fw - t_ref[...]
        acc_ref[0:1, :] += jnp.sum(0.5 * jnp.sum(diff * diff, axis=1, keepdims=True) / d, axis=0, keepdims=True)
        dyv = diff * (1.0 / d)
        acc_ref[1:2, :] += jnp.sum(dyv * on, axis=0, keepdims=True)
        don = dyv * fw
        dout = r * (don - on * jnp.mean(don * on, axis=1, keepdims=True))
        dres_ref[...] = dout
        acc_ref[2:3, :] += jnp.sum(dout * mo, axis=0, keepdims=True)
        dmo = _bf(dout * gate)
        dmo_ref[...] = dmo
        dmer = _nn(dmo, wot_ref[...])
        dpm, dps = _bf(dmer * gm), _bf(dmer * gs)
        dpm_ref[...] = dpm
        dps_ref[...] = dps
        dmg_ref[:, 0:d] = _bf(dmer * pm * gm * (1.0 - gm))
        dmg_ref[:, d:2 * d] = _bf(dmer * ps * gs * (1.0 - gs))
        dym_ref[...] = _bf(_nn(dpm, wpmt_ref[...]))
        dys_ref[...] = _bf(_nn(dps, wpst_ref[...]))

    t1 = pl.BlockSpec((ts, d), lambda i: (i, 0))
    t2 = pl.BlockSpec((ts, 2 * d), lambda i: (i, 0))
    row = pl.BlockSpec((1, d), lambda i: (0, 0))
    whole = pl.BlockSpec(memory_space=pltpu.VMEM)
    sd = lambda w, dt: jax.ShapeDtypeStruct((s, w), dt)
    return pl.pallas_call(
        body, name="merge_fwd_bwd", grid=(s // ts,),
        in_specs=[t1, t2, t2, pl.BlockSpec((ts, 2 * d), lambda i: (i, O_MG // (2 * d))), t1, row, row] + [whole] * 6,
        out_specs=[t1, t1, t1, t1, t1, t2, t2, pl.BlockSpec((ts, 2 * d), lambda i: (i, O_MG // (2 * d))),
                   pl.BlockSpec((8, d), lambda i: (0, 0))],
        out_shape=[sd(d, F32), sd(d, BF16), sd(d, BF16), sd(d, BF16), sd(d, BF16), sd(2 * d, BF16), sd(2 * d, BF16),
                   sd(NP, BF16), jax.ShapeDtypeStruct((8, d), F32)],
        compiler_params=_cparams("arbitrary"))(x, ym, ys, proj, target, gate, final_w, wpm, wps, wo, wpm_t, wps_t, wo_t)


def _adamw(w, g, m, v, tr):
    if w.ndim == 2 and w.shape[0] % 8:
        tile, steps = pl.BlockSpec((w.shape[0], tr), lambda i: (0, i)), w.shape[1] // tr
    else:
        lead = (None,) * (w.ndim - 2)
        tile, steps = pl.BlockSpec(lead + (tr, w.shape[-1]), lambda i: (0,) * len(lead) + (i, 0)), w.shape[-2] // tr

    def body(w_ref, g_ref, m_ref, v_ref, d_ref, nm_ref, nv_ref):
        gv = g_ref[...]
        m2 = ADAM_B1 * m_ref[...] + (1.0 - ADAM_B1) * gv
        v2 = ADAM_B2 * v_ref[...] + (1.0 - ADAM_B2) * (gv * gv)
        m_hat = m2 / (1.0 - ADAM_B1 ** ADAM_STEP)
        v_hat = v2 / (1.0 - ADAM_B2 ** ADAM_STEP)
        d_ref[...] = -ADAM_LR * (m_hat / (jnp.sqrt(v_hat) + ADAM_EPS) + ADAM_WD * w_ref[...])
        nm_ref[...] = m2
        nv_ref[...] = v2

    return pl.pallas_call(
        body, name="adamw", grid=(steps,), in_specs=[tile] * 4, out_specs=[tile] * 3,
        out_shape=[jax.ShapeDtypeStruct(w.shape, F32)] * 3,
        compiler_params=_cparams("parallel"))(w, g.reshape(w.shape), m, v)


def _sum_parts(own, parts, tr, dtype=F32):
    p, rows, cols = parts.shape

    def body(*refs):
        p_ref, o_ref = refs[-2], refs[-1]
        acc = p_ref[0].astype(F32) if own is None else refs[0][...].astype(F32) + p_ref[0].astype(F32)
        for i in range(1, p):
            acc = acc + p_ref[i].astype(F32)
        o_ref[...] = acc.astype(dtype)

    tile = pl.BlockSpec((tr, cols), lambda i: (i, 0))
    ins = ([] if own is None else [tile]) + [pl.BlockSpec((p, tr, cols), lambda i: (0, i, 0))]
    args = ([] if own is None else [own]) + [parts]
    return pl.pallas_call(
        body, name="sum_parts", grid=(rows // tr,), in_specs=ins, out_specs=tile,
        out_shape=jax.ShapeDtypeStruct((rows, cols), dtype), compiler_params=_cparams("parallel"))(*args)


def _position():
    return lax.axis_index("x"), lax.axis_index("y"), lax.axis_index("c")


def _flip(pos, k):
    return tuple(1 - p if (k >> s) & 1 else p for p, s in zip(pos, (2, 1, 0)))


def _allgather8(block):
    rows, cols = block.shape

    def body(x_ref, o_ref, send_sems, recv_sems, local_sem):
        pos = _position()
        me = 4 * pos[0] + 2 * pos[1] + pos[2]
        mine = pltpu.make_async_copy(x_ref, o_ref.at[me], local_sem)
        mine.start()
        copies = [pltpu.make_async_remote_copy(src_ref=x_ref, dst_ref=o_ref.at[me], send_sem=send_sems.at[k - 1],
                                               recv_sem=recv_sems.at[k - 1], device_id=_flip(pos, k), device_id_type=MESH)
                  for k in range(1, N_DEV)]
        for cp in copies:
            cp.start()
        for cp in copies:
            cp.wait()
        mine.wait()

    vmem = pl.BlockSpec(memory_space=pltpu.VMEM)
    return pl.pallas_call(
        body, name="allgather8", in_specs=[vmem], out_specs=vmem,
        out_shape=jax.ShapeDtypeStruct((N_DEV, rows, cols), block.dtype),
        scratch_shapes=[pltpu.SemaphoreType.DMA((N_DEV - 1,)), pltpu.SemaphoreType.DMA((N_DEV - 1,)),
                        pltpu.SemaphoreType.DMA],
        compiler_params=pltpu.CompilerParams(vmem_limit_bytes=VMEM_LIMIT))(block)


COPY_BYTES = 1 << 20


def _row_chunks(rows, row_bytes):
    n = max(1, min(rows // 16, -(-rows * row_bytes // COPY_BYTES)))
    while rows % (16 * n):
        n -= 1
    return [(i * (rows // n), rows // n) for i in range(n)]


def _weight_gather(shards):
    n = len(shards)
    pieces = [_row_chunks(a.shape[1], a.shape[2] * a.dtype.itemsize) for a in shards]
    plan = [(a, k, r0, nr) for a in range(n) for k in range(1, N_CHIPS) for r0, nr in pieces[a]]

    def body(*refs):
        ins, outs = refs[:n], refs[n:2 * n]
        ici_send, ici_recv, d2d_send, d2d_recv = refs[2 * n:]
        pos = _position()
        chip, core = 2 * pos[0] + pos[1], pos[2]
        sibling = _flip(pos, 1)
        sent = []
        for i, (a, k, r0, nr) in enumerate(plan):
            cp = pltpu.make_async_remote_copy(
                src_ref=ins[a].at[core, pl.ds(r0, nr)], dst_ref=outs[a].at[chip, core, pl.ds(r0, nr)],
                send_sem=ici_send.at[i], recv_sem=ici_recv.at[i], device_id=_flip(pos, 2 * k), device_id_type=MESH)
            cp.start()
            sent.append(cp)
        passed = []
        for i, (a, k, r0, nr) in enumerate(plan):
            there = _flip(pos, 2 * k)
            landed = outs[a].at[2 * there[0] + there[1], core, pl.ds(r0, nr)]
            sent[i].wait_recv()
            cp = pltpu.make_async_remote_copy(src_ref=landed, dst_ref=landed, send_sem=d2d_send.at[i],
                                              recv_sem=d2d_recv.at[i], device_id=sibling, device_id_type=MESH)
            cp.start()
            passed.append(cp)
        for cp in passed:
            cp.wait()
        for cp in sent:
            cp.wait_send()

    hbm = pl.BlockSpec(memory_space=pl.ANY)
    sems = pltpu.SemaphoreType.DMA((len(plan),))
    return pl.pallas_call(
        body, name="weight_gather", in_specs=[hbm] * n, out_specs=[hbm] * n,
        out_shape=[jax.ShapeDtypeStruct((N_CHIPS,) + a.shape, a.dtype) for a in shards],
        scratch_shapes=[sems, sems, sems, sems],
        compiler_params=pltpu.CompilerParams(has_side_effects=True))(*shards)


def _exchange(name, arrays, out_shapes, plan, n_remote, n_local):
    n, m = len(arrays), len(out_shapes)

    def body(*refs):
        send_sems, recv_sems, local_sems = refs[n + m:]
        remote, local = plan(_position(), refs[:n], refs[n:n + m])
        assert (len(remote), len(local)) == (n_remote, n_local)
        copies = [pltpu.make_async_copy(src, dst, local_sems.at[i]) for i, (src, dst) in enumerate(local)]
        copies += [pltpu.make_async_remote_copy(src_ref=src, dst_ref=dst, send_sem=send_sems.at[i], recv_sem=recv_sems.at[i],
                                                device_id=dev, device_id_type=MESH)
                   for i, (src, dst, dev) in enumerate(remote)]
        for cp in copies:
            cp.start()
        for cp in copies:
            cp.wait()

    hbm = pl.BlockSpec(memory_space=pl.ANY)
    return pl.pallas_call(
        body, name=name, in_specs=[hbm] * n, out_specs=[hbm] * m, out_shape=out_shapes,
        scratch_shapes=[pltpu.SemaphoreType.DMA((n_remote,)), pltpu.SemaphoreType.DMA((n_remote,)),
                        pltpu.SemaphoreType.DMA((max(n_local, 1),))],
        compiler_params=pltpu.CompilerParams(has_side_effects=True))(*arrays)


def _pair_send(slabs):
    n = len(slabs)
    pieces = [_row_chunks(g.shape[2], g.shape[3] * g.dtype.itemsize) for g in slabs]

    def plan(pos, ins, outs):
        return [(ins[a].at[j, 1 - pos[2], pl.ds(r0, nr)], outs[a].at[j, pl.ds(r0, nr)], _flip(pos, 1))
                for a in range(n) for j in range(N_CHIPS) for r0, nr in pieces[a]], []

    return _exchange("pair_send", slabs, [jax.ShapeDtypeStruct((N_CHIPS,) + g.shape[2:], g.dtype) for g in slabs], plan,
                     N_CHIPS * sum(len(p) for p in pieces), 0)


def _chip_scatter_copies(pos, sums, lands, send_sems, recv_sems):
    copies = []
    for a in range(len(sums)):
        for k in range(1, N_CHIPS):
            to = _flip(pos, 2 * k)
            for r0, nr in _row_chunks(sums[a].shape[1], sums[a].shape[2] * sums[a].dtype.itemsize):
                i = len(copies)
                copies.append(pltpu.make_async_remote_copy(
                    src_ref=sums[a].at[2 * to[0] + to[1], pl.ds(r0, nr)], dst_ref=lands[a].at[k - 1, pl.ds(r0, nr)],
                    send_sem=send_sems.at[i], recv_sem=recv_sems.at[i], device_id=to, device_id_type=MESH))
    return copies


def _chip_scatter_start(sums):
    n = len(sums)
    n_copies = (N_CHIPS - 1) * sum(len(_row_chunks(g.shape[1], g.shape[2] * g.dtype.itemsize)) for g in sums)
    lands = [lax.empty((N_CHIPS - 1,) + g.shape[1:], g.dtype) for g in sums]

    def body(*refs):
        send_sems, recv_sems = refs[2 * n], refs[2 * n + 1]
        for cp in _chip_scatter_copies(_position(), refs[:n], refs[n:2 * n], send_sems, recv_sems):
            cp.start()
        refs[-1][...] = jnp.zeros((8, LANE), F32)

    hbm = pl.BlockSpec(memory_space=pltpu.HBM)
    sem = pl.BlockSpec(memory_space=pltpu.SEMAPHORE)
    operands = [pltpu.with_memory_space_constraint(t, pltpu.HBM) for t in list(sums) + lands]
    out = pl.pallas_call(
        body, name="chip_scatter_start", in_specs=[hbm] * (2 * n),
        out_specs=[sem, sem] + [hbm] * (2 * n) + [pl.BlockSpec(memory_space=pltpu.VMEM)],
        out_shape=[pltpu.SemaphoreType.DMA((n_copies,)), pltpu.SemaphoreType.DMA((n_copies,))]
        + [pltpu.HBM(t.shape, t.dtype) for t in operands] + [jax.ShapeDtypeStruct((8, LANE), F32)],
        input_output_aliases={i: 2 + i for i in range(2 * n)},
        compiler_params=pltpu.CompilerParams(has_side_effects=pltpu.SideEffectType.DATAFLOW_SIDE_EFFECTING))(*operands)
    return out[0], out[1], out[2:2 + n], out[2 + n:2 + 2 * n], out[-1]


def _chip_scatter_wait(send_sems, recv_sems, sums, lands, after):
    n = len(sums)

    def body(*refs):
        for cp in _chip_scatter_copies(_position(), refs[:n], refs[n:2 * n], refs[2 * n], refs[2 * n + 1]):
            cp.wait_send()
            cp.wait_recv()

    hbm = pl.BlockSpec(memory_space=pltpu.HBM)
    sem = pl.BlockSpec(memory_space=pltpu.SEMAPHORE)
    out = pl.pallas_call(
        body, name="chip_scatter_wait", in_specs=[hbm] * (2 * n) + [sem, sem, pl.BlockSpec(memory_space=pl.ANY)],
        out_specs=[hbm] * (2 * n), out_shape=[pltpu.HBM(t.shape, t.dtype) for t in list(sums) + list(lands)],
        input_output_aliases={i: i for i in range(2 * n)},
        compiler_params=pltpu.CompilerParams(has_side_effects=pltpu.SideEffectType.DATAFLOW_SIDE_EFFECTING))(
            *sums, *lands, send_sems, recv_sems, after)
    return out[:n], out[n:]


def _pair_exchange(halves):
    n = len(halves)
    pieces = [_row_chunks(h.shape[0], h.shape[1] * h.dtype.itemsize) for h in halves]

    def plan(pos, ins, outs):
        return [(ins[a].at[pl.ds(r0, nr)], outs[a].at[pl.ds(r0, nr)], _flip(pos, 1))
                for a in range(n) for r0, nr in pieces[a]], []

    return _exchange("pair_exchange", halves, [jax.ShapeDtypeStruct(h.shape, h.dtype) for h in halves], plan,
                     sum(len(p) for p in pieces), 0)


def _pack(arrays):
    flat = jnp.concatenate([a.reshape(-1).astype(F32) for a in arrays])
    size = -(-flat.shape[0] // (8 * LANE)) * (8 * LANE)
    return jnp.pad(flat, (0, size - flat.shape[0])).reshape(size // LANE, LANE)


def _unpack(buf, shapes):
    flat = buf.reshape(-1)
    out, off = [], 0
    for shp in shapes:
        n = math.prod(shp)
        out.append(flat[off:off + n].reshape(shp))
        off += n
    return out


def _unpack_rows(bufs, shapes):
    flat = bufs.reshape(bufs.shape[0], -1)
    out, off = [], 0
    for shp in shapes:
        n = math.prod(shp)
        out.append(flat[:, off:off + n].reshape((bufs.shape[0],) + shp))
        off += n
    return out


def _taps8(w):
    return jnp.pad(w, ((0, 8 - CONV_K), (0, 0)))


def _local_step(xs, tgt, scale, shift, gate, norm_w, w_in_p, b_in_p, ml_conv_w, ml_conv_b, ml_norm_w, ssm_conv_w,
                ssm_conv_b, ssm_a_log, ssm_d, ssm_norm_w, wpm, wps, wo, final_w, start_exchange=None):
    s = xs.shape[0]
    ts = min(512, s)
    tm = min(2048, s)
    u = _prenorm_fwd(xs, norm_w, scale, shift, ts)
    proj = _matmul_bias(u, w_in_p, b_in_p, tm, 512, 0, O_IF, BF16)
    gates = _matmul_bias(u, w_in_p, b_in_p, tm, 512, O_IF, NP - O_IF, F32)
    mlw8, ssw8 = _taps8(ml_conv_w), _taps8(ssm_conv_w)
    qk, qk_dact = _conv_fwd(proj, O_QK, 2048, mlw8, ml_conv_b, ts)
    xbc, xbc_dact = _conv_fwd(proj, O_XBC, 3072, ssw8, ssm_conv_b, ts)
    gt = gates[:, :LANE].T
    dtt = gates[:, O_DT - O_IF:O_DT - O_IF + LANE].T
    hm, cst, nm = _mlstm_fwd(qk, proj, gates, gt)
    alog_row = jnp.pad(ssm_a_log, ((0, 0), (0, LANE - SSM_HEADS)))
    alog_col = alog_row.reshape(LANE, 1)
    dskip_x = jnp.repeat(ssm_d[0], SSM_HEADDIM)[None]
    expand = _head_expand()
    yssd, sst = _ssd_fwd(xbc, gates, dtt, alog_row, alog_col, dskip_x, expand)
    tp = min(128, s)
    ym, ys = _post_fwd(hm, yssd, proj, ml_norm_w, ssm_norm_w, tp)
    dxres, merged, dmo, dpm, dps, dym, dys, dproj, acc_m = _merge(xs, ym, ys, proj, tgt, gate, final_w, wpm, wps, wo, tp)
    dh, dyssd, dproj, acc_p = _post_bwd(dym, dys, hm, yssd, proj, ml_norm_w, ssm_norm_w, dproj, tp)
    dqk, dproj, dif = _mlstm_bwd(qk, proj, gates, gt, hm, dh, cst, nm, dproj)
    dxbc, ddt, accd, acca = _ssd_bwd(xbc, gates, dtt, alog_row, alog_col, dskip_x, expand, expand.T, dyssd, sst)
    dproj, acc_cq = _conv_bwd(proj, O_QK, 2048, mlw8, qk_dact, dqk, dproj, ts)
    dproj, acc_cx = _conv_bwd(proj, O_XBC, 3072, ssw8, xbc_dact, dxbc, dproj, ts)
    dproj = dproj.at[:, O_IF:O_IF + SMALL_W].set(dif).at[:, O_DT:O_DT + SMALL_W].set(ddt)
    gw_in_p, gb_in_p = _matmul_tn(u.T, dproj, tm, 512, with_colsum=True, a_is_transposed=True)
    g_wpm = _matmul_tn(ym, dpm, tm, 512)
    g_wps = _matmul_tn(ys, dps, tm, 512)
    g_wo = _matmul_tn(merged, dmo, tm, 512)
    token, in_flight = (None, None) if start_exchange is None else start_exchange(gw_in_p, g_wpm, g_wps, g_wo)
    du = _matmul_nt(dproj, w_in_p, tm, 512, after=token)
    grad_x, acc_n = _prenorm_bwd(du, xs, dxres, norm_w, scale, ts)
    a_coef = -jnp.exp(ssm_a_log[0])
    small = dict(
        mod=jnp.concatenate([acc_n[2], acc_n[1], acc_m[2]]), norm_w=acc_n[0], b_in=_unpad_cols(gb_in_p[0]),
        ml_conv_w=acc_cq[0:CONV_K], ml_conv_b=acc_cq[CONV_K], ml_norm_w=acc_p[0], ssm_conv_w=acc_cx[0:CONV_K],
        ssm_conv_b=acc_cx[CONV_K], ssm_a_log=acca[0, :SSM_HEADS] * a_coef,
        ssm_d=accd[0].reshape(SSM_HEADS, SSM_HEADDIM).sum(axis=1), ssm_norm_w=acc_p[1], final_w=acc_m[1], loss=acc_m[0, 0:1])
    return grad_x, small, gw_in_p, g_wpm, g_wps, g_wo, in_flight


WEIGHTS = ("norm_w", "ada_w", "ada_b", "w_in", "b_in", "ml_conv_w", "ml_conv_b", "ml_norm_w", "ssm_conv_w", "ssm_conv_b",
           "ssm_a_log", "ssm_d", "ssm_norm_w", "w_proj_m", "w_proj_s", "w_out", "final_w")
LARGE = ("ada_w", "w_in", "w_proj_m", "w_proj_s", "w_out")
SMALL_SUMS = (("mod", (3 * D_MODEL,)), ("norm_w", (D_MODEL,)), ("b_in", (IN_WIDTH,)), ("ml_conv_w", (CONV_K, 2048)),
              ("ml_conv_b", (2048,)), ("ml_norm_w", (2048,)), ("ssm_conv_w", (CONV_K, 3072)), ("ssm_conv_b", (3072,)),
              ("ssm_a_log", (SSM_HEADS,)), ("ssm_d", (SSM_HEADS,)), ("ssm_norm_w", (2048,)), ("final_w", (D_MODEL,)),
              ("loss", (1,)))


def kernel(x, c, norm_w, ada_w, ada_b, w_in, b_in, ml_conv_w, ml_conv_b, ml_norm_w, ssm_conv_w, ssm_conv_b, ssm_a_log, ssm_d, ssm_norm_w, w_proj_m, w_proj_s, w_out, final_w, loss_target, m_norm_w, m_ada_w, m_ada_b, m_w_in, m_b_in, m_ml_conv_w, m_ml_conv_b, m_ml_norm_w, m_ssm_conv_w, m_ssm_conv_b, m_ssm_a_log, m_ssm_d, m_ssm_norm_w, m_w_proj_m, m_w_proj_s, m_w_out, m_final_w, v_norm_w, v_ada_w, v_ada_b, v_w_in, v_b_in, v_ml_conv_w, v_ml_conv_b, v_ml_norm_w, v_ssm_conv_w, v_ssm_conv_b, v_ssm_a_log, v_ssm_d, v_ssm_norm_w, v_w_proj_m, v_w_proj_s, v_w_out, v_final_w):
    w = dict(norm_w=norm_w, ada_w=ada_w, ada_b=ada_b, w_in=w_in, b_in=b_in, ml_conv_w=ml_conv_w, ml_conv_b=ml_conv_b,
             ml_norm_w=ml_norm_w, ssm_conv_w=ssm_conv_w, ssm_conv_b=ssm_conv_b, ssm_a_log=ssm_a_log, ssm_d=ssm_d,
             ssm_norm_w=ssm_norm_w, w_proj_m=w_proj_m, w_proj_s=w_proj_s, w_out=w_out, final_w=final_w)
    m = dict(zip(WEIGHTS, (m_norm_w, m_ada_w, m_ada_b, m_w_in, m_b_in, m_ml_conv_w, m_ml_conv_b, m_ml_norm_w, m_ssm_conv_w,
                           m_ssm_conv_b, m_ssm_a_log, m_ssm_d, m_ssm_norm_w, m_w_proj_m, m_w_proj_s, m_w_out, m_final_w)))
    v = dict(zip(WEIGHTS, (v_norm_w, v_ada_w, v_ada_b, v_w_in, v_b_in, v_ml_conv_w, v_ml_conv_b, v_ml_norm_w, v_ssm_conv_w,
                           v_ssm_conv_b, v_ssm_a_log, v_ssm_d, v_ssm_norm_w, v_w_proj_m, v_w_proj_s, v_w_out, v_final_w)))
    pos = _position()
    chip = 2 * pos[0] + pos[1]
    dev = 2 * chip + pos[2]
    mlw_cols, ssw_cols, ada_cols = ml_conv_w.shape[2], ssm_conv_w.shape[2], ada_w.shape[2]

    g0 = _allgather8(_pack([c, ml_conv_w, ssm_conv_w]))
    c_all, mlw_all, ssw_all = _unpack_rows(g0, [(D_MODEL,), (CONV_K, mlw_cols), (CONV_K, ssw_cols)])
    ml_conv_full = mlw_all[0::2].transpose(1, 0, 2).reshape(CONV_K, N_CHIPS * mlw_cols)
    ssm_conv_full = ssw_all[0::2].transpose(1, 0, 2).reshape(CONV_K, N_CHIPS * ssw_cols)

    ada_b_mine = lax.dynamic_slice_in_dim(ada_b, chip * ada_cols, ada_cols, axis=1)
    g1 = _allgather8(_ada_fwd(c_all, ada_w[0], ada_b_mine))
    mod = lax.dynamic_index_in_dim(g1[0::2], dev, axis=1, keepdims=False).reshape(1, 3 * D_MODEL)
    shift, scale, gate = mod[:, :D_MODEL], mod[:, D_MODEL:2 * D_MODEL], mod[:, 2 * D_MODEL:]

    mine = [_bf(a[0]).reshape(2, a.shape[1] // 2, a.shape[2]) for a in (w_in, w_proj_m, w_proj_s, w_out)]
    gw = [lax.dynamic_update_index_in_dim(got, own, chip, 0).reshape(N_CHIPS, -1, own.shape[-1])
          for got, own in zip(_weight_gather(mine), mine)]
    w_in_p = _shards_to_padded(gw[0])
    wpm, wps, wo = (a.reshape(-1, D_MODEL) for a in gw[1:])

    def start_exchange(g_w_in, g_wpm, g_wps, g_wo):
        split = lambda g, rows: _bf(g).reshape(N_CHIPS, 2, rows // (2 * N_CHIPS), g.shape[-1])
        slabs = [split(_padded_to_shards(_bf(g_w_in)), N_CHIPS * D_MODEL),
                 split(g_wpm, g_wpm.shape[0]), split(g_wps, g_wps.shape[0]), split(g_wo, g_wo.shape[0])]
        pair_sums = []
        for slab, rec in zip(slabs, _pair_send(slabs)):
            kept = lax.dynamic_index_in_dim(slab, pos[2], 1, keepdims=False)
            rows = kept.shape[0] * kept.shape[1]
            both = _sum_parts(kept.reshape(rows, -1), rec.reshape(1, rows, -1), 32, BF16)
            pair_sums.append(both.reshape(kept.shape))
        send_sems, recv_sems, sums, lands, token = _chip_scatter_start(pair_sums)
        return token, (send_sems, recv_sems, sums, lands)

    grad_x, small, _, _, _, _, in_flight = _local_step(
        x[0], loss_target[0], scale, shift, gate, norm_w, w_in_p, _pad_cols(b_in), ml_conv_full, ml_conv_b, ml_norm_w,
        ssm_conv_full, ssm_conv_b, ssm_a_log, ssm_d, ssm_norm_w, wpm, wps, wo, final_w[None], start_exchange)

    g2 = _allgather8(_pack([small[name] for name, _ in SMALL_SUMS]))
    total = dict(zip([name for name, _ in SMALL_SUMS], _unpack(_sum_parts(None, g2, g2.shape[1]), [s for _, s in SMALL_SUMS])))
    dmod_all = g2[:, :3 * D_MODEL // LANE].reshape(N_DEV, 3 * D_MODEL)
    grads = dict(total)
    grads["ada_b"] = total["mod"]
    grads["ml_conv_w"] = lax.dynamic_slice_in_dim(total["ml_conv_w"], chip * mlw_cols, mlw_cols, axis=1)
    grads["ssm_conv_w"] = lax.dynamic_slice_in_dim(total["ssm_conv_w"], chip * ssw_cols, ssw_cols, axis=1)
    grads["ada_w"] = _ada_bwd(c_all, lax.dynamic_slice_in_dim(dmod_all, chip * ada_cols, ada_cols, axis=1))

    halves = []
    for both, rec in zip(*_chip_scatter_wait(*in_flight, grad_x)):
        halves.append(_sum_parts(lax.dynamic_index_in_dim(both, chip, 0, keepdims=False), rec, 32))
    for name, half, other in zip(("w_in", "w_proj_m", "w_proj_s", "w_out"), halves, _pair_exchange(halves)):
        grads[name] = lax.cond(pos[2] == 0, lambda mine, theirs: jnp.concatenate([mine, theirs]),
                               lambda mine, theirs: jnp.concatenate([theirs, mine]), half, other)

    delta, new_m, new_v = {}, {}, {}
    for name in LARGE:
        if w[name].shape[-1] % LANE:
            flat = lambda a: a.reshape(a.shape[-2:]).T
            back = lambda a: a.T.reshape(w[name].shape)
            g_flat = flat(grads[name])
            delta[name], new_m[name], new_v[name] = (back(a) for a in _adamw(flat(w[name]), g_flat, flat(m[name]), flat(v[name]), LANE))
            grads[name] = back(g_flat)
        else:
            delta[name], new_m[name], new_v[name] = _adamw(w[name], grads[name], m[name], v[name], 64)
    rest = [name for name in WEIGHTS if name not in LARGE]
    packed = [_pack([t[name] for name in rest]) for t in (w, grads, m, v)]
    for out, buf in zip((delta, new_m, new_v), _adamw(*packed, packed[0].shape[0])):
        out.update(zip(rest, _unpack(buf, [w[name].shape for name in rest])))
    loss = total["loss"][0]
    return (loss, grad_x[None], *[grads[name].reshape(w[name].shape) for name in WEIGHTS], *[delta[name] for name in WEIGHTS],
            *[new_m[name] for name in WEIGHTS], *[new_v[name] for name in WEIGHTS])
```

```python
import functools
import math

import jax
import jax.numpy as jnp
from jax import lax
from jax.experimental import pallas as pl
from jax.experimental.pallas import tpu as pltpu

F32 = jnp.float32
BF16 = jnp.bfloat16
HI = lax.Precision.HIGHEST
MESH = pl.DeviceIdType.MESH

D_MODEL = 1024
EPS = 1e-6
CONV_K = 4
ML_HEADS = 8
ML_DQK = 128
ML_DV = 256
SSM_HEADS = 32
SSM_HEADDIM = 64
SSM_GROUPS = 4
SSM_STATE = 128
IN_WIDTH = 15408
N_CHIPS = 4
N_DEV = 8
ADAM_LR, ADAM_B1, ADAM_B2, ADAM_EPS, ADAM_WD, ADAM_STEP = 0.001, 0.9, 0.999, 1e-08, 0.01, 10

O_O, O_ZM, O_ZS, O_MG, O_QK, O_V, O_XBC, O_IF, O_DT = 0, 2048, 4096, 6144, 8192, 10240, 12288, 15360, 15616
SMALL_W = 256
NP = 15872
LANE = 128
CHUNK = 128
NEG = -1e30
VMEM_LIMIT = 48 * 1024 * 1024
MERGE_VMEM = 60 * 1024 * 1024


def _cparams(*sem, vmem=VMEM_LIMIT):
    return pltpu.CompilerParams(dimension_semantics=sem, vmem_limit_bytes=vmem)


def _pad_cols(w):
    z = lambda n: jnp.zeros(w.shape[:-1] + (n,), w.dtype)
    return jnp.concatenate([w[..., 4096:8192], w[..., 11280:13328], w[..., 13360:15408], w[..., :4096], w[..., 8208:11280],
                            w[..., 8192:8208], z(SMALL_W - 16), w[..., 13328:13360], z(SMALL_W - 32)], axis=-1)


def _unpad_cols(g):
    return jnp.concatenate([g[..., O_QK:O_QK + 4096], g[..., O_O:O_O + 4096], g[..., O_IF:O_IF + 16],
                            g[..., O_XBC:O_XBC + 3072], g[..., O_ZS:O_ZS + 2048], g[..., O_DT:O_DT + 32],
                            g[..., O_MG:O_MG + 2048]], axis=-1)


PADDED_SEGMENTS = ((4096, 8192, 0), (11280, 13328, 0), (13360, 15408, 0), (0, 4096, 0), (8208, 11280, 0),
                   (8192, 8208, SMALL_W - 16), (13328, 13360, SMALL_W - 32))
SHARD_W = IN_WIDTH // N_CHIPS


def _shards_to_padded(shards):
    parts = []
    for first, last, pad in PADDED_SEGMENTS:
        for j in range(N_CHIPS):
            lo, hi = max(first, j * SHARD_W), min(last, (j + 1) * SHARD_W)
            if lo < hi:
                parts.append(shards[j][:, lo - j * SHARD_W:hi - j * SHARD_W])
        if pad:
            parts.append(jnp.zeros((shards.shape[1], pad), shards.dtype))
    return jnp.concatenate(parts, axis=1)


def _padded_to_shards(g):
    offsets, off = {}, 0
    for first, last, pad in PADDED_SEGMENTS:
        offsets[first] = off
        off += last - first + pad
    shards = []
    for j in range(N_CHIPS):
        parts = []
        for first, last, _ in sorted(PADDED_SEGMENTS):
            lo, hi = max(first, j * SHARD_W), min(last, (j + 1) * SHARD_W)
            if lo < hi:
                parts.append(g[:, offsets[first] + lo - first:offsets[first] + hi - first])
        shards.append(jnp.concatenate(parts, axis=1))
    return jnp.stack(shards)


def _sigmoid(x):
    return 1.0 / (1.0 + jnp.exp(-x))


def _silu(x):
    return x * _sigmoid(x)


def _dsilu(x):
    s = _sigmoid(x)
    return s + x * s * (1.0 - s)


def _softplus(x):
    return jnp.maximum(x, 0.0) + jnp.log(1.0 + jnp.exp(-jnp.abs(x)))


def _logsigmoid(x):
    return jnp.minimum(x, 0.0) - jnp.log(1.0 + jnp.exp(-jnp.abs(x)))


def _dot(a, b, dims, precision=None):
    return lax.dot_general(a, b, (dims, ((), ())), preferred_element_type=F32, precision=precision)


def _nn(a, b, precision=None):
    return _dot(a, b, ((1,), (0,)), precision)


def _nt(a, b, precision=None):
    return _dot(a, b, ((1,), (1,)), precision)


def _tn(a, b, precision=None):
    return _dot(a, b, ((0,), (0,)), precision)


def _bf(x):
    return x.astype(BF16)


def _split(x, terms):
    parts = []
    for _ in range(terms):
        part = _bf(x)
        parts.append(part)
        x = x - part.astype(F32)
    return parts


def _pick_right(x, pick, terms):
    pick = _bf(pick)
    out = None
    for part in _split(x, terms):
        out = _nn(part, pick) if out is None else out + _nn(part, pick)
    return out


def _pick_left(pick, x, terms):
    pick = _bf(pick)
    out = None
    for part in _split(x, terms):
        out = _nn(pick, part) if out is None else out + _nn(pick, part)
    return out


def _lane_col(x, lane):
    idx = lax.broadcasted_iota(jnp.int32, x.shape, 1)
    return jnp.sum(jnp.where(idx == lane, x, 0.0), axis=1, keepdims=True)


def _tri(n, upper):
    r = lax.broadcasted_iota(jnp.int32, (n, n), 0)
    c = lax.broadcasted_iota(jnp.int32, (n, n), 1)
    return jnp.where((r <= c) if upper else (r >= c), 1.0, 0.0).astype(F32)


def _eye(n):
    return jnp.where(lax.broadcasted_iota(jnp.int32, (n, n), 0) == lax.broadcasted_iota(jnp.int32, (n, n), 1), 1.0, 0.0)


def _sum_all(x):
    return jnp.sum(jnp.sum(x, axis=1, keepdims=True), axis=0, keepdims=True)


def _crossing(p):
    L = p.shape[0]
    below = _nn(_bf(_tri(L, True)), _bf(p))
    strict = lax.broadcasted_iota(jnp.int32, (L, L), 0) > lax.broadcasted_iota(jnp.int32, (L, L), 1)
    return [jnp.sum(jnp.where(strict, below[:, b * L:(b + 1) * L], 0.0), axis=1, keepdims=True)
            for b in range(p.shape[1] // L)]


def _matmul_bias(a, w, bias, tm, tn, col0, ncols, dtype):
    m, k = a.shape
    j0 = col0 // tn

    def body(a_ref, w_ref, b_ref, o_ref):
        o_ref[...] = (_nn(a_ref[...], w_ref[...]) + b_ref[...]).astype(dtype)

    return pl.pallas_call(
        body, name="matmul_bias", grid=(m // tm, ncols // tn),
        in_specs=[pl.BlockSpec((tm, k), lambda i, j: (i, 0)), pl.BlockSpec((k, tn), lambda i, j: (0, j0 + j)),
                  pl.BlockSpec((1, tn), lambda i, j: (0, j0 + j))],
        out_specs=pl.BlockSpec((tm, tn), lambda i, j: (i, j)),
        out_shape=jax.ShapeDtypeStruct((m, ncols), dtype),
        compiler_params=_cparams("parallel", "arbitrary"))(a, w, bias)


def _matmul_nt(a, w, tm, tk, after=None):
    m, n = a.shape
    k = w.shape[0]

    def body(a_ref, w_ref, *rest):
        o_ref = rest[-1]

        @pl.when(pl.program_id(1) == 0)
        def _():
            o_ref[...] = jnp.zeros_like(o_ref)
        o_ref[...] += _nt(a_ref[...], w_ref[...])

    extra = [] if after is None else [after]
    return pl.pallas_call(
        body, name="matmul_nt", grid=(m // tm, n // tk),
        in_specs=[pl.BlockSpec((tm, tk), lambda i, j: (i, j)), pl.BlockSpec((k, tk), lambda i, j: (0, j))]
        + [pl.BlockSpec(memory_space=pl.ANY)] * len(extra),
        out_specs=pl.BlockSpec((tm, k), lambda i, j: (i, 0)),
        out_shape=jax.ShapeDtypeStruct((m, k), F32),
        compiler_params=_cparams("parallel", "arbitrary"))(a, w, *extra)


def _matmul_tn(a, b, tm, tn, with_colsum=False, a_is_transposed=False):
    k, m = a.shape if a_is_transposed else a.shape[::-1]
    n = b.shape[1]

    def body(a_ref, b_ref, o_ref, *rest):
        first = pl.program_id(1) == 0

        @pl.when(first)
        def _():
            o_ref[...] = jnp.zeros_like(o_ref)
        o_ref[...] += _nn(a_ref[...], b_ref[...]) if a_is_transposed else _tn(a_ref[...], b_ref[...])
        if with_colsum:
            s_ref = rest[0]

            @pl.when(first)
            def _():
                s_ref[...] = jnp.zeros_like(s_ref)
            s_ref[...] += jnp.sum(b_ref[...].astype(F32), axis=0, keepdims=True)

    out_specs = [pl.BlockSpec((k, tn), lambda j, i: (0, j))]
    out_shape = [jax.ShapeDtypeStruct((k, n), F32)]
    if with_colsum:
        out_specs.append(pl.BlockSpec((1, tn), lambda j, i: (0, j)))
        out_shape.append(jax.ShapeDtypeStruct((1, n), F32))
    out = pl.pallas_call(
        body, name="matmul_tn", grid=(n // tn, m // tm),
        in_specs=[pl.BlockSpec((k, tm), lambda j, i: (0, i)) if a_is_transposed else pl.BlockSpec((tm, k), lambda j, i: (i, 0)),
                  pl.BlockSpec((tm, tn), lambda j, i: (i, j))],
        out_specs=out_specs, out_shape=out_shape,
        compiler_params=_cparams("parallel", "arbitrary"))(a, b)
    return out if with_colsum else out[0]


def _ada_fwd(c_all, ada_w, ada_b):
    def body(c_ref, w_ref, b_ref, o_ref):
        o_ref[...] = _nn(_bf(_silu(c_ref[...])), _bf(w_ref[...])) + b_ref[...]

    return pl.pallas_call(body, name="ada_fwd", out_shape=jax.ShapeDtypeStruct((c_all.shape[0], ada_w.shape[1]), F32),
                          compiler_params=_cparams())(c_all, ada_w, ada_b)


def _ada_bwd(c_all, dmod):
    def body(c_ref, d_ref, o_ref):
        o_ref[...] = _tn(_bf(_silu(c_ref[...])), _bf(d_ref[...]))

    return pl.pallas_call(body, name="ada_bwd", out_shape=jax.ShapeDtypeStruct((c_all.shape[1], dmod.shape[1]), F32),
                          compiler_params=_cparams())(c_all, dmod)


def _prenorm_fwd(x, norm_w, scale, shift, ts):
    s, d = x.shape

    def body(x_ref, nw_ref, sc_ref, sh_ref, u_ref):
        xv = x_ref[...]
        r = lax.rsqrt(jnp.mean(xv * xv, axis=1, keepdims=True) + EPS)
        u_ref[...] = _bf(xv * r * nw_ref[...] * (1.0 + sc_ref[...]) + sh_ref[...])

    row = pl.BlockSpec((1, d), lambda i: (0, 0))
    return pl.pallas_call(
        body, name="prenorm_fwd", grid=(s // ts,),
        in_specs=[pl.BlockSpec((ts, d), lambda i: (i, 0)), row, row, row],
        out_specs=pl.BlockSpec((ts, d), lambda i: (i, 0)), out_shape=jax.ShapeDtypeStruct((s, d), BF16),
        compiler_params=_cparams("parallel"))(x, norm_w, scale, shift)


def _prenorm_bwd(du, x, dxres, norm_w, scale, ts):
    s, d = x.shape

    def body(du_ref, x_ref, dr_ref, nw_ref, sc_ref, gx_ref, acc_ref):
        @pl.when(pl.program_id(0) == 0)
        def _():
            acc_ref[...] = jnp.zeros_like(acc_ref)
        xv, duv = x_ref[...], du_ref[...]
        r = lax.rsqrt(jnp.mean(xv * xv, axis=1, keepdims=True) + EPS)
        xn = xv * r
        nw, sc1 = nw_ref[...], 1.0 + sc_ref[...]
        dxn = duv * (nw * sc1)
        gx_ref[...] = r * (dxn - xn * jnp.mean(dxn * xn, axis=1, keepdims=True)) + dr_ref[...]
        t = duv * xn
        acc_ref[0:1, :] += jnp.sum(t, axis=0, keepdims=True) * sc1
        acc_ref[1:2, :] += jnp.sum(t, axis=0, keepdims=True) * nw
        acc_ref[2:3, :] += jnp.sum(duv, axis=0, keepdims=True)

    tile = pl.BlockSpec((ts, d), lambda i: (i, 0))
    row = pl.BlockSpec((1, d), lambda i: (0, 0))
    return pl.pallas_call(
        body, name="prenorm_bwd", grid=(s // ts,),
        in_specs=[tile, tile, tile, row, row],
        out_specs=[tile, pl.BlockSpec((8, d), lambda i: (0, 0))],
        out_shape=[jax.ShapeDtypeStruct((s, d), F32), jax.ShapeDtypeStruct((8, d), F32)],
        compiler_params=_cparams("arbitrary"))(du, x, dxres, norm_w, scale)


CONV_CB = 512


def _conv_taps(buf_ref, ts):
    return [buf_ref[pl.ds(8 - (CONV_K - 1) + j, ts), :] for j in range(CONV_K)]


def _conv_fwd(proj, col0, width, w8, b, ts):
    s = proj.shape[0]
    cb = CONV_CB
    nt = s // ts

    def body(x_ref, w_ref, b_ref, o_ref, ds_ref, buf_ref):
        @pl.when(pl.program_id(1) == 0)
        def _():
            buf_ref[0:8, :] = jnp.zeros((8, cb), F32)
        buf_ref[pl.ds(8, ts), :] = x_ref[...].astype(F32)
        acc = b_ref[...] + jnp.zeros((ts, cb), F32)
        for j, tap in enumerate(_conv_taps(buf_ref, ts)):
            acc = acc + tap * w_ref[j:j + 1, :]
        sg = _sigmoid(acc)
        o_ref[...] = acc * sg
        ds_ref[...] = _bf(sg + acc * sg * (1.0 - sg))
        buf_ref[0:8, :] = buf_ref[pl.ds(ts, 8), :]

    c0 = col0 // cb
    tile = pl.BlockSpec((ts, cb), lambda c, i: (i, c))
    return pl.pallas_call(
        body, name="conv_fwd", grid=(width // cb, nt),
        in_specs=[pl.BlockSpec((ts, cb), lambda c, i: (i, c0 + c)), pl.BlockSpec((8, cb), lambda c, i: (0, c)),
                  pl.BlockSpec((1, cb), lambda c, i: (0, c))],
        out_specs=[tile, tile],
        out_shape=[jax.ShapeDtypeStruct((s, width), F32), jax.ShapeDtypeStruct((s, width), BF16)],
        scratch_shapes=[pltpu.VMEM((ts + 8, cb), F32)],
        compiler_params=_cparams("parallel", "arbitrary"))(proj, w8, b)


def _conv_bwd(proj, col0, width, w8, dact, dpost, dproj, ts):
    s = proj.shape[0]
    cb = CONV_CB
    nt = s // ts
    c0 = col0 // cb

    def body(x_ref, da_ref, dp_ref, w_ref, _, dx_ref, acc_ref, dbuf_ref):
        @pl.when(pl.program_id(1) == 0)
        def _():
            acc_ref[...] = jnp.zeros_like(acc_ref)
            dbuf_ref[pl.ds(ts, 8), :] = jnp.zeros((8, cb), F32)
        dconv = dp_ref[...].astype(F32) * da_ref[...].astype(F32)
        acc_ref[CONV_K:CONV_K + 1, :] += jnp.sum(dconv, axis=0, keepdims=True)
        dbuf_ref[pl.ds(0, ts), :] = dconv
        xv = x_ref[...].astype(F32)
        dx = jnp.zeros((ts, cb), F32)
        for j in range(CONV_K):
            shifted = dbuf_ref[pl.ds(CONV_K - 1 - j, ts), :]
            dx = dx + shifted * w_ref[j:j + 1, :]
            acc_ref[j:j + 1, :] += jnp.sum(xv * shifted, axis=0, keepdims=True)
        dx_ref[...] = _bf(dx)
        dbuf_ref[pl.ds(ts, 8), :] = dconv[0:8, :]

    tile = pl.BlockSpec((ts, cb), lambda c, i: (nt - 1 - i, c))
    wide = pl.BlockSpec((ts, cb), lambda c, i: (nt - 1 - i, c0 + c))
    return pl.pallas_call(
        body, name="conv_bwd", grid=(width // cb, nt),
        in_specs=[wide, tile, tile, pl.BlockSpec((8, cb), lambda c, i: (0, c)), pl.BlockSpec(memory_space=pl.ANY)],
        out_specs=[wide, pl.BlockSpec((8, cb), lambda c, i: (0, c))],
        out_shape=[jax.ShapeDtypeStruct(dproj.shape, dproj.dtype), jax.ShapeDtypeStruct((8, width), F32)],
        input_output_aliases={4: 0},
        scratch_shapes=[pltpu.VMEM((ts + 8, cb), F32)],
        compiler_params=_cparams("parallel", "arbitrary"))(proj, dact, dpost, w8, dproj)


def _mlstm_gates(gif_ref, gt_ref, a_scr, at_scr):
    L = gif_ref.shape[0]
    fb = _logsigmoid(gif_ref[...])
    a_scr[...] = _pick_left(_tri(L, False), fb, 3)
    at_scr[...] = _pick_right(_logsigmoid(gt_ref[...]), _tri(L, True), 3)
    return jnp.sum(fb, axis=0, keepdims=True)


def _mlstm_head(h, qk_ref, v_ref, gif, gt_ref, a, at_scr, a_last_row, c_mat, n_row, m_prev):
    L = gif.shape[0]
    q = qk_ref[:, h * ML_DQK:(h + 1) * ML_DQK] * (ML_DQK ** -0.5)
    k = qk_ref[:, (ML_HEADS + h) * ML_DQK:(ML_HEADS + h + 1) * ML_DQK]
    v = v_ref[:, h * ML_DV:(h + 1) * ML_DV]
    i_col, a_col = _lane_col(gif, h), _lane_col(a, ML_HEADS + h)
    i_row, a_row = gt_ref[h:h + 1, :], at_scr[ML_HEADS + h:ML_HEADS + h + 1, :]
    causal = lax.broadcasted_iota(jnp.int32, (L, L), 0) >= lax.broadcasted_iota(jnp.int32, (L, L), 1)
    dmat = jnp.where(causal, a_col - a_row + i_row, NEG)
    inter = a_col + m_prev
    m_t = jnp.maximum(inter, jnp.max(dmat, axis=1, keepdims=True))
    w_intra = jnp.exp(dmat - m_t)
    w_inter = jnp.exp(inter - m_t)
    sc = _nt(_bf(q), _bf(k)) * w_intra
    den = jnp.sum(sc, axis=1, keepdims=True) + w_inter * jnp.sum(q * n_row, axis=1, keepdims=True)
    floor = jnp.exp(-m_t)
    a_last = _lane_col(a_last_row, ML_HEADS + h)
    g = a_last - a_col + i_col
    m_new = jnp.maximum(a_last + m_prev, jnp.max(g, axis=0, keepdims=True))
    wk = jnp.exp(g - m_new)
    decay = jnp.exp(a_last + m_prev - m_new)
    return dict(q=q, k=k, v=v, w_intra=w_intra, w_inter=w_inter, sc=sc, den=den, floor=floor, m_new=m_new, wk=wk,
                decay=decay)


def _state_tile(n_row, m11):
    r = lax.broadcasted_iota(jnp.int32, (8, LANE), 0)
    return jnp.where(r == 0, n_row, jnp.where(r == 1, m11, 0.0))


def _mlstm_fwd(qk, proj, gates, gt):
    s = qk.shape[0]
    L = CHUNK
    nc = s // L

    def body(qk_ref, v_ref, gif_ref, gt_ref, h_ref, cst_ref, nm_ref, c_scr, nm_scr, a_scr, at_scr):
        @pl.when(pl.program_id(0) == 0)
        def _():
            c_scr[...] = jnp.zeros_like(c_scr)
            nm_scr[...] = jnp.zeros_like(nm_scr)
        a_last_row = _mlstm_gates(gif_ref, gt_ref, a_scr, at_scr)
        gif, a = gif_ref[...], a_scr[...]
        for h in range(ML_HEADS):
            c_mat, n_row = c_scr[h], nm_scr[h, 0:1, :]
            m_prev = jnp.max(nm_scr[h, 1:2, :], axis=1, keepdims=True)
            cst_ref[0, h] = c_mat
            nm_ref[0, h] = nm_scr[h]
            t = _mlstm_head(h, qk_ref, v_ref, gif, gt_ref, a, at_scr, a_last_row, c_mat, n_row, m_prev)
            num = _nn(_bf(t["sc"]), _bf(t["v"])) + t["w_inter"] * _nn(_bf(t["q"]), _bf(c_mat))
            h_ref[:, h * ML_DV:(h + 1) * ML_DV] = _bf(num / jnp.maximum(jnp.abs(t["den"]), t["floor"]))
            kw = t["k"] * t["wk"]
            c_scr[h] = t["decay"] * c_mat + _tn(_bf(kw), _bf(t["v"]))
            nm_scr[h] = _state_tile(t["decay"] * n_row + jnp.sum(kw, axis=0, keepdims=True), t["m_new"])

    return pl.pallas_call(
        body, name="mlstm_fwd", grid=(nc,),
        in_specs=[pl.BlockSpec((L, 2048), lambda c: (c, 0)), pl.BlockSpec((L, 2048), lambda c: (c, O_V // 2048)),
                  pl.BlockSpec((L, LANE), lambda c: (c, 0)), pl.BlockSpec((LANE, L), lambda c: (0, c))],
        out_specs=[pl.BlockSpec((L, 2048), lambda c: (c, 0)),
                   pl.BlockSpec((1, ML_HEADS, ML_DQK, ML_DV), lambda c: (c, 0, 0, 0)),
                   pl.BlockSpec((1, ML_HEADS, 8, LANE), lambda c: (c, 0, 0, 0))],
        out_shape=[jax.ShapeDtypeStruct((s, 2048), BF16), jax.ShapeDtypeStruct((nc, ML_HEADS, ML_DQK, ML_DV), F32),
                   jax.ShapeDtypeStruct((nc, ML_HEADS, 8, LANE), F32)],
        scratch_shapes=[pltpu.VMEM((ML_HEADS, ML_DQK, ML_DV), F32), pltpu.VMEM((ML_HEADS, 8, LANE), F32),
                        pltpu.VMEM((L, LANE), F32), pltpu.VMEM((LANE, L), F32)],
        compiler_params=_cparams("arbitrary"))(qk, proj, gates, gt)


def _mlstm_bwd(qk, proj, gates, gt, hout, dh, cst, nm, dproj):
    s = qk.shape[0]
    L = CHUNK
    nc = s // L

    def body(qk_ref, v_ref, gif_ref, gt_ref, h_ref, dh_ref, cst_ref, nm_ref, _, dqk_ref, dv_ref, dif_ref,
             dc_scr, dn_scr, a_scr, at_scr):
        @pl.when(pl.program_id(0) == 0)
        def _():
            dc_scr[...] = jnp.zeros_like(dc_scr)
            dn_scr[...] = jnp.zeros_like(dn_scr)
        a_last_row = _mlstm_gates(gif_ref, gt_ref, a_scr, at_scr)
        gif, a = gif_ref[...], a_scr[...]
        lane = lax.broadcasted_iota(jnp.int32, (L, LANE), 1)
        last = lax.broadcasted_iota(jnp.int32, (L, 1), 0) == L - 1
        di_tile = jnp.zeros((L, LANE), F32)
        cross = [jnp.zeros((L, LANE), F32)] * 3
        dlogw = []
        for h in range(ML_HEADS):
            c_mat, n_row = cst_ref[0, h], nm_ref[0, h, 0:1, :]
            m_prev = jnp.max(nm_ref[0, h, 1:2, :], axis=1, keepdims=True)
            t = _mlstm_head(h, qk_ref, v_ref, gif, gt_ref, a, at_scr, a_last_row, c_mat, n_row, m_prev)
            q, k, v, den = t["q"], t["k"], t["v"], t["den"]
            dhh = dh_ref[:, h * ML_DV:(h + 1) * ML_DV].astype(F32)
            hh = h_ref[:, h * ML_DV:(h + 1) * ML_DV].astype(F32)
            dnorm = jnp.maximum(jnp.abs(den), t["floor"])
            dnum = dhh / dnorm
            d_dn = -jnp.sum(dhh * hh, axis=1, keepdims=True) / dnorm
            dden = jnp.where(jnp.abs(den) >= t["floor"], jnp.where(den >= 0.0, d_dn, -d_dn), 0.0)
            dsc = _nt(_bf(dnum), _bf(v)) + dden
            ds = dsc * t["w_intra"]
            dq_inter = t["w_inter"] * (_nt(_bf(dnum), _bf(c_mat)) + dden * n_row)
            dq = _nn(_bf(ds), _bf(k)) + dq_inter
            dc, dn_row = dc_scr[h], dn_scr[h, 0:1, :]
            dk_state = t["wk"] * (_nt(_bf(v), _bf(dc)) + dn_row)
            dk = _tn(_bf(ds), _bf(q)) + dk_state
            dv = _tn(_bf(t["sc"]), _bf(dnum)) + t["wk"] * _nn(_bf(k), _bf(dc))
            qi = q * t["w_inter"]
            dc_scr[h] = t["decay"] * dc + _tn(_bf(qi), _bf(dnum))
            dn_scr[h] = jnp.broadcast_to(t["decay"] * dn_row + jnp.sum(qi * dden, axis=0, keepdims=True), (8, LANE))
            dqk_ref[:, h * ML_DQK:(h + 1) * ML_DQK] = _bf(dq * (ML_DQK ** -0.5))
            dqk_ref[:, (ML_HEADS + h) * ML_DQK:(ML_HEADS + h + 1) * ML_DQK] = _bf(dk)
            dv_ref[:, h * ML_DV:(h + 1) * ML_DV] = _bf(dv)
            di_tile = di_tile + jnp.where(lane == h, jnp.sum(k * dk, axis=1, keepdims=True), 0.0)
            carried = t["decay"] * (_sum_all(dc * c_mat) + jnp.sum(dn_row * n_row, axis=1, keepdims=True))
            dlogw.append(dsc * t["sc"])
            parts = (jnp.sum(q * dq_inter, axis=1, keepdims=True) + jnp.where(last, carried, 0.0),
                     jnp.sum(k * dk_state, axis=1, keepdims=True))
            cross[1:] = [c + jnp.where(lane == ML_HEADS + h, p, 0.0) for c, p in zip(cross[1:], parts)]
        for h, col in enumerate(_crossing(jnp.concatenate(dlogw, axis=1))):
            cross[0] = cross[0] + jnp.where(lane == ML_HEADS + h, col, 0.0)
        dfb = cross[0] + _pick_left(_tri(L, True), cross[1], 2) + _pick_left(_tri(L, False) - _eye(L), cross[2], 2)
        dif_ref[:, 0:LANE] = _bf(di_tile + dfb * _sigmoid(-gif))
        dif_ref[:, LANE:SMALL_W] = jnp.zeros((L, SMALL_W - LANE), BF16)

    rev = lambda c: nc - 1 - c
    return pl.pallas_call(
        body, name="mlstm_bwd", grid=(nc,),
        in_specs=[pl.BlockSpec((L, 2048), lambda c: (rev(c), 0)), pl.BlockSpec((L, 2048), lambda c: (rev(c), O_V // 2048)),
                  pl.BlockSpec((L, LANE), lambda c: (rev(c), 0)), pl.BlockSpec((LANE, L), lambda c: (0, rev(c))),
                  pl.BlockSpec((L, 2048), lambda c: (rev(c), 0)), pl.BlockSpec((L, 2048), lambda c: (rev(c), 0)),
                  pl.BlockSpec((1, ML_HEADS, ML_DQK, ML_DV), lambda c: (rev(c), 0, 0, 0)),
                  pl.BlockSpec((1, ML_HEADS, 8, LANE), lambda c: (rev(c), 0, 0, 0)), pl.BlockSpec(memory_space=pl.ANY)],
        out_specs=[pl.BlockSpec((L, 2048), lambda c: (rev(c), 0)), pl.BlockSpec((L, 2048), lambda c: (rev(c), O_V // 2048)),
                   pl.BlockSpec((L, SMALL_W), lambda c: (rev(c), 0))],
        out_shape=[jax.ShapeDtypeStruct((s, 2048), BF16), jax.ShapeDtypeStruct(dproj.shape, dproj.dtype),
                   jax.ShapeDtypeStruct((s, SMALL_W), BF16)],
        input_output_aliases={8: 1},
        scratch_shapes=[pltpu.VMEM((ML_HEADS, ML_DQK, ML_DV), F32), pltpu.VMEM((ML_HEADS, 8, LANE), F32),
                        pltpu.VMEM((L, LANE), F32), pltpu.VMEM((LANE, L), F32)],
        compiler_params=_cparams("arbitrary"))(qk, proj, gates, gt, hout, dh, cst, nm, dproj)


GROUP_W = SSM_HEADS // SSM_GROUPS * SSM_HEADDIM
O_B = SSM_HEADS * SSM_HEADDIM
O_C = O_B + SSM_GROUPS * SSM_STATE


def _head_expand():
    r = jnp.arange(LANE)[:, None]
    c = jnp.arange(SSM_HEADS * SSM_HEADDIM)[None, :] // SSM_HEADDIM
    return (r == c).astype(F32)


def _ssd_gates(dt_ref, dtt_ref, alog_row_ref, alog_col_ref, at_scr):
    L = dt_ref.shape[0]
    dt = _softplus(dt_ref[...])
    acoef = -jnp.exp(alog_row_ref[...])
    a = _pick_left(_tri(L, False), dt * acoef, 3)
    at_scr[...] = _pick_right(_softplus(dtt_ref[...]) * (-jnp.exp(alog_col_ref[...])), _tri(L, True), 3)
    return dt, acoef, a


def _ssd_group(g, xbc_ref, dt, a, e_ref, ax_scr):
    eg = e_ref[:, g * GROUP_W:(g + 1) * GROUP_W]
    ax_scr[...] = _pick_right(a, eg, 3)
    ax = ax_scr[...]
    alx = ax_scr[ax.shape[0] - 1:ax.shape[0], :]
    dtx = _pick_right(dt, eg, 2)
    xg = xbc_ref[:, g * GROUP_W:(g + 1) * GROUP_W]
    bg = xbc_ref[:, O_B + g * SSM_STATE:O_B + (g + 1) * SSM_STATE]
    cg = xbc_ref[:, O_C + g * SSM_STATE:O_C + (g + 1) * SSM_STATE]
    return dict(ax=ax, alx=alx, dtx=dtx, xg=xg, bg=bg, cg=cg, xdt=xg * dtx, gmat=_nt(_bf(cg), _bf(bg)))


def _ssd_decay(hh, a, at_scr):
    L = a.shape[0]
    causal = lax.broadcasted_iota(jnp.int32, (L, L), 0) >= lax.broadcasted_iota(jnp.int32, (L, L), 1)
    return jnp.exp(jnp.where(causal, _lane_col(a, hh) - at_scr[hh:hh + 1, :], NEG))


def _ssd_fwd(xbc, gates, dtt, alog_row, alog_col, dskip_x, expand):
    s = xbc.shape[0]
    L = CHUNK
    nc = s // L
    half = SSM_HEADDIM

    def body(xbc_ref, dt_ref, dtt_ref, ar_ref, ac_ref, dk_ref, e_ref, y_ref, st_ref, st_scr, at_scr, ax_scr):
        @pl.when(pl.program_id(0) == 0)
        def _():
            st_scr[...] = jnp.zeros_like(st_scr)
        dt, _, a = _ssd_gates(dt_ref, dtt_ref, ar_ref, ac_ref, at_scr)
        lane = lax.broadcasted_iota(jnp.int32, (L, LANE), 1)
        for g in range(SSM_GROUPS):
            t = _ssd_group(g, xbc_ref, dt, a, e_ref, ax_scr)
            st = st_scr[g]
            st_ref[0, g] = st
            pairs = []
            for j in range(GROUP_W // LANE):
                xp = _bf(t["xdt"][:, j * LANE:(j + 1) * LANE])
                hh = g * (SSM_HEADS // SSM_GROUPS) + 2 * j
                both = jnp.concatenate([_bf(t["gmat"] * _ssd_decay(hh, a, at_scr)),
                                        _bf(t["gmat"] * _ssd_decay(hh + 1, a, at_scr))], axis=0)
                ys = _nn(both, xp)
                pairs.append(jnp.where(lane < half, ys[0:L], ys[L:2 * L]))
            y = jnp.concatenate(pairs, axis=1) + _nn(_bf(t["cg"]), _bf(st)) * jnp.exp(t["ax"])
            y_ref[:, g * GROUP_W:(g + 1) * GROUP_W] = _bf(y + dk_ref[:, g * GROUP_W:(g + 1) * GROUP_W] * t["xg"])
            wts = jnp.exp(t["alx"] - t["ax"])
            st_scr[g] = jnp.exp(t["alx"]) * st + _tn(_bf(t["bg"]), _bf(t["xdt"] * wts))

    row = lambda w: pl.BlockSpec((1, w), lambda c: (0, 0))
    return pl.pallas_call(
        body, name="ssd_fwd", grid=(nc,),
        in_specs=[pl.BlockSpec((L, 3072), lambda c: (c, 0)), pl.BlockSpec((L, LANE), lambda c: (c, (O_DT - O_IF) // LANE)),
                  pl.BlockSpec((LANE, L), lambda c: (0, c)), row(LANE), pl.BlockSpec((LANE, 1), lambda c: (0, 0)),
                  row(2048), pl.BlockSpec((LANE, 2048), lambda c: (0, 0))],
        out_specs=[pl.BlockSpec((L, 2048), lambda c: (c, 0)),
                   pl.BlockSpec((1, SSM_GROUPS, SSM_STATE, GROUP_W), lambda c: (c, 0, 0, 0))],
        out_shape=[jax.ShapeDtypeStruct((s, 2048), BF16),
                   jax.ShapeDtypeStruct((nc, SSM_GROUPS, SSM_STATE, GROUP_W), F32)],
        scratch_shapes=[pltpu.VMEM((SSM_GROUPS, SSM_STATE, GROUP_W), F32), pltpu.VMEM((LANE, L), F32),
                        pltpu.VMEM((L, GROUP_W), F32)],
        compiler_params=_cparams("arbitrary"))(xbc, gates, dtt, alog_row, alog_col, dskip_x, expand)


def _ssd_bwd(xbc, gates, dtt, alog_row, alog_col, dskip_x, expand, expand_t, dy, states):
    s = xbc.shape[0]
    L = CHUNK
    nc = s // L
    half = SSM_HEADDIM

    def body(xbc_ref, dt_ref, dtt_ref, ar_ref, ac_ref, dk_ref, e_ref, et_ref, dy_ref, st_ref,
             dxbc_ref, ddt_ref, accd_ref, acca_ref, dst_scr, at_scr, ax_scr):
        @pl.when(pl.program_id(0) == 0)
        def _():
            dst_scr[...] = jnp.zeros_like(dst_scr)
            accd_ref[...] = jnp.zeros_like(accd_ref)
            acca_ref[...] = jnp.zeros_like(acca_ref)
        dt, acoef, a = _ssd_gates(dt_ref, dtt_ref, ar_ref, ac_ref, at_scr)
        lane = lax.broadcasted_iota(jnp.int32, (L, LANE), 1)
        low = lane < half
        last = lax.broadcasted_iota(jnp.int32, (L, 1), 0) == L - 1
        cross = [jnp.zeros((L, LANE), F32)] * 3
        ddt_tile = jnp.zeros((L, LANE), F32)
        for g in range(SSM_GROUPS):
            t = _ssd_group(g, xbc_ref, dt, a, e_ref, ax_scr)
            xg, bg, cg, xdt, gmat = t["xg"], t["bg"], t["cg"], t["xdt"], t["gmat"]
            st, dst = st_ref[0, g], dst_scr[g]
            dyg = dy_ref[:, g * GROUP_W:(g + 1) * GROUP_W].astype(F32)
            ea, eal = jnp.exp(t["ax"]), jnp.exp(t["alx"])
            wts = jnp.exp(t["alx"] - t["ax"])
            dyi = dyg * ea
            y_inter = _nn(_bf(cg), _bf(st)) * ea
            dc = _nt(_bf(dyi), _bf(st))
            d_xdt_state = _nn(_bf(bg), _bf(dst)) * wts
            db = _nt(_bf(xdt * wts), _bf(dst))
            dst_scr[g] = eal * dst + _tn(_bf(cg), _bf(dyi))
            dg = jnp.zeros((L, L), F32)
            dx_pairs, dlogw = [], []
            for j in range(GROUP_W // LANE):
                xp = _bf(xdt[:, j * LANE:(j + 1) * LANE])
                dyp = dyg[:, j * LANE:(j + 1) * LANE]
                hh = g * (SSM_HEADS // SSM_GROUPS) + 2 * j
                decs = [_ssd_decay(hh, a, at_scr), _ssd_decay(hh + 1, a, at_scr)]
                ws = [gmat * decs[0], gmat * decs[1]]
                dxs = _tn(_bf(jnp.concatenate(ws, axis=1)), _bf(dyp))
                dws = _nt(_bf(jnp.concatenate([jnp.where(low, dyp, 0.0), jnp.where(low, 0.0, dyp)], axis=0)), xp)
                dw0, dw1 = dws[0:L], dws[L:2 * L]
                dg = dg + dw0 * decs[0] + dw1 * decs[1]
                dlogw += [dw0 * ws[0], dw1 * ws[1]]
                dx_pairs.append(jnp.where(low, dxs[0:L], dxs[L:2 * L]))
            for b, col in enumerate(_crossing(jnp.concatenate(dlogw, axis=1))):
                cross[0] = cross[0] + jnp.where(lane == g * (SSM_HEADS // SSM_GROUPS) + b, col, 0.0)
            d_xdt = d_xdt_state + jnp.concatenate(dx_pairs, axis=1)
            dc = dc + _nn(_bf(dg), _bf(bg))
            db = db + _tn(_bf(dg), _bf(cg))
            etg = et_ref[g * GROUP_W:(g + 1) * GROUP_W, :]
            carried = jnp.sum(dst * st, axis=0, keepdims=True) * eal
            cross[1] = cross[1] + _pick_right(dyg * y_inter + jnp.where(last, carried, 0.0), etg, 2)
            cross[2] = cross[2] + _pick_right(xdt * d_xdt_state, etg, 2)
            ddt_tile = ddt_tile + _pick_right(d_xdt * xg, etg, 2)
            dxbc_ref[:, g * GROUP_W:(g + 1) * GROUP_W] = _bf(d_xdt * t["dtx"] + dk_ref[:, g * GROUP_W:(g + 1) * GROUP_W] * dyg)
            dxbc_ref[:, O_B + g * SSM_STATE:O_B + (g + 1) * SSM_STATE] = _bf(db)
            dxbc_ref[:, O_C + g * SSM_STATE:O_C + (g + 1) * SSM_STATE] = _bf(dc)
            accd_ref[0:1, g * GROUP_W:(g + 1) * GROUP_W] += jnp.sum(dyg * xg, axis=0, keepdims=True)
        d_da = cross[0] + _pick_left(_tri(L, True), cross[1], 2) + _pick_left(_tri(L, False) - _eye(L), cross[2], 2)
        acca_ref[0:1, :] += jnp.sum(d_da * dt, axis=0, keepdims=True)
        ddt_ref[:, 0:LANE] = _bf((ddt_tile + d_da * acoef) * _sigmoid(dt_ref[...]))
        ddt_ref[:, LANE:SMALL_W] = jnp.zeros((L, SMALL_W - LANE), BF16)

    rev = lambda c: nc - 1 - c
    row = lambda w: pl.BlockSpec((1, w), lambda c: (0, 0))
    return pl.pallas_call(
        body, name="ssd_bwd", grid=(nc,),
        in_specs=[pl.BlockSpec((L, 3072), lambda c: (rev(c), 0)), pl.BlockSpec((L, LANE), lambda c: (rev(c), (O_DT - O_IF) // LANE)),
                  pl.BlockSpec((LANE, L), lambda c: (0, rev(c))), row(LANE), pl.BlockSpec((LANE, 1), lambda c: (0, 0)),
                  row(2048), pl.BlockSpec((LANE, 2048), lambda c: (0, 0)), pl.BlockSpec((2048, LANE), lambda c: (0, 0)),
                  pl.BlockSpec((L, 2048), lambda c: (rev(c), 0)),
                  pl.BlockSpec((1, SSM_GROUPS, SSM_STATE, GROUP_W), lambda c: (rev(c), 0, 0, 0))],
        out_specs=[pl.BlockSpec((L, 3072), lambda c: (rev(c), 0)), pl.BlockSpec((L, SMALL_W), lambda c: (rev(c), 0)),
                   pl.BlockSpec((8, 2048), lambda c: (0, 0)), pl.BlockSpec((8, LANE), lambda c: (0, 0))],
        out_shape=[jax.ShapeDtypeStruct((s, 3072), BF16), jax.ShapeDtypeStruct((s, SMALL_W), BF16),
                   jax.ShapeDtypeStruct((8, 2048), F32), jax.ShapeDtypeStruct((8, LANE), F32)],
        scratch_shapes=[pltpu.VMEM((SSM_GROUPS, SSM_STATE, GROUP_W), F32),
                        pltpu.VMEM((LANE, L), F32), pltpu.VMEM((L, GROUP_W), F32)],
        compiler_params=_cparams("arbitrary"))(xbc, gates, dtt, alog_row, alog_col, dskip_x, expand, expand_t, dy, states)


def _group_norm(v, width):
    outs, rs = [], []
    for k in range(v.shape[1] // width):
        blk = v[:, k * width:(k + 1) * width]
        r = lax.rsqrt(jnp.mean(blk * blk, axis=1, keepdims=True) + EPS)
        outs.append(blk * r)
        rs.append(jnp.broadcast_to(r, blk.shape))
    return jnp.concatenate(outs, axis=1), jnp.concatenate(rs, axis=1)


def _group_mean(v, width):
    return jnp.concatenate([jnp.broadcast_to(jnp.mean(v[:, k * width:(k + 1) * width], axis=1, keepdims=True),
                                             (v.shape[0], width)) for k in range(v.shape[1] // width)], axis=1)


def _post_fwd(hm, yssd, proj, ml_norm_w, ssm_norm_w, ts):
    s = hm.shape[0]

    def body(h_ref, ys_ref, o_ref, zm_ref, zs_ref, wm_ref, ws_ref, ym_ref, yso_ref):
        hn, _ = _group_norm(h_ref[...].astype(F32), ML_DV)
        ym_ref[...] = _bf(_sigmoid(o_ref[...].astype(F32)) * hn * wm_ref[...] * _silu(zm_ref[...].astype(F32)))
        pn, _ = _group_norm(ys_ref[...].astype(F32) * _silu(zs_ref[...].astype(F32)), GROUP_W)
        yso_ref[...] = _bf(pn * ws_ref[...])

    tile = pl.BlockSpec((ts, 2048), lambda i: (i, 0))
    col = lambda off: pl.BlockSpec((ts, 2048), lambda i: (i, off // 2048))
    row = pl.BlockSpec((1, 2048), lambda i: (0, 0))
    return pl.pallas_call(
        body, name="post_fwd", grid=(s // ts,),
        in_specs=[tile, tile, col(O_O), col(O_ZM), col(O_ZS), row, row],
        out_specs=[tile, tile],
        out_shape=[jax.ShapeDtypeStruct((s, 2048), BF16)] * 2,
        compiler_params=_cparams("parallel"))(hm, yssd, proj, proj, proj, ml_norm_w, ssm_norm_w)


def _post_bwd(dym, dys, hm, yssd, proj, ml_norm_w, ssm_norm_w, dproj, ts):
    s = hm.shape[0]

    def body(dym_ref, dys_ref, h_ref, ys_ref, o_ref, zm_ref, zs_ref, wm_ref, ws_ref, _,
             dh_ref, dyssd_ref, dp_ref, acc_ref):
        @pl.when(pl.program_id(0) == 0)
        def _():
            acc_ref[...] = jnp.zeros_like(acc_ref)
        hn, r = _group_norm(h_ref[...].astype(F32), ML_DV)
        so, zm, wm, d_ym = _sigmoid(o_ref[...].astype(F32)), zm_ref[...].astype(F32), wm_ref[...], dym_ref[...].astype(F32)
        sz = _silu(zm)
        hnw = hn * wm
        dp_ref[:, O_O:O_O + 2048] = _bf(d_ym * hnw * sz * so * (1.0 - so))
        dp_ref[:, O_ZM:O_ZM + 2048] = _bf(d_ym * so * hnw * _dsilu(zm))
        dhnw = d_ym * so * sz
        acc_ref[0:1, :] += jnp.sum(dhnw * hn, axis=0, keepdims=True)
        dhn = dhnw * wm
        dh_ref[...] = _bf(r * (dhn - hn * _group_mean(dhn * hn, ML_DV)))
        ysv, zs, d_ys = ys_ref[...].astype(F32), zs_ref[...].astype(F32), dys_ref[...].astype(F32)
        szs = _silu(zs)
        pn, r2 = _group_norm(ysv * szs, GROUP_W)
        acc_ref[1:2, :] += jnp.sum(d_ys * pn, axis=0, keepdims=True)
        dpn = d_ys * ws_ref[...]
        dp = r2 * (dpn - pn * _group_mean(dpn * pn, GROUP_W))
        dyssd_ref[...] = _bf(dp * szs)
        dp_ref[:, O_ZS:O_ZS + 2048] = _bf(dp * ysv * _dsilu(zs))

    tile = pl.BlockSpec((ts, 2048), lambda i: (i, 0))
    col = lambda off: pl.BlockSpec((ts, 2048), lambda i: (i, off // 2048))
    row = pl.BlockSpec((1, 2048), lambda i: (0, 0))
    sds = lambda dt: jax.ShapeDtypeStruct((s, 2048), dt)
    return pl.pallas_call(
        body, name="post_bwd", grid=(s // ts,),
        in_specs=[tile, tile, tile, tile, col(O_O), col(O_ZM), col(O_ZS), row, row, pl.BlockSpec(memory_space=pl.ANY)],
        out_specs=[tile, tile, pl.BlockSpec((ts, O_MG), lambda i: (i, 0)), pl.BlockSpec((8, 2048), lambda i: (0, 0))],
        out_shape=[sds(BF16), sds(BF16), jax.ShapeDtypeStruct(dproj.shape, dproj.dtype), jax.ShapeDtypeStruct((8, 2048), F32)],
        input_output_aliases={9: 2},
        compiler_params=_cparams("arbitrary"))(dym, dys, hm, yssd, proj, proj, proj, ml_norm_w, ssm_norm_w, dproj)


def _merge(x, ym, ys, proj, target, gate, final_w, wpm, wps, wo, ts):
    wpm_t, wps_t, wo_t = wpm.T, wps.T, wo.T
    s, d = x.shape

    def body(x_ref, ym_ref, ys_ref, mg_ref, t_ref, gate_ref, fw_ref, wpm_ref, wps_ref, wo_ref, wpmt_ref, wpst_ref, wot_ref,
             dres_ref, mer_ref, dmo_ref, dpm_ref, dps_ref, dym_ref, dys_ref, dmg_ref, acc_ref):
        @pl.when(pl.program_id(0) == 0)
        def _():
            acc_ref[...] = jnp.zeros_like(acc_ref)
        gm, gs = _sigmoid(mg_ref[:, 0:d].astype(F32)), _sigmoid(mg_ref[:, d:2 * d].astype(F32))
        pm = _nn(ym_ref[...], wpm_ref[...])
        ps = _nn(ys_ref[...], wps_ref[...])
        merged = _bf(gm * pm + gs * ps)
        mer_ref[...] = merged
        mo = _nn(merged, wo_ref[...])
        gate, fw = gate_ref[...], fw_ref[...]
        out = x_ref[...] + gate * mo
        r = lax.rsqrt(jnp.mean(out * out, axis=1, keepdims=True) + EPS)
        on = out * r
        diff = on * fw - t_ref[...]
        acc_ref[0:1, :] += jnp.sum(0.5 * jnp.sum(diff * diff, axis=1, keepdims=True) / d, axis=0, keepdims=True)
        dyv = diff * (1.0 / d)
        acc_ref[1:2, :] += jnp.sum(dyv * on, axis=0, keepdims=True)
        don = dyv * fw
        dout = r * (don - on * jnp.mean(don * on, axis=1, keepdims=True))
        dres_ref[...] = dout
        acc_ref[2:3, :] += jnp.sum(dout * mo, axis=0, keepdims=True)
        dmo = _bf(dout * gate)
        dmo_ref[...] = dmo
        dmer = _nn(dmo, wot_ref[...])
        dpm, dps = _bf(dmer * gm), _bf(dmer * gs)
        dpm_ref[...] = dpm
        dps_ref[...] = dps
        dmg_ref[:, 0:d] = _bf(dmer * pm * gm * (1.0 - gm))
        dmg_ref[:, d:2 * d] = _bf(dmer * ps * gs * (1.0 - gs))
        dym_ref[...] = _bf(_nn(dpm, wpmt_ref[...]))
        dys_ref[...] = _bf(_nn(dps, wpst_ref[...]))

    t1 = pl.BlockSpec((ts, d), lambda i: (i, 0))
    t2 = pl.BlockSpec((ts, 2 * d), lambda i: (i, 0))
    row = pl.BlockSpec((1, d), lambda i: (0, 0))
    whole = pl.BlockSpec(memory_space=pltpu.VMEM)
    sd = lambda w, dt: jax.ShapeDtypeStruct((s, w), dt)
    return pl.pallas_call(
        body, name="merge_fwd_bwd", grid=(s // ts,),
        in_specs=[t1, t2, t2, pl.BlockSpec((ts, 2 * d), lambda i: (i, O_MG // (2 * d))), t1, row, row] + [whole] * 6,
        out_specs=[t1, t1, t1, t1, t1, t2, t2, pl.BlockSpec((ts, 2 * d), lambda i: (i, O_MG // (2 * d))),
                   pl.BlockSpec((8, d), lambda i: (0, 0))],
        out_shape=[sd(d, F32), sd(d, BF16), sd(d, BF16), sd(d, BF16), sd(d, BF16), sd(2 * d, BF16), sd(2 * d, BF16),
                   sd(NP, BF16), jax.ShapeDtypeStruct((8, d), F32)],
        compiler_params=_cparams("arbitrary", vmem=MERGE_VMEM))(x, ym, ys, proj, target, gate, final_w, wpm, wps, wo, wpm_t, wps_t, wo_t)


def _adamw(w, g, m, v, tr):
    if w.ndim == 2 and w.shape[0] % 8:
        tile, steps = pl.BlockSpec((w.shape[0], tr), lambda i: (0, i)), w.shape[1] // tr
    else:
        lead = (None,) * (w.ndim - 2)
        tile, steps = pl.BlockSpec(lead + (tr, w.shape[-1]), lambda i: (0,) * len(lead) + (i, 0)), w.shape[-2] // tr

    def body(w_ref, g_ref, m_ref, v_ref, d_ref, nm_ref, nv_ref):
        gv = g_ref[...]
        m2 = ADAM_B1 * m_ref[...] + (1.0 - ADAM_B1) * gv
        v2 = ADAM_B2 * v_ref[...] + (1.0 - ADAM_B2) * (gv * gv)
        m_hat = m2 / (1.0 - ADAM_B1 ** ADAM_STEP)
        v_hat = v2 / (1.0 - ADAM_B2 ** ADAM_STEP)
        d_ref[...] = -ADAM_LR * (m_hat / (jnp.sqrt(v_hat) + ADAM_EPS) + ADAM_WD * w_ref[...])
        nm_ref[...] = m2
        nv_ref[...] = v2

    return pl.pallas_call(
        body, name="adamw", grid=(steps,), in_specs=[tile] * 4, out_specs=[tile] * 3,
        out_shape=[jax.ShapeDtypeStruct(w.shape, F32)] * 3,
        compiler_params=_cparams("parallel"))(w, g.reshape(w.shape), m, v)


def _sum_parts(own, parts, tr, dtype=F32):
    p, rows, cols = parts.shape

    def body(*refs):
        p_ref, o_ref = refs[-2], refs[-1]
        acc = p_ref[0].astype(F32) if own is None else refs[0][...].astype(F32) + p_ref[0].astype(F32)
        for i in range(1, p):
            acc = acc + p_ref[i].astype(F32)
        o_ref[...] = acc.astype(dtype)

    tile = pl.BlockSpec((tr, cols), lambda i: (i, 0))
    ins = ([] if own is None else [tile]) + [pl.BlockSpec((p, tr, cols), lambda i: (0, i, 0))]
    args = ([] if own is None else [own]) + [parts]
    return pl.pallas_call(
        body, name="sum_parts", grid=(rows // tr,), in_specs=ins, out_specs=tile,
        out_shape=jax.ShapeDtypeStruct((rows, cols), dtype), compiler_params=_cparams("parallel"))(*args)


def _position():
    return lax.axis_index("x"), lax.axis_index("y"), lax.axis_index("c")


def _flip(pos, k):
    return tuple(1 - p if (k >> s) & 1 else p for p, s in zip(pos, (2, 1, 0)))


def _allgather8(block):
    rows, cols = block.shape

    def body(x_ref, o_ref, send_sems, recv_sems, local_sem):
        pos = _position()
        me = 4 * pos[0] + 2 * pos[1] + pos[2]
        mine = pltpu.make_async_copy(x_ref, o_ref.at[me], local_sem)
        mine.start()
        copies = [pltpu.make_async_remote_copy(src_ref=x_ref, dst_ref=o_ref.at[me], send_sem=send_sems.at[k - 1],
                                               recv_sem=recv_sems.at[k - 1], device_id=_flip(pos, k), device_id_type=MESH)
                  for k in range(1, N_DEV)]
        for cp in copies:
            cp.start()
        for cp in copies:
            cp.wait()
        mine.wait()

    vmem = pl.BlockSpec(memory_space=pltpu.VMEM)
    return pl.pallas_call(
        body, name="allgather8", in_specs=[vmem], out_specs=vmem,
        out_shape=jax.ShapeDtypeStruct((N_DEV, rows, cols), block.dtype),
        scratch_shapes=[pltpu.SemaphoreType.DMA((N_DEV - 1,)), pltpu.SemaphoreType.DMA((N_DEV - 1,)),
                        pltpu.SemaphoreType.DMA],
        compiler_params=pltpu.CompilerParams(vmem_limit_bytes=VMEM_LIMIT))(block)


COPY_BYTES = 1 << 20


def _row_chunks(rows, row_bytes):
    n = max(1, min(rows // 16, -(-rows * row_bytes // COPY_BYTES)))
    while rows % (16 * n):
        n -= 1
    return [(i * (rows // n), rows // n) for i in range(n)]


def _weight_gather(shards):
    n = len(shards)
    pieces = [_row_chunks(a.shape[1], a.shape[2] * a.dtype.itemsize) for a in shards]
    plan = [(a, k, r0, nr) for a in range(n) for k in range(1, N_CHIPS) for r0, nr in pieces[a]]

    def body(*refs):
        ins, outs = refs[:n], refs[n:2 * n]
        ici_send, ici_recv, d2d_send, d2d_recv = refs[2 * n:]
        pos = _position()
        chip, core = 2 * pos[0] + pos[1], pos[2]
        sibling = _flip(pos, 1)
        sent = []
        for i, (a, k, r0, nr) in enumerate(plan):
            cp = pltpu.make_async_remote_copy(
                src_ref=ins[a].at[core, pl.ds(r0, nr)], dst_ref=outs[a].at[chip, core, pl.ds(r0, nr)],
                send_sem=ici_send.at[i], recv_sem=ici_recv.at[i], device_id=_flip(pos, 2 * k), device_id_type=MESH)
            cp.start()
            sent.append(cp)
        passed = []
        for i, (a, k, r0, nr) in enumerate(plan):
            there = _flip(pos, 2 * k)
            landed = outs[a].at[2 * there[0] + there[1], core, pl.ds(r0, nr)]
            sent[i].wait_recv()
            cp = pltpu.make_async_remote_copy(src_ref=landed, dst_ref=landed, send_sem=d2d_send.at[i],
                                              recv_sem=d2d_recv.at[i], device_id=sibling, device_id_type=MESH)
            cp.start()
            passed.append(cp)
        for cp in passed:
            cp.wait()
        for cp in sent:
            cp.wait_send()

    hbm = pl.BlockSpec(memory_space=pl.ANY)
    sems = pltpu.SemaphoreType.DMA((len(plan),))
    return pl.pallas_call(
        body, name="weight_gather", in_specs=[hbm] * n, out_specs=[hbm] * n,
        out_shape=[jax.ShapeDtypeStruct((N_CHIPS,) + a.shape, a.dtype) for a in shards],
        scratch_shapes=[sems, sems, sems, sems],
        compiler_params=pltpu.CompilerParams(has_side_effects=True))(*shards)


def _exchange(name, arrays, out_shapes, plan, n_remote, n_local):
    n, m = len(arrays), len(out_shapes)

    def body(*refs):
        send_sems, recv_sems, local_sems = refs[n + m:]
        remote, local = plan(_position(), refs[:n], refs[n:n + m])
        assert (len(remote), len(local)) == (n_remote, n_local)
        copies = [pltpu.make_async_copy(src, dst, local_sems.at[i]) for i, (src, dst) in enumerate(local)]
        copies += [pltpu.make_async_remote_copy(src_ref=src, dst_ref=dst, send_sem=send_sems.at[i], recv_sem=recv_sems.at[i],
                                                device_id=dev, device_id_type=MESH)
                   for i, (src, dst, dev) in enumerate(remote)]
        for cp in copies:
            cp.start()
        for cp in copies:
            cp.wait()

    hbm = pl.BlockSpec(memory_space=pl.ANY)
    return pl.pallas_call(
        body, name=name, in_specs=[hbm] * n, out_specs=[hbm] * m, out_shape=out_shapes,
        scratch_shapes=[pltpu.SemaphoreType.DMA((n_remote,)), pltpu.SemaphoreType.DMA((n_remote,)),
                        pltpu.SemaphoreType.DMA((max(n_local, 1),))],
        compiler_params=pltpu.CompilerParams(has_side_effects=True))(*arrays)


def _pair_send(slabs):
    n = len(slabs)
    pieces = [_row_chunks(g.shape[2], g.shape[3] * g.dtype.itemsize) for g in slabs]

    def plan(pos, ins, outs):
        return [(ins[a].at[j, 1 - pos[2], pl.ds(r0, nr)], outs[a].at[j, pl.ds(r0, nr)], _flip(pos, 1))
                for a in range(n) for j in range(N_CHIPS) for r0, nr in pieces[a]], []

    return _exchange("pair_send", slabs, [jax.ShapeDtypeStruct((N_CHIPS,) + g.shape[2:], g.dtype) for g in slabs], plan,
                     N_CHIPS * sum(len(p) for p in pieces), 0)


def _chip_scatter_copies(pos, sums, lands, send_sems, recv_sems):
    copies = []
    for a in range(len(sums)):
        for k in range(1, N_CHIPS):
            to = _flip(pos, 2 * k)
            for r0, nr in _row_chunks(sums[a].shape[1], sums[a].shape[2] * sums[a].dtype.itemsize):
                i = len(copies)
                copies.append(pltpu.make_async_remote_copy(
                    src_ref=sums[a].at[2 * to[0] + to[1], pl.ds(r0, nr)], dst_ref=lands[a].at[k - 1, pl.ds(r0, nr)],
                    send_sem=send_sems.at[i], recv_sem=recv_sems.at[i], device_id=to, device_id_type=MESH))
    return copies


def _chip_scatter_start(sums):
    n = len(sums)
    n_copies = (N_CHIPS - 1) * sum(len(_row_chunks(g.shape[1], g.shape[2] * g.dtype.itemsize)) for g in sums)
    lands = [lax.empty((N_CHIPS - 1,) + g.shape[1:], g.dtype) for g in sums]

    def body(*refs):
        send_sems, recv_sems = refs[2 * n], refs[2 * n + 1]
        for cp in _chip_scatter_copies(_position(), refs[:n], refs[n:2 * n], send_sems, recv_sems):
            cp.start()
        refs[-1][...] = jnp.zeros((8, LANE), F32)

    hbm = pl.BlockSpec(memory_space=pltpu.HBM)
    sem = pl.BlockSpec(memory_space=pltpu.SEMAPHORE)
    operands = [pltpu.with_memory_space_constraint(t, pltpu.HBM) for t in list(sums) + lands]
    out = pl.pallas_call(
        body, name="chip_scatter_start", in_specs=[hbm] * (2 * n),
        out_specs=[sem, sem] + [hbm] * (2 * n) + [pl.BlockSpec(memory_space=pltpu.VMEM)],
        out_shape=[pltpu.SemaphoreType.DMA((n_copies,)), pltpu.SemaphoreType.DMA((n_copies,))]
        + [pltpu.HBM(t.shape, t.dtype) for t in operands] + [jax.ShapeDtypeStruct((8, LANE), F32)],
        input_output_aliases={i: 2 + i for i in range(2 * n)},
        compiler_params=pltpu.CompilerParams(has_side_effects=pltpu.SideEffectType.DATAFLOW_SIDE_EFFECTING))(*operands)
    return out[0], out[1], out[2:2 + n], out[2 + n:2 + 2 * n], out[-1]


def _chip_scatter_wait(send_sems, recv_sems, sums, lands, after):
    n = len(sums)

    def body(*refs):
        for cp in _chip_scatter_copies(_position(), refs[:n], refs[n:2 * n], refs[2 * n], refs[2 * n + 1]):
            cp.wait_send()
            cp.wait_recv()

    hbm = pl.BlockSpec(memory_space=pltpu.HBM)
    sem = pl.BlockSpec(memory_space=pltpu.SEMAPHORE)
    out = pl.pallas_call(
        body, name="chip_scatter_wait", in_specs=[hbm] * (2 * n) + [sem, sem, pl.BlockSpec(memory_space=pl.ANY)],
        out_specs=[hbm] * (2 * n), out_shape=[pltpu.HBM(t.shape, t.dtype) for t in list(sums) + list(lands)],
        input_output_aliases={i: i for i in range(2 * n)},
        compiler_params=pltpu.CompilerParams(has_side_effects=pltpu.SideEffectType.DATAFLOW_SIDE_EFFECTING))(
            *sums, *lands, send_sems, recv_sems, after)
    return out[:n], out[n:]


def _pair_exchange(halves):
    n = len(halves)
    pieces = [_row_chunks(h.shape[0], h.shape[1] * h.dtype.itemsize) for h in halves]

    def plan(pos, ins, outs):
        return [(ins[a].at[pl.ds(r0, nr)], outs[a].at[pl.ds(r0, nr)], _flip(pos, 1))
                for a in range(n) for r0, nr in pieces[a]], []

    return _exchange("pair_exchange", halves, [jax.ShapeDtypeStruct(h.shape, h.dtype) for h in halves], plan,
                     sum(len(p) for p in pieces), 0)


def _pack(arrays):
    flat = jnp.concatenate([a.reshape(-1).astype(F32) for a in arrays])
    size = -(-flat.shape[0] // (8 * LANE)) * (8 * LANE)
    return jnp.pad(flat, (0, size - flat.shape[0])).reshape(size // LANE, LANE)


def _unpack(buf, shapes):
    flat = buf.reshape(-1)
    out, off = [], 0
    for shp in shapes:
        n = math.prod(shp)
        out.append(flat[off:off + n].reshape(shp))
        off += n
    return out


def _unpack_rows(bufs, shapes):
    flat = bufs.reshape(bufs.shape[0], -1)
    out, off = [], 0
    for shp in shapes:
        n = math.prod(shp)
        out.append(flat[:, off:off + n].reshape((bufs.shape[0],) + shp))
        off += n
    return out


def _taps8(w):
    return jnp.pad(w, ((0, 8 - CONV_K), (0, 0)))


def _local_step(xs, tgt, scale, shift, gate, norm_w, w_in_p, b_in_p, ml_conv_w, ml_conv_b, ml_norm_w, ssm_conv_w,
                ssm_conv_b, ssm_a_log, ssm_d, ssm_norm_w, wpm, wps, wo, final_w, start_exchange=None):
    s = xs.shape[0]
    ts = min(512, s)
    tm = min(2048, s)
    u = _prenorm_fwd(xs, norm_w, scale, shift, ts)
    proj = _matmul_bias(u, w_in_p, b_in_p, tm, 512, 0, O_IF, BF16)
    gates = _matmul_bias(u, w_in_p, b_in_p, tm, 512, O_IF, NP - O_IF, F32)
    mlw8, ssw8 = _taps8(ml_conv_w), _taps8(ssm_conv_w)
    qk, qk_dact = _conv_fwd(proj, O_QK, 2048, mlw8, ml_conv_b, ts)
    xbc, xbc_dact = _conv_fwd(proj, O_XBC, 3072, ssw8, ssm_conv_b, ts)
    gt = gates[:, :LANE].T
    dtt = gates[:, O_DT - O_IF:O_DT - O_IF + LANE].T
    hm, cst, nm = _mlstm_fwd(qk, proj, gates, gt)
    alog_row = jnp.pad(ssm_a_log, ((0, 0), (0, LANE - SSM_HEADS)))
    alog_col = alog_row.reshape(LANE, 1)
    dskip_x = jnp.repeat(ssm_d[0], SSM_HEADDIM)[None]
    expand = _head_expand()
    yssd, sst = _ssd_fwd(xbc, gates, dtt, alog_row, alog_col, dskip_x, expand)
    tp = min(128, s)
    ym, ys = _post_fwd(hm, yssd, proj, ml_norm_w, ssm_norm_w, tp)
    dxres, merged, dmo, dpm, dps, dym, dys, dproj, acc_m = _merge(xs, ym, ys, proj, tgt, gate, final_w, wpm, wps, wo,
                                                                  min(256, s))
    dh, dyssd, dproj, acc_p = _post_bwd(dym, dys, hm, yssd, proj, ml_norm_w, ssm_norm_w, dproj, tp)
    dqk, dproj, dif = _mlstm_bwd(qk, proj, gates, gt, hm, dh, cst, nm, dproj)
    dxbc, ddt, accd, acca = _ssd_bwd(xbc, gates, dtt, alog_row, alog_col, dskip_x, expand, expand.T, dyssd, sst)
    dproj, acc_cq = _conv_bwd(proj, O_QK, 2048, mlw8, qk_dact, dqk, dproj, ts)
    dproj, acc_cx = _conv_bwd(proj, O_XBC, 3072, ssw8, xbc_dact, dxbc, dproj, ts)
    dproj = dproj.at[:, O_IF:O_IF + SMALL_W].set(dif).at[:, O_DT:O_DT + SMALL_W].set(ddt)
    gw_in_p, gb_in_p = _matmul_tn(u.T, dproj, tm, 512, with_colsum=True, a_is_transposed=True)
    g_wpm = _matmul_tn(ym, dpm, tm, 512)
    g_wps = _matmul_tn(ys, dps, tm, 512)
    g_wo = _matmul_tn(merged, dmo, tm, 512)
    token, in_flight = (None, None) if start_exchange is None else start_exchange(gw_in_p, g_wpm, g_wps, g_wo)
    du = _matmul_nt(dproj, w_in_p, tm, 512, after=token)
    grad_x, acc_n = _prenorm_bwd(du, xs, dxres, norm_w, scale, ts)
    a_coef = -jnp.exp(ssm_a_log[0])
    small = dict(
        mod=jnp.concatenate([acc_n[2], acc_n[1], acc_m[2]]), norm_w=acc_n[0], b_in=_unpad_cols(gb_in_p[0]),
        ml_conv_w=acc_cq[0:CONV_K], ml_conv_b=acc_cq[CONV_K], ml_norm_w=acc_p[0], ssm_conv_w=acc_cx[0:CONV_K],
        ssm_conv_b=acc_cx[CONV_K], ssm_a_log=acca[0, :SSM_HEADS] * a_coef,
        ssm_d=accd[0].reshape(SSM_HEADS, SSM_HEADDIM).sum(axis=1), ssm_norm_w=acc_p[1], final_w=acc_m[1], loss=acc_m[0, 0:1])
    return grad_x, small, gw_in_p, g_wpm, g_wps, g_wo, in_flight


WEIGHTS = ("norm_w", "ada_w", "ada_b", "w_in", "b_in", "ml_conv_w", "ml_conv_b", "ml_norm_w", "ssm_conv_w", "ssm_conv_b",
           "ssm_a_log", "ssm_d", "ssm_norm_w", "w_proj_m", "w_proj_s", "w_out", "final_w")
LARGE = ("ada_w", "w_in", "w_proj_m", "w_proj_s", "w_out")
SMALL_SUMS = (("mod", (3 * D_MODEL,)), ("norm_w", (D_MODEL,)), ("b_in", (IN_WIDTH,)), ("ml_conv_w", (CONV_K, 2048)),
              ("ml_conv_b", (2048,)), ("ml_norm_w", (2048,)), ("ssm_conv_w", (CONV_K, 3072)), ("ssm_conv_b", (3072,)),
              ("ssm_a_log", (SSM_HEADS,)), ("ssm_d", (SSM_HEADS,)), ("ssm_norm_w", (2048,)), ("final_w", (D_MODEL,)),
              ("loss", (1,)))


def kernel(x, c, norm_w, ada_w, ada_b, w_in, b_in, ml_conv_w, ml_conv_b, ml_norm_w, ssm_conv_w, ssm_conv_b, ssm_a_log, ssm_d, ssm_norm_w, w_proj_m, w_proj_s, w_out, final_w, loss_target, m_norm_w, m_ada_w, m_ada_b, m_w_in, m_b_in, m_ml_conv_w, m_ml_conv_b, m_ml_norm_w, m_ssm_conv_w, m_ssm_conv_b, m_ssm_a_log, m_ssm_d, m_ssm_norm_w, m_w_proj_m, m_w_proj_s, m_w_out, m_final_w, v_norm_w, v_ada_w, v_ada_b, v_w_in, v_b_in, v_ml_conv_w, v_ml_conv_b, v_ml_norm_w, v_ssm_conv_w, v_ssm_conv_b, v_ssm_a_log, v_ssm_d, v_ssm_norm_w, v_w_proj_m, v_w_proj_s, v_w_out, v_final_w):
    w = dict(norm_w=norm_w, ada_w=ada_w, ada_b=ada_b, w_in=w_in, b_in=b_in, ml_conv_w=ml_conv_w, ml_conv_b=ml_conv_b,
             ml_norm_w=ml_norm_w, ssm_conv_w=ssm_conv_w, ssm_conv_b=ssm_conv_b, ssm_a_log=ssm_a_log, ssm_d=ssm_d,
             ssm_norm_w=ssm_norm_w, w_proj_m=w_proj_m, w_proj_s=w_proj_s, w_out=w_out, final_w=final_w)
    m = dict(zip(WEIGHTS, (m_norm_w, m_ada_w, m_ada_b, m_w_in, m_b_in, m_ml_conv_w, m_ml_conv_b, m_ml_norm_w, m_ssm_conv_w,
                           m_ssm_conv_b, m_ssm_a_log, m_ssm_d, m_ssm_norm_w, m_w_proj_m, m_w_proj_s, m_w_out, m_final_w)))
    v = dict(zip(WEIGHTS, (v_norm_w, v_ada_w, v_ada_b, v_w_in, v_b_in, v_ml_conv_w, v_ml_conv_b, v_ml_norm_w, v_ssm_conv_w,
                           v_ssm_conv_b, v_ssm_a_log, v_ssm_d, v_ssm_norm_w, v_w_proj_m, v_w_proj_s, v_w_out, v_final_w)))
    pos = _position()
    chip = 2 * pos[0] + pos[1]
    dev = 2 * chip + pos[2]
    mlw_cols, ssw_cols, ada_cols = ml_conv_w.shape[2], ssm_conv_w.shape[2], ada_w.shape[2]

    g0 = _allgather8(_pack([c, ml_conv_w, ssm_conv_w]))
    c_all, mlw_all, ssw_all = _unpack_rows(g0, [(D_MODEL,), (CONV_K, mlw_cols), (CONV_K, ssw_cols)])
    ml_conv_full = mlw_all[0::2].transpose(1, 0, 2).reshape(CONV_K, N_CHIPS * mlw_cols)
    ssm_conv_full = ssw_all[0::2].transpose(1, 0, 2).reshape(CONV_K, N_CHIPS * ssw_cols)

    ada_b_mine = lax.dynamic_slice_in_dim(ada_b, chip * ada_cols, ada_cols, axis=1)
    g1 = _allgather8(_ada_fwd(c_all, ada_w[0], ada_b_mine))
    mod = lax.dynamic_index_in_dim(g1[0::2], dev, axis=1, keepdims=False).reshape(1, 3 * D_MODEL)
    shift, scale, gate = mod[:, :D_MODEL], mod[:, D_MODEL:2 * D_MODEL], mod[:, 2 * D_MODEL:]

    mine = [_bf(a[0]).reshape(2, a.shape[1] // 2, a.shape[2]) for a in (w_in, w_proj_m, w_proj_s, w_out)]
    gw = [lax.dynamic_update_index_in_dim(got, own, chip, 0).reshape(N_CHIPS, -1, own.shape[-1])
          for got, own in zip(_weight_gather(mine), mine)]
    w_in_p = _shards_to_padded(gw[0])
    wpm, wps, wo = (a.reshape(-1, D_MODEL) for a in gw[1:])

    def start_exchange(g_w_in, g_wpm, g_wps, g_wo):
        split = lambda g, rows: _bf(g).reshape(N_CHIPS, 2, rows // (2 * N_CHIPS), g.shape[-1])
        slabs = [split(_padded_to_shards(_bf(g_w_in)), N_CHIPS * D_MODEL),
                 split(g_wpm, g_wpm.shape[0]), split(g_wps, g_wps.shape[0]), split(g_wo, g_wo.shape[0])]
        pair_sums = []
        for slab, rec in zip(slabs, _pair_send(slabs)):
            kept = lax.dynamic_index_in_dim(slab, pos[2], 1, keepdims=False)
            rows = kept.shape[0] * kept.shape[1]
            both = _sum_parts(kept.reshape(rows, -1), rec.reshape(1, rows, -1), 32, BF16)
            pair_sums.append(both.reshape(kept.shape))
        send_sems, recv_sems, sums, lands, token = _chip_scatter_start(pair_sums)
        return token, (send_sems, recv_sems, sums, lands)

    grad_x, small, _, _, _, _, in_flight = _local_step(
        x[0], loss_target[0], scale, shift, gate, norm_w, w_in_p, _pad_cols(b_in), ml_conv_full, ml_conv_b, ml_norm_w,
        ssm_conv_full, ssm_conv_b, ssm_a_log, ssm_d, ssm_norm_w, wpm, wps, wo, final_w[None], start_exchange)

    g2 = _allgather8(_pack([small[name] for name, _ in SMALL_SUMS]))
    total = dict(zip([name for name, _ in SMALL_SUMS], _unpack(_sum_parts(None, g2, g2.shape[1]), [s for _, s in SMALL_SUMS])))
    dmod_all = g2[:, :3 * D_MODEL // LANE].reshape(N_DEV, 3 * D_MODEL)
    grads = dict(total)
    grads["ada_b"] = total["mod"]
    grads["ml_conv_w"] = lax.dynamic_slice_in_dim(total["ml_conv_w"], chip * mlw_cols, mlw_cols, axis=1)
    grads["ssm_conv_w"] = lax.dynamic_slice_in_dim(total["ssm_conv_w"], chip * ssw_cols, ssw_cols, axis=1)
    grads["ada_w"] = _ada_bwd(c_all, lax.dynamic_slice_in_dim(dmod_all, chip * ada_cols, ada_cols, axis=1))

    halves = []
    for both, rec in zip(*_chip_scatter_wait(*in_flight, grad_x)):
        halves.append(_sum_parts(lax.dynamic_index_in_dim(both, chip, 0, keepdims=False), rec, 32))
    for name, half, other in zip(("w_in", "w_proj_m", "w_proj_s", "w_out"), halves, _pair_exchange(halves)):
        grads[name] = lax.cond(pos[2] == 0, lambda mine, theirs: jnp.concatenate([mine, theirs]),
                               lambda mine, theirs: jnp.concatenate([theirs, mine]), half, other)

    delta, new_m, new_v = {}, {}, {}
    for name in LARGE:
        if w[name].shape[-1] % LANE:
            flat = lambda a: a.reshape(a.shape[-2:]).T
            back = lambda a: a.T.reshape(w[name].shape)
            g_flat = flat(grads[name])
            delta[name], new_m[name], new_v[name] = (back(a) for a in _adamw(flat(w[name]), g_flat, flat(m[name]), flat(v[name]), LANE))
            grads[name] = back(g_flat)
        else:
            delta[name], new_m[name], new_v[name] = _adamw(w[name], grads[name], m[name], v[name], 64)
    rest = [name for name in WEIGHTS if name not in LARGE]
    packed = [_pack([t[name] for name in rest]) for t in (w, grads, m, v)]
    for out, buf in zip((delta, new_m, new_v), _adamw(*packed, packed[0].shape[0])):
        out.update(zip(rest, _unpack(buf, [w[name].shape for name in rest])))
    loss = total["loss"][0]
    return (loss, grad_x[None], *[grads[name].reshape(w[name].shape) for name in WEIGHTS], *[delta[name] for name in WEIGHTS],
            *[new_m[name] for name in WEIGHTS], *[new_v[name] for name in WEIGHTS])
```

```python
import functools
import math

import jax
import jax.numpy as jnp
from jax import lax
from jax.experimental import pallas as pl
from jax.experimental.pallas import tpu as pltpu

F32 = jnp.float32
BF16 = jnp.bfloat16
HI = lax.Precision.HIGHEST
MESH = pl.DeviceIdType.MESH

D_MODEL = 1024
EPS = 1e-6
CONV_K = 4
ML_HEADS = 8
ML_DQK = 128
ML_DV = 256
SSM_HEADS = 32
SSM_HEADDIM = 64
SSM_GROUPS = 4
SSM_STATE = 128
IN_WIDTH = 15408
N_CHIPS = 4
N_DEV = 8
ADAM_LR, ADAM_B1, ADAM_B2, ADAM_EPS, ADAM_WD, ADAM_STEP = 0.001, 0.9, 0.999, 1e-08, 0.01, 10

O_O, O_ZM, O_ZS, O_MG, O_QK, O_V, O_XBC, O_IF, O_DT = 0, 2048, 4096, 6144, 8192, 10240, 12288, 15360, 15616
SMALL_W = 256
NP = 15872
LANE = 128
CHUNK = 128
NEG = -1e30
VMEM_LIMIT = 48 * 1024 * 1024
MERGE_VMEM = 60 * 1024 * 1024


def _cparams(*sem, vmem=VMEM_LIMIT):
    return pltpu.CompilerParams(dimension_semantics=sem, vmem_limit_bytes=vmem)


def _pad_cols(w):
    z = lambda n: jnp.zeros(w.shape[:-1] + (n,), w.dtype)
    return jnp.concatenate([w[..., 4096:8192], w[..., 11280:13328], w[..., 13360:15408], w[..., :4096], w[..., 8208:11280],
                            w[..., 8192:8208], z(SMALL_W - 16), w[..., 13328:13360], z(SMALL_W - 32)], axis=-1)


def _unpad_cols(g):
    return jnp.concatenate([g[..., O_QK:O_QK + 4096], g[..., O_O:O_O + 4096], g[..., O_IF:O_IF + 16],
                            g[..., O_XBC:O_XBC + 3072], g[..., O_ZS:O_ZS + 2048], g[..., O_DT:O_DT + 32],
                            g[..., O_MG:O_MG + 2048]], axis=-1)


PADDED_SEGMENTS = ((4096, 8192, 0), (11280, 13328, 0), (13360, 15408, 0), (0, 4096, 0), (8208, 11280, 0),
                   (8192, 8208, SMALL_W - 16), (13328, 13360, SMALL_W - 32))
SHARD_W = IN_WIDTH // N_CHIPS


def _shards_to_padded(shards):
    parts = []
    for first, last, pad in PADDED_SEGMENTS:
        for j in range(N_CHIPS):
            lo, hi = max(first, j * SHARD_W), min(last, (j + 1) * SHARD_W)
            if lo < hi:
                parts.append(shards[j][:, lo - j * SHARD_W:hi - j * SHARD_W])
        if pad:
            parts.append(jnp.zeros((shards.shape[1], pad), shards.dtype))
    return jnp.concatenate(parts, axis=1)


def _padded_to_shards(g):
    offsets, off = {}, 0
    for first, last, pad in PADDED_SEGMENTS:
        offsets[first] = off
        off += last - first + pad
    shards = []
    for j in range(N_CHIPS):
        parts = []
        for first, last, _ in sorted(PADDED_SEGMENTS):
            lo, hi = max(first, j * SHARD_W), min(last, (j + 1) * SHARD_W)
            if lo < hi:
                parts.append(g[:, offsets[first] + lo - first:offsets[first] + hi - first])
        shards.append(jnp.concatenate(parts, axis=1))
    return jnp.stack(shards)


def _sigmoid(x):
    return 1.0 / (1.0 + jnp.exp(-x))


def _silu(x):
    return x * _sigmoid(x)


def _dsilu(x):
    s = _sigmoid(x)
    return s + x * s * (1.0 - s)


def _softplus(x):
    return jnp.maximum(x, 0.0) + jnp.log(1.0 + jnp.exp(-jnp.abs(x)))


def _logsigmoid(x):
    return jnp.minimum(x, 0.0) - jnp.log(1.0 + jnp.exp(-jnp.abs(x)))


def _dot(a, b, dims, precision=None):
    return lax.dot_general(a, b, (dims, ((), ())), preferred_element_type=F32, precision=precision)


def _nn(a, b, precision=None):
    return _dot(a, b, ((1,), (0,)), precision)


def _nt(a, b, precision=None):
    return _dot(a, b, ((1,), (1,)), precision)


def _tn(a, b, precision=None):
    return _dot(a, b, ((0,), (0,)), precision)


def _bf(x):
    return x.astype(BF16)


def _split(x, terms):
    parts = []
    for _ in range(terms):
        part = _bf(x)
        parts.append(part)
        x = x - part.astype(F32)
    return parts


def _pick_right(x, pick, terms):
    pick = _bf(pick)
    out = None
    for part in _split(x, terms):
        out = _nn(part, pick) if out is None else out + _nn(part, pick)
    return out


def _pick_left(pick, x, terms):
    pick = _bf(pick)
    out = None
    for part in _split(x, terms):
        out = _nn(pick, part) if out is None else out + _nn(pick, part)
    return out


def _lane_col(x, lane):
    idx = lax.broadcasted_iota(jnp.int32, x.shape, 1)
    return jnp.sum(jnp.where(idx == lane, x, 0.0), axis=1, keepdims=True)


def _tri(n, upper):
    r = lax.broadcasted_iota(jnp.int32, (n, n), 0)
    c = lax.broadcasted_iota(jnp.int32, (n, n), 1)
    return jnp.where((r <= c) if upper else (r >= c), 1.0, 0.0).astype(F32)


def _eye(n):
    return jnp.where(lax.broadcasted_iota(jnp.int32, (n, n), 0) == lax.broadcasted_iota(jnp.int32, (n, n), 1), 1.0, 0.0)


def _sum_all(x):
    return jnp.sum(jnp.sum(x, axis=1, keepdims=True), axis=0, keepdims=True)


def _crossing(p):
    L = p.shape[0]
    below = _nn(_bf(_tri(L, True)), _bf(p))
    strict = lax.broadcasted_iota(jnp.int32, (L, L), 0) > lax.broadcasted_iota(jnp.int32, (L, L), 1)
    return [jnp.sum(jnp.where(strict, below[:, b * L:(b + 1) * L], 0.0), axis=1, keepdims=True)
            for b in range(p.shape[1] // L)]


def _matmul_bias(a, w, bias, tm, tn, col0, ncols, dtype):
    m, k = a.shape
    j0 = col0 // tn

    def body(a_ref, w_ref, b_ref, o_ref):
        o_ref[...] = (_nn(a_ref[...], w_ref[...]) + b_ref[...]).astype(dtype)

    return pl.pallas_call(
        body, name="matmul_bias", grid=(m // tm, ncols // tn),
        in_specs=[pl.BlockSpec((tm, k), lambda i, j: (i, 0)), pl.BlockSpec((k, tn), lambda i, j: (0, j0 + j)),
                  pl.BlockSpec((1, tn), lambda i, j: (0, j0 + j))],
        out_specs=pl.BlockSpec((tm, tn), lambda i, j: (i, j)),
        out_shape=jax.ShapeDtypeStruct((m, ncols), dtype),
        compiler_params=_cparams("parallel", "arbitrary"))(a, w, bias)


def _matmul_nt(a, w, tm, tk, after=None):
    m, n = a.shape
    k = w.shape[0]

    def body(a_ref, w_ref, *rest):
        o_ref = rest[-1]

        @pl.when(pl.program_id(1) == 0)
        def _():
            o_ref[...] = jnp.zeros_like(o_ref)
        o_ref[...] += _nt(a_ref[...], w_ref[...])

    extra = [] if after is None else [after]
    return pl.pallas_call(
        body, name="matmul_nt", grid=(m // tm, n // tk),
        in_specs=[pl.BlockSpec((tm, tk), lambda i, j: (i, j)), pl.BlockSpec((k, tk), lambda i, j: (0, j))]
        + [pl.BlockSpec(memory_space=pl.ANY)] * len(extra),
        out_specs=pl.BlockSpec((tm, k), lambda i, j: (i, 0)),
        out_shape=jax.ShapeDtypeStruct((m, k), F32),
        compiler_params=_cparams("parallel", "arbitrary"))(a, w, *extra)


def _matmul_tn(a, b, tm, tn, with_colsum=False, a_is_transposed=False):
    k, m = a.shape if a_is_transposed else a.shape[::-1]
    n = b.shape[1]

    def body(a_ref, b_ref, o_ref, *rest):
        first = pl.program_id(1) == 0

        @pl.when(first)
        def _():
            o_ref[...] = jnp.zeros_like(o_ref)
        o_ref[...] += _nn(a_ref[...], b_ref[...]) if a_is_transposed else _tn(a_ref[...], b_ref[...])
        if with_colsum:
            s_ref = rest[0]

            @pl.when(first)
            def _():
                s_ref[...] = jnp.zeros_like(s_ref)
            s_ref[...] += jnp.sum(b_ref[...].astype(F32), axis=0, keepdims=True)

    out_specs = [pl.BlockSpec((k, tn), lambda j, i: (0, j))]
    out_shape = [jax.ShapeDtypeStruct((k, n), F32)]
    if with_colsum:
        out_specs.append(pl.BlockSpec((1, tn), lambda j, i: (0, j)))
        out_shape.append(jax.ShapeDtypeStruct((1, n), F32))
    out = pl.pallas_call(
        body, name="matmul_tn", grid=(n // tn, m // tm),
        in_specs=[pl.BlockSpec((k, tm), lambda j, i: (0, i)) if a_is_transposed else pl.BlockSpec((tm, k), lambda j, i: (i, 0)),
                  pl.BlockSpec((tm, tn), lambda j, i: (i, j))],
        out_specs=out_specs, out_shape=out_shape,
        compiler_params=_cparams("parallel", "arbitrary"))(a, b)
    return out if with_colsum else out[0]


def _ada_fwd(c_all, ada_w, ada_b):
    def body(c_ref, w_ref, b_ref, o_ref):
        o_ref[...] = _nn(_bf(_silu(c_ref[...])), _bf(w_ref[...])) + b_ref[...]

    return pl.pallas_call(body, name="ada_fwd", out_shape=jax.ShapeDtypeStruct((c_all.shape[0], ada_w.shape[1]), F32),
                          compiler_params=_cparams())(c_all, ada_w, ada_b)


def _ada_bwd(c_all, dmod):
    def body(c_ref, d_ref, o_ref):
        o_ref[...] = _tn(_bf(_silu(c_ref[...])), _bf(d_ref[...]))

    return pl.pallas_call(body, name="ada_bwd", out_shape=jax.ShapeDtypeStruct((c_all.shape[1], dmod.shape[1]), F32),
                          compiler_params=_cparams())(c_all, dmod)


def _prenorm_fwd(x, norm_w, scale, shift, ts):
    s, d = x.shape

    def body(x_ref, nw_ref, sc_ref, sh_ref, u_ref):
        xv = x_ref[...]
        r = lax.rsqrt(jnp.mean(xv * xv, axis=1, keepdims=True) + EPS)
        u_ref[...] = _bf(xv * r * nw_ref[...] * (1.0 + sc_ref[...]) + sh_ref[...])

    row = pl.BlockSpec((1, d), lambda i: (0, 0))
    return pl.pallas_call(
        body, name="prenorm_fwd", grid=(s // ts,),
        in_specs=[pl.BlockSpec((ts, d), lambda i: (i, 0)), row, row, row],
        out_specs=pl.BlockSpec((ts, d), lambda i: (i, 0)), out_shape=jax.ShapeDtypeStruct((s, d), BF16),
        compiler_params=_cparams("parallel"))(x, norm_w, scale, shift)


def _prenorm_bwd(du, x, dxres, norm_w, scale, ts):
    s, d = x.shape

    def body(du_ref, x_ref, dr_ref, nw_ref, sc_ref, gx_ref, acc_ref):
        @pl.when(pl.program_id(0) == 0)
        def _():
            acc_ref[...] = jnp.zeros_like(acc_ref)
        xv, duv = x_ref[...], du_ref[...]
        r = lax.rsqrt(jnp.mean(xv * xv, axis=1, keepdims=True) + EPS)
        xn = xv * r
        nw, sc1 = nw_ref[...], 1.0 + sc_ref[...]
        dxn = duv * (nw * sc1)
        gx_ref[...] = r * (dxn - xn * jnp.mean(dxn * xn, axis=1, keepdims=True)) + dr_ref[...]
        t = duv * xn
        acc_ref[0:1, :] += jnp.sum(t, axis=0, keepdims=True) * sc1
        acc_ref[1:2, :] += jnp.sum(t, axis=0, keepdims=True) * nw
        acc_ref[2:3, :] += jnp.sum(duv, axis=0, keepdims=True)

    tile = pl.BlockSpec((ts, d), lambda i: (i, 0))
    row = pl.BlockSpec((1, d), lambda i: (0, 0))
    return pl.pallas_call(
        body, name="prenorm_bwd", grid=(s // ts,),
        in_specs=[tile, tile, tile, row, row],
        out_specs=[tile, pl.BlockSpec((8, d), lambda i: (0, 0))],
        out_shape=[jax.ShapeDtypeStruct((s, d), F32), jax.ShapeDtypeStruct((8, d), F32)],
        compiler_params=_cparams("arbitrary"))(du, x, dxres, norm_w, scale)


CONV_CB = 512


def _conv_taps(buf_ref, ts):
    return [buf_ref[pl.ds(8 - (CONV_K - 1) + j, ts), :] for j in range(CONV_K)]


def _conv_fwd(proj, col0, width, w8, b, ts):
    s = proj.shape[0]
    cb = CONV_CB
    nt = s // ts

    def body(x_ref, w_ref, b_ref, o_ref, ds_ref, buf_ref):
        @pl.when(pl.program_id(1) == 0)
        def _():
            buf_ref[0:8, :] = jnp.zeros((8, cb), F32)
        buf_ref[pl.ds(8, ts), :] = x_ref[...].astype(F32)
        acc = b_ref[...] + jnp.zeros((ts, cb), F32)
        for j, tap in enumerate(_conv_taps(buf_ref, ts)):
            acc = acc + tap * w_ref[j:j + 1, :]
        sg = _sigmoid(acc)
        o_ref[...] = acc * sg
        ds_ref[...] = _bf(sg + acc * sg * (1.0 - sg))
        buf_ref[0:8, :] = buf_ref[pl.ds(ts, 8), :]

    c0 = col0 // cb
    tile = pl.BlockSpec((ts, cb), lambda c, i: (i, c))
    return pl.pallas_call(
        body, name="conv_fwd", grid=(width // cb, nt),
        in_specs=[pl.BlockSpec((ts, cb), lambda c, i: (i, c0 + c)), pl.BlockSpec((8, cb), lambda c, i: (0, c)),
                  pl.BlockSpec((1, cb), lambda c, i: (0, c))],
        out_specs=[tile, tile],
        out_shape=[jax.ShapeDtypeStruct((s, width), F32), jax.ShapeDtypeStruct((s, width), BF16)],
        scratch_shapes=[pltpu.VMEM((ts + 8, cb), F32)],
        compiler_params=_cparams("parallel", "arbitrary"))(proj, w8, b)


def _conv_bwd(proj, col0, width, w8, dact, dpost, dproj, ts):
    s = proj.shape[0]
    cb = CONV_CB
    nt = s // ts
    c0 = col0 // cb

    def body(x_ref, da_ref, dp_ref, w_ref, _, dx_ref, acc_ref, dbuf_ref):
        @pl.when(pl.program_id(1) == 0)
        def _():
            acc_ref[...] = jnp.zeros_like(acc_ref)
            dbuf_ref[pl.ds(ts, 8), :] = jnp.zeros((8, cb), F32)
        dconv = dp_ref[...].astype(F32) * da_ref[...].astype(F32)
        acc_ref[CONV_K:CONV_K + 1, :] += jnp.sum(dconv, axis=0, keepdims=True)
        dbuf_ref[pl.ds(0, ts), :] = dconv
        xv = x_ref[...].astype(F32)
        dx = jnp.zeros((ts, cb), F32)
        for j in range(CONV_K):
            shifted = dbuf_ref[pl.ds(CONV_K - 1 - j, ts), :]
            dx = dx + shifted * w_ref[j:j + 1, :]
            acc_ref[j:j + 1, :] += jnp.sum(xv * shifted, axis=0, keepdims=True)
        dx_ref[...] = _bf(dx)
        dbuf_ref[pl.ds(ts, 8), :] = dconv[0:8, :]

    tile = pl.BlockSpec((ts, cb), lambda c, i: (nt - 1 - i, c))
    wide = pl.BlockSpec((ts, cb), lambda c, i: (nt - 1 - i, c0 + c))
    return pl.pallas_call(
        body, name="conv_bwd", grid=(width // cb, nt),
        in_specs=[wide, tile, tile, pl.BlockSpec((8, cb), lambda c, i: (0, c)), pl.BlockSpec(memory_space=pl.ANY)],
        out_specs=[wide, pl.BlockSpec((8, cb), lambda c, i: (0, c))],
        out_shape=[jax.ShapeDtypeStruct(dproj.shape, dproj.dtype), jax.ShapeDtypeStruct((8, width), F32)],
        input_output_aliases={4: 0},
        scratch_shapes=[pltpu.VMEM((ts + 8, cb), F32)],
        compiler_params=_cparams("parallel", "arbitrary"))(proj, dact, dpost, w8, dproj)


def _mlstm_gates(gif_ref, gt_ref, a_scr, at_scr):
    L = gif_ref.shape[0]
    fb = _logsigmoid(gif_ref[...])
    a_scr[...] = _pick_left(_tri(L, False), fb, 3)
    at_scr[...] = _pick_right(_logsigmoid(gt_ref[...]), _tri(L, True), 3)
    return jnp.sum(fb, axis=0, keepdims=True)


def _mlstm_head(h, qk_ref, v_ref, gif, gt_ref, a, at_scr, a_last_row, c_mat, n_row, m_prev):
    L = gif.shape[0]
    q = qk_ref[:, h * ML_DQK:(h + 1) * ML_DQK] * (ML_DQK ** -0.5)
    k = qk_ref[:, (ML_HEADS + h) * ML_DQK:(ML_HEADS + h + 1) * ML_DQK]
    v = v_ref[:, h * ML_DV:(h + 1) * ML_DV]
    i_col, a_col = _lane_col(gif, h), _lane_col(a, ML_HEADS + h)
    i_row, a_row = gt_ref[h:h + 1, :], at_scr[ML_HEADS + h:ML_HEADS + h + 1, :]
    causal = lax.broadcasted_iota(jnp.int32, (L, L), 0) >= lax.broadcasted_iota(jnp.int32, (L, L), 1)
    dmat = jnp.where(causal, a_col - a_row + i_row, NEG)
    inter = a_col + m_prev
    m_t = jnp.maximum(inter, jnp.max(dmat, axis=1, keepdims=True))
    w_intra = jnp.exp(dmat - m_t)
    w_inter = jnp.exp(inter - m_t)
    sc = _nt(_bf(q), _bf(k)) * w_intra
    den = jnp.sum(sc, axis=1, keepdims=True) + w_inter * jnp.sum(q * n_row, axis=1, keepdims=True)
    floor = jnp.exp(-m_t)
    a_last = _lane_col(a_last_row, ML_HEADS + h)
    g = a_last - a_col + i_col
    m_new = jnp.maximum(a_last + m_prev, jnp.max(g, axis=0, keepdims=True))
    wk = jnp.exp(g - m_new)
    decay = jnp.exp(a_last + m_prev - m_new)
    return dict(q=q, k=k, v=v, w_intra=w_intra, w_inter=w_inter, sc=sc, den=den, floor=floor, m_new=m_new, wk=wk,
                decay=decay)


def _state_tile(n_row, m11):
    r = lax.broadcasted_iota(jnp.int32, (8, LANE), 0)
    return jnp.where(r == 0, n_row, jnp.where(r == 1, m11, 0.0))


def _mlstm_fwd(qk, proj, gates, gt):
    s = qk.shape[0]
    L = CHUNK
    nc = s // L

    def body(qk_ref, v_ref, gif_ref, gt_ref, h_ref, cst_ref, nm_ref, c_scr, nm_scr, a_scr, at_scr):
        @pl.when(pl.program_id(0) == 0)
        def _():
            c_scr[...] = jnp.zeros_like(c_scr)
            nm_scr[...] = jnp.zeros_like(nm_scr)
        a_last_row = _mlstm_gates(gif_ref, gt_ref, a_scr, at_scr)
        gif, a = gif_ref[...], a_scr[...]
        for h in range(ML_HEADS):
            c_mat, n_row = c_scr[h], nm_scr[h, 0:1, :]
            m_prev = jnp.max(nm_scr[h, 1:2, :], axis=1, keepdims=True)
            cst_ref[0, h] = c_mat
            nm_ref[0, h] = nm_scr[h]
            t = _mlstm_head(h, qk_ref, v_ref, gif, gt_ref, a, at_scr, a_last_row, c_mat, n_row, m_prev)
            num = _nn(_bf(t["sc"]), _bf(t["v"])) + t["w_inter"] * _nn(_bf(t["q"]), _bf(c_mat))
            h_ref[:, h * ML_DV:(h + 1) * ML_DV] = _bf(num / jnp.maximum(jnp.abs(t["den"]), t["floor"]))
            kw = t["k"] * t["wk"]
            c_scr[h] = t["decay"] * c_mat + _tn(_bf(kw), _bf(t["v"]))
            nm_scr[h] = _state_tile(t["decay"] * n_row + jnp.sum(kw, axis=0, keepdims=True), t["m_new"])

    return pl.pallas_call(
        body, name="mlstm_fwd", grid=(nc,),
        in_specs=[pl.BlockSpec((L, 2048), lambda c: (c, 0)), pl.BlockSpec((L, 2048), lambda c: (c, O_V // 2048)),
                  pl.BlockSpec((L, LANE), lambda c: (c, 0)), pl.BlockSpec((LANE, L), lambda c: (0, c))],
        out_specs=[pl.BlockSpec((L, 2048), lambda c: (c, 0)),
                   pl.BlockSpec((1, ML_HEADS, ML_DQK, ML_DV), lambda c: (c, 0, 0, 0)),
                   pl.BlockSpec((1, ML_HEADS, 8, LANE), lambda c: (c, 0, 0, 0))],
        out_shape=[jax.ShapeDtypeStruct((s, 2048), BF16), jax.ShapeDtypeStruct((nc, ML_HEADS, ML_DQK, ML_DV), F32),
                   jax.ShapeDtypeStruct((nc, ML_HEADS, 8, LANE), F32)],
        scratch_shapes=[pltpu.VMEM((ML_HEADS, ML_DQK, ML_DV), F32), pltpu.VMEM((ML_HEADS, 8, LANE), F32),
                        pltpu.VMEM((L, LANE), F32), pltpu.VMEM((LANE, L), F32)],
        compiler_params=_cparams("arbitrary"))(qk, proj, gates, gt)


def _mlstm_bwd(qk, proj, gates, gt, hout, dh, cst, nm, dproj):
    s = qk.shape[0]
    L = CHUNK
    nc = s // L

    def body(qk_ref, v_ref, gif_ref, gt_ref, h_ref, dh_ref, cst_ref, nm_ref, _, dqk_ref, dv_ref, dif_ref,
             dc_scr, dn_scr, a_scr, at_scr):
        @pl.when(pl.program_id(0) == 0)
        def _():
            dc_scr[...] = jnp.zeros_like(dc_scr)
            dn_scr[...] = jnp.zeros_like(dn_scr)
        a_last_row = _mlstm_gates(gif_ref, gt_ref, a_scr, at_scr)
        gif, a = gif_ref[...], a_scr[...]
        lane = lax.broadcasted_iota(jnp.int32, (L, LANE), 1)
        last = lax.broadcasted_iota(jnp.int32, (L, 1), 0) == L - 1
        di_tile = jnp.zeros((L, LANE), F32)
        cross = [jnp.zeros((L, LANE), F32)] * 3
        dlogw = []
        for h in range(ML_HEADS):
            c_mat, n_row = cst_ref[0, h], nm_ref[0, h, 0:1, :]
            m_prev = jnp.max(nm_ref[0, h, 1:2, :], axis=1, keepdims=True)
            t = _mlstm_head(h, qk_ref, v_ref, gif, gt_ref, a, at_scr, a_last_row, c_mat, n_row, m_prev)
            q, k, v, den = t["q"], t["k"], t["v"], t["den"]
            dhh = dh_ref[:, h * ML_DV:(h + 1) * ML_DV].astype(F32)
            hh = h_ref[:, h * ML_DV:(h + 1) * ML_DV].astype(F32)
            dnorm = jnp.maximum(jnp.abs(den), t["floor"])
            dnum = dhh / dnorm
            d_dn = -jnp.sum(dhh * hh, axis=1, keepdims=True) / dnorm
            dden = jnp.where(jnp.abs(den) >= t["floor"], jnp.where(den >= 0.0, d_dn, -d_dn), 0.0)
            dsc = _nt(_bf(dnum), _bf(v)) + dden
            ds = dsc * t["w_intra"]
            dq_inter = t["w_inter"] * (_nt(_bf(dnum), _bf(c_mat)) + dden * n_row)
            dq = _nn(_bf(ds), _bf(k)) + dq_inter
            dc, dn_row = dc_scr[h], dn_scr[h, 0:1, :]
            dk_state = t["wk"] * (_nt(_bf(v), _bf(dc)) + dn_row)
            dk = _tn(_bf(ds), _bf(q)) + dk_state
            dv = _tn(_bf(t["sc"]), _bf(dnum)) + t["wk"] * _nn(_bf(k), _bf(dc))
            qi = q * t["w_inter"]
            dc_scr[h] = t["decay"] * dc + _tn(_bf(qi), _bf(dnum))
            dn_scr[h] = jnp.broadcast_to(t["decay"] * dn_row + jnp.sum(qi * dden, axis=0, keepdims=True), (8, LANE))
            dqk_ref[:, h * ML_DQK:(h + 1) * ML_DQK] = _bf(dq * (ML_DQK ** -0.5))
            dqk_ref[:, (ML_HEADS + h) * ML_DQK:(ML_HEADS + h + 1) * ML_DQK] = _bf(dk)
            dv_ref[:, h * ML_DV:(h + 1) * ML_DV] = _bf(dv)
            di_tile = di_tile + jnp.where(lane == h, jnp.sum(k * dk, axis=1, keepdims=True), 0.0)
            carried = t["decay"] * (_sum_all(dc * c_mat) + jnp.sum(dn_row * n_row, axis=1, keepdims=True))
            dlogw.append(dsc * t["sc"])
            parts = (jnp.sum(q * dq_inter, axis=1, keepdims=True) + jnp.where(last, carried, 0.0),
                     jnp.sum(k * dk_state, axis=1, keepdims=True))
            cross[1:] = [c + jnp.where(lane == ML_HEADS + h, p, 0.0) for c, p in zip(cross[1:], parts)]
        for h, col in enumerate(_crossing(jnp.concatenate(dlogw, axis=1))):
            cross[0] = cross[0] + jnp.where(lane == ML_HEADS + h, col, 0.0)
        dfb = cross[0] + _pick_left(_tri(L, True), cross[1], 2) + _pick_left(_tri(L, False) - _eye(L), cross[2], 2)
        dif_ref[:, 0:LANE] = _bf(di_tile + dfb * _sigmoid(-gif))
        dif_ref[:, LANE:SMALL_W] = jnp.zeros((L, SMALL_W - LANE), BF16)

    rev = lambda c: nc - 1 - c
    return pl.pallas_call(
        body, name="mlstm_bwd", grid=(nc,),
        in_specs=[pl.BlockSpec((L, 2048), lambda c: (rev(c), 0)), pl.BlockSpec((L, 2048), lambda c: (rev(c), O_V // 2048)),
                  pl.BlockSpec((L, LANE), lambda c: (rev(c), 0)), pl.BlockSpec((LANE, L), lambda c: (0, rev(c))),
                  pl.BlockSpec((L, 2048), lambda c: (rev(c), 0)), pl.BlockSpec((L, 2048), lambda c: (rev(c), 0)),
                  pl.BlockSpec((1, ML_HEADS, ML_DQK, ML_DV), lambda c: (rev(c), 0, 0, 0)),
                  pl.BlockSpec((1, ML_HEADS, 8, LANE), lambda c: (rev(c), 0, 0, 0)), pl.BlockSpec(memory_space=pl.ANY)],
        out_specs=[pl.BlockSpec((L, 2048), lambda c: (rev(c), 0)), pl.BlockSpec((L, 2048), lambda c: (rev(c), O_V // 2048)),
                   pl.BlockSpec((L, SMALL_W), lambda c: (rev(c), 0))],
        out_shape=[jax.ShapeDtypeStruct((s, 2048), BF16), jax.ShapeDtypeStruct(dproj.shape, dproj.dtype),
                   jax.ShapeDtypeStruct((s, SMALL_W), BF16)],
        input_output_aliases={8: 1},
        scratch_shapes=[pltpu.VMEM((ML_HEADS, ML_DQK, ML_DV), F32), pltpu.VMEM((ML_HEADS, 8, LANE), F32),
                        pltpu.VMEM((L, LANE), F32), pltpu.VMEM((LANE, L), F32)],
        compiler_params=_cparams("arbitrary"))(qk, proj, gates, gt, hout, dh, cst, nm, dproj)


GROUP_W = SSM_HEADS // SSM_GROUPS * SSM_HEADDIM
O_B = SSM_HEADS * SSM_HEADDIM
O_C = O_B + SSM_GROUPS * SSM_STATE


def _head_expand():
    r = jnp.arange(LANE)[:, None]
    c = jnp.arange(SSM_HEADS * SSM_HEADDIM)[None, :] // SSM_HEADDIM
    return (r == c).astype(F32)


def _ssd_gates(dt_ref, dtt_ref, alog_row_ref, alog_col_ref, at_scr):
    L = dt_ref.shape[0]
    dt = _softplus(dt_ref[...])
    acoef = -jnp.exp(alog_row_ref[...])
    a = _pick_left(_tri(L, False), dt * acoef, 3)
    at_scr[...] = _pick_right(_softplus(dtt_ref[...]) * (-jnp.exp(alog_col_ref[...])), _tri(L, True), 3)
    return dt, acoef, a


def _ssd_group(g, xbc_ref, dt, a, e_ref, ax_scr):
    eg = e_ref[:, g * GROUP_W:(g + 1) * GROUP_W]
    ax_scr[...] = _pick_right(a, eg, 3)
    ax = ax_scr[...]
    alx = ax_scr[ax.shape[0] - 1:ax.shape[0], :]
    dtx = _pick_right(dt, eg, 2)
    xg = xbc_ref[:, g * GROUP_W:(g + 1) * GROUP_W]
    bg = xbc_ref[:, O_B + g * SSM_STATE:O_B + (g + 1) * SSM_STATE]
    cg = xbc_ref[:, O_C + g * SSM_STATE:O_C + (g + 1) * SSM_STATE]
    return dict(ax=ax, alx=alx, dtx=dtx, xg=xg, bg=bg, cg=cg, xdt=xg * dtx, gmat=_nt(_bf(cg), _bf(bg)))


def _ssd_decay(hh, a, at_scr):
    L = a.shape[0]
    causal = lax.broadcasted_iota(jnp.int32, (L, L), 0) >= lax.broadcasted_iota(jnp.int32, (L, L), 1)
    return jnp.exp(jnp.where(causal, _lane_col(a, hh) - at_scr[hh:hh + 1, :], NEG))


def _ssd_fwd(xbc, gates, dtt, alog_row, alog_col, dskip_x, expand):
    s = xbc.shape[0]
    L = CHUNK
    nc = s // L
    half = SSM_HEADDIM

    def body(xbc_ref, dt_ref, dtt_ref, ar_ref, ac_ref, dk_ref, e_ref, y_ref, st_ref, st_scr, at_scr, ax_scr):
        @pl.when(pl.program_id(0) == 0)
        def _():
            st_scr[...] = jnp.zeros_like(st_scr)
        dt, _, a = _ssd_gates(dt_ref, dtt_ref, ar_ref, ac_ref, at_scr)
        lane = lax.broadcasted_iota(jnp.int32, (L, LANE), 1)
        for g in range(SSM_GROUPS):
            t = _ssd_group(g, xbc_ref, dt, a, e_ref, ax_scr)
            st = st_scr[g]
            st_ref[0, g] = st
            pairs = []
            for j in range(GROUP_W // LANE):
                xp = _bf(t["xdt"][:, j * LANE:(j + 1) * LANE])
                hh = g * (SSM_HEADS // SSM_GROUPS) + 2 * j
                both = jnp.concatenate([_bf(t["gmat"] * _ssd_decay(hh, a, at_scr)),
                                        _bf(t["gmat"] * _ssd_decay(hh + 1, a, at_scr))], axis=0)
                ys = _nn(both, xp)
                pairs.append(jnp.where(lane < half, ys[0:L], ys[L:2 * L]))
            y = jnp.concatenate(pairs, axis=1) + _nn(_bf(t["cg"]), _bf(st)) * jnp.exp(t["ax"])
            y_ref[:, g * GROUP_W:(g + 1) * GROUP_W] = _bf(y + dk_ref[:, g * GROUP_W:(g + 1) * GROUP_W] * t["xg"])
            wts = jnp.exp(t["alx"] - t["ax"])
            st_scr[g] = jnp.exp(t["alx"]) * st + _tn(_bf(t["bg"]), _bf(t["xdt"] * wts))

    row = lambda w: pl.BlockSpec((1, w), lambda c: (0, 0))
    return pl.pallas_call(
        body, name="ssd_fwd", grid=(nc,),
        in_specs=[pl.BlockSpec((L, 3072), lambda c: (c, 0)), pl.BlockSpec((L, LANE), lambda c: (c, (O_DT - O_IF) // LANE)),
                  pl.BlockSpec((LANE, L), lambda c: (0, c)), row(LANE), pl.BlockSpec((LANE, 1), lambda c: (0, 0)),
                  row(2048), pl.BlockSpec((LANE, 2048), lambda c: (0, 0))],
        out_specs=[pl.BlockSpec((L, 2048), lambda c: (c, 0)),
                   pl.BlockSpec((1, SSM_GROUPS, SSM_STATE, GROUP_W), lambda c: (c, 0, 0, 0))],
        out_shape=[jax.ShapeDtypeStruct((s, 2048), BF16),
                   jax.ShapeDtypeStruct((nc, SSM_GROUPS, SSM_STATE, GROUP_W), F32)],
        scratch_shapes=[pltpu.VMEM((SSM_GROUPS, SSM_STATE, GROUP_W), F32), pltpu.VMEM((LANE, L), F32),
                        pltpu.VMEM((L, GROUP_W), F32)],
        compiler_params=_cparams("arbitrary"))(xbc, gates, dtt, alog_row, alog_col, dskip_x, expand)


def _ssd_bwd(xbc, gates, dtt, alog_row, alog_col, dskip_x, expand, expand_t, dy, states):
    s = xbc.shape[0]
    L = CHUNK
    nc = s // L
    half = SSM_HEADDIM

    def body(xbc_ref, dt_ref, dtt_ref, ar_ref, ac_ref, dk_ref, e_ref, et_ref, dy_ref, st_ref,
             dxbc_ref, ddt_ref, accd_ref, acca_ref, dst_scr, at_scr, ax_scr):
        @pl.when(pl.program_id(0) == 0)
        def _():
            dst_scr[...] = jnp.zeros_like(dst_scr)
            accd_ref[...] = jnp.zeros_like(accd_ref)
            acca_ref[...] = jnp.zeros_like(acca_ref)
        dt, acoef, a = _ssd_gates(dt_ref, dtt_ref, ar_ref, ac_ref, at_scr)
        lane = lax.broadcasted_iota(jnp.int32, (L, LANE), 1)
        low = lane < half
        last = lax.broadcasted_iota(jnp.int32, (L, 1), 0) == L - 1
        cross = [jnp.zeros((L, LANE), F32)] * 3
        ddt_tile = jnp.zeros((L, LANE), F32)
        for g in range(SSM_GROUPS):
            t = _ssd_group(g, xbc_ref, dt, a, e_ref, ax_scr)
            xg, bg, cg, xdt, gmat = t["xg"], t["bg"], t["cg"], t["xdt"], t["gmat"]
            st, dst = st_ref[0, g], dst_scr[g]
            dyg = dy_ref[:, g * GROUP_W:(g + 1) * GROUP_W].astype(F32)
            ea, eal = jnp.exp(t["ax"]), jnp.exp(t["alx"])
            wts = jnp.exp(t["alx"] - t["ax"])
            dyi = dyg * ea
            y_inter = _nn(_bf(cg), _bf(st)) * ea
            dc = _nt(_bf(dyi), _bf(st))
            d_xdt_state = _nn(_bf(bg), _bf(dst)) * wts
            db = _nt(_bf(xdt * wts), _bf(dst))
            dst_scr[g] = eal * dst + _tn(_bf(cg), _bf(dyi))
            dg = jnp.zeros((L, L), F32)
            dx_pairs, dlogw = [], []
            for j in range(GROUP_W // LANE):
                xp = _bf(xdt[:, j * LANE:(j + 1) * LANE])
                dyp = dyg[:, j * LANE:(j + 1) * LANE]
                hh = g * (SSM_HEADS // SSM_GROUPS) + 2 * j
                decs = [_ssd_decay(hh, a, at_scr), _ssd_decay(hh + 1, a, at_scr)]
                ws = [gmat * decs[0], gmat * decs[1]]
                dxs = _tn(_bf(jnp.concatenate(ws, axis=1)), _bf(dyp))
                dws = _nt(_bf(jnp.concatenate([jnp.where(low, dyp, 0.0), jnp.where(low, 0.0, dyp)], axis=0)), xp)
                dw0, dw1 = dws[0:L], dws[L:2 * L]
                dg = dg + dw0 * decs[0] + dw1 * decs[1]
                dlogw += [dw0 * ws[0], dw1 * ws[1]]
                dx_pairs.append(jnp.where(low, dxs[0:L], dxs[L:2 * L]))
            for b, col in enumerate(_crossing(jnp.concatenate(dlogw, axis=1))):
                cross[0] = cross[0] + jnp.where(lane == g * (SSM_HEADS // SSM_GROUPS) + b, col, 0.0)
            d_xdt = d_xdt_state + jnp.concatenate(dx_pairs, axis=1)
            dc = dc + _nn(_bf(dg), _bf(bg))
            db = db + _tn(_bf(dg), _bf(cg))
            etg = et_ref[g * GROUP_W:(g + 1) * GROUP_W, :]
            carried = jnp.sum(dst * st, axis=0, keepdims=True) * eal
            cross[1] = cross[1] + _pick_right(dyg * y_inter + jnp.where(last, carried, 0.0), etg, 2)
            cross[2] = cross[2] + _pick_right(xdt * d_xdt_state, etg, 2)
            ddt_tile = ddt_tile + _pick_right(d_xdt * xg, etg, 2)
            dxbc_ref[:, g * GROUP_W:(g + 1) * GROUP_W] = _bf(d_xdt * t["dtx"] + dk_ref[:, g * GROUP_W:(g + 1) * GROUP_W] * dyg)
            dxbc_ref[:, O_B + g * SSM_STATE:O_B + (g + 1) * SSM_STATE] = _bf(db)
            dxbc_ref[:, O_C + g * SSM_STATE:O_C + (g + 1) * SSM_STATE] = _bf(dc)
            accd_ref[0:1, g * GROUP_W:(g + 1) * GROUP_W] += jnp.sum(dyg * xg, axis=0, keepdims=True)
        d_da = cross[0] + _pick_left(_tri(L, True), cross[1], 2) + _pick_left(_tri(L, False) - _eye(L), cross[2], 2)
        acca_ref[0:1, :] += jnp.sum(d_da * dt, axis=0, keepdims=True)
        ddt_ref[:, 0:LANE] = _bf((ddt_tile + d_da * acoef) * _sigmoid(dt_ref[...]))
        ddt_ref[:, LANE:SMALL_W] = jnp.zeros((L, SMALL_W - LANE), BF16)

    rev = lambda c: nc - 1 - c
    row = lambda w: pl.BlockSpec((1, w), lambda c: (0, 0))
    return pl.pallas_call(
        body, name="ssd_bwd", grid=(nc,),
        in_specs=[pl.BlockSpec((L, 3072), lambda c: (rev(c), 0)), pl.BlockSpec((L, LANE), lambda c: (rev(c), (O_DT - O_IF) // LANE)),
                  pl.BlockSpec((LANE, L), lambda c: (0, rev(c))), row(LANE), pl.BlockSpec((LANE, 1), lambda c: (0, 0)),
                  row(2048), pl.BlockSpec((LANE, 2048), lambda c: (0, 0)), pl.BlockSpec((2048, LANE), lambda c: (0, 0)),
                  pl.BlockSpec((L, 2048), lambda c: (rev(c), 0)),
                  pl.BlockSpec((1, SSM_GROUPS, SSM_STATE, GROUP_W), lambda c: (rev(c), 0, 0, 0))],
        out_specs=[pl.BlockSpec((L, 3072), lambda c: (rev(c), 0)), pl.BlockSpec((L, SMALL_W), lambda c: (rev(c), 0)),
                   pl.BlockSpec((8, 2048), lambda c: (0, 0)), pl.BlockSpec((8, LANE), lambda c: (0, 0))],
        out_shape=[jax.ShapeDtypeStruct((s, 3072), BF16), jax.ShapeDtypeStruct((s, SMALL_W), BF16),
                   jax.ShapeDtypeStruct((8, 2048), F32), jax.ShapeDtypeStruct((8, LANE), F32)],
        scratch_shapes=[pltpu.VMEM((SSM_GROUPS, SSM_STATE, GROUP_W), F32),
                        pltpu.VMEM((LANE, L), F32), pltpu.VMEM((L, GROUP_W), F32)],
        compiler_params=_cparams("arbitrary"))(xbc, gates, dtt, alog_row, alog_col, dskip_x, expand, expand_t, dy, states)


def _group_norm(v, width):
    outs, rs = [], []
    for k in range(v.shape[1] // width):
        blk = v[:, k * width:(k + 1) * width]
        r = lax.rsqrt(jnp.mean(blk * blk, axis=1, keepdims=True) + EPS)
        outs.append(blk * r)
        rs.append(jnp.broadcast_to(r, blk.shape))
    return jnp.concatenate(outs, axis=1), jnp.concatenate(rs, axis=1)


def _group_mean(v, width):
    return jnp.concatenate([jnp.broadcast_to(jnp.mean(v[:, k * width:(k + 1) * width], axis=1, keepdims=True),
                                             (v.shape[0], width)) for k in range(v.shape[1] // width)], axis=1)


def _post_fwd(hm, yssd, proj, ml_norm_w, ssm_norm_w, ts):
    s = hm.shape[0]

    def body(h_ref, ys_ref, o_ref, zm_ref, zs_ref, wm_ref, ws_ref, ym_ref, yso_ref):
        hn, _ = _group_norm(h_ref[...].astype(F32), ML_DV)
        ym_ref[...] = _bf(_sigmoid(o_ref[...].astype(F32)) * hn * wm_ref[...] * _silu(zm_ref[...].astype(F32)))
        pn, _ = _group_norm(ys_ref[...].astype(F32) * _silu(zs_ref[...].astype(F32)), GROUP_W)
        yso_ref[...] = _bf(pn * ws_ref[...])

    tile = pl.BlockSpec((ts, 2048), lambda i: (i, 0))
    col = lambda off: pl.BlockSpec((ts, 2048), lambda i: (i, off // 2048))
    row = pl.BlockSpec((1, 2048), lambda i: (0, 0))
    return pl.pallas_call(
        body, name="post_fwd", grid=(s // ts,),
        in_specs=[tile, tile, col(O_O), col(O_ZM), col(O_ZS), row, row],
        out_specs=[tile, tile],
        out_shape=[jax.ShapeDtypeStruct((s, 2048), BF16)] * 2,
        compiler_params=_cparams("parallel"))(hm, yssd, proj, proj, proj, ml_norm_w, ssm_norm_w)


def _post_bwd(dym, dys, hm, yssd, proj, ml_norm_w, ssm_norm_w, dproj, ts):
    s = hm.shape[0]

    def body(dym_ref, dys_ref, h_ref, ys_ref, o_ref, zm_ref, zs_ref, wm_ref, ws_ref, _,
             dh_ref, dyssd_ref, dp_ref, acc_ref):
        @pl.when(pl.program_id(0) == 0)
        def _():
            acc_ref[...] = jnp.zeros_like(acc_ref)
        hn, r = _group_norm(h_ref[...].astype(F32), ML_DV)
        so, zm, wm, d_ym = _sigmoid(o_ref[...].astype(F32)), zm_ref[...].astype(F32), wm_ref[...], dym_ref[...].astype(F32)
        sz = _silu(zm)
        hnw = hn * wm
        dp_ref[:, O_O:O_O + 2048] = _bf(d_ym * hnw * sz * so * (1.0 - so))
        dp_ref[:, O_ZM:O_ZM + 2048] = _bf(d_ym * so * hnw * _dsilu(zm))
        dhnw = d_ym * so * sz
        acc_ref[0:1, :] += jnp.sum(dhnw * hn, axis=0, keepdims=True)
        dhn = dhnw * wm
        dh_ref[...] = _bf(r * (dhn - hn * _group_mean(dhn * hn, ML_DV)))
        ysv, zs, d_ys = ys_ref[...].astype(F32), zs_ref[...].astype(F32), dys_ref[...].astype(F32)
        szs = _silu(zs)
        pn, r2 = _group_norm(ysv * szs, GROUP_W)
        acc_ref[1:2, :] += jnp.sum(d_ys * pn, axis=0, keepdims=True)
        dpn = d_ys * ws_ref[...]
        dp = r2 * (dpn - pn * _group_mean(dpn * pn, GROUP_W))
        dyssd_ref[...] = _bf(dp * szs)
        dp_ref[:, O_ZS:O_ZS + 2048] = _bf(dp * ysv * _dsilu(zs))

    tile = pl.BlockSpec((ts, 2048), lambda i: (i, 0))
    col = lambda off: pl.BlockSpec((ts, 2048), lambda i: (i, off // 2048))
    row = pl.BlockSpec((1, 2048), lambda i: (0, 0))
    sds = lambda dt: jax.ShapeDtypeStruct((s, 2048), dt)
    return pl.pallas_call(
        body, name="post_bwd", grid=(s // ts,),
        in_specs=[tile, tile, tile, tile, col(O_O), col(O_ZM), col(O_ZS), row, row, pl.BlockSpec(memory_space=pl.ANY)],
        out_specs=[tile, tile, pl.BlockSpec((ts, O_MG), lambda i: (i, 0)), pl.BlockSpec((8, 2048), lambda i: (0, 0))],
        out_shape=[sds(BF16), sds(BF16), jax.ShapeDtypeStruct(dproj.shape, dproj.dtype), jax.ShapeDtypeStruct((8, 2048), F32)],
        input_output_aliases={9: 2},
        compiler_params=_cparams("arbitrary"))(dym, dys, hm, yssd, proj, proj, proj, ml_norm_w, ssm_norm_w, dproj)


def _merge(x, ym, ys, proj, target, gate, final_w, wpm, wps, wo, ts):
    wpm_t, wps_t, wo_t = wpm.T, wps.T, wo.T
    s, d = x.shape

    def body(x_ref, ym_ref, ys_ref, mg_ref, t_ref, gate_ref, fw_ref, wpm_ref, wps_ref, wo_ref, wpmt_ref, wpst_ref, wot_ref,
             dres_ref, mer_ref, dmo_ref, dpm_ref, dps_ref, dym_ref, dys_ref, dmg_ref, acc_ref):
        @pl.when(pl.program_id(0) == 0)
        def _():
            acc_ref[...] = jnp.zeros_like(acc_ref)
        gm, gs = _sigmoid(mg_ref[:, 0:d].astype(F32)), _sigmoid(mg_ref[:, d:2 * d].astype(F32))
        pm = _nn(ym_ref[...], wpm_ref[...])
        ps = _nn(ys_ref[...], wps_ref[...])
        merged = _bf(gm * pm + gs * ps)
        mer_ref[...] = merged
        mo = _nn(merged, wo_ref[...])
        gate, fw = gate_ref[...], fw_ref[...]
        out = x_ref[...] + gate * mo
        r = lax.rsqrt(jnp.mean(out * out, axis=1, keepdims=True) + EPS)
        on = out * r
        diff = on * fw - t_ref[...]
        acc_ref[0:1, :] += jnp.sum(0.5 * jnp.sum(diff * diff, axis=1, keepdims=True) / d, axis=0, keepdims=True)
        dyv = diff * (1.0 / d)
        acc_ref[1:2, :] += jnp.sum(dyv * on, axis=0, keepdims=True)
        don = dyv * fw
        dout = r * (don - on * jnp.mean(don * on, axis=1, keepdims=True))
        dres_ref[...] = dout
        acc_ref[2:3, :] += jnp.sum(dout * mo, axis=0, keepdims=True)
        dmo = _bf(dout * gate)
        dmo_ref[...] = dmo
        dmer = _nn(dmo, wot_ref[...])
        dpm, dps = _bf(dmer * gm), _bf(dmer * gs)
        dpm_ref[...] = dpm
        dps_ref[...] = dps
        dmg_ref[:, 0:d] = _bf(dmer * pm * gm * (1.0 - gm))
        dmg_ref[:, d:2 * d] = _bf(dmer * ps * gs * (1.0 - gs))
        dym_ref[...] = _bf(_nn(dpm, wpmt_ref[...]))
        dys_ref[...] = _bf(_nn(dps, wpst_ref[...]))

    t1 = pl.BlockSpec((ts, d), lambda i: (i, 0))
    t2 = pl.BlockSpec((ts, 2 * d), lambda i: (i, 0))
    row = pl.BlockSpec((1, d), lambda i: (0, 0))
    whole = pl.BlockSpec(memory_space=pltpu.VMEM)
    sd = lambda w, dt: jax.ShapeDtypeStruct((s, w), dt)
    return pl.pallas_call(
        body, name="merge_fwd_bwd", grid=(s // ts,),
        in_specs=[t1, t2, t2, pl.BlockSpec((ts, 2 * d), lambda i: (i, O_MG // (2 * d))), t1, row, row] + [whole] * 6,
        out_specs=[t1, t1, t1, t1, t1, t2, t2, pl.BlockSpec((ts, 2 * d), lambda i: (i, O_MG // (2 * d))),
                   pl.BlockSpec((8, d), lambda i: (0, 0))],
        out_shape=[sd(d, F32), sd(d, BF16), sd(d, BF16), sd(d, BF16), sd(d, BF16), sd(2 * d, BF16), sd(2 * d, BF16),
                   sd(NP, BF16), jax.ShapeDtypeStruct((8, d), F32)],
        compiler_params=_cparams("arbitrary", vmem=MERGE_VMEM))(x, ym, ys, proj, target, gate, final_w, wpm, wps, wo, wpm_t, wps_t, wo_t)


def _adamw(w, g, m, v, tr):
    if w.ndim == 2 and w.shape[0] % 8:
        tile, steps = pl.BlockSpec((w.shape[0], tr), lambda i: (0, i)), w.shape[1] // tr
    else:
        lead = (None,) * (w.ndim - 2)
        tile, steps = pl.BlockSpec(lead + (tr, w.shape[-1]), lambda i: (0,) * len(lead) + (i, 0)), w.shape[-2] // tr

    def body(w_ref, g_ref, m_ref, v_ref, d_ref, nm_ref, nv_ref):
        gv = g_ref[...]
        m2 = ADAM_B1 * m_ref[...] + (1.0 - ADAM_B1) * gv
        v2 = ADAM_B2 * v_ref[...] + (1.0 - ADAM_B2) * (gv * gv)
        m_hat = m2 / (1.0 - ADAM_B1 ** ADAM_STEP)
        v_hat = v2 / (1.0 - ADAM_B2 ** ADAM_STEP)
        d_ref[...] = -ADAM_LR * (m_hat / (jnp.sqrt(v_hat) + ADAM_EPS) + ADAM_WD * w_ref[...])
        nm_ref[...] = m2
        nv_ref[...] = v2

    return pl.pallas_call(
        body, name="adamw", grid=(steps,), in_specs=[tile] * 4, out_specs=[tile] * 3,
        out_shape=[jax.ShapeDtypeStruct(w.shape, F32)] * 3,
        compiler_params=_cparams("parallel"))(w, g.reshape(w.shape), m, v)


def _sum_parts(own, parts, tr, dtype=F32):
    p, rows, cols = parts.shape

    def body(*refs):
        p_ref, o_ref = refs[-2], refs[-1]
        acc = p_ref[0].astype(F32) if own is None else refs[0][...].astype(F32) + p_ref[0].astype(F32)
        for i in range(1, p):
            acc = acc + p_ref[i].astype(F32)
        o_ref[...] = acc.astype(dtype)

    tile = pl.BlockSpec((tr, cols), lambda i: (i, 0))
    ins = ([] if own is None else [tile]) + [pl.BlockSpec((p, tr, cols), lambda i: (0, i, 0))]
    args = ([] if own is None else [own]) + [parts]
    return pl.pallas_call(
        body, name="sum_parts", grid=(rows // tr,), in_specs=ins, out_specs=tile,
        out_shape=jax.ShapeDtypeStruct((rows, cols), dtype), compiler_params=_cparams("parallel"))(*args)


def _position():
    return lax.axis_index("x"), lax.axis_index("y"), lax.axis_index("c")


def _flip(pos, k):
    return tuple(1 - p if (k >> s) & 1 else p for p, s in zip(pos, (2, 1, 0)))


def _allgather8(block):
    rows, cols = block.shape

    def body(x_ref, o_ref, send_sems, recv_sems, local_sem):
        pos = _position()
        me = 4 * pos[0] + 2 * pos[1] + pos[2]
        mine = pltpu.make_async_copy(x_ref, o_ref.at[me], local_sem)
        mine.start()
        copies = [pltpu.make_async_remote_copy(src_ref=x_ref, dst_ref=o_ref.at[me], send_sem=send_sems.at[k - 1],
                                               recv_sem=recv_sems.at[k - 1], device_id=_flip(pos, k), device_id_type=MESH)
                  for k in range(1, N_DEV)]
        for cp in copies:
            cp.start()
        for cp in copies:
            cp.wait()
        mine.wait()

    vmem = pl.BlockSpec(memory_space=pltpu.VMEM)
    return pl.pallas_call(
        body, name="allgather8", in_specs=[vmem], out_specs=vmem,
        out_shape=jax.ShapeDtypeStruct((N_DEV, rows, cols), block.dtype),
        scratch_shapes=[pltpu.SemaphoreType.DMA((N_DEV - 1,)), pltpu.SemaphoreType.DMA((N_DEV - 1,)),
                        pltpu.SemaphoreType.DMA],
        compiler_params=pltpu.CompilerParams(vmem_limit_bytes=VMEM_LIMIT))(block)


COPY_BYTES = 1 << 20


def _row_chunks(rows, row_bytes):
    n = max(1, min(rows // 16, -(-rows * row_bytes // COPY_BYTES)))
    while rows % (16 * n):
        n -= 1
    return [(i * (rows // n), rows // n) for i in range(n)]


def _split_start(name, make_copies, n_copies, sources, lands):
    n, m = len(sources), len(lands)

    def body(*refs):
        for cp in make_copies(_position(), refs[:n], refs[n:n + m], refs[n + m], refs[n + m + 1]):
            cp.start()
        refs[-1][...] = jnp.zeros((8, LANE), F32)

    hbm = pl.BlockSpec(memory_space=pltpu.HBM)
    sem = pl.BlockSpec(memory_space=pltpu.SEMAPHORE)
    operands = [pltpu.with_memory_space_constraint(t, pltpu.HBM) for t in list(sources) + list(lands)]
    out = pl.pallas_call(
        body, name=name, in_specs=[hbm] * (n + m),
        out_specs=[sem, sem] + [hbm] * (n + m) + [pl.BlockSpec(memory_space=pltpu.VMEM)],
        out_shape=[pltpu.SemaphoreType.DMA((n_copies,)), pltpu.SemaphoreType.DMA((n_copies,))]
        + [pltpu.HBM(t.shape, t.dtype) for t in operands] + [jax.ShapeDtypeStruct((8, LANE), F32)],
        input_output_aliases={i: 2 + i for i in range(n + m)},
        compiler_params=pltpu.CompilerParams(has_side_effects=pltpu.SideEffectType.DATAFLOW_SIDE_EFFECTING))(*operands)
    return out[0], out[1], out[2:2 + n], out[2 + n:2 + n + m], out[-1]


def _split_wait(name, make_copies, send_sems, recv_sems, sources, lands, after):
    n, m = len(sources), len(lands)

    def body(*refs):
        for cp in make_copies(_position(), refs[:n], refs[n:n + m], refs[n + m], refs[n + m + 1]):
            cp.wait_send()
            cp.wait_recv()

    hbm = pl.BlockSpec(memory_space=pltpu.HBM)
    sem = pl.BlockSpec(memory_space=pltpu.SEMAPHORE)
    out = pl.pallas_call(
        body, name=name, in_specs=[hbm] * (n + m) + [sem, sem, pl.BlockSpec(memory_space=pl.ANY)],
        out_specs=[hbm] * (n + m), out_shape=[pltpu.HBM(t.shape, t.dtype) for t in list(sources) + list(lands)],
        input_output_aliases={i: i for i in range(n + m)},
        compiler_params=pltpu.CompilerParams(has_side_effects=pltpu.SideEffectType.DATAFLOW_SIDE_EFFECTING))(
            *sources, *lands, send_sems, recv_sems, after)
    return out[:n], out[n:]


def _gather_copies(pos, halves, lands, send_sems, recv_sems):
    chip, core = 2 * pos[0] + pos[1], pos[2]
    copies = []
    for a in range(len(halves)):
        for k in range(1, N_CHIPS):
            for r0, nr in _row_chunks(halves[a].shape[1], halves[a].shape[2] * halves[a].dtype.itemsize):
                i = len(copies)
                copies.append(pltpu.make_async_remote_copy(
                    src_ref=halves[a].at[core, pl.ds(r0, nr)], dst_ref=lands[a].at[chip, core, pl.ds(r0, nr)],
                    send_sem=send_sems.at[i], recv_sem=recv_sems.at[i], device_id=_flip(pos, 2 * k), device_id_type=MESH))
    return copies


def _gather_pieces(halves):
    return (N_CHIPS - 1) * sum(len(_row_chunks(a.shape[1], a.shape[2] * a.dtype.itemsize)) for a in halves)


def _pair_forward(lands):
    n = len(lands)

    def plan(pos, ins, outs):
        remote = []
        for a in range(n):
            for k in range(1, N_CHIPS):
                there = _flip(pos, 2 * k)
                for r0, nr in _row_chunks(lands[a].shape[2], lands[a].shape[3] * lands[a].dtype.itemsize):
                    slot = (2 * there[0] + there[1], pos[2], pl.ds(r0, nr))
                    remote.append((ins[a].at[slot], outs[a].at[slot], _flip(pos, 1)))
        return remote, []

    return _exchange("pair_forward", lands, [jax.ShapeDtypeStruct(t.shape, t.dtype) for t in lands], plan,
                     _gather_pieces([jax.ShapeDtypeStruct(t.shape[1:], t.dtype) for t in lands]), 0, in_place=True)


def _exchange(name, arrays, out_shapes, plan, n_remote, n_local, in_place=False):
    n, m = len(arrays), len(out_shapes)

    def body(*refs):
        send_sems, recv_sems, local_sems = refs[n + m:]
        remote, local = plan(_position(), refs[:n], refs[n:n + m])
        assert (len(remote), len(local)) == (n_remote, n_local)
        copies = [pltpu.make_async_copy(src, dst, local_sems.at[i]) for i, (src, dst) in enumerate(local)]
        copies += [pltpu.make_async_remote_copy(src_ref=src, dst_ref=dst, send_sem=send_sems.at[i], recv_sem=recv_sems.at[i],
                                                device_id=dev, device_id_type=MESH)
                   for i, (src, dst, dev) in enumerate(remote)]
        for cp in copies:
            cp.start()
        for cp in copies:
            cp.wait()

    hbm = pl.BlockSpec(memory_space=pl.ANY)
    return pl.pallas_call(
        body, name=name, in_specs=[hbm] * n, out_specs=[hbm] * m, out_shape=out_shapes,
        input_output_aliases={i: i for i in range(n)} if in_place else {},
        scratch_shapes=[pltpu.SemaphoreType.DMA((n_remote,)), pltpu.SemaphoreType.DMA((n_remote,)),
                        pltpu.SemaphoreType.DMA((max(n_local, 1),))],
        compiler_params=pltpu.CompilerParams(has_side_effects=True))(*arrays)


def _pair_send(slabs):
    n = len(slabs)
    pieces = [_row_chunks(g.shape[2], g.shape[3] * g.dtype.itemsize) for g in slabs]

    def plan(pos, ins, outs):
        return [(ins[a].at[j, 1 - pos[2], pl.ds(r0, nr)], outs[a].at[j, pl.ds(r0, nr)], _flip(pos, 1))
                for a in range(n) for j in range(N_CHIPS) for r0, nr in pieces[a]], []

    return _exchange("pair_send", slabs, [jax.ShapeDtypeStruct((N_CHIPS,) + g.shape[2:], g.dtype) for g in slabs], plan,
                     N_CHIPS * sum(len(p) for p in pieces), 0)


def _chip_scatter_copies(pos, sums, lands, send_sems, recv_sems):
    copies = []
    for a in range(len(sums)):
        for k in range(1, N_CHIPS):
            to = _flip(pos, 2 * k)
            for r0, nr in _row_chunks(sums[a].shape[1], sums[a].shape[2] * sums[a].dtype.itemsize):
                i = len(copies)
                copies.append(pltpu.make_async_remote_copy(
                    src_ref=sums[a].at[2 * to[0] + to[1], pl.ds(r0, nr)], dst_ref=lands[a].at[k - 1, pl.ds(r0, nr)],
                    send_sem=send_sems.at[i], recv_sem=recv_sems.at[i], device_id=to, device_id_type=MESH))
    return copies


def _chip_scatter_start(sums):
    n_copies = (N_CHIPS - 1) * sum(len(_row_chunks(g.shape[1], g.shape[2] * g.dtype.itemsize)) for g in sums)
    lands = [lax.empty((N_CHIPS - 1,) + g.shape[1:], g.dtype) for g in sums]
    return _split_start("chip_scatter_start", _chip_scatter_copies, n_copies, sums, lands)


def _chip_scatter_wait(send_sems, recv_sems, sums, lands, after):
    return _split_wait("chip_scatter_wait", _chip_scatter_copies, send_sems, recv_sems, sums, lands, after)


def _pair_exchange(halves):
    n = len(halves)
    pieces = [_row_chunks(h.shape[0], h.shape[1] * h.dtype.itemsize) for h in halves]

    def plan(pos, ins, outs):
        return [(ins[a].at[pl.ds(r0, nr)], outs[a].at[pl.ds(r0, nr)], _flip(pos, 1))
                for a in range(n) for r0, nr in pieces[a]], []

    return _exchange("pair_exchange", halves, [jax.ShapeDtypeStruct(h.shape, h.dtype) for h in halves], plan,
                     sum(len(p) for p in pieces), 0)


def _pack(arrays):
    flat = jnp.concatenate([a.reshape(-1).astype(F32) for a in arrays])
    size = -(-flat.shape[0] // (8 * LANE)) * (8 * LANE)
    return jnp.pad(flat, (0, size - flat.shape[0])).reshape(size // LANE, LANE)


def _unpack(buf, shapes):
    flat = buf.reshape(-1)
    out, off = [], 0
    for shp in shapes:
        n = math.prod(shp)
        out.append(flat[off:off + n].reshape(shp))
        off += n
    return out


def _unpack_rows(bufs, shapes):
    flat = bufs.reshape(bufs.shape[0], -1)
    out, off = [], 0
    for shp in shapes:
        n = math.prod(shp)
        out.append(flat[:, off:off + n].reshape((bufs.shape[0],) + shp))
        off += n
    return out


def _taps8(w):
    return jnp.pad(w, ((0, 8 - CONV_K), (0, 0)))


def _local_step(xs, tgt, scale, shift, gate, norm_w, w_in_p, b_in_p, ml_conv_w, ml_conv_b, ml_norm_w, ssm_conv_w,
                ssm_conv_b, ssm_a_log, ssm_d, ssm_norm_w, wpm, wps, wo, final_w, start_exchange=None, late_weights=None):
    s = xs.shape[0]
    ts = min(512, s)
    tm = min(2048, s)
    u = _prenorm_fwd(xs, norm_w, scale, shift, ts)
    proj = _matmul_bias(u, w_in_p, b_in_p, tm, 512, 0, O_IF, BF16)
    gates = _matmul_bias(u, w_in_p, b_in_p, tm, 512, O_IF, NP - O_IF, F32)
    mlw8, ssw8 = _taps8(ml_conv_w), _taps8(ssm_conv_w)
    qk, qk_dact = _conv_fwd(proj, O_QK, 2048, mlw8, ml_conv_b, ts)
    xbc, xbc_dact = _conv_fwd(proj, O_XBC, 3072, ssw8, ssm_conv_b, ts)
    gt = gates[:, :LANE].T
    dtt = gates[:, O_DT - O_IF:O_DT - O_IF + LANE].T
    hm, cst, nm = _mlstm_fwd(qk, proj, gates, gt)
    alog_row = jnp.pad(ssm_a_log, ((0, 0), (0, LANE - SSM_HEADS)))
    alog_col = alog_row.reshape(LANE, 1)
    dskip_x = jnp.repeat(ssm_d[0], SSM_HEADDIM)[None]
    expand = _head_expand()
    yssd, sst = _ssd_fwd(xbc, gates, dtt, alog_row, alog_col, dskip_x, expand)
    tp = min(128, s)
    ym, ys = _post_fwd(hm, yssd, proj, ml_norm_w, ssm_norm_w, tp)
    if late_weights is not None:
        wpm, wps, wo = late_weights(ym)
    dxres, merged, dmo, dpm, dps, dym, dys, dproj, acc_m = _merge(xs, ym, ys, proj, tgt, gate, final_w, wpm, wps, wo,
                                                                  min(256, s))
    dh, dyssd, dproj, acc_p = _post_bwd(dym, dys, hm, yssd, proj, ml_norm_w, ssm_norm_w, dproj, tp)
    dqk, dproj, dif = _mlstm_bwd(qk, proj, gates, gt, hm, dh, cst, nm, dproj)
    dxbc, ddt, accd, acca = _ssd_bwd(xbc, gates, dtt, alog_row, alog_col, dskip_x, expand, expand.T, dyssd, sst)
    dproj, acc_cq = _conv_bwd(proj, O_QK, 2048, mlw8, qk_dact, dqk, dproj, ts)
    dproj, acc_cx = _conv_bwd(proj, O_XBC, 3072, ssw8, xbc_dact, dxbc, dproj, ts)
    dproj = dproj.at[:, O_IF:O_IF + SMALL_W].set(dif).at[:, O_DT:O_DT + SMALL_W].set(ddt)
    gw_in_p, gb_in_p = _matmul_tn(u.T, dproj, tm, 512, with_colsum=True, a_is_transposed=True)
    g_wpm = _matmul_tn(ym, dpm, tm, 512)
    g_wps = _matmul_tn(ys, dps, tm, 512)
    g_wo = _matmul_tn(merged, dmo, tm, 512)
    token, in_flight = (None, None) if start_exchange is None else start_exchange(gw_in_p, g_wpm, g_wps, g_wo)
    du = _matmul_nt(dproj, w_in_p, tm, 512, after=token)
    grad_x, acc_n = _prenorm_bwd(du, xs, dxres, norm_w, scale, ts)
    a_coef = -jnp.exp(ssm_a_log[0])
    small = dict(
        mod=jnp.concatenate([acc_n[2], acc_n[1], acc_m[2]]), norm_w=acc_n[0], b_in=_unpad_cols(gb_in_p[0]),
        ml_conv_w=acc_cq[0:CONV_K], ml_conv_b=acc_cq[CONV_K], ml_norm_w=acc_p[0], ssm_conv_w=acc_cx[0:CONV_K],
        ssm_conv_b=acc_cx[CONV_K], ssm_a_log=acca[0, :SSM_HEADS] * a_coef,
        ssm_d=accd[0].reshape(SSM_HEADS, SSM_HEADDIM).sum(axis=1), ssm_norm_w=acc_p[1], final_w=acc_m[1], loss=acc_m[0, 0:1])
    return grad_x, small, gw_in_p, g_wpm, g_wps, g_wo, in_flight


WEIGHTS = ("norm_w", "ada_w", "ada_b", "w_in", "b_in", "ml_conv_w", "ml_conv_b", "ml_norm_w", "ssm_conv_w", "ssm_conv_b",
           "ssm_a_log", "ssm_d", "ssm_norm_w", "w_proj_m", "w_proj_s", "w_out", "final_w")
LARGE = ("ada_w", "w_in", "w_proj_m", "w_proj_s", "w_out")
SMALL_SUMS = (("mod", (3 * D_MODEL,)), ("norm_w", (D_MODEL,)), ("b_in", (IN_WIDTH,)), ("ml_conv_w", (CONV_K, 2048)),
              ("ml_conv_b", (2048,)), ("ml_norm_w", (2048,)), ("ssm_conv_w", (CONV_K, 3072)), ("ssm_conv_b", (3072,)),
              ("ssm_a_log", (SSM_HEADS,)), ("ssm_d", (SSM_HEADS,)), ("ssm_norm_w", (2048,)), ("final_w", (D_MODEL,)),
              ("loss", (1,)))


def kernel(x, c, norm_w, ada_w, ada_b, w_in, b_in, ml_conv_w, ml_conv_b, ml_norm_w, ssm_conv_w, ssm_conv_b, ssm_a_log, ssm_d, ssm_norm_w, w_proj_m, w_proj_s, w_out, final_w, loss_target, m_norm_w, m_ada_w, m_ada_b, m_w_in, m_b_in, m_ml_conv_w, m_ml_conv_b, m_ml_norm_w, m_ssm_conv_w, m_ssm_conv_b, m_ssm_a_log, m_ssm_d, m_ssm_norm_w, m_w_proj_m, m_w_proj_s, m_w_out, m_final_w, v_norm_w, v_ada_w, v_ada_b, v_w_in, v_b_in, v_ml_conv_w, v_ml_conv_b, v_ml_norm_w, v_ssm_conv_w, v_ssm_conv_b, v_ssm_a_log, v_ssm_d, v_ssm_norm_w, v_w_proj_m, v_w_proj_s, v_w_out, v_final_w):
    w = dict(norm_w=norm_w, ada_w=ada_w, ada_b=ada_b, w_in=w_in, b_in=b_in, ml_conv_w=ml_conv_w, ml_conv_b=ml_conv_b,
             ml_norm_w=ml_norm_w, ssm_conv_w=ssm_conv_w, ssm_conv_b=ssm_conv_b, ssm_a_log=ssm_a_log, ssm_d=ssm_d,
             ssm_norm_w=ssm_norm_w, w_proj_m=w_proj_m, w_proj_s=w_proj_s, w_out=w_out, final_w=final_w)
    m = dict(zip(WEIGHTS, (m_norm_w, m_ada_w, m_ada_b, m_w_in, m_b_in, m_ml_conv_w, m_ml_conv_b, m_ml_norm_w, m_ssm_conv_w,
                           m_ssm_conv_b, m_ssm_a_log, m_ssm_d, m_ssm_norm_w, m_w_proj_m, m_w_proj_s, m_w_out, m_final_w)))
    v = dict(zip(WEIGHTS, (v_norm_w, v_ada_w, v_ada_b, v_w_in, v_b_in, v_ml_conv_w, v_ml_conv_b, v_ml_norm_w, v_ssm_conv_w,
                           v_ssm_conv_b, v_ssm_a_log, v_ssm_d, v_ssm_norm_w, v_w_proj_m, v_w_proj_s, v_w_out, v_final_w)))
    pos = _position()
    chip = 2 * pos[0] + pos[1]
    dev = 2 * chip + pos[2]
    mlw_cols, ssw_cols, ada_cols = ml_conv_w.shape[2], ssm_conv_w.shape[2], ada_w.shape[2]

    mine = [_bf(a[0]).reshape(2, a.shape[1] // 2, a.shape[2]) for a in (w_in, w_proj_m, w_proj_s, w_out)]
    landing = lambda own: lax.empty((N_CHIPS,) + own.shape, own.dtype)
    w_send, w_recv, w_src, w_land, w_token = _split_start("w_in_gather_start", _gather_copies, _gather_pieces(mine[:1]),
                                                          mine[:1], [landing(mine[0])])

    g0 = _allgather8(_pack([c + w_token[0, 0], ml_conv_w, ssm_conv_w]))
    c_all, mlw_all, ssw_all = _unpack_rows(g0, [(D_MODEL,), (CONV_K, mlw_cols), (CONV_K, ssw_cols)])
    ml_conv_full = mlw_all[0::2].transpose(1, 0, 2).reshape(CONV_K, N_CHIPS * mlw_cols)
    ssm_conv_full = ssw_all[0::2].transpose(1, 0, 2).reshape(CONV_K, N_CHIPS * ssw_cols)

    ada_b_mine = lax.dynamic_slice_in_dim(ada_b, chip * ada_cols, ada_cols, axis=1)
    g1 = _allgather8(_ada_fwd(c_all, ada_w[0], ada_b_mine))
    mod = lax.dynamic_index_in_dim(g1[0::2], dev, axis=1, keepdims=False).reshape(1, 3 * D_MODEL)
    shift, scale, gate = mod[:, :D_MODEL], mod[:, D_MODEL:2 * D_MODEL], mod[:, 2 * D_MODEL:]

    def whole(lands, owns):
        return [lax.dynamic_update_index_in_dim(got, own, chip, 0).reshape(N_CHIPS, -1, own.shape[-1])
                for got, own in zip(_pair_forward(lands), owns)]

    w_src, w_land = _split_wait("w_in_gather_wait", _gather_copies, w_send, w_recv, w_src, w_land, mod)
    behind = (w_src[0][0, 0:1, 0:1] * 0).astype(BF16)
    later = [a + behind for a in mine[1:]]
    p_send, p_recv, p_src, p_land, _ = _split_start("merge_gather_start", _gather_copies, _gather_pieces(later), later,
                                                    [landing(a) for a in later])
    w_in_p = _shards_to_padded(whole(w_land, w_src)[0])

    def late_weights(after):
        srcs, lands = _split_wait("merge_gather_wait", _gather_copies, p_send, p_recv, p_src, p_land, after)
        return [a.reshape(-1, D_MODEL) for a in whole(lands, srcs)]

    def start_exchange(g_w_in, g_wpm, g_wps, g_wo):
        split = lambda g, rows: _bf(g).reshape(N_CHIPS, 2, rows // (2 * N_CHIPS), g.shape[-1])
        slabs = [split(_padded_to_shards(_bf(g_w_in)), N_CHIPS * D_MODEL),
                 split(g_wpm, g_wpm.shape[0]), split(g_wps, g_wps.shape[0]), split(g_wo, g_wo.shape[0])]
        pair_sums = []
        for slab, rec in zip(slabs, _pair_send(slabs)):
            kept = lax.dynamic_index_in_dim(slab, pos[2], 1, keepdims=False)
            rows = kept.shape[0] * kept.shape[1]
            both = _sum_parts(kept.reshape(rows, -1), rec.reshape(1, rows, -1), 32, BF16)
            pair_sums.append(both.reshape(kept.shape))
        send_sems, recv_sems, sums, lands, token = _chip_scatter_start(pair_sums)
        return token, (send_sems, recv_sems, sums, lands)

    grad_x, small, _, _, _, _, in_flight = _local_step(
        x[0], loss_target[0], scale, shift, gate, norm_w, w_in_p, _pad_cols(b_in), ml_conv_full, ml_conv_b, ml_norm_w,
        ssm_conv_full, ssm_conv_b, ssm_a_log, ssm_d, ssm_norm_w, None, None, None, final_w[None], start_exchange,
        late_weights)

    g2 = _allgather8(_pack([small[name] for name, _ in SMALL_SUMS]))
    total = dict(zip([name for name, _ in SMALL_SUMS], _unpack(_sum_parts(None, g2, g2.shape[1]), [s for _, s in SMALL_SUMS])))
    dmod_all = g2[:, :3 * D_MODEL // LANE].reshape(N_DEV, 3 * D_MODEL)
    grads = dict(total)
    grads["ada_b"] = total["mod"]
    grads["ml_conv_w"] = lax.dynamic_slice_in_dim(total["ml_conv_w"], chip * mlw_cols, mlw_cols, axis=1)
    grads["ssm_conv_w"] = lax.dynamic_slice_in_dim(total["ssm_conv_w"], chip * ssw_cols, ssw_cols, axis=1)
    grads["ada_w"] = _ada_bwd(c_all, lax.dynamic_slice_in_dim(dmod_all, chip * ada_cols, ada_cols, axis=1))

    halves = []
    for both, rec in zip(*_chip_scatter_wait(*in_flight, grad_x)):
        halves.append(_sum_parts(lax.dynamic_index_in_dim(both, chip, 0, keepdims=False), rec, 32))
    for name, half, other in zip(("w_in", "w_proj_m", "w_proj_s", "w_out"), halves, _pair_exchange(halves)):
        grads[name] = lax.cond(pos[2] == 0, lambda mine, theirs: jnp.concatenate([mine, theirs]),
                               lambda mine, theirs: jnp.concatenate([theirs, mine]), half, other)

    delta, new_m, new_v = {}, {}, {}
    for name in LARGE:
        if w[name].shape[-1] % LANE:
            flat = lambda a: a.reshape(a.shape[-2:]).T
            back = lambda a: a.T.reshape(w[name].shape)
            g_flat = flat(grads[name])
            delta[name], new_m[name], new_v[name] = (back(a) for a in _adamw(flat(w[name]), g_flat, flat(m[name]), flat(v[name]), LANE))
            grads[name] = back(g_flat)
        else:
            delta[name], new_m[name], new_v[name] = _adamw(w[name], grads[name], m[name], v[name], 64)
    rest = [name for name in WEIGHTS if name not in LARGE]
    packed = [_pack([t[name] for name in rest]) for t in (w, grads, m, v)]
    for out, buf in zip((delta, new_m, new_v), _adamw(*packed, packed[0].shape[0])):
        out.update(zip(rest, _unpack(buf, [w[name].shape for name in rest])))
    loss = total["loss"][0]
    return (loss, grad_x[None], *[grads[name].reshape(w[name].shape) for name in WEIGHTS], *[delta[name] for name in WEIGHTS],
            *[new_m[name] for name in WEIGHTS], *[new_v[name] for name in WEIGHTS])
```

```python
import functools
import math

import jax
import jax.numpy as jnp
from jax import lax
from jax.experimental import pallas as pl
from jax.experimental.pallas import tpu as pltpu

F32 = jnp.float32
BF16 = jnp.bfloat16
HI = lax.Precision.HIGHEST
MESH = pl.DeviceIdType.MESH

D_MODEL = 1024
EPS = 1e-6
CONV_K = 4
ML_HEADS = 8
ML_DQK = 128
ML_DV = 256
SSM_HEADS = 32
SSM_HEADDIM = 64
SSM_GROUPS = 4
SSM_STATE = 128
IN_WIDTH = 15408
N_CHIPS = 4
N_DEV = 8
ADAM_LR, ADAM_B1, ADAM_B2, ADAM_EPS, ADAM_WD, ADAM_STEP = 0.001, 0.9, 0.999, 1e-08, 0.01, 10

O_O, O_ZM, O_ZS, O_MG, O_QK, O_V, O_XBC, O_IF, O_DT = 0, 2048, 4096, 6144, 8192, 10240, 12288, 15360, 15616
SMALL_W = 256
NP = 15872
LANE = 128
CHUNK = 128
NEG = -1e30
VMEM_LIMIT = 48 * 1024 * 1024
MERGE_VMEM = 60 * 1024 * 1024


def _cparams(*sem, vmem=VMEM_LIMIT):
    return pltpu.CompilerParams(dimension_semantics=sem, vmem_limit_bytes=vmem)


def _pad_cols(w):
    z = lambda n: jnp.zeros(w.shape[:-1] + (n,), w.dtype)
    return jnp.concatenate([w[..., 4096:8192], w[..., 11280:13328], w[..., 13360:15408], w[..., :4096], w[..., 8208:11280],
                            w[..., 8192:8208], z(SMALL_W - 16), w[..., 13328:13360], z(SMALL_W - 32)], axis=-1)


def _unpad_cols(g):
    return jnp.concatenate([g[..., O_QK:O_QK + 4096], g[..., O_O:O_O + 4096], g[..., O_IF:O_IF + 16],
                            g[..., O_XBC:O_XBC + 3072], g[..., O_ZS:O_ZS + 2048], g[..., O_DT:O_DT + 32],
                            g[..., O_MG:O_MG + 2048]], axis=-1)


PADDED_SEGMENTS = ((4096, 8192, 0), (11280, 13328, 0), (13360, 15408, 0), (0, 4096, 0), (8208, 11280, 0),
                   (8192, 8208, SMALL_W - 16), (13328, 13360, SMALL_W - 32))
SHARD_W = IN_WIDTH // N_CHIPS


def _shards_to_padded(shards):
    parts = []
    for first, last, pad in PADDED_SEGMENTS:
        for j in range(N_CHIPS):
            lo, hi = max(first, j * SHARD_W), min(last, (j + 1) * SHARD_W)
            if lo < hi:
                parts.append(shards[j][:, lo - j * SHARD_W:hi - j * SHARD_W])
        if pad:
            parts.append(jnp.zeros((shards.shape[1], pad), shards.dtype))
    return jnp.concatenate(parts, axis=1)


def _padded_to_shards(g):
    offsets, off = {}, 0
    for first, last, pad in PADDED_SEGMENTS:
        offsets[first] = off
        off += last - first + pad
    shards = []
    for j in range(N_CHIPS):
        parts = []
        for first, last, _ in sorted(PADDED_SEGMENTS):
            lo, hi = max(first, j * SHARD_W), min(last, (j + 1) * SHARD_W)
            if lo < hi:
                parts.append(g[:, offsets[first] + lo - first:offsets[first] + hi - first])
        shards.append(jnp.concatenate(parts, axis=1))
    return jnp.stack(shards)


def _sigmoid(x):
    return 1.0 / (1.0 + jnp.exp(-x))


def _silu(x):
    return x * _sigmoid(x)


def _dsilu(x):
    s = _sigmoid(x)
    return s + x * s * (1.0 - s)


def _softplus(x):
    return jnp.maximum(x, 0.0) + jnp.log(1.0 + jnp.exp(-jnp.abs(x)))


def _logsigmoid(x):
    return jnp.minimum(x, 0.0) - jnp.log(1.0 + jnp.exp(-jnp.abs(x)))


def _dot(a, b, dims, precision=None):
    return lax.dot_general(a, b, (dims, ((), ())), preferred_element_type=F32, precision=precision)


def _nn(a, b, precision=None):
    return _dot(a, b, ((1,), (0,)), precision)


def _nt(a, b, precision=None):
    return _dot(a, b, ((1,), (1,)), precision)


def _tn(a, b, precision=None):
    return _dot(a, b, ((0,), (0,)), precision)


def _bf(x):
    return x.astype(BF16)


def _split(x, terms):
    parts = []
    for _ in range(terms):
        part = _bf(x)
        parts.append(part)
        x = x - part.astype(F32)
    return parts


def _pick_right(x, pick, terms):
    pick = _bf(pick)
    out = None
    for part in _split(x, terms):
        out = _nn(part, pick) if out is None else out + _nn(part, pick)
    return out


def _pick_left(pick, x, terms):
    pick = _bf(pick)
    out = None
    for part in _split(x, terms):
        out = _nn(pick, part) if out is None else out + _nn(pick, part)
    return out


def _lane_col(x, lane):
    idx = lax.broadcasted_iota(jnp.int32, x.shape, 1)
    return jnp.sum(jnp.where(idx == lane, x, 0.0), axis=1, keepdims=True)


def _tri(n, upper):
    r = lax.broadcasted_iota(jnp.int32, (n, n), 0)
    c = lax.broadcasted_iota(jnp.int32, (n, n), 1)
    return jnp.where((r <= c) if upper else (r >= c), 1.0, 0.0).astype(F32)


def _eye(n):
    return jnp.where(lax.broadcasted_iota(jnp.int32, (n, n), 0) == lax.broadcasted_iota(jnp.int32, (n, n), 1), 1.0, 0.0)


def _sum_all(x):
    return jnp.sum(jnp.sum(x, axis=1, keepdims=True), axis=0, keepdims=True)


def _crossing(p):
    L = p.shape[0]
    below = _nn(_bf(_tri(L, True)), _bf(p))
    strict = lax.broadcasted_iota(jnp.int32, (L, L), 0) > lax.broadcasted_iota(jnp.int32, (L, L), 1)
    return [jnp.sum(jnp.where(strict, below[:, b * L:(b + 1) * L], 0.0), axis=1, keepdims=True)
            for b in range(p.shape[1] // L)]


def _matmul_bias(a, w, bias, tm, tn, col0, ncols, dtype):
    m, k = a.shape
    j0 = col0 // tn

    def body(a_ref, w_ref, b_ref, o_ref):
        o_ref[...] = (_nn(a_ref[...], w_ref[...]) + b_ref[...]).astype(dtype)

    return pl.pallas_call(
        body, name="matmul_bias", grid=(m // tm, ncols // tn),
        in_specs=[pl.BlockSpec((tm, k), lambda i, j: (i, 0)), pl.BlockSpec((k, tn), lambda i, j: (0, j0 + j)),
                  pl.BlockSpec((1, tn), lambda i, j: (0, j0 + j))],
        out_specs=pl.BlockSpec((tm, tn), lambda i, j: (i, j)),
        out_shape=jax.ShapeDtypeStruct((m, ncols), dtype),
        compiler_params=_cparams("parallel", "arbitrary"))(a, w, bias)


def _matmul_nt(a, w, tm, tk, after=None):
    m, n = a.shape
    k = w.shape[0]

    def body(a_ref, w_ref, *rest):
        o_ref = rest[-1]

        @pl.when(pl.program_id(1) == 0)
        def _():
            o_ref[...] = jnp.zeros_like(o_ref)
        o_ref[...] += _nt(a_ref[...], w_ref[...])

    extra = [] if after is None else [after]
    return pl.pallas_call(
        body, name="matmul_nt", grid=(m // tm, n // tk),
        in_specs=[pl.BlockSpec((tm, tk), lambda i, j: (i, j)), pl.BlockSpec((k, tk), lambda i, j: (0, j))]
        + [pl.BlockSpec(memory_space=pl.ANY)] * len(extra),
        out_specs=pl.BlockSpec((tm, k), lambda i, j: (i, 0)),
        out_shape=jax.ShapeDtypeStruct((m, k), F32),
        compiler_params=_cparams("parallel", "arbitrary"))(a, w, *extra)


def _matmul_tn(a, b, tm, tn, with_colsum=False, a_is_transposed=False):
    k, m = a.shape if a_is_transposed else a.shape[::-1]
    n = b.shape[1]

    def body(a_ref, b_ref, o_ref, *rest):
        first = pl.program_id(1) == 0

        @pl.when(first)
        def _():
            o_ref[...] = jnp.zeros_like(o_ref)
        o_ref[...] += _nn(a_ref[...], b_ref[...]) if a_is_transposed else _tn(a_ref[...], b_ref[...])
        if with_colsum:
            s_ref = rest[0]

            @pl.when(first)
            def _():
                s_ref[...] = jnp.zeros_like(s_ref)
            s_ref[...] += jnp.sum(b_ref[...].astype(F32), axis=0, keepdims=True)

    out_specs = [pl.BlockSpec((k, tn), lambda j, i: (0, j))]
    out_shape = [jax.ShapeDtypeStruct((k, n), F32)]
    if with_colsum:
        out_specs.append(pl.BlockSpec((1, tn), lambda j, i: (0, j)))
        out_shape.append(jax.ShapeDtypeStruct((1, n), F32))
    out = pl.pallas_call(
        body, name="matmul_tn", grid=(n // tn, m // tm),
        in_specs=[pl.BlockSpec((k, tm), lambda j, i: (0, i)) if a_is_transposed else pl.BlockSpec((tm, k), lambda j, i: (i, 0)),
                  pl.BlockSpec((tm, tn), lambda j, i: (i, j))],
        out_specs=out_specs, out_shape=out_shape,
        compiler_params=_cparams("parallel", "arbitrary"))(a, b)
    return out if with_colsum else out[0]


def _ada_fwd(c_all, ada_w, ada_b):
    def body(c_ref, w_ref, b_ref, o_ref):
        o_ref[...] = _nn(_bf(_silu(c_ref[...])), _bf(w_ref[...])) + b_ref[...]

    return pl.pallas_call(body, name="ada_fwd", out_shape=jax.ShapeDtypeStruct((c_all.shape[0], ada_w.shape[1]), F32),
                          compiler_params=_cparams())(c_all, ada_w, ada_b)


def _ada_bwd(c_all, dmod):
    def body(c_ref, d_ref, o_ref):
        o_ref[...] = _tn(_bf(_silu(c_ref[...])), _bf(d_ref[...]))

    return pl.pallas_call(body, name="ada_bwd", out_shape=jax.ShapeDtypeStruct((c_all.shape[1], dmod.shape[1]), F32),
                          compiler_params=_cparams())(c_all, dmod)


def _prenorm_fwd(x, norm_w, scale, shift, ts):
    s, d = x.shape

    def body(x_ref, nw_ref, sc_ref, sh_ref, u_ref):
        xv = x_ref[...]
        r = lax.rsqrt(jnp.mean(xv * xv, axis=1, keepdims=True) + EPS)
        u_ref[...] = _bf(xv * r * nw_ref[...] * (1.0 + sc_ref[...]) + sh_ref[...])

    row = pl.BlockSpec((1, d), lambda i: (0, 0))
    return pl.pallas_call(
        body, name="prenorm_fwd", grid=(s // ts,),
        in_specs=[pl.BlockSpec((ts, d), lambda i: (i, 0)), row, row, row],
        out_specs=pl.BlockSpec((ts, d), lambda i: (i, 0)), out_shape=jax.ShapeDtypeStruct((s, d), BF16),
        compiler_params=_cparams("parallel"))(x, norm_w, scale, shift)


def _prenorm_bwd(du, x, dxres, norm_w, scale, ts):
    s, d = x.shape

    def body(du_ref, x_ref, dr_ref, nw_ref, sc_ref, gx_ref, acc_ref):
        @pl.when(pl.program_id(0) == 0)
        def _():
            acc_ref[...] = jnp.zeros_like(acc_ref)
        xv, duv = x_ref[...], du_ref[...]
        r = lax.rsqrt(jnp.mean(xv * xv, axis=1, keepdims=True) + EPS)
        xn = xv * r
        nw, sc1 = nw_ref[...], 1.0 + sc_ref[...]
        dxn = duv * (nw * sc1)
        gx_ref[...] = r * (dxn - xn * jnp.mean(dxn * xn, axis=1, keepdims=True)) + dr_ref[...]
        t = duv * xn
        acc_ref[0:1, :] += jnp.sum(t, axis=0, keepdims=True) * sc1
        acc_ref[1:2, :] += jnp.sum(t, axis=0, keepdims=True) * nw
        acc_ref[2:3, :] += jnp.sum(duv, axis=0, keepdims=True)

    tile = pl.BlockSpec((ts, d), lambda i: (i, 0))
    row = pl.BlockSpec((1, d), lambda i: (0, 0))
    return pl.pallas_call(
        body, name="prenorm_bwd", grid=(s // ts,),
        in_specs=[tile, tile, tile, row, row],
        out_specs=[tile, pl.BlockSpec((8, d), lambda i: (0, 0))],
        out_shape=[jax.ShapeDtypeStruct((s, d), F32), jax.ShapeDtypeStruct((8, d), F32)],
        compiler_params=_cparams("arbitrary"))(du, x, dxres, norm_w, scale)


CONV_CB = 512


def _conv_taps(buf_ref, ts):
    return [buf_ref[pl.ds(8 - (CONV_K - 1) + j, ts), :] for j in range(CONV_K)]


def _conv_fwd(proj, col0, width, w8, b, ts):
    s = proj.shape[0]
    cb = CONV_CB
    nt = s // ts

    def body(x_ref, w_ref, b_ref, o_ref, ds_ref, buf_ref):
        @pl.when(pl.program_id(1) == 0)
        def _():
            buf_ref[0:8, :] = jnp.zeros((8, cb), F32)
        buf_ref[pl.ds(8, ts), :] = x_ref[...].astype(F32)
        acc = b_ref[...] + jnp.zeros((ts, cb), F32)
        for j, tap in enumerate(_conv_taps(buf_ref, ts)):
            acc = acc + tap * w_ref[j:j + 1, :]
        sg = _sigmoid(acc)
        o_ref[...] = acc * sg
        ds_ref[...] = _bf(sg + acc * sg * (1.0 - sg))
        buf_ref[0:8, :] = buf_ref[pl.ds(ts, 8), :]

    c0 = col0 // cb
    tile = pl.BlockSpec((ts, cb), lambda c, i: (i, c))
    return pl.pallas_call(
        body, name="conv_fwd", grid=(width // cb, nt),
        in_specs=[pl.BlockSpec((ts, cb), lambda c, i: (i, c0 + c)), pl.BlockSpec((8, cb), lambda c, i: (0, c)),
                  pl.BlockSpec((1, cb), lambda c, i: (0, c))],
        out_specs=[tile, tile],
        out_shape=[jax.ShapeDtypeStruct((s, width), F32), jax.ShapeDtypeStruct((s, width), BF16)],
        scratch_shapes=[pltpu.VMEM((ts + 8, cb), F32)],
        compiler_params=_cparams("parallel", "arbitrary"))(proj, w8, b)


def _conv_bwd(proj, col0, width, w8, dact, dpost, dproj, ts):
    s = proj.shape[0]
    cb = CONV_CB
    nt = s // ts
    c0 = col0 // cb

    def body(x_ref, da_ref, dp_ref, w_ref, _, dx_ref, acc_ref, dbuf_ref):
        @pl.when(pl.program_id(1) == 0)
        def _():
            acc_ref[...] = jnp.zeros_like(acc_ref)
            dbuf_ref[pl.ds(ts, 8), :] = jnp.zeros((8, cb), F32)
        dconv = dp_ref[...].astype(F32) * da_ref[...].astype(F32)
        acc_ref[CONV_K:CONV_K + 1, :] += jnp.sum(dconv, axis=0, keepdims=True)
        dbuf_ref[pl.ds(0, ts), :] = dconv
        xv = x_ref[...].astype(F32)
        dx = jnp.zeros((ts, cb), F32)
        for j in range(CONV_K):
            shifted = dbuf_ref[pl.ds(CONV_K - 1 - j, ts), :]
            dx = dx + shifted * w_ref[j:j + 1, :]
            acc_ref[j:j + 1, :] += jnp.sum(xv * shifted, axis=0, keepdims=True)
        dx_ref[...] = _bf(dx)
        dbuf_ref[pl.ds(ts, 8), :] = dconv[0:8, :]

    tile = pl.BlockSpec((ts, cb), lambda c, i: (nt - 1 - i, c))
    wide = pl.BlockSpec((ts, cb), lambda c, i: (nt - 1 - i, c0 + c))
    return pl.pallas_call(
        body, name="conv_bwd", grid=(width // cb, nt),
        in_specs=[wide, tile, tile, pl.BlockSpec((8, cb), lambda c, i: (0, c)), pl.BlockSpec(memory_space=pl.ANY)],
        out_specs=[wide, pl.BlockSpec((8, cb), lambda c, i: (0, c))],
        out_shape=[jax.ShapeDtypeStruct(dproj.shape, dproj.dtype), jax.ShapeDtypeStruct((8, width), F32)],
        input_output_aliases={4: 0},
        scratch_shapes=[pltpu.VMEM((ts + 8, cb), F32)],
        compiler_params=_cparams("parallel", "arbitrary"))(proj, dact, dpost, w8, dproj)


def _mlstm_gates(gif_ref, gt_ref, a_scr, at_scr):
    L = gif_ref.shape[0]
    fb = _logsigmoid(gif_ref[...])
    a_scr[...] = _pick_left(_tri(L, False), fb, 3)
    at_scr[...] = _pick_right(_logsigmoid(gt_ref[...]), _tri(L, True), 3)
    return jnp.sum(fb, axis=0, keepdims=True)


def _mlstm_head(h, qk_ref, v_ref, gif, gt_ref, a, at_scr, a_last_row, c_mat, n_row, m_prev):
    L = gif.shape[0]
    q = qk_ref[:, h * ML_DQK:(h + 1) * ML_DQK] * (ML_DQK ** -0.5)
    k = qk_ref[:, (ML_HEADS + h) * ML_DQK:(ML_HEADS + h + 1) * ML_DQK]
    v = v_ref[:, h * ML_DV:(h + 1) * ML_DV]
    i_col, a_col = _lane_col(gif, h), _lane_col(a, ML_HEADS + h)
    i_row, a_row = gt_ref[h:h + 1, :], at_scr[ML_HEADS + h:ML_HEADS + h + 1, :]
    causal = lax.broadcasted_iota(jnp.int32, (L, L), 0) >= lax.broadcasted_iota(jnp.int32, (L, L), 1)
    dmat = jnp.where(causal, a_col - a_row + i_row, NEG)
    inter = a_col + m_prev
    m_t = jnp.maximum(inter, jnp.max(dmat, axis=1, keepdims=True))
    w_intra = jnp.exp(dmat - m_t)
    w_inter = jnp.exp(inter - m_t)
    sc = _nt(_bf(q), _bf(k)) * w_intra
    den = jnp.sum(sc, axis=1, keepdims=True) + w_inter * jnp.sum(q * n_row, axis=1, keepdims=True)
    floor = jnp.exp(-m_t)
    a_last = _lane_col(a_last_row, ML_HEADS + h)
    g = a_last - a_col + i_col
    m_new = jnp.maximum(a_last + m_prev, jnp.max(g, axis=0, keepdims=True))
    wk = jnp.exp(g - m_new)
    decay = jnp.exp(a_last + m_prev - m_new)
    return dict(q=q, k=k, v=v, w_intra=w_intra, w_inter=w_inter, sc=sc, den=den, floor=floor, m_new=m_new, wk=wk,
                decay=decay)


def _state_tile(n_row, m11):
    r = lax.broadcasted_iota(jnp.int32, (8, LANE), 0)
    return jnp.where(r == 0, n_row, jnp.where(r == 1, m11, 0.0))


def _mlstm_fwd(qk, proj, gates, gt):
    s = qk.shape[0]
    L = CHUNK
    nc = s // L

    def body(qk_ref, v_ref, gif_ref, gt_ref, h_ref, cst_ref, nm_ref, c_scr, nm_scr, a_scr, at_scr):
        @pl.when(pl.program_id(0) == 0)
        def _():
            c_scr[...] = jnp.zeros_like(c_scr)
            nm_scr[...] = jnp.zeros_like(nm_scr)
        a_last_row = _mlstm_gates(gif_ref, gt_ref, a_scr, at_scr)
        gif, a = gif_ref[...], a_scr[...]
        for h in range(ML_HEADS):
            c_mat, n_row = c_scr[h], nm_scr[h, 0:1, :]
            m_prev = jnp.max(nm_scr[h, 1:2, :], axis=1, keepdims=True)
            cst_ref[0, h] = c_mat
            nm_ref[0, h] = nm_scr[h]
            t = _mlstm_head(h, qk_ref, v_ref, gif, gt_ref, a, at_scr, a_last_row, c_mat, n_row, m_prev)
            num = _nn(_bf(t["sc"]), _bf(t["v"])) + t["w_inter"] * _nn(_bf(t["q"]), _bf(c_mat))
            h_ref[:, h * ML_DV:(h + 1) * ML_DV] = _bf(num / jnp.maximum(jnp.abs(t["den"]), t["floor"]))
            kw = t["k"] * t["wk"]
            c_scr[h] = t["decay"] * c_mat + _tn(_bf(kw), _bf(t["v"]))
            nm_scr[h] = _state_tile(t["decay"] * n_row + jnp.sum(kw, axis=0, keepdims=True), t["m_new"])

    return pl.pallas_call(
        body, name="mlstm_fwd", grid=(nc,),
        in_specs=[pl.BlockSpec((L, 2048), lambda c: (c, 0)), pl.BlockSpec((L, 2048), lambda c: (c, O_V // 2048)),
                  pl.BlockSpec((L, LANE), lambda c: (c, 0)), pl.BlockSpec((LANE, L), lambda c: (0, c))],
        out_specs=[pl.BlockSpec((L, 2048), lambda c: (c, 0)),
                   pl.BlockSpec((1, ML_HEADS, ML_DQK, ML_DV), lambda c: (c, 0, 0, 0)),
                   pl.BlockSpec((1, ML_HEADS, 8, LANE), lambda c: (c, 0, 0, 0))],
        out_shape=[jax.ShapeDtypeStruct((s, 2048), BF16), jax.ShapeDtypeStruct((nc, ML_HEADS, ML_DQK, ML_DV), F32),
                   jax.ShapeDtypeStruct((nc, ML_HEADS, 8, LANE), F32)],
        scratch_shapes=[pltpu.VMEM((ML_HEADS, ML_DQK, ML_DV), F32), pltpu.VMEM((ML_HEADS, 8, LANE), F32),
                        pltpu.VMEM((L, LANE), F32), pltpu.VMEM((LANE, L), F32)],
        compiler_params=_cparams("arbitrary"))(qk, proj, gates, gt)


def _mlstm_bwd(qk, proj, gates, gt, hout, dh, cst, nm, dproj):
    s = qk.shape[0]
    L = CHUNK
    nc = s // L

    def body(qk_ref, v_ref, gif_ref, gt_ref, h_ref, dh_ref, cst_ref, nm_ref, _, dqk_ref, dv_ref, dif_ref,
             dc_scr, dn_scr, a_scr, at_scr):
        @pl.when(pl.program_id(0) == 0)
        def _():
            dc_scr[...] = jnp.zeros_like(dc_scr)
            dn_scr[...] = jnp.zeros_like(dn_scr)
        a_last_row = _mlstm_gates(gif_ref, gt_ref, a_scr, at_scr)
        gif, a = gif_ref[...], a_scr[...]
        lane = lax.broadcasted_iota(jnp.int32, (L, LANE), 1)
        last = lax.broadcasted_iota(jnp.int32, (L, 1), 0) == L - 1
        di_tile = jnp.zeros((L, LANE), F32)
        cross = [jnp.zeros((L, LANE), F32)] * 3
        dlogw = []
        for h in range(ML_HEADS):
            c_mat, n_row = cst_ref[0, h], nm_ref[0, h, 0:1, :]
            m_prev = jnp.max(nm_ref[0, h, 1:2, :], axis=1, keepdims=True)
            t = _mlstm_head(h, qk_ref, v_ref, gif, gt_ref, a, at_scr, a_last_row, c_mat, n_row, m_prev)
            q, k, v, den = t["q"], t["k"], t["v"], t["den"]
            dhh = dh_ref[:, h * ML_DV:(h + 1) * ML_DV].astype(F32)
            hh = h_ref[:, h * ML_DV:(h + 1) * ML_DV].astype(F32)
            dnorm = jnp.maximum(jnp.abs(den), t["floor"])
            dnum = dhh / dnorm
            d_dn = -jnp.sum(dhh * hh, axis=1, keepdims=True) / dnorm
            dden = jnp.where(jnp.abs(den) >= t["floor"], jnp.where(den >= 0.0, d_dn, -d_dn), 0.0)
            dsc = _nt(_bf(dnum), _bf(v)) + dden
            ds = dsc * t["w_intra"]
            dq_inter = t["w_inter"] * (_nt(_bf(dnum), _bf(c_mat)) + dden * n_row)
            dq = _nn(_bf(ds), _bf(k)) + dq_inter
            dc, dn_row = dc_scr[h], dn_scr[h, 0:1, :]
            dk_state = t["wk"] * (_nt(_bf(v), _bf(dc)) + dn_row)
            dk = _tn(_bf(ds), _bf(q)) + dk_state
            dv = _tn(_bf(t["sc"]), _bf(dnum)) + t["wk"] * _nn(_bf(k), _bf(dc))
            qi = q * t["w_inter"]
            dc_scr[h] = t["decay"] * dc + _tn(_bf(qi), _bf(dnum))
            dn_scr[h] = jnp.broadcast_to(t["decay"] * dn_row + jnp.sum(qi * dden, axis=0, keepdims=True), (8, LANE))
            dqk_ref[:, h * ML_DQK:(h + 1) * ML_DQK] = _bf(dq * (ML_DQK ** -0.5))
            dqk_ref[:, (ML_HEADS + h) * ML_DQK:(ML_HEADS + h + 1) * ML_DQK] = _bf(dk)
            dv_ref[:, h * ML_DV:(h + 1) * ML_DV] = _bf(dv)
            di_tile = di_tile + jnp.where(lane == h, jnp.sum(k * dk, axis=1, keepdims=True), 0.0)
            carried = t["decay"] * (_sum_all(dc * c_mat) + jnp.sum(dn_row * n_row, axis=1, keepdims=True))
            dlogw.append(dsc * t["sc"])
            parts = (jnp.sum(q * dq_inter, axis=1, keepdims=True) + jnp.where(last, carried, 0.0),
                     jnp.sum(k * dk_state, axis=1, keepdims=True))
            cross[1:] = [c + jnp.where(lane == ML_HEADS + h, p, 0.0) for c, p in zip(cross[1:], parts)]
        for h, col in enumerate(_crossing(jnp.concatenate(dlogw, axis=1))):
            cross[0] = cross[0] + jnp.where(lane == ML_HEADS + h, col, 0.0)
        dfb = cross[0] + _pick_left(_tri(L, True), cross[1], 2) + _pick_left(_tri(L, False) - _eye(L), cross[2], 2)
        dif_ref[:, 0:LANE] = _bf(di_tile + dfb * _sigmoid(-gif))
        dif_ref[:, LANE:SMALL_W] = jnp.zeros((L, SMALL_W - LANE), BF16)

    rev = lambda c: nc - 1 - c
    return pl.pallas_call(
        body, name="mlstm_bwd", grid=(nc,),
        in_specs=[pl.BlockSpec((L, 2048), lambda c: (rev(c), 0)), pl.BlockSpec((L, 2048), lambda c: (rev(c), O_V // 2048)),
                  pl.BlockSpec((L, LANE), lambda c: (rev(c), 0)), pl.BlockSpec((LANE, L), lambda c: (0, rev(c))),
                  pl.BlockSpec((L, 2048), lambda c: (rev(c), 0)), pl.BlockSpec((L, 2048), lambda c: (rev(c), 0)),
                  pl.BlockSpec((1, ML_HEADS, ML_DQK, ML_DV), lambda c: (rev(c), 0, 0, 0)),
                  pl.BlockSpec((1, ML_HEADS, 8, LANE), lambda c: (rev(c), 0, 0, 0)), pl.BlockSpec(memory_space=pl.ANY)],
        out_specs=[pl.BlockSpec((L, 2048), lambda c: (rev(c), 0)), pl.BlockSpec((L, 2048), lambda c: (rev(c), O_V // 2048)),
                   pl.BlockSpec((L, SMALL_W), lambda c: (rev(c), 0))],
        out_shape=[jax.ShapeDtypeStruct((s, 2048), BF16), jax.ShapeDtypeStruct(dproj.shape, dproj.dtype),
                   jax.ShapeDtypeStruct((s, SMALL_W), BF16)],
        input_output_aliases={8: 1},
        scratch_shapes=[pltpu.VMEM((ML_HEADS, ML_DQK, ML_DV), F32), pltpu.VMEM((ML_HEADS, 8, LANE), F32),
                        pltpu.VMEM((L, LANE), F32), pltpu.VMEM((LANE, L), F32)],
        compiler_params=_cparams("arbitrary"))(qk, proj, gates, gt, hout, dh, cst, nm, dproj)


GROUP_W = SSM_HEADS // SSM_GROUPS * SSM_HEADDIM
O_B = SSM_HEADS * SSM_HEADDIM
O_C = O_B + SSM_GROUPS * SSM_STATE


def _head_expand():
    r = jnp.arange(LANE)[:, None]
    c = jnp.arange(SSM_HEADS * SSM_HEADDIM)[None, :] // SSM_HEADDIM
    return (r == c).astype(F32)


def _ssd_gates(dt_ref, dtt_ref, alog_row_ref, alog_col_ref, at_scr):
    L = dt_ref.shape[0]
    dt = _softplus(dt_ref[...])
    acoef = -jnp.exp(alog_row_ref[...])
    a = _pick_left(_tri(L, False), dt * acoef, 3)
    at_scr[...] = _pick_right(_softplus(dtt_ref[...]) * (-jnp.exp(alog_col_ref[...])), _tri(L, True), 3)
    return dt, acoef, a


def _ssd_group(g, xbc_ref, dt, a, e_ref, ax_scr):
    eg = e_ref[:, g * GROUP_W:(g + 1) * GROUP_W]
    ax_scr[...] = _pick_right(a, eg, 3)
    ax = ax_scr[...]
    alx = ax_scr[ax.shape[0] - 1:ax.shape[0], :]
    dtx = _pick_right(dt, eg, 2)
    xg = xbc_ref[:, g * GROUP_W:(g + 1) * GROUP_W]
    bg = xbc_ref[:, O_B + g * SSM_STATE:O_B + (g + 1) * SSM_STATE]
    cg = xbc_ref[:, O_C + g * SSM_STATE:O_C + (g + 1) * SSM_STATE]
    return dict(ax=ax, alx=alx, dtx=dtx, xg=xg, bg=bg, cg=cg, xdt=xg * dtx, gmat=_nt(_bf(cg), _bf(bg)))


def _ssd_decay(hh, a, at_scr):
    L = a.shape[0]
    causal = lax.broadcasted_iota(jnp.int32, (L, L), 0) >= lax.broadcasted_iota(jnp.int32, (L, L), 1)
    return jnp.exp(jnp.where(causal, _lane_col(a, hh) - at_scr[hh:hh + 1, :], NEG))


def _ssd_fwd(xbc, gates, dtt, alog_row, alog_col, dskip_x, expand):
    s = xbc.shape[0]
    L = CHUNK
    nc = s // L
    half = SSM_HEADDIM

    def body(xbc_ref, dt_ref, dtt_ref, ar_ref, ac_ref, dk_ref, e_ref, y_ref, st_ref, st_scr, at_scr, ax_scr):
        @pl.when(pl.program_id(0) == 0)
        def _():
            st_scr[...] = jnp.zeros_like(st_scr)
        dt, _, a = _ssd_gates(dt_ref, dtt_ref, ar_ref, ac_ref, at_scr)
        lane = lax.broadcasted_iota(jnp.int32, (L, LANE), 1)
        for g in range(SSM_GROUPS):
            t = _ssd_group(g, xbc_ref, dt, a, e_ref, ax_scr)
            st = st_scr[g]
            st_ref[0, g] = st
            pairs = []
            for j in range(GROUP_W // LANE):
                xp = _bf(t["xdt"][:, j * LANE:(j + 1) * LANE])
                hh = g * (SSM_HEADS // SSM_GROUPS) + 2 * j
                both = jnp.concatenate([_bf(t["gmat"] * _ssd_decay(hh, a, at_scr)),
                                        _bf(t["gmat"] * _ssd_decay(hh + 1, a, at_scr))], axis=0)
                ys = _nn(both, xp)
                pairs.append(jnp.where(lane < half, ys[0:L], ys[L:2 * L]))
            y = jnp.concatenate(pairs, axis=1) + _nn(_bf(t["cg"]), _bf(st)) * jnp.exp(t["ax"])
            y_ref[:, g * GROUP_W:(g + 1) * GROUP_W] = _bf(y + dk_ref[:, g * GROUP_W:(g + 1) * GROUP_W] * t["xg"])
            wts = jnp.exp(t["alx"] - t["ax"])
            st_scr[g] = jnp.exp(t["alx"]) * st + _tn(_bf(t["bg"]), _bf(t["xdt"] * wts))

    row = lambda w: pl.BlockSpec((1, w), lambda c: (0, 0))
    return pl.pallas_call(
        body, name="ssd_fwd", grid=(nc,),
        in_specs=[pl.BlockSpec((L, 3072), lambda c: (c, 0)), pl.BlockSpec((L, LANE), lambda c: (c, (O_DT - O_IF) // LANE)),
                  pl.BlockSpec((LANE, L), lambda c: (0, c)), row(LANE), pl.BlockSpec((LANE, 1), lambda c: (0, 0)),
                  row(2048), pl.BlockSpec((LANE, 2048), lambda c: (0, 0))],
        out_specs=[pl.BlockSpec((L, 2048), lambda c: (c, 0)),
                   pl.BlockSpec((1, SSM_GROUPS, SSM_STATE, GROUP_W), lambda c: (c, 0, 0, 0))],
        out_shape=[jax.ShapeDtypeStruct((s, 2048), BF16),
                   jax.ShapeDtypeStruct((nc, SSM_GROUPS, SSM_STATE, GROUP_W), F32)],
        scratch_shapes=[pltpu.VMEM((SSM_GROUPS, SSM_STATE, GROUP_W), F32), pltpu.VMEM((LANE, L), F32),
                        pltpu.VMEM((L, GROUP_W), F32)],
        compiler_params=_cparams("arbitrary"))(xbc, gates, dtt, alog_row, alog_col, dskip_x, expand)


def _ssd_bwd(xbc, gates, dtt, alog_row, alog_col, dskip_x, expand, expand_t, dy, states):
    s = xbc.shape[0]
    L = CHUNK
    nc = s // L
    half = SSM_HEADDIM

    def body(xbc_ref, dt_ref, dtt_ref, ar_ref, ac_ref, dk_ref, e_ref, et_ref, dy_ref, st_ref,
             dxbc_ref, ddt_ref, accd_ref, acca_ref, dst_scr, at_scr, ax_scr):
        @pl.when(pl.program_id(0) == 0)
        def _():
            dst_scr[...] = jnp.zeros_like(dst_scr)
            accd_ref[...] = jnp.zeros_like(accd_ref)
            acca_ref[...] = jnp.zeros_like(acca_ref)
        dt, acoef, a = _ssd_gates(dt_ref, dtt_ref, ar_ref, ac_ref, at_scr)
        lane = lax.broadcasted_iota(jnp.int32, (L, LANE), 1)
        low = lane < half
        last = lax.broadcasted_iota(jnp.int32, (L, 1), 0) == L - 1
        cross = [jnp.zeros((L, LANE), F32)] * 3
        ddt_tile = jnp.zeros((L, LANE), F32)
        for g in range(SSM_GROUPS):
            t = _ssd_group(g, xbc_ref, dt, a, e_ref, ax_scr)
            xg, bg, cg, xdt, gmat = t["xg"], t["bg"], t["cg"], t["xdt"], t["gmat"]
            st, dst = st_ref[0, g], dst_scr[g]
            dyg = dy_ref[:, g * GROUP_W:(g + 1) * GROUP_W].astype(F32)
            ea, eal = jnp.exp(t["ax"]), jnp.exp(t["alx"])
            wts = jnp.exp(t["alx"] - t["ax"])
            dyi = dyg * ea
            y_inter = _nn(_bf(cg), _bf(st)) * ea
            dc = _nt(_bf(dyi), _bf(st))
            d_xdt_state = _nn(_bf(bg), _bf(dst)) * wts
            db = _nt(_bf(xdt * wts), _bf(dst))
            dst_scr[g] = eal * dst + _tn(_bf(cg), _bf(dyi))
            dg = jnp.zeros((L, L), F32)
            dx_pairs, dlogw = [], []
            for j in range(GROUP_W // LANE):
                xp = _bf(xdt[:, j * LANE:(j + 1) * LANE])
                dyp = dyg[:, j * LANE:(j + 1) * LANE]
                hh = g * (SSM_HEADS // SSM_GROUPS) + 2 * j
                decs = [_ssd_decay(hh, a, at_scr), _ssd_decay(hh + 1, a, at_scr)]
                ws = [gmat * decs[0], gmat * decs[1]]
                dxs = _tn(_bf(jnp.concatenate(ws, axis=1)), _bf(dyp))
                dws = _nt(_bf(jnp.concatenate([jnp.where(low, dyp, 0.0), jnp.where(low, 0.0, dyp)], axis=0)), xp)
                dw0, dw1 = dws[0:L], dws[L:2 * L]
                dg = dg + dw0 * decs[0] + dw1 * decs[1]
                dlogw += [dw0 * ws[0], dw1 * ws[1]]
                dx_pairs.append(jnp.where(low, dxs[0:L], dxs[L:2 * L]))
            for b, col in enumerate(_crossing(jnp.concatenate(dlogw, axis=1))):
                cross[0] = cross[0] + jnp.where(lane == g * (SSM_HEADS // SSM_GROUPS) + b, col, 0.0)
            d_xdt = d_xdt_state + jnp.concatenate(dx_pairs, axis=1)
            dc = dc + _nn(_bf(dg), _bf(bg))
            db = db + _tn(_bf(dg), _bf(cg))
            etg = et_ref[g * GROUP_W:(g + 1) * GROUP_W, :]
            carried = jnp.sum(dst * st, axis=0, keepdims=True) * eal
            cross[1] = cross[1] + _pick_right(dyg * y_inter + jnp.where(last, carried, 0.0), etg, 2)
            cross[2] = cross[2] + _pick_right(xdt * d_xdt_state, etg, 2)
            ddt_tile = ddt_tile + _pick_right(d_xdt * xg, etg, 2)
            dxbc_ref[:, g * GROUP_W:(g + 1) * GROUP_W] = _bf(d_xdt * t["dtx"] + dk_ref[:, g * GROUP_W:(g + 1) * GROUP_W] * dyg)
            dxbc_ref[:, O_B + g * SSM_STATE:O_B + (g + 1) * SSM_STATE] = _bf(db)
            dxbc_ref[:, O_C + g * SSM_STATE:O_C + (g + 1) * SSM_STATE] = _bf(dc)
            accd_ref[0:1, g * GROUP_W:(g + 1) * GROUP_W] += jnp.sum(dyg * xg, axis=0, keepdims=True)
        d_da = cross[0] + _pick_left(_tri(L, True), cross[1], 2) + _pick_left(_tri(L, False) - _eye(L), cross[2], 2)
        acca_ref[0:1, :] += jnp.sum(d_da * dt, axis=0, keepdims=True)
        ddt_ref[:, 0:LANE] = _bf((ddt_tile + d_da * acoef) * _sigmoid(dt_ref[...]))
        ddt_ref[:, LANE:SMALL_W] = jnp.zeros((L, SMALL_W - LANE), BF16)

    rev = lambda c: nc - 1 - c
    row = lambda w: pl.BlockSpec((1, w), lambda c: (0, 0))
    return pl.pallas_call(
        body, name="ssd_bwd", grid=(nc,),
        in_specs=[pl.BlockSpec((L, 3072), lambda c: (rev(c), 0)), pl.BlockSpec((L, LANE), lambda c: (rev(c), (O_DT - O_IF) // LANE)),
                  pl.BlockSpec((LANE, L), lambda c: (0, rev(c))), row(LANE), pl.BlockSpec((LANE, 1), lambda c: (0, 0)),
                  row(2048), pl.BlockSpec((LANE, 2048), lambda c: (0, 0)), pl.BlockSpec((2048, LANE), lambda c: (0, 0)),
                  pl.BlockSpec((L, 2048), lambda c: (rev(c), 0)),
                  pl.BlockSpec((1, SSM_GROUPS, SSM_STATE, GROUP_W), lambda c: (rev(c), 0, 0, 0))],
        out_specs=[pl.BlockSpec((L, 3072), lambda c: (rev(c), 0)), pl.BlockSpec((L, SMALL_W), lambda c: (rev(c), 0)),
                   pl.BlockSpec((8, 2048), lambda c: (0, 0)), pl.BlockSpec((8, LANE), lambda c: (0, 0))],
        out_shape=[jax.ShapeDtypeStruct((s, 3072), BF16), jax.ShapeDtypeStruct((s, SMALL_W), BF16),
                   jax.ShapeDtypeStruct((8, 2048), F32), jax.ShapeDtypeStruct((8, LANE), F32)],
        scratch_shapes=[pltpu.VMEM((SSM_GROUPS, SSM_STATE, GROUP_W), F32),
                        pltpu.VMEM((LANE, L), F32), pltpu.VMEM((L, GROUP_W), F32)],
        compiler_params=_cparams("arbitrary"))(xbc, gates, dtt, alog_row, alog_col, dskip_x, expand, expand_t, dy, states)


def _group_norm(v, width):
    outs, rs = [], []
    for k in range(v.shape[1] // width):
        blk = v[:, k * width:(k + 1) * width]
        r = lax.rsqrt(jnp.mean(blk * blk, axis=1, keepdims=True) + EPS)
        outs.append(blk * r)
        rs.append(jnp.broadcast_to(r, blk.shape))
    return jnp.concatenate(outs, axis=1), jnp.concatenate(rs, axis=1)


def _group_mean(v, width):
    return jnp.concatenate([jnp.broadcast_to(jnp.mean(v[:, k * width:(k + 1) * width], axis=1, keepdims=True),
                                             (v.shape[0], width)) for k in range(v.shape[1] // width)], axis=1)


def _post_fwd(hm, yssd, proj, ml_norm_w, ssm_norm_w, ts):
    s = hm.shape[0]

    def body(h_ref, ys_ref, o_ref, zm_ref, zs_ref, wm_ref, ws_ref, ym_ref, yso_ref):
        hn, _ = _group_norm(h_ref[...].astype(F32), ML_DV)
        ym_ref[...] = _bf(_sigmoid(o_ref[...].astype(F32)) * hn * wm_ref[...] * _silu(zm_ref[...].astype(F32)))
        pn, _ = _group_norm(ys_ref[...].astype(F32) * _silu(zs_ref[...].astype(F32)), GROUP_W)
        yso_ref[...] = _bf(pn * ws_ref[...])

    tile = pl.BlockSpec((ts, 2048), lambda i: (i, 0))
    col = lambda off: pl.BlockSpec((ts, 2048), lambda i: (i, off // 2048))
    row = pl.BlockSpec((1, 2048), lambda i: (0, 0))
    return pl.pallas_call(
        body, name="post_fwd", grid=(s // ts,),
        in_specs=[tile, tile, col(O_O), col(O_ZM), col(O_ZS), row, row],
        out_specs=[tile, tile],
        out_shape=[jax.ShapeDtypeStruct((s, 2048), BF16)] * 2,
        compiler_params=_cparams("parallel"))(hm, yssd, proj, proj, proj, ml_norm_w, ssm_norm_w)


def _post_bwd(dym, dys, hm, yssd, proj, ml_norm_w, ssm_norm_w, dproj, ts):
    s = hm.shape[0]

    def body(dym_ref, dys_ref, h_ref, ys_ref, o_ref, zm_ref, zs_ref, wm_ref, ws_ref, _,
             dh_ref, dyssd_ref, dp_ref, acc_ref):
        @pl.when(pl.program_id(0) == 0)
        def _():
            acc_ref[...] = jnp.zeros_like(acc_ref)
        hn, r = _group_norm(h_ref[...].astype(F32), ML_DV)
        so, zm, wm, d_ym = _sigmoid(o_ref[...].astype(F32)), zm_ref[...].astype(F32), wm_ref[...], dym_ref[...].astype(F32)
        sz = _silu(zm)
        hnw = hn * wm
        dp_ref[:, O_O:O_O + 2048] = _bf(d_ym * hnw * sz * so * (1.0 - so))
        dp_ref[:, O_ZM:O_ZM + 2048] = _bf(d_ym * so * hnw * _dsilu(zm))
        dhnw = d_ym * so * sz
        acc_ref[0:1, :] += jnp.sum(dhnw * hn, axis=0, keepdims=True)
        dhn = dhnw * wm
        dh_ref[...] = _bf(r * (dhn - hn * _group_mean(dhn * hn, ML_DV)))
        ysv, zs, d_ys = ys_ref[...].astype(F32), zs_ref[...].astype(F32), dys_ref[...].astype(F32)
        szs = _silu(zs)
        pn, r2 = _group_norm(ysv * szs, GROUP_W)
        acc_ref[1:2, :] += jnp.sum(d_ys * pn, axis=0, keepdims=True)
        dpn = d_ys * ws_ref[...]
        dp = r2 * (dpn - pn * _group_mean(dpn * pn, GROUP_W))
        dyssd_ref[...] = _bf(dp * szs)
        dp_ref[:, O_ZS:O_ZS + 2048] = _bf(dp * ysv * _dsilu(zs))

    tile = pl.BlockSpec((ts, 2048), lambda i: (i, 0))
    col = lambda off: pl.BlockSpec((ts, 2048), lambda i: (i, off // 2048))
    row = pl.BlockSpec((1, 2048), lambda i: (0, 0))
    sds = lambda dt: jax.ShapeDtypeStruct((s, 2048), dt)
    return pl.pallas_call(
        body, name="post_bwd", grid=(s // ts,),
        in_specs=[tile, tile, tile, tile, col(O_O), col(O_ZM), col(O_ZS), row, row, pl.BlockSpec(memory_space=pl.ANY)],
        out_specs=[tile, tile, pl.BlockSpec((ts, O_MG), lambda i: (i, 0)), pl.BlockSpec((8, 2048), lambda i: (0, 0))],
        out_shape=[sds(BF16), sds(BF16), jax.ShapeDtypeStruct(dproj.shape, dproj.dtype), jax.ShapeDtypeStruct((8, 2048), F32)],
        input_output_aliases={9: 2},
        compiler_params=_cparams("arbitrary"))(dym, dys, hm, yssd, proj, proj, proj, ml_norm_w, ssm_norm_w, dproj)


def _merge(x, ym, ys, proj, target, gate, final_w, wpm, wps, wo, ts):
    wpm_t, wps_t, wo_t = wpm.T, wps.T, wo.T
    s, d = x.shape

    def body(x_ref, ym_ref, ys_ref, mg_ref, t_ref, gate_ref, fw_ref, wpm_ref, wps_ref, wo_ref, wpmt_ref, wpst_ref, wot_ref,
             dres_ref, mer_ref, dmo_ref, dpm_ref, dps_ref, dym_ref, dys_ref, dmg_ref, acc_ref):
        @pl.when(pl.program_id(0) == 0)
        def _():
            acc_ref[...] = jnp.zeros_like(acc_ref)
        gm, gs = _sigmoid(mg_ref[:, 0:d].astype(F32)), _sigmoid(mg_ref[:, d:2 * d].astype(F32))
        pm = _nn(ym_ref[...], wpm_ref[...])
        ps = _nn(ys_ref[...], wps_ref[...])
        merged = _bf(gm * pm + gs * ps)
        mer_ref[...] = merged
        mo = _nn(merged, wo_ref[...])
        gate, fw = gate_ref[...], fw_ref[...]
        out = x_ref[...] + gate * mo
        r = lax.rsqrt(jnp.mean(out * out, axis=1, keepdims=True) + EPS)
        on = out * r
        diff = on * fw - t_ref[...]
        acc_ref[0:1, :] += jnp.sum(0.5 * jnp.sum(diff * diff, axis=1, keepdims=True) / d, axis=0, keepdims=True)
        dyv = diff * (1.0 / d)
        acc_ref[1:2, :] += jnp.sum(dyv * on, axis=0, keepdims=True)
        don = dyv * fw
        dout = r * (don - on * jnp.mean(don * on, axis=1, keepdims=True))
        dres_ref[...] = dout
        acc_ref[2:3, :] += jnp.sum(dout * mo, axis=0, keepdims=True)
        dmo = _bf(dout * gate)
        dmo_ref[...] = dmo
        dmer = _nn(dmo, wot_ref[...])
        dpm, dps = _bf(dmer * gm), _bf(dmer * gs)
        dpm_ref[...] = dpm
        dps_ref[...] = dps
        dmg_ref[:, 0:d] = _bf(dmer * pm * gm * (1.0 - gm))
        dmg_ref[:, d:2 * d] = _bf(dmer * ps * gs * (1.0 - gs))
        dym_ref[...] = _bf(_nn(dpm, wpmt_ref[...]))
        dys_ref[...] = _bf(_nn(dps, wpst_ref[...]))

    t1 = pl.BlockSpec((ts, d), lambda i: (i, 0))
    t2 = pl.BlockSpec((ts, 2 * d), lambda i: (i, 0))
    row = pl.BlockSpec((1, d), lambda i: (0, 0))
    whole = pl.BlockSpec(memory_space=pltpu.VMEM)
    sd = lambda w, dt: jax.ShapeDtypeStruct((s, w), dt)
    return pl.pallas_call(
        body, name="merge_fwd_bwd", grid=(s // ts,),
        in_specs=[t1, t2, t2, pl.BlockSpec((ts, 2 * d), lambda i: (i, O_MG // (2 * d))), t1, row, row] + [whole] * 6,
        out_specs=[t1, t1, t1, t1, t1, t2, t2, pl.BlockSpec((ts, 2 * d), lambda i: (i, O_MG // (2 * d))),
                   pl.BlockSpec((8, d), lambda i: (0, 0))],
        out_shape=[sd(d, F32), sd(d, BF16), sd(d, BF16), sd(d, BF16), sd(d, BF16), sd(2 * d, BF16), sd(2 * d, BF16),
                   sd(NP, BF16), jax.ShapeDtypeStruct((8, d), F32)],
        compiler_params=_cparams("arbitrary", vmem=MERGE_VMEM))(x, ym, ys, proj, target, gate, final_w, wpm, wps, wo, wpm_t, wps_t, wo_t)


def _adamw(w, g, m, v, tr):
    if w.ndim == 2 and w.shape[0] % 8:
        tile, steps = pl.BlockSpec((w.shape[0], tr), lambda i: (0, i)), w.shape[1] // tr
    else:
        lead = (None,) * (w.ndim - 2)
        tile, steps = pl.BlockSpec(lead + (tr, w.shape[-1]), lambda i: (0,) * len(lead) + (i, 0)), w.shape[-2] // tr

    def body(w_ref, g_ref, m_ref, v_ref, d_ref, nm_ref, nv_ref):
        gv = g_ref[...]
        m2 = ADAM_B1 * m_ref[...] + (1.0 - ADAM_B1) * gv
        v2 = ADAM_B2 * v_ref[...] + (1.0 - ADAM_B2) * (gv * gv)
        m_hat = m2 / (1.0 - ADAM_B1 ** ADAM_STEP)
        v_hat = v2 / (1.0 - ADAM_B2 ** ADAM_STEP)
        d_ref[...] = -ADAM_LR * (m_hat / (jnp.sqrt(v_hat) + ADAM_EPS) + ADAM_WD * w_ref[...])
        nm_ref[...] = m2
        nv_ref[...] = v2

    return pl.pallas_call(
        body, name="adamw", grid=(steps,), in_specs=[tile] * 4, out_specs=[tile] * 3,
        out_shape=[jax.ShapeDtypeStruct(w.shape, F32)] * 3,
        compiler_params=_cparams("parallel"))(w, g.reshape(w.shape), m, v)


def _sum_parts(own, parts, tr, dtype=F32):
    p, rows, cols = parts.shape

    def body(*refs):
        p_ref, o_ref = refs[-2], refs[-1]
        acc = p_ref[0].astype(F32) if own is None else refs[0][...].astype(F32) + p_ref[0].astype(F32)
        for i in range(1, p):
            acc = acc + p_ref[i].astype(F32)
        o_ref[...] = acc.astype(dtype)

    tile = pl.BlockSpec((tr, cols), lambda i: (i, 0))
    ins = ([] if own is None else [tile]) + [pl.BlockSpec((p, tr, cols), lambda i: (0, i, 0))]
    args = ([] if own is None else [own]) + [parts]
    return pl.pallas_call(
        body, name="sum_parts", grid=(rows // tr,), in_specs=ins, out_specs=tile,
        out_shape=jax.ShapeDtypeStruct((rows, cols), dtype), compiler_params=_cparams("parallel"))(*args)


def _position():
    return lax.axis_index("x"), lax.axis_index("y"), lax.axis_index("c")


def _flip(pos, k):
    return tuple(1 - p if (k >> s) & 1 else p for p, s in zip(pos, (2, 1, 0)))


def _allgather8(block):
    rows, cols = block.shape

    def body(x_ref, o_ref, send_sems, recv_sems, local_sem):
        pos = _position()
        me = 4 * pos[0] + 2 * pos[1] + pos[2]
        mine = pltpu.make_async_copy(x_ref, o_ref.at[me], local_sem)
        mine.start()
        copies = [pltpu.make_async_remote_copy(src_ref=x_ref, dst_ref=o_ref.at[me], send_sem=send_sems.at[k - 1],
                                               recv_sem=recv_sems.at[k - 1], device_id=_flip(pos, k), device_id_type=MESH)
                  for k in range(1, N_DEV)]
        for cp in copies:
            cp.start()
        for cp in copies:
            cp.wait()
        mine.wait()

    vmem = pl.BlockSpec(memory_space=pltpu.VMEM)
    return pl.pallas_call(
        body, name="allgather8", in_specs=[vmem], out_specs=vmem,
        out_shape=jax.ShapeDtypeStruct((N_DEV, rows, cols), block.dtype),
        scratch_shapes=[pltpu.SemaphoreType.DMA((N_DEV - 1,)), pltpu.SemaphoreType.DMA((N_DEV - 1,)),
                        pltpu.SemaphoreType.DMA],
        compiler_params=pltpu.CompilerParams(vmem_limit_bytes=VMEM_LIMIT))(block)


COPY_BYTES = 1 << 20


def _row_chunks(rows, row_bytes):
    n = max(1, min(rows // 16, -(-rows * row_bytes // COPY_BYTES)))
    while rows % (16 * n):
        n -= 1
    return [(i * (rows // n), rows // n) for i in range(n)]


def _split_start(name, make_copies, n_copies, sources, lands):
    n, m = len(sources), len(lands)

    def body(*refs):
        for cp in make_copies(_position(), refs[:n], refs[n:n + m], refs[n + m], refs[n + m + 1]):
            cp.start()
        refs[-1][...] = jnp.zeros((8, LANE), F32)

    hbm = pl.BlockSpec(memory_space=pltpu.HBM)
    sem = pl.BlockSpec(memory_space=pltpu.SEMAPHORE)
    operands = [pltpu.with_memory_space_constraint(t, pltpu.HBM) for t in list(sources) + list(lands)]
    out = pl.pallas_call(
        body, name=name, in_specs=[hbm] * (n + m),
        out_specs=[sem, sem] + [hbm] * (n + m) + [pl.BlockSpec(memory_space=pltpu.VMEM)],
        out_shape=[pltpu.SemaphoreType.DMA((n_copies,)), pltpu.SemaphoreType.DMA((n_copies,))]
        + [pltpu.HBM(t.shape, t.dtype) for t in operands] + [jax.ShapeDtypeStruct((8, LANE), F32)],
        input_output_aliases={i: 2 + i for i in range(n + m)},
        compiler_params=pltpu.CompilerParams(has_side_effects=pltpu.SideEffectType.DATAFLOW_SIDE_EFFECTING))(*operands)
    return out[0], out[1], out[2:2 + n], out[2 + n:2 + n + m], out[-1]


def _split_wait(name, make_copies, send_sems, recv_sems, sources, lands, after):
    n, m = len(sources), len(lands)

    def body(*refs):
        for cp in make_copies(_position(), refs[:n], refs[n:n + m], refs[n + m], refs[n + m + 1]):
            cp.wait_send()
            cp.wait_recv()

    hbm = pl.BlockSpec(memory_space=pltpu.HBM)
    sem = pl.BlockSpec(memory_space=pltpu.SEMAPHORE)
    out = pl.pallas_call(
        body, name=name, in_specs=[hbm] * (n + m) + [sem, sem, pl.BlockSpec(memory_space=pl.ANY)],
        out_specs=[hbm] * (n + m), out_shape=[pltpu.HBM(t.shape, t.dtype) for t in list(sources) + list(lands)],
        input_output_aliases={i: i for i in range(n + m)},
        compiler_params=pltpu.CompilerParams(has_side_effects=pltpu.SideEffectType.DATAFLOW_SIDE_EFFECTING))(
            *sources, *lands, send_sems, recv_sems, after)
    return out[:n], out[n:]


def _gather_copies(pos, halves, lands, send_sems, recv_sems):
    chip, core = 2 * pos[0] + pos[1], pos[2]
    copies = []
    for a in range(len(halves)):
        for k in range(1, N_CHIPS):
            for r0, nr in _row_chunks(halves[a].shape[1], halves[a].shape[2] * halves[a].dtype.itemsize):
                i = len(copies)
                copies.append(pltpu.make_async_remote_copy(
                    src_ref=halves[a].at[core, pl.ds(r0, nr)], dst_ref=lands[a].at[chip, core, pl.ds(r0, nr)],
                    send_sem=send_sems.at[i], recv_sem=recv_sems.at[i], device_id=_flip(pos, 2 * k), device_id_type=MESH))
    return copies


def _gather_pieces(halves):
    return (N_CHIPS - 1) * sum(len(_row_chunks(a.shape[1], a.shape[2] * a.dtype.itemsize)) for a in halves)


def _pair_forward(lands):
    n = len(lands)

    def plan(pos, ins, outs):
        remote = []
        for a in range(n):
            for k in range(1, N_CHIPS):
                there = _flip(pos, 2 * k)
                for r0, nr in _row_chunks(lands[a].shape[2], lands[a].shape[3] * lands[a].dtype.itemsize):
                    slot = (2 * there[0] + there[1], pos[2], pl.ds(r0, nr))
                    remote.append((ins[a].at[slot], outs[a].at[slot], _flip(pos, 1)))
        return remote, []

    return _exchange("pair_forward", lands, [jax.ShapeDtypeStruct(t.shape, t.dtype) for t in lands], plan,
                     _gather_pieces([jax.ShapeDtypeStruct(t.shape[1:], t.dtype) for t in lands]), 0, in_place=True)


def _exchange(name, arrays, out_shapes, plan, n_remote, n_local, in_place=False):
    n, m = len(arrays), len(out_shapes)

    def body(*refs):
        send_sems, recv_sems, local_sems = refs[n + m:]
        remote, local = plan(_position(), refs[:n], refs[n:n + m])
        assert (len(remote), len(local)) == (n_remote, n_local)
        copies = [pltpu.make_async_copy(src, dst, local_sems.at[i]) for i, (src, dst) in enumerate(local)]
        copies += [pltpu.make_async_remote_copy(src_ref=src, dst_ref=dst, send_sem=send_sems.at[i], recv_sem=recv_sems.at[i],
                                                device_id=dev, device_id_type=MESH)
                   for i, (src, dst, dev) in enumerate(remote)]
        for cp in copies:
            cp.start()
        for cp in copies:
            cp.wait()

    hbm = pl.BlockSpec(memory_space=pl.ANY)
    return pl.pallas_call(
        body, name=name, in_specs=[hbm] * n, out_specs=[hbm] * m, out_shape=out_shapes,
        input_output_aliases={i: i for i in range(n)} if in_place else {},
        scratch_shapes=[pltpu.SemaphoreType.DMA((n_remote,)), pltpu.SemaphoreType.DMA((n_remote,)),
                        pltpu.SemaphoreType.DMA((max(n_local, 1),))],
        compiler_params=pltpu.CompilerParams(has_side_effects=True))(*arrays)


def _pair_send(slabs):
    n = len(slabs)
    pieces = [_row_chunks(g.shape[2], g.shape[3] * g.dtype.itemsize) for g in slabs]

    def plan(pos, ins, outs):
        return [(ins[a].at[j, 1 - pos[2], pl.ds(r0, nr)], outs[a].at[j, pl.ds(r0, nr)], _flip(pos, 1))
                for a in range(n) for j in range(N_CHIPS) for r0, nr in pieces[a]], []

    return _exchange("pair_send", slabs, [jax.ShapeDtypeStruct((N_CHIPS,) + g.shape[2:], g.dtype) for g in slabs], plan,
                     N_CHIPS * sum(len(p) for p in pieces), 0)


def _chip_scatter_copies(pos, sums, lands, send_sems, recv_sems):
    copies = []
    for a in range(len(sums)):
        for k in range(1, N_CHIPS):
            to = _flip(pos, 2 * k)
            for r0, nr in _row_chunks(sums[a].shape[1], sums[a].shape[2] * sums[a].dtype.itemsize):
                i = len(copies)
                copies.append(pltpu.make_async_remote_copy(
                    src_ref=sums[a].at[2 * to[0] + to[1], pl.ds(r0, nr)], dst_ref=lands[a].at[k - 1, pl.ds(r0, nr)],
                    send_sem=send_sems.at[i], recv_sem=recv_sems.at[i], device_id=to, device_id_type=MESH))
    return copies


def _chip_scatter_start(sums):
    n_copies = (N_CHIPS - 1) * sum(len(_row_chunks(g.shape[1], g.shape[2] * g.dtype.itemsize)) for g in sums)
    lands = [lax.empty((N_CHIPS - 1,) + g.shape[1:], g.dtype) for g in sums]
    return _split_start("chip_scatter_start", _chip_scatter_copies, n_copies, sums, lands)


def _chip_scatter_wait(send_sems, recv_sems, sums, lands, after):
    return _split_wait("chip_scatter_wait", _chip_scatter_copies, send_sems, recv_sems, sums, lands, after)


def _pair_exchange(halves):
    n = len(halves)
    pieces = [_row_chunks(h.shape[0], h.shape[1] * h.dtype.itemsize) for h in halves]

    def plan(pos, ins, outs):
        return [(ins[a].at[pl.ds(r0, nr)], outs[a].at[pl.ds(r0, nr)], _flip(pos, 1))
                for a in range(n) for r0, nr in pieces[a]], []

    return _exchange("pair_exchange", halves, [jax.ShapeDtypeStruct(h.shape, h.dtype) for h in halves], plan,
                     sum(len(p) for p in pieces), 0)


def _pack(arrays):
    flat = jnp.concatenate([a.reshape(-1).astype(F32) for a in arrays])
    size = -(-flat.shape[0] // (8 * LANE)) * (8 * LANE)
    return jnp.pad(flat, (0, size - flat.shape[0])).reshape(size // LANE, LANE)


def _unpack(buf, shapes):
    flat = buf.reshape(-1)
    out, off = [], 0
    for shp in shapes:
        n = math.prod(shp)
        out.append(flat[off:off + n].reshape(shp))
        off += n
    return out


def _unpack_rows(bufs, shapes):
    flat = bufs.reshape(bufs.shape[0], -1)
    out, off = [], 0
    for shp in shapes:
        n = math.prod(shp)
        out.append(flat[:, off:off + n].reshape((bufs.shape[0],) + shp))
        off += n
    return out


def _taps8(w):
    return jnp.pad(w, ((0, 8 - CONV_K), (0, 0)))


def _local_step(xs, tgt, scale, shift, gate, norm_w, w_in_p, b_in_p, ml_conv_w, ml_conv_b, ml_norm_w, ssm_conv_w,
                ssm_conv_b, ssm_a_log, ssm_d, ssm_norm_w, wpm, wps, wo, final_w, start_exchange=None, late_weights=None,
                u=None):
    s = xs.shape[0]
    ts = min(512, s)
    tm = min(2048, s)
    if u is None:
        u = _prenorm_fwd(xs, norm_w, scale, shift, ts)
    proj = _matmul_bias(u, w_in_p, b_in_p, tm, 512, 0, O_IF, BF16)
    gates = _matmul_bias(u, w_in_p, b_in_p, tm, 512, O_IF, NP - O_IF, F32)
    mlw8, ssw8 = _taps8(ml_conv_w), _taps8(ssm_conv_w)
    qk, qk_dact = _conv_fwd(proj, O_QK, 2048, mlw8, ml_conv_b, ts)
    xbc, xbc_dact = _conv_fwd(proj, O_XBC, 3072, ssw8, ssm_conv_b, ts)
    gt = gates[:, :LANE].T
    dtt = gates[:, O_DT - O_IF:O_DT - O_IF + LANE].T
    hm, cst, nm = _mlstm_fwd(qk, proj, gates, gt)
    alog_row = jnp.pad(ssm_a_log, ((0, 0), (0, LANE - SSM_HEADS)))
    alog_col = alog_row.reshape(LANE, 1)
    dskip_x = jnp.repeat(ssm_d[0], SSM_HEADDIM)[None]
    expand = _head_expand()
    yssd, sst = _ssd_fwd(xbc, gates, dtt, alog_row, alog_col, dskip_x, expand)
    tp = min(128, s)
    ym, ys = _post_fwd(hm, yssd, proj, ml_norm_w, ssm_norm_w, tp)
    if late_weights is not None:
        wpm, wps, wo = late_weights(ym)
    dxres, merged, dmo, dpm, dps, dym, dys, dproj, acc_m = _merge(xs, ym, ys, proj, tgt, gate, final_w, wpm, wps, wo,
                                                                  min(256, s))
    dh, dyssd, dproj, acc_p = _post_bwd(dym, dys, hm, yssd, proj, ml_norm_w, ssm_norm_w, dproj, tp)
    dqk, dproj, dif = _mlstm_bwd(qk, proj, gates, gt, hm, dh, cst, nm, dproj)
    dxbc, ddt, accd, acca = _ssd_bwd(xbc, gates, dtt, alog_row, alog_col, dskip_x, expand, expand.T, dyssd, sst)
    dproj, acc_cq = _conv_bwd(proj, O_QK, 2048, mlw8, qk_dact, dqk, dproj, ts)
    dproj, acc_cx = _conv_bwd(proj, O_XBC, 3072, ssw8, xbc_dact, dxbc, dproj, ts)
    dproj = dproj.at[:, O_IF:O_IF + SMALL_W].set(dif).at[:, O_DT:O_DT + SMALL_W].set(ddt)
    gw_in_p, gb_in_p = _matmul_tn(u.T, dproj, tm, 512, with_colsum=True, a_is_transposed=True)
    g_wpm = _matmul_tn(ym, dpm, tm, 512)
    g_wps = _matmul_tn(ys, dps, tm, 512)
    g_wo = _matmul_tn(merged, dmo, tm, 512)
    token, in_flight = (None, None) if start_exchange is None else start_exchange(gw_in_p, g_wpm, g_wps, g_wo)
    du = _matmul_nt(dproj, w_in_p, tm, 512, after=token)
    grad_x, acc_n = _prenorm_bwd(du, xs, dxres, norm_w, scale, ts)
    a_coef = -jnp.exp(ssm_a_log[0])
    small = dict(
        mod=jnp.concatenate([acc_n[2], acc_n[1], acc_m[2]]), norm_w=acc_n[0], b_in=_unpad_cols(gb_in_p[0]),
        ml_conv_w=acc_cq[0:CONV_K], ml_conv_b=acc_cq[CONV_K], ml_norm_w=acc_p[0], ssm_conv_w=acc_cx[0:CONV_K],
        ssm_conv_b=acc_cx[CONV_K], ssm_a_log=acca[0, :SSM_HEADS] * a_coef,
        ssm_d=accd[0].reshape(SSM_HEADS, SSM_HEADDIM).sum(axis=1), ssm_norm_w=acc_p[1], final_w=acc_m[1], loss=acc_m[0, 0:1])
    return grad_x, small, gw_in_p, g_wpm, g_wps, g_wo, in_flight


WEIGHTS = ("norm_w", "ada_w", "ada_b", "w_in", "b_in", "ml_conv_w", "ml_conv_b", "ml_norm_w", "ssm_conv_w", "ssm_conv_b",
           "ssm_a_log", "ssm_d", "ssm_norm_w", "w_proj_m", "w_proj_s", "w_out", "final_w")
LARGE = ("ada_w", "w_in", "w_proj_m", "w_proj_s", "w_out")
SMALL_SUMS = (("mod", (3 * D_MODEL,)), ("norm_w", (D_MODEL,)), ("b_in", (IN_WIDTH,)), ("ml_conv_w", (CONV_K, 2048)),
              ("ml_conv_b", (2048,)), ("ml_norm_w", (2048,)), ("ssm_conv_w", (CONV_K, 3072)), ("ssm_conv_b", (3072,)),
              ("ssm_a_log", (SSM_HEADS,)), ("ssm_d", (SSM_HEADS,)), ("ssm_norm_w", (2048,)), ("final_w", (D_MODEL,)),
              ("loss", (1,)))


def kernel(x, c, norm_w, ada_w, ada_b, w_in, b_in, ml_conv_w, ml_conv_b, ml_norm_w, ssm_conv_w, ssm_conv_b, ssm_a_log, ssm_d, ssm_norm_w, w_proj_m, w_proj_s, w_out, final_w, loss_target, m_norm_w, m_ada_w, m_ada_b, m_w_in, m_b_in, m_ml_conv_w, m_ml_conv_b, m_ml_norm_w, m_ssm_conv_w, m_ssm_conv_b, m_ssm_a_log, m_ssm_d, m_ssm_norm_w, m_w_proj_m, m_w_proj_s, m_w_out, m_final_w, v_norm_w, v_ada_w, v_ada_b, v_w_in, v_b_in, v_ml_conv_w, v_ml_conv_b, v_ml_norm_w, v_ssm_conv_w, v_ssm_conv_b, v_ssm_a_log, v_ssm_d, v_ssm_norm_w, v_w_proj_m, v_w_proj_s, v_w_out, v_final_w):
    w = dict(norm_w=norm_w, ada_w=ada_w, ada_b=ada_b, w_in=w_in, b_in=b_in, ml_conv_w=ml_conv_w, ml_conv_b=ml_conv_b,
             ml_norm_w=ml_norm_w, ssm_conv_w=ssm_conv_w, ssm_conv_b=ssm_conv_b, ssm_a_log=ssm_a_log, ssm_d=ssm_d,
             ssm_norm_w=ssm_norm_w, w_proj_m=w_proj_m, w_proj_s=w_proj_s, w_out=w_out, final_w=final_w)
    m = dict(zip(WEIGHTS, (m_norm_w, m_ada_w, m_ada_b, m_w_in, m_b_in, m_ml_conv_w, m_ml_conv_b, m_ml_norm_w, m_ssm_conv_w,
                           m_ssm_conv_b, m_ssm_a_log, m_ssm_d, m_ssm_norm_w, m_w_proj_m, m_w_proj_s, m_w_out, m_final_w)))
    v = dict(zip(WEIGHTS, (v_norm_w, v_ada_w, v_ada_b, v_w_in, v_b_in, v_ml_conv_w, v_ml_conv_b, v_ml_norm_w, v_ssm_conv_w,
                           v_ssm_conv_b, v_ssm_a_log, v_ssm_d, v_ssm_norm_w, v_w_proj_m, v_w_proj_s, v_w_out, v_final_w)))
    pos = _position()
    chip = 2 * pos[0] + pos[1]
    dev = 2 * chip + pos[2]
    mlw_cols, ssw_cols, ada_cols = ml_conv_w.shape[2], ssm_conv_w.shape[2], ada_w.shape[2]

    g0 = _allgather8(_pack([c, ml_conv_w, ssm_conv_w]))
    c_all, mlw_all, ssw_all = _unpack_rows(g0, [(D_MODEL,), (CONV_K, mlw_cols), (CONV_K, ssw_cols)])
    ml_conv_full = mlw_all[0::2].transpose(1, 0, 2).reshape(CONV_K, N_CHIPS * mlw_cols)
    ssm_conv_full = ssw_all[0::2].transpose(1, 0, 2).reshape(CONV_K, N_CHIPS * ssw_cols)

    ada_b_mine = lax.dynamic_slice_in_dim(ada_b, chip * ada_cols, ada_cols, axis=1)
    g1 = _allgather8(_ada_fwd(c_all, ada_w[0], ada_b_mine))
    mod = lax.dynamic_index_in_dim(g1[0::2], dev, axis=1, keepdims=False).reshape(1, 3 * D_MODEL)
    shift, scale, gate = mod[:, :D_MODEL], mod[:, D_MODEL:2 * D_MODEL], mod[:, 2 * D_MODEL:]

    def whole(lands, owns):
        return [lax.dynamic_update_index_in_dim(got, own, chip, 0).reshape(N_CHIPS, -1, own.shape[-1])
                for got, own in zip(_pair_forward(lands), owns)]

    mine = [_bf(a[0]).reshape(2, a.shape[1] // 2, a.shape[2]) for a in (w_in, w_proj_m, w_proj_s, w_out)]
    landing = lambda own: lax.empty((N_CHIPS,) + own.shape, own.dtype)
    w_send, w_recv, w_src, w_land, w_token = _split_start("w_in_gather_start", _gather_copies, _gather_pieces(mine[:1]),
                                                          mine[:1], [landing(mine[0])])
    u = _prenorm_fwd(x[0], norm_w, scale + w_token[0:1, 0:1], shift, 512)
    w_src, w_land = _split_wait("w_in_gather_wait", _gather_copies, w_send, w_recv, w_src, w_land, u)
    behind = (w_src[0][0, 0:1, 0:1] * 0).astype(BF16)
    later = [a + behind for a in mine[1:]]
    p_send, p_recv, p_src, p_land, p_token = _split_start("merge_gather_start", _gather_copies, _gather_pieces(later),
                                                          later, [landing(a) for a in later])
    w_in_p = _shards_to_padded(whole(w_land, w_src)[0])
    b_in_p = _pad_cols(b_in) + p_token[0:1, 0:1]

    def late_weights(after):
        srcs, lands = _split_wait("merge_gather_wait", _gather_copies, p_send, p_recv, p_src, p_land, after)
        return [a.reshape(-1, D_MODEL) for a in whole(lands, srcs)]

    def start_exchange(g_w_in, g_wpm, g_wps, g_wo):
        split = lambda g, rows: _bf(g).reshape(N_CHIPS, 2, rows // (2 * N_CHIPS), g.shape[-1])
        slabs = [split(_padded_to_shards(_bf(g_w_in)), N_CHIPS * D_MODEL),
                 split(g_wpm, g_wpm.shape[0]), split(g_wps, g_wps.shape[0]), split(g_wo, g_wo.shape[0])]
        pair_sums = []
        for slab, rec in zip(slabs, _pair_send(slabs)):
            kept = lax.dynamic_index_in_dim(slab, pos[2], 1, keepdims=False)
            rows = kept.shape[0] * kept.shape[1]
            both = _sum_parts(kept.reshape(rows, -1), rec.reshape(1, rows, -1), 32, BF16)
            pair_sums.append(both.reshape(kept.shape))
        send_sems, recv_sems, sums, lands, token = _chip_scatter_start(pair_sums)
        return token, (send_sems, recv_sems, sums, lands)

    grad_x, small, _, _, _, _, in_flight = _local_step(
        x[0], loss_target[0], scale, shift, gate, norm_w, w_in_p, b_in_p, ml_conv_full, ml_conv_b, ml_norm_w,
        ssm_conv_full, ssm_conv_b, ssm_a_log, ssm_d, ssm_norm_w, None, None, None, final_w[None], start_exchange,
        late_weights, u)

    g2 = _allgather8(_pack([small[name] for name, _ in SMALL_SUMS]))
    total = dict(zip([name for name, _ in SMALL_SUMS], _unpack(_sum_parts(None, g2, g2.shape[1]), [s for _, s in SMALL_SUMS])))
    dmod_all = g2[:, :3 * D_MODEL // LANE].reshape(N_DEV, 3 * D_MODEL)
    grads = dict(total)
    grads["ada_b"] = total["mod"]
    grads["ml_conv_w"] = lax.dynamic_slice_in_dim(total["ml_conv_w"], chip * mlw_cols, mlw_cols, axis=1)
    grads["ssm_conv_w"] = lax.dynamic_slice_in_dim(total["ssm_conv_w"], chip * ssw_cols, ssw_cols, axis=1)
    grads["ada_w"] = _ada_bwd(c_all, lax.dynamic_slice_in_dim(dmod_all, chip * ada_cols, ada_cols, axis=1))

    halves = []
    for both, rec in zip(*_chip_scatter_wait(*in_flight, grad_x)):
        halves.append(_sum_parts(lax.dynamic_index_in_dim(both, chip, 0, keepdims=False), rec, 32))
    for name, half, other in zip(("w_in", "w_proj_m", "w_proj_s", "w_out"), halves, _pair_exchange(halves)):
        grads[name] = lax.cond(pos[2] == 0, lambda mine, theirs: jnp.concatenate([mine, theirs]),
                               lambda mine, theirs: jnp.concatenate([theirs, mine]), half, other)

    delta, new_m, new_v = {}, {}, {}
    for name in LARGE:
        if w[name].shape[-1] % LANE:
            flat = lambda a: a.reshape(a.shape[-2:]).T
            back = lambda a: a.T.reshape(w[name].shape)
            g_flat = flat(grads[name])
            delta[name], new_m[name], new_v[name] = (back(a) for a in _adamw(flat(w[name]), g_flat, flat(m[name]), flat(v[name]), LANE))
            grads[name] = back(g_flat)
        else:
            delta[name], new_m[name], new_v[name] = _adamw(w[name], grads[name], m[name], v[name], 64)
    rest = [name for name in WEIGHTS if name not in LARGE]
    packed = [_pack([t[name] for name in rest]) for t in (w, grads, m, v)]
    for out, buf in zip((delta, new_m, new_v), _adamw(*packed, packed[0].shape[0])):
        out.update(zip(rest, _unpack(buf, [w[name].shape for name in rest])))
    loss = total["loss"][0]
    return (loss, grad_x[None], *[grads[name].reshape(w[name].shape) for name in WEIGHTS], *[delta[name] for name in WEIGHTS],
            *[new_m[name] for name in WEIGHTS], *[new_v[name] for name in WEIGHTS])
```

```python
import functools
import math

import jax
import jax.numpy as jnp
from jax import lax
from jax.experimental import pallas as pl
from jax.experimental.pallas import tpu as pltpu

F32 = jnp.float32
BF16 = jnp.bfloat16
HI = lax.Precision.HIGHEST
MESH = pl.DeviceIdType.MESH

D_MODEL = 1024
EPS = 1e-6
CONV_K = 4
ML_HEADS = 8
ML_DQK = 128
ML_DV = 256
SSM_HEADS = 32
SSM_HEADDIM = 64
SSM_GROUPS = 4
SSM_STATE = 128
IN_WIDTH = 15408
N_CHIPS = 4
N_DEV = 8
ADAM_LR, ADAM_B1, ADAM_B2, ADAM_EPS, ADAM_WD, ADAM_STEP = 0.001, 0.9, 0.999, 1e-08, 0.01, 10

O_O, O_ZM, O_ZS, O_MG, O_QK, O_V, O_XBC, O_IF, O_DT = 0, 2048, 4096, 6144, 8192, 10240, 12288, 15360, 15616
SMALL_W = 256
NP = 15872
LANE = 128
CHUNK = 128
NEG = -1e30
VMEM_LIMIT = 48 * 1024 * 1024
MERGE_VMEM = 60 * 1024 * 1024


def _cparams(*sem, vmem=VMEM_LIMIT):
    return pltpu.CompilerParams(dimension_semantics=sem, vmem_limit_bytes=vmem)


def _pad_cols(w):
    z = lambda n: jnp.zeros(w.shape[:-1] + (n,), w.dtype)
    return jnp.concatenate([w[..., 4096:8192], w[..., 11280:13328], w[..., 13360:15408], w[..., :4096], w[..., 8208:11280],
                            w[..., 8192:8208], z(SMALL_W - 16), w[..., 13328:13360], z(SMALL_W - 32)], axis=-1)


def _unpad_cols(g):
    return jnp.concatenate([g[..., O_QK:O_QK + 4096], g[..., O_O:O_O + 4096], g[..., O_IF:O_IF + 16],
                            g[..., O_XBC:O_XBC + 3072], g[..., O_ZS:O_ZS + 2048], g[..., O_DT:O_DT + 32],
                            g[..., O_MG:O_MG + 2048]], axis=-1)


PADDED_SEGMENTS = ((4096, 8192, 0), (11280, 13328, 0), (13360, 15408, 0), (0, 4096, 0), (8208, 11280, 0),
                   (8192, 8208, SMALL_W - 16), (13328, 13360, SMALL_W - 32))
SHARD_W = IN_WIDTH // N_CHIPS


def _shards_to_padded(shards):
    parts = []
    for first, last, pad in PADDED_SEGMENTS:
        for j in range(N_CHIPS):
            lo, hi = max(first, j * SHARD_W), min(last, (j + 1) * SHARD_W)
            if lo < hi:
                parts.append(shards[j][:, lo - j * SHARD_W:hi - j * SHARD_W])
        if pad:
            parts.append(jnp.zeros((shards.shape[1], pad), shards.dtype))
    return jnp.concatenate(parts, axis=1)


def _padded_to_shards(g):
    offsets, off = {}, 0
    for first, last, pad in PADDED_SEGMENTS:
        offsets[first] = off
        off += last - first + pad
    shards = []
    for j in range(N_CHIPS):
        parts = []
        for first, last, _ in sorted(PADDED_SEGMENTS):
            lo, hi = max(first, j * SHARD_W), min(last, (j + 1) * SHARD_W)
            if lo < hi:
                parts.append(g[:, offsets[first] + lo - first:offsets[first] + hi - first])
        shards.append(jnp.concatenate(parts, axis=1))
    return jnp.stack(shards)


def _sigmoid(x):
    return 0.5 * jnp.tanh(0.5 * x) + 0.5


def _silu(x):
    return x * _sigmoid(x)


def _dsilu(x):
    s = _sigmoid(x)
    return s + x * s * (1.0 - s)


def _softplus(x):
    return jnp.maximum(x, 0.0) + jnp.log(1.0 + jnp.exp(-jnp.abs(x)))


def _logsigmoid(x):
    return jnp.minimum(x, 0.0) - jnp.log(1.0 + jnp.exp(-jnp.abs(x)))


def _dot(a, b, dims, precision=None):
    return lax.dot_general(a, b, (dims, ((), ())), preferred_element_type=F32, precision=precision)


def _nn(a, b, precision=None):
    return _dot(a, b, ((1,), (0,)), precision)


def _nt(a, b, precision=None):
    return _dot(a, b, ((1,), (1,)), precision)


def _tn(a, b, precision=None):
    return _dot(a, b, ((0,), (0,)), precision)


def _bf(x):
    return x.astype(BF16)


def _split(x, terms):
    parts = []
    for _ in range(terms):
        part = _bf(x)
        parts.append(part)
        x = x - part.astype(F32)
    return parts


def _pick_right(x, pick, terms):
    pick = _bf(pick)
    out = None
    for part in _split(x, terms):
        out = _nn(part, pick) if out is None else out + _nn(part, pick)
    return out


def _pick_left(pick, x, terms):
    pick = _bf(pick)
    out = None
    for part in _split(x, terms):
        out = _nn(pick, part) if out is None else out + _nn(pick, part)
    return out


def _lane_col(x, lane):
    idx = lax.broadcasted_iota(jnp.int32, x.shape, 1)
    return jnp.sum(jnp.where(idx == lane, x, 0.0), axis=1, keepdims=True)


def _tri(n, upper):
    r = lax.broadcasted_iota(jnp.int32, (n, n), 0)
    c = lax.broadcasted_iota(jnp.int32, (n, n), 1)
    return jnp.where((r <= c) if upper else (r >= c), 1.0, 0.0).astype(F32)


def _eye(n):
    return jnp.where(lax.broadcasted_iota(jnp.int32, (n, n), 0) == lax.broadcasted_iota(jnp.int32, (n, n), 1), 1.0, 0.0)


def _sum_all(x):
    return jnp.sum(jnp.sum(x, axis=1, keepdims=True), axis=0, keepdims=True)


def _crossing(p):
    L = p.shape[0]
    below = _nn(_bf(_tri(L, True)), _bf(p))
    strict = lax.broadcasted_iota(jnp.int32, (L, L), 0) > lax.broadcasted_iota(jnp.int32, (L, L), 1)
    return [jnp.sum(jnp.where(strict, below[:, b * L:(b + 1) * L], 0.0), axis=1, keepdims=True)
            for b in range(p.shape[1] // L)]


def _matmul_bias(a, w, bias, tm, tn, col0, ncols, dtype):
    m, k = a.shape
    j0 = col0 // tn

    def body(a_ref, w_ref, b_ref, o_ref):
        o_ref[...] = (_nn(a_ref[...], w_ref[...]) + b_ref[...]).astype(dtype)

    return pl.pallas_call(
        body, name="matmul_bias", grid=(m // tm, ncols // tn),
        in_specs=[pl.BlockSpec((tm, k), lambda i, j: (i, 0)), pl.BlockSpec((k, tn), lambda i, j: (0, j0 + j)),
                  pl.BlockSpec((1, tn), lambda i, j: (0, j0 + j))],
        out_specs=pl.BlockSpec((tm, tn), lambda i, j: (i, j)),
        out_shape=jax.ShapeDtypeStruct((m, ncols), dtype),
        compiler_params=_cparams("parallel", "arbitrary"))(a, w, bias)


def _matmul_nt(a, w, tm, tk, after=None):
    m, n = a.shape
    k = w.shape[0]

    def body(a_ref, w_ref, *rest):
        o_ref = rest[-1]

        @pl.when(pl.program_id(1) == 0)
        def _():
            o_ref[...] = jnp.zeros_like(o_ref)
        o_ref[...] += _nt(a_ref[...], w_ref[...])

    extra = [] if after is None else [after]
    return pl.pallas_call(
        body, name="matmul_nt", grid=(m // tm, n // tk),
        in_specs=[pl.BlockSpec((tm, tk), lambda i, j: (i, j)), pl.BlockSpec((k, tk), lambda i, j: (0, j))]
        + [pl.BlockSpec(memory_space=pl.ANY)] * len(extra),
        out_specs=pl.BlockSpec((tm, k), lambda i, j: (i, 0)),
        out_shape=jax.ShapeDtypeStruct((m, k), F32),
        compiler_params=_cparams("parallel", "arbitrary"))(a, w, *extra)


def _matmul_tn(a, b, tm, tn, with_colsum=False, a_is_transposed=False):
    k, m = a.shape if a_is_transposed else a.shape[::-1]
    n = b.shape[1]

    def body(a_ref, b_ref, o_ref, *rest):
        first = pl.program_id(1) == 0

        @pl.when(first)
        def _():
            o_ref[...] = jnp.zeros_like(o_ref)
        o_ref[...] += _nn(a_ref[...], b_ref[...]) if a_is_transposed else _tn(a_ref[...], b_ref[...])
        if with_colsum:
            s_ref = rest[0]

            @pl.when(first)
            def _():
                s_ref[...] = jnp.zeros_like(s_ref)
            s_ref[...] += jnp.sum(b_ref[...].astype(F32), axis=0, keepdims=True)

    out_specs = [pl.BlockSpec((k, tn), lambda j, i: (0, j))]
    out_shape = [jax.ShapeDtypeStruct((k, n), F32)]
    if with_colsum:
        out_specs.append(pl.BlockSpec((1, tn), lambda j, i: (0, j)))
        out_shape.append(jax.ShapeDtypeStruct((1, n), F32))
    out = pl.pallas_call(
        body, name="matmul_tn", grid=(n // tn, m // tm),
        in_specs=[pl.BlockSpec((k, tm), lambda j, i: (0, i)) if a_is_transposed else pl.BlockSpec((tm, k), lambda j, i: (i, 0)),
                  pl.BlockSpec((tm, tn), lambda j, i: (i, j))],
        out_specs=out_specs, out_shape=out_shape,
        compiler_params=_cparams("parallel", "arbitrary"))(a, b)
    return out if with_colsum else out[0]


def _ada_fwd(c_all, ada_w, ada_b):
    def body(c_ref, w_ref, b_ref, o_ref):
        o_ref[...] = _nn(_bf(_silu(c_ref[...])), _bf(w_ref[...])) + b_ref[...]

    return pl.pallas_call(body, name="ada_fwd", out_shape=jax.ShapeDtypeStruct((c_all.shape[0], ada_w.shape[1]), F32),
                          compiler_params=_cparams())(c_all, ada_w, ada_b)


def _ada_bwd(c_all, dmod):
    def body(c_ref, d_ref, o_ref):
        o_ref[...] = _tn(_bf(_silu(c_ref[...])), _bf(d_ref[...]))

    return pl.pallas_call(body, name="ada_bwd", out_shape=jax.ShapeDtypeStruct((c_all.shape[1], dmod.shape[1]), F32),
                          compiler_params=_cparams())(c_all, dmod)


def _prenorm_fwd(x, norm_w, scale, shift, ts):
    s, d = x.shape

    def body(x_ref, nw_ref, sc_ref, sh_ref, u_ref):
        xv = x_ref[...]
        r = lax.rsqrt(jnp.mean(xv * xv, axis=1, keepdims=True) + EPS)
        u_ref[...] = _bf(xv * r * nw_ref[...] * (1.0 + sc_ref[...]) + sh_ref[...])

    row = pl.BlockSpec((1, d), lambda i: (0, 0))
    return pl.pallas_call(
        body, name="prenorm_fwd", grid=(s // ts,),
        in_specs=[pl.BlockSpec((ts, d), lambda i: (i, 0)), row, row, row],
        out_specs=pl.BlockSpec((ts, d), lambda i: (i, 0)), out_shape=jax.ShapeDtypeStruct((s, d), BF16),
        compiler_params=_cparams("parallel"))(x, norm_w, scale, shift)


def _prenorm_bwd(du, x, dxres, norm_w, scale, ts):
    s, d = x.shape

    def body(du_ref, x_ref, dr_ref, nw_ref, sc_ref, gx_ref, acc_ref):
        @pl.when(pl.program_id(0) == 0)
        def _():
            acc_ref[...] = jnp.zeros_like(acc_ref)
        xv, duv = x_ref[...], du_ref[...]
        r = lax.rsqrt(jnp.mean(xv * xv, axis=1, keepdims=True) + EPS)
        xn = xv * r
        nw, sc1 = nw_ref[...], 1.0 + sc_ref[...]
        dxn = duv * (nw * sc1)
        gx_ref[...] = r * (dxn - xn * jnp.mean(dxn * xn, axis=1, keepdims=True)) + dr_ref[...]
        t = duv * xn
        acc_ref[0:1, :] += jnp.sum(t, axis=0, keepdims=True) * sc1
        acc_ref[1:2, :] += jnp.sum(t, axis=0, keepdims=True) * nw
        acc_ref[2:3, :] += jnp.sum(duv, axis=0, keepdims=True)

    tile = pl.BlockSpec((ts, d), lambda i: (i, 0))
    row = pl.BlockSpec((1, d), lambda i: (0, 0))
    return pl.pallas_call(
        body, name="prenorm_bwd", grid=(s // ts,),
        in_specs=[tile, tile, tile, row, row],
        out_specs=[tile, pl.BlockSpec((8, d), lambda i: (0, 0))],
        out_shape=[jax.ShapeDtypeStruct((s, d), F32), jax.ShapeDtypeStruct((8, d), F32)],
        compiler_params=_cparams("arbitrary"))(du, x, dxres, norm_w, scale)


CONV_CB = 512


def _conv_taps(buf_ref, ts):
    return [buf_ref[pl.ds(8 - (CONV_K - 1) + j, ts), :] for j in range(CONV_K)]


def _conv_fwd(proj, col0, width, w8, b, ts):
    s = proj.shape[0]
    cb = CONV_CB
    nt = s // ts

    def body(x_ref, w_ref, b_ref, o_ref, ds_ref, buf_ref):
        @pl.when(pl.program_id(1) == 0)
        def _():
            buf_ref[0:8, :] = jnp.zeros((8, cb), F32)
        buf_ref[pl.ds(8, ts), :] = x_ref[...].astype(F32)
        acc = b_ref[...] + jnp.zeros((ts, cb), F32)
        for j, tap in enumerate(_conv_taps(buf_ref, ts)):
            acc = acc + tap * w_ref[j:j + 1, :]
        sg = _sigmoid(acc)
        o_ref[...] = acc * sg
        ds_ref[...] = _bf(sg + acc * sg * (1.0 - sg))
        buf_ref[0:8, :] = buf_ref[pl.ds(ts, 8), :]

    c0 = col0 // cb
    tile = pl.BlockSpec((ts, cb), lambda c, i: (i, c))
    return pl.pallas_call(
        body, name="conv_fwd", grid=(width // cb, nt),
        in_specs=[pl.BlockSpec((ts, cb), lambda c, i: (i, c0 + c)), pl.BlockSpec((8, cb), lambda c, i: (0, c)),
                  pl.BlockSpec((1, cb), lambda c, i: (0, c))],
        out_specs=[tile, tile],
        out_shape=[jax.ShapeDtypeStruct((s, width), F32), jax.ShapeDtypeStruct((s, width), BF16)],
        scratch_shapes=[pltpu.VMEM((ts + 8, cb), F32)],
        compiler_params=_cparams("parallel", "arbitrary"))(proj, w8, b)


def _conv_bwd(proj, col0, width, w8, dact, dpost, dproj, ts):
    s = proj.shape[0]
    cb = CONV_CB
    nt = s // ts
    c0 = col0 // cb

    def body(x_ref, da_ref, dp_ref, w_ref, _, dx_ref, acc_ref, dbuf_ref):
        @pl.when(pl.program_id(1) == 0)
        def _():
            acc_ref[...] = jnp.zeros_like(acc_ref)
            dbuf_ref[pl.ds(ts, 8), :] = jnp.zeros((8, cb), F32)
        dconv = dp_ref[...].astype(F32) * da_ref[...].astype(F32)
        acc_ref[CONV_K:CONV_K + 1, :] += jnp.sum(dconv, axis=0, keepdims=True)
        dbuf_ref[pl.ds(0, ts), :] = dconv
        xv = x_ref[...].astype(F32)
        dx = jnp.zeros((ts, cb), F32)
        for j in range(CONV_K):
            shifted = dbuf_ref[pl.ds(CONV_K - 1 - j, ts), :]
            dx = dx + shifted * w_ref[j:j + 1, :]
            acc_ref[j:j + 1, :] += jnp.sum(xv * shifted, axis=0, keepdims=True)
        dx_ref[...] = _bf(dx)
        dbuf_ref[pl.ds(ts, 8), :] = dconv[0:8, :]

    tile = pl.BlockSpec((ts, cb), lambda c, i: (nt - 1 - i, c))
    wide = pl.BlockSpec((ts, cb), lambda c, i: (nt - 1 - i, c0 + c))
    return pl.pallas_call(
        body, name="conv_bwd", grid=(width // cb, nt),
        in_specs=[wide, tile, tile, pl.BlockSpec((8, cb), lambda c, i: (0, c)), pl.BlockSpec(memory_space=pl.ANY)],
        out_specs=[wide, pl.BlockSpec((8, cb), lambda c, i: (0, c))],
        out_shape=[jax.ShapeDtypeStruct(dproj.shape, dproj.dtype), jax.ShapeDtypeStruct((8, width), F32)],
        input_output_aliases={4: 0},
        scratch_shapes=[pltpu.VMEM((ts + 8, cb), F32)],
        compiler_params=_cparams("parallel", "arbitrary"))(proj, dact, dpost, w8, dproj)


def _mlstm_gates(gif_ref, gt_ref, a_scr, at_scr):
    L = gif_ref.shape[0]
    fb = _logsigmoid(gif_ref[...])
    a_scr[...] = _pick_left(_tri(L, False), fb, 3)
    at_scr[...] = _pick_right(_logsigmoid(gt_ref[...]), _tri(L, True), 3)
    return jnp.sum(fb, axis=0, keepdims=True)


def _mlstm_head(h, qk_ref, v_ref, gif, gt_ref, a, at_scr, a_last_row, c_mat, n_row, m_prev):
    L = gif.shape[0]
    q = qk_ref[:, h * ML_DQK:(h + 1) * ML_DQK] * (ML_DQK ** -0.5)
    k = qk_ref[:, (ML_HEADS + h) * ML_DQK:(ML_HEADS + h + 1) * ML_DQK]
    v = v_ref[:, h * ML_DV:(h + 1) * ML_DV]
    i_col, a_col = _lane_col(gif, h), _lane_col(a, ML_HEADS + h)
    i_row, a_row = gt_ref[h:h + 1, :], at_scr[ML_HEADS + h:ML_HEADS + h + 1, :]
    causal = lax.broadcasted_iota(jnp.int32, (L, L), 0) >= lax.broadcasted_iota(jnp.int32, (L, L), 1)
    dmat = jnp.where(causal, a_col - a_row + i_row, NEG)
    inter = a_col + m_prev
    m_t = jnp.maximum(inter, jnp.max(dmat, axis=1, keepdims=True))
    w_intra = jnp.exp(dmat - m_t)
    w_inter = jnp.exp(inter - m_t)
    sc = _nt(_bf(q), _bf(k)) * w_intra
    den = jnp.sum(sc, axis=1, keepdims=True) + w_inter * jnp.sum(q * n_row, axis=1, keepdims=True)
    floor = jnp.exp(-m_t)
    a_last = _lane_col(a_last_row, ML_HEADS + h)
    g = a_last - a_col + i_col
    m_new = jnp.maximum(a_last + m_prev, jnp.max(g, axis=0, keepdims=True))
    wk = jnp.exp(g - m_new)
    decay = jnp.exp(a_last + m_prev - m_new)
    return dict(q=q, k=k, v=v, w_intra=w_intra, w_inter=w_inter, sc=sc, den=den, floor=floor, m_new=m_new, wk=wk,
                decay=decay)


def _state_tile(n_row, m11):
    r = lax.broadcasted_iota(jnp.int32, (8, LANE), 0)
    return jnp.where(r == 0, n_row, jnp.where(r == 1, m11, 0.0))


def _mlstm_fwd(qk, proj, gates, gt):
    s = qk.shape[0]
    L = CHUNK
    nc = s // L

    def body(qk_ref, v_ref, gif_ref, gt_ref, h_ref, cst_ref, nm_ref, c_scr, nm_scr, a_scr, at_scr):
        @pl.when(pl.program_id(0) == 0)
        def _():
            c_scr[...] = jnp.zeros_like(c_scr)
            nm_scr[...] = jnp.zeros_like(nm_scr)
        a_last_row = _mlstm_gates(gif_ref, gt_ref, a_scr, at_scr)
        gif, a = gif_ref[...], a_scr[...]
        for h in range(ML_HEADS):
            c_mat, n_row = c_scr[h], nm_scr[h, 0:1, :]
            m_prev = jnp.max(nm_scr[h, 1:2, :], axis=1, keepdims=True)
            cst_ref[0, h] = c_mat
            nm_ref[0, h] = nm_scr[h]
            t = _mlstm_head(h, qk_ref, v_ref, gif, gt_ref, a, at_scr, a_last_row, c_mat, n_row, m_prev)
            num = _nn(_bf(t["sc"]), _bf(t["v"])) + t["w_inter"] * _nn(_bf(t["q"]), _bf(c_mat))
            h_ref[:, h * ML_DV:(h + 1) * ML_DV] = _bf(num * (1.0 / jnp.maximum(jnp.abs(t["den"]), t["floor"])))
            kw = t["k"] * t["wk"]
            c_scr[h] = t["decay"] * c_mat + _tn(_bf(kw), _bf(t["v"]))
            nm_scr[h] = _state_tile(t["decay"] * n_row + jnp.sum(kw, axis=0, keepdims=True), t["m_new"])

    return pl.pallas_call(
        body, name="mlstm_fwd", grid=(nc,),
        in_specs=[pl.BlockSpec((L, 2048), lambda c: (c, 0)), pl.BlockSpec((L, 2048), lambda c: (c, O_V // 2048)),
                  pl.BlockSpec((L, LANE), lambda c: (c, 0)), pl.BlockSpec((LANE, L), lambda c: (0, c))],
        out_specs=[pl.BlockSpec((L, 2048), lambda c: (c, 0)),
                   pl.BlockSpec((1, ML_HEADS, ML_DQK, ML_DV), lambda c: (c, 0, 0, 0)),
                   pl.BlockSpec((1, ML_HEADS, 8, LANE), lambda c: (c, 0, 0, 0))],
        out_shape=[jax.ShapeDtypeStruct((s, 2048), BF16), jax.ShapeDtypeStruct((nc, ML_HEADS, ML_DQK, ML_DV), F32),
                   jax.ShapeDtypeStruct((nc, ML_HEADS, 8, LANE), F32)],
        scratch_shapes=[pltpu.VMEM((ML_HEADS, ML_DQK, ML_DV), F32), pltpu.VMEM((ML_HEADS, 8, LANE), F32),
                        pltpu.VMEM((L, LANE), F32), pltpu.VMEM((LANE, L), F32)],
        compiler_params=_cparams("arbitrary"))(qk, proj, gates, gt)


def _mlstm_bwd(qk, proj, gates, gt, hout, dh, cst, nm, dproj):
    s = qk.shape[0]
    L = CHUNK
    nc = s // L

    def body(qk_ref, v_ref, gif_ref, gt_ref, h_ref, dh_ref, cst_ref, nm_ref, _, dqk_ref, dv_ref, dif_ref,
             dc_scr, dn_scr, a_scr, at_scr):
        @pl.when(pl.program_id(0) == 0)
        def _():
            dc_scr[...] = jnp.zeros_like(dc_scr)
            dn_scr[...] = jnp.zeros_like(dn_scr)
        a_last_row = _mlstm_gates(gif_ref, gt_ref, a_scr, at_scr)
        gif, a = gif_ref[...], a_scr[...]
        lane = lax.broadcasted_iota(jnp.int32, (L, LANE), 1)
        last = lax.broadcasted_iota(jnp.int32, (L, 1), 0) == L - 1
        di_tile = jnp.zeros((L, LANE), F32)
        cross = [jnp.zeros((L, LANE), F32)] * 3
        dlogw = []
        for h in range(ML_HEADS):
            c_mat, n_row = cst_ref[0, h], nm_ref[0, h, 0:1, :]
            m_prev = jnp.max(nm_ref[0, h, 1:2, :], axis=1, keepdims=True)
            t = _mlstm_head(h, qk_ref, v_ref, gif, gt_ref, a, at_scr, a_last_row, c_mat, n_row, m_prev)
            q, k, v, den = t["q"], t["k"], t["v"], t["den"]
            dhh = dh_ref[:, h * ML_DV:(h + 1) * ML_DV].astype(F32)
            hh = h_ref[:, h * ML_DV:(h + 1) * ML_DV].astype(F32)
            dnorm = jnp.maximum(jnp.abs(den), t["floor"])
            inv = 1.0 / dnorm
            dnum = dhh * inv
            d_dn = -jnp.sum(dhh * hh, axis=1, keepdims=True) * inv
            dden = jnp.where(jnp.abs(den) >= t["floor"], jnp.where(den >= 0.0, d_dn, -d_dn), 0.0)
            dsc = _nt(_bf(dnum), _bf(v)) + dden
            ds = dsc * t["w_intra"]
            dq_inter = t["w_inter"] * (_nt(_bf(dnum), _bf(c_mat)) + dden * n_row)
            dq = _nn(_bf(ds), _bf(k)) + dq_inter
            dc, dn_row = dc_scr[h], dn_scr[h, 0:1, :]
            dk_state = t["wk"] * (_nt(_bf(v), _bf(dc)) + dn_row)
            dk = _tn(_bf(ds), _bf(q)) + dk_state
            dv = _tn(_bf(t["sc"]), _bf(dnum)) + t["wk"] * _nn(_bf(k), _bf(dc))
            qi = q * t["w_inter"]
            dc_scr[h] = t["decay"] * dc + _tn(_bf(qi), _bf(dnum))
            dn_scr[h] = jnp.broadcast_to(t["decay"] * dn_row + jnp.sum(qi * dden, axis=0, keepdims=True), (8, LANE))
            dqk_ref[:, h * ML_DQK:(h + 1) * ML_DQK] = _bf(dq * (ML_DQK ** -0.5))
            dqk_ref[:, (ML_HEADS + h) * ML_DQK:(ML_HEADS + h + 1) * ML_DQK] = _bf(dk)
            dv_ref[:, h * ML_DV:(h + 1) * ML_DV] = _bf(dv)
            di_tile = di_tile + jnp.where(lane == h, jnp.sum(k * dk, axis=1, keepdims=True), 0.0)
            carried = t["decay"] * (_sum_all(dc * c_mat) + jnp.sum(dn_row * n_row, axis=1, keepdims=True))
            dlogw.append(dsc * t["sc"])
            parts = (jnp.sum(q * dq_inter, axis=1, keepdims=True) + jnp.where(last, carried, 0.0),
                     jnp.sum(k * dk_state, axis=1, keepdims=True))
            cross[1:] = [c + jnp.where(lane == ML_HEADS + h, p, 0.0) for c, p in zip(cross[1:], parts)]
        for h, col in enumerate(_crossing(jnp.concatenate(dlogw, axis=1))):
            cross[0] = cross[0] + jnp.where(lane == ML_HEADS + h, col, 0.0)
        dfb = cross[0] + _pick_left(_tri(L, True), cross[1], 2) + _pick_left(_tri(L, False) - _eye(L), cross[2], 2)
        dif_ref[:, 0:LANE] = _bf(di_tile + dfb * _sigmoid(-gif))
        dif_ref[:, LANE:SMALL_W] = jnp.zeros((L, SMALL_W - LANE), BF16)

    rev = lambda c: nc - 1 - c
    return pl.pallas_call(
        body, name="mlstm_bwd", grid=(nc,),
        in_specs=[pl.BlockSpec((L, 2048), lambda c: (rev(c), 0)), pl.BlockSpec((L, 2048), lambda c: (rev(c), O_V // 2048)),
                  pl.BlockSpec((L, LANE), lambda c: (rev(c), 0)), pl.BlockSpec((LANE, L), lambda c: (0, rev(c))),
                  pl.BlockSpec((L, 2048), lambda c: (rev(c), 0)), pl.BlockSpec((L, 2048), lambda c: (rev(c), 0)),
                  pl.BlockSpec((1, ML_HEADS, ML_DQK, ML_DV), lambda c: (rev(c), 0, 0, 0)),
                  pl.BlockSpec((1, ML_HEADS, 8, LANE), lambda c: (rev(c), 0, 0, 0)), pl.BlockSpec(memory_space=pl.ANY)],
        out_specs=[pl.BlockSpec((L, 2048), lambda c: (rev(c), 0)), pl.BlockSpec((L, 2048), lambda c: (rev(c), O_V // 2048)),
                   pl.BlockSpec((L, SMALL_W), lambda c: (rev(c), 0))],
        out_shape=[jax.ShapeDtypeStruct((s, 2048), BF16), jax.ShapeDtypeStruct(dproj.shape, dproj.dtype),
                   jax.ShapeDtypeStruct((s, SMALL_W), BF16)],
        input_output_aliases={8: 1},
        scratch_shapes=[pltpu.VMEM((ML_HEADS, ML_DQK, ML_DV), F32), pltpu.VMEM((ML_HEADS, 8, LANE), F32),
                        pltpu.VMEM((L, LANE), F32), pltpu.VMEM((LANE, L), F32)],
        compiler_params=_cparams("arbitrary"))(qk, proj, gates, gt, hout, dh, cst, nm, dproj)


GROUP_W = SSM_HEADS // SSM_GROUPS * SSM_HEADDIM
O_B = SSM_HEADS * SSM_HEADDIM
O_C = O_B + SSM_GROUPS * SSM_STATE


def _head_expand():
    r = jnp.arange(LANE)[:, None]
    c = jnp.arange(SSM_HEADS * SSM_HEADDIM)[None, :] // SSM_HEADDIM
    return (r == c).astype(F32)


def _ssd_gates(dt_ref, dtt_ref, alog_row_ref, alog_col_ref, at_scr):
    L = dt_ref.shape[0]
    dt = _softplus(dt_ref[...])
    acoef = -jnp.exp(alog_row_ref[...])
    a = _pick_left(_tri(L, False), dt * acoef, 3)
    at_scr[...] = _pick_right(_softplus(dtt_ref[...]) * (-jnp.exp(alog_col_ref[...])), _tri(L, True), 3)
    return dt, acoef, a


def _ssd_group(g, xbc_ref, dt, a, e_ref, ax_scr):
    eg = e_ref[:, g * GROUP_W:(g + 1) * GROUP_W]
    ax_scr[...] = _pick_right(a, eg, 3)
    ax = ax_scr[...]
    alx = ax_scr[ax.shape[0] - 1:ax.shape[0], :]
    dtx = _pick_right(dt, eg, 2)
    xg = xbc_ref[:, g * GROUP_W:(g + 1) * GROUP_W]
    bg = xbc_ref[:, O_B + g * SSM_STATE:O_B + (g + 1) * SSM_STATE]
    cg = xbc_ref[:, O_C + g * SSM_STATE:O_C + (g + 1) * SSM_STATE]
    return dict(ax=ax, alx=alx, dtx=dtx, xg=xg, bg=bg, cg=cg, xdt=xg * dtx, gmat=_nt(_bf(cg), _bf(bg)))


def _ssd_decay(hh, a, at_scr):
    L = a.shape[0]
    causal = lax.broadcasted_iota(jnp.int32, (L, L), 0) >= lax.broadcasted_iota(jnp.int32, (L, L), 1)
    return jnp.exp(jnp.where(causal, _lane_col(a, hh) - at_scr[hh:hh + 1, :], NEG))


def _ssd_fwd(xbc, gates, dtt, alog_row, alog_col, dskip_x, expand):
    s = xbc.shape[0]
    L = CHUNK
    nc = s // L
    half = SSM_HEADDIM

    def body(xbc_ref, dt_ref, dtt_ref, ar_ref, ac_ref, dk_ref, e_ref, y_ref, st_ref, st_scr, at_scr, ax_scr):
        @pl.when(pl.program_id(0) == 0)
        def _():
            st_scr[...] = jnp.zeros_like(st_scr)
        dt, _, a = _ssd_gates(dt_ref, dtt_ref, ar_ref, ac_ref, at_scr)
        lane = lax.broadcasted_iota(jnp.int32, (L, LANE), 1)
        for g in range(SSM_GROUPS):
            t = _ssd_group(g, xbc_ref, dt, a, e_ref, ax_scr)
            st = st_scr[g]
            st_ref[0, g] = st
            pairs = []
            for j in range(GROUP_W // LANE):
                xp = _bf(t["xdt"][:, j * LANE:(j + 1) * LANE])
                hh = g * (SSM_HEADS // SSM_GROUPS) + 2 * j
                both = jnp.concatenate([_bf(t["gmat"] * _ssd_decay(hh, a, at_scr)),
                                        _bf(t["gmat"] * _ssd_decay(hh + 1, a, at_scr))], axis=0)
                ys = _nn(both, xp)
                pairs.append(jnp.where(lane < half, ys[0:L], ys[L:2 * L]))
            y = jnp.concatenate(pairs, axis=1) + _nn(_bf(t["cg"]), _bf(st)) * jnp.exp(t["ax"])
            y_ref[:, g * GROUP_W:(g + 1) * GROUP_W] = _bf(y + dk_ref[:, g * GROUP_W:(g + 1) * GROUP_W] * t["xg"])
            wts = jnp.exp(t["alx"] - t["ax"])
            st_scr[g] = jnp.exp(t["alx"]) * st + _tn(_bf(t["bg"]), _bf(t["xdt"] * wts))

    row = lambda w: pl.BlockSpec((1, w), lambda c: (0, 0))
    return pl.pallas_call(
        body, name="ssd_fwd", grid=(nc,),
        in_specs=[pl.BlockSpec((L, 3072), lambda c: (c, 0)), pl.BlockSpec((L, LANE), lambda c: (c, (O_DT - O_IF) // LANE)),
                  pl.BlockSpec((LANE, L), lambda c: (0, c)), row(LANE), pl.BlockSpec((LANE, 1), lambda c: (0, 0)),
                  row(2048), pl.BlockSpec((LANE, 2048), lambda c: (0, 0))],
        out_specs=[pl.BlockSpec((L, 2048), lambda c: (c, 0)),
                   pl.BlockSpec((1, SSM_GROUPS, SSM_STATE, GROUP_W), lambda c: (c, 0, 0, 0))],
        out_shape=[jax.ShapeDtypeStruct((s, 2048), BF16),
                   jax.ShapeDtypeStruct((nc, SSM_GROUPS, SSM_STATE, GROUP_W), F32)],
        scratch_shapes=[pltpu.VMEM((SSM_GROUPS, SSM_STATE, GROUP_W), F32), pltpu.VMEM((LANE, L), F32),
                        pltpu.VMEM((L, GROUP_W), F32)],
        compiler_params=_cparams("arbitrary"))(xbc, gates, dtt, alog_row, alog_col, dskip_x, expand)


def _ssd_bwd(xbc, gates, dtt, alog_row, alog_col, dskip_x, expand, expand_t, dy, states):
    s = xbc.shape[0]
    L = CHUNK
    nc = s // L
    half = SSM_HEADDIM

    def body(xbc_ref, dt_ref, dtt_ref, ar_ref, ac_ref, dk_ref, e_ref, et_ref, dy_ref, st_ref,
             dxbc_ref, ddt_ref, accd_ref, acca_ref, dst_scr, at_scr, ax_scr):
        @pl.when(pl.program_id(0) == 0)
        def _():
            dst_scr[...] = jnp.zeros_like(dst_scr)
            accd_ref[...] = jnp.zeros_like(accd_ref)
            acca_ref[...] = jnp.zeros_like(acca_ref)
        dt, acoef, a = _ssd_gates(dt_ref, dtt_ref, ar_ref, ac_ref, at_scr)
        lane = lax.broadcasted_iota(jnp.int32, (L, LANE), 1)
        low = lane < half
        last = lax.broadcasted_iota(jnp.int32, (L, 1), 0) == L - 1
        cross = [jnp.zeros((L, LANE), F32)] * 3
        ddt_tile = jnp.zeros((L, LANE), F32)
        for g in range(SSM_GROUPS):
            t = _ssd_group(g, xbc_ref, dt, a, e_ref, ax_scr)
            xg, bg, cg, xdt, gmat = t["xg"], t["bg"], t["cg"], t["xdt"], t["gmat"]
            st, dst = st_ref[0, g], dst_scr[g]
            dyg = dy_ref[:, g * GROUP_W:(g + 1) * GROUP_W].astype(F32)
            ea, eal = jnp.exp(t["ax"]), jnp.exp(t["alx"])
            wts = jnp.exp(t["alx"] - t["ax"])
            dyi = dyg * ea
            y_inter = _nn(_bf(cg), _bf(st)) * ea
            dc = _nt(_bf(dyi), _bf(st))
            d_xdt_state = _nn(_bf(bg), _bf(dst)) * wts
            db = _nt(_bf(xdt * wts), _bf(dst))
            dst_scr[g] = eal * dst + _tn(_bf(cg), _bf(dyi))
            dg = jnp.zeros((L, L), F32)
            dx_pairs, dlogw = [], []
            for j in range(GROUP_W // LANE):
                xp = _bf(xdt[:, j * LANE:(j + 1) * LANE])
                dyp = dyg[:, j * LANE:(j + 1) * LANE]
                hh = g * (SSM_HEADS // SSM_GROUPS) + 2 * j
                decs = [_ssd_decay(hh, a, at_scr), _ssd_decay(hh + 1, a, at_scr)]
                ws = [gmat * decs[0], gmat * decs[1]]
                dxs = _tn(_bf(jnp.concatenate(ws, axis=1)), _bf(dyp))
                dws = _nt(_bf(jnp.concatenate([jnp.where(low, dyp, 0.0), jnp.where(low, 0.0, dyp)], axis=0)), xp)
                dw0, dw1 = dws[0:L], dws[L:2 * L]
                dg = dg + dw0 * decs[0] + dw1 * decs[1]
                dlogw += [dw0 * ws[0], dw1 * ws[1]]
                dx_pairs.append(jnp.where(low, dxs[0:L], dxs[L:2 * L]))
            for b, col in enumerate(_crossing(jnp.concatenate(dlogw, axis=1))):
                cross[0] = cross[0] + jnp.where(lane == g * (SSM_HEADS // SSM_GROUPS) + b, col, 0.0)
            d_xdt = d_xdt_state + jnp.concatenate(dx_pairs, axis=1)
            dc = dc + _nn(_bf(dg), _bf(bg))
            db = db + _tn(_bf(dg), _bf(cg))
            etg = et_ref[g * GROUP_W:(g + 1) * GROUP_W, :]
            carried = jnp.sum(dst * st, axis=0, keepdims=True) * eal
            cross[1] = cross[1] + _pick_right(dyg * y_inter + jnp.where(last, carried, 0.0), etg, 2)
            cross[2] = cross[2] + _pick_right(xdt * d_xdt_state, etg, 2)
            ddt_tile = ddt_tile + _pick_right(d_xdt * xg, etg, 2)
            dxbc_ref[:, g * GROUP_W:(g + 1) * GROUP_W] = _bf(d_xdt * t["dtx"] + dk_ref[:, g * GROUP_W:(g + 1) * GROUP_W] * dyg)
            dxbc_ref[:, O_B + g * SSM_STATE:O_B + (g + 1) * SSM_STATE] = _bf(db)
            dxbc_ref[:, O_C + g * SSM_STATE:O_C + (g + 1) * SSM_STATE] = _bf(dc)
            accd_ref[0:1, g * GROUP_W:(g + 1) * GROUP_W] += jnp.sum(dyg * xg, axis=0, keepdims=True)
        d_da = cross[0] + _pick_left(_tri(L, True), cross[1], 2) + _pick_left(_tri(L, False) - _eye(L), cross[2], 2)
        acca_ref[0:1, :] += jnp.sum(d_da * dt, axis=0, keepdims=True)
        ddt_ref[:, 0:LANE] = _bf((ddt_tile + d_da * acoef) * _sigmoid(dt_ref[...]))
        ddt_ref[:, LANE:SMALL_W] = jnp.zeros((L, SMALL_W - LANE), BF16)

    rev = lambda c: nc - 1 - c
    row = lambda w: pl.BlockSpec((1, w), lambda c: (0, 0))
    return pl.pallas_call(
        body, name="ssd_bwd", grid=(nc,),
        in_specs=[pl.BlockSpec((L, 3072), lambda c: (rev(c), 0)), pl.BlockSpec((L, LANE), lambda c: (rev(c), (O_DT - O_IF) // LANE)),
                  pl.BlockSpec((LANE, L), lambda c: (0, rev(c))), row(LANE), pl.BlockSpec((LANE, 1), lambda c: (0, 0)),
                  row(2048), pl.BlockSpec((LANE, 2048), lambda c: (0, 0)), pl.BlockSpec((2048, LANE), lambda c: (0, 0)),
                  pl.BlockSpec((L, 2048), lambda c: (rev(c), 0)),
                  pl.BlockSpec((1, SSM_GROUPS, SSM_STATE, GROUP_W), lambda c: (rev(c), 0, 0, 0))],
        out_specs=[pl.BlockSpec((L, 3072), lambda c: (rev(c), 0)), pl.BlockSpec((L, SMALL_W), lambda c: (rev(c), 0)),
                   pl.BlockSpec((8, 2048), lambda c: (0, 0)), pl.BlockSpec((8, LANE), lambda c: (0, 0))],
        out_shape=[jax.ShapeDtypeStruct((s, 3072), BF16), jax.ShapeDtypeStruct((s, SMALL_W), BF16),
                   jax.ShapeDtypeStruct((8, 2048), F32), jax.ShapeDtypeStruct((8, LANE), F32)],
        scratch_shapes=[pltpu.VMEM((SSM_GROUPS, SSM_STATE, GROUP_W), F32),
                        pltpu.VMEM((LANE, L), F32), pltpu.VMEM((L, GROUP_W), F32)],
        compiler_params=_cparams("arbitrary"))(xbc, gates, dtt, alog_row, alog_col, dskip_x, expand, expand_t, dy, states)


def _group_norm(v, width):
    outs, rs = [], []
    for k in range(v.shape[1] // width):
        blk = v[:, k * width:(k + 1) * width]
        r = lax.rsqrt(jnp.mean(blk * blk, axis=1, keepdims=True) + EPS)
        outs.append(blk * r)
        rs.append(jnp.broadcast_to(r, blk.shape))
    return jnp.concatenate(outs, axis=1), jnp.concatenate(rs, axis=1)


def _group_mean(v, width):
    return jnp.concatenate([jnp.broadcast_to(jnp.mean(v[:, k * width:(k + 1) * width], axis=1, keepdims=True),
                                             (v.shape[0], width)) for k in range(v.shape[1] // width)], axis=1)


def _post_fwd(hm, yssd, proj, ml_norm_w, ssm_norm_w, ts):
    s = hm.shape[0]

    def body(h_ref, ys_ref, o_ref, zm_ref, zs_ref, wm_ref, ws_ref, ym_ref, yso_ref):
        hn, _ = _group_norm(h_ref[...].astype(F32), ML_DV)
        ym_ref[...] = _bf(_sigmoid(o_ref[...].astype(F32)) * hn * wm_ref[...] * _silu(zm_ref[...].astype(F32)))
        pn, _ = _group_norm(ys_ref[...].astype(F32) * _silu(zs_ref[...].astype(F32)), GROUP_W)
        yso_ref[...] = _bf(pn * ws_ref[...])

    tile = pl.BlockSpec((ts, 2048), lambda i: (i, 0))
    col = lambda off: pl.BlockSpec((ts, 2048), lambda i: (i, off // 2048))
    row = pl.BlockSpec((1, 2048), lambda i: (0, 0))
    return pl.pallas_call(
        body, name="post_fwd", grid=(s // ts,),
        in_specs=[tile, tile, col(O_O), col(O_ZM), col(O_ZS), row, row],
        out_specs=[tile, tile],
        out_shape=[jax.ShapeDtypeStruct((s, 2048), BF16)] * 2,
        compiler_params=_cparams("parallel"))(hm, yssd, proj, proj, proj, ml_norm_w, ssm_norm_w)


def _post_bwd(dym, dys, hm, yssd, proj, ml_norm_w, ssm_norm_w, dproj, ts):
    s = hm.shape[0]

    def body(dym_ref, dys_ref, h_ref, ys_ref, o_ref, zm_ref, zs_ref, wm_ref, ws_ref, _,
             dh_ref, dyssd_ref, dp_ref, acc_ref):
        @pl.when(pl.program_id(0) == 0)
        def _():
            acc_ref[...] = jnp.zeros_like(acc_ref)
        hn, r = _group_norm(h_ref[...].astype(F32), ML_DV)
        so, zm, wm, d_ym = _sigmoid(o_ref[...].astype(F32)), zm_ref[...].astype(F32), wm_ref[...], dym_ref[...].astype(F32)
        sz = _silu(zm)
        hnw = hn * wm
        dp_ref[:, O_O:O_O + 2048] = _bf(d_ym * hnw * sz * so * (1.0 - so))
        dp_ref[:, O_ZM:O_ZM + 2048] = _bf(d_ym * so * hnw * _dsilu(zm))
        dhnw = d_ym * so * sz
        acc_ref[0:1, :] += jnp.sum(dhnw * hn, axis=0, keepdims=True)
        dhn = dhnw * wm
        dh_ref[...] = _bf(r * (dhn - hn * _group_mean(dhn * hn, ML_DV)))
        ysv, zs, d_ys = ys_ref[...].astype(F32), zs_ref[...].astype(F32), dys_ref[...].astype(F32)
        szs = _silu(zs)
        pn, r2 = _group_norm(ysv * szs, GROUP_W)
        acc_ref[1:2, :] += jnp.sum(d_ys * pn, axis=0, keepdims=True)
        dpn = d_ys * ws_ref[...]
        dp = r2 * (dpn - pn * _group_mean(dpn * pn, GROUP_W))
        dyssd_ref[...] = _bf(dp * szs)
        dp_ref[:, O_ZS:O_ZS + 2048] = _bf(dp * ysv * _dsilu(zs))

    tile = pl.BlockSpec((ts, 2048), lambda i: (i, 0))
    col = lambda off: pl.BlockSpec((ts, 2048), lambda i: (i, off // 2048))
    row = pl.BlockSpec((1, 2048), lambda i: (0, 0))
    sds = lambda dt: jax.ShapeDtypeStruct((s, 2048), dt)
    return pl.pallas_call(
        body, name="post_bwd", grid=(s // ts,),
        in_specs=[tile, tile, tile, tile, col(O_O), col(O_ZM), col(O_ZS), row, row, pl.BlockSpec(memory_space=pl.ANY)],
        out_specs=[tile, tile, pl.BlockSpec((ts, O_MG), lambda i: (i, 0)), pl.BlockSpec((8, 2048), lambda i: (0, 0))],
        out_shape=[sds(BF16), sds(BF16), jax.ShapeDtypeStruct(dproj.shape, dproj.dtype), jax.ShapeDtypeStruct((8, 2048), F32)],
        input_output_aliases={9: 2},
        compiler_params=_cparams("arbitrary"))(dym, dys, hm, yssd, proj, proj, proj, ml_norm_w, ssm_norm_w, dproj)


def _merge(x, ym, ys, proj, target, gate, final_w, wpm, wps, wo, ts):
    wpm_t, wps_t, wo_t = wpm.T, wps.T, wo.T
    s, d = x.shape

    def body(x_ref, ym_ref, ys_ref, mg_ref, t_ref, gate_ref, fw_ref, wpm_ref, wps_ref, wo_ref, wpmt_ref, wpst_ref, wot_ref,
             dres_ref, mer_ref, dmo_ref, dpm_ref, dps_ref, dym_ref, dys_ref, dmg_ref, acc_ref):
        @pl.when(pl.program_id(0) == 0)
        def _():
            acc_ref[...] = jnp.zeros_like(acc_ref)
        gm, gs = _sigmoid(mg_ref[:, 0:d].astype(F32)), _sigmoid(mg_ref[:, d:2 * d].astype(F32))
        pm = _nn(ym_ref[...], wpm_ref[...])
        ps = _nn(ys_ref[...], wps_ref[...])
        merged = _bf(gm * pm + gs * ps)
        mer_ref[...] = merged
        mo = _nn(merged, wo_ref[...])
        gate, fw = gate_ref[...], fw_ref[...]
        out = x_ref[...] + gate * mo
        r = lax.rsqrt(jnp.mean(out * out, axis=1, keepdims=True) + EPS)
        on = out * r
        diff = on * fw - t_ref[...]
        acc_ref[0:1, :] += jnp.sum(0.5 * jnp.sum(diff * diff, axis=1, keepdims=True) / d, axis=0, keepdims=True)
        dyv = diff * (1.0 / d)
        acc_ref[1:2, :] += jnp.sum(dyv * on, axis=0, keepdims=True)
        don = dyv * fw
        dout = r * (don - on * jnp.mean(don * on, axis=1, keepdims=True))
        dres_ref[...] = dout
        acc_ref[2:3, :] += jnp.sum(dout * mo, axis=0, keepdims=True)
        dmo = _bf(dout * gate)
        dmo_ref[...] = dmo
        dmer = _nn(dmo, wot_ref[...])
        dpm, dps = _bf(dmer * gm), _bf(dmer * gs)
        dpm_ref[...] = dpm
        dps_ref[...] = dps
        dmg_ref[:, 0:d] = _bf(dmer * pm * gm * (1.0 - gm))
        dmg_ref[:, d:2 * d] = _bf(dmer * ps * gs * (1.0 - gs))
        dym_ref[...] = _bf(_nn(dpm, wpmt_ref[...]))
        dys_ref[...] = _bf(_nn(dps, wpst_ref[...]))

    t1 = pl.BlockSpec((ts, d), lambda i: (i, 0))
    t2 = pl.BlockSpec((ts, 2 * d), lambda i: (i, 0))
    row = pl.BlockSpec((1, d), lambda i: (0, 0))
    whole = pl.BlockSpec(memory_space=pltpu.VMEM)
    sd = lambda w, dt: jax.ShapeDtypeStruct((s, w), dt)
    return pl.pallas_call(
        body, name="merge_fwd_bwd", grid=(s // ts,),
        in_specs=[t1, t2, t2, pl.BlockSpec((ts, 2 * d), lambda i: (i, O_MG // (2 * d))), t1, row, row] + [whole] * 6,
        out_specs=[t1, t1, t1, t1, t1, t2, t2, pl.BlockSpec((ts, 2 * d), lambda i: (i, O_MG // (2 * d))),
                   pl.BlockSpec((8, d), lambda i: (0, 0))],
        out_shape=[sd(d, F32), sd(d, BF16), sd(d, BF16), sd(d, BF16), sd(d, BF16), sd(2 * d, BF16), sd(2 * d, BF16),
                   sd(NP, BF16), jax.ShapeDtypeStruct((8, d), F32)],
        compiler_params=_cparams("arbitrary", vmem=MERGE_VMEM))(x, ym, ys, proj, target, gate, final_w, wpm, wps, wo, wpm_t, wps_t, wo_t)


def _adamw(w, g, m, v, tr):
    if w.ndim == 2 and w.shape[0] % 8:
        tile, steps = pl.BlockSpec((w.shape[0], tr), lambda i: (0, i)), w.shape[1] // tr
    else:
        lead = (None,) * (w.ndim - 2)
        tile, steps = pl.BlockSpec(lead + (tr, w.shape[-1]), lambda i: (0,) * len(lead) + (i, 0)), w.shape[-2] // tr

    def body(w_ref, g_ref, m_ref, v_ref, d_ref, nm_ref, nv_ref):
        gv = g_ref[...]
        m2 = ADAM_B1 * m_ref[...] + (1.0 - ADAM_B1) * gv
        v2 = ADAM_B2 * v_ref[...] + (1.0 - ADAM_B2) * (gv * gv)
        m_hat = m2 / (1.0 - ADAM_B1 ** ADAM_STEP)
        v_hat = v2 / (1.0 - ADAM_B2 ** ADAM_STEP)
        d_ref[...] = -ADAM_LR * (m_hat / (jnp.sqrt(v_hat) + ADAM_EPS) + ADAM_WD * w_ref[...])
        nm_ref[...] = m2
        nv_ref[...] = v2

    return pl.pallas_call(
        body, name="adamw", grid=(steps,), in_specs=[tile] * 4, out_specs=[tile] * 3,
        out_shape=[jax.ShapeDtypeStruct(w.shape, F32)] * 3,
        compiler_params=_cparams("parallel"))(w, g.reshape(w.shape), m, v)


def _sum_parts(own, parts, tr, dtype=F32):
    p, rows, cols = parts.shape

    def body(*refs):
        p_ref, o_ref = refs[-2], refs[-1]
        acc = p_ref[0].astype(F32) if own is None else refs[0][...].astype(F32) + p_ref[0].astype(F32)
        for i in range(1, p):
            acc = acc + p_ref[i].astype(F32)
        o_ref[...] = acc.astype(dtype)

    tile = pl.BlockSpec((tr, cols), lambda i: (i, 0))
    ins = ([] if own is None else [tile]) + [pl.BlockSpec((p, tr, cols), lambda i: (0, i, 0))]
    args = ([] if own is None else [own]) + [parts]
    return pl.pallas_call(
        body, name="sum_parts", grid=(rows // tr,), in_specs=ins, out_specs=tile,
        out_shape=jax.ShapeDtypeStruct((rows, cols), dtype), compiler_params=_cparams("parallel"))(*args)


def _position():
    return lax.axis_index("x"), lax.axis_index("y"), lax.axis_index("c")


def _flip(pos, k):
    return tuple(1 - p if (k >> s) & 1 else p for p, s in zip(pos, (2, 1, 0)))


def _allgather8(block):
    rows, cols = block.shape

    def body(x_ref, o_ref, send_sems, recv_sems, local_sem):
        pos = _position()
        me = 4 * pos[0] + 2 * pos[1] + pos[2]
        mine = pltpu.make_async_copy(x_ref, o_ref.at[me], local_sem)
        mine.start()
        copies = [pltpu.make_async_remote_copy(src_ref=x_ref, dst_ref=o_ref.at[me], send_sem=send_sems.at[k - 1],
                                               recv_sem=recv_sems.at[k - 1], device_id=_flip(pos, k), device_id_type=MESH)
                  for k in range(1, N_DEV)]
        for cp in copies:
            cp.start()
        for cp in copies:
            cp.wait()
        mine.wait()

    vmem = pl.BlockSpec(memory_space=pltpu.VMEM)
    return pl.pallas_call(
        body, name="allgather8", in_specs=[vmem], out_specs=vmem,
        out_shape=jax.ShapeDtypeStruct((N_DEV, rows, cols), block.dtype),
        scratch_shapes=[pltpu.SemaphoreType.DMA((N_DEV - 1,)), pltpu.SemaphoreType.DMA((N_DEV - 1,)),
                        pltpu.SemaphoreType.DMA],
        compiler_params=pltpu.CompilerParams(vmem_limit_bytes=VMEM_LIMIT))(block)


COPY_BYTES = 1 << 20


def _row_chunks(rows, row_bytes):
    n = max(1, min(rows // 16, -(-rows * row_bytes // COPY_BYTES)))
    while rows % (16 * n):
        n -= 1
    return [(i * (rows // n), rows // n) for i in range(n)]


def _split_start(name, make_copies, n_copies, sources, lands):
    n, m = len(sources), len(lands)

    def body(*refs):
        for cp in make_copies(_position(), refs[:n], refs[n:n + m], refs[n + m], refs[n + m + 1]):
            cp.start()
        refs[-1][...] = jnp.zeros((8, LANE), F32)

    hbm = pl.BlockSpec(memory_space=pltpu.HBM)
    sem = pl.BlockSpec(memory_space=pltpu.SEMAPHORE)
    operands = [pltpu.with_memory_space_constraint(t, pltpu.HBM) for t in list(sources) + list(lands)]
    out = pl.pallas_call(
        body, name=name, in_specs=[hbm] * (n + m),
        out_specs=[sem, sem] + [hbm] * (n + m) + [pl.BlockSpec(memory_space=pltpu.VMEM)],
        out_shape=[pltpu.SemaphoreType.DMA((n_copies,)), pltpu.SemaphoreType.DMA((n_copies,))]
        + [pltpu.HBM(t.shape, t.dtype) for t in operands] + [jax.ShapeDtypeStruct((8, LANE), F32)],
        input_output_aliases={i: 2 + i for i in range(n + m)},
        compiler_params=pltpu.CompilerParams(has_side_effects=pltpu.SideEffectType.DATAFLOW_SIDE_EFFECTING))(*operands)
    return out[0], out[1], out[2:2 + n], out[2 + n:2 + n + m], out[-1]


def _split_wait(name, make_copies, send_sems, recv_sems, sources, lands, after):
    n, m = len(sources), len(lands)

    def body(*refs):
        for cp in make_copies(_position(), refs[:n], refs[n:n + m], refs[n + m], refs[n + m + 1]):
            cp.wait_send()
            cp.wait_recv()

    hbm = pl.BlockSpec(memory_space=pltpu.HBM)
    sem = pl.BlockSpec(memory_space=pltpu.SEMAPHORE)
    out = pl.pallas_call(
        body, name=name, in_specs=[hbm] * (n + m) + [sem, sem, pl.BlockSpec(memory_space=pl.ANY)],
        out_specs=[hbm] * (n + m), out_shape=[pltpu.HBM(t.shape, t.dtype) for t in list(sources) + list(lands)],
        input_output_aliases={i: i for i in range(n + m)},
        compiler_params=pltpu.CompilerParams(has_side_effects=pltpu.SideEffectType.DATAFLOW_SIDE_EFFECTING))(
            *sources, *lands, send_sems, recv_sems, after)
    return out[:n], out[n:]


def _gather_copies(pos, halves, lands, send_sems, recv_sems):
    chip, core = 2 * pos[0] + pos[1], pos[2]
    copies = []
    for a in range(len(halves)):
        for k in range(1, N_CHIPS):
            for r0, nr in _row_chunks(halves[a].shape[1], halves[a].shape[2] * halves[a].dtype.itemsize):
                i = len(copies)
                copies.append(pltpu.make_async_remote_copy(
                    src_ref=halves[a].at[core, pl.ds(r0, nr)], dst_ref=lands[a].at[chip, core, pl.ds(r0, nr)],
                    send_sem=send_sems.at[i], recv_sem=recv_sems.at[i], device_id=_flip(pos, 2 * k), device_id_type=MESH))
    return copies


def _gather_pieces(halves):
    return (N_CHIPS - 1) * sum(len(_row_chunks(a.shape[1], a.shape[2] * a.dtype.itemsize)) for a in halves)


def _pair_forward(lands):
    n = len(lands)

    def plan(pos, ins, outs):
        remote = []
        for a in range(n):
            for k in range(1, N_CHIPS):
                there = _flip(pos, 2 * k)
                for r0, nr in _row_chunks(lands[a].shape[2], lands[a].shape[3] * lands[a].dtype.itemsize):
                    slot = (2 * there[0] + there[1], pos[2], pl.ds(r0, nr))
                    remote.append((ins[a].at[slot], outs[a].at[slot], _flip(pos, 1)))
        return remote, []

    return _exchange("pair_forward", lands, [jax.ShapeDtypeStruct(t.shape, t.dtype) for t in lands], plan,
                     _gather_pieces([jax.ShapeDtypeStruct(t.shape[1:], t.dtype) for t in lands]), 0, in_place=True)


def _exchange(name, arrays, out_shapes, plan, n_remote, n_local, in_place=False):
    n, m = len(arrays), len(out_shapes)

    def body(*refs):
        send_sems, recv_sems, local_sems = refs[n + m:]
        remote, local = plan(_position(), refs[:n], refs[n:n + m])
        assert (len(remote), len(local)) == (n_remote, n_local)
        copies = [pltpu.make_async_copy(src, dst, local_sems.at[i]) for i, (src, dst) in enumerate(local)]
        copies += [pltpu.make_async_remote_copy(src_ref=src, dst_ref=dst, send_sem=send_sems.at[i], recv_sem=recv_sems.at[i],
                                                device_id=dev, device_id_type=MESH)
                   for i, (src, dst, dev) in enumerate(remote)]
        for cp in copies:
            cp.start()
        for cp in copies:
            cp.wait()

    hbm = pl.BlockSpec(memory_space=pl.ANY)
    return pl.pallas_call(
        body, name=name, in_specs=[hbm] * n, out_specs=[hbm] * m, out_shape=out_shapes,
        input_output_aliases={i: i for i in range(n)} if in_place else {},
        scratch_shapes=[pltpu.SemaphoreType.DMA((n_remote,)), pltpu.SemaphoreType.DMA((n_remote,)),
                        pltpu.SemaphoreType.DMA((max(n_local, 1),))],
        compiler_params=pltpu.CompilerParams(has_side_effects=True))(*arrays)


def _pair_send(slabs):
    n = len(slabs)
    pieces = [_row_chunks(g.shape[2], g.shape[3] * g.dtype.itemsize) for g in slabs]

    def plan(pos, ins, outs):
        return [(ins[a].at[j, 1 - pos[2], pl.ds(r0, nr)], outs[a].at[j, pl.ds(r0, nr)], _flip(pos, 1))
                for a in range(n) for j in range(N_CHIPS) for r0, nr in pieces[a]], []

    return _exchange("pair_send", slabs, [jax.ShapeDtypeStruct((N_CHIPS,) + g.shape[2:], g.dtype) for g in slabs], plan,
                     N_CHIPS * sum(len(p) for p in pieces), 0)


def _chip_scatter_copies(pos, sums, lands, send_sems, recv_sems):
    copies = []
    for a in range(len(sums)):
        for k in range(1, N_CHIPS):
            to = _flip(pos, 2 * k)
            for r0, nr in _row_chunks(sums[a].shape[1], sums[a].shape[2] * sums[a].dtype.itemsize):
                i = len(copies)
                copies.append(pltpu.make_async_remote_copy(
                    src_ref=sums[a].at[2 * to[0] + to[1], pl.ds(r0, nr)], dst_ref=lands[a].at[k - 1, pl.ds(r0, nr)],
                    send_sem=send_sems.at[i], recv_sem=recv_sems.at[i], device_id=to, device_id_type=MESH))
    return copies


def _chip_scatter_start(sums):
    n_copies = (N_CHIPS - 1) * sum(len(_row_chunks(g.shape[1], g.shape[2] * g.dtype.itemsize)) for g in sums)
    lands = [lax.empty((N_CHIPS - 1,) + g.shape[1:], g.dtype) for g in sums]
    return _split_start("chip_scatter_start", _chip_scatter_copies, n_copies, sums, lands)


def _chip_scatter_wait(send_sems, recv_sems, sums, lands, after):
    return _split_wait("chip_scatter_wait", _chip_scatter_copies, send_sems, recv_sems, sums, lands, after)


def _pair_exchange(halves):
    n = len(halves)
    pieces = [_row_chunks(h.shape[0], h.shape[1] * h.dtype.itemsize) for h in halves]

    def plan(pos, ins, outs):
        return [(ins[a].at[pl.ds(r0, nr)], outs[a].at[pl.ds(r0, nr)], _flip(pos, 1))
                for a in range(n) for r0, nr in pieces[a]], []

    return _exchange("pair_exchange", halves, [jax.ShapeDtypeStruct(h.shape, h.dtype) for h in halves], plan,
                     sum(len(p) for p in pieces), 0)


def _pack(arrays):
    flat = jnp.concatenate([a.reshape(-1).astype(F32) for a in arrays])
    size = -(-flat.shape[0] // (8 * LANE)) * (8 * LANE)
    return jnp.pad(flat, (0, size - flat.shape[0])).reshape(size // LANE, LANE)


def _unpack(buf, shapes):
    flat = buf.reshape(-1)
    out, off = [], 0
    for shp in shapes:
        n = math.prod(shp)
        out.append(flat[off:off + n].reshape(shp))
        off += n
    return out


def _unpack_rows(bufs, shapes):
    flat = bufs.reshape(bufs.shape[0], -1)
    out, off = [], 0
    for shp in shapes:
        n = math.prod(shp)
        out.append(flat[:, off:off + n].reshape((bufs.shape[0],) + shp))
        off += n
    return out


def _taps8(w):
    return jnp.pad(w, ((0, 8 - CONV_K), (0, 0)))


def _local_step(xs, tgt, scale, shift, gate, norm_w, w_in_p, b_in_p, ml_conv_w, ml_conv_b, ml_norm_w, ssm_conv_w,
                ssm_conv_b, ssm_a_log, ssm_d, ssm_norm_w, wpm, wps, wo, final_w, start_exchange=None, late_weights=None,
                u=None):
    s = xs.shape[0]
    ts = min(512, s)
    tm = min(2048, s)
    if u is None:
        u = _prenorm_fwd(xs, norm_w, scale, shift, ts)
    proj = _matmul_bias(u, w_in_p, b_in_p, tm, 512, 0, O_IF, BF16)
    gates = _matmul_bias(u, w_in_p, b_in_p, tm, 512, O_IF, NP - O_IF, F32)
    mlw8, ssw8 = _taps8(ml_conv_w), _taps8(ssm_conv_w)
    qk, qk_dact = _conv_fwd(proj, O_QK, 2048, mlw8, ml_conv_b, ts)
    xbc, xbc_dact = _conv_fwd(proj, O_XBC, 3072, ssw8, ssm_conv_b, ts)
    gt = gates[:, :LANE].T
    dtt = gates[:, O_DT - O_IF:O_DT - O_IF + LANE].T
    hm, cst, nm = _mlstm_fwd(qk, proj, gates, gt)
    alog_row = jnp.pad(ssm_a_log, ((0, 0), (0, LANE - SSM_HEADS)))
    alog_col = alog_row.reshape(LANE, 1)
    dskip_x = jnp.repeat(ssm_d[0], SSM_HEADDIM)[None]
    expand = _head_expand()
    yssd, sst = _ssd_fwd(xbc, gates, dtt, alog_row, alog_col, dskip_x, expand)
    tp = min(128, s)
    ym, ys = _post_fwd(hm, yssd, proj, ml_norm_w, ssm_norm_w, tp)
    if late_weights is not None:
        wpm, wps, wo = late_weights(ym)
    dxres, merged, dmo, dpm, dps, dym, dys, dproj, acc_m = _merge(xs, ym, ys, proj, tgt, gate, final_w, wpm, wps, wo,
                                                                  min(256, s))
    dh, dyssd, dproj, acc_p = _post_bwd(dym, dys, hm, yssd, proj, ml_norm_w, ssm_norm_w, dproj, tp)
    dqk, dproj, dif = _mlstm_bwd(qk, proj, gates, gt, hm, dh, cst, nm, dproj)
    dxbc, ddt, accd, acca = _ssd_bwd(xbc, gates, dtt, alog_row, alog_col, dskip_x, expand, expand.T, dyssd, sst)
    dproj, acc_cq = _conv_bwd(proj, O_QK, 2048, mlw8, qk_dact, dqk, dproj, ts)
    dproj, acc_cx = _conv_bwd(proj, O_XBC, 3072, ssw8, xbc_dact, dxbc, dproj, ts)
    dproj = dproj.at[:, O_IF:O_IF + SMALL_W].set(dif).at[:, O_DT:O_DT + SMALL_W].set(ddt)
    gw_in_p, gb_in_p = _matmul_tn(u.T, dproj, tm, 512, with_colsum=True, a_is_transposed=True)
    g_wpm = _matmul_tn(ym, dpm, tm, 512)
    g_wps = _matmul_tn(ys, dps, tm, 512)
    g_wo = _matmul_tn(merged, dmo, tm, 512)
    token, in_flight = (None, None) if start_exchange is None else start_exchange(gw_in_p, g_wpm, g_wps, g_wo)
    du = _matmul_nt(dproj, w_in_p, tm, 512, after=token)
    grad_x, acc_n = _prenorm_bwd(du, xs, dxres, norm_w, scale, ts)
    a_coef = -jnp.exp(ssm_a_log[0])
    small = dict(
        mod=jnp.concatenate([acc_n[2], acc_n[1], acc_m[2]]), norm_w=acc_n[0], b_in=_unpad_cols(gb_in_p[0]),
        ml_conv_w=acc_cq[0:CONV_K], ml_conv_b=acc_cq[CONV_K], ml_norm_w=acc_p[0], ssm_conv_w=acc_cx[0:CONV_K],
        ssm_conv_b=acc_cx[CONV_K], ssm_a_log=acca[0, :SSM_HEADS] * a_coef,
        ssm_d=accd[0].reshape(SSM_HEADS, SSM_HEADDIM).sum(axis=1), ssm_norm_w=acc_p[1], final_w=acc_m[1], loss=acc_m[0, 0:1])
    return grad_x, small, gw_in_p, g_wpm, g_wps, g_wo, in_flight


WEIGHTS = ("norm_w", "ada_w", "ada_b", "w_in", "b_in", "ml_conv_w", "ml_conv_b", "ml_norm_w", "ssm_conv_w", "ssm_conv_b",
           "ssm_a_log", "ssm_d", "ssm_norm_w", "w_proj_m", "w_proj_s", "w_out", "final_w")
LARGE = ("ada_w", "w_in", "w_proj_m", "w_proj_s", "w_out")
SMALL_SUMS = (("mod", (3 * D_MODEL,)), ("norm_w", (D_MODEL,)), ("b_in", (IN_WIDTH,)), ("ml_conv_w", (CONV_K, 2048)),
              ("ml_conv_b", (2048,)), ("ml_norm_w", (2048,)), ("ssm_conv_w", (CONV_K, 3072)), ("ssm_conv_b", (3072,)),
              ("ssm_a_log", (SSM_HEADS,)), ("ssm_d", (SSM_HEADS,)), ("ssm_norm_w", (2048,)), ("final_w", (D_MODEL,)),
              ("loss", (1,)))


def kernel(x, c, norm_w, ada_w, ada_b, w_in, b_in, ml_conv_w, ml_conv_b, ml_norm_w, ssm_conv_w, ssm_conv_b, ssm_a_log, ssm_d, ssm_norm_w, w_proj_m, w_proj_s, w_out, final_w, loss_target, m_norm_w, m_ada_w, m_ada_b, m_w_in, m_b_in, m_ml_conv_w, m_ml_conv_b, m_ml_norm_w, m_ssm_conv_w, m_ssm_conv_b, m_ssm_a_log, m_ssm_d, m_ssm_norm_w, m_w_proj_m, m_w_proj_s, m_w_out, m_final_w, v_norm_w, v_ada_w, v_ada_b, v_w_in, v_b_in, v_ml_conv_w, v_ml_conv_b, v_ml_norm_w, v_ssm_conv_w, v_ssm_conv_b, v_ssm_a_log, v_ssm_d, v_ssm_norm_w, v_w_proj_m, v_w_proj_s, v_w_out, v_final_w):
    w = dict(norm_w=norm_w, ada_w=ada_w, ada_b=ada_b, w_in=w_in, b_in=b_in, ml_conv_w=ml_conv_w, ml_conv_b=ml_conv_b,
             ml_norm_w=ml_norm_w, ssm_conv_w=ssm_conv_w, ssm_conv_b=ssm_conv_b, ssm_a_log=ssm_a_log, ssm_d=ssm_d,
             ssm_norm_w=ssm_norm_w, w_proj_m=w_proj_m, w_proj_s=w_proj_s, w_out=w_out, final_w=final_w)
    m = dict(zip(WEIGHTS, (m_norm_w, m_ada_w, m_ada_b, m_w_in, m_b_in, m_ml_conv_w, m_ml_conv_b, m_ml_norm_w, m_ssm_conv_w,
                           m_ssm_conv_b, m_ssm_a_log, m_ssm_d, m_ssm_norm_w, m_w_proj_m, m_w_proj_s, m_w_out, m_final_w)))
    v = dict(zip(WEIGHTS, (v_norm_w, v_ada_w, v_ada_b, v_w_in, v_b_in, v_ml_conv_w, v_ml_conv_b, v_ml_norm_w, v_ssm_conv_w,
                           v_ssm_conv_b, v_ssm_a_log, v_ssm_d, v_ssm_norm_w, v_w_proj_m, v_w_proj_s, v_w_out, v_final_w)))
    pos = _position()
    chip = 2 * pos[0] + pos[1]
    dev = 2 * chip + pos[2]
    mlw_cols, ssw_cols, ada_cols = ml_conv_w.shape[2], ssm_conv_w.shape[2], ada_w.shape[2]

    g0 = _allgather8(_pack([c, ml_conv_w, ssm_conv_w]))
    c_all, mlw_all, ssw_all = _unpack_rows(g0, [(D_MODEL,), (CONV_K, mlw_cols), (CONV_K, ssw_cols)])
    ml_conv_full = mlw_all[0::2].transpose(1, 0, 2).reshape(CONV_K, N_CHIPS * mlw_cols)
    ssm_conv_full = ssw_all[0::2].transpose(1, 0, 2).reshape(CONV_K, N_CHIPS * ssw_cols)

    ada_b_mine = lax.dynamic_slice_in_dim(ada_b, chip * ada_cols, ada_cols, axis=1)
    g1 = _allgather8(_ada_fwd(c_all, ada_w[0], ada_b_mine))
    mod = lax.dynamic_index_in_dim(g1[0::2], dev, axis=1, keepdims=False).reshape(1, 3 * D_MODEL)
    shift, scale, gate = mod[:, :D_MODEL], mod[:, D_MODEL:2 * D_MODEL], mod[:, 2 * D_MODEL:]

    def whole(lands, owns):
        return [lax.dynamic_update_index_in_dim(got, own, chip, 0).reshape(N_CHIPS, -1, own.shape[-1])
                for got, own in zip(_pair_forward(lands), owns)]

    mine = [_bf(a[0]).reshape(2, a.shape[1] // 2, a.shape[2]) for a in (w_in, w_proj_m, w_proj_s, w_out)]
    landing = lambda own: lax.empty((N_CHIPS,) + own.shape, own.dtype)
    w_send, w_recv, w_src, w_land, w_token = _split_start("w_in_gather_start", _gather_copies, _gather_pieces(mine[:1]),
                                                          mine[:1], [landing(mine[0])])
    u = _prenorm_fwd(x[0], norm_w, scale + w_token[0:1, 0:1], shift, 512)
    w_src, w_land = _split_wait("w_in_gather_wait", _gather_copies, w_send, w_recv, w_src, w_land, u)
    behind = (w_src[0][0, 0:1, 0:1] * 0).astype(BF16)
    later = [a + behind for a in mine[1:]]
    p_send, p_recv, p_src, p_land, p_token = _split_start("merge_gather_start", _gather_copies, _gather_pieces(later),
                                                          later, [landing(a) for a in later])
    w_in_p = _shards_to_padded(whole(w_land, w_src)[0])
    b_in_p = _pad_cols(b_in) + p_token[0:1, 0:1]

    def late_weights(after):
        srcs, lands = _split_wait("merge_gather_wait", _gather_copies, p_send, p_recv, p_src, p_land, after)
        return [a.reshape(-1, D_MODEL) for a in whole(lands, srcs)]

    def start_exchange(g_w_in, g_wpm, g_wps, g_wo):
        split = lambda g, rows: _bf(g).reshape(N_CHIPS, 2, rows // (2 * N_CHIPS), g.shape[-1])
        slabs = [split(_padded_to_shards(_bf(g_w_in)), N_CHIPS * D_MODEL),
                 split(g_wpm, g_wpm.shape[0]), split(g_wps, g_wps.shape[0]), split(g_wo, g_wo.shape[0])]
        pair_sums = []
        for slab, rec in zip(slabs, _pair_send(slabs)):
            kept = lax.dynamic_index_in_dim(slab, pos[2], 1, keepdims=False)
            rows = kept.shape[0] * kept.shape[1]
            both = _sum_parts(kept.reshape(rows, -1), rec.reshape(1, rows, -1), 32, BF16)
            pair_sums.append(both.reshape(kept.shape))
        send_sems, recv_sems, sums, lands, token = _chip_scatter_start(pair_sums)
        return token, (send_sems, recv_sems, sums, lands)

    grad_x, small, _, _, _, _, in_flight = _local_step(
        x[0], loss_target[0], scale, shift, gate, norm_w, w_in_p, b_in_p, ml_conv_full, ml_conv_b, ml_norm_w,
        ssm_conv_full, ssm_conv_b, ssm_a_log, ssm_d, ssm_norm_w, None, None, None, final_w[None], start_exchange,
        late_weights, u)

    g2 = _allgather8(_pack([small[name] for name, _ in SMALL_SUMS]))
    total = dict(zip([name for name, _ in SMALL_SUMS], _unpack(_sum_parts(None, g2, g2.shape[1]), [s for _, s in SMALL_SUMS])))
    dmod_all = g2[:, :3 * D_MODEL // LANE].reshape(N_DEV, 3 * D_MODEL)
    grads = dict(total)
    grads["ada_b"] = total["mod"]
    grads["ml_conv_w"] = lax.dynamic_slice_in_dim(total["ml_conv_w"], chip * mlw_cols, mlw_cols, axis=1)
    grads["ssm_conv_w"] = lax.dynamic_slice_in_dim(total["ssm_conv_w"], chip * ssw_cols, ssw_cols, axis=1)
    grads["ada_w"] = _ada_bwd(c_all, lax.dynamic_slice_in_dim(dmod_all, chip * ada_cols, ada_cols, axis=1))

    halves = []
    for both, rec in zip(*_chip_scatter_wait(*in_flight, grad_x)):
        halves.append(_sum_parts(lax.dynamic_index_in_dim(both, chip, 0, keepdims=False), rec, 32))
    for name, half, other in zip(("w_in", "w_proj_m", "w_proj_s", "w_out"), halves, _pair_exchange(halves)):
        grads[name] = lax.cond(pos[2] == 0, lambda mine, theirs: jnp.concatenate([mine, theirs]),
                               lambda mine, theirs: jnp.concatenate([theirs, mine]), half, other)

    delta, new_m, new_v = {}, {}, {}
    for name in LARGE:
        if w[name].shape[-1] % LANE:
            flat = lambda a: a.reshape(a.shape[-2:]).T
            back = lambda a: a.T.reshape(w[name].shape)
            g_flat = flat(grads[name])
            delta[name], new_m[name], new_v[name] = (back(a) for a in _adamw(flat(w[name]), g_flat, flat(m[name]), flat(v[name]), LANE))
            grads[name] = back(g_flat)
        else:
            delta[name], new_m[name], new_v[name] = _adamw(w[name], grads[name], m[name], v[name], 64)
    rest = [name for name in WEIGHTS if name not in LARGE]
    packed = [_pack([t[name] for name in rest]) for t in (w, grads, m, v)]
    for out, buf in zip((delta, new_m, new_v), _adamw(*packed, packed[0].shape[0])):
        out.update(zip(rest, _unpack(buf, [w[name].shape for name in rest])))
    loss = total["loss"][0]
    return (loss, grad_x[None], *[grads[name].reshape(w[name].shape) for name in WEIGHTS], *[delta[name] for name in WEIGHTS],
            *[new_m[name] for name in WEIGHTS], *[new_v[name] for name in WEIGHTS])
```

```python
import functools
import math

import jax
import jax.numpy as jnp
from jax import lax
from jax.experimental import pallas as pl
from jax.experimental.pallas import tpu as pltpu

F32 = jnp.float32
BF16 = jnp.bfloat16
HI = lax.Precision.HIGHEST
MESH = pl.DeviceIdType.MESH

D_MODEL = 1024
EPS = 1e-6
CONV_K = 4
ML_HEADS = 8
ML_DQK = 128
ML_DV = 256
SSM_HEADS = 32
SSM_HEADDIM = 64
SSM_GROUPS = 4
SSM_STATE = 128
IN_WIDTH = 15408
N_CHIPS = 4
N_DEV = 8
ADAM_LR, ADAM_B1, ADAM_B2, ADAM_EPS, ADAM_WD, ADAM_STEP = 0.001, 0.9, 0.999, 1e-08, 0.01, 10

O_O, O_ZM, O_ZS, O_MG, O_QK, O_V, O_XBC, O_IF, O_DT = 0, 2048, 4096, 6144, 8192, 10240, 12288, 15360, 15616
SMALL_W = 256
NP = 15872
LANE = 128
CHUNK = 128
NEG = -1e30
VMEM_LIMIT = 48 * 1024 * 1024
MERGE_VMEM = 60 * 1024 * 1024


def _cparams(*sem, vmem=VMEM_LIMIT):
    return pltpu.CompilerParams(dimension_semantics=sem, vmem_limit_bytes=vmem)


def _pad_cols(w):
    z = lambda n: jnp.zeros(w.shape[:-1] + (n,), w.dtype)
    return jnp.concatenate([w[..., 4096:8192], w[..., 11280:13328], w[..., 13360:15408], w[..., :4096], w[..., 8208:11280],
                            w[..., 8192:8208], z(SMALL_W - 16), w[..., 13328:13360], z(SMALL_W - 32)], axis=-1)


def _unpad_cols(g):
    return jnp.concatenate([g[..., O_QK:O_QK + 4096], g[..., O_O:O_O + 4096], g[..., O_IF:O_IF + 16],
                            g[..., O_XBC:O_XBC + 3072], g[..., O_ZS:O_ZS + 2048], g[..., O_DT:O_DT + 32],
                            g[..., O_MG:O_MG + 2048]], axis=-1)


PADDED_SEGMENTS = ((4096, 8192, 0), (11280, 13328, 0), (13360, 15408, 0), (0, 4096, 0), (8208, 11280, 0),
                   (8192, 8208, SMALL_W - 16), (13328, 13360, SMALL_W - 32))
SHARD_W = IN_WIDTH // N_CHIPS


def _shards_to_padded(shards):
    parts = []
    for first, last, pad in PADDED_SEGMENTS:
        for j in range(N_CHIPS):
            lo, hi = max(first, j * SHARD_W), min(last, (j + 1) * SHARD_W)
            if lo < hi:
                parts.append(shards[j][:, lo - j * SHARD_W:hi - j * SHARD_W])
        if pad:
            parts.append(jnp.zeros((shards.shape[1], pad), shards.dtype))
    return jnp.concatenate(parts, axis=1)


def _padded_to_shards(g):
    offsets, off = {}, 0
    for first, last, pad in PADDED_SEGMENTS:
        offsets[first] = off
        off += last - first + pad
    shards = []
    for j in range(N_CHIPS):
        parts = []
        for first, last, _ in sorted(PADDED_SEGMENTS):
            lo, hi = max(first, j * SHARD_W), min(last, (j + 1) * SHARD_W)
            if lo < hi:
                parts.append(g[:, offsets[first] + lo - first:offsets[first] + hi - first])
        shards.append(jnp.concatenate(parts, axis=1))
    return jnp.stack(shards)


def _sigmoid(x):
    return 0.5 * jnp.tanh(0.5 * x) + 0.5


def _silu(x):
    return x * _sigmoid(x)


def _dsilu(x):
    s = _sigmoid(x)
    return s + x * s * (1.0 - s)


def _softplus(x):
    return jnp.maximum(x, 0.0) + jnp.log(1.0 + jnp.exp(-jnp.abs(x)))


def _logsigmoid(x):
    return jnp.minimum(x, 0.0) - jnp.log(1.0 + jnp.exp(-jnp.abs(x)))


def _dot(a, b, dims, precision=None):
    return lax.dot_general(a, b, (dims, ((), ())), preferred_element_type=F32, precision=precision)


def _nn(a, b, precision=None):
    return _dot(a, b, ((1,), (0,)), precision)


def _nt(a, b, precision=None):
    return _dot(a, b, ((1,), (1,)), precision)


def _tn(a, b, precision=None):
    return _dot(a, b, ((0,), (0,)), precision)


def _bf(x):
    return x.astype(BF16)


def _split(x, terms):
    parts = []
    for _ in range(terms):
        part = _bf(x)
        parts.append(part)
        x = x - part.astype(F32)
    return parts


def _pick_right(x, pick, terms):
    pick = _bf(pick)
    out = None
    for part in _split(x, terms):
        out = _nn(part, pick) if out is None else out + _nn(part, pick)
    return out


def _pick_left(pick, x, terms):
    pick = _bf(pick)
    out = None
    for part in _split(x, terms):
        out = _nn(pick, part) if out is None else out + _nn(pick, part)
    return out


def _lane_col(x, lane):
    idx = lax.broadcasted_iota(jnp.int32, x.shape, 1)
    return jnp.sum(jnp.where(idx == lane, x, 0.0), axis=1, keepdims=True)


def _tri(n, upper):
    r = lax.broadcasted_iota(jnp.int32, (n, n), 0)
    c = lax.broadcasted_iota(jnp.int32, (n, n), 1)
    return jnp.where((r <= c) if upper else (r >= c), 1.0, 0.0).astype(F32)


def _eye(n):
    return jnp.where(lax.broadcasted_iota(jnp.int32, (n, n), 0) == lax.broadcasted_iota(jnp.int32, (n, n), 1), 1.0, 0.0)


def _sum_all(x):
    return jnp.sum(jnp.sum(x, axis=1, keepdims=True), axis=0, keepdims=True)


def _crossing(p):
    L = p.shape[0]
    below = _nn(_bf(_tri(L, True)), _bf(p))
    strict = lax.broadcasted_iota(jnp.int32, (L, L), 0) > lax.broadcasted_iota(jnp.int32, (L, L), 1)
    return [jnp.sum(jnp.where(strict, below[:, b * L:(b + 1) * L], 0.0), axis=1, keepdims=True)
            for b in range(p.shape[1] // L)]


def _matmul_bias(a, w, bias, tm, tn, col0, ncols, dtype):
    m, k = a.shape
    j0 = col0 // tn

    def body(a_ref, w_ref, b_ref, o_ref):
        o_ref[...] = (_nn(a_ref[...], w_ref[...]) + b_ref[...]).astype(dtype)

    return pl.pallas_call(
        body, name="matmul_bias", grid=(m // tm, ncols // tn),
        in_specs=[pl.BlockSpec((tm, k), lambda i, j: (i, 0)), pl.BlockSpec((k, tn), lambda i, j: (0, j0 + j)),
                  pl.BlockSpec((1, tn), lambda i, j: (0, j0 + j))],
        out_specs=pl.BlockSpec((tm, tn), lambda i, j: (i, j)),
        out_shape=jax.ShapeDtypeStruct((m, ncols), dtype),
        compiler_params=_cparams("parallel", "arbitrary"))(a, w, bias)


def _matmul_nt(a, w, tm, tk, after=None):
    m, n = a.shape
    k = w.shape[0]

    def body(a_ref, w_ref, *rest):
        o_ref = rest[-1]

        @pl.when(pl.program_id(1) == 0)
        def _():
            o_ref[...] = jnp.zeros_like(o_ref)
        o_ref[...] += _nt(a_ref[...], w_ref[...])

    extra = [] if after is None else [after]
    return pl.pallas_call(
        body, name="matmul_nt", grid=(m // tm, n // tk),
        in_specs=[pl.BlockSpec((tm, tk), lambda i, j: (i, j)), pl.BlockSpec((k, tk), lambda i, j: (0, j))]
        + [pl.BlockSpec(memory_space=pl.ANY)] * len(extra),
        out_specs=pl.BlockSpec((tm, k), lambda i, j: (i, 0)),
        out_shape=jax.ShapeDtypeStruct((m, k), F32),
        compiler_params=_cparams("parallel", "arbitrary"))(a, w, *extra)


def _matmul_tn(a, b, tm, tn, with_colsum=False, a_is_transposed=False, dtype=F32):
    k, m = a.shape if a_is_transposed else a.shape[::-1]
    n = b.shape[1]
    steps = m // tm

    def body(a_ref, b_ref, o_ref, *rest):
        acc_ref = o_ref if dtype == F32 else rest[-1]
        first = pl.program_id(1) == 0

        @pl.when(first)
        def _():
            acc_ref[...] = jnp.zeros_like(acc_ref)
        acc_ref[...] += _nn(a_ref[...], b_ref[...]) if a_is_transposed else _tn(a_ref[...], b_ref[...])
        if dtype != F32:
            @pl.when(pl.program_id(1) == steps - 1)
            def _():
                o_ref[...] = acc_ref[...].astype(dtype)
        if with_colsum:
            s_ref = rest[0]

            @pl.when(first)
            def _():
                s_ref[...] = jnp.zeros_like(s_ref)
            s_ref[...] += jnp.sum(b_ref[...].astype(F32), axis=0, keepdims=True)

    out_specs = [pl.BlockSpec((k, tn), lambda j, i: (0, j))]
    out_shape = [jax.ShapeDtypeStruct((k, n), dtype)]
    if with_colsum:
        out_specs.append(pl.BlockSpec((1, tn), lambda j, i: (0, j)))
        out_shape.append(jax.ShapeDtypeStruct((1, n), F32))
    out = pl.pallas_call(
        body, name="matmul_tn", grid=(n // tn, m // tm),
        in_specs=[pl.BlockSpec((k, tm), lambda j, i: (0, i)) if a_is_transposed else pl.BlockSpec((tm, k), lambda j, i: (i, 0)),
                  pl.BlockSpec((tm, tn), lambda j, i: (i, j))],
        out_specs=out_specs, out_shape=out_shape,
        scratch_shapes=[] if dtype == F32 else [pltpu.VMEM((k, tn), F32)],
        compiler_params=_cparams("parallel", "arbitrary"))(a, b)
    return out if with_colsum else out[0]


def _ada_fwd(c_all, ada_w, ada_b):
    def body(c_ref, w_ref, b_ref, o_ref):
        o_ref[...] = _nn(_bf(_silu(c_ref[...])), _bf(w_ref[...])) + b_ref[...]

    return pl.pallas_call(body, name="ada_fwd", out_shape=jax.ShapeDtypeStruct((c_all.shape[0], ada_w.shape[1]), F32),
                          compiler_params=_cparams())(c_all, ada_w, ada_b)


def _ada_bwd(c_all, dmod):
    def body(c_ref, d_ref, o_ref):
        o_ref[...] = _tn(_bf(_silu(c_ref[...])), _bf(d_ref[...]))

    return pl.pallas_call(body, name="ada_bwd", out_shape=jax.ShapeDtypeStruct((c_all.shape[1], dmod.shape[1]), F32),
                          compiler_params=_cparams())(c_all, dmod)


def _prenorm_fwd(x, norm_w, scale, shift, ts):
    s, d = x.shape

    def body(x_ref, nw_ref, sc_ref, sh_ref, u_ref):
        xv = x_ref[...]
        r = lax.rsqrt(jnp.mean(xv * xv, axis=1, keepdims=True) + EPS)
        u_ref[...] = _bf(xv * r * nw_ref[...] * (1.0 + sc_ref[...]) + sh_ref[...])

    row = pl.BlockSpec((1, d), lambda i: (0, 0))
    return pl.pallas_call(
        body, name="prenorm_fwd", grid=(s // ts,),
        in_specs=[pl.BlockSpec((ts, d), lambda i: (i, 0)), row, row, row],
        out_specs=pl.BlockSpec((ts, d), lambda i: (i, 0)), out_shape=jax.ShapeDtypeStruct((s, d), BF16),
        compiler_params=_cparams("parallel"))(x, norm_w, scale, shift)


def _prenorm_bwd(du, x, dxres, norm_w, scale, ts):
    s, d = x.shape

    def body(du_ref, x_ref, dr_ref, nw_ref, sc_ref, gx_ref, acc_ref):
        @pl.when(pl.program_id(0) == 0)
        def _():
            acc_ref[...] = jnp.zeros_like(acc_ref)
        xv, duv = x_ref[...], du_ref[...]
        r = lax.rsqrt(jnp.mean(xv * xv, axis=1, keepdims=True) + EPS)
        xn = xv * r
        nw, sc1 = nw_ref[...], 1.0 + sc_ref[...]
        dxn = duv * (nw * sc1)
        gx_ref[...] = r * (dxn - xn * jnp.mean(dxn * xn, axis=1, keepdims=True)) + dr_ref[...]
        t = duv * xn
        acc_ref[0:1, :] += jnp.sum(t, axis=0, keepdims=True) * sc1
        acc_ref[1:2, :] += jnp.sum(t, axis=0, keepdims=True) * nw
        acc_ref[2:3, :] += jnp.sum(duv, axis=0, keepdims=True)

    tile = pl.BlockSpec((ts, d), lambda i: (i, 0))
    row = pl.BlockSpec((1, d), lambda i: (0, 0))
    return pl.pallas_call(
        body, name="prenorm_bwd", grid=(s // ts,),
        in_specs=[tile, tile, tile, row, row],
        out_specs=[tile, pl.BlockSpec((8, d), lambda i: (0, 0))],
        out_shape=[jax.ShapeDtypeStruct((s, d), F32), jax.ShapeDtypeStruct((8, d), F32)],
        compiler_params=_cparams("arbitrary"))(du, x, dxres, norm_w, scale)


CONV_CB = 512


def _conv_taps(buf_ref, ts):
    return [buf_ref[pl.ds(8 - (CONV_K - 1) + j, ts), :] for j in range(CONV_K)]


def _conv_fwd(proj, col0, width, w8, b, ts):
    s = proj.shape[0]
    cb = CONV_CB
    nt = s // ts

    def body(x_ref, w_ref, b_ref, o_ref, ds_ref, buf_ref):
        @pl.when(pl.program_id(1) == 0)
        def _():
            buf_ref[0:8, :] = jnp.zeros((8, cb), F32)
        buf_ref[pl.ds(8, ts), :] = x_ref[...].astype(F32)
        acc = b_ref[...] + jnp.zeros((ts, cb), F32)
        for j, tap in enumerate(_conv_taps(buf_ref, ts)):
            acc = acc + tap * w_ref[j:j + 1, :]
        sg = _sigmoid(acc)
        o_ref[...] = acc * sg
        ds_ref[...] = _bf(sg + acc * sg * (1.0 - sg))
        buf_ref[0:8, :] = buf_ref[pl.ds(ts, 8), :]

    c0 = col0 // cb
    tile = pl.BlockSpec((ts, cb), lambda c, i: (i, c))
    return pl.pallas_call(
        body, name="conv_fwd", grid=(width // cb, nt),
        in_specs=[pl.BlockSpec((ts, cb), lambda c, i: (i, c0 + c)), pl.BlockSpec((8, cb), lambda c, i: (0, c)),
                  pl.BlockSpec((1, cb), lambda c, i: (0, c))],
        out_specs=[tile, tile],
        out_shape=[jax.ShapeDtypeStruct((s, width), F32), jax.ShapeDtypeStruct((s, width), BF16)],
        scratch_shapes=[pltpu.VMEM((ts + 8, cb), F32)],
        compiler_params=_cparams("parallel", "arbitrary"))(proj, w8, b)


def _conv_bwd(proj, col0, width, w8, dact, dpost, dproj, ts):
    s = proj.shape[0]
    cb = CONV_CB
    nt = s // ts
    c0 = col0 // cb

    def body(x_ref, da_ref, dp_ref, w_ref, _, dx_ref, acc_ref, dbuf_ref):
        @pl.when(pl.program_id(1) == 0)
        def _():
            acc_ref[...] = jnp.zeros_like(acc_ref)
            dbuf_ref[pl.ds(ts, 8), :] = jnp.zeros((8, cb), F32)
        dconv = dp_ref[...].astype(F32) * da_ref[...].astype(F32)
        acc_ref[CONV_K:CONV_K + 1, :] += jnp.sum(dconv, axis=0, keepdims=True)
        dbuf_ref[pl.ds(0, ts), :] = dconv
        xv = x_ref[...].astype(F32)
        dx = jnp.zeros((ts, cb), F32)
        for j in range(CONV_K):
            shifted = dbuf_ref[pl.ds(CONV_K - 1 - j, ts), :]
            dx = dx + shifted * w_ref[j:j + 1, :]
            acc_ref[j:j + 1, :] += jnp.sum(xv * shifted, axis=0, keepdims=True)
        dx_ref[...] = _bf(dx)
        dbuf_ref[pl.ds(ts, 8), :] = dconv[0:8, :]

    tile = pl.BlockSpec((ts, cb), lambda c, i: (nt - 1 - i, c))
    wide = pl.BlockSpec((ts, cb), lambda c, i: (nt - 1 - i, c0 + c))
    return pl.pallas_call(
        body, name="conv_bwd", grid=(width // cb, nt),
        in_specs=[wide, tile, tile, pl.BlockSpec((8, cb), lambda c, i: (0, c)), pl.BlockSpec(memory_space=pl.ANY)],
        out_specs=[wide, pl.BlockSpec((8, cb), lambda c, i: (0, c))],
        out_shape=[jax.ShapeDtypeStruct(dproj.shape, dproj.dtype), jax.ShapeDtypeStruct((8, width), F32)],
        input_output_aliases={4: 0},
        scratch_shapes=[pltpu.VMEM((ts + 8, cb), F32)],
        compiler_params=_cparams("parallel", "arbitrary"))(proj, dact, dpost, w8, dproj)


def _mlstm_gates(gif_ref, gt_ref, a_scr, at_scr):
    L = gif_ref.shape[0]
    fb = _logsigmoid(gif_ref[...])
    a_scr[...] = _pick_left(_tri(L, False), fb, 3)
    at_scr[...] = _pick_right(_logsigmoid(gt_ref[...]), _tri(L, True), 3)
    return jnp.sum(fb, axis=0, keepdims=True)


def _mlstm_head(h, qk_ref, v_ref, gif, gt_ref, a, at_scr, a_last_row, c_mat, n_row, m_prev):
    L = gif.shape[0]
    q = qk_ref[:, h * ML_DQK:(h + 1) * ML_DQK] * (ML_DQK ** -0.5)
    k = qk_ref[:, (ML_HEADS + h) * ML_DQK:(ML_HEADS + h + 1) * ML_DQK]
    v = v_ref[:, h * ML_DV:(h + 1) * ML_DV]
    i_col, a_col = _lane_col(gif, h), _lane_col(a, ML_HEADS + h)
    i_row, a_row = gt_ref[h:h + 1, :], at_scr[ML_HEADS + h:ML_HEADS + h + 1, :]
    causal = lax.broadcasted_iota(jnp.int32, (L, L), 0) >= lax.broadcasted_iota(jnp.int32, (L, L), 1)
    dmat = jnp.where(causal, a_col - a_row + i_row, NEG)
    inter = a_col + m_prev
    m_t = jnp.maximum(inter, jnp.max(dmat, axis=1, keepdims=True))
    w_intra = jnp.exp(dmat - m_t)
    w_inter = jnp.exp(inter - m_t)
    sc = _nt(_bf(q), _bf(k)) * w_intra
    den = jnp.sum(sc, axis=1, keepdims=True) + w_inter * jnp.sum(q * n_row, axis=1, keepdims=True)
    floor = jnp.exp(-m_t)
    a_last = _lane_col(a_last_row, ML_HEADS + h)
    g = a_last - a_col + i_col
    m_new = jnp.maximum(a_last + m_prev, jnp.max(g, axis=0, keepdims=True))
    wk = jnp.exp(g - m_new)
    decay = jnp.exp(a_last + m_prev - m_new)
    return dict(q=q, k=k, v=v, w_intra=w_intra, w_inter=w_inter, sc=sc, den=den, floor=floor, m_new=m_new, wk=wk,
                decay=decay)


def _state_tile(n_row, m11):
    r = lax.broadcasted_iota(jnp.int32, (8, LANE), 0)
    return jnp.where(r == 0, n_row, jnp.where(r == 1, m11, 0.0))


def _mlstm_fwd(qk, proj, gates, gt):
    s = qk.shape[0]
    L = CHUNK
    nc = s // L

    def body(qk_ref, v_ref, gif_ref, gt_ref, h_ref, cst_ref, nm_ref, c_scr, nm_scr, a_scr, at_scr):
        @pl.when(pl.program_id(0) == 0)
        def _():
            c_scr[...] = jnp.zeros_like(c_scr)
            nm_scr[...] = jnp.zeros_like(nm_scr)
        a_last_row = _mlstm_gates(gif_ref, gt_ref, a_scr, at_scr)
        gif, a = gif_ref[...], a_scr[...]
        for h in range(ML_HEADS):
            c_mat, n_row = c_scr[h], nm_scr[h, 0:1, :]
            m_prev = jnp.max(nm_scr[h, 1:2, :], axis=1, keepdims=True)
            cst_ref[0, h] = c_mat
            nm_ref[0, h] = nm_scr[h]
            t = _mlstm_head(h, qk_ref, v_ref, gif, gt_ref, a, at_scr, a_last_row, c_mat, n_row, m_prev)
            num = _nn(_bf(t["sc"]), _bf(t["v"])) + t["w_inter"] * _nn(_bf(t["q"]), _bf(c_mat))
            h_ref[:, h * ML_DV:(h + 1) * ML_DV] = _bf(num * (1.0 / jnp.maximum(jnp.abs(t["den"]), t["floor"])))
            kw = t["k"] * t["wk"]
            c_scr[h] = t["decay"] * c_mat + _tn(_bf(kw), _bf(t["v"]))
            nm_scr[h] = _state_tile(t["decay"] * n_row + jnp.sum(kw, axis=0, keepdims=True), t["m_new"])

    return pl.pallas_call(
        body, name="mlstm_fwd", grid=(nc,),
        in_specs=[pl.BlockSpec((L, 2048), lambda c: (c, 0)), pl.BlockSpec((L, 2048), lambda c: (c, O_V // 2048)),
                  pl.BlockSpec((L, LANE), lambda c: (c, 0)), pl.BlockSpec((LANE, L), lambda c: (0, c))],
        out_specs=[pl.BlockSpec((L, 2048), lambda c: (c, 0)),
                   pl.BlockSpec((1, ML_HEADS, ML_DQK, ML_DV), lambda c: (c, 0, 0, 0)),
                   pl.BlockSpec((1, ML_HEADS, 8, LANE), lambda c: (c, 0, 0, 0))],
        out_shape=[jax.ShapeDtypeStruct((s, 2048), BF16), jax.ShapeDtypeStruct((nc, ML_HEADS, ML_DQK, ML_DV), F32),
                   jax.ShapeDtypeStruct((nc, ML_HEADS, 8, LANE), F32)],
        scratch_shapes=[pltpu.VMEM((ML_HEADS, ML_DQK, ML_DV), F32), pltpu.VMEM((ML_HEADS, 8, LANE), F32),
                        pltpu.VMEM((L, LANE), F32), pltpu.VMEM((LANE, L), F32)],
        compiler_params=_cparams("arbitrary"))(qk, proj, gates, gt)


def _mlstm_bwd(qk, proj, gates, gt, hout, dh, cst, nm, dproj):
    s = qk.shape[0]
    L = CHUNK
    nc = s // L

    def body(qk_ref, v_ref, gif_ref, gt_ref, h_ref, dh_ref, cst_ref, nm_ref, _, dqk_ref, dv_ref, dif_ref,
             dc_scr, dn_scr, a_scr, at_scr):
        @pl.when(pl.program_id(0) == 0)
        def _():
            dc_scr[...] = jnp.zeros_like(dc_scr)
            dn_scr[...] = jnp.zeros_like(dn_scr)
        a_last_row = _mlstm_gates(gif_ref, gt_ref, a_scr, at_scr)
        gif, a = gif_ref[...], a_scr[...]
        lane = lax.broadcasted_iota(jnp.int32, (L, LANE), 1)
        last = lax.broadcasted_iota(jnp.int32, (L, 1), 0) == L - 1
        di_tile = jnp.zeros((L, LANE), F32)
        cross = [jnp.zeros((L, LANE), F32)] * 3
        dlogw = []
        for h in range(ML_HEADS):
            c_mat, n_row = cst_ref[0, h], nm_ref[0, h, 0:1, :]
            m_prev = jnp.max(nm_ref[0, h, 1:2, :], axis=1, keepdims=True)
            t = _mlstm_head(h, qk_ref, v_ref, gif, gt_ref, a, at_scr, a_last_row, c_mat, n_row, m_prev)
            q, k, v, den = t["q"], t["k"], t["v"], t["den"]
            dhh = dh_ref[:, h * ML_DV:(h + 1) * ML_DV].astype(F32)
            hh = h_ref[:, h * ML_DV:(h + 1) * ML_DV].astype(F32)
            dnorm = jnp.maximum(jnp.abs(den), t["floor"])
            inv = 1.0 / dnorm
            dnum = dhh * inv
            d_dn = -jnp.sum(dhh * hh, axis=1, keepdims=True) * inv
            dden = jnp.where(jnp.abs(den) >= t["floor"], jnp.where(den >= 0.0, d_dn, -d_dn), 0.0)
            dsc = _nt(_bf(dnum), _bf(v)) + dden
            ds = dsc * t["w_intra"]
            dq_inter = t["w_inter"] * (_nt(_bf(dnum), _bf(c_mat)) + dden * n_row)
            dq = _nn(_bf(ds), _bf(k)) + dq_inter
            dc, dn_row = dc_scr[h], dn_scr[h, 0:1, :]
            dk_state = t["wk"] * (_nt(_bf(v), _bf(dc)) + dn_row)
            dk = _tn(_bf(ds), _bf(q)) + dk_state
            dv = _tn(_bf(t["sc"]), _bf(dnum)) + t["wk"] * _nn(_bf(k), _bf(dc))
            qi = q * t["w_inter"]
            dc_scr[h] = t["decay"] * dc + _tn(_bf(qi), _bf(dnum))
            dn_scr[h] = jnp.broadcast_to(t["decay"] * dn_row + jnp.sum(qi * dden, axis=0, keepdims=True), (8, LANE))
            dqk_ref[:, h * ML_DQK:(h + 1) * ML_DQK] = _bf(dq * (ML_DQK ** -0.5))
            dqk_ref[:, (ML_HEADS + h) * ML_DQK:(ML_HEADS + h + 1) * ML_DQK] = _bf(dk)
            dv_ref[:, h * ML_DV:(h + 1) * ML_DV] = _bf(dv)
            di_tile = di_tile + jnp.where(lane == h, jnp.sum(k * dk, axis=1, keepdims=True), 0.0)
            carried = t["decay"] * (_sum_all(dc * c_mat) + jnp.sum(dn_row * n_row, axis=1, keepdims=True))
            dlogw.append(dsc * t["sc"])
            parts = (jnp.sum(q * dq_inter, axis=1, keepdims=True) + jnp.where(last, carried, 0.0),
                     jnp.sum(k * dk_state, axis=1, keepdims=True))
            cross[1:] = [c + jnp.where(lane == ML_HEADS + h, p, 0.0) for c, p in zip(cross[1:], parts)]
        for h, col in enumerate(_crossing(jnp.concatenate(dlogw, axis=1))):
            cross[0] = cross[0] + jnp.where(lane == ML_HEADS + h, col, 0.0)
        dfb = cross[0] + _pick_left(_tri(L, True), cross[1], 2) + _pick_left(_tri(L, False) - _eye(L), cross[2], 2)
        dif_ref[:, 0:LANE] = _bf(di_tile + dfb * _sigmoid(-gif))
        dif_ref[:, LANE:SMALL_W] = jnp.zeros((L, SMALL_W - LANE), BF16)

    rev = lambda c: nc - 1 - c
    return pl.pallas_call(
        body, name="mlstm_bwd", grid=(nc,),
        in_specs=[pl.BlockSpec((L, 2048), lambda c: (rev(c), 0)), pl.BlockSpec((L, 2048), lambda c: (rev(c), O_V // 2048)),
                  pl.BlockSpec((L, LANE), lambda c: (rev(c), 0)), pl.BlockSpec((LANE, L), lambda c: (0, rev(c))),
                  pl.BlockSpec((L, 2048), lambda c: (rev(c), 0)), pl.BlockSpec((L, 2048), lambda c: (rev(c), 0)),
                  pl.BlockSpec((1, ML_HEADS, ML_DQK, ML_DV), lambda c: (rev(c), 0, 0, 0)),
                  pl.BlockSpec((1, ML_HEADS, 8, LANE), lambda c: (rev(c), 0, 0, 0)), pl.BlockSpec(memory_space=pl.ANY)],
        out_specs=[pl.BlockSpec((L, 2048), lambda c: (rev(c), 0)), pl.BlockSpec((L, 2048), lambda c: (rev(c), O_V // 2048)),
                   pl.BlockSpec((L, SMALL_W), lambda c: (rev(c), 0))],
        out_shape=[jax.ShapeDtypeStruct((s, 2048), BF16), jax.ShapeDtypeStruct(dproj.shape, dproj.dtype),
                   jax.ShapeDtypeStruct((s, SMALL_W), BF16)],
        input_output_aliases={8: 1},
        scratch_shapes=[pltpu.VMEM((ML_HEADS, ML_DQK, ML_DV), F32), pltpu.VMEM((ML_HEADS, 8, LANE), F32),
                        pltpu.VMEM((L, LANE), F32), pltpu.VMEM((LANE, L), F32)],
        compiler_params=_cparams("arbitrary"))(qk, proj, gates, gt, hout, dh, cst, nm, dproj)


GROUP_W = SSM_HEADS // SSM_GROUPS * SSM_HEADDIM
O_B = SSM_HEADS * SSM_HEADDIM
O_C = O_B + SSM_GROUPS * SSM_STATE


def _head_expand():
    r = jnp.arange(LANE)[:, None]
    c = jnp.arange(SSM_HEADS * SSM_HEADDIM)[None, :] // SSM_HEADDIM
    return (r == c).astype(F32)


def _ssd_gates(dt_ref, dtt_ref, alog_row_ref, alog_col_ref, at_scr):
    L = dt_ref.shape[0]
    dt = _softplus(dt_ref[...])
    acoef = -jnp.exp(alog_row_ref[...])
    a = _pick_left(_tri(L, False), dt * acoef, 3)
    at_scr[...] = _pick_right(_softplus(dtt_ref[...]) * (-jnp.exp(alog_col_ref[...])), _tri(L, True), 3)
    return dt, acoef, a


def _ssd_group(g, xbc_ref, dt, a, e_ref, ax_scr):
    eg = e_ref[:, g * GROUP_W:(g + 1) * GROUP_W]
    ax_scr[...] = _pick_right(a, eg, 3)
    ax = ax_scr[...]
    alx = ax_scr[ax.shape[0] - 1:ax.shape[0], :]
    dtx = _pick_right(dt, eg, 2)
    xg = xbc_ref[:, g * GROUP_W:(g + 1) * GROUP_W]
    bg = xbc_ref[:, O_B + g * SSM_STATE:O_B + (g + 1) * SSM_STATE]
    cg = xbc_ref[:, O_C + g * SSM_STATE:O_C + (g + 1) * SSM_STATE]
    return dict(ax=ax, alx=alx, dtx=dtx, xg=xg, bg=bg, cg=cg, xdt=xg * dtx, gmat=_nt(_bf(cg), _bf(bg)))


def _ssd_decay(hh, a, at_scr):
    L = a.shape[0]
    causal = lax.broadcasted_iota(jnp.int32, (L, L), 0) >= lax.broadcasted_iota(jnp.int32, (L, L), 1)
    return jnp.exp(jnp.where(causal, _lane_col(a, hh) - at_scr[hh:hh + 1, :], NEG))


def _ssd_fwd(xbc, gates, dtt, alog_row, alog_col, dskip_x, expand):
    s = xbc.shape[0]
    L = CHUNK
    nc = s // L
    half = SSM_HEADDIM

    def body(xbc_ref, dt_ref, dtt_ref, ar_ref, ac_ref, dk_ref, e_ref, y_ref, st_ref, st_scr, at_scr, ax_scr):
        @pl.when(pl.program_id(0) == 0)
        def _():
            st_scr[...] = jnp.zeros_like(st_scr)
        dt, _, a = _ssd_gates(dt_ref, dtt_ref, ar_ref, ac_ref, at_scr)
        lane = lax.broadcasted_iota(jnp.int32, (L, LANE), 1)
        for g in range(SSM_GROUPS):
            t = _ssd_group(g, xbc_ref, dt, a, e_ref, ax_scr)
            st = st_scr[g]
            st_ref[0, g] = st
            pairs = []
            for j in range(GROUP_W // LANE):
                xp = _bf(t["xdt"][:, j * LANE:(j + 1) * LANE])
                hh = g * (SSM_HEADS // SSM_GROUPS) + 2 * j
                both = jnp.concatenate([_bf(t["gmat"] * _ssd_decay(hh, a, at_scr)),
                                        _bf(t["gmat"] * _ssd_decay(hh + 1, a, at_scr))], axis=0)
                ys = _nn(both, xp)
                pairs.append(jnp.where(lane < half, ys[0:L], ys[L:2 * L]))
            y = jnp.concatenate(pairs, axis=1) + _nn(_bf(t["cg"]), _bf(st)) * jnp.exp(t["ax"])
            y_ref[:, g * GROUP_W:(g + 1) * GROUP_W] = _bf(y + dk_ref[:, g * GROUP_W:(g + 1) * GROUP_W] * t["xg"])
            wts = jnp.exp(t["alx"] - t["ax"])
            st_scr[g] = jnp.exp(t["alx"]) * st + _tn(_bf(t["bg"]), _bf(t["xdt"] * wts))

    row = lambda w: pl.BlockSpec((1, w), lambda c: (0, 0))
    return pl.pallas_call(
        body, name="ssd_fwd", grid=(nc,),
        in_specs=[pl.BlockSpec((L, 3072), lambda c: (c, 0)), pl.BlockSpec((L, LANE), lambda c: (c, (O_DT - O_IF) // LANE)),
                  pl.BlockSpec((LANE, L), lambda c: (0, c)), row(LANE), pl.BlockSpec((LANE, 1), lambda c: (0, 0)),
                  row(2048), pl.BlockSpec((LANE, 2048), lambda c: (0, 0))],
        out_specs=[pl.BlockSpec((L, 2048), lambda c: (c, 0)),
                   pl.BlockSpec((1, SSM_GROUPS, SSM_STATE, GROUP_W), lambda c: (c, 0, 0, 0))],
        out_shape=[jax.ShapeDtypeStruct((s, 2048), BF16),
                   jax.ShapeDtypeStruct((nc, SSM_GROUPS, SSM_STATE, GROUP_W), F32)],
        scratch_shapes=[pltpu.VMEM((SSM_GROUPS, SSM_STATE, GROUP_W), F32), pltpu.VMEM((LANE, L), F32),
                        pltpu.VMEM((L, GROUP_W), F32)],
        compiler_params=_cparams("arbitrary"))(xbc, gates, dtt, alog_row, alog_col, dskip_x, expand)


def _ssd_bwd(xbc, gates, dtt, alog_row, alog_col, dskip_x, expand, expand_t, dy, states):
    s = xbc.shape[0]
    L = CHUNK
    nc = s // L
    half = SSM_HEADDIM

    def body(xbc_ref, dt_ref, dtt_ref, ar_ref, ac_ref, dk_ref, e_ref, et_ref, dy_ref, st_ref,
             dxbc_ref, ddt_ref, accd_ref, acca_ref, dst_scr, at_scr, ax_scr):
        @pl.when(pl.program_id(0) == 0)
        def _():
            dst_scr[...] = jnp.zeros_like(dst_scr)
            accd_ref[...] = jnp.zeros_like(accd_ref)
            acca_ref[...] = jnp.zeros_like(acca_ref)
        dt, acoef, a = _ssd_gates(dt_ref, dtt_ref, ar_ref, ac_ref, at_scr)
        lane = lax.broadcasted_iota(jnp.int32, (L, LANE), 1)
        low = lane < half
        last = lax.broadcasted_iota(jnp.int32, (L, 1), 0) == L - 1
        cross = [jnp.zeros((L, LANE), F32)] * 3
        ddt_tile = jnp.zeros((L, LANE), F32)
        for g in range(SSM_GROUPS):
            t = _ssd_group(g, xbc_ref, dt, a, e_ref, ax_scr)
            xg, bg, cg, xdt, gmat = t["xg"], t["bg"], t["cg"], t["xdt"], t["gmat"]
            st, dst = st_ref[0, g], dst_scr[g]
            dyg = dy_ref[:, g * GROUP_W:(g + 1) * GROUP_W].astype(F32)
            ea, eal = jnp.exp(t["ax"]), jnp.exp(t["alx"])
            wts = jnp.exp(t["alx"] - t["ax"])
            dyi = dyg * ea
            y_inter = _nn(_bf(cg), _bf(st)) * ea
            dc = _nt(_bf(dyi), _bf(st))
            d_xdt_state = _nn(_bf(bg), _bf(dst)) * wts
            db = _nt(_bf(xdt * wts), _bf(dst))
            dst_scr[g] = eal * dst + _tn(_bf(cg), _bf(dyi))
            dg = jnp.zeros((L, L), F32)
            dx_pairs, dlogw = [], []
            for j in range(GROUP_W // LANE):
                xp = _bf(xdt[:, j * LANE:(j + 1) * LANE])
                dyp = dyg[:, j * LANE:(j + 1) * LANE]
                hh = g * (SSM_HEADS // SSM_GROUPS) + 2 * j
                decs = [_ssd_decay(hh, a, at_scr), _ssd_decay(hh + 1, a, at_scr)]
                ws = [gmat * decs[0], gmat * decs[1]]
                dxs = _tn(_bf(jnp.concatenate(ws, axis=1)), _bf(dyp))
                dws = _nt(_bf(jnp.concatenate([jnp.where(low, dyp, 0.0), jnp.where(low, 0.0, dyp)], axis=0)), xp)
                dw0, dw1 = dws[0:L], dws[L:2 * L]
                dg = dg + dw0 * decs[0] + dw1 * decs[1]
                dlogw += [dw0 * ws[0], dw1 * ws[1]]
                dx_pairs.append(jnp.where(low, dxs[0:L], dxs[L:2 * L]))
            for b, col in enumerate(_crossing(jnp.concatenate(dlogw, axis=1))):
                cross[0] = cross[0] + jnp.where(lane == g * (SSM_HEADS // SSM_GROUPS) + b, col, 0.0)
            d_xdt = d_xdt_state + jnp.concatenate(dx_pairs, axis=1)
            dc = dc + _nn(_bf(dg), _bf(bg))
            db = db + _tn(_bf(dg), _bf(cg))
            etg = et_ref[g * GROUP_W:(g + 1) * GROUP_W, :]
            carried = jnp.sum(dst * st, axis=0, keepdims=True) * eal
            cross[1] = cross[1] + _pick_right(dyg * y_inter + jnp.where(last, carried, 0.0), etg, 2)
            cross[2] = cross[2] + _pick_right(xdt * d_xdt_state, etg, 2)
            ddt_tile = ddt_tile + _pick_right(d_xdt * xg, etg, 2)
            dxbc_ref[:, g * GROUP_W:(g + 1) * GROUP_W] = _bf(d_xdt * t["dtx"] + dk_ref[:, g * GROUP_W:(g + 1) * GROUP_W] * dyg)
            dxbc_ref[:, O_B + g * SSM_STATE:O_B + (g + 1) * SSM_STATE] = _bf(db)
            dxbc_ref[:, O_C + g * SSM_STATE:O_C + (g + 1) * SSM_STATE] = _bf(dc)
            accd_ref[0:1, g * GROUP_W:(g + 1) * GROUP_W] += jnp.sum(dyg * xg, axis=0, keepdims=True)
        d_da = cross[0] + _pick_left(_tri(L, True), cross[1], 2) + _pick_left(_tri(L, False) - _eye(L), cross[2], 2)
        acca_ref[0:1, :] += jnp.sum(d_da * dt, axis=0, keepdims=True)
        ddt_ref[:, 0:LANE] = _bf((ddt_tile + d_da * acoef) * _sigmoid(dt_ref[...]))
        ddt_ref[:, LANE:SMALL_W] = jnp.zeros((L, SMALL_W - LANE), BF16)

    rev = lambda c: nc - 1 - c
    row = lambda w: pl.BlockSpec((1, w), lambda c: (0, 0))
    return pl.pallas_call(
        body, name="ssd_bwd", grid=(nc,),
        in_specs=[pl.BlockSpec((L, 3072), lambda c: (rev(c), 0)), pl.BlockSpec((L, LANE), lambda c: (rev(c), (O_DT - O_IF) // LANE)),
                  pl.BlockSpec((LANE, L), lambda c: (0, rev(c))), row(LANE), pl.BlockSpec((LANE, 1), lambda c: (0, 0)),
                  row(2048), pl.BlockSpec((LANE, 2048), lambda c: (0, 0)), pl.BlockSpec((2048, LANE), lambda c: (0, 0)),
                  pl.BlockSpec((L, 2048), lambda c: (rev(c), 0)),
                  pl.BlockSpec((1, SSM_GROUPS, SSM_STATE, GROUP_W), lambda c: (rev(c), 0, 0, 0))],
        out_specs=[pl.BlockSpec((L, 3072), lambda c: (rev(c), 0)), pl.BlockSpec((L, SMALL_W), lambda c: (rev(c), 0)),
                   pl.BlockSpec((8, 2048), lambda c: (0, 0)), pl.BlockSpec((8, LANE), lambda c: (0, 0))],
        out_shape=[jax.ShapeDtypeStruct((s, 3072), BF16), jax.ShapeDtypeStruct((s, SMALL_W), BF16),
                   jax.ShapeDtypeStruct((8, 2048), F32), jax.ShapeDtypeStruct((8, LANE), F32)],
        scratch_shapes=[pltpu.VMEM((SSM_GROUPS, SSM_STATE, GROUP_W), F32),
                        pltpu.VMEM((LANE, L), F32), pltpu.VMEM((L, GROUP_W), F32)],
        compiler_params=_cparams("arbitrary"))(xbc, gates, dtt, alog_row, alog_col, dskip_x, expand, expand_t, dy, states)


def _group_norm(v, width):
    outs, rs = [], []
    for k in range(v.shape[1] // width):
        blk = v[:, k * width:(k + 1) * width]
        r = lax.rsqrt(jnp.mean(blk * blk, axis=1, keepdims=True) + EPS)
        outs.append(blk * r)
        rs.append(jnp.broadcast_to(r, blk.shape))
    return jnp.concatenate(outs, axis=1), jnp.concatenate(rs, axis=1)


def _group_mean(v, width):
    return jnp.concatenate([jnp.broadcast_to(jnp.mean(v[:, k * width:(k + 1) * width], axis=1, keepdims=True),
                                             (v.shape[0], width)) for k in range(v.shape[1] // width)], axis=1)


def _post_fwd(hm, yssd, proj, ml_norm_w, ssm_norm_w, ts):
    s = hm.shape[0]

    def body(h_ref, ys_ref, o_ref, zm_ref, zs_ref, wm_ref, ws_ref, ym_ref, yso_ref):
        hn, _ = _group_norm(h_ref[...].astype(F32), ML_DV)
        ym_ref[...] = _bf(_sigmoid(o_ref[...].astype(F32)) * hn * wm_ref[...] * _silu(zm_ref[...].astype(F32)))
        pn, _ = _group_norm(ys_ref[...].astype(F32) * _silu(zs_ref[...].astype(F32)), GROUP_W)
        yso_ref[...] = _bf(pn * ws_ref[...])

    tile = pl.BlockSpec((ts, 2048), lambda i: (i, 0))
    col = lambda off: pl.BlockSpec((ts, 2048), lambda i: (i, off // 2048))
    row = pl.BlockSpec((1, 2048), lambda i: (0, 0))
    return pl.pallas_call(
        body, name="post_fwd", grid=(s // ts,),
        in_specs=[tile, tile, col(O_O), col(O_ZM), col(O_ZS), row, row],
        out_specs=[tile, tile],
        out_shape=[jax.ShapeDtypeStruct((s, 2048), BF16)] * 2,
        compiler_params=_cparams("parallel"))(hm, yssd, proj, proj, proj, ml_norm_w, ssm_norm_w)


def _post_bwd(dym, dys, hm, yssd, proj, ml_norm_w, ssm_norm_w, dproj, ts):
    s = hm.shape[0]

    def body(dym_ref, dys_ref, h_ref, ys_ref, o_ref, zm_ref, zs_ref, wm_ref, ws_ref, _,
             dh_ref, dyssd_ref, dp_ref, acc_ref):
        @pl.when(pl.program_id(0) == 0)
        def _():
            acc_ref[...] = jnp.zeros_like(acc_ref)
        hn, r = _group_norm(h_ref[...].astype(F32), ML_DV)
        so, zm, wm, d_ym = _sigmoid(o_ref[...].astype(F32)), zm_ref[...].astype(F32), wm_ref[...], dym_ref[...].astype(F32)
        sz = _silu(zm)
        hnw = hn * wm
        dp_ref[:, O_O:O_O + 2048] = _bf(d_ym * hnw * sz * so * (1.0 - so))
        dp_ref[:, O_ZM:O_ZM + 2048] = _bf(d_ym * so * hnw * _dsilu(zm))
        dhnw = d_ym * so * sz
        acc_ref[0:1, :] += jnp.sum(dhnw * hn, axis=0, keepdims=True)
        dhn = dhnw * wm
        dh_ref[...] = _bf(r * (dhn - hn * _group_mean(dhn * hn, ML_DV)))
        ysv, zs, d_ys = ys_ref[...].astype(F32), zs_ref[...].astype(F32), dys_ref[...].astype(F32)
        szs = _silu(zs)
        pn, r2 = _group_norm(ysv * szs, GROUP_W)
        acc_ref[1:2, :] += jnp.sum(d_ys * pn, axis=0, keepdims=True)
        dpn = d_ys * ws_ref[...]
        dp = r2 * (dpn - pn * _group_mean(dpn * pn, GROUP_W))
        dyssd_ref[...] = _bf(dp * szs)
        dp_ref[:, O_ZS:O_ZS + 2048] = _bf(dp * ysv * _dsilu(zs))

    tile = pl.BlockSpec((ts, 2048), lambda i: (i, 0))
    col = lambda off: pl.BlockSpec((ts, 2048), lambda i: (i, off // 2048))
    row = pl.BlockSpec((1, 2048), lambda i: (0, 0))
    sds = lambda dt: jax.ShapeDtypeStruct((s, 2048), dt)
    return pl.pallas_call(
        body, name="post_bwd", grid=(s // ts,),
        in_specs=[tile, tile, tile, tile, col(O_O), col(O_ZM), col(O_ZS), row, row, pl.BlockSpec(memory_space=pl.ANY)],
        out_specs=[tile, tile, pl.BlockSpec((ts, O_MG), lambda i: (i, 0)), pl.BlockSpec((8, 2048), lambda i: (0, 0))],
        out_shape=[sds(BF16), sds(BF16), jax.ShapeDtypeStruct(dproj.shape, dproj.dtype), jax.ShapeDtypeStruct((8, 2048), F32)],
        input_output_aliases={9: 2},
        compiler_params=_cparams("arbitrary"))(dym, dys, hm, yssd, proj, proj, proj, ml_norm_w, ssm_norm_w, dproj)


def _merge(x, ym, ys, proj, target, gate, final_w, wpm, wps, wo, ts):
    wpm_t, wps_t, wo_t = wpm.T, wps.T, wo.T
    s, d = x.shape

    def body(x_ref, ym_ref, ys_ref, mg_ref, t_ref, gate_ref, fw_ref, wpm_ref, wps_ref, wo_ref, wpmt_ref, wpst_ref, wot_ref,
             dres_ref, mer_ref, dmo_ref, dpm_ref, dps_ref, dym_ref, dys_ref, dmg_ref, acc_ref):
        @pl.when(pl.program_id(0) == 0)
        def _():
            acc_ref[...] = jnp.zeros_like(acc_ref)
        gm, gs = _sigmoid(mg_ref[:, 0:d].astype(F32)), _sigmoid(mg_ref[:, d:2 * d].astype(F32))
        pm = _nn(ym_ref[...], wpm_ref[...])
        ps = _nn(ys_ref[...], wps_ref[...])
        merged = _bf(gm * pm + gs * ps)
        mer_ref[...] = merged
        mo = _nn(merged, wo_ref[...])
        gate, fw = gate_ref[...], fw_ref[...]
        out = x_ref[...] + gate * mo
        r = lax.rsqrt(jnp.mean(out * out, axis=1, keepdims=True) + EPS)
        on = out * r
        diff = on * fw - t_ref[...]
        acc_ref[0:1, :] += jnp.sum(0.5 * jnp.sum(diff * diff, axis=1, keepdims=True) / d, axis=0, keepdims=True)
        dyv = diff * (1.0 / d)
        acc_ref[1:2, :] += jnp.sum(dyv * on, axis=0, keepdims=True)
        don = dyv * fw
        dout = r * (don - on * jnp.mean(don * on, axis=1, keepdims=True))
        dres_ref[...] = dout
        acc_ref[2:3, :] += jnp.sum(dout * mo, axis=0, keepdims=True)
        dmo = _bf(dout * gate)
        dmo_ref[...] = dmo
        dmer = _nn(dmo, wot_ref[...])
        dpm, dps = _bf(dmer * gm), _bf(dmer * gs)
        dpm_ref[...] = dpm
        dps_ref[...] = dps
        dmg_ref[:, 0:d] = _bf(dmer * pm * gm * (1.0 - gm))
        dmg_ref[:, d:2 * d] = _bf(dmer * ps * gs * (1.0 - gs))
        dym_ref[...] = _bf(_nn(dpm, wpmt_ref[...]))
        dys_ref[...] = _bf(_nn(dps, wpst_ref[...]))

    t1 = pl.BlockSpec((ts, d), lambda i: (i, 0))
    t2 = pl.BlockSpec((ts, 2 * d), lambda i: (i, 0))
    row = pl.BlockSpec((1, d), lambda i: (0, 0))
    whole = pl.BlockSpec(memory_space=pltpu.VMEM)
    sd = lambda w, dt: jax.ShapeDtypeStruct((s, w), dt)
    return pl.pallas_call(
        body, name="merge_fwd_bwd", grid=(s // ts,),
        in_specs=[t1, t2, t2, pl.BlockSpec((ts, 2 * d), lambda i: (i, O_MG // (2 * d))), t1, row, row] + [whole] * 6,
        out_specs=[t1, t1, t1, t1, t1, t2, t2, pl.BlockSpec((ts, 2 * d), lambda i: (i, O_MG // (2 * d))),
                   pl.BlockSpec((8, d), lambda i: (0, 0))],
        out_shape=[sd(d, F32), sd(d, BF16), sd(d, BF16), sd(d, BF16), sd(d, BF16), sd(2 * d, BF16), sd(2 * d, BF16),
                   sd(NP, BF16), jax.ShapeDtypeStruct((8, d), F32)],
        compiler_params=_cparams("arbitrary", vmem=MERGE_VMEM))(x, ym, ys, proj, target, gate, final_w, wpm, wps, wo, wpm_t, wps_t, wo_t)


def _adamw(w, g, m, v, tr):
    if w.ndim == 2 and w.shape[0] % 8:
        tile, steps = pl.BlockSpec((w.shape[0], tr), lambda i: (0, i)), w.shape[1] // tr
    else:
        lead = (None,) * (w.ndim - 2)
        tile, steps = pl.BlockSpec(lead + (tr, w.shape[-1]), lambda i: (0,) * len(lead) + (i, 0)), w.shape[-2] // tr

    def body(w_ref, g_ref, m_ref, v_ref, d_ref, nm_ref, nv_ref):
        gv = g_ref[...]
        m2 = ADAM_B1 * m_ref[...] + (1.0 - ADAM_B1) * gv
        v2 = ADAM_B2 * v_ref[...] + (1.0 - ADAM_B2) * (gv * gv)
        m_hat = m2 / (1.0 - ADAM_B1 ** ADAM_STEP)
        v_hat = v2 / (1.0 - ADAM_B2 ** ADAM_STEP)
        d_ref[...] = -ADAM_LR * (m_hat / (jnp.sqrt(v_hat) + ADAM_EPS) + ADAM_WD * w_ref[...])
        nm_ref[...] = m2
        nv_ref[...] = v2

    return pl.pallas_call(
        body, name="adamw", grid=(steps,), in_specs=[tile] * 4, out_specs=[tile] * 3,
        out_shape=[jax.ShapeDtypeStruct(w.shape, F32)] * 3,
        compiler_params=_cparams("parallel"))(w, g.reshape(w.shape), m, v)


def _sum_parts(own, parts, tr, dtype=F32, slot=None):
    p, rows, cols = parts.shape

    def body(*refs):
        p_ref, o_ref = refs[-2], refs[-1]
        first = None if own is None else refs[-3]
        acc = p_ref[0].astype(F32) if first is None else first[...].astype(F32) + p_ref[0].astype(F32)
        for i in range(1, p):
            acc = acc + p_ref[i].astype(F32)
        o_ref[...] = acc.astype(dtype)

    args = ([] if own is None else [own]) + [parts]
    if slot is None:
        tile = pl.BlockSpec((tr, cols), lambda i: (i, 0))
        ins = ([] if own is None else [tile]) + [pl.BlockSpec((p, tr, cols), lambda i: (0, i, 0))]
        return pl.pallas_call(
            body, name="sum_parts", grid=(rows // tr,), in_specs=ins, out_specs=tile,
            out_shape=jax.ShapeDtypeStruct((rows, cols), dtype), compiler_params=_cparams("parallel"))(*args)
    tile = pl.BlockSpec((tr, cols), lambda i, s: (i, 0))
    ins = ([] if own is None else [tile]) + [pl.BlockSpec((p, tr, cols), lambda i, s: (0, i, 0))]
    return pl.pallas_call(
        body, name="sum_parts_half", out_shape=jax.ShapeDtypeStruct((2, rows, cols), dtype),
        grid_spec=pltpu.PrefetchScalarGridSpec(
            num_scalar_prefetch=1, grid=(rows // tr,), in_specs=ins,
            out_specs=pl.BlockSpec((None, tr, cols), lambda i, s: (s[0], i, 0))),
        compiler_params=_cparams("parallel"))(jnp.reshape(slot, (1,)).astype(jnp.int32), *args)


def _position():
    return lax.axis_index("x"), lax.axis_index("y"), lax.axis_index("c")


def _flip(pos, k):
    return tuple(1 - p if (k >> s) & 1 else p for p, s in zip(pos, (2, 1, 0)))


def _allgather8(block):
    rows, cols = block.shape

    def body(x_ref, o_ref, send_sems, recv_sems, local_sem):
        pos = _position()
        me = 4 * pos[0] + 2 * pos[1] + pos[2]
        mine = pltpu.make_async_copy(x_ref, o_ref.at[me], local_sem)
        mine.start()
        copies = [pltpu.make_async_remote_copy(src_ref=x_ref, dst_ref=o_ref.at[me], send_sem=send_sems.at[k - 1],
                                               recv_sem=recv_sems.at[k - 1], device_id=_flip(pos, k), device_id_type=MESH)
                  for k in range(1, N_DEV)]
        for cp in copies:
            cp.start()
        for cp in copies:
            cp.wait()
        mine.wait()

    vmem = pl.BlockSpec(memory_space=pltpu.VMEM)
    return pl.pallas_call(
        body, name="allgather8", in_specs=[vmem], out_specs=vmem,
        out_shape=jax.ShapeDtypeStruct((N_DEV, rows, cols), block.dtype),
        scratch_shapes=[pltpu.SemaphoreType.DMA((N_DEV - 1,)), pltpu.SemaphoreType.DMA((N_DEV - 1,)),
                        pltpu.SemaphoreType.DMA],
        compiler_params=pltpu.CompilerParams(vmem_limit_bytes=VMEM_LIMIT))(block)


COPY_BYTES = 1 << 20


def _row_chunks(rows, row_bytes):
    n = max(1, min(rows // 16, -(-rows * row_bytes // COPY_BYTES)))
    while rows % (16 * n):
        n -= 1
    return [(i * (rows // n), rows // n) for i in range(n)]


def _split_start(name, make_copies, n_copies, sources, lands):
    n, m = len(sources), len(lands)

    def body(*refs):
        for cp in make_copies(_position(), refs[:n], refs[n:n + m], refs[n + m], refs[n + m + 1]):
            cp.start()
        refs[-1][...] = jnp.zeros((8, LANE), F32)

    hbm = pl.BlockSpec(memory_space=pltpu.HBM)
    sem = pl.BlockSpec(memory_space=pltpu.SEMAPHORE)
    operands = [pltpu.with_memory_space_constraint(t, pltpu.HBM) for t in list(sources) + list(lands)]
    out = pl.pallas_call(
        body, name=name, in_specs=[hbm] * (n + m),
        out_specs=[sem, sem] + [hbm] * (n + m) + [pl.BlockSpec(memory_space=pltpu.VMEM)],
        out_shape=[pltpu.SemaphoreType.DMA((n_copies,)), pltpu.SemaphoreType.DMA((n_copies,))]
        + [pltpu.HBM(t.shape, t.dtype) for t in operands] + [jax.ShapeDtypeStruct((8, LANE), F32)],
        input_output_aliases={i: 2 + i for i in range(n + m)},
        compiler_params=pltpu.CompilerParams(has_side_effects=pltpu.SideEffectType.DATAFLOW_SIDE_EFFECTING))(*operands)
    return out[0], out[1], out[2:2 + n], out[2 + n:2 + n + m], out[-1]


def _split_wait(name, make_copies, send_sems, recv_sems, sources, lands, after):
    n, m = len(sources), len(lands)

    def body(*refs):
        for cp in make_copies(_position(), refs[:n], refs[n:n + m], refs[n + m], refs[n + m + 1]):
            cp.wait_send()
            cp.wait_recv()

    hbm = pl.BlockSpec(memory_space=pltpu.HBM)
    sem = pl.BlockSpec(memory_space=pltpu.SEMAPHORE)
    out = pl.pallas_call(
        body, name=name, in_specs=[hbm] * (n + m) + [sem, sem, pl.BlockSpec(memory_space=pl.ANY)],
        out_specs=[hbm] * (n + m), out_shape=[pltpu.HBM(t.shape, t.dtype) for t in list(sources) + list(lands)],
        input_output_aliases={i: i for i in range(n + m)},
        compiler_params=pltpu.CompilerParams(has_side_effects=pltpu.SideEffectType.DATAFLOW_SIDE_EFFECTING))(
            *sources, *lands, send_sems, recv_sems, after)
    return out[:n], out[n:]


def _gather_copies(pos, halves, lands, send_sems, recv_sems):
    chip, core = 2 * pos[0] + pos[1], pos[2]
    copies = []
    for a in range(len(halves)):
        for k in range(1, N_CHIPS):
            for r0, nr in _row_chunks(halves[a].shape[1], halves[a].shape[2] * halves[a].dtype.itemsize):
                i = len(copies)
                copies.append(pltpu.make_async_remote_copy(
                    src_ref=halves[a].at[core, pl.ds(r0, nr)], dst_ref=lands[a].at[chip, core, pl.ds(r0, nr)],
                    send_sem=send_sems.at[i], recv_sem=recv_sems.at[i], device_id=_flip(pos, 2 * k), device_id_type=MESH))
    return copies


def _gather_pieces(halves):
    return (N_CHIPS - 1) * sum(len(_row_chunks(a.shape[1], a.shape[2] * a.dtype.itemsize)) for a in halves)


def _pair_forward(lands):
    n = len(lands)

    def plan(pos, ins, outs):
        remote = []
        for a in range(n):
            for k in range(1, N_CHIPS):
                there = _flip(pos, 2 * k)
                for r0, nr in _row_chunks(lands[a].shape[2], lands[a].shape[3] * lands[a].dtype.itemsize):
                    slot = (2 * there[0] + there[1], pos[2], pl.ds(r0, nr))
                    remote.append((ins[a].at[slot], outs[a].at[slot], _flip(pos, 1)))
        return remote, []

    return _exchange("pair_forward", lands, [jax.ShapeDtypeStruct(t.shape, t.dtype) for t in lands], plan,
                     _gather_pieces([jax.ShapeDtypeStruct(t.shape[1:], t.dtype) for t in lands]), 0, in_place=True)


def _exchange(name, arrays, out_shapes, plan, n_remote, n_local, in_place=False):
    n, m = len(arrays), len(out_shapes)

    def body(*refs):
        send_sems, recv_sems, local_sems = refs[n + m:]
        remote, local = plan(_position(), refs[:n], refs[n:n + m])
        assert (len(remote), len(local)) == (n_remote, n_local)
        copies = [pltpu.make_async_copy(src, dst, local_sems.at[i]) for i, (src, dst) in enumerate(local)]
        copies += [pltpu.make_async_remote_copy(src_ref=src, dst_ref=dst, send_sem=send_sems.at[i], recv_sem=recv_sems.at[i],
                                                device_id=dev, device_id_type=MESH)
                   for i, (src, dst, dev) in enumerate(remote)]
        for cp in copies:
            cp.start()
        for cp in copies:
            cp.wait()

    hbm = pl.BlockSpec(memory_space=pl.ANY)
    return pl.pallas_call(
        body, name=name, in_specs=[hbm] * n, out_specs=[hbm] * m, out_shape=out_shapes,
        input_output_aliases={i: i for i in range(n)} if in_place else {},
        scratch_shapes=[pltpu.SemaphoreType.DMA((n_remote,)), pltpu.SemaphoreType.DMA((n_remote,)),
                        pltpu.SemaphoreType.DMA((max(n_local, 1),))],
        compiler_params=pltpu.CompilerParams(has_side_effects=True))(*arrays)


def _pair_send(slabs):
    n = len(slabs)
    pieces = [_row_chunks(g.shape[2], g.shape[3] * g.dtype.itemsize) for g in slabs]

    def plan(pos, ins, outs):
        return [(ins[a].at[j, 1 - pos[2], pl.ds(r0, nr)], outs[a].at[j, pl.ds(r0, nr)], _flip(pos, 1))
                for a in range(n) for j in range(N_CHIPS) for r0, nr in pieces[a]], []

    return _exchange("pair_send", slabs, [jax.ShapeDtypeStruct((N_CHIPS,) + g.shape[2:], g.dtype) for g in slabs], plan,
                     N_CHIPS * sum(len(p) for p in pieces), 0)


def _chip_scatter_copies(pos, sums, lands, send_sems, recv_sems):
    copies = []
    for a in range(len(sums)):
        for k in range(1, N_CHIPS):
            to = _flip(pos, 2 * k)
            for r0, nr in _row_chunks(sums[a].shape[1], sums[a].shape[2] * sums[a].dtype.itemsize):
                i = len(copies)
                copies.append(pltpu.make_async_remote_copy(
                    src_ref=sums[a].at[2 * to[0] + to[1], pl.ds(r0, nr)], dst_ref=lands[a].at[k - 1, pl.ds(r0, nr)],
                    send_sem=send_sems.at[i], recv_sem=recv_sems.at[i], device_id=to, device_id_type=MESH))
    return copies


def _chip_scatter_start(sums):
    n_copies = (N_CHIPS - 1) * sum(len(_row_chunks(g.shape[1], g.shape[2] * g.dtype.itemsize)) for g in sums)
    lands = [lax.empty((N_CHIPS - 1,) + g.shape[1:], g.dtype) for g in sums]
    return _split_start("chip_scatter_start", _chip_scatter_copies, n_copies, sums, lands)


def _chip_scatter_wait(send_sems, recv_sems, sums, lands, after):
    return _split_wait("chip_scatter_wait", _chip_scatter_copies, send_sems, recv_sems, sums, lands, after)


def _pair_exchange(pairs):
    n = len(pairs)
    pieces = [_row_chunks(h.shape[1], h.shape[2] * h.dtype.itemsize) for h in pairs]

    def plan(pos, ins, outs):
        return [(ins[a].at[pos[2], pl.ds(r0, nr)], outs[a].at[pos[2], pl.ds(r0, nr)], _flip(pos, 1))
                for a in range(n) for r0, nr in pieces[a]], []

    return _exchange("pair_exchange", pairs, [jax.ShapeDtypeStruct(h.shape, h.dtype) for h in pairs], plan,
                     sum(len(p) for p in pieces), 0, in_place=True)


def _pack(arrays):
    flat = jnp.concatenate([a.reshape(-1).astype(F32) for a in arrays])
    size = -(-flat.shape[0] // (8 * LANE)) * (8 * LANE)
    return jnp.pad(flat, (0, size - flat.shape[0])).reshape(size // LANE, LANE)


def _unpack(buf, shapes):
    flat = buf.reshape(-1)
    out, off = [], 0
    for shp in shapes:
        n = math.prod(shp)
        out.append(flat[off:off + n].reshape(shp))
        off += n
    return out


def _unpack_rows(bufs, shapes):
    flat = bufs.reshape(bufs.shape[0], -1)
    out, off = [], 0
    for shp in shapes:
        n = math.prod(shp)
        out.append(flat[:, off:off + n].reshape((bufs.shape[0],) + shp))
        off += n
    return out


def _taps8(w):
    return jnp.pad(w, ((0, 8 - CONV_K), (0, 0)))


def _local_step(xs, tgt, scale, shift, gate, norm_w, w_in_p, b_in_p, ml_conv_w, ml_conv_b, ml_norm_w, ssm_conv_w,
                ssm_conv_b, ssm_a_log, ssm_d, ssm_norm_w, wpm, wps, wo, final_w, start_exchange=None, late_weights=None,
                u=None):
    s = xs.shape[0]
    ts = min(512, s)
    tm = min(2048, s)
    if u is None:
        u = _prenorm_fwd(xs, norm_w, scale, shift, ts)
    proj = _matmul_bias(u, w_in_p, b_in_p, tm, 512, 0, O_IF, BF16)
    gates = _matmul_bias(u, w_in_p, b_in_p, tm, 512, O_IF, NP - O_IF, F32)
    mlw8, ssw8 = _taps8(ml_conv_w), _taps8(ssm_conv_w)
    qk, qk_dact = _conv_fwd(proj, O_QK, 2048, mlw8, ml_conv_b, ts)
    xbc, xbc_dact = _conv_fwd(proj, O_XBC, 3072, ssw8, ssm_conv_b, ts)
    gt = gates[:, :LANE].T
    dtt = gates[:, O_DT - O_IF:O_DT - O_IF + LANE].T
    hm, cst, nm = _mlstm_fwd(qk, proj, gates, gt)
    alog_row = jnp.pad(ssm_a_log, ((0, 0), (0, LANE - SSM_HEADS)))
    alog_col = alog_row.reshape(LANE, 1)
    dskip_x = jnp.repeat(ssm_d[0], SSM_HEADDIM)[None]
    expand = _head_expand()
    yssd, sst = _ssd_fwd(xbc, gates, dtt, alog_row, alog_col, dskip_x, expand)
    tp = min(128, s)
    ym, ys = _post_fwd(hm, yssd, proj, ml_norm_w, ssm_norm_w, tp)
    if late_weights is not None:
        wpm, wps, wo = late_weights(ym)
    dxres, merged, dmo, dpm, dps, dym, dys, dproj, acc_m = _merge(xs, ym, ys, proj, tgt, gate, final_w, wpm, wps, wo,
                                                                  min(256, s))
    dh, dyssd, dproj, acc_p = _post_bwd(dym, dys, hm, yssd, proj, ml_norm_w, ssm_norm_w, dproj, tp)
    dqk, dproj, dif = _mlstm_bwd(qk, proj, gates, gt, hm, dh, cst, nm, dproj)
    dxbc, ddt, accd, acca = _ssd_bwd(xbc, gates, dtt, alog_row, alog_col, dskip_x, expand, expand.T, dyssd, sst)
    dproj, acc_cq = _conv_bwd(proj, O_QK, 2048, mlw8, qk_dact, dqk, dproj, ts)
    dproj, acc_cx = _conv_bwd(proj, O_XBC, 3072, ssw8, xbc_dact, dxbc, dproj, ts)
    dproj = dproj.at[:, O_IF:O_IF + SMALL_W].set(dif).at[:, O_DT:O_DT + SMALL_W].set(ddt)
    gw_in_p, gb_in_p = _matmul_tn(u.T, dproj, min(4096, s), 512, with_colsum=True, a_is_transposed=True, dtype=BF16)
    g_wpm = _matmul_tn(ym, dpm, tm, 512)
    g_wps = _matmul_tn(ys, dps, tm, 512)
    g_wo = _matmul_tn(merged, dmo, tm, 512)
    token, in_flight = (None, None) if start_exchange is None else start_exchange(gw_in_p, g_wpm, g_wps, g_wo)
    du = _matmul_nt(dproj, w_in_p, tm, 512, after=token)
    grad_x, acc_n = _prenorm_bwd(du, xs, dxres, norm_w, scale, ts)
    a_coef = -jnp.exp(ssm_a_log[0])
    small = dict(
        mod=jnp.concatenate([acc_n[2], acc_n[1], acc_m[2]]), norm_w=acc_n[0], b_in=_unpad_cols(gb_in_p[0]),
        ml_conv_w=acc_cq[0:CONV_K], ml_conv_b=acc_cq[CONV_K], ml_norm_w=acc_p[0], ssm_conv_w=acc_cx[0:CONV_K],
        ssm_conv_b=acc_cx[CONV_K], ssm_a_log=acca[0, :SSM_HEADS] * a_coef,
        ssm_d=accd[0].reshape(SSM_HEADS, SSM_HEADDIM).sum(axis=1), ssm_norm_w=acc_p[1], final_w=acc_m[1], loss=acc_m[0, 0:1])
    return grad_x, small, gw_in_p, g_wpm, g_wps, g_wo, in_flight


WEIGHTS = ("norm_w", "ada_w", "ada_b", "w_in", "b_in", "ml_conv_w", "ml_conv_b", "ml_norm_w", "ssm_conv_w", "ssm_conv_b",
           "ssm_a_log", "ssm_d", "ssm_norm_w", "w_proj_m", "w_proj_s", "w_out", "final_w")
LARGE = ("ada_w", "w_in", "w_proj_m", "w_proj_s", "w_out")
SMALL_SUMS = (("mod", (3 * D_MODEL,)), ("norm_w", (D_MODEL,)), ("b_in", (IN_WIDTH,)), ("ml_conv_w", (CONV_K, 2048)),
              ("ml_conv_b", (2048,)), ("ml_norm_w", (2048,)), ("ssm_conv_w", (CONV_K, 3072)), ("ssm_conv_b", (3072,)),
              ("ssm_a_log", (SSM_HEADS,)), ("ssm_d", (SSM_HEADS,)), ("ssm_norm_w", (2048,)), ("final_w", (D_MODEL,)),
              ("loss", (1,)))


def kernel(x, c, norm_w, ada_w, ada_b, w_in, b_in, ml_conv_w, ml_conv_b, ml_norm_w, ssm_conv_w, ssm_conv_b, ssm_a_log, ssm_d, ssm_norm_w, w_proj_m, w_proj_s, w_out, final_w, loss_target, m_norm_w, m_ada_w, m_ada_b, m_w_in, m_b_in, m_ml_conv_w, m_ml_conv_b, m_ml_norm_w, m_ssm_conv_w, m_ssm_conv_b, m_ssm_a_log, m_ssm_d, m_ssm_norm_w, m_w_proj_m, m_w_proj_s, m_w_out, m_final_w, v_norm_w, v_ada_w, v_ada_b, v_w_in, v_b_in, v_ml_conv_w, v_ml_conv_b, v_ml_norm_w, v_ssm_conv_w, v_ssm_conv_b, v_ssm_a_log, v_ssm_d, v_ssm_norm_w, v_w_proj_m, v_w_proj_s, v_w_out, v_final_w):
    w = dict(norm_w=norm_w, ada_w=ada_w, ada_b=ada_b, w_in=w_in, b_in=b_in, ml_conv_w=ml_conv_w, ml_conv_b=ml_conv_b,
             ml_norm_w=ml_norm_w, ssm_conv_w=ssm_conv_w, ssm_conv_b=ssm_conv_b, ssm_a_log=ssm_a_log, ssm_d=ssm_d,
             ssm_norm_w=ssm_norm_w, w_proj_m=w_proj_m, w_proj_s=w_proj_s, w_out=w_out, final_w=final_w)
    m = dict(zip(WEIGHTS, (m_norm_w, m_ada_w, m_ada_b, m_w_in, m_b_in, m_ml_conv_w, m_ml_conv_b, m_ml_norm_w, m_ssm_conv_w,
                           m_ssm_conv_b, m_ssm_a_log, m_ssm_d, m_ssm_norm_w, m_w_proj_m, m_w_proj_s, m_w_out, m_final_w)))
    v = dict(zip(WEIGHTS, (v_norm_w, v_ada_w, v_ada_b, v_w_in, v_b_in, v_ml_conv_w, v_ml_conv_b, v_ml_norm_w, v_ssm_conv_w,
                           v_ssm_conv_b, v_ssm_a_log, v_ssm_d, v_ssm_norm_w, v_w_proj_m, v_w_proj_s, v_w_out, v_final_w)))
    pos = _position()
    chip = 2 * pos[0] + pos[1]
    dev = 2 * chip + pos[2]
    mlw_cols, ssw_cols, ada_cols = ml_conv_w.shape[2], ssm_conv_w.shape[2], ada_w.shape[2]

    g0 = _allgather8(_pack([c, ml_conv_w, ssm_conv_w]))
    c_all, mlw_all, ssw_all = _unpack_rows(g0, [(D_MODEL,), (CONV_K, mlw_cols), (CONV_K, ssw_cols)])
    ml_conv_full = mlw_all[0::2].transpose(1, 0, 2).reshape(CONV_K, N_CHIPS * mlw_cols)
    ssm_conv_full = ssw_all[0::2].transpose(1, 0, 2).reshape(CONV_K, N_CHIPS * ssw_cols)

    ada_b_mine = lax.dynamic_slice_in_dim(ada_b, chip * ada_cols, ada_cols, axis=1)
    g1 = _allgather8(_ada_fwd(c_all, ada_w[0], ada_b_mine))
    mod = lax.dynamic_index_in_dim(g1[0::2], dev, axis=1, keepdims=False).reshape(1, 3 * D_MODEL)
    shift, scale, gate = mod[:, :D_MODEL], mod[:, D_MODEL:2 * D_MODEL], mod[:, 2 * D_MODEL:]

    def whole(lands, owns):
        return [lax.dynamic_update_index_in_dim(got, own, chip, 0).reshape(N_CHIPS, -1, own.shape[-1])
                for got, own in zip(_pair_forward(lands), owns)]

    mine = [_bf(a[0]).reshape(2, a.shape[1] // 2, a.shape[2]) for a in (w_in, w_proj_m, w_proj_s, w_out)]
    landing = lambda own: lax.empty((N_CHIPS,) + own.shape, own.dtype)
    w_send, w_recv, w_src, w_land, w_token = _split_start("w_in_gather_start", _gather_copies, _gather_pieces(mine[:1]),
                                                          mine[:1], [landing(mine[0])])
    u = _prenorm_fwd(x[0], norm_w, scale + w_token[0:1, 0:1], shift, 512)
    w_src, w_land = _split_wait("w_in_gather_wait", _gather_copies, w_send, w_recv, w_src, w_land, u)
    behind = (w_src[0][0, 0:1, 0:1] * 0).astype(BF16)
    later = [a + behind for a in mine[1:]]
    p_send, p_recv, p_src, p_land, p_token = _split_start("merge_gather_start", _gather_copies, _gather_pieces(later),
                                                          later, [landing(a) for a in later])
    w_in_p = _shards_to_padded(whole(w_land, w_src)[0])
    b_in_p = _pad_cols(b_in) + p_token[0:1, 0:1]

    def late_weights(after):
        srcs, lands = _split_wait("merge_gather_wait", _gather_copies, p_send, p_recv, p_src, p_land, after)
        return [a.reshape(-1, D_MODEL) for a in whole(lands, srcs)]

    def start_exchange(g_w_in, g_wpm, g_wps, g_wo):
        split = lambda g, rows: _bf(g).reshape(N_CHIPS, 2, rows // (2 * N_CHIPS), g.shape[-1])
        slabs = [split(_padded_to_shards(_bf(g_w_in)), N_CHIPS * D_MODEL),
                 split(g_wpm, g_wpm.shape[0]), split(g_wps, g_wps.shape[0]), split(g_wo, g_wo.shape[0])]
        pair_sums = []
        for slab, rec in zip(slabs, _pair_send(slabs)):
            kept = lax.dynamic_index_in_dim(slab, pos[2], 1, keepdims=False)
            rows = kept.shape[0] * kept.shape[1]
            both = _sum_parts(kept.reshape(rows, -1), rec.reshape(1, rows, -1), 32, BF16)
            pair_sums.append(both.reshape(kept.shape))
        send_sems, recv_sems, sums, lands, token = _chip_scatter_start(pair_sums)
        return token, (send_sems, recv_sems, sums, lands)

    grad_x, small, _, _, _, _, in_flight = _local_step(
        x[0], loss_target[0], scale, shift, gate, norm_w, w_in_p, b_in_p, ml_conv_full, ml_conv_b, ml_norm_w,
        ssm_conv_full, ssm_conv_b, ssm_a_log, ssm_d, ssm_norm_w, None, None, None, final_w[None], start_exchange,
        late_weights, u)

    g2 = _allgather8(_pack([small[name] for name, _ in SMALL_SUMS]))
    total = dict(zip([name for name, _ in SMALL_SUMS], _unpack(_sum_parts(None, g2, g2.shape[1]), [s for _, s in SMALL_SUMS])))
    dmod_all = g2[:, :3 * D_MODEL // LANE].reshape(N_DEV, 3 * D_MODEL)
    grads = dict(total)
    grads["ada_b"] = total["mod"]
    grads["ml_conv_w"] = lax.dynamic_slice_in_dim(total["ml_conv_w"], chip * mlw_cols, mlw_cols, axis=1)
    grads["ssm_conv_w"] = lax.dynamic_slice_in_dim(total["ssm_conv_w"], chip * ssw_cols, ssw_cols, axis=1)
    grads["ada_w"] = _ada_bwd(c_all, lax.dynamic_slice_in_dim(dmod_all, chip * ada_cols, ada_cols, axis=1))

    pairs = []
    for both, rec in zip(*_chip_scatter_wait(*in_flight, grad_x)):
        pairs.append(_sum_parts(lax.dynamic_index_in_dim(both, chip, 0, keepdims=False), rec, 32, slot=pos[2]))
    for name, full in zip(("w_in", "w_proj_m", "w_proj_s", "w_out"), _pair_exchange(pairs)):
        grads[name] = full.reshape(-1, full.shape[-1])

    delta, new_m, new_v = {}, {}, {}
    for name in LARGE:
        if w[name].shape[-1] % LANE:
            flat = lambda a: a.reshape(a.shape[-2:]).T
            back = lambda a: a.T.reshape(w[name].shape)
            g_flat = flat(grads[name])
            delta[name], new_m[name], new_v[name] = (back(a) for a in _adamw(flat(w[name]), g_flat, flat(m[name]), flat(v[name]), LANE))
            grads[name] = back(g_flat)
        else:
            delta[name], new_m[name], new_v[name] = _adamw(w[name], grads[name], m[name], v[name], 64)
    rest = [name for name in WEIGHTS if name not in LARGE]
    packed = [_pack([t[name] for name in rest]) for t in (w, grads, m, v)]
    for out, buf in zip((delta, new_m, new_v), _adamw(*packed, packed[0].shape[0])):
        out.update(zip(rest, _unpack(buf, [w[name].shape for name in rest])))
    loss = total["loss"][0]
    return (loss, grad_x[None], *[grads[name].reshape(w[name].shape) for name in WEIGHTS], *[delta[name] for name in WEIGHTS],
            *[new_m[name] for name in WEIGHTS], *[new_v[name] for name in WEIGHTS])
```

```python
import math

import jax
import jax.numpy as jnp
from jax import lax
from jax.experimental import pallas as pl
from jax.experimental.pallas import tpu as pltpu

F32 = jnp.float32
BF16 = jnp.bfloat16
MESH = pl.DeviceIdType.MESH

D_MODEL = 1024
EPS = 1e-6
CONV_K = 4
ML_HEADS = 8
ML_DQK = 128
ML_DV = 256
SSM_HEADS = 32
SSM_HEADDIM = 64
SSM_GROUPS = 4
SSM_STATE = 128
IN_WIDTH = 15408
N_CHIPS = 4
N_DEV = 8
ADAM_LR, ADAM_B1, ADAM_B2, ADAM_EPS, ADAM_WD, ADAM_STEP = 0.001, 0.9, 0.999, 1e-08, 0.01, 10

O_O, O_ZM, O_ZS, O_MG, O_QK, O_V, O_XBC, O_IF, O_DT = 0, 2048, 4096, 6144, 8192, 10240, 12288, 15360, 15616
SMALL_W = 256
NP = 15872
LANE = 128
CHUNK = 128
NEG = -1e30
VMEM_LIMIT = 48 * 1024 * 1024
MERGE_VMEM = 60 * 1024 * 1024
ROWS_ELEMENTWISE = 512
ROWS_MATMUL = 2048
ROWS_WIDE_MATMUL = 4096
COLS_MATMUL = 512
ROWS_POST = 128
ROWS_MERGE = 256
ROWS_SUM = 32
ROWS_ADAMW = 64


def _cparams(*sem, vmem=VMEM_LIMIT):
    return pltpu.CompilerParams(dimension_semantics=sem, vmem_limit_bytes=vmem)


def _pad_cols(w):
    z = lambda n: jnp.zeros(w.shape[:-1] + (n,), w.dtype)
    return jnp.concatenate([w[..., 4096:8192], w[..., 11280:13328], w[..., 13360:15408], w[..., :4096], w[..., 8208:11280],
                            w[..., 8192:8208], z(SMALL_W - 16), w[..., 13328:13360], z(SMALL_W - 32)], axis=-1)


def _unpad_cols(g):
    return jnp.concatenate([g[..., O_QK:O_QK + 4096], g[..., O_O:O_O + 4096], g[..., O_IF:O_IF + 16],
                            g[..., O_XBC:O_XBC + 3072], g[..., O_ZS:O_ZS + 2048], g[..., O_DT:O_DT + 32],
                            g[..., O_MG:O_MG + 2048]], axis=-1)


PADDED_SEGMENTS = ((4096, 8192, 0), (11280, 13328, 0), (13360, 15408, 0), (0, 4096, 0), (8208, 11280, 0),
                   (8192, 8208, SMALL_W - 16), (13328, 13360, SMALL_W - 32))
SHARD_W = IN_WIDTH // N_CHIPS


def _shards_to_padded(shards):
    parts = []
    for first, last, pad in PADDED_SEGMENTS:
        for j in range(N_CHIPS):
            lo, hi = max(first, j * SHARD_W), min(last, (j + 1) * SHARD_W)
            if lo < hi:
                parts.append(shards[j][:, lo - j * SHARD_W:hi - j * SHARD_W])
        if pad:
            parts.append(jnp.zeros((shards.shape[1], pad), shards.dtype))
    return jnp.concatenate(parts, axis=1)


def _padded_to_shards(g):
    offsets, off = {}, 0
    for first, last, pad in PADDED_SEGMENTS:
        offsets[first] = off
        off += last - first + pad
    shards = []
    for j in range(N_CHIPS):
        parts = []
        for first, last, _ in sorted(PADDED_SEGMENTS):
            lo, hi = max(first, j * SHARD_W), min(last, (j + 1) * SHARD_W)
            if lo < hi:
                parts.append(g[:, offsets[first] + lo - first:offsets[first] + hi - first])
        shards.append(jnp.concatenate(parts, axis=1))
    return jnp.stack(shards)


def _sigmoid(x):
    return 0.5 * jnp.tanh(0.5 * x) + 0.5


def _silu(x):
    return x * _sigmoid(x)


def _dsilu(x):
    s = _sigmoid(x)
    return s + x * s * (1.0 - s)


def _softplus(x):
    return jnp.maximum(x, 0.0) + jnp.log(1.0 + jnp.exp(-jnp.abs(x)))


def _logsigmoid(x):
    return jnp.minimum(x, 0.0) - jnp.log(1.0 + jnp.exp(-jnp.abs(x)))


def _dot(a, b, dims):
    return lax.dot_general(a, b, (dims, ((), ())), preferred_element_type=F32)


def _nn(a, b):
    return _dot(a, b, ((1,), (0,)))


def _nt(a, b):
    return _dot(a, b, ((1,), (1,)))


def _tn(a, b):
    return _dot(a, b, ((0,), (0,)))


def _bf(x):
    return x.astype(BF16)


def _split(x, terms):
    parts = []
    for _ in range(terms):
        part = _bf(x)
        parts.append(part)
        x = x - part.astype(F32)
    return parts


def _pick_right(x, pick, terms):
    pick = _bf(pick)
    out = None
    for part in _split(x, terms):
        out = _nn(part, pick) if out is None else out + _nn(part, pick)
    return out


def _pick_left(pick, x, terms):
    pick = _bf(pick)
    out = None
    for part in _split(x, terms):
        out = _nn(pick, part) if out is None else out + _nn(pick, part)
    return out


def _lane_col(x, lane):
    idx = lax.broadcasted_iota(jnp.int32, x.shape, 1)
    return jnp.sum(jnp.where(idx == lane, x, 0.0), axis=1, keepdims=True)


def _tri(n, upper):
    r = lax.broadcasted_iota(jnp.int32, (n, n), 0)
    c = lax.broadcasted_iota(jnp.int32, (n, n), 1)
    return jnp.where((r <= c) if upper else (r >= c), 1.0, 0.0).astype(F32)


def _eye(n):
    return jnp.where(lax.broadcasted_iota(jnp.int32, (n, n), 0) == lax.broadcasted_iota(jnp.int32, (n, n), 1), 1.0, 0.0)


def _sum_all(x):
    return jnp.sum(jnp.sum(x, axis=1, keepdims=True), axis=0, keepdims=True)


def _crossing(p):
    L = p.shape[0]
    below = _nn(_bf(_tri(L, True)), _bf(p))
    strict = lax.broadcasted_iota(jnp.int32, (L, L), 0) > lax.broadcasted_iota(jnp.int32, (L, L), 1)
    return [jnp.sum(jnp.where(strict, below[:, b * L:(b + 1) * L], 0.0), axis=1, keepdims=True)
            for b in range(p.shape[1] // L)]


def _matmul_bias(a, w, bias, tm, tn, col0, ncols, dtype):
    m, k = a.shape
    j0 = col0 // tn

    def body(a_ref, w_ref, b_ref, o_ref):
        o_ref[...] = (_nn(a_ref[...], w_ref[...]) + b_ref[...]).astype(dtype)

    return pl.pallas_call(
        body, name="matmul_bias", grid=(m // tm, ncols // tn),
        in_specs=[pl.BlockSpec((tm, k), lambda i, j: (i, 0)), pl.BlockSpec((k, tn), lambda i, j: (0, j0 + j)),
                  pl.BlockSpec((1, tn), lambda i, j: (0, j0 + j))],
        out_specs=pl.BlockSpec((tm, tn), lambda i, j: (i, j)),
        out_shape=jax.ShapeDtypeStruct((m, ncols), dtype),
        compiler_params=_cparams("parallel", "arbitrary"))(a, w, bias)


def _matmul_nt(a, w, tm, tk, after=None):
    m, n = a.shape
    k = w.shape[0]

    def body(a_ref, w_ref, *rest):
        o_ref = rest[-1]

        @pl.when(pl.program_id(1) == 0)
        def _():
            o_ref[...] = jnp.zeros_like(o_ref)
        o_ref[...] += _nt(a_ref[...], w_ref[...])

    extra = [] if after is None else [after]
    return pl.pallas_call(
        body, name="matmul_nt", grid=(m // tm, n // tk),
        in_specs=[pl.BlockSpec((tm, tk), lambda i, j: (i, j)), pl.BlockSpec((k, tk), lambda i, j: (0, j))]
        + [pl.BlockSpec(memory_space=pl.ANY)] * len(extra),
        out_specs=pl.BlockSpec((tm, k), lambda i, j: (i, 0)),
        out_shape=jax.ShapeDtypeStruct((m, k), F32),
        compiler_params=_cparams("parallel", "arbitrary"))(a, w, *extra)


def _matmul_tn(a, b, tm, tn, with_colsum=False, a_is_transposed=False, dtype=F32):
    k, m = a.shape if a_is_transposed else a.shape[::-1]
    n = b.shape[1]
    steps = m // tm

    def body(a_ref, b_ref, o_ref, *rest):
        acc_ref = o_ref if dtype == F32 else rest[-1]
        first = pl.program_id(1) == 0

        @pl.when(first)
        def _():
            acc_ref[...] = jnp.zeros_like(acc_ref)
        acc_ref[...] += _nn(a_ref[...], b_ref[...]) if a_is_transposed else _tn(a_ref[...], b_ref[...])
        if dtype != F32:
            @pl.when(pl.program_id(1) == steps - 1)
            def _():
                o_ref[...] = acc_ref[...].astype(dtype)
        if with_colsum:
            s_ref = rest[0]

            @pl.when(first)
            def _():
                s_ref[...] = jnp.zeros_like(s_ref)
            s_ref[...] += jnp.sum(b_ref[...].astype(F32), axis=0, keepdims=True)

    out_specs = [pl.BlockSpec((k, tn), lambda j, i: (0, j))]
    out_shape = [jax.ShapeDtypeStruct((k, n), dtype)]
    if with_colsum:
        out_specs.append(pl.BlockSpec((1, tn), lambda j, i: (0, j)))
        out_shape.append(jax.ShapeDtypeStruct((1, n), F32))
    out = pl.pallas_call(
        body, name="matmul_tn", grid=(n // tn, m // tm),
        in_specs=[pl.BlockSpec((k, tm), lambda j, i: (0, i)) if a_is_transposed else pl.BlockSpec((tm, k), lambda j, i: (i, 0)),
                  pl.BlockSpec((tm, tn), lambda j, i: (i, j))],
        out_specs=out_specs, out_shape=out_shape,
        scratch_shapes=[] if dtype == F32 else [pltpu.VMEM((k, tn), F32)],
        compiler_params=_cparams("parallel", "arbitrary"))(a, b)
    return out if with_colsum else out[0]


def _ada_fwd(c_all, ada_w, ada_b):
    def body(c_ref, w_ref, b_ref, o_ref):
        o_ref[...] = _nn(_bf(_silu(c_ref[...])), _bf(w_ref[...])) + b_ref[...]

    return pl.pallas_call(body, name="ada_fwd", out_shape=jax.ShapeDtypeStruct((c_all.shape[0], ada_w.shape[1]), F32),
                          compiler_params=_cparams())(c_all, ada_w, ada_b)


def _ada_bwd(c_all, dmod):
    def body(c_ref, d_ref, o_ref):
        o_ref[...] = _tn(_bf(_silu(c_ref[...])), _bf(d_ref[...]))

    return pl.pallas_call(body, name="ada_bwd", out_shape=jax.ShapeDtypeStruct((c_all.shape[1], dmod.shape[1]), F32),
                          compiler_params=_cparams())(c_all, dmod)


def _prenorm_fwd(x, norm_w, scale, shift, ts):
    s, d = x.shape

    def body(x_ref, nw_ref, sc_ref, sh_ref, u_ref):
        xv = x_ref[...]
        r = lax.rsqrt(jnp.mean(xv * xv, axis=1, keepdims=True) + EPS)
        u_ref[...] = _bf(xv * r * nw_ref[...] * (1.0 + sc_ref[...]) + sh_ref[...])

    row = pl.BlockSpec((1, d), lambda i: (0, 0))
    return pl.pallas_call(
        body, name="prenorm_fwd", grid=(s // ts,),
        in_specs=[pl.BlockSpec((ts, d), lambda i: (i, 0)), row, row, row],
        out_specs=pl.BlockSpec((ts, d), lambda i: (i, 0)), out_shape=jax.ShapeDtypeStruct((s, d), BF16),
        compiler_params=_cparams("parallel"))(x, norm_w, scale, shift)


def _prenorm_bwd(du, x, dxres, norm_w, scale, ts):
    s, d = x.shape

    def body(du_ref, x_ref, dr_ref, nw_ref, sc_ref, gx_ref, acc_ref):
        @pl.when(pl.program_id(0) == 0)
        def _():
            acc_ref[...] = jnp.zeros_like(acc_ref)
        xv, duv = x_ref[...], du_ref[...]
        r = lax.rsqrt(jnp.mean(xv * xv, axis=1, keepdims=True) + EPS)
        xn = xv * r
        nw, sc1 = nw_ref[...], 1.0 + sc_ref[...]
        dxn = duv * (nw * sc1)
        gx_ref[...] = r * (dxn - xn * jnp.mean(dxn * xn, axis=1, keepdims=True)) + dr_ref[...]
        t = duv * xn
        acc_ref[0:1, :] += jnp.sum(t, axis=0, keepdims=True) * sc1
        acc_ref[1:2, :] += jnp.sum(t, axis=0, keepdims=True) * nw
        acc_ref[2:3, :] += jnp.sum(duv, axis=0, keepdims=True)

    tile = pl.BlockSpec((ts, d), lambda i: (i, 0))
    row = pl.BlockSpec((1, d), lambda i: (0, 0))
    return pl.pallas_call(
        body, name="prenorm_bwd", grid=(s // ts,),
        in_specs=[tile, tile, tile, row, row],
        out_specs=[tile, pl.BlockSpec((8, d), lambda i: (0, 0))],
        out_shape=[jax.ShapeDtypeStruct((s, d), F32), jax.ShapeDtypeStruct((8, d), F32)],
        compiler_params=_cparams("arbitrary"))(du, x, dxres, norm_w, scale)


CONV_CB = 512


def _conv_taps(buf_ref, ts):
    return [buf_ref[pl.ds(8 - (CONV_K - 1) + j, ts), :] for j in range(CONV_K)]


def _conv_fwd(proj, col0, width, w8, b, ts):
    s = proj.shape[0]
    cb = CONV_CB
    nt = s // ts

    def body(x_ref, w_ref, b_ref, o_ref, ds_ref, buf_ref):
        @pl.when(pl.program_id(1) == 0)
        def _():
            buf_ref[0:8, :] = jnp.zeros((8, cb), F32)
        buf_ref[pl.ds(8, ts), :] = x_ref[...].astype(F32)
        acc = b_ref[...] + jnp.zeros((ts, cb), F32)
        for j, tap in enumerate(_conv_taps(buf_ref, ts)):
            acc = acc + tap * w_ref[j:j + 1, :]
        sg = _sigmoid(acc)
        o_ref[...] = acc * sg
        ds_ref[...] = _bf(sg + acc * sg * (1.0 - sg))
        buf_ref[0:8, :] = buf_ref[pl.ds(ts, 8), :]

    c0 = col0 // cb
    tile = pl.BlockSpec((ts, cb), lambda c, i: (i, c))
    return pl.pallas_call(
        body, name="conv_fwd", grid=(width // cb, nt),
        in_specs=[pl.BlockSpec((ts, cb), lambda c, i: (i, c0 + c)), pl.BlockSpec((8, cb), lambda c, i: (0, c)),
                  pl.BlockSpec((1, cb), lambda c, i: (0, c))],
        out_specs=[tile, tile],
        out_shape=[jax.ShapeDtypeStruct((s, width), F32), jax.ShapeDtypeStruct((s, width), BF16)],
        scratch_shapes=[pltpu.VMEM((ts + 8, cb), F32)],
        compiler_params=_cparams("parallel", "arbitrary"))(proj, w8, b)


def _conv_bwd(proj, col0, width, w8, dact, dpost, dproj, ts):
    s = proj.shape[0]
    cb = CONV_CB
    nt = s // ts
    c0 = col0 // cb

    def body(x_ref, da_ref, dp_ref, w_ref, _, dx_ref, acc_ref, dbuf_ref):
        @pl.when(pl.program_id(1) == 0)
        def _():
            acc_ref[...] = jnp.zeros_like(acc_ref)
            dbuf_ref[pl.ds(ts, 8), :] = jnp.zeros((8, cb), F32)
        dconv = dp_ref[...].astype(F32) * da_ref[...].astype(F32)
        acc_ref[CONV_K:CONV_K + 1, :] += jnp.sum(dconv, axis=0, keepdims=True)
        dbuf_ref[pl.ds(0, ts), :] = dconv
        xv = x_ref[...].astype(F32)
        dx = jnp.zeros((ts, cb), F32)
        for j in range(CONV_K):
            shifted = dbuf_ref[pl.ds(CONV_K - 1 - j, ts), :]
            dx = dx + shifted * w_ref[j:j + 1, :]
            acc_ref[j:j + 1, :] += jnp.sum(xv * shifted, axis=0, keepdims=True)
        dx_ref[...] = _bf(dx)
        dbuf_ref[pl.ds(ts, 8), :] = dconv[0:8, :]

    tile = pl.BlockSpec((ts, cb), lambda c, i: (nt - 1 - i, c))
    wide = pl.BlockSpec((ts, cb), lambda c, i: (nt - 1 - i, c0 + c))
    return pl.pallas_call(
        body, name="conv_bwd", grid=(width // cb, nt),
        in_specs=[wide, tile, tile, pl.BlockSpec((8, cb), lambda c, i: (0, c)), pl.BlockSpec(memory_space=pl.ANY)],
        out_specs=[wide, pl.BlockSpec((8, cb), lambda c, i: (0, c))],
        out_shape=[jax.ShapeDtypeStruct(dproj.shape, dproj.dtype), jax.ShapeDtypeStruct((8, width), F32)],
        input_output_aliases={4: 0},
        scratch_shapes=[pltpu.VMEM((ts + 8, cb), F32)],
        compiler_params=_cparams("parallel", "arbitrary"))(proj, dact, dpost, w8, dproj)


def _mlstm_gates(gif_ref, gt_ref, a_scr, at_scr):
    L = gif_ref.shape[0]
    fb = _logsigmoid(gif_ref[...])
    a_scr[...] = _pick_left(_tri(L, False), fb, 3)
    at_scr[...] = _pick_right(_logsigmoid(gt_ref[...]), _tri(L, True), 3)
    return jnp.sum(fb, axis=0, keepdims=True)


def _mlstm_head(h, qk_ref, v_ref, gif, gt_ref, a, at_scr, a_last_row, c_mat, n_row, m_prev):
    L = gif.shape[0]
    q = qk_ref[:, h * ML_DQK:(h + 1) * ML_DQK] * (ML_DQK ** -0.5)
    k = qk_ref[:, (ML_HEADS + h) * ML_DQK:(ML_HEADS + h + 1) * ML_DQK]
    v = v_ref[:, h * ML_DV:(h + 1) * ML_DV]
    i_col, a_col = _lane_col(gif, h), _lane_col(a, ML_HEADS + h)
    i_row, a_row = gt_ref[h:h + 1, :], at_scr[ML_HEADS + h:ML_HEADS + h + 1, :]
    causal = lax.broadcasted_iota(jnp.int32, (L, L), 0) >= lax.broadcasted_iota(jnp.int32, (L, L), 1)
    dmat = jnp.where(causal, a_col - a_row + i_row, NEG)
    inter = a_col + m_prev
    m_t = jnp.maximum(inter, jnp.max(dmat, axis=1, keepdims=True))
    w_intra = jnp.exp(dmat - m_t)
    w_inter = jnp.exp(inter - m_t)
    sc = _nt(_bf(q), _bf(k)) * w_intra
    den = jnp.sum(sc, axis=1, keepdims=True) + w_inter * jnp.sum(q * n_row, axis=1, keepdims=True)
    floor = jnp.exp(-m_t)
    a_last = _lane_col(a_last_row, ML_HEADS + h)
    g = a_last - a_col + i_col
    m_new = jnp.maximum(a_last + m_prev, jnp.max(g, axis=0, keepdims=True))
    wk = jnp.exp(g - m_new)
    decay = jnp.exp(a_last + m_prev - m_new)
    return dict(q=q, k=k, v=v, w_intra=w_intra, w_inter=w_inter, sc=sc, den=den, floor=floor, m_new=m_new, wk=wk,
                decay=decay)


def _state_tile(n_row, m11):
    r = lax.broadcasted_iota(jnp.int32, (8, LANE), 0)
    return jnp.where(r == 0, n_row, jnp.where(r == 1, m11, 0.0))


def _mlstm_fwd(qk, proj, gates, gt):
    s = qk.shape[0]
    L = CHUNK
    nc = s // L

    def body(qk_ref, v_ref, gif_ref, gt_ref, h_ref, cst_ref, nm_ref, c_scr, nm_scr, a_scr, at_scr):
        @pl.when(pl.program_id(0) == 0)
        def _():
            c_scr[...] = jnp.zeros_like(c_scr)
            nm_scr[...] = jnp.zeros_like(nm_scr)
        a_last_row = _mlstm_gates(gif_ref, gt_ref, a_scr, at_scr)
        gif, a = gif_ref[...], a_scr[...]
        for h in range(ML_HEADS):
            c_mat, n_row = c_scr[h], nm_scr[h, 0:1, :]
            m_prev = jnp.max(nm_scr[h, 1:2, :], axis=1, keepdims=True)
            cst_ref[0, h] = c_mat
            nm_ref[0, h] = nm_scr[h]
            t = _mlstm_head(h, qk_ref, v_ref, gif, gt_ref, a, at_scr, a_last_row, c_mat, n_row, m_prev)
            num = _nn(_bf(t["sc"]), _bf(t["v"])) + t["w_inter"] * _nn(_bf(t["q"]), _bf(c_mat))
            h_ref[:, h * ML_DV:(h + 1) * ML_DV] = _bf(num * (1.0 / jnp.maximum(jnp.abs(t["den"]), t["floor"])))
            kw = t["k"] * t["wk"]
            c_scr[h] = t["decay"] * c_mat + _tn(_bf(kw), _bf(t["v"]))
            nm_scr[h] = _state_tile(t["decay"] * n_row + jnp.sum(kw, axis=0, keepdims=True), t["m_new"])

    return pl.pallas_call(
        body, name="mlstm_fwd", grid=(nc,),
        in_specs=[pl.BlockSpec((L, 2048), lambda c: (c, 0)), pl.BlockSpec((L, 2048), lambda c: (c, O_V // 2048)),
                  pl.BlockSpec((L, LANE), lambda c: (c, 0)), pl.BlockSpec((LANE, L), lambda c: (0, c))],
        out_specs=[pl.BlockSpec((L, 2048), lambda c: (c, 0)),
                   pl.BlockSpec((1, ML_HEADS, ML_DQK, ML_DV), lambda c: (c, 0, 0, 0)),
                   pl.BlockSpec((1, ML_HEADS, 8, LANE), lambda c: (c, 0, 0, 0))],
        out_shape=[jax.ShapeDtypeStruct((s, 2048), BF16), jax.ShapeDtypeStruct((nc, ML_HEADS, ML_DQK, ML_DV), F32),
                   jax.ShapeDtypeStruct((nc, ML_HEADS, 8, LANE), F32)],
        scratch_shapes=[pltpu.VMEM((ML_HEADS, ML_DQK, ML_DV), F32), pltpu.VMEM((ML_HEADS, 8, LANE), F32),
                        pltpu.VMEM((L, LANE), F32), pltpu.VMEM((LANE, L), F32)],
        compiler_params=_cparams("arbitrary"))(qk, proj, gates, gt)


def _mlstm_bwd(qk, proj, gates, gt, hout, dh, cst, nm, dproj):
    s = qk.shape[0]
    L = CHUNK
    nc = s // L

    def body(qk_ref, v_ref, gif_ref, gt_ref, h_ref, dh_ref, cst_ref, nm_ref, _, dqk_ref, dv_ref, dif_ref,
             dc_scr, dn_scr, a_scr, at_scr):
        @pl.when(pl.program_id(0) == 0)
        def _():
            dc_scr[...] = jnp.zeros_like(dc_scr)
            dn_scr[...] = jnp.zeros_like(dn_scr)
        a_last_row = _mlstm_gates(gif_ref, gt_ref, a_scr, at_scr)
        gif, a = gif_ref[...], a_scr[...]
        lane = lax.broadcasted_iota(jnp.int32, (L, LANE), 1)
        last = lax.broadcasted_iota(jnp.int32, (L, 1), 0) == L - 1
        di_tile = jnp.zeros((L, LANE), F32)
        cross = [jnp.zeros((L, LANE), F32)] * 3
        dlogw = []
        for h in range(ML_HEADS):
            c_mat, n_row = cst_ref[0, h], nm_ref[0, h, 0:1, :]
            m_prev = jnp.max(nm_ref[0, h, 1:2, :], axis=1, keepdims=True)
            t = _mlstm_head(h, qk_ref, v_ref, gif, gt_ref, a, at_scr, a_last_row, c_mat, n_row, m_prev)
            q, k, v, den = t["q"], t["k"], t["v"], t["den"]
            dhh = dh_ref[:, h * ML_DV:(h + 1) * ML_DV].astype(F32)
            hh = h_ref[:, h * ML_DV:(h + 1) * ML_DV].astype(F32)
            dnorm = jnp.maximum(jnp.abs(den), t["floor"])
            inv = 1.0 / dnorm
            dnum = dhh * inv
            d_dn = -jnp.sum(dhh * hh, axis=1, keepdims=True) * inv
            dden = jnp.where(jnp.abs(den) >= t["floor"], jnp.where(den >= 0.0, d_dn, -d_dn), 0.0)
            dsc = _nt(_bf(dnum), _bf(v)) + dden
            ds = dsc * t["w_intra"]
            dq_inter = t["w_inter"] * (_nt(_bf(dnum), _bf(c_mat)) + dden * n_row)
            dq = _nn(_bf(ds), _bf(k)) + dq_inter
            dc, dn_row = dc_scr[h], dn_scr[h, 0:1, :]
            dk_state = t["wk"] * (_nt(_bf(v), _bf(dc)) + dn_row)
            dk = _tn(_bf(ds), _bf(q)) + dk_state
            dv = _tn(_bf(t["sc"]), _bf(dnum)) + t["wk"] * _nn(_bf(k), _bf(dc))
            qi = q * t["w_inter"]
            dc_scr[h] = t["decay"] * dc + _tn(_bf(qi), _bf(dnum))
            dn_scr[h] = jnp.broadcast_to(t["decay"] * dn_row + jnp.sum(qi * dden, axis=0, keepdims=True), (8, LANE))
            dqk_ref[:, h * ML_DQK:(h + 1) * ML_DQK] = _bf(dq * (ML_DQK ** -0.5))
            dqk_ref[:, (ML_HEADS + h) * ML_DQK:(ML_HEADS + h + 1) * ML_DQK] = _bf(dk)
            dv_ref[:, h * ML_DV:(h + 1) * ML_DV] = _bf(dv)
            di_tile = di_tile + jnp.where(lane == h, jnp.sum(k * dk, axis=1, keepdims=True), 0.0)
            carried = t["decay"] * (_sum_all(dc * c_mat) + jnp.sum(dn_row * n_row, axis=1, keepdims=True))
            dlogw.append(dsc * t["sc"])
            parts = (jnp.sum(q * dq_inter, axis=1, keepdims=True) + jnp.where(last, carried, 0.0),
                     jnp.sum(k * dk_state, axis=1, keepdims=True))
            cross[1:] = [c + jnp.where(lane == ML_HEADS + h, p, 0.0) for c, p in zip(cross[1:], parts)]
        for h, col in enumerate(_crossing(jnp.concatenate(dlogw, axis=1))):
            cross[0] = cross[0] + jnp.where(lane == ML_HEADS + h, col, 0.0)
        dfb = cross[0] + _pick_left(_tri(L, True), cross[1], 2) + _pick_left(_tri(L, False) - _eye(L), cross[2], 2)
        dif_ref[:, 0:LANE] = _bf(di_tile + dfb * _sigmoid(-gif))
        dif_ref[:, LANE:SMALL_W] = jnp.zeros((L, SMALL_W - LANE), BF16)

    rev = lambda c: nc - 1 - c
    return pl.pallas_call(
        body, name="mlstm_bwd", grid=(nc,),
        in_specs=[pl.BlockSpec((L, 2048), lambda c: (rev(c), 0)), pl.BlockSpec((L, 2048), lambda c: (rev(c), O_V // 2048)),
                  pl.BlockSpec((L, LANE), lambda c: (rev(c), 0)), pl.BlockSpec((LANE, L), lambda c: (0, rev(c))),
                  pl.BlockSpec((L, 2048), lambda c: (rev(c), 0)), pl.BlockSpec((L, 2048), lambda c: (rev(c), 0)),
                  pl.BlockSpec((1, ML_HEADS, ML_DQK, ML_DV), lambda c: (rev(c), 0, 0, 0)),
                  pl.BlockSpec((1, ML_HEADS, 8, LANE), lambda c: (rev(c), 0, 0, 0)), pl.BlockSpec(memory_space=pl.ANY)],
        out_specs=[pl.BlockSpec((L, 2048), lambda c: (rev(c), 0)), pl.BlockSpec((L, 2048), lambda c: (rev(c), O_V // 2048)),
                   pl.BlockSpec((L, SMALL_W), lambda c: (rev(c), 0))],
        out_shape=[jax.ShapeDtypeStruct((s, 2048), BF16), jax.ShapeDtypeStruct(dproj.shape, dproj.dtype),
                   jax.ShapeDtypeStruct((s, SMALL_W), BF16)],
        input_output_aliases={8: 1},
        scratch_shapes=[pltpu.VMEM((ML_HEADS, ML_DQK, ML_DV), F32), pltpu.VMEM((ML_HEADS, 8, LANE), F32),
                        pltpu.VMEM((L, LANE), F32), pltpu.VMEM((LANE, L), F32)],
        compiler_params=_cparams("arbitrary"))(qk, proj, gates, gt, hout, dh, cst, nm, dproj)


GROUP_W = SSM_HEADS // SSM_GROUPS * SSM_HEADDIM
O_B = SSM_HEADS * SSM_HEADDIM
O_C = O_B + SSM_GROUPS * SSM_STATE


def _head_expand():
    r = jnp.arange(LANE)[:, None]
    c = jnp.arange(SSM_HEADS * SSM_HEADDIM)[None, :] // SSM_HEADDIM
    return (r == c).astype(F32)


def _ssd_gates(dt_ref, dtt_ref, alog_row_ref, alog_col_ref, at_scr):
    L = dt_ref.shape[0]
    dt = _softplus(dt_ref[...])
    acoef = -jnp.exp(alog_row_ref[...])
    a = _pick_left(_tri(L, False), dt * acoef, 3)
    at_scr[...] = _pick_right(_softplus(dtt_ref[...]) * (-jnp.exp(alog_col_ref[...])), _tri(L, True), 3)
    return dt, acoef, a


def _ssd_group(g, xbc_ref, dt, a, e_ref, ax_scr):
    eg = e_ref[:, g * GROUP_W:(g + 1) * GROUP_W]
    ax_scr[...] = _pick_right(a, eg, 3)
    ax = ax_scr[...]
    alx = ax_scr[ax.shape[0] - 1:ax.shape[0], :]
    dtx = _pick_right(dt, eg, 2)
    xg = xbc_ref[:, g * GROUP_W:(g + 1) * GROUP_W]
    bg = xbc_ref[:, O_B + g * SSM_STATE:O_B + (g + 1) * SSM_STATE]
    cg = xbc_ref[:, O_C + g * SSM_STATE:O_C + (g + 1) * SSM_STATE]
    return dict(ax=ax, alx=alx, dtx=dtx, xg=xg, bg=bg, cg=cg, xdt=xg * dtx, gmat=_nt(_bf(cg), _bf(bg)))


def _ssd_decay(hh, a, at_scr):
    L = a.shape[0]
    causal = lax.broadcasted_iota(jnp.int32, (L, L), 0) >= lax.broadcasted_iota(jnp.int32, (L, L), 1)
    return jnp.exp(jnp.where(causal, _lane_col(a, hh) - at_scr[hh:hh + 1, :], NEG))


def _ssd_fwd(xbc, gates, dtt, alog_row, alog_col, dskip_x, expand):
    s = xbc.shape[0]
    L = CHUNK
    nc = s // L
    half = SSM_HEADDIM

    def body(xbc_ref, dt_ref, dtt_ref, ar_ref, ac_ref, dk_ref, e_ref, y_ref, st_ref, st_scr, at_scr, ax_scr):
        @pl.when(pl.program_id(0) == 0)
        def _():
            st_scr[...] = jnp.zeros_like(st_scr)
        dt, _, a = _ssd_gates(dt_ref, dtt_ref, ar_ref, ac_ref, at_scr)
        lane = lax.broadcasted_iota(jnp.int32, (L, LANE), 1)
        for g in range(SSM_GROUPS):
            t = _ssd_group(g, xbc_ref, dt, a, e_ref, ax_scr)
            st = st_scr[g]
            st_ref[0, g] = st
            pairs = []
            for j in range(GROUP_W // LANE):
                xp = _bf(t["xdt"][:, j * LANE:(j + 1) * LANE])
                hh = g * (SSM_HEADS // SSM_GROUPS) + 2 * j
                both = jnp.concatenate([_bf(t["gmat"] * _ssd_decay(hh, a, at_scr)),
                                        _bf(t["gmat"] * _ssd_decay(hh + 1, a, at_scr))], axis=0)
                ys = _nn(both, xp)
                pairs.append(jnp.where(lane < half, ys[0:L], ys[L:2 * L]))
            y = jnp.concatenate(pairs, axis=1) + _nn(_bf(t["cg"]), _bf(st)) * jnp.exp(t["ax"])
            y_ref[:, g * GROUP_W:(g + 1) * GROUP_W] = _bf(y + dk_ref[:, g * GROUP_W:(g + 1) * GROUP_W] * t["xg"])
            wts = jnp.exp(t["alx"] - t["ax"])
            st_scr[g] = jnp.exp(t["alx"]) * st + _tn(_bf(t["bg"]), _bf(t["xdt"] * wts))

    row = lambda w: pl.BlockSpec((1, w), lambda c: (0, 0))
    return pl.pallas_call(
        body, name="ssd_fwd", grid=(nc,),
        in_specs=[pl.BlockSpec((L, 3072), lambda c: (c, 0)), pl.BlockSpec((L, LANE), lambda c: (c, (O_DT - O_IF) // LANE)),
                  pl.BlockSpec((LANE, L), lambda c: (0, c)), row(LANE), pl.BlockSpec((LANE, 1), lambda c: (0, 0)),
                  row(2048), pl.BlockSpec((LANE, 2048), lambda c: (0, 0))],
        out_specs=[pl.BlockSpec((L, 2048), lambda c: (c, 0)),
                   pl.BlockSpec((1, SSM_GROUPS, SSM_STATE, GROUP_W), lambda c: (c, 0, 0, 0))],
        out_shape=[jax.ShapeDtypeStruct((s, 2048), BF16),
                   jax.ShapeDtypeStruct((nc, SSM_GROUPS, SSM_STATE, GROUP_W), F32)],
        scratch_shapes=[pltpu.VMEM((SSM_GROUPS, SSM_STATE, GROUP_W), F32), pltpu.VMEM((LANE, L), F32),
                        pltpu.VMEM((L, GROUP_W), F32)],
        compiler_params=_cparams("arbitrary"))(xbc, gates, dtt, alog_row, alog_col, dskip_x, expand)


def _ssd_bwd(xbc, gates, dtt, alog_row, alog_col, dskip_x, expand, expand_t, dy, states):
    s = xbc.shape[0]
    L = CHUNK
    nc = s // L
    half = SSM_HEADDIM

    def body(xbc_ref, dt_ref, dtt_ref, ar_ref, ac_ref, dk_ref, e_ref, et_ref, dy_ref, st_ref,
             dxbc_ref, ddt_ref, accd_ref, acca_ref, dst_scr, at_scr, ax_scr):
        @pl.when(pl.program_id(0) == 0)
        def _():
            dst_scr[...] = jnp.zeros_like(dst_scr)
            accd_ref[...] = jnp.zeros_like(accd_ref)
            acca_ref[...] = jnp.zeros_like(acca_ref)
        dt, acoef, a = _ssd_gates(dt_ref, dtt_ref, ar_ref, ac_ref, at_scr)
        lane = lax.broadcasted_iota(jnp.int32, (L, LANE), 1)
        low = lane < half
        last = lax.broadcasted_iota(jnp.int32, (L, 1), 0) == L - 1
        cross = [jnp.zeros((L, LANE), F32)] * 3
        ddt_tile = jnp.zeros((L, LANE), F32)
        for g in range(SSM_GROUPS):
            t = _ssd_group(g, xbc_ref, dt, a, e_ref, ax_scr)
            xg, bg, cg, xdt, gmat = t["xg"], t["bg"], t["cg"], t["xdt"], t["gmat"]
            st, dst = st_ref[0, g], dst_scr[g]
            dyg = dy_ref[:, g * GROUP_W:(g + 1) * GROUP_W].astype(F32)
            ea, eal = jnp.exp(t["ax"]), jnp.exp(t["alx"])
            wts = jnp.exp(t["alx"] - t["ax"])
            dyi = dyg * ea
            y_inter = _nn(_bf(cg), _bf(st)) * ea
            dc = _nt(_bf(dyi), _bf(st))
            d_xdt_state = _nn(_bf(bg), _bf(dst)) * wts
            db = _nt(_bf(xdt * wts), _bf(dst))
            dst_scr[g] = eal * dst + _tn(_bf(cg), _bf(dyi))
            dg = jnp.zeros((L, L), F32)
            dx_pairs, dlogw = [], []
            for j in range(GROUP_W // LANE):
                xp = _bf(xdt[:, j * LANE:(j + 1) * LANE])
                dyp = dyg[:, j * LANE:(j + 1) * LANE]
                hh = g * (SSM_HEADS // SSM_GROUPS) + 2 * j
                decs = [_ssd_decay(hh, a, at_scr), _ssd_decay(hh + 1, a, at_scr)]
                ws = [gmat * decs[0], gmat * decs[1]]
                dxs = _tn(_bf(jnp.concatenate(ws, axis=1)), _bf(dyp))
                dws = _nt(_bf(jnp.concatenate([jnp.where(low, dyp, 0.0), jnp.where(low, 0.0, dyp)], axis=0)), xp)
                dw0, dw1 = dws[0:L], dws[L:2 * L]
                dg = dg + dw0 * decs[0] + dw1 * decs[1]
                dlogw += [dw0 * ws[0], dw1 * ws[1]]
                dx_pairs.append(jnp.where(low, dxs[0:L], dxs[L:2 * L]))
            for b, col in enumerate(_crossing(jnp.concatenate(dlogw, axis=1))):
                cross[0] = cross[0] + jnp.where(lane == g * (SSM_HEADS // SSM_GROUPS) + b, col, 0.0)
            d_xdt = d_xdt_state + jnp.concatenate(dx_pairs, axis=1)
            dc = dc + _nn(_bf(dg), _bf(bg))
            db = db + _tn(_bf(dg), _bf(cg))
            etg = et_ref[g * GROUP_W:(g + 1) * GROUP_W, :]
            carried = jnp.sum(dst * st, axis=0, keepdims=True) * eal
            cross[1] = cross[1] + _pick_right(dyg * y_inter + jnp.where(last, carried, 0.0), etg, 2)
            cross[2] = cross[2] + _pick_right(xdt * d_xdt_state, etg, 2)
            ddt_tile = ddt_tile + _pick_right(d_xdt * xg, etg, 2)
            dxbc_ref[:, g * GROUP_W:(g + 1) * GROUP_W] = _bf(d_xdt * t["dtx"] + dk_ref[:, g * GROUP_W:(g + 1) * GROUP_W] * dyg)
            dxbc_ref[:, O_B + g * SSM_STATE:O_B + (g + 1) * SSM_STATE] = _bf(db)
            dxbc_ref[:, O_C + g * SSM_STATE:O_C + (g + 1) * SSM_STATE] = _bf(dc)
            accd_ref[0:1, g * GROUP_W:(g + 1) * GROUP_W] += jnp.sum(dyg * xg, axis=0, keepdims=True)
        d_da = cross[0] + _pick_left(_tri(L, True), cross[1], 2) + _pick_left(_tri(L, False) - _eye(L), cross[2], 2)
        acca_ref[0:1, :] += jnp.sum(d_da * dt, axis=0, keepdims=True)
        ddt_ref[:, 0:LANE] = _bf((ddt_tile + d_da * acoef) * _sigmoid(dt_ref[...]))
        ddt_ref[:, LANE:SMALL_W] = jnp.zeros((L, SMALL_W - LANE), BF16)

    rev = lambda c: nc - 1 - c
    row = lambda w: pl.BlockSpec((1, w), lambda c: (0, 0))
    return pl.pallas_call(
        body, name="ssd_bwd", grid=(nc,),
        in_specs=[pl.BlockSpec((L, 3072), lambda c: (rev(c), 0)), pl.BlockSpec((L, LANE), lambda c: (rev(c), (O_DT - O_IF) // LANE)),
                  pl.BlockSpec((LANE, L), lambda c: (0, rev(c))), row(LANE), pl.BlockSpec((LANE, 1), lambda c: (0, 0)),
                  row(2048), pl.BlockSpec((LANE, 2048), lambda c: (0, 0)), pl.BlockSpec((2048, LANE), lambda c: (0, 0)),
                  pl.BlockSpec((L, 2048), lambda c: (rev(c), 0)),
                  pl.BlockSpec((1, SSM_GROUPS, SSM_STATE, GROUP_W), lambda c: (rev(c), 0, 0, 0))],
        out_specs=[pl.BlockSpec((L, 3072), lambda c: (rev(c), 0)), pl.BlockSpec((L, SMALL_W), lambda c: (rev(c), 0)),
                   pl.BlockSpec((8, 2048), lambda c: (0, 0)), pl.BlockSpec((8, LANE), lambda c: (0, 0))],
        out_shape=[jax.ShapeDtypeStruct((s, 3072), BF16), jax.ShapeDtypeStruct((s, SMALL_W), BF16),
                   jax.ShapeDtypeStruct((8, 2048), F32), jax.ShapeDtypeStruct((8, LANE), F32)],
        scratch_shapes=[pltpu.VMEM((SSM_GROUPS, SSM_STATE, GROUP_W), F32),
                        pltpu.VMEM((LANE, L), F32), pltpu.VMEM((L, GROUP_W), F32)],
        compiler_params=_cparams("arbitrary"))(xbc, gates, dtt, alog_row, alog_col, dskip_x, expand, expand_t, dy, states)


def _group_norm(v, width):
    outs, rs = [], []
    for k in range(v.shape[1] // width):
        blk = v[:, k * width:(k + 1) * width]
        r = lax.rsqrt(jnp.mean(blk * blk, axis=1, keepdims=True) + EPS)
        outs.append(blk * r)
        rs.append(jnp.broadcast_to(r, blk.shape))
    return jnp.concatenate(outs, axis=1), jnp.concatenate(rs, axis=1)


def _group_mean(v, width):
    return jnp.concatenate([jnp.broadcast_to(jnp.mean(v[:, k * width:(k + 1) * width], axis=1, keepdims=True),
                                             (v.shape[0], width)) for k in range(v.shape[1] // width)], axis=1)


def _post_fwd(hm, yssd, proj, ml_norm_w, ssm_norm_w, ts):
    s = hm.shape[0]

    def body(h_ref, ys_ref, o_ref, zm_ref, zs_ref, wm_ref, ws_ref, ym_ref, yso_ref):
        hn, _ = _group_norm(h_ref[...].astype(F32), ML_DV)
        ym_ref[...] = _bf(_sigmoid(o_ref[...].astype(F32)) * hn * wm_ref[...] * _silu(zm_ref[...].astype(F32)))
        pn, _ = _group_norm(ys_ref[...].astype(F32) * _silu(zs_ref[...].astype(F32)), GROUP_W)
        yso_ref[...] = _bf(pn * ws_ref[...])

    tile = pl.BlockSpec((ts, 2048), lambda i: (i, 0))
    col = lambda off: pl.BlockSpec((ts, 2048), lambda i: (i, off // 2048))
    row = pl.BlockSpec((1, 2048), lambda i: (0, 0))
    return pl.pallas_call(
        body, name="post_fwd", grid=(s // ts,),
        in_specs=[tile, tile, col(O_O), col(O_ZM), col(O_ZS), row, row],
        out_specs=[tile, tile],
        out_shape=[jax.ShapeDtypeStruct((s, 2048), BF16)] * 2,
        compiler_params=_cparams("parallel"))(hm, yssd, proj, proj, proj, ml_norm_w, ssm_norm_w)


def _post_bwd(dym, dys, hm, yssd, proj, ml_norm_w, ssm_norm_w, dproj, ts):
    s = hm.shape[0]

    def body(dym_ref, dys_ref, h_ref, ys_ref, o_ref, zm_ref, zs_ref, wm_ref, ws_ref, _,
             dh_ref, dyssd_ref, dp_ref, acc_ref):
        @pl.when(pl.program_id(0) == 0)
        def _():
            acc_ref[...] = jnp.zeros_like(acc_ref)
        hn, r = _group_norm(h_ref[...].astype(F32), ML_DV)
        so, zm, wm, d_ym = _sigmoid(o_ref[...].astype(F32)), zm_ref[...].astype(F32), wm_ref[...], dym_ref[...].astype(F32)
        sz = _silu(zm)
        hnw = hn * wm
        dp_ref[:, O_O:O_O + 2048] = _bf(d_ym * hnw * sz * so * (1.0 - so))
        dp_ref[:, O_ZM:O_ZM + 2048] = _bf(d_ym * so * hnw * _dsilu(zm))
        dhnw = d_ym * so * sz
        acc_ref[0:1, :] += jnp.sum(dhnw * hn, axis=0, keepdims=True)
        dhn = dhnw * wm
        dh_ref[...] = _bf(r * (dhn - hn * _group_mean(dhn * hn, ML_DV)))
        ysv, zs, d_ys = ys_ref[...].astype(F32), zs_ref[...].astype(F32), dys_ref[...].astype(F32)
        szs = _silu(zs)
        pn, r2 = _group_norm(ysv * szs, GROUP_W)
        acc_ref[1:2, :] += jnp.sum(d_ys * pn, axis=0, keepdims=True)
        dpn = d_ys * ws_ref[...]
        dp = r2 * (dpn - pn * _group_mean(dpn * pn, GROUP_W))
        dyssd_ref[...] = _bf(dp * szs)
        dp_ref[:, O_ZS:O_ZS + 2048] = _bf(dp * ysv * _dsilu(zs))

    tile = pl.BlockSpec((ts, 2048), lambda i: (i, 0))
    col = lambda off: pl.BlockSpec((ts, 2048), lambda i: (i, off // 2048))
    row = pl.BlockSpec((1, 2048), lambda i: (0, 0))
    sds = lambda dt: jax.ShapeDtypeStruct((s, 2048), dt)
    return pl.pallas_call(
        body, name="post_bwd", grid=(s // ts,),
        in_specs=[tile, tile, tile, tile, col(O_O), col(O_ZM), col(O_ZS), row, row, pl.BlockSpec(memory_space=pl.ANY)],
        out_specs=[tile, tile, pl.BlockSpec((ts, O_MG), lambda i: (i, 0)), pl.BlockSpec((8, 2048), lambda i: (0, 0))],
        out_shape=[sds(BF16), sds(BF16), jax.ShapeDtypeStruct(dproj.shape, dproj.dtype), jax.ShapeDtypeStruct((8, 2048), F32)],
        input_output_aliases={9: 2},
        compiler_params=_cparams("arbitrary"))(dym, dys, hm, yssd, proj, proj, proj, ml_norm_w, ssm_norm_w, dproj)


def _merge(x, ym, ys, proj, target, gate, final_w, wpm, wps, wo, ts):
    wpm_t, wps_t, wo_t = wpm.T, wps.T, wo.T
    s, d = x.shape

    def body(x_ref, ym_ref, ys_ref, mg_ref, t_ref, gate_ref, fw_ref, wpm_ref, wps_ref, wo_ref, wpmt_ref, wpst_ref, wot_ref,
             dres_ref, mer_ref, dmo_ref, dpm_ref, dps_ref, dym_ref, dys_ref, dmg_ref, acc_ref):
        @pl.when(pl.program_id(0) == 0)
        def _():
            acc_ref[...] = jnp.zeros_like(acc_ref)
        gm, gs = _sigmoid(mg_ref[:, 0:d].astype(F32)), _sigmoid(mg_ref[:, d:2 * d].astype(F32))
        pm = _nn(ym_ref[...], wpm_ref[...])
        ps = _nn(ys_ref[...], wps_ref[...])
        merged = _bf(gm * pm + gs * ps)
        mer_ref[...] = merged
        mo = _nn(merged, wo_ref[...])
        gate, fw = gate_ref[...], fw_ref[...]
        out = x_ref[...] + gate * mo
        r = lax.rsqrt(jnp.mean(out * out, axis=1, keepdims=True) + EPS)
        on = out * r
        diff = on * fw - t_ref[...]
        acc_ref[0:1, :] += jnp.sum(0.5 * jnp.sum(diff * diff, axis=1, keepdims=True) / d, axis=0, keepdims=True)
        dyv = diff * (1.0 / d)
        acc_ref[1:2, :] += jnp.sum(dyv * on, axis=0, keepdims=True)
        don = dyv * fw
        dout = r * (don - on * jnp.mean(don * on, axis=1, keepdims=True))
        dres_ref[...] = dout
        acc_ref[2:3, :] += jnp.sum(dout * mo, axis=0, keepdims=True)
        dmo = _bf(dout * gate)
        dmo_ref[...] = dmo
        dmer = _nn(dmo, wot_ref[...])
        dpm, dps = _bf(dmer * gm), _bf(dmer * gs)
        dpm_ref[...] = dpm
        dps_ref[...] = dps
        dmg_ref[:, 0:d] = _bf(dmer * pm * gm * (1.0 - gm))
        dmg_ref[:, d:2 * d] = _bf(dmer * ps * gs * (1.0 - gs))
        dym_ref[...] = _bf(_nn(dpm, wpmt_ref[...]))
        dys_ref[...] = _bf(_nn(dps, wpst_ref[...]))

    t1 = pl.BlockSpec((ts, d), lambda i: (i, 0))
    t2 = pl.BlockSpec((ts, 2 * d), lambda i: (i, 0))
    row = pl.BlockSpec((1, d), lambda i: (0, 0))
    whole = pl.BlockSpec(memory_space=pltpu.VMEM)
    sd = lambda w, dt: jax.ShapeDtypeStruct((s, w), dt)
    return pl.pallas_call(
        body, name="merge_fwd_bwd", grid=(s // ts,),
        in_specs=[t1, t2, t2, pl.BlockSpec((ts, 2 * d), lambda i: (i, O_MG // (2 * d))), t1, row, row] + [whole] * 6,
        out_specs=[t1, t1, t1, t1, t1, t2, t2, pl.BlockSpec((ts, 2 * d), lambda i: (i, O_MG // (2 * d))),
                   pl.BlockSpec((8, d), lambda i: (0, 0))],
        out_shape=[sd(d, F32), sd(d, BF16), sd(d, BF16), sd(d, BF16), sd(d, BF16), sd(2 * d, BF16), sd(2 * d, BF16),
                   sd(NP, BF16), jax.ShapeDtypeStruct((8, d), F32)],
        compiler_params=_cparams("arbitrary", vmem=MERGE_VMEM))(x, ym, ys, proj, target, gate, final_w, wpm, wps, wo, wpm_t, wps_t, wo_t)


def _adamw(w, g, m, v, tr):
    if w.ndim == 2 and w.shape[0] % 8:
        tile, steps = pl.BlockSpec((w.shape[0], tr), lambda i: (0, i)), w.shape[1] // tr
    else:
        lead = (None,) * (w.ndim - 2)
        tile, steps = pl.BlockSpec(lead + (tr, w.shape[-1]), lambda i: (0,) * len(lead) + (i, 0)), w.shape[-2] // tr

    def body(w_ref, g_ref, m_ref, v_ref, d_ref, nm_ref, nv_ref):
        gv = g_ref[...]
        m2 = ADAM_B1 * m_ref[...] + (1.0 - ADAM_B1) * gv
        v2 = ADAM_B2 * v_ref[...] + (1.0 - ADAM_B2) * (gv * gv)
        m_hat = m2 / (1.0 - ADAM_B1 ** ADAM_STEP)
        v_hat = v2 / (1.0 - ADAM_B2 ** ADAM_STEP)
        d_ref[...] = -ADAM_LR * (m_hat / (jnp.sqrt(v_hat) + ADAM_EPS) + ADAM_WD * w_ref[...])
        nm_ref[...] = m2
        nv_ref[...] = v2

    return pl.pallas_call(
        body, name="adamw", grid=(steps,), in_specs=[tile] * 4, out_specs=[tile] * 3,
        out_shape=[jax.ShapeDtypeStruct(w.shape, F32)] * 3,
        compiler_params=_cparams("parallel"))(w, g.reshape(w.shape), m, v)


def _sum_parts(own, parts, tr, dtype=F32, slot=None):
    p, rows, cols = parts.shape

    def body(*refs):
        p_ref, o_ref = refs[-2], refs[-1]
        first = None if own is None else refs[-3]
        acc = p_ref[0].astype(F32) if first is None else first[...].astype(F32) + p_ref[0].astype(F32)
        for i in range(1, p):
            acc = acc + p_ref[i].astype(F32)
        o_ref[...] = acc.astype(dtype)

    args = ([] if own is None else [own]) + [parts]
    if slot is None:
        tile = pl.BlockSpec((tr, cols), lambda i: (i, 0))
        ins = ([] if own is None else [tile]) + [pl.BlockSpec((p, tr, cols), lambda i: (0, i, 0))]
        return pl.pallas_call(
            body, name="sum_parts", grid=(rows // tr,), in_specs=ins, out_specs=tile,
            out_shape=jax.ShapeDtypeStruct((rows, cols), dtype), compiler_params=_cparams("parallel"))(*args)
    tile = pl.BlockSpec((tr, cols), lambda i, s: (i, 0))
    ins = ([] if own is None else [tile]) + [pl.BlockSpec((p, tr, cols), lambda i, s: (0, i, 0))]
    return pl.pallas_call(
        body, name="sum_parts_half", out_shape=jax.ShapeDtypeStruct((2, rows, cols), dtype),
        grid_spec=pltpu.PrefetchScalarGridSpec(
            num_scalar_prefetch=1, grid=(rows // tr,), in_specs=ins,
            out_specs=pl.BlockSpec((None, tr, cols), lambda i, s: (s[0], i, 0))),
        compiler_params=_cparams("parallel"))(jnp.reshape(slot, (1,)).astype(jnp.int32), *args)


def _position():
    return lax.axis_index("x"), lax.axis_index("y"), lax.axis_index("c")


def _flip(pos, k):
    return tuple(1 - p if (k >> s) & 1 else p for p, s in zip(pos, (2, 1, 0)))


def _allgather8(block):
    rows, cols = block.shape

    def body(x_ref, o_ref, send_sems, recv_sems, local_sem):
        pos = _position()
        me = 4 * pos[0] + 2 * pos[1] + pos[2]
        mine = pltpu.make_async_copy(x_ref, o_ref.at[me], local_sem)
        mine.start()
        copies = [pltpu.make_async_remote_copy(src_ref=x_ref, dst_ref=o_ref.at[me], send_sem=send_sems.at[k - 1],
                                               recv_sem=recv_sems.at[k - 1], device_id=_flip(pos, k), device_id_type=MESH)
                  for k in range(1, N_DEV)]
        for cp in copies:
            cp.start()
        for cp in copies:
            cp.wait()
        mine.wait()

    vmem = pl.BlockSpec(memory_space=pltpu.VMEM)
    return pl.pallas_call(
        body, name="allgather8", in_specs=[vmem], out_specs=vmem,
        out_shape=jax.ShapeDtypeStruct((N_DEV, rows, cols), block.dtype),
        scratch_shapes=[pltpu.SemaphoreType.DMA((N_DEV - 1,)), pltpu.SemaphoreType.DMA((N_DEV - 1,)),
                        pltpu.SemaphoreType.DMA],
        compiler_params=pltpu.CompilerParams(vmem_limit_bytes=VMEM_LIMIT))(block)


COPY_BYTES = 1 << 20


def _row_chunks(rows, row_bytes):
    n = max(1, min(rows // 16, -(-rows * row_bytes // COPY_BYTES)))
    while rows % (16 * n):
        n -= 1
    return [(i * (rows // n), rows // n) for i in range(n)]


def _split_start(name, make_copies, n_copies, sources, lands):
    n, m = len(sources), len(lands)

    def body(*refs):
        for cp in make_copies(_position(), refs[:n], refs[n:n + m], refs[n + m], refs[n + m + 1]):
            cp.start()
        refs[-1][...] = jnp.zeros((8, LANE), F32)

    hbm = pl.BlockSpec(memory_space=pltpu.HBM)
    sem = pl.BlockSpec(memory_space=pltpu.SEMAPHORE)
    operands = [pltpu.with_memory_space_constraint(t, pltpu.HBM) for t in list(sources) + list(lands)]
    out = pl.pallas_call(
        body, name=name, in_specs=[hbm] * (n + m),
        out_specs=[sem, sem] + [hbm] * (n + m) + [pl.BlockSpec(memory_space=pltpu.VMEM)],
        out_shape=[pltpu.SemaphoreType.DMA((n_copies,)), pltpu.SemaphoreType.DMA((n_copies,))]
        + [pltpu.HBM(t.shape, t.dtype) for t in operands] + [jax.ShapeDtypeStruct((8, LANE), F32)],
        input_output_aliases={i: 2 + i for i in range(n + m)},
        compiler_params=pltpu.CompilerParams(has_side_effects=pltpu.SideEffectType.DATAFLOW_SIDE_EFFECTING))(*operands)
    return out[0], out[1], out[2:2 + n], out[2 + n:2 + n + m], out[-1]


def _split_wait(name, make_copies, send_sems, recv_sems, sources, lands, after):
    n, m = len(sources), len(lands)

    def body(*refs):
        for cp in make_copies(_position(), refs[:n], refs[n:n + m], refs[n + m], refs[n + m + 1]):
            cp.wait_send()
            cp.wait_recv()

    hbm = pl.BlockSpec(memory_space=pltpu.HBM)
    sem = pl.BlockSpec(memory_space=pltpu.SEMAPHORE)
    out = pl.pallas_call(
        body, name=name, in_specs=[hbm] * (n + m) + [sem, sem, pl.BlockSpec(memory_space=pl.ANY)],
        out_specs=[hbm] * (n + m), out_shape=[pltpu.HBM(t.shape, t.dtype) for t in list(sources) + list(lands)],
        input_output_aliases={i: i for i in range(n + m)},
        compiler_params=pltpu.CompilerParams(has_side_effects=pltpu.SideEffectType.DATAFLOW_SIDE_EFFECTING))(
            *sources, *lands, send_sems, recv_sems, after)
    return out[:n], out[n:]


def _gather_copies(pos, halves, lands, send_sems, recv_sems):
    chip, core = 2 * pos[0] + pos[1], pos[2]
    copies = []
    for a in range(len(halves)):
        for k in range(1, N_CHIPS):
            for r0, nr in _row_chunks(halves[a].shape[1], halves[a].shape[2] * halves[a].dtype.itemsize):
                i = len(copies)
                copies.append(pltpu.make_async_remote_copy(
                    src_ref=halves[a].at[core, pl.ds(r0, nr)], dst_ref=lands[a].at[chip, core, pl.ds(r0, nr)],
                    send_sem=send_sems.at[i], recv_sem=recv_sems.at[i], device_id=_flip(pos, 2 * k), device_id_type=MESH))
    return copies


def _gather_pieces(halves):
    return (N_CHIPS - 1) * sum(len(_row_chunks(a.shape[1], a.shape[2] * a.dtype.itemsize)) for a in halves)


def _pair_forward(lands):
    n = len(lands)

    def plan(pos, ins, outs):
        remote = []
        for a in range(n):
            for k in range(1, N_CHIPS):
                there = _flip(pos, 2 * k)
                for r0, nr in _row_chunks(lands[a].shape[2], lands[a].shape[3] * lands[a].dtype.itemsize):
                    slot = (2 * there[0] + there[1], pos[2], pl.ds(r0, nr))
                    remote.append((ins[a].at[slot], outs[a].at[slot], _flip(pos, 1)))
        return remote, []

    return _exchange("pair_forward", lands, [jax.ShapeDtypeStruct(t.shape, t.dtype) for t in lands], plan,
                     _gather_pieces([jax.ShapeDtypeStruct(t.shape[1:], t.dtype) for t in lands]), 0, in_place=True)


def _exchange(name, arrays, out_shapes, plan, n_remote, n_local, in_place=False):
    n, m = len(arrays), len(out_shapes)

    def body(*refs):
        send_sems, recv_sems, local_sems = refs[n + m:]
        remote, local = plan(_position(), refs[:n], refs[n:n + m])
        assert (len(remote), len(local)) == (n_remote, n_local)
        copies = [pltpu.make_async_copy(src, dst, local_sems.at[i]) for i, (src, dst) in enumerate(local)]
        copies += [pltpu.make_async_remote_copy(src_ref=src, dst_ref=dst, send_sem=send_sems.at[i], recv_sem=recv_sems.at[i],
                                                device_id=dev, device_id_type=MESH)
                   for i, (src, dst, dev) in enumerate(remote)]
        for cp in copies:
            cp.start()
        for cp in copies:
            cp.wait()

    hbm = pl.BlockSpec(memory_space=pl.ANY)
    return pl.pallas_call(
        body, name=name, in_specs=[hbm] * n, out_specs=[hbm] * m, out_shape=out_shapes,
        input_output_aliases={i: i for i in range(n)} if in_place else {},
        scratch_shapes=[pltpu.SemaphoreType.DMA((n_remote,)), pltpu.SemaphoreType.DMA((n_remote,)),
                        pltpu.SemaphoreType.DMA((max(n_local, 1),))],
        compiler_params=pltpu.CompilerParams(has_side_effects=True))(*arrays)


def _pair_send(slabs):
    n = len(slabs)
    pieces = [_row_chunks(g.shape[2], g.shape[3] * g.dtype.itemsize) for g in slabs]

    def plan(pos, ins, outs):
        return [(ins[a].at[j, 1 - pos[2], pl.ds(r0, nr)], outs[a].at[j, pl.ds(r0, nr)], _flip(pos, 1))
                for a in range(n) for j in range(N_CHIPS) for r0, nr in pieces[a]], []

    return _exchange("pair_send", slabs, [jax.ShapeDtypeStruct((N_CHIPS,) + g.shape[2:], g.dtype) for g in slabs], plan,
                     N_CHIPS * sum(len(p) for p in pieces), 0)


def _chip_scatter_copies(pos, sums, lands, send_sems, recv_sems):
    copies = []
    for a in range(len(sums)):
        for k in range(1, N_CHIPS):
            to = _flip(pos, 2 * k)
            for r0, nr in _row_chunks(sums[a].shape[1], sums[a].shape[2] * sums[a].dtype.itemsize):
                i = len(copies)
                copies.append(pltpu.make_async_remote_copy(
                    src_ref=sums[a].at[2 * to[0] + to[1], pl.ds(r0, nr)], dst_ref=lands[a].at[k - 1, pl.ds(r0, nr)],
                    send_sem=send_sems.at[i], recv_sem=recv_sems.at[i], device_id=to, device_id_type=MESH))
    return copies


def _chip_scatter_start(sums):
    n_copies = (N_CHIPS - 1) * sum(len(_row_chunks(g.shape[1], g.shape[2] * g.dtype.itemsize)) for g in sums)
    lands = [lax.empty((N_CHIPS - 1,) + g.shape[1:], g.dtype) for g in sums]
    return _split_start("chip_scatter_start", _chip_scatter_copies, n_copies, sums, lands)


def _chip_scatter_wait(send_sems, recv_sems, sums, lands, after):
    return _split_wait("chip_scatter_wait", _chip_scatter_copies, send_sems, recv_sems, sums, lands, after)


def _pair_exchange(pairs):
    n = len(pairs)
    pieces = [_row_chunks(h.shape[1], h.shape[2] * h.dtype.itemsize) for h in pairs]

    def plan(pos, ins, outs):
        return [(ins[a].at[pos[2], pl.ds(r0, nr)], outs[a].at[pos[2], pl.ds(r0, nr)], _flip(pos, 1))
                for a in range(n) for r0, nr in pieces[a]], []

    return _exchange("pair_exchange", pairs, [jax.ShapeDtypeStruct(h.shape, h.dtype) for h in pairs], plan,
                     sum(len(p) for p in pieces), 0, in_place=True)


def _pack(arrays):
    flat = jnp.concatenate([a.reshape(-1).astype(F32) for a in arrays])
    size = -(-flat.shape[0] // (8 * LANE)) * (8 * LANE)
    return jnp.pad(flat, (0, size - flat.shape[0])).reshape(size // LANE, LANE)


def _unpack(buf, shapes):
    flat = buf.reshape(-1)
    out, off = [], 0
    for shp in shapes:
        n = math.prod(shp)
        out.append(flat[off:off + n].reshape(shp))
        off += n
    return out


def _unpack_rows(bufs, shapes):
    flat = bufs.reshape(bufs.shape[0], -1)
    out, off = [], 0
    for shp in shapes:
        n = math.prod(shp)
        out.append(flat[:, off:off + n].reshape((bufs.shape[0],) + shp))
        off += n
    return out


def _taps8(w):
    return jnp.pad(w, ((0, 8 - CONV_K), (0, 0)))


def _local_step(xs, tgt, scale, shift, gate, norm_w, w_in_p, b_in_p, ml_conv_w, ml_conv_b, ml_norm_w, ssm_conv_w,
                ssm_conv_b, ssm_a_log, ssm_d, ssm_norm_w, wpm, wps, wo, final_w, start_exchange=None, late_weights=None,
                u=None):
    s = xs.shape[0]
    ts, tm, tw, tn = min(ROWS_ELEMENTWISE, s), min(ROWS_MATMUL, s), min(ROWS_WIDE_MATMUL, s), COLS_MATMUL
    if u is None:
        u = _prenorm_fwd(xs, norm_w, scale, shift, ts)
    proj = _matmul_bias(u, w_in_p, b_in_p, tw, tn, 0, O_IF, BF16)
    gates = _matmul_bias(u, w_in_p, b_in_p, tw, tn, O_IF, NP - O_IF, F32)
    mlw8, ssw8 = _taps8(ml_conv_w), _taps8(ssm_conv_w)
    qk, qk_dact = _conv_fwd(proj, O_QK, 2048, mlw8, ml_conv_b, ts)
    xbc, xbc_dact = _conv_fwd(proj, O_XBC, 3072, ssw8, ssm_conv_b, ts)
    gt = gates[:, :LANE].T
    dtt = gates[:, O_DT - O_IF:O_DT - O_IF + LANE].T
    hm, cst, nm = _mlstm_fwd(qk, proj, gates, gt)
    alog_row = jnp.pad(ssm_a_log, ((0, 0), (0, LANE - SSM_HEADS)))
    alog_col = alog_row.reshape(LANE, 1)
    dskip_x = jnp.repeat(ssm_d[0], SSM_HEADDIM)[None]
    expand = _head_expand()
    yssd, sst = _ssd_fwd(xbc, gates, dtt, alog_row, alog_col, dskip_x, expand)
    tp = min(ROWS_POST, s)
    ym, ys = _post_fwd(hm, yssd, proj, ml_norm_w, ssm_norm_w, tp)
    if late_weights is not None:
        wpm, wps, wo = late_weights(ym)
    dxres, merged, dmo, dpm, dps, dym, dys, dproj, acc_m = _merge(xs, ym, ys, proj, tgt, gate, final_w, wpm, wps, wo,
                                                                  min(ROWS_MERGE, s))
    dh, dyssd, dproj, acc_p = _post_bwd(dym, dys, hm, yssd, proj, ml_norm_w, ssm_norm_w, dproj, tp)
    dqk, dproj, dif = _mlstm_bwd(qk, proj, gates, gt, hm, dh, cst, nm, dproj)
    dxbc, ddt, accd, acca = _ssd_bwd(xbc, gates, dtt, alog_row, alog_col, dskip_x, expand, expand.T, dyssd, sst)
    dproj, acc_cq = _conv_bwd(proj, O_QK, 2048, mlw8, qk_dact, dqk, dproj, ts)
    dproj, acc_cx = _conv_bwd(proj, O_XBC, 3072, ssw8, xbc_dact, dxbc, dproj, ts)
    dproj = dproj.at[:, O_IF:O_IF + SMALL_W].set(dif).at[:, O_DT:O_DT + SMALL_W].set(ddt)
    gw_in_p, gb_in_p = _matmul_tn(u.T, dproj, tw, tn, with_colsum=True, a_is_transposed=True, dtype=BF16)
    g_wpm = _matmul_tn(ym, dpm, tm, tn)
    g_wps = _matmul_tn(ys, dps, tm, tn)
    g_wo = _matmul_tn(merged, dmo, tm, tn)
    token, in_flight = (None, None) if start_exchange is None else start_exchange(gw_in_p, g_wpm, g_wps, g_wo)
    du = _matmul_nt(dproj, w_in_p, tm, tn, after=token)
    grad_x, acc_n = _prenorm_bwd(du, xs, dxres, norm_w, scale, ts)
    a_coef = -jnp.exp(ssm_a_log[0])
    small = dict(
        mod=jnp.concatenate([acc_n[2], acc_n[1], acc_m[2]]), norm_w=acc_n[0], b_in=_unpad_cols(gb_in_p[0]),
        ml_conv_w=acc_cq[0:CONV_K], ml_conv_b=acc_cq[CONV_K], ml_norm_w=acc_p[0], ssm_conv_w=acc_cx[0:CONV_K],
        ssm_conv_b=acc_cx[CONV_K], ssm_a_log=acca[0, :SSM_HEADS] * a_coef,
        ssm_d=accd[0].reshape(SSM_HEADS, SSM_HEADDIM).sum(axis=1), ssm_norm_w=acc_p[1], final_w=acc_m[1], loss=acc_m[0, 0:1])
    return grad_x, small, gw_in_p, g_wpm, g_wps, g_wo, in_flight


WEIGHTS = ("norm_w", "ada_w", "ada_b", "w_in", "b_in", "ml_conv_w", "ml_conv_b", "ml_norm_w", "ssm_conv_w", "ssm_conv_b",
           "ssm_a_log", "ssm_d", "ssm_norm_w", "w_proj_m", "w_proj_s", "w_out", "final_w")
LARGE = ("ada_w", "w_in", "w_proj_m", "w_proj_s", "w_out")
SMALL_SUMS = (("mod", (3 * D_MODEL,)), ("norm_w", (D_MODEL,)), ("b_in", (IN_WIDTH,)), ("ml_conv_w", (CONV_K, 2048)),
              ("ml_conv_b", (2048,)), ("ml_norm_w", (2048,)), ("ssm_conv_w", (CONV_K, 3072)), ("ssm_conv_b", (3072,)),
              ("ssm_a_log", (SSM_HEADS,)), ("ssm_d", (SSM_HEADS,)), ("ssm_norm_w", (2048,)), ("final_w", (D_MODEL,)),
              ("loss", (1,)))


def kernel(x, c, norm_w, ada_w, ada_b, w_in, b_in, ml_conv_w, ml_conv_b, ml_norm_w, ssm_conv_w, ssm_conv_b, ssm_a_log, ssm_d, ssm_norm_w, w_proj_m, w_proj_s, w_out, final_w, loss_target, m_norm_w, m_ada_w, m_ada_b, m_w_in, m_b_in, m_ml_conv_w, m_ml_conv_b, m_ml_norm_w, m_ssm_conv_w, m_ssm_conv_b, m_ssm_a_log, m_ssm_d, m_ssm_norm_w, m_w_proj_m, m_w_proj_s, m_w_out, m_final_w, v_norm_w, v_ada_w, v_ada_b, v_w_in, v_b_in, v_ml_conv_w, v_ml_conv_b, v_ml_norm_w, v_ssm_conv_w, v_ssm_conv_b, v_ssm_a_log, v_ssm_d, v_ssm_norm_w, v_w_proj_m, v_w_proj_s, v_w_out, v_final_w):
    w = dict(norm_w=norm_w, ada_w=ada_w, ada_b=ada_b, w_in=w_in, b_in=b_in, ml_conv_w=ml_conv_w, ml_conv_b=ml_conv_b,
             ml_norm_w=ml_norm_w, ssm_conv_w=ssm_conv_w, ssm_conv_b=ssm_conv_b, ssm_a_log=ssm_a_log, ssm_d=ssm_d,
             ssm_norm_w=ssm_norm_w, w_proj_m=w_proj_m, w_proj_s=w_proj_s, w_out=w_out, final_w=final_w)
    m = dict(zip(WEIGHTS, (m_norm_w, m_ada_w, m_ada_b, m_w_in, m_b_in, m_ml_conv_w, m_ml_conv_b, m_ml_norm_w, m_ssm_conv_w,
                           m_ssm_conv_b, m_ssm_a_log, m_ssm_d, m_ssm_norm_w, m_w_proj_m, m_w_proj_s, m_w_out, m_final_w)))
    v = dict(zip(WEIGHTS, (v_norm_w, v_ada_w, v_ada_b, v_w_in, v_b_in, v_ml_conv_w, v_ml_conv_b, v_ml_norm_w, v_ssm_conv_w,
                           v_ssm_conv_b, v_ssm_a_log, v_ssm_d, v_ssm_norm_w, v_w_proj_m, v_w_proj_s, v_w_out, v_final_w)))
    pos = _position()
    chip = 2 * pos[0] + pos[1]
    dev = 2 * chip + pos[2]
    mlw_cols, ssw_cols, ada_cols = ml_conv_w.shape[2], ssm_conv_w.shape[2], ada_w.shape[2]

    g0 = _allgather8(_pack([c, ml_conv_w, ssm_conv_w]))
    c_all, mlw_all, ssw_all = _unpack_rows(g0, [(D_MODEL,), (CONV_K, mlw_cols), (CONV_K, ssw_cols)])
    ml_conv_full = mlw_all[0::2].transpose(1, 0, 2).reshape(CONV_K, N_CHIPS * mlw_cols)
    ssm_conv_full = ssw_all[0::2].transpose(1, 0, 2).reshape(CONV_K, N_CHIPS * ssw_cols)

    ada_b_mine = lax.dynamic_slice_in_dim(ada_b, chip * ada_cols, ada_cols, axis=1)
    g1 = _allgather8(_ada_fwd(c_all, ada_w[0], ada_b_mine))
    mod = lax.dynamic_index_in_dim(g1[0::2], dev, axis=1, keepdims=False).reshape(1, 3 * D_MODEL)
    shift, scale, gate = mod[:, :D_MODEL], mod[:, D_MODEL:2 * D_MODEL], mod[:, 2 * D_MODEL:]

    def whole(lands, owns):
        return [lax.dynamic_update_index_in_dim(got, own, chip, 0).reshape(N_CHIPS, -1, own.shape[-1])
                for got, own in zip(_pair_forward(lands), owns)]

    mine = [_bf(a[0]).reshape(2, a.shape[1] // 2, a.shape[2]) for a in (w_in, w_proj_m, w_proj_s, w_out)]
    landing = lambda own: lax.empty((N_CHIPS,) + own.shape, own.dtype)
    w_send, w_recv, w_src, w_land, w_token = _split_start("w_in_gather_start", _gather_copies, _gather_pieces(mine[:1]),
                                                          mine[:1], [landing(mine[0])])
    u = _prenorm_fwd(x[0], norm_w, scale + w_token[0:1, 0:1], shift, ROWS_ELEMENTWISE)
    w_src, w_land = _split_wait("w_in_gather_wait", _gather_copies, w_send, w_recv, w_src, w_land, u)
    behind = (w_src[0][0, 0:1, 0:1] * 0).astype(BF16)
    later = [a + behind for a in mine[1:]]
    p_send, p_recv, p_src, p_land, p_token = _split_start("merge_gather_start", _gather_copies, _gather_pieces(later),
                                                          later, [landing(a) for a in later])
    w_in_p = _shards_to_padded(whole(w_land, w_src)[0])
    b_in_p = _pad_cols(b_in) + p_token[0:1, 0:1]

    def late_weights(after):
        srcs, lands = _split_wait("merge_gather_wait", _gather_copies, p_send, p_recv, p_src, p_land, after)
        return [a.reshape(-1, D_MODEL) for a in whole(lands, srcs)]

    def start_exchange(g_w_in, g_wpm, g_wps, g_wo):
        split = lambda g, rows: _bf(g).reshape(N_CHIPS, 2, rows // (2 * N_CHIPS), g.shape[-1])
        slabs = [split(_padded_to_shards(_bf(g_w_in)), N_CHIPS * D_MODEL),
                 split(g_wpm, g_wpm.shape[0]), split(g_wps, g_wps.shape[0]), split(g_wo, g_wo.shape[0])]
        pair_sums = []
        for slab, rec in zip(slabs, _pair_send(slabs)):
            kept = lax.dynamic_index_in_dim(slab, pos[2], 1, keepdims=False)
            rows = kept.shape[0] * kept.shape[1]
            both = _sum_parts(kept.reshape(rows, -1), rec.reshape(1, rows, -1), ROWS_SUM, BF16)
            pair_sums.append(both.reshape(kept.shape))
        send_sems, recv_sems, sums, lands, token = _chip_scatter_start(pair_sums)
        return token, (send_sems, recv_sems, sums, lands)

    grad_x, small, _, _, _, _, in_flight = _local_step(
        x[0], loss_target[0], scale, shift, gate, norm_w, w_in_p, b_in_p, ml_conv_full, ml_conv_b, ml_norm_w,
        ssm_conv_full, ssm_conv_b, ssm_a_log, ssm_d, ssm_norm_w, None, None, None, final_w[None], start_exchange,
        late_weights, u)

    g2 = _allgather8(_pack([small[name] for name, _ in SMALL_SUMS]))
    total = dict(zip([name for name, _ in SMALL_SUMS], _unpack(_sum_parts(None, g2, g2.shape[1]), [s for _, s in SMALL_SUMS])))
    dmod_all = g2[:, :3 * D_MODEL // LANE].reshape(N_DEV, 3 * D_MODEL)
    grads = dict(total)
    grads["ada_b"] = total["mod"]
    grads["ml_conv_w"] = lax.dynamic_slice_in_dim(total["ml_conv_w"], chip * mlw_cols, mlw_cols, axis=1)
    grads["ssm_conv_w"] = lax.dynamic_slice_in_dim(total["ssm_conv_w"], chip * ssw_cols, ssw_cols, axis=1)
    grads["ada_w"] = _ada_bwd(c_all, lax.dynamic_slice_in_dim(dmod_all, chip * ada_cols, ada_cols, axis=1))

    pairs = []
    for both, rec in zip(*_chip_scatter_wait(*in_flight, grad_x)):
        pairs.append(_sum_parts(lax.dynamic_index_in_dim(both, chip, 0, keepdims=False), rec, ROWS_SUM, slot=pos[2]))
    for name, full in zip(("w_in", "w_proj_m", "w_proj_s", "w_out"), _pair_exchange(pairs)):
        grads[name] = full.reshape(-1, full.shape[-1])

    delta, new_m, new_v = {}, {}, {}
    for name in LARGE:
        if w[name].shape[-1] % LANE:
            flat = lambda a: a.reshape(a.shape[-2:]).T
            back = lambda a: a.T.reshape(w[name].shape)
            g_flat = flat(grads[name])
            delta[name], new_m[name], new_v[name] = (back(a) for a in _adamw(flat(w[name]), g_flat, flat(m[name]), flat(v[name]), LANE))
            grads[name] = back(g_flat)
        else:
            delta[name], new_m[name], new_v[name] = _adamw(w[name], grads[name], m[name], v[name], ROWS_ADAMW)
    rest = [name for name in WEIGHTS if name not in LARGE]
    packed = [_pack([t[name] for name in rest]) for t in (w, grads, m, v)]
    for out, buf in zip((delta, new_m, new_v), _adamw(*packed, packed[0].shape[0])):
        out.update(zip(rest, _unpack(buf, [w[name].shape for name in rest])))
    loss = total["loss"][0]
    return (loss, grad_x[None], *[grads[name].reshape(w[name].shape) for name in WEIGHTS], *[delta[name] for name in WEIGHTS],
            *[new_m[name] for name in WEIGHTS], *[new_v[name] for name in WEIGHTS])
```

```python
import math

import jax
import jax.numpy as jnp
from jax import lax
from jax.experimental import pallas as pl
from jax.experimental.pallas import tpu as pltpu

F32 = jnp.float32
BF16 = jnp.bfloat16
MESH = pl.DeviceIdType.MESH

D_MODEL = 1024
EPS = 1e-6
CONV_K = 4
ML_HEADS = 8
ML_DQK = 128
ML_DV = 256
SSM_HEADS = 32
SSM_HEADDIM = 64
SSM_GROUPS = 4
SSM_STATE = 128
IN_WIDTH = 15408
N_CHIPS = 4
N_DEV = 8
ADAM_LR, ADAM_B1, ADAM_B2, ADAM_EPS, ADAM_WD, ADAM_STEP = 0.001, 0.9, 0.999, 1e-08, 0.01, 10

O_O, O_ZM, O_ZS, O_MG, O_QK, O_V, O_XBC, O_IF, O_DT = 0, 2048, 4096, 6144, 8192, 10240, 12288, 15360, 15616
SMALL_W = 256
NP = 15872
LANE = 128
CHUNK = 128
NEG = -1e30
VMEM_LIMIT = 48 * 1024 * 1024
MERGE_VMEM = 60 * 1024 * 1024
ROWS_ELEMENTWISE = 512
ROWS_MATMUL = 2048
ROWS_WIDE_MATMUL = 4096
COLS_MATMUL = 512
ROWS_POST = 128
ROWS_MERGE = 256
ROWS_SUM = 32
ROWS_ADAMW = 64


def _cparams(*sem, vmem=VMEM_LIMIT):
    return pltpu.CompilerParams(dimension_semantics=sem, vmem_limit_bytes=vmem)


def _pad_cols(w):
    z = lambda n: jnp.zeros(w.shape[:-1] + (n,), w.dtype)
    return jnp.concatenate([w[..., 4096:8192], w[..., 11280:13328], w[..., 13360:15408], w[..., :4096], w[..., 8208:11280],
                            w[..., 8192:8208], z(SMALL_W - 16), w[..., 13328:13360], z(SMALL_W - 32)], axis=-1)


def _unpad_cols(g):
    return jnp.concatenate([g[..., O_QK:O_QK + 4096], g[..., O_O:O_O + 4096], g[..., O_IF:O_IF + 16],
                            g[..., O_XBC:O_XBC + 3072], g[..., O_ZS:O_ZS + 2048], g[..., O_DT:O_DT + 32],
                            g[..., O_MG:O_MG + 2048]], axis=-1)


PADDED_SEGMENTS = ((4096, 8192, 0), (11280, 13328, 0), (13360, 15408, 0), (0, 4096, 0), (8208, 11280, 0),
                   (8192, 8208, SMALL_W - 16), (13328, 13360, SMALL_W - 32))
SHARD_W = IN_WIDTH // N_CHIPS


def _shards_to_padded(shards):
    parts = []
    for first, last, pad in PADDED_SEGMENTS:
        for j in range(N_CHIPS):
            lo, hi = max(first, j * SHARD_W), min(last, (j + 1) * SHARD_W)
            if lo < hi:
                parts.append(shards[j][:, lo - j * SHARD_W:hi - j * SHARD_W])
        if pad:
            parts.append(jnp.zeros((shards.shape[1], pad), shards.dtype))
    return jnp.concatenate(parts, axis=1)


def _padded_to_shards(g):
    offsets, off = {}, 0
    for first, last, pad in PADDED_SEGMENTS:
        offsets[first] = off
        off += last - first + pad
    shards = []
    for j in range(N_CHIPS):
        parts = []
        for first, last, _ in sorted(PADDED_SEGMENTS):
            lo, hi = max(first, j * SHARD_W), min(last, (j + 1) * SHARD_W)
            if lo < hi:
                parts.append(g[:, offsets[first] + lo - first:offsets[first] + hi - first])
        shards.append(jnp.concatenate(parts, axis=1))
    return jnp.stack(shards)


def _sigmoid(x):
    return 0.5 * jnp.tanh(0.5 * x) + 0.5


def _silu(x):
    return x * _sigmoid(x)


def _dsilu(x):
    s = _sigmoid(x)
    return s + x * s * (1.0 - s)


def _softplus(x):
    return jnp.maximum(x, 0.0) + jnp.log(1.0 + jnp.exp(-jnp.abs(x)))


def _logsigmoid(x):
    return jnp.minimum(x, 0.0) - jnp.log(1.0 + jnp.exp(-jnp.abs(x)))


def _dot(a, b, dims):
    return lax.dot_general(a, b, (dims, ((), ())), preferred_element_type=F32)


def _nn(a, b):
    return _dot(a, b, ((1,), (0,)))


def _nt(a, b):
    return _dot(a, b, ((1,), (1,)))


def _tn(a, b):
    return _dot(a, b, ((0,), (0,)))


def _bf(x):
    return x.astype(BF16)


def _split(x, terms):
    parts = []
    for _ in range(terms):
        part = _bf(x)
        parts.append(part)
        x = x - part.astype(F32)
    return parts


def _pick_right(x, pick, terms):
    pick = _bf(pick)
    out = None
    for part in _split(x, terms):
        out = _nn(part, pick) if out is None else out + _nn(part, pick)
    return out


def _pick_left(pick, x, terms):
    pick = _bf(pick)
    out = None
    for part in _split(x, terms):
        out = _nn(pick, part) if out is None else out + _nn(pick, part)
    return out


def _lane_col(x, lane):
    idx = lax.broadcasted_iota(jnp.int32, x.shape, 1)
    return jnp.sum(jnp.where(idx == lane, x, 0.0), axis=1, keepdims=True)


def _tri(n, upper):
    r = lax.broadcasted_iota(jnp.int32, (n, n), 0)
    c = lax.broadcasted_iota(jnp.int32, (n, n), 1)
    return jnp.where((r <= c) if upper else (r >= c), 1.0, 0.0).astype(F32)


def _eye(n):
    return jnp.where(lax.broadcasted_iota(jnp.int32, (n, n), 0) == lax.broadcasted_iota(jnp.int32, (n, n), 1), 1.0, 0.0)


def _sum_all(x):
    return jnp.sum(jnp.sum(x, axis=1, keepdims=True), axis=0, keepdims=True)


def _crossing(p):
    L = p.shape[0]
    below = _nn(_bf(_tri(L, True)), _bf(p))
    strict = lax.broadcasted_iota(jnp.int32, (L, L), 0) > lax.broadcasted_iota(jnp.int32, (L, L), 1)
    return [jnp.sum(jnp.where(strict, below[:, b * L:(b + 1) * L], 0.0), axis=1, keepdims=True)
            for b in range(p.shape[1] // L)]


def _matmul_bias(a, w, bias, tm, tn, col0, ncols, dtype):
    m, k = a.shape
    j0 = col0 // tn

    def body(a_ref, w_ref, b_ref, o_ref):
        o_ref[...] = (_nn(a_ref[...], w_ref[...]) + b_ref[...]).astype(dtype)

    return pl.pallas_call(
        body, name="matmul_bias", grid=(m // tm, ncols // tn),
        in_specs=[pl.BlockSpec((tm, k), lambda i, j: (i, 0)), pl.BlockSpec((k, tn), lambda i, j: (0, j0 + j)),
                  pl.BlockSpec((1, tn), lambda i, j: (0, j0 + j))],
        out_specs=pl.BlockSpec((tm, tn), lambda i, j: (i, j)),
        out_shape=jax.ShapeDtypeStruct((m, ncols), dtype),
        compiler_params=_cparams("parallel", "arbitrary"))(a, w, bias)


def _matmul_nt(a, w, tm, tk, after=None):
    m, n = a.shape
    k = w.shape[0]

    def body(a_ref, w_ref, *rest):
        o_ref = rest[-1]

        @pl.when(pl.program_id(1) == 0)
        def _():
            o_ref[...] = jnp.zeros_like(o_ref)
        o_ref[...] += _nt(a_ref[...], w_ref[...])

    extra = [] if after is None else [after]
    return pl.pallas_call(
        body, name="matmul_nt", grid=(m // tm, n // tk),
        in_specs=[pl.BlockSpec((tm, tk), lambda i, j: (i, j)), pl.BlockSpec((k, tk), lambda i, j: (0, j))]
        + [pl.BlockSpec(memory_space=pl.ANY)] * len(extra),
        out_specs=pl.BlockSpec((tm, k), lambda i, j: (i, 0)),
        out_shape=jax.ShapeDtypeStruct((m, k), F32),
        compiler_params=_cparams("parallel", "arbitrary"))(a, w, *extra)


def _matmul_tn(a, b, tm, tn, with_colsum=False, a_is_transposed=False, dtype=F32):
    k, m = a.shape if a_is_transposed else a.shape[::-1]
    n = b.shape[1]
    steps = m // tm

    def body(a_ref, b_ref, o_ref, *rest):
        acc_ref = o_ref if dtype == F32 else rest[-1]
        first = pl.program_id(1) == 0

        @pl.when(first)
        def _():
            acc_ref[...] = jnp.zeros_like(acc_ref)
        acc_ref[...] += _nn(a_ref[...], b_ref[...]) if a_is_transposed else _tn(a_ref[...], b_ref[...])
        if dtype != F32:
            @pl.when(pl.program_id(1) == steps - 1)
            def _():
                o_ref[...] = acc_ref[...].astype(dtype)
        if with_colsum:
            s_ref = rest[0]

            @pl.when(first)
            def _():
                s_ref[...] = jnp.zeros_like(s_ref)
            s_ref[...] += jnp.sum(b_ref[...].astype(F32), axis=0, keepdims=True)

    out_specs = [pl.BlockSpec((k, tn), lambda j, i: (0, j))]
    out_shape = [jax.ShapeDtypeStruct((k, n), dtype)]
    if with_colsum:
        out_specs.append(pl.BlockSpec((1, tn), lambda j, i: (0, j)))
        out_shape.append(jax.ShapeDtypeStruct((1, n), F32))
    out = pl.pallas_call(
        body, name="matmul_tn", grid=(n // tn, m // tm),
        in_specs=[pl.BlockSpec((k, tm), lambda j, i: (0, i)) if a_is_transposed else pl.BlockSpec((tm, k), lambda j, i: (i, 0)),
                  pl.BlockSpec((tm, tn), lambda j, i: (i, j))],
        out_specs=out_specs, out_shape=out_shape,
        scratch_shapes=[] if dtype == F32 else [pltpu.VMEM((k, tn), F32)],
        compiler_params=_cparams("parallel", "arbitrary"))(a, b)
    return out if with_colsum else out[0]


def _ada_fwd(c_all, ada_w, ada_b):
    def body(c_ref, w_ref, b_ref, o_ref):
        o_ref[...] = _nn(_bf(_silu(c_ref[...])), _bf(w_ref[...])) + b_ref[...]

    return pl.pallas_call(body, name="ada_fwd", out_shape=jax.ShapeDtypeStruct((c_all.shape[0], ada_w.shape[1]), F32),
                          compiler_params=_cparams())(c_all, ada_w, ada_b)


def _ada_bwd(c_all, dmod):
    def body(c_ref, d_ref, o_ref):
        o_ref[...] = _tn(_bf(_silu(c_ref[...])), _bf(d_ref[...]))

    return pl.pallas_call(body, name="ada_bwd", out_shape=jax.ShapeDtypeStruct((c_all.shape[1], dmod.shape[1]), F32),
                          compiler_params=_cparams())(c_all, dmod)


def _prenorm_fwd(x, norm_w, scale, shift, ts):
    s, d = x.shape

    def body(x_ref, nw_ref, sc_ref, sh_ref, u_ref):
        xv = x_ref[...]
        r = lax.rsqrt(jnp.mean(xv * xv, axis=1, keepdims=True) + EPS)
        u_ref[...] = _bf(xv * r * nw_ref[...] * (1.0 + sc_ref[...]) + sh_ref[...])

    row = pl.BlockSpec((1, d), lambda i: (0, 0))
    return pl.pallas_call(
        body, name="prenorm_fwd", grid=(s // ts,),
        in_specs=[pl.BlockSpec((ts, d), lambda i: (i, 0)), row, row, row],
        out_specs=pl.BlockSpec((ts, d), lambda i: (i, 0)), out_shape=jax.ShapeDtypeStruct((s, d), BF16),
        compiler_params=_cparams("parallel"))(x, norm_w, scale, shift)


def _prenorm_bwd(du, x, dxres, norm_w, scale, ts):
    s, d = x.shape

    def body(du_ref, x_ref, dr_ref, nw_ref, sc_ref, gx_ref, acc_ref):
        @pl.when(pl.program_id(0) == 0)
        def _():
            acc_ref[...] = jnp.zeros_like(acc_ref)
        xv, duv = x_ref[...], du_ref[...]
        r = lax.rsqrt(jnp.mean(xv * xv, axis=1, keepdims=True) + EPS)
        xn = xv * r
        nw, sc1 = nw_ref[...], 1.0 + sc_ref[...]
        dxn = duv * (nw * sc1)
        gx_ref[...] = r * (dxn - xn * jnp.mean(dxn * xn, axis=1, keepdims=True)) + dr_ref[...]
        t = duv * xn
        acc_ref[0:1, :] += jnp.sum(t, axis=0, keepdims=True) * sc1
        acc_ref[1:2, :] += jnp.sum(t, axis=0, keepdims=True) * nw
        acc_ref[2:3, :] += jnp.sum(duv, axis=0, keepdims=True)

    tile = pl.BlockSpec((ts, d), lambda i: (i, 0))
    row = pl.BlockSpec((1, d), lambda i: (0, 0))
    return pl.pallas_call(
        body, name="prenorm_bwd", grid=(s // ts,),
        in_specs=[tile, tile, tile, row, row],
        out_specs=[tile, pl.BlockSpec((8, d), lambda i: (0, 0))],
        out_shape=[jax.ShapeDtypeStruct((s, d), F32), jax.ShapeDtypeStruct((8, d), F32)],
        compiler_params=_cparams("arbitrary"))(du, x, dxres, norm_w, scale)


CONV_CB = 512


def _conv_taps(buf_ref, ts):
    return [buf_ref[pl.ds(8 - (CONV_K - 1) + j, ts), :] for j in range(CONV_K)]


def _conv_fwd(proj, col0, width, w8, b, ts):
    s = proj.shape[0]
    cb = CONV_CB
    nt = s // ts

    def body(x_ref, w_ref, b_ref, o_ref, ds_ref, buf_ref):
        @pl.when(pl.program_id(1) == 0)
        def _():
            buf_ref[0:8, :] = jnp.zeros((8, cb), F32)
        buf_ref[pl.ds(8, ts), :] = x_ref[...].astype(F32)
        acc = b_ref[...] + jnp.zeros((ts, cb), F32)
        for j, tap in enumerate(_conv_taps(buf_ref, ts)):
            acc = acc + tap * w_ref[j:j + 1, :]
        sg = _sigmoid(acc)
        o_ref[...] = acc * sg
        ds_ref[...] = _bf(sg + acc * sg * (1.0 - sg))
        buf_ref[0:8, :] = buf_ref[pl.ds(ts, 8), :]

    c0 = col0 // cb
    tile = pl.BlockSpec((ts, cb), lambda c, i: (i, c))
    return pl.pallas_call(
        body, name="conv_fwd", grid=(width // cb, nt),
        in_specs=[pl.BlockSpec((ts, cb), lambda c, i: (i, c0 + c)), pl.BlockSpec((8, cb), lambda c, i: (0, c)),
                  pl.BlockSpec((1, cb), lambda c, i: (0, c))],
        out_specs=[tile, tile],
        out_shape=[jax.ShapeDtypeStruct((s, width), F32), jax.ShapeDtypeStruct((s, width), BF16)],
        scratch_shapes=[pltpu.VMEM((ts + 8, cb), F32)],
        compiler_params=_cparams("parallel", "arbitrary"))(proj, w8, b)


def _conv_bwd(proj, col0, width, w8, dact, dpost, dproj, ts):
    s = proj.shape[0]
    cb = CONV_CB
    nt = s // ts
    c0 = col0 // cb

    def body(x_ref, da_ref, dp_ref, w_ref, _, dx_ref, acc_ref, dbuf_ref):
        @pl.when(pl.program_id(1) == 0)
        def _():
            acc_ref[...] = jnp.zeros_like(acc_ref)
            dbuf_ref[pl.ds(ts, 8), :] = jnp.zeros((8, cb), F32)
        dconv = dp_ref[...].astype(F32) * da_ref[...].astype(F32)
        acc_ref[CONV_K:CONV_K + 1, :] += jnp.sum(dconv, axis=0, keepdims=True)
        dbuf_ref[pl.ds(0, ts), :] = dconv
        xv = x_ref[...].astype(F32)
        dx = jnp.zeros((ts, cb), F32)
        for j in range(CONV_K):
            shifted = dbuf_ref[pl.ds(CONV_K - 1 - j, ts), :]
            dx = dx + shifted * w_ref[j:j + 1, :]
            acc_ref[j:j + 1, :] += jnp.sum(xv * shifted, axis=0, keepdims=True)
        dx_ref[...] = _bf(dx)
        dbuf_ref[pl.ds(ts, 8), :] = dconv[0:8, :]

    tile = pl.BlockSpec((ts, cb), lambda c, i: (nt - 1 - i, c))
    wide = pl.BlockSpec((ts, cb), lambda c, i: (nt - 1 - i, c0 + c))
    return pl.pallas_call(
        body, name="conv_bwd", grid=(width // cb, nt),
        in_specs=[wide, tile, tile, pl.BlockSpec((8, cb), lambda c, i: (0, c)), pl.BlockSpec(memory_space=pl.ANY)],
        out_specs=[wide, pl.BlockSpec((8, cb), lambda c, i: (0, c))],
        out_shape=[jax.ShapeDtypeStruct(dproj.shape, dproj.dtype), jax.ShapeDtypeStruct((8, width), F32)],
        input_output_aliases={4: 0},
        scratch_shapes=[pltpu.VMEM((ts + 8, cb), F32)],
        compiler_params=_cparams("parallel", "arbitrary"))(proj, dact, dpost, w8, dproj)


def _mlstm_gates(gif_ref, gt_ref, a_scr, at_scr):
    L = gif_ref.shape[0]
    fb = _logsigmoid(gif_ref[...])
    a_scr[...] = _pick_left(_tri(L, False), fb, 3)
    at_scr[...] = _pick_right(_logsigmoid(gt_ref[...]), _tri(L, True), 3)
    return jnp.sum(fb, axis=0, keepdims=True)


def _mlstm_head(h, qk_ref, v_ref, gif, gt_ref, a, at_scr, a_last_row, c_mat, n_row, m_prev):
    L = gif.shape[0]
    q = qk_ref[:, h * ML_DQK:(h + 1) * ML_DQK] * (ML_DQK ** -0.5)
    k = qk_ref[:, (ML_HEADS + h) * ML_DQK:(ML_HEADS + h + 1) * ML_DQK]
    v = v_ref[:, h * ML_DV:(h + 1) * ML_DV]
    i_col, a_col = _lane_col(gif, h), _lane_col(a, ML_HEADS + h)
    i_row, a_row = gt_ref[h:h + 1, :], at_scr[ML_HEADS + h:ML_HEADS + h + 1, :]
    causal = lax.broadcasted_iota(jnp.int32, (L, L), 0) >= lax.broadcasted_iota(jnp.int32, (L, L), 1)
    dmat = jnp.where(causal, a_col - a_row + i_row, NEG)
    inter = a_col + m_prev
    m_t = jnp.maximum(inter, jnp.max(dmat, axis=1, keepdims=True))
    w_intra = jnp.exp(dmat - m_t)
    w_inter = jnp.exp(inter - m_t)
    sc = _nt(_bf(q), _bf(k)) * w_intra
    den = jnp.sum(sc, axis=1, keepdims=True) + w_inter * jnp.sum(q * n_row, axis=1, keepdims=True)
    floor = jnp.exp(-m_t)
    a_last = _lane_col(a_last_row, ML_HEADS + h)
    g = a_last - a_col + i_col
    m_new = jnp.maximum(a_last + m_prev, jnp.max(g, axis=0, keepdims=True))
    wk = jnp.exp(g - m_new)
    decay = jnp.exp(a_last + m_prev - m_new)
    return dict(q=q, k=k, v=v, w_intra=w_intra, w_inter=w_inter, sc=sc, den=den, floor=floor, m_new=m_new, wk=wk,
                decay=decay)


def _state_tile(n_row, m11):
    r = lax.broadcasted_iota(jnp.int32, (8, LANE), 0)
    return jnp.where(r == 0, n_row, jnp.where(r == 1, m11, 0.0))


def _mlstm_fwd(qk, proj, gates, gt):
    s = qk.shape[0]
    L = CHUNK
    nc = s // L

    def body(qk_ref, v_ref, gif_ref, gt_ref, h_ref, cst_ref, nm_ref, c_scr, nm_scr, a_scr, at_scr):
        @pl.when(pl.program_id(0) == 0)
        def _():
            c_scr[...] = jnp.zeros_like(c_scr)
            nm_scr[...] = jnp.zeros_like(nm_scr)
        a_last_row = _mlstm_gates(gif_ref, gt_ref, a_scr, at_scr)
        gif, a = gif_ref[...], a_scr[...]
        for h in range(ML_HEADS):
            c_mat, n_row = c_scr[h], nm_scr[h, 0:1, :]
            m_prev = jnp.max(nm_scr[h, 1:2, :], axis=1, keepdims=True)
            cst_ref[0, h] = c_mat
            nm_ref[0, h] = nm_scr[h]
            t = _mlstm_head(h, qk_ref, v_ref, gif, gt_ref, a, at_scr, a_last_row, c_mat, n_row, m_prev)
            num = _nn(_bf(t["sc"]), _bf(t["v"])) + t["w_inter"] * _nn(_bf(t["q"]), _bf(c_mat))
            h_ref[:, h * ML_DV:(h + 1) * ML_DV] = _bf(num * (1.0 / jnp.maximum(jnp.abs(t["den"]), t["floor"])))
            kw = t["k"] * t["wk"]
            c_scr[h] = t["decay"] * c_mat + _tn(_bf(kw), _bf(t["v"]))
            nm_scr[h] = _state_tile(t["decay"] * n_row + jnp.sum(kw, axis=0, keepdims=True), t["m_new"])

    return pl.pallas_call(
        body, name="mlstm_fwd", grid=(nc,),
        in_specs=[pl.BlockSpec((L, 2048), lambda c: (c, 0)), pl.BlockSpec((L, 2048), lambda c: (c, O_V // 2048)),
                  pl.BlockSpec((L, LANE), lambda c: (c, 0)), pl.BlockSpec((LANE, L), lambda c: (0, c))],
        out_specs=[pl.BlockSpec((L, 2048), lambda c: (c, 0)),
                   pl.BlockSpec((1, ML_HEADS, ML_DQK, ML_DV), lambda c: (c, 0, 0, 0)),
                   pl.BlockSpec((1, ML_HEADS, 8, LANE), lambda c: (c, 0, 0, 0))],
        out_shape=[jax.ShapeDtypeStruct((s, 2048), BF16), jax.ShapeDtypeStruct((nc, ML_HEADS, ML_DQK, ML_DV), F32),
                   jax.ShapeDtypeStruct((nc, ML_HEADS, 8, LANE), F32)],
        scratch_shapes=[pltpu.VMEM((ML_HEADS, ML_DQK, ML_DV), F32), pltpu.VMEM((ML_HEADS, 8, LANE), F32),
                        pltpu.VMEM((L, LANE), F32), pltpu.VMEM((LANE, L), F32)],
        compiler_params=_cparams("arbitrary"))(qk, proj, gates, gt)


def _mlstm_bwd(qk, proj, gates, gt, hout, dh, cst, nm, dproj):
    s = qk.shape[0]
    L = CHUNK
    nc = s // L

    def body(qk_ref, v_ref, gif_ref, gt_ref, h_ref, dh_ref, cst_ref, nm_ref, _, dqk_ref, dv_ref, dif_ref,
             dc_scr, dn_scr, a_scr, at_scr):
        @pl.when(pl.program_id(0) == 0)
        def _():
            dc_scr[...] = jnp.zeros_like(dc_scr)
            dn_scr[...] = jnp.zeros_like(dn_scr)
        a_last_row = _mlstm_gates(gif_ref, gt_ref, a_scr, at_scr)
        gif, a = gif_ref[...], a_scr[...]
        lane = lax.broadcasted_iota(jnp.int32, (L, LANE), 1)
        last = lax.broadcasted_iota(jnp.int32, (L, 1), 0) == L - 1
        di_tile = jnp.zeros((L, LANE), F32)
        cross = [jnp.zeros((L, LANE), F32)] * 3
        dlogw = []
        for h in range(ML_HEADS):
            c_mat, n_row = cst_ref[0, h], nm_ref[0, h, 0:1, :]
            m_prev = jnp.max(nm_ref[0, h, 1:2, :], axis=1, keepdims=True)
            t = _mlstm_head(h, qk_ref, v_ref, gif, gt_ref, a, at_scr, a_last_row, c_mat, n_row, m_prev)
            q, k, v, den = t["q"], t["k"], t["v"], t["den"]
            dhh = dh_ref[:, h * ML_DV:(h + 1) * ML_DV].astype(F32)
            hh = h_ref[:, h * ML_DV:(h + 1) * ML_DV].astype(F32)
            dnorm = jnp.maximum(jnp.abs(den), t["floor"])
            inv = 1.0 / dnorm
            dnum = dhh * inv
            d_dn = -jnp.sum(dhh * hh, axis=1, keepdims=True) * inv
            dden = jnp.where(jnp.abs(den) >= t["floor"], jnp.where(den >= 0.0, d_dn, -d_dn), 0.0)
            dsc = _nt(_bf(dnum), _bf(v)) + dden
            ds = dsc * t["w_intra"]
            dq_inter = t["w_inter"] * (_nt(_bf(dnum), _bf(c_mat)) + dden * n_row)
            dq = _nn(_bf(ds), _bf(k)) + dq_inter
            dc, dn_row = dc_scr[h], dn_scr[h, 0:1, :]
            dk_state = t["wk"] * (_nt(_bf(v), _bf(dc)) + dn_row)
            dk = _tn(_bf(ds), _bf(q)) + dk_state
            dv = _tn(_bf(t["sc"]), _bf(dnum)) + t["wk"] * _nn(_bf(k), _bf(dc))
            qi = q * t["w_inter"]
            dc_scr[h] = t["decay"] * dc + _tn(_bf(qi), _bf(dnum))
            dn_scr[h] = jnp.broadcast_to(t["decay"] * dn_row + jnp.sum(qi * dden, axis=0, keepdims=True), (8, LANE))
            dqk_ref[:, h * ML_DQK:(h + 1) * ML_DQK] = _bf(dq * (ML_DQK ** -0.5))
            dqk_ref[:, (ML_HEADS + h) * ML_DQK:(ML_HEADS + h + 1) * ML_DQK] = _bf(dk)
            dv_ref[:, h * ML_DV:(h + 1) * ML_DV] = _bf(dv)
            di_tile = di_tile + jnp.where(lane == h, jnp.sum(k * dk, axis=1, keepdims=True), 0.0)
            carried = t["decay"] * (_sum_all(dc * c_mat) + jnp.sum(dn_row * n_row, axis=1, keepdims=True))
            dlogw.append(dsc * t["sc"])
            parts = (jnp.sum(q * dq_inter, axis=1, keepdims=True) + jnp.where(last, carried, 0.0),
                     jnp.sum(k * dk_state, axis=1, keepdims=True))
            cross[1:] = [c + jnp.where(lane == ML_HEADS + h, p, 0.0) for c, p in zip(cross[1:], parts)]
        for h, col in enumerate(_crossing(jnp.concatenate(dlogw, axis=1))):
            cross[0] = cross[0] + jnp.where(lane == ML_HEADS + h, col, 0.0)
        dfb = cross[0] + _pick_left(_tri(L, True), cross[1], 2) + _pick_left(_tri(L, False) - _eye(L), cross[2], 2)
        dif_ref[:, 0:LANE] = _bf(di_tile + dfb * _sigmoid(-gif))
        dif_ref[:, LANE:SMALL_W] = jnp.zeros((L, SMALL_W - LANE), BF16)

    rev = lambda c: nc - 1 - c
    return pl.pallas_call(
        body, name="mlstm_bwd", grid=(nc,),
        in_specs=[pl.BlockSpec((L, 2048), lambda c: (rev(c), 0)), pl.BlockSpec((L, 2048), lambda c: (rev(c), O_V // 2048)),
                  pl.BlockSpec((L, LANE), lambda c: (rev(c), 0)), pl.BlockSpec((LANE, L), lambda c: (0, rev(c))),
                  pl.BlockSpec((L, 2048), lambda c: (rev(c), 0)), pl.BlockSpec((L, 2048), lambda c: (rev(c), 0)),
                  pl.BlockSpec((1, ML_HEADS, ML_DQK, ML_DV), lambda c: (rev(c), 0, 0, 0)),
                  pl.BlockSpec((1, ML_HEADS, 8, LANE), lambda c: (rev(c), 0, 0, 0)), pl.BlockSpec(memory_space=pl.ANY)],
        out_specs=[pl.BlockSpec((L, 2048), lambda c: (rev(c), 0)), pl.BlockSpec((L, 2048), lambda c: (rev(c), O_V // 2048)),
                   pl.BlockSpec((L, SMALL_W), lambda c: (rev(c), 0))],
        out_shape=[jax.ShapeDtypeStruct((s, 2048), BF16), jax.ShapeDtypeStruct(dproj.shape, dproj.dtype),
                   jax.ShapeDtypeStruct((s, SMALL_W), BF16)],
        input_output_aliases={8: 1},
        scratch_shapes=[pltpu.VMEM((ML_HEADS, ML_DQK, ML_DV), F32), pltpu.VMEM((ML_HEADS, 8, LANE), F32),
                        pltpu.VMEM((L, LANE), F32), pltpu.VMEM((LANE, L), F32)],
        compiler_params=_cparams("arbitrary"))(qk, proj, gates, gt, hout, dh, cst, nm, dproj)


GROUP_W = SSM_HEADS // SSM_GROUPS * SSM_HEADDIM
O_B = SSM_HEADS * SSM_HEADDIM
O_C = O_B + SSM_GROUPS * SSM_STATE


def _head_expand():
    r = jnp.arange(LANE)[:, None]
    c = jnp.arange(SSM_HEADS * SSM_HEADDIM)[None, :] // SSM_HEADDIM
    return (r == c).astype(F32)


def _ssd_gates(dt_ref, dtt_ref, alog_row_ref, alog_col_ref, at_scr):
    L = dt_ref.shape[0]
    dt = _softplus(dt_ref[...])
    acoef = -jnp.exp(alog_row_ref[...])
    a = _pick_left(_tri(L, False), dt * acoef, 3)
    at_scr[...] = _pick_right(_softplus(dtt_ref[...]) * (-jnp.exp(alog_col_ref[...])), _tri(L, True), 3)
    return dt, acoef, a


def _ssd_group(g, xbc_ref, dt, a, e_ref, ax_scr):
    eg = e_ref[:, g * GROUP_W:(g + 1) * GROUP_W]
    ax_scr[...] = _pick_right(a, eg, 3)
    ax = ax_scr[...]
    alx = ax_scr[ax.shape[0] - 1:ax.shape[0], :]
    dtx = _pick_right(dt, eg, 2)
    xg = xbc_ref[:, g * GROUP_W:(g + 1) * GROUP_W]
    bg = xbc_ref[:, O_B + g * SSM_STATE:O_B + (g + 1) * SSM_STATE]
    cg = xbc_ref[:, O_C + g * SSM_STATE:O_C + (g + 1) * SSM_STATE]
    return dict(ax=ax, alx=alx, dtx=dtx, xg=xg, bg=bg, cg=cg, xdt=xg * dtx, gmat=_nt(_bf(cg), _bf(bg)))


def _ssd_decay(hh, a, at_scr):
    L = a.shape[0]
    causal = lax.broadcasted_iota(jnp.int32, (L, L), 0) >= lax.broadcasted_iota(jnp.int32, (L, L), 1)
    return jnp.exp(jnp.where(causal, _lane_col(a, hh) - at_scr[hh:hh + 1, :], NEG))


def _ssd_fwd(xbc, gates, dtt, alog_row, alog_col, dskip_x, expand):
    s = xbc.shape[0]
    L = CHUNK
    nc = s // L
    half = SSM_HEADDIM

    def body(xbc_ref, dt_ref, dtt_ref, ar_ref, ac_ref, dk_ref, e_ref, y_ref, st_ref, st_scr, at_scr, ax_scr):
        @pl.when(pl.program_id(0) == 0)
        def _():
            st_scr[...] = jnp.zeros_like(st_scr)
        dt, _, a = _ssd_gates(dt_ref, dtt_ref, ar_ref, ac_ref, at_scr)
        lane = lax.broadcasted_iota(jnp.int32, (L, LANE), 1)
        for g in range(SSM_GROUPS):
            t = _ssd_group(g, xbc_ref, dt, a, e_ref, ax_scr)
            st = st_scr[g]
            st_ref[0, g] = st
            pairs = []
            for j in range(GROUP_W // LANE):
                xp = _bf(t["xdt"][:, j * LANE:(j + 1) * LANE])
                hh = g * (SSM_HEADS // SSM_GROUPS) + 2 * j
                both = jnp.concatenate([_bf(t["gmat"] * _ssd_decay(hh, a, at_scr)),
                                        _bf(t["gmat"] * _ssd_decay(hh + 1, a, at_scr))], axis=0)
                ys = _nn(both, xp)
                pairs.append(jnp.where(lane < half, ys[0:L], ys[L:2 * L]))
            y = jnp.concatenate(pairs, axis=1) + _nn(_bf(t["cg"]), _bf(st)) * jnp.exp(t["ax"])
            y_ref[:, g * GROUP_W:(g + 1) * GROUP_W] = _bf(y + dk_ref[:, g * GROUP_W:(g + 1) * GROUP_W] * t["xg"])
            wts = jnp.exp(t["alx"] - t["ax"])
            st_scr[g] = jnp.exp(t["alx"]) * st + _tn(_bf(t["bg"]), _bf(t["xdt"] * wts))

    row = lambda w: pl.BlockSpec((1, w), lambda c: (0, 0))
    return pl.pallas_call(
        body, name="ssd_fwd", grid=(nc,),
        in_specs=[pl.BlockSpec((L, 3072), lambda c: (c, 0)), pl.BlockSpec((L, LANE), lambda c: (c, (O_DT - O_IF) // LANE)),
                  pl.BlockSpec((LANE, L), lambda c: (0, c)), row(LANE), pl.BlockSpec((LANE, 1), lambda c: (0, 0)),
                  row(2048), pl.BlockSpec((LANE, 2048), lambda c: (0, 0))],
        out_specs=[pl.BlockSpec((L, 2048), lambda c: (c, 0)),
                   pl.BlockSpec((1, SSM_GROUPS, SSM_STATE, GROUP_W), lambda c: (c, 0, 0, 0))],
        out_shape=[jax.ShapeDtypeStruct((s, 2048), BF16),
                   jax.ShapeDtypeStruct((nc, SSM_GROUPS, SSM_STATE, GROUP_W), F32)],
        scratch_shapes=[pltpu.VMEM((SSM_GROUPS, SSM_STATE, GROUP_W), F32), pltpu.VMEM((LANE, L), F32),
                        pltpu.VMEM((L, GROUP_W), F32)],
        compiler_params=_cparams("arbitrary"))(xbc, gates, dtt, alog_row, alog_col, dskip_x, expand)


def _ssd_bwd(xbc, gates, dtt, alog_row, alog_col, dskip_x, expand, expand_t, dy, states):
    s = xbc.shape[0]
    L = CHUNK
    nc = s // L
    half = SSM_HEADDIM

    def body(xbc_ref, dt_ref, dtt_ref, ar_ref, ac_ref, dk_ref, e_ref, et_ref, dy_ref, st_ref,
             dxbc_ref, ddt_ref, accd_ref, acca_ref, dst_scr, at_scr, ax_scr):
        @pl.when(pl.program_id(0) == 0)
        def _():
            dst_scr[...] = jnp.zeros_like(dst_scr)
            accd_ref[...] = jnp.zeros_like(accd_ref)
            acca_ref[...] = jnp.zeros_like(acca_ref)
        dt, acoef, a = _ssd_gates(dt_ref, dtt_ref, ar_ref, ac_ref, at_scr)
        lane = lax.broadcasted_iota(jnp.int32, (L, LANE), 1)
        low = lane < half
        last = lax.broadcasted_iota(jnp.int32, (L, 1), 0) == L - 1
        cross = [jnp.zeros((L, LANE), F32)] * 3
        ddt_tile = jnp.zeros((L, LANE), F32)
        for g in range(SSM_GROUPS):
            t = _ssd_group(g, xbc_ref, dt, a, e_ref, ax_scr)
            xg, bg, cg, xdt, gmat = t["xg"], t["bg"], t["cg"], t["xdt"], t["gmat"]
            st, dst = st_ref[0, g], dst_scr[g]
            dyg = dy_ref[:, g * GROUP_W:(g + 1) * GROUP_W].astype(F32)
            ea, eal = jnp.exp(t["ax"]), jnp.exp(t["alx"])
            wts = jnp.exp(t["alx"] - t["ax"])
            dyi = dyg * ea
            y_inter = _nn(_bf(cg), _bf(st)) * ea
            dc = _nt(_bf(dyi), _bf(st))
            d_xdt_state = _nn(_bf(bg), _bf(dst)) * wts
            db = _nt(_bf(xdt * wts), _bf(dst))
            dst_scr[g] = eal * dst + _tn(_bf(cg), _bf(dyi))
            dg = jnp.zeros((L, L), F32)
            dx_pairs, dlogw = [], []
            for j in range(GROUP_W // LANE):
                xp = _bf(xdt[:, j * LANE:(j + 1) * LANE])
                dyp = dyg[:, j * LANE:(j + 1) * LANE]
                hh = g * (SSM_HEADS // SSM_GROUPS) + 2 * j
                decs = [_ssd_decay(hh, a, at_scr), _ssd_decay(hh + 1, a, at_scr)]
                ws = [gmat * decs[0], gmat * decs[1]]
                dxs = _tn(_bf(jnp.concatenate(ws, axis=1)), _bf(dyp))
                dws = _nt(_bf(jnp.concatenate([jnp.where(low, dyp, 0.0), jnp.where(low, 0.0, dyp)], axis=0)), xp)
                dw0, dw1 = dws[0:L], dws[L:2 * L]
                dg = dg + dw0 * decs[0] + dw1 * decs[1]
                dlogw += [dw0 * ws[0], dw1 * ws[1]]
                dx_pairs.append(jnp.where(low, dxs[0:L], dxs[L:2 * L]))
            for b, col in enumerate(_crossing(jnp.concatenate(dlogw, axis=1))):
                cross[0] = cross[0] + jnp.where(lane == g * (SSM_HEADS // SSM_GROUPS) + b, col, 0.0)
            d_xdt = d_xdt_state + jnp.concatenate(dx_pairs, axis=1)
            dc = dc + _nn(_bf(dg), _bf(bg))
            db = db + _tn(_bf(dg), _bf(cg))
            etg = et_ref[g * GROUP_W:(g + 1) * GROUP_W, :]
            carried = jnp.sum(dst * st, axis=0, keepdims=True) * eal
            cross[1] = cross[1] + _pick_right(dyg * y_inter + jnp.where(last, carried, 0.0), etg, 2)
            cross[2] = cross[2] + _pick_right(xdt * d_xdt_state, etg, 2)
            ddt_tile = ddt_tile + _pick_right(d_xdt * xg, etg, 2)
            dxbc_ref[:, g * GROUP_W:(g + 1) * GROUP_W] = _bf(d_xdt * t["dtx"] + dk_ref[:, g * GROUP_W:(g + 1) * GROUP_W] * dyg)
            dxbc_ref[:, O_B + g * SSM_STATE:O_B + (g + 1) * SSM_STATE] = _bf(db)
            dxbc_ref[:, O_C + g * SSM_STATE:O_C + (g + 1) * SSM_STATE] = _bf(dc)
            accd_ref[0:1, g * GROUP_W:(g + 1) * GROUP_W] += jnp.sum(dyg * xg, axis=0, keepdims=True)
        d_da = cross[0] + _pick_left(_tri(L, True), cross[1], 2) + _pick_left(_tri(L, False) - _eye(L), cross[2], 2)
        acca_ref[0:1, :] += jnp.sum(d_da * dt, axis=0, keepdims=True)
        ddt_ref[:, 0:LANE] = _bf((ddt_tile + d_da * acoef) * _sigmoid(dt_ref[...]))
        ddt_ref[:, LANE:SMALL_W] = jnp.zeros((L, SMALL_W - LANE), BF16)

    rev = lambda c: nc - 1 - c
    row = lambda w: pl.BlockSpec((1, w), lambda c: (0, 0))
    return pl.pallas_call(
        body, name="ssd_bwd", grid=(nc,),
        in_specs=[pl.BlockSpec((L, 3072), lambda c: (rev(c), 0)), pl.BlockSpec((L, LANE), lambda c: (rev(c), (O_DT - O_IF) // LANE)),
                  pl.BlockSpec((LANE, L), lambda c: (0, rev(c))), row(LANE), pl.BlockSpec((LANE, 1), lambda c: (0, 0)),
                  row(2048), pl.BlockSpec((LANE, 2048), lambda c: (0, 0)), pl.BlockSpec((2048, LANE), lambda c: (0, 0)),
                  pl.BlockSpec((L, 2048), lambda c: (rev(c), 0)),
                  pl.BlockSpec((1, SSM_GROUPS, SSM_STATE, GROUP_W), lambda c: (rev(c), 0, 0, 0))],
        out_specs=[pl.BlockSpec((L, 3072), lambda c: (rev(c), 0)), pl.BlockSpec((L, SMALL_W), lambda c: (rev(c), 0)),
                   pl.BlockSpec((8, 2048), lambda c: (0, 0)), pl.BlockSpec((8, LANE), lambda c: (0, 0))],
        out_shape=[jax.ShapeDtypeStruct((s, 3072), BF16), jax.ShapeDtypeStruct((s, SMALL_W), BF16),
                   jax.ShapeDtypeStruct((8, 2048), F32), jax.ShapeDtypeStruct((8, LANE), F32)],
        scratch_shapes=[pltpu.VMEM((SSM_GROUPS, SSM_STATE, GROUP_W), F32),
                        pltpu.VMEM((LANE, L), F32), pltpu.VMEM((L, GROUP_W), F32)],
        compiler_params=_cparams("arbitrary"))(xbc, gates, dtt, alog_row, alog_col, dskip_x, expand, expand_t, dy, states)


def _group_norm(v, width):
    outs, rs = [], []
    for k in range(v.shape[1] // width):
        blk = v[:, k * width:(k + 1) * width]
        r = lax.rsqrt(jnp.mean(blk * blk, axis=1, keepdims=True) + EPS)
        outs.append(blk * r)
        rs.append(jnp.broadcast_to(r, blk.shape))
    return jnp.concatenate(outs, axis=1), jnp.concatenate(rs, axis=1)


def _group_mean(v, width):
    return jnp.concatenate([jnp.broadcast_to(jnp.mean(v[:, k * width:(k + 1) * width], axis=1, keepdims=True),
                                             (v.shape[0], width)) for k in range(v.shape[1] // width)], axis=1)


def _post_fwd(hm, yssd, proj, ml_norm_w, ssm_norm_w, ts):
    s = hm.shape[0]

    def body(h_ref, ys_ref, o_ref, zm_ref, zs_ref, wm_ref, ws_ref, ym_ref, yso_ref):
        hn, _ = _group_norm(h_ref[...].astype(F32), ML_DV)
        ym_ref[...] = _bf(_sigmoid(o_ref[...].astype(F32)) * hn * wm_ref[...] * _silu(zm_ref[...].astype(F32)))
        pn, _ = _group_norm(ys_ref[...].astype(F32) * _silu(zs_ref[...].astype(F32)), GROUP_W)
        yso_ref[...] = _bf(pn * ws_ref[...])

    tile = pl.BlockSpec((ts, 2048), lambda i: (i, 0))
    col = lambda off: pl.BlockSpec((ts, 2048), lambda i: (i, off // 2048))
    row = pl.BlockSpec((1, 2048), lambda i: (0, 0))
    return pl.pallas_call(
        body, name="post_fwd", grid=(s // ts,),
        in_specs=[tile, tile, col(O_O), col(O_ZM), col(O_ZS), row, row],
        out_specs=[tile, tile],
        out_shape=[jax.ShapeDtypeStruct((s, 2048), BF16)] * 2,
        compiler_params=_cparams("parallel"))(hm, yssd, proj, proj, proj, ml_norm_w, ssm_norm_w)


def _post_bwd(dym, dys, hm, yssd, proj, ml_norm_w, ssm_norm_w, dproj, ts):
    s = hm.shape[0]

    def body(dym_ref, dys_ref, h_ref, ys_ref, o_ref, zm_ref, zs_ref, wm_ref, ws_ref, _,
             dh_ref, dyssd_ref, dp_ref, acc_ref):
        @pl.when(pl.program_id(0) == 0)
        def _():
            acc_ref[...] = jnp.zeros_like(acc_ref)
        hn, r = _group_norm(h_ref[...].astype(F32), ML_DV)
        so, zm, wm, d_ym = _sigmoid(o_ref[...].astype(F32)), zm_ref[...].astype(F32), wm_ref[...], dym_ref[...].astype(F32)
        sz = _silu(zm)
        hnw = hn * wm
        dp_ref[:, O_O:O_O + 2048] = _bf(d_ym * hnw * sz * so * (1.0 - so))
        dp_ref[:, O_ZM:O_ZM + 2048] = _bf(d_ym * so * hnw * _dsilu(zm))
        dhnw = d_ym * so * sz
        acc_ref[0:1, :] += jnp.sum(dhnw * hn, axis=0, keepdims=True)
        dhn = dhnw * wm
        dh_ref[...] = _bf(r * (dhn - hn * _group_mean(dhn * hn, ML_DV)))
        ysv, zs, d_ys = ys_ref[...].astype(F32), zs_ref[...].astype(F32), dys_ref[...].astype(F32)
        szs = _silu(zs)
        pn, r2 = _group_norm(ysv * szs, GROUP_W)
        acc_ref[1:2, :] += jnp.sum(d_ys * pn, axis=0, keepdims=True)
        dpn = d_ys * ws_ref[...]
        dp = r2 * (dpn - pn * _group_mean(dpn * pn, GROUP_W))
        dyssd_ref[...] = _bf(dp * szs)
        dp_ref[:, O_ZS:O_ZS + 2048] = _bf(dp * ysv * _dsilu(zs))

    tile = pl.BlockSpec((ts, 2048), lambda i: (i, 0))
    col = lambda off: pl.BlockSpec((ts, 2048), lambda i: (i, off // 2048))
    row = pl.BlockSpec((1, 2048), lambda i: (0, 0))
    sds = lambda dt: jax.ShapeDtypeStruct((s, 2048), dt)
    return pl.pallas_call(
        body, name="post_bwd", grid=(s // ts,),
        in_specs=[tile, tile, tile, tile, col(O_O), col(O_ZM), col(O_ZS), row, row, pl.BlockSpec(memory_space=pl.ANY)],
        out_specs=[tile, tile, pl.BlockSpec((ts, O_MG), lambda i: (i, 0)), pl.BlockSpec((8, 2048), lambda i: (0, 0))],
        out_shape=[sds(BF16), sds(BF16), jax.ShapeDtypeStruct(dproj.shape, dproj.dtype), jax.ShapeDtypeStruct((8, 2048), F32)],
        input_output_aliases={9: 2},
        compiler_params=_cparams("arbitrary"))(dym, dys, hm, yssd, proj, proj, proj, ml_norm_w, ssm_norm_w, dproj)


def _merge(x, ym, ys, proj, target, gate, final_w, wpm, wps, wo, ts):
    wpm_t, wps_t, wo_t = wpm.T, wps.T, wo.T
    s, d = x.shape

    def body(x_ref, ym_ref, ys_ref, mg_ref, t_ref, gate_ref, fw_ref, wpm_ref, wps_ref, wo_ref, wpmt_ref, wpst_ref, wot_ref,
             dres_ref, mer_ref, dmo_ref, dpm_ref, dps_ref, dym_ref, dys_ref, dmg_ref, acc_ref):
        @pl.when(pl.program_id(0) == 0)
        def _():
            acc_ref[...] = jnp.zeros_like(acc_ref)
        gm, gs = _sigmoid(mg_ref[:, 0:d].astype(F32)), _sigmoid(mg_ref[:, d:2 * d].astype(F32))
        pm = _nn(ym_ref[...], wpm_ref[...])
        ps = _nn(ys_ref[...], wps_ref[...])
        merged = _bf(gm * pm + gs * ps)
        mer_ref[...] = merged
        mo = _nn(merged, wo_ref[...])
        gate, fw = gate_ref[...], fw_ref[...]
        out = x_ref[...] + gate * mo
        r = lax.rsqrt(jnp.mean(out * out, axis=1, keepdims=True) + EPS)
        on = out * r
        diff = on * fw - t_ref[...]
        acc_ref[0:1, :] += jnp.sum(0.5 * jnp.sum(diff * diff, axis=1, keepdims=True) / d, axis=0, keepdims=True)
        dyv = diff * (1.0 / d)
        acc_ref[1:2, :] += jnp.sum(dyv * on, axis=0, keepdims=True)
        don = dyv * fw
        dout = r * (don - on * jnp.mean(don * on, axis=1, keepdims=True))
        dres_ref[...] = dout
        acc_ref[2:3, :] += jnp.sum(dout * mo, axis=0, keepdims=True)
        dmo = _bf(dout * gate)
        dmo_ref[...] = dmo
        dmer = _nn(dmo, wot_ref[...])
        dpm, dps = _bf(dmer * gm), _bf(dmer * gs)
        dpm_ref[...] = dpm
        dps_ref[...] = dps
        dmg_ref[:, 0:d] = _bf(dmer * pm * gm * (1.0 - gm))
        dmg_ref[:, d:2 * d] = _bf(dmer * ps * gs * (1.0 - gs))
        dym_ref[...] = _bf(_nn(dpm, wpmt_ref[...]))
        dys_ref[...] = _bf(_nn(dps, wpst_ref[...]))

    t1 = pl.BlockSpec((ts, d), lambda i: (i, 0))
    t2 = pl.BlockSpec((ts, 2 * d), lambda i: (i, 0))
    row = pl.BlockSpec((1, d), lambda i: (0, 0))
    whole = pl.BlockSpec(memory_space=pltpu.VMEM)
    sd = lambda w, dt: jax.ShapeDtypeStruct((s, w), dt)
    return pl.pallas_call(
        body, name="merge_fwd_bwd", grid=(s // ts,),
        in_specs=[t1, t2, t2, pl.BlockSpec((ts, 2 * d), lambda i: (i, O_MG // (2 * d))), t1, row, row] + [whole] * 6,
        out_specs=[t1, t1, t1, t1, t1, t2, t2, pl.BlockSpec((ts, 2 * d), lambda i: (i, O_MG // (2 * d))),
                   pl.BlockSpec((8, d), lambda i: (0, 0))],
        out_shape=[sd(d, F32), sd(d, BF16), sd(d, BF16), sd(d, BF16), sd(d, BF16), sd(2 * d, BF16), sd(2 * d, BF16),
                   sd(NP, BF16), jax.ShapeDtypeStruct((8, d), F32)],
        compiler_params=_cparams("arbitrary", vmem=MERGE_VMEM))(x, ym, ys, proj, target, gate, final_w, wpm, wps, wo, wpm_t, wps_t, wo_t)


def _adamw(w, g, m, v, tr):
    if w.ndim == 2 and w.shape[0] % 8:
        tile, steps = pl.BlockSpec((w.shape[0], tr), lambda i: (0, i)), w.shape[1] // tr
    else:
        lead = (None,) * (w.ndim - 2)
        tile, steps = pl.BlockSpec(lead + (tr, w.shape[-1]), lambda i: (0,) * len(lead) + (i, 0)), w.shape[-2] // tr

    def body(w_ref, g_ref, m_ref, v_ref, d_ref, nm_ref, nv_ref):
        gv = g_ref[...]
        m2 = ADAM_B1 * m_ref[...] + (1.0 - ADAM_B1) * gv
        v2 = ADAM_B2 * v_ref[...] + (1.0 - ADAM_B2) * (gv * gv)
        m_hat = m2 / (1.0 - ADAM_B1 ** ADAM_STEP)
        v_hat = v2 / (1.0 - ADAM_B2 ** ADAM_STEP)
        d_ref[...] = -ADAM_LR * (m_hat / (jnp.sqrt(v_hat) + ADAM_EPS) + ADAM_WD * w_ref[...])
        nm_ref[...] = m2
        nv_ref[...] = v2

    return pl.pallas_call(
        body, name="adamw", grid=(steps,), in_specs=[tile] * 4, out_specs=[tile] * 3,
        out_shape=[jax.ShapeDtypeStruct(w.shape, F32)] * 3,
        compiler_params=_cparams("parallel"))(w, g.reshape(w.shape), m, v)


def _sum_parts(own, parts, tr, dtype=F32, slot=None):
    p, rows, cols = parts.shape

    def body(*refs):
        p_ref, o_ref = refs[-2], refs[-1]
        first = None if own is None else refs[-3]
        acc = p_ref[0].astype(F32) if first is None else first[...].astype(F32) + p_ref[0].astype(F32)
        for i in range(1, p):
            acc = acc + p_ref[i].astype(F32)
        o_ref[...] = acc.astype(dtype)

    args = ([] if own is None else [own]) + [parts]
    if slot is None:
        tile = pl.BlockSpec((tr, cols), lambda i: (i, 0))
        ins = ([] if own is None else [tile]) + [pl.BlockSpec((p, tr, cols), lambda i: (0, i, 0))]
        return pl.pallas_call(
            body, name="sum_parts", grid=(rows // tr,), in_specs=ins, out_specs=tile,
            out_shape=jax.ShapeDtypeStruct((rows, cols), dtype), compiler_params=_cparams("parallel"))(*args)
    tile = pl.BlockSpec((tr, cols), lambda i, s: (i, 0))
    ins = ([] if own is None else [tile]) + [pl.BlockSpec((p, tr, cols), lambda i, s: (0, i, 0))]
    return pl.pallas_call(
        body, name="sum_parts_half", out_shape=jax.ShapeDtypeStruct((2, rows, cols), dtype),
        grid_spec=pltpu.PrefetchScalarGridSpec(
            num_scalar_prefetch=1, grid=(rows // tr,), in_specs=ins,
            out_specs=pl.BlockSpec((None, tr, cols), lambda i, s: (s[0], i, 0))),
        compiler_params=_cparams("parallel"))(jnp.reshape(slot, (1,)).astype(jnp.int32), *args)


def _position():
    return lax.axis_index("x"), lax.axis_index("y"), lax.axis_index("c")


def _flip(pos, k):
    return tuple(1 - p if (k >> s) & 1 else p for p, s in zip(pos, (2, 1, 0)))


def _allgather8(block):
    rows, cols = block.shape

    def body(x_ref, o_ref, send_sems, recv_sems, local_sem):
        pos = _position()
        me = 4 * pos[0] + 2 * pos[1] + pos[2]
        mine = pltpu.make_async_copy(x_ref, o_ref.at[me], local_sem)
        mine.start()
        copies = [pltpu.make_async_remote_copy(src_ref=x_ref, dst_ref=o_ref.at[me], send_sem=send_sems.at[k - 1],
                                               recv_sem=recv_sems.at[k - 1], device_id=_flip(pos, k), device_id_type=MESH)
                  for k in range(1, N_DEV)]
        for cp in copies:
            cp.start()
        for cp in copies:
            cp.wait()
        mine.wait()

    vmem = pl.BlockSpec(memory_space=pltpu.VMEM)
    return pl.pallas_call(
        body, name="allgather8", in_specs=[vmem], out_specs=vmem,
        out_shape=jax.ShapeDtypeStruct((N_DEV, rows, cols), block.dtype),
        scratch_shapes=[pltpu.SemaphoreType.DMA((N_DEV - 1,)), pltpu.SemaphoreType.DMA((N_DEV - 1,)),
                        pltpu.SemaphoreType.DMA],
        compiler_params=pltpu.CompilerParams(vmem_limit_bytes=VMEM_LIMIT))(block)


COPY_BYTES = 1 << 20


def _row_chunks(rows, row_bytes):
    n = max(1, min(rows // 16, -(-rows * row_bytes // COPY_BYTES)))
    while rows % (16 * n):
        n -= 1
    return [(i * (rows // n), rows // n) for i in range(n)]


def _split_start(name, make_copies, n_copies, sources, lands):
    n, m = len(sources), len(lands)

    def body(*refs):
        for cp in make_copies(_position(), refs[:n], refs[n:n + m], refs[n + m], refs[n + m + 1]):
            cp.start()
        refs[-1][...] = jnp.zeros((8, LANE), F32)

    hbm = pl.BlockSpec(memory_space=pltpu.HBM)
    sem = pl.BlockSpec(memory_space=pltpu.SEMAPHORE)
    operands = [pltpu.with_memory_space_constraint(t, pltpu.HBM) for t in list(sources) + list(lands)]
    out = pl.pallas_call(
        body, name=name, in_specs=[hbm] * (n + m),
        out_specs=[sem, sem] + [hbm] * (n + m) + [pl.BlockSpec(memory_space=pltpu.VMEM)],
        out_shape=[pltpu.SemaphoreType.DMA((n_copies,)), pltpu.SemaphoreType.DMA((n_copies,))]
        + [pltpu.HBM(t.shape, t.dtype) for t in operands] + [jax.ShapeDtypeStruct((8, LANE), F32)],
        input_output_aliases={i: 2 + i for i in range(n + m)},
        compiler_params=pltpu.CompilerParams(has_side_effects=pltpu.SideEffectType.DATAFLOW_SIDE_EFFECTING))(*operands)
    return out[0], out[1], out[2:2 + n], out[2 + n:2 + n + m], out[-1]


def _split_wait(name, make_copies, send_sems, recv_sems, sources, lands, after):
    n, m = len(sources), len(lands)

    def body(*refs):
        for cp in make_copies(_position(), refs[:n], refs[n:n + m], refs[n + m], refs[n + m + 1]):
            cp.wait_send()
            cp.wait_recv()

    hbm = pl.BlockSpec(memory_space=pltpu.HBM)
    sem = pl.BlockSpec(memory_space=pltpu.SEMAPHORE)
    out = pl.pallas_call(
        body, name=name, in_specs=[hbm] * (n + m) + [sem, sem, pl.BlockSpec(memory_space=pl.ANY)],
        out_specs=[hbm] * (n + m), out_shape=[pltpu.HBM(t.shape, t.dtype) for t in list(sources) + list(lands)],
        input_output_aliases={i: i for i in range(n + m)},
        compiler_params=pltpu.CompilerParams(has_side_effects=pltpu.SideEffectType.DATAFLOW_SIDE_EFFECTING))(
            *sources, *lands, send_sems, recv_sems, after)
    return out[:n], out[n:]


def _gather_copies(pos, halves, lands, send_sems, recv_sems):
    chip, core = 2 * pos[0] + pos[1], pos[2]
    copies = []
    for a in range(len(halves)):
        for k in range(1, N_CHIPS):
            for r0, nr in _row_chunks(halves[a].shape[1], halves[a].shape[2] * halves[a].dtype.itemsize):
                i = len(copies)
                copies.append(pltpu.make_async_remote_copy(
                    src_ref=halves[a].at[core, pl.ds(r0, nr)], dst_ref=lands[a].at[chip, core, pl.ds(r0, nr)],
                    send_sem=send_sems.at[i], recv_sem=recv_sems.at[i], device_id=_flip(pos, 2 * k), device_id_type=MESH))
    return copies


def _gather_pieces(halves):
    return (N_CHIPS - 1) * sum(len(_row_chunks(a.shape[1], a.shape[2] * a.dtype.itemsize)) for a in halves)


def _pair_forward(lands):
    n = len(lands)

    def plan(pos, ins, outs):
        remote = []
        for a in range(n):
            for k in range(1, N_CHIPS):
                there = _flip(pos, 2 * k)
                for r0, nr in _row_chunks(lands[a].shape[2], lands[a].shape[3] * lands[a].dtype.itemsize):
                    slot = (2 * there[0] + there[1], pos[2], pl.ds(r0, nr))
                    remote.append((ins[a].at[slot], outs[a].at[slot], _flip(pos, 1)))
        return remote, []

    return _exchange("pair_forward", lands, [jax.ShapeDtypeStruct(t.shape, t.dtype) for t in lands], plan,
                     _gather_pieces([jax.ShapeDtypeStruct(t.shape[1:], t.dtype) for t in lands]), 0, in_place=True)


def _exchange(name, arrays, out_shapes, plan, n_remote, n_local, in_place=False):
    n, m = len(arrays), len(out_shapes)

    def body(*refs):
        send_sems, recv_sems, local_sems = refs[n + m:]
        remote, local = plan(_position(), refs[:n], refs[n:n + m])
        assert (len(remote), len(local)) == (n_remote, n_local)
        copies = [pltpu.make_async_copy(src, dst, local_sems.at[i]) for i, (src, dst) in enumerate(local)]
        copies += [pltpu.make_async_remote_copy(src_ref=src, dst_ref=dst, send_sem=send_sems.at[i], recv_sem=recv_sems.at[i],
                                                device_id=dev, device_id_type=MESH)
                   for i, (src, dst, dev) in enumerate(remote)]
        for cp in copies:
            cp.start()
        for cp in copies:
            cp.wait()

    hbm = pl.BlockSpec(memory_space=pl.ANY)
    return pl.pallas_call(
        body, name=name, in_specs=[hbm] * n, out_specs=[hbm] * m, out_shape=out_shapes,
        input_output_aliases={i: i for i in range(n)} if in_place else {},
        scratch_shapes=[pltpu.SemaphoreType.DMA((n_remote,)), pltpu.SemaphoreType.DMA((n_remote,)),
                        pltpu.SemaphoreType.DMA((max(n_local, 1),))],
        compiler_params=pltpu.CompilerParams(has_side_effects=True))(*arrays)


def _pair_send(slabs):
    n = len(slabs)
    pieces = [_row_chunks(g.shape[2], g.shape[3] * g.dtype.itemsize) for g in slabs]

    def plan(pos, ins, outs):
        return [(ins[a].at[j, 1 - pos[2], pl.ds(r0, nr)], outs[a].at[j, pl.ds(r0, nr)], _flip(pos, 1))
                for a in range(n) for j in range(N_CHIPS) for r0, nr in pieces[a]], []

    return _exchange("pair_send", slabs, [jax.ShapeDtypeStruct((N_CHIPS,) + g.shape[2:], g.dtype) for g in slabs], plan,
                     N_CHIPS * sum(len(p) for p in pieces), 0)


def _chip_scatter_copies(pos, sums, lands, send_sems, recv_sems):
    copies = []
    for a in range(len(sums)):
        for k in range(1, N_CHIPS):
            to = _flip(pos, 2 * k)
            for r0, nr in _row_chunks(sums[a].shape[1], sums[a].shape[2] * sums[a].dtype.itemsize):
                i = len(copies)
                copies.append(pltpu.make_async_remote_copy(
                    src_ref=sums[a].at[2 * to[0] + to[1], pl.ds(r0, nr)], dst_ref=lands[a].at[k - 1, pl.ds(r0, nr)],
                    send_sem=send_sems.at[i], recv_sem=recv_sems.at[i], device_id=to, device_id_type=MESH))
    return copies


def _chip_scatter_start(sums):
    n_copies = (N_CHIPS - 1) * sum(len(_row_chunks(g.shape[1], g.shape[2] * g.dtype.itemsize)) for g in sums)
    lands = [lax.empty((N_CHIPS - 1,) + g.shape[1:], g.dtype) for g in sums]
    return _split_start("chip_scatter_start", _chip_scatter_copies, n_copies, sums, lands)


def _chip_scatter_wait(send_sems, recv_sems, sums, lands, after):
    return _split_wait("chip_scatter_wait", _chip_scatter_copies, send_sems, recv_sems, sums, lands, after)


def _pair_exchange(pairs):
    n = len(pairs)
    pieces = [_row_chunks(h.shape[1], h.shape[2] * h.dtype.itemsize) for h in pairs]

    def plan(pos, ins, outs):
        return [(ins[a].at[pos[2], pl.ds(r0, nr)], outs[a].at[pos[2], pl.ds(r0, nr)], _flip(pos, 1))
                for a in range(n) for r0, nr in pieces[a]], []

    return _exchange("pair_exchange", pairs, [jax.ShapeDtypeStruct(h.shape, h.dtype) for h in pairs], plan,
                     sum(len(p) for p in pieces), 0, in_place=True)


def _pack(arrays):
    flat = jnp.concatenate([a.reshape(-1).astype(F32) for a in arrays])
    size = -(-flat.shape[0] // (8 * LANE)) * (8 * LANE)
    return jnp.pad(flat, (0, size - flat.shape[0])).reshape(size // LANE, LANE)


def _unpack(buf, shapes):
    flat = buf.reshape(-1)
    out, off = [], 0
    for shp in shapes:
        n = math.prod(shp)
        out.append(flat[off:off + n].reshape(shp))
        off += n
    return out


def _unpack_rows(bufs, shapes):
    flat = bufs.reshape(bufs.shape[0], -1)
    out, off = [], 0
    for shp in shapes:
        n = math.prod(shp)
        out.append(flat[:, off:off + n].reshape((bufs.shape[0],) + shp))
        off += n
    return out


def _taps8(w):
    return jnp.pad(w, ((0, 8 - CONV_K), (0, 0)))


def _local_step(xs, tgt, scale, shift, gate, norm_w, w_in_p, b_in_p, ml_conv_w, ml_conv_b, ml_norm_w, ssm_conv_w,
                ssm_conv_b, ssm_a_log, ssm_d, ssm_norm_w, wpm, wps, wo, final_w, start_exchange=None, late_weights=None,
                u=None):
    s = xs.shape[0]
    ts, tm, tw, tn = min(ROWS_ELEMENTWISE, s), min(ROWS_MATMUL, s), min(ROWS_WIDE_MATMUL, s), COLS_MATMUL
    if u is None:
        u = _prenorm_fwd(xs, norm_w, scale, shift, ts)
    proj = _matmul_bias(u, w_in_p, b_in_p, tw, tn, 0, O_IF, BF16)
    gates = _matmul_bias(u, w_in_p, b_in_p, tw, tn, O_IF, NP - O_IF, F32)
    mlw8, ssw8 = _taps8(ml_conv_w), _taps8(ssm_conv_w)
    qk, qk_dact = _conv_fwd(proj, O_QK, 2048, mlw8, ml_conv_b, ts)
    xbc, xbc_dact = _conv_fwd(proj, O_XBC, 3072, ssw8, ssm_conv_b, ts)
    gt = gates[:, :LANE].T
    dtt = gates[:, O_DT - O_IF:O_DT - O_IF + LANE].T
    hm, cst, nm = _mlstm_fwd(qk, proj, gates, gt)
    alog_row = jnp.pad(ssm_a_log, ((0, 0), (0, LANE - SSM_HEADS)))
    alog_col = alog_row.reshape(LANE, 1)
    dskip_x = jnp.repeat(ssm_d[0], SSM_HEADDIM)[None]
    expand = _head_expand()
    yssd, sst = _ssd_fwd(xbc, gates, dtt, alog_row, alog_col, dskip_x, expand)
    tp = min(ROWS_POST, s)
    ym, ys = _post_fwd(hm, yssd, proj, ml_norm_w, ssm_norm_w, tp)
    if late_weights is not None:
        wpm, wps, wo = late_weights(ym)
    dxres, merged, dmo, dpm, dps, dym, dys, dproj, acc_m = _merge(xs, ym, ys, proj, tgt, gate, final_w, wpm, wps, wo,
                                                                  min(ROWS_MERGE, s))
    dh, dyssd, dproj, acc_p = _post_bwd(dym, dys, hm, yssd, proj, ml_norm_w, ssm_norm_w, dproj, tp)
    dqk, dproj, dif = _mlstm_bwd(qk, proj, gates, gt, hm, dh, cst, nm, dproj)
    dxbc, ddt, accd, acca = _ssd_bwd(xbc, gates, dtt, alog_row, alog_col, dskip_x, expand, expand.T, dyssd, sst)
    dproj, acc_cq = _conv_bwd(proj, O_QK, 2048, mlw8, qk_dact, dqk, dproj, ts)
    dproj, acc_cx = _conv_bwd(proj, O_XBC, 3072, ssw8, xbc_dact, dxbc, dproj, ts)
    dproj = dproj.at[:, O_IF:O_IF + SMALL_W].set(dif).at[:, O_DT:O_DT + SMALL_W].set(ddt)
    gw_in_p, gb_in_p = _matmul_tn(u.T, dproj, tw, tn, with_colsum=True, a_is_transposed=True, dtype=BF16)
    g_wpm = _matmul_tn(ym, dpm, tm, tn, dtype=BF16)
    g_wps = _matmul_tn(ys, dps, tm, tn, dtype=BF16)
    g_wo = _matmul_tn(merged, dmo, tm, tn, dtype=BF16)
    token, in_flight = (None, None) if start_exchange is None else start_exchange(gw_in_p, g_wpm, g_wps, g_wo)
    du = _matmul_nt(dproj, w_in_p, tm, tn, after=token)
    grad_x, acc_n = _prenorm_bwd(du, xs, dxres, norm_w, scale, ts)
    a_coef = -jnp.exp(ssm_a_log[0])
    small = dict(
        mod=jnp.concatenate([acc_n[2], acc_n[1], acc_m[2]]), norm_w=acc_n[0], b_in=_unpad_cols(gb_in_p[0]),
        ml_conv_w=acc_cq[0:CONV_K], ml_conv_b=acc_cq[CONV_K], ml_norm_w=acc_p[0], ssm_conv_w=acc_cx[0:CONV_K],
        ssm_conv_b=acc_cx[CONV_K], ssm_a_log=acca[0, :SSM_HEADS] * a_coef,
        ssm_d=accd[0].reshape(SSM_HEADS, SSM_HEADDIM).sum(axis=1), ssm_norm_w=acc_p[1], final_w=acc_m[1], loss=acc_m[0, 0:1])
    return grad_x, small, gw_in_p, g_wpm, g_wps, g_wo, in_flight


WEIGHTS = ("norm_w", "ada_w", "ada_b", "w_in", "b_in", "ml_conv_w", "ml_conv_b", "ml_norm_w", "ssm_conv_w", "ssm_conv_b",
           "ssm_a_log", "ssm_d", "ssm_norm_w", "w_proj_m", "w_proj_s", "w_out", "final_w")
LARGE = ("ada_w", "w_in", "w_proj_m", "w_proj_s", "w_out")
SMALL_SUMS = (("mod", (3 * D_MODEL,)), ("norm_w", (D_MODEL,)), ("b_in", (IN_WIDTH,)), ("ml_conv_w", (CONV_K, 2048)),
              ("ml_conv_b", (2048,)), ("ml_norm_w", (2048,)), ("ssm_conv_w", (CONV_K, 3072)), ("ssm_conv_b", (3072,)),
              ("ssm_a_log", (SSM_HEADS,)), ("ssm_d", (SSM_HEADS,)), ("ssm_norm_w", (2048,)), ("final_w", (D_MODEL,)),
              ("loss", (1,)))


def kernel(x, c, norm_w, ada_w, ada_b, w_in, b_in, ml_conv_w, ml_conv_b, ml_norm_w, ssm_conv_w, ssm_conv_b, ssm_a_log, ssm_d, ssm_norm_w, w_proj_m, w_proj_s, w_out, final_w, loss_target, m_norm_w, m_ada_w, m_ada_b, m_w_in, m_b_in, m_ml_conv_w, m_ml_conv_b, m_ml_norm_w, m_ssm_conv_w, m_ssm_conv_b, m_ssm_a_log, m_ssm_d, m_ssm_norm_w, m_w_proj_m, m_w_proj_s, m_w_out, m_final_w, v_norm_w, v_ada_w, v_ada_b, v_w_in, v_b_in, v_ml_conv_w, v_ml_conv_b, v_ml_norm_w, v_ssm_conv_w, v_ssm_conv_b, v_ssm_a_log, v_ssm_d, v_ssm_norm_w, v_w_proj_m, v_w_proj_s, v_w_out, v_final_w):
    w = dict(norm_w=norm_w, ada_w=ada_w, ada_b=ada_b, w_in=w_in, b_in=b_in, ml_conv_w=ml_conv_w, ml_conv_b=ml_conv_b,
             ml_norm_w=ml_norm_w, ssm_conv_w=ssm_conv_w, ssm_conv_b=ssm_conv_b, ssm_a_log=ssm_a_log, ssm_d=ssm_d,
             ssm_norm_w=ssm_norm_w, w_proj_m=w_proj_m, w_proj_s=w_proj_s, w_out=w_out, final_w=final_w)
    m = dict(zip(WEIGHTS, (m_norm_w, m_ada_w, m_ada_b, m_w_in, m_b_in, m_ml_conv_w, m_ml_conv_b, m_ml_norm_w, m_ssm_conv_w,
                           m_ssm_conv_b, m_ssm_a_log, m_ssm_d, m_ssm_norm_w, m_w_proj_m, m_w_proj_s, m_w_out, m_final_w)))
    v = dict(zip(WEIGHTS, (v_norm_w, v_ada_w, v_ada_b, v_w_in, v_b_in, v_ml_conv_w, v_ml_conv_b, v_ml_norm_w, v_ssm_conv_w,
                           v_ssm_conv_b, v_ssm_a_log, v_ssm_d, v_ssm_norm_w, v_w_proj_m, v_w_proj_s, v_w_out, v_final_w)))
    pos = _position()
    chip = 2 * pos[0] + pos[1]
    dev = 2 * chip + pos[2]
    mlw_cols, ssw_cols, ada_cols = ml_conv_w.shape[2], ssm_conv_w.shape[2], ada_w.shape[2]

    g0 = _allgather8(_pack([c, ml_conv_w, ssm_conv_w]))
    c_all, mlw_all, ssw_all = _unpack_rows(g0, [(D_MODEL,), (CONV_K, mlw_cols), (CONV_K, ssw_cols)])
    ml_conv_full = mlw_all[0::2].transpose(1, 0, 2).reshape(CONV_K, N_CHIPS * mlw_cols)
    ssm_conv_full = ssw_all[0::2].transpose(1, 0, 2).reshape(CONV_K, N_CHIPS * ssw_cols)

    ada_b_mine = lax.dynamic_slice_in_dim(ada_b, chip * ada_cols, ada_cols, axis=1)
    g1 = _allgather8(_ada_fwd(c_all, ada_w[0], ada_b_mine))
    mod = lax.dynamic_index_in_dim(g1[0::2], dev, axis=1, keepdims=False).reshape(1, 3 * D_MODEL)
    shift, scale, gate = mod[:, :D_MODEL], mod[:, D_MODEL:2 * D_MODEL], mod[:, 2 * D_MODEL:]

    def whole(lands, owns):
        return [lax.dynamic_update_index_in_dim(got, own, chip, 0).reshape(N_CHIPS, -1, own.shape[-1])
                for got, own in zip(_pair_forward(lands), owns)]

    mine = [_bf(a[0]).reshape(2, a.shape[1] // 2, a.shape[2]) for a in (w_in, w_proj_m, w_proj_s, w_out)]
    landing = lambda own: lax.empty((N_CHIPS,) + own.shape, own.dtype)
    w_send, w_recv, w_src, w_land, w_token = _split_start("w_in_gather_start", _gather_copies, _gather_pieces(mine[:1]),
                                                          mine[:1], [landing(mine[0])])
    u = _prenorm_fwd(x[0], norm_w, scale + w_token[0:1, 0:1], shift, ROWS_ELEMENTWISE)
    w_src, w_land = _split_wait("w_in_gather_wait", _gather_copies, w_send, w_recv, w_src, w_land, u)
    behind = (w_src[0][0, 0:1, 0:1] * 0).astype(BF16)
    later = [a + behind for a in mine[1:]]
    p_send, p_recv, p_src, p_land, p_token = _split_start("merge_gather_start", _gather_copies, _gather_pieces(later),
                                                          later, [landing(a) for a in later])
    w_in_p = _shards_to_padded(whole(w_land, w_src)[0])
    b_in_p = _pad_cols(b_in) + p_token[0:1, 0:1]

    def late_weights(after):
        srcs, lands = _split_wait("merge_gather_wait", _gather_copies, p_send, p_recv, p_src, p_land, after)
        return [a.reshape(-1, D_MODEL) for a in whole(lands, srcs)]

    def start_exchange(g_w_in, g_wpm, g_wps, g_wo):
        split = lambda g, rows: _bf(g).reshape(N_CHIPS, 2, rows // (2 * N_CHIPS), g.shape[-1])
        slabs = [split(_padded_to_shards(_bf(g_w_in)), N_CHIPS * D_MODEL),
                 split(g_wpm, g_wpm.shape[0]), split(g_wps, g_wps.shape[0]), split(g_wo, g_wo.shape[0])]
        pair_sums = []
        for slab, rec in zip(slabs, _pair_send(slabs)):
            kept = lax.dynamic_index_in_dim(slab, pos[2], 1, keepdims=False)
            rows = kept.shape[0] * kept.shape[1]
            both = _sum_parts(kept.reshape(rows, -1), rec.reshape(1, rows, -1), ROWS_SUM, BF16)
            pair_sums.append(both.reshape(kept.shape))
        send_sems, recv_sems, sums, lands, token = _chip_scatter_start(pair_sums)
        return token, (send_sems, recv_sems, sums, lands)

    grad_x, small, _, _, _, _, in_flight = _local_step(
        x[0], loss_target[0], scale, shift, gate, norm_w, w_in_p, b_in_p, ml_conv_full, ml_conv_b, ml_norm_w,
        ssm_conv_full, ssm_conv_b, ssm_a_log, ssm_d, ssm_norm_w, None, None, None, final_w[None], start_exchange,
        late_weights, u)

    g2 = _allgather8(_pack([small[name] for name, _ in SMALL_SUMS]))
    total = dict(zip([name for name, _ in SMALL_SUMS], _unpack(_sum_parts(None, g2, g2.shape[1]), [s for _, s in SMALL_SUMS])))
    dmod_all = g2[:, :3 * D_MODEL // LANE].reshape(N_DEV, 3 * D_MODEL)
    grads = dict(total)
    grads["ada_b"] = total["mod"]
    grads["ml_conv_w"] = lax.dynamic_slice_in_dim(total["ml_conv_w"], chip * mlw_cols, mlw_cols, axis=1)
    grads["ssm_conv_w"] = lax.dynamic_slice_in_dim(total["ssm_conv_w"], chip * ssw_cols, ssw_cols, axis=1)
    grads["ada_w"] = _ada_bwd(c_all, lax.dynamic_slice_in_dim(dmod_all, chip * ada_cols, ada_cols, axis=1))

    pairs = []
    for both, rec in zip(*_chip_scatter_wait(*in_flight, grad_x)):
        pairs.append(_sum_parts(lax.dynamic_index_in_dim(both, chip, 0, keepdims=False), rec, ROWS_SUM, slot=pos[2]))
    for name, full in zip(("w_in", "w_proj_m", "w_proj_s", "w_out"), _pair_exchange(pairs)):
        grads[name] = full.reshape(-1, full.shape[-1])

    delta, new_m, new_v = {}, {}, {}
    for name in LARGE:
        if w[name].shape[-1] % LANE:
            flat = lambda a: a.reshape(a.shape[-2:]).T
            back = lambda a: a.T.reshape(w[name].shape)
            g_flat = flat(grads[name])
            delta[name], new_m[name], new_v[name] = (back(a) for a in _adamw(flat(w[name]), g_flat, flat(m[name]), flat(v[name]), LANE))
            grads[name] = back(g_flat)
        else:
            delta[name], new_m[name], new_v[name] = _adamw(w[name], grads[name], m[name], v[name], ROWS_ADAMW)
    rest = [name for name in WEIGHTS if name not in LARGE]
    packed = [_pack([t[name] for name in rest]) for t in (w, grads, m, v)]
    for out, buf in zip((delta, new_m, new_v), _adamw(*packed, packed[0].shape[0])):
        out.update(zip(rest, _unpack(buf, [w[name].shape for name in rest])))
    loss = total["loss"][0]
    return (loss, grad_x[None], *[grads[name].reshape(w[name].shape) for name in WEIGHTS], *[delta[name] for name in WEIGHTS],
            *[new_m[name] for name in WEIGHTS], *[new_v[name] for name in WEIGHTS])
```

```python
import math

import jax
import jax.numpy as jnp
from jax import lax
from jax.experimental import pallas as pl
from jax.experimental.pallas import tpu as pltpu

F32 = jnp.float32
BF16 = jnp.bfloat16
MESH = pl.DeviceIdType.MESH

D_MODEL = 1024
EPS = 1e-6
CONV_K = 4
ML_HEADS = 8
ML_DQK = 128
ML_DV = 256
SSM_HEADS = 32
SSM_HEADDIM = 64
SSM_GROUPS = 4
SSM_STATE = 128
IN_WIDTH = 15408
N_CHIPS = 4
N_DEV = 8
ADAM_LR, ADAM_B1, ADAM_B2, ADAM_EPS, ADAM_WD, ADAM_STEP = 0.001, 0.9, 0.999, 1e-08, 0.01, 10

O_O, O_ZM, O_ZS, O_MG, O_QK, O_V, O_XBC, O_IF, O_DT = 0, 2048, 4096, 6144, 8192, 10240, 12288, 15360, 15616
SMALL_W = 256
NP = 15872
LANE = 128
CHUNK = 128
NEG = -1e30
VMEM_LIMIT = 48 * 1024 * 1024
MERGE_VMEM = 60 * 1024 * 1024
ROWS_ELEMENTWISE = 512
ROWS_MATMUL = 2048
ROWS_WIDE_MATMUL = 4096
COLS_MATMUL = 512
ROWS_POST = 128
ROWS_MERGE = 256
ROWS_SUM = 32
ROWS_PAIR_SUM = 128
ROWS_ADAMW = 64


def _cparams(*sem, vmem=VMEM_LIMIT):
    return pltpu.CompilerParams(dimension_semantics=sem, vmem_limit_bytes=vmem)


def _pad_cols(w):
    z = lambda n: jnp.zeros(w.shape[:-1] + (n,), w.dtype)
    return jnp.concatenate([w[..., 4096:8192], w[..., 11280:13328], w[..., 13360:15408], w[..., :4096], w[..., 8208:11280],
                            w[..., 8192:8208], z(SMALL_W - 16), w[..., 13328:13360], z(SMALL_W - 32)], axis=-1)


def _unpad_cols(g):
    return jnp.concatenate([g[..., O_QK:O_QK + 4096], g[..., O_O:O_O + 4096], g[..., O_IF:O_IF + 16],
                            g[..., O_XBC:O_XBC + 3072], g[..., O_ZS:O_ZS + 2048], g[..., O_DT:O_DT + 32],
                            g[..., O_MG:O_MG + 2048]], axis=-1)


PADDED_SEGMENTS = ((4096, 8192, 0), (11280, 13328, 0), (13360, 15408, 0), (0, 4096, 0), (8208, 11280, 0),
                   (8192, 8208, SMALL_W - 16), (13328, 13360, SMALL_W - 32))
SHARD_W = IN_WIDTH // N_CHIPS


def _shards_to_padded(shards):
    parts = []
    for first, last, pad in PADDED_SEGMENTS:
        for j in range(N_CHIPS):
            lo, hi = max(first, j * SHARD_W), min(last, (j + 1) * SHARD_W)
            if lo < hi:
                parts.append(shards[j][:, lo - j * SHARD_W:hi - j * SHARD_W])
        if pad:
            parts.append(jnp.zeros((shards.shape[1], pad), shards.dtype))
    return jnp.concatenate(parts, axis=1)


def _padded_to_shards(g):
    offsets, off = {}, 0
    for first, last, pad in PADDED_SEGMENTS:
        offsets[first] = off
        off += last - first + pad
    shards = []
    for j in range(N_CHIPS):
        parts = []
        for first, last, _ in sorted(PADDED_SEGMENTS):
            lo, hi = max(first, j * SHARD_W), min(last, (j + 1) * SHARD_W)
            if lo < hi:
                parts.append(g[:, offsets[first] + lo - first:offsets[first] + hi - first])
        shards.append(jnp.concatenate(parts, axis=1))
    return jnp.stack(shards)


def _sigmoid(x):
    return 0.5 * jnp.tanh(0.5 * x) + 0.5


def _silu(x):
    return x * _sigmoid(x)


def _dsilu(x):
    s = _sigmoid(x)
    return s + x * s * (1.0 - s)


def _softplus(x):
    return jnp.maximum(x, 0.0) + jnp.log(1.0 + jnp.exp(-jnp.abs(x)))


def _logsigmoid(x):
    return jnp.minimum(x, 0.0) - jnp.log(1.0 + jnp.exp(-jnp.abs(x)))


def _dot(a, b, dims):
    return lax.dot_general(a, b, (dims, ((), ())), preferred_element_type=F32)


def _nn(a, b):
    return _dot(a, b, ((1,), (0,)))


def _nt(a, b):
    return _dot(a, b, ((1,), (1,)))


def _tn(a, b):
    return _dot(a, b, ((0,), (0,)))


def _bf(x):
    return x.astype(BF16)


def _split(x, terms):
    parts = []
    for _ in range(terms):
        part = _bf(x)
        parts.append(part)
        x = x - part.astype(F32)
    return parts


def _pick_right(x, pick, terms):
    pick = _bf(pick)
    out = None
    for part in _split(x, terms):
        out = _nn(part, pick) if out is None else out + _nn(part, pick)
    return out


def _pick_left(pick, x, terms):
    pick = _bf(pick)
    out = None
    for part in _split(x, terms):
        out = _nn(pick, part) if out is None else out + _nn(pick, part)
    return out


def _lane_col(x, lane):
    idx = lax.broadcasted_iota(jnp.int32, x.shape, 1)
    return jnp.sum(jnp.where(idx == lane, x, 0.0), axis=1, keepdims=True)


def _tri(n, upper):
    r = lax.broadcasted_iota(jnp.int32, (n, n), 0)
    c = lax.broadcasted_iota(jnp.int32, (n, n), 1)
    return jnp.where((r <= c) if upper else (r >= c), 1.0, 0.0).astype(F32)


def _eye(n):
    return jnp.where(lax.broadcasted_iota(jnp.int32, (n, n), 0) == lax.broadcasted_iota(jnp.int32, (n, n), 1), 1.0, 0.0)


def _sum_all(x):
    return jnp.sum(jnp.sum(x, axis=1, keepdims=True), axis=0, keepdims=True)


def _crossing(p):
    L = p.shape[0]
    below = _nn(_bf(_tri(L, True)), _bf(p))
    strict = lax.broadcasted_iota(jnp.int32, (L, L), 0) > lax.broadcasted_iota(jnp.int32, (L, L), 1)
    return [jnp.sum(jnp.where(strict, below[:, b * L:(b + 1) * L], 0.0), axis=1, keepdims=True)
            for b in range(p.shape[1] // L)]


def _matmul_bias(a, w, bias, tm, tn, col0, ncols, dtype):
    m, k = a.shape
    j0 = col0 // tn

    def body(a_ref, w_ref, b_ref, o_ref):
        o_ref[...] = (_nn(a_ref[...], w_ref[...]) + b_ref[...]).astype(dtype)

    return pl.pallas_call(
        body, name="matmul_bias", grid=(m // tm, ncols // tn),
        in_specs=[pl.BlockSpec((tm, k), lambda i, j: (i, 0)), pl.BlockSpec((k, tn), lambda i, j: (0, j0 + j)),
                  pl.BlockSpec((1, tn), lambda i, j: (0, j0 + j))],
        out_specs=pl.BlockSpec((tm, tn), lambda i, j: (i, j)),
        out_shape=jax.ShapeDtypeStruct((m, ncols), dtype),
        compiler_params=_cparams("parallel", "arbitrary"))(a, w, bias)


def _matmul_nt(a, w, tm, tk, after=None):
    m, n = a.shape
    k = w.shape[0]

    def body(a_ref, w_ref, *rest):
        o_ref = rest[-1]

        @pl.when(pl.program_id(1) == 0)
        def _():
            o_ref[...] = jnp.zeros_like(o_ref)
        o_ref[...] += _nt(a_ref[...], w_ref[...])

    extra = [] if after is None else [after]
    return pl.pallas_call(
        body, name="matmul_nt", grid=(m // tm, n // tk),
        in_specs=[pl.BlockSpec((tm, tk), lambda i, j: (i, j)), pl.BlockSpec((k, tk), lambda i, j: (0, j))]
        + [pl.BlockSpec(memory_space=pl.ANY)] * len(extra),
        out_specs=pl.BlockSpec((tm, k), lambda i, j: (i, 0)),
        out_shape=jax.ShapeDtypeStruct((m, k), F32),
        compiler_params=_cparams("parallel", "arbitrary"))(a, w, *extra)


def _matmul_tn(a, b, tm, tn, with_colsum=False, a_is_transposed=False, dtype=F32):
    k, m = a.shape if a_is_transposed else a.shape[::-1]
    n = b.shape[1]
    steps = m // tm

    def body(a_ref, b_ref, o_ref, *rest):
        acc_ref = o_ref if dtype == F32 else rest[-1]
        first = pl.program_id(1) == 0

        @pl.when(first)
        def _():
            acc_ref[...] = jnp.zeros_like(acc_ref)
        acc_ref[...] += _nn(a_ref[...], b_ref[...]) if a_is_transposed else _tn(a_ref[...], b_ref[...])
        if dtype != F32:
            @pl.when(pl.program_id(1) == steps - 1)
            def _():
                o_ref[...] = acc_ref[...].astype(dtype)
        if with_colsum:
            s_ref = rest[0]

            @pl.when(first)
            def _():
                s_ref[...] = jnp.zeros_like(s_ref)
            s_ref[...] += jnp.sum(b_ref[...].astype(F32), axis=0, keepdims=True)

    out_specs = [pl.BlockSpec((k, tn), lambda j, i: (0, j))]
    out_shape = [jax.ShapeDtypeStruct((k, n), dtype)]
    if with_colsum:
        out_specs.append(pl.BlockSpec((1, tn), lambda j, i: (0, j)))
        out_shape.append(jax.ShapeDtypeStruct((1, n), F32))
    out = pl.pallas_call(
        body, name="matmul_tn", grid=(n // tn, m // tm),
        in_specs=[pl.BlockSpec((k, tm), lambda j, i: (0, i)) if a_is_transposed else pl.BlockSpec((tm, k), lambda j, i: (i, 0)),
                  pl.BlockSpec((tm, tn), lambda j, i: (i, j))],
        out_specs=out_specs, out_shape=out_shape,
        scratch_shapes=[] if dtype == F32 else [pltpu.VMEM((k, tn), F32)],
        compiler_params=_cparams("parallel", "arbitrary"))(a, b)
    return out if with_colsum else out[0]


def _ada_fwd(c_all, ada_w, ada_b):
    def body(c_ref, w_ref, b_ref, o_ref):
        o_ref[...] = _nn(_bf(_silu(c_ref[...])), _bf(w_ref[...])) + b_ref[...]

    return pl.pallas_call(body, name="ada_fwd", out_shape=jax.ShapeDtypeStruct((c_all.shape[0], ada_w.shape[1]), F32),
                          compiler_params=_cparams())(c_all, ada_w, ada_b)


def _ada_bwd(c_all, dmod):
    def body(c_ref, d_ref, o_ref):
        o_ref[...] = _tn(_bf(_silu(c_ref[...])), _bf(d_ref[...]))

    return pl.pallas_call(body, name="ada_bwd", out_shape=jax.ShapeDtypeStruct((c_all.shape[1], dmod.shape[1]), F32),
                          compiler_params=_cparams())(c_all, dmod)


def _prenorm_fwd(x, norm_w, scale, shift, ts):
    s, d = x.shape

    def body(x_ref, nw_ref, sc_ref, sh_ref, u_ref):
        xv = x_ref[...]
        r = lax.rsqrt(jnp.mean(xv * xv, axis=1, keepdims=True) + EPS)
        u_ref[...] = _bf(xv * r * nw_ref[...] * (1.0 + sc_ref[...]) + sh_ref[...])

    row = pl.BlockSpec((1, d), lambda i: (0, 0))
    return pl.pallas_call(
        body, name="prenorm_fwd", grid=(s // ts,),
        in_specs=[pl.BlockSpec((ts, d), lambda i: (i, 0)), row, row, row],
        out_specs=pl.BlockSpec((ts, d), lambda i: (i, 0)), out_shape=jax.ShapeDtypeStruct((s, d), BF16),
        compiler_params=_cparams("parallel"))(x, norm_w, scale, shift)


def _prenorm_bwd(du, x, dxres, norm_w, scale, ts):
    s, d = x.shape

    def body(du_ref, x_ref, dr_ref, nw_ref, sc_ref, gx_ref, acc_ref):
        @pl.when(pl.program_id(0) == 0)
        def _():
            acc_ref[...] = jnp.zeros_like(acc_ref)
        xv, duv = x_ref[...], du_ref[...]
        r = lax.rsqrt(jnp.mean(xv * xv, axis=1, keepdims=True) + EPS)
        xn = xv * r
        nw, sc1 = nw_ref[...], 1.0 + sc_ref[...]
        dxn = duv * (nw * sc1)
        gx_ref[...] = r * (dxn - xn * jnp.mean(dxn * xn, axis=1, keepdims=True)) + dr_ref[...]
        t = duv * xn
        acc_ref[0:1, :] += jnp.sum(t, axis=0, keepdims=True) * sc1
        acc_ref[1:2, :] += jnp.sum(t, axis=0, keepdims=True) * nw
        acc_ref[2:3, :] += jnp.sum(duv, axis=0, keepdims=True)

    tile = pl.BlockSpec((ts, d), lambda i: (i, 0))
    row = pl.BlockSpec((1, d), lambda i: (0, 0))
    return pl.pallas_call(
        body, name="prenorm_bwd", grid=(s // ts,),
        in_specs=[tile, tile, tile, row, row],
        out_specs=[tile, pl.BlockSpec((8, d), lambda i: (0, 0))],
        out_shape=[jax.ShapeDtypeStruct((s, d), F32), jax.ShapeDtypeStruct((8, d), F32)],
        compiler_params=_cparams("arbitrary"))(du, x, dxres, norm_w, scale)


CONV_CB = 512


def _conv_taps(buf_ref, ts):
    return [buf_ref[pl.ds(8 - (CONV_K - 1) + j, ts), :] for j in range(CONV_K)]


def _conv_fwd(proj, col0, width, w8, b, ts):
    s = proj.shape[0]
    cb = CONV_CB
    nt = s // ts

    def body(x_ref, w_ref, b_ref, o_ref, ds_ref, buf_ref):
        @pl.when(pl.program_id(1) == 0)
        def _():
            buf_ref[0:8, :] = jnp.zeros((8, cb), F32)
        buf_ref[pl.ds(8, ts), :] = x_ref[...].astype(F32)
        acc = b_ref[...] + jnp.zeros((ts, cb), F32)
        for j, tap in enumerate(_conv_taps(buf_ref, ts)):
            acc = acc + tap * w_ref[j:j + 1, :]
        sg = _sigmoid(acc)
        o_ref[...] = acc * sg
        ds_ref[...] = _bf(sg + acc * sg * (1.0 - sg))
        buf_ref[0:8, :] = buf_ref[pl.ds(ts, 8), :]

    c0 = col0 // cb
    tile = pl.BlockSpec((ts, cb), lambda c, i: (i, c))
    return pl.pallas_call(
        body, name="conv_fwd", grid=(width // cb, nt),
        in_specs=[pl.BlockSpec((ts, cb), lambda c, i: (i, c0 + c)), pl.BlockSpec((8, cb), lambda c, i: (0, c)),
                  pl.BlockSpec((1, cb), lambda c, i: (0, c))],
        out_specs=[tile, tile],
        out_shape=[jax.ShapeDtypeStruct((s, width), F32), jax.ShapeDtypeStruct((s, width), BF16)],
        scratch_shapes=[pltpu.VMEM((ts + 8, cb), F32)],
        compiler_params=_cparams("parallel", "arbitrary"))(proj, w8, b)


def _conv_bwd(proj, col0, width, w8, dact, dpost, dproj, ts):
    s = proj.shape[0]
    cb = CONV_CB
    nt = s // ts
    c0 = col0 // cb

    def body(x_ref, da_ref, dp_ref, w_ref, _, dx_ref, acc_ref, dbuf_ref):
        @pl.when(pl.program_id(1) == 0)
        def _():
            acc_ref[...] = jnp.zeros_like(acc_ref)
            dbuf_ref[pl.ds(ts, 8), :] = jnp.zeros((8, cb), F32)
        dconv = dp_ref[...].astype(F32) * da_ref[...].astype(F32)
        acc_ref[CONV_K:CONV_K + 1, :] += jnp.sum(dconv, axis=0, keepdims=True)
        dbuf_ref[pl.ds(0, ts), :] = dconv
        xv = x_ref[...].astype(F32)
        dx = jnp.zeros((ts, cb), F32)
        for j in range(CONV_K):
            shifted = dbuf_ref[pl.ds(CONV_K - 1 - j, ts), :]
            dx = dx + shifted * w_ref[j:j + 1, :]
            acc_ref[j:j + 1, :] += jnp.sum(xv * shifted, axis=0, keepdims=True)
        dx_ref[...] = _bf(dx)
        dbuf_ref[pl.ds(ts, 8), :] = dconv[0:8, :]

    tile = pl.BlockSpec((ts, cb), lambda c, i: (nt - 1 - i, c))
    wide = pl.BlockSpec((ts, cb), lambda c, i: (nt - 1 - i, c0 + c))
    return pl.pallas_call(
        body, name="conv_bwd", grid=(width // cb, nt),
        in_specs=[wide, tile, tile, pl.BlockSpec((8, cb), lambda c, i: (0, c)), pl.BlockSpec(memory_space=pl.ANY)],
        out_specs=[wide, pl.BlockSpec((8, cb), lambda c, i: (0, c))],
        out_shape=[jax.ShapeDtypeStruct(dproj.shape, dproj.dtype), jax.ShapeDtypeStruct((8, width), F32)],
        input_output_aliases={4: 0},
        scratch_shapes=[pltpu.VMEM((ts + 8, cb), F32)],
        compiler_params=_cparams("parallel", "arbitrary"))(proj, dact, dpost, w8, dproj)


def _mlstm_gates(gif_ref, gt_ref, a_scr, at_scr):
    L = gif_ref.shape[0]
    fb = _logsigmoid(gif_ref[...])
    a_scr[...] = _pick_left(_tri(L, False), fb, 3)
    at_scr[...] = _pick_right(_logsigmoid(gt_ref[...]), _tri(L, True), 3)
    return jnp.sum(fb, axis=0, keepdims=True)


def _mlstm_head(h, qk_ref, v_ref, gif, gt_ref, a, at_scr, a_last_row, c_mat, n_row, m_prev):
    L = gif.shape[0]
    q = qk_ref[:, h * ML_DQK:(h + 1) * ML_DQK] * (ML_DQK ** -0.5)
    k = qk_ref[:, (ML_HEADS + h) * ML_DQK:(ML_HEADS + h + 1) * ML_DQK]
    v = v_ref[:, h * ML_DV:(h + 1) * ML_DV]
    i_col, a_col = _lane_col(gif, h), _lane_col(a, ML_HEADS + h)
    i_row, a_row = gt_ref[h:h + 1, :], at_scr[ML_HEADS + h:ML_HEADS + h + 1, :]
    causal = lax.broadcasted_iota(jnp.int32, (L, L), 0) >= lax.broadcasted_iota(jnp.int32, (L, L), 1)
    dmat = jnp.where(causal, a_col - a_row + i_row, NEG)
    inter = a_col + m_prev
    m_t = jnp.maximum(inter, jnp.max(dmat, axis=1, keepdims=True))
    w_intra = jnp.exp(dmat - m_t)
    w_inter = jnp.exp(inter - m_t)
    sc = _nt(_bf(q), _bf(k)) * w_intra
    den = jnp.sum(sc, axis=1, keepdims=True) + w_inter * jnp.sum(q * n_row, axis=1, keepdims=True)
    floor = jnp.exp(-m_t)
    a_last = _lane_col(a_last_row, ML_HEADS + h)
    g = a_last - a_col + i_col
    m_new = jnp.maximum(a_last + m_prev, jnp.max(g, axis=0, keepdims=True))
    wk = jnp.exp(g - m_new)
    decay = jnp.exp(a_last + m_prev - m_new)
    return dict(q=q, k=k, v=v, w_intra=w_intra, w_inter=w_inter, sc=sc, den=den, floor=floor, m_new=m_new, wk=wk,
                decay=decay)


def _state_tile(n_row, m11):
    r = lax.broadcasted_iota(jnp.int32, (8, LANE), 0)
    return jnp.where(r == 0, n_row, jnp.where(r == 1, m11, 0.0))


def _mlstm_fwd(qk, proj, gates, gt):
    s = qk.shape[0]
    L = CHUNK
    nc = s // L

    def body(qk_ref, v_ref, gif_ref, gt_ref, h_ref, cst_ref, nm_ref, c_scr, nm_scr, a_scr, at_scr):
        @pl.when(pl.program_id(0) == 0)
        def _():
            c_scr[...] = jnp.zeros_like(c_scr)
            nm_scr[...] = jnp.zeros_like(nm_scr)
        a_last_row = _mlstm_gates(gif_ref, gt_ref, a_scr, at_scr)
        gif, a = gif_ref[...], a_scr[...]
        for h in range(ML_HEADS):
            c_mat, n_row = c_scr[h], nm_scr[h, 0:1, :]
            m_prev = jnp.max(nm_scr[h, 1:2, :], axis=1, keepdims=True)
            cst_ref[0, h] = c_mat
            nm_ref[0, h] = nm_scr[h]
            t = _mlstm_head(h, qk_ref, v_ref, gif, gt_ref, a, at_scr, a_last_row, c_mat, n_row, m_prev)
            num = _nn(_bf(t["sc"]), _bf(t["v"])) + t["w_inter"] * _nn(_bf(t["q"]), _bf(c_mat))
            h_ref[:, h * ML_DV:(h + 1) * ML_DV] = _bf(num * (1.0 / jnp.maximum(jnp.abs(t["den"]), t["floor"])))
            kw = t["k"] * t["wk"]
            c_scr[h] = t["decay"] * c_mat + _tn(_bf(kw), _bf(t["v"]))
            nm_scr[h] = _state_tile(t["decay"] * n_row + jnp.sum(kw, axis=0, keepdims=True), t["m_new"])

    return pl.pallas_call(
        body, name="mlstm_fwd", grid=(nc,),
        in_specs=[pl.BlockSpec((L, 2048), lambda c: (c, 0)), pl.BlockSpec((L, 2048), lambda c: (c, O_V // 2048)),
                  pl.BlockSpec((L, LANE), lambda c: (c, 0)), pl.BlockSpec((LANE, L), lambda c: (0, c))],
        out_specs=[pl.BlockSpec((L, 2048), lambda c: (c, 0)),
                   pl.BlockSpec((1, ML_HEADS, ML_DQK, ML_DV), lambda c: (c, 0, 0, 0)),
                   pl.BlockSpec((1, ML_HEADS, 8, LANE), lambda c: (c, 0, 0, 0))],
        out_shape=[jax.ShapeDtypeStruct((s, 2048), BF16), jax.ShapeDtypeStruct((nc, ML_HEADS, ML_DQK, ML_DV), F32),
                   jax.ShapeDtypeStruct((nc, ML_HEADS, 8, LANE), F32)],
        scratch_shapes=[pltpu.VMEM((ML_HEADS, ML_DQK, ML_DV), F32), pltpu.VMEM((ML_HEADS, 8, LANE), F32),
                        pltpu.VMEM((L, LANE), F32), pltpu.VMEM((LANE, L), F32)],
        compiler_params=_cparams("arbitrary"))(qk, proj, gates, gt)


def _mlstm_bwd(qk, proj, gates, gt, hout, dh, cst, nm, dproj):
    s = qk.shape[0]
    L = CHUNK
    nc = s // L

    def body(qk_ref, v_ref, gif_ref, gt_ref, h_ref, dh_ref, cst_ref, nm_ref, _, dqk_ref, dv_ref, dif_ref,
             dc_scr, dn_scr, a_scr, at_scr):
        @pl.when(pl.program_id(0) == 0)
        def _():
            dc_scr[...] = jnp.zeros_like(dc_scr)
            dn_scr[...] = jnp.zeros_like(dn_scr)
        a_last_row = _mlstm_gates(gif_ref, gt_ref, a_scr, at_scr)
        gif, a = gif_ref[...], a_scr[...]
        lane = lax.broadcasted_iota(jnp.int32, (L, LANE), 1)
        last = lax.broadcasted_iota(jnp.int32, (L, 1), 0) == L - 1
        di_tile = jnp.zeros((L, LANE), F32)
        cross = [jnp.zeros((L, LANE), F32)] * 3
        dlogw = []
        for h in range(ML_HEADS):
            c_mat, n_row = cst_ref[0, h], nm_ref[0, h, 0:1, :]
            m_prev = jnp.max(nm_ref[0, h, 1:2, :], axis=1, keepdims=True)
            t = _mlstm_head(h, qk_ref, v_ref, gif, gt_ref, a, at_scr, a_last_row, c_mat, n_row, m_prev)
            q, k, v, den = t["q"], t["k"], t["v"], t["den"]
            dhh = dh_ref[:, h * ML_DV:(h + 1) * ML_DV].astype(F32)
            hh = h_ref[:, h * ML_DV:(h + 1) * ML_DV].astype(F32)
            dnorm = jnp.maximum(jnp.abs(den), t["floor"])
            inv = 1.0 / dnorm
            dnum = dhh * inv
            d_dn = -jnp.sum(dhh * hh, axis=1, keepdims=True) * inv
            dden = jnp.where(jnp.abs(den) >= t["floor"], jnp.where(den >= 0.0, d_dn, -d_dn), 0.0)
            dsc = _nt(_bf(dnum), _bf(v)) + dden
            ds = dsc * t["w_intra"]
            dq_inter = t["w_inter"] * (_nt(_bf(dnum), _bf(c_mat)) + dden * n_row)
            dq = _nn(_bf(ds), _bf(k)) + dq_inter
            dc, dn_row = dc_scr[h], dn_scr[h, 0:1, :]
            dk_state = t["wk"] * (_nt(_bf(v), _bf(dc)) + dn_row)
            dk = _tn(_bf(ds), _bf(q)) + dk_state
            dv = _tn(_bf(t["sc"]), _bf(dnum)) + t["wk"] * _nn(_bf(k), _bf(dc))
            qi = q * t["w_inter"]
            dc_scr[h] = t["decay"] * dc + _tn(_bf(qi), _bf(dnum))
            dn_scr[h] = jnp.broadcast_to(t["decay"] * dn_row + jnp.sum(qi * dden, axis=0, keepdims=True), (8, LANE))
            dqk_ref[:, h * ML_DQK:(h + 1) * ML_DQK] = _bf(dq * (ML_DQK ** -0.5))
            dqk_ref[:, (ML_HEADS + h) * ML_DQK:(ML_HEADS + h + 1) * ML_DQK] = _bf(dk)
            dv_ref[:, h * ML_DV:(h + 1) * ML_DV] = _bf(dv)
            di_tile = di_tile + jnp.where(lane == h, jnp.sum(k * dk, axis=1, keepdims=True), 0.0)
            carried = t["decay"] * (_sum_all(dc * c_mat) + jnp.sum(dn_row * n_row, axis=1, keepdims=True))
            dlogw.append(dsc * t["sc"])
            parts = (jnp.sum(q * dq_inter, axis=1, keepdims=True) + jnp.where(last, carried, 0.0),
                     jnp.sum(k * dk_state, axis=1, keepdims=True))
            cross[1:] = [c + jnp.where(lane == ML_HEADS + h, p, 0.0) for c, p in zip(cross[1:], parts)]
        for h, col in enumerate(_crossing(jnp.concatenate(dlogw, axis=1))):
            cross[0] = cross[0] + jnp.where(lane == ML_HEADS + h, col, 0.0)
        dfb = cross[0] + _pick_left(_tri(L, True), cross[1], 2) + _pick_left(_tri(L, False) - _eye(L), cross[2], 2)
        dif_ref[:, 0:LANE] = _bf(di_tile + dfb * _sigmoid(-gif))
        dif_ref[:, LANE:SMALL_W] = jnp.zeros((L, SMALL_W - LANE), BF16)

    rev = lambda c: nc - 1 - c
    return pl.pallas_call(
        body, name="mlstm_bwd", grid=(nc,),
        in_specs=[pl.BlockSpec((L, 2048), lambda c: (rev(c), 0)), pl.BlockSpec((L, 2048), lambda c: (rev(c), O_V // 2048)),
                  pl.BlockSpec((L, LANE), lambda c: (rev(c), 0)), pl.BlockSpec((LANE, L), lambda c: (0, rev(c))),
                  pl.BlockSpec((L, 2048), lambda c: (rev(c), 0)), pl.BlockSpec((L, 2048), lambda c: (rev(c), 0)),
                  pl.BlockSpec((1, ML_HEADS, ML_DQK, ML_DV), lambda c: (rev(c), 0, 0, 0)),
                  pl.BlockSpec((1, ML_HEADS, 8, LANE), lambda c: (rev(c), 0, 0, 0)), pl.BlockSpec(memory_space=pl.ANY)],
        out_specs=[pl.BlockSpec((L, 2048), lambda c: (rev(c), 0)), pl.BlockSpec((L, 2048), lambda c: (rev(c), O_V // 2048)),
                   pl.BlockSpec((L, SMALL_W), lambda c: (rev(c), 0))],
        out_shape=[jax.ShapeDtypeStruct((s, 2048), BF16), jax.ShapeDtypeStruct(dproj.shape, dproj.dtype),
                   jax.ShapeDtypeStruct((s, SMALL_W), BF16)],
        input_output_aliases={8: 1},
        scratch_shapes=[pltpu.VMEM((ML_HEADS, ML_DQK, ML_DV), F32), pltpu.VMEM((ML_HEADS, 8, LANE), F32),
                        pltpu.VMEM((L, LANE), F32), pltpu.VMEM((LANE, L), F32)],
        compiler_params=_cparams("arbitrary"))(qk, proj, gates, gt, hout, dh, cst, nm, dproj)


GROUP_W = SSM_HEADS // SSM_GROUPS * SSM_HEADDIM
O_B = SSM_HEADS * SSM_HEADDIM
O_C = O_B + SSM_GROUPS * SSM_STATE


def _head_expand():
    r = jnp.arange(LANE)[:, None]
    c = jnp.arange(SSM_HEADS * SSM_HEADDIM)[None, :] // SSM_HEADDIM
    return (r == c).astype(F32)


def _ssd_gates(dt_ref, dtt_ref, alog_row_ref, alog_col_ref, at_scr):
    L = dt_ref.shape[0]
    dt = _softplus(dt_ref[...])
    acoef = -jnp.exp(alog_row_ref[...])
    a = _pick_left(_tri(L, False), dt * acoef, 3)
    at_scr[...] = _pick_right(_softplus(dtt_ref[...]) * (-jnp.exp(alog_col_ref[...])), _tri(L, True), 3)
    return dt, acoef, a


def _ssd_group(g, xbc_ref, dt, a, e_ref, ax_scr):
    eg = e_ref[:, g * GROUP_W:(g + 1) * GROUP_W]
    ax_scr[...] = _pick_right(a, eg, 3)
    ax = ax_scr[...]
    alx = ax_scr[ax.shape[0] - 1:ax.shape[0], :]
    dtx = _pick_right(dt, eg, 2)
    xg = xbc_ref[:, g * GROUP_W:(g + 1) * GROUP_W]
    bg = xbc_ref[:, O_B + g * SSM_STATE:O_B + (g + 1) * SSM_STATE]
    cg = xbc_ref[:, O_C + g * SSM_STATE:O_C + (g + 1) * SSM_STATE]
    return dict(ax=ax, alx=alx, dtx=dtx, xg=xg, bg=bg, cg=cg, xdt=xg * dtx, gmat=_nt(_bf(cg), _bf(bg)))


def _ssd_decay(hh, a, at_scr):
    L = a.shape[0]
    causal = lax.broadcasted_iota(jnp.int32, (L, L), 0) >= lax.broadcasted_iota(jnp.int32, (L, L), 1)
    return jnp.exp(jnp.where(causal, _lane_col(a, hh) - at_scr[hh:hh + 1, :], NEG))


def _ssd_fwd(xbc, gates, dtt, alog_row, alog_col, dskip_x, expand):
    s = xbc.shape[0]
    L = CHUNK
    nc = s // L
    half = SSM_HEADDIM

    def body(xbc_ref, dt_ref, dtt_ref, ar_ref, ac_ref, dk_ref, e_ref, y_ref, st_ref, st_scr, at_scr, ax_scr):
        @pl.when(pl.program_id(0) == 0)
        def _():
            st_scr[...] = jnp.zeros_like(st_scr)
        dt, _, a = _ssd_gates(dt_ref, dtt_ref, ar_ref, ac_ref, at_scr)
        lane = lax.broadcasted_iota(jnp.int32, (L, LANE), 1)
        for g in range(SSM_GROUPS):
            t = _ssd_group(g, xbc_ref, dt, a, e_ref, ax_scr)
            st = st_scr[g]
            st_ref[0, g] = st
            pairs = []
            for j in range(GROUP_W // LANE):
                xp = _bf(t["xdt"][:, j * LANE:(j + 1) * LANE])
                hh = g * (SSM_HEADS // SSM_GROUPS) + 2 * j
                both = jnp.concatenate([_bf(t["gmat"] * _ssd_decay(hh, a, at_scr)),
                                        _bf(t["gmat"] * _ssd_decay(hh + 1, a, at_scr))], axis=0)
                ys = _nn(both, xp)
                pairs.append(jnp.where(lane < half, ys[0:L], ys[L:2 * L]))
            y = jnp.concatenate(pairs, axis=1) + _nn(_bf(t["cg"]), _bf(st)) * jnp.exp(t["ax"])
            y_ref[:, g * GROUP_W:(g + 1) * GROUP_W] = _bf(y + dk_ref[:, g * GROUP_W:(g + 1) * GROUP_W] * t["xg"])
            wts = jnp.exp(t["alx"] - t["ax"])
            st_scr[g] = jnp.exp(t["alx"]) * st + _tn(_bf(t["bg"]), _bf(t["xdt"] * wts))

    row = lambda w: pl.BlockSpec((1, w), lambda c: (0, 0))
    return pl.pallas_call(
        body, name="ssd_fwd", grid=(nc,),
        in_specs=[pl.BlockSpec((L, 3072), lambda c: (c, 0)), pl.BlockSpec((L, LANE), lambda c: (c, (O_DT - O_IF) // LANE)),
                  pl.BlockSpec((LANE, L), lambda c: (0, c)), row(LANE), pl.BlockSpec((LANE, 1), lambda c: (0, 0)),
                  row(2048), pl.BlockSpec((LANE, 2048), lambda c: (0, 0))],
        out_specs=[pl.BlockSpec((L, 2048), lambda c: (c, 0)),
                   pl.BlockSpec((1, SSM_GROUPS, SSM_STATE, GROUP_W), lambda c: (c, 0, 0, 0))],
        out_shape=[jax.ShapeDtypeStruct((s, 2048), BF16),
                   jax.ShapeDtypeStruct((nc, SSM_GROUPS, SSM_STATE, GROUP_W), F32)],
        scratch_shapes=[pltpu.VMEM((SSM_GROUPS, SSM_STATE, GROUP_W), F32), pltpu.VMEM((LANE, L), F32),
                        pltpu.VMEM((L, GROUP_W), F32)],
        compiler_params=_cparams("arbitrary"))(xbc, gates, dtt, alog_row, alog_col, dskip_x, expand)


def _ssd_bwd(xbc, gates, dtt, alog_row, alog_col, dskip_x, expand, expand_t, dy, states):
    s = xbc.shape[0]
    L = CHUNK
    nc = s // L
    half = SSM_HEADDIM

    def body(xbc_ref, dt_ref, dtt_ref, ar_ref, ac_ref, dk_ref, e_ref, et_ref, dy_ref, st_ref,
             dxbc_ref, ddt_ref, accd_ref, acca_ref, dst_scr, at_scr, ax_scr):
        @pl.when(pl.program_id(0) == 0)
        def _():
            dst_scr[...] = jnp.zeros_like(dst_scr)
            accd_ref[...] = jnp.zeros_like(accd_ref)
            acca_ref[...] = jnp.zeros_like(acca_ref)
        dt, acoef, a = _ssd_gates(dt_ref, dtt_ref, ar_ref, ac_ref, at_scr)
        lane = lax.broadcasted_iota(jnp.int32, (L, LANE), 1)
        low = lane < half
        last = lax.broadcasted_iota(jnp.int32, (L, 1), 0) == L - 1
        cross = [jnp.zeros((L, LANE), F32)] * 3
        ddt_tile = jnp.zeros((L, LANE), F32)
        for g in range(SSM_GROUPS):
            t = _ssd_group(g, xbc_ref, dt, a, e_ref, ax_scr)
            xg, bg, cg, xdt, gmat = t["xg"], t["bg"], t["cg"], t["xdt"], t["gmat"]
            st, dst = st_ref[0, g], dst_scr[g]
            dyg = dy_ref[:, g * GROUP_W:(g + 1) * GROUP_W].astype(F32)
            ea, eal = jnp.exp(t["ax"]), jnp.exp(t["alx"])
            wts = jnp.exp(t["alx"] - t["ax"])
            dyi = dyg * ea
            y_inter = _nn(_bf(cg), _bf(st)) * ea
            dc = _nt(_bf(dyi), _bf(st))
            d_xdt_state = _nn(_bf(bg), _bf(dst)) * wts
            db = _nt(_bf(xdt * wts), _bf(dst))
            dst_scr[g] = eal * dst + _tn(_bf(cg), _bf(dyi))
            dg = jnp.zeros((L, L), F32)
            dx_pairs, dlogw = [], []
            for j in range(GROUP_W // LANE):
                xp = _bf(xdt[:, j * LANE:(j + 1) * LANE])
                dyp = dyg[:, j * LANE:(j + 1) * LANE]
                hh = g * (SSM_HEADS // SSM_GROUPS) + 2 * j
                decs = [_ssd_decay(hh, a, at_scr), _ssd_decay(hh + 1, a, at_scr)]
                ws = [gmat * decs[0], gmat * decs[1]]
                dxs = _tn(_bf(jnp.concatenate(ws, axis=1)), _bf(dyp))
                dws = _nt(_bf(jnp.concatenate([jnp.where(low, dyp, 0.0), jnp.where(low, 0.0, dyp)], axis=0)), xp)
                dw0, dw1 = dws[0:L], dws[L:2 * L]
                dg = dg + dw0 * decs[0] + dw1 * decs[1]
                dlogw += [dw0 * ws[0], dw1 * ws[1]]
                dx_pairs.append(jnp.where(low, dxs[0:L], dxs[L:2 * L]))
            for b, col in enumerate(_crossing(jnp.concatenate(dlogw, axis=1))):
                cross[0] = cross[0] + jnp.where(lane == g * (SSM_HEADS // SSM_GROUPS) + b, col, 0.0)
            d_xdt = d_xdt_state + jnp.concatenate(dx_pairs, axis=1)
            dc = dc + _nn(_bf(dg), _bf(bg))
            db = db + _tn(_bf(dg), _bf(cg))
            etg = et_ref[g * GROUP_W:(g + 1) * GROUP_W, :]
            carried = jnp.sum(dst * st, axis=0, keepdims=True) * eal
            cross[1] = cross[1] + _pick_right(dyg * y_inter + jnp.where(last, carried, 0.0), etg, 2)
            cross[2] = cross[2] + _pick_right(xdt * d_xdt_state, etg, 2)
            ddt_tile = ddt_tile + _pick_right(d_xdt * xg, etg, 2)
            dxbc_ref[:, g * GROUP_W:(g + 1) * GROUP_W] = _bf(d_xdt * t["dtx"] + dk_ref[:, g * GROUP_W:(g + 1) * GROUP_W] * dyg)
            dxbc_ref[:, O_B + g * SSM_STATE:O_B + (g + 1) * SSM_STATE] = _bf(db)
            dxbc_ref[:, O_C + g * SSM_STATE:O_C + (g + 1) * SSM_STATE] = _bf(dc)
            accd_ref[0:1, g * GROUP_W:(g + 1) * GROUP_W] += jnp.sum(dyg * xg, axis=0, keepdims=True)
        d_da = cross[0] + _pick_left(_tri(L, True), cross[1], 2) + _pick_left(_tri(L, False) - _eye(L), cross[2], 2)
        acca_ref[0:1, :] += jnp.sum(d_da * dt, axis=0, keepdims=True)
        ddt_ref[:, 0:LANE] = _bf((ddt_tile + d_da * acoef) * _sigmoid(dt_ref[...]))
        ddt_ref[:, LANE:SMALL_W] = jnp.zeros((L, SMALL_W - LANE), BF16)

    rev = lambda c: nc - 1 - c
    row = lambda w: pl.BlockSpec((1, w), lambda c: (0, 0))
    return pl.pallas_call(
        body, name="ssd_bwd", grid=(nc,),
        in_specs=[pl.BlockSpec((L, 3072), lambda c: (rev(c), 0)), pl.BlockSpec((L, LANE), lambda c: (rev(c), (O_DT - O_IF) // LANE)),
                  pl.BlockSpec((LANE, L), lambda c: (0, rev(c))), row(LANE), pl.BlockSpec((LANE, 1), lambda c: (0, 0)),
                  row(2048), pl.BlockSpec((LANE, 2048), lambda c: (0, 0)), pl.BlockSpec((2048, LANE), lambda c: (0, 0)),
                  pl.BlockSpec((L, 2048), lambda c: (rev(c), 0)),
                  pl.BlockSpec((1, SSM_GROUPS, SSM_STATE, GROUP_W), lambda c: (rev(c), 0, 0, 0))],
        out_specs=[pl.BlockSpec((L, 3072), lambda c: (rev(c), 0)), pl.BlockSpec((L, SMALL_W), lambda c: (rev(c), 0)),
                   pl.BlockSpec((8, 2048), lambda c: (0, 0)), pl.BlockSpec((8, LANE), lambda c: (0, 0))],
        out_shape=[jax.ShapeDtypeStruct((s, 3072), BF16), jax.ShapeDtypeStruct((s, SMALL_W), BF16),
                   jax.ShapeDtypeStruct((8, 2048), F32), jax.ShapeDtypeStruct((8, LANE), F32)],
        scratch_shapes=[pltpu.VMEM((SSM_GROUPS, SSM_STATE, GROUP_W), F32),
                        pltpu.VMEM((LANE, L), F32), pltpu.VMEM((L, GROUP_W), F32)],
        compiler_params=_cparams("arbitrary"))(xbc, gates, dtt, alog_row, alog_col, dskip_x, expand, expand_t, dy, states)


def _group_norm(v, width):
    outs, rs = [], []
    for k in range(v.shape[1] // width):
        blk = v[:, k * width:(k + 1) * width]
        r = lax.rsqrt(jnp.mean(blk * blk, axis=1, keepdims=True) + EPS)
        outs.append(blk * r)
        rs.append(jnp.broadcast_to(r, blk.shape))
    return jnp.concatenate(outs, axis=1), jnp.concatenate(rs, axis=1)


def _group_mean(v, width):
    return jnp.concatenate([jnp.broadcast_to(jnp.mean(v[:, k * width:(k + 1) * width], axis=1, keepdims=True),
                                             (v.shape[0], width)) for k in range(v.shape[1] // width)], axis=1)


def _post_fwd(hm, yssd, proj, ml_norm_w, ssm_norm_w, ts):
    s = hm.shape[0]

    def body(h_ref, ys_ref, o_ref, zm_ref, zs_ref, wm_ref, ws_ref, ym_ref, yso_ref):
        hn, _ = _group_norm(h_ref[...].astype(F32), ML_DV)
        ym_ref[...] = _bf(_sigmoid(o_ref[...].astype(F32)) * hn * wm_ref[...] * _silu(zm_ref[...].astype(F32)))
        pn, _ = _group_norm(ys_ref[...].astype(F32) * _silu(zs_ref[...].astype(F32)), GROUP_W)
        yso_ref[...] = _bf(pn * ws_ref[...])

    tile = pl.BlockSpec((ts, 2048), lambda i: (i, 0))
    col = lambda off: pl.BlockSpec((ts, 2048), lambda i: (i, off // 2048))
    row = pl.BlockSpec((1, 2048), lambda i: (0, 0))
    return pl.pallas_call(
        body, name="post_fwd", grid=(s // ts,),
        in_specs=[tile, tile, col(O_O), col(O_ZM), col(O_ZS), row, row],
        out_specs=[tile, tile],
        out_shape=[jax.ShapeDtypeStruct((s, 2048), BF16)] * 2,
        compiler_params=_cparams("parallel"))(hm, yssd, proj, proj, proj, ml_norm_w, ssm_norm_w)


def _post_bwd(dym, dys, hm, yssd, proj, ml_norm_w, ssm_norm_w, dproj, ts):
    s = hm.shape[0]

    def body(dym_ref, dys_ref, h_ref, ys_ref, o_ref, zm_ref, zs_ref, wm_ref, ws_ref, _,
             dh_ref, dyssd_ref, dp_ref, acc_ref):
        @pl.when(pl.program_id(0) == 0)
        def _():
            acc_ref[...] = jnp.zeros_like(acc_ref)
        hn, r = _group_norm(h_ref[...].astype(F32), ML_DV)
        so, zm, wm, d_ym = _sigmoid(o_ref[...].astype(F32)), zm_ref[...].astype(F32), wm_ref[...], dym_ref[...].astype(F32)
        sz = _silu(zm)
        hnw = hn * wm
        dp_ref[:, O_O:O_O + 2048] = _bf(d_ym * hnw * sz * so * (1.0 - so))
        dp_ref[:, O_ZM:O_ZM + 2048] = _bf(d_ym * so * hnw * _dsilu(zm))
        dhnw = d_ym * so * sz
        acc_ref[0:1, :] += jnp.sum(dhnw * hn, axis=0, keepdims=True)
        dhn = dhnw * wm
        dh_ref[...] = _bf(r * (dhn - hn * _group_mean(dhn * hn, ML_DV)))
        ysv, zs, d_ys = ys_ref[...].astype(F32), zs_ref[...].astype(F32), dys_ref[...].astype(F32)
        szs = _silu(zs)
        pn, r2 = _group_norm(ysv * szs, GROUP_W)
        acc_ref[1:2, :] += jnp.sum(d_ys * pn, axis=0, keepdims=True)
        dpn = d_ys * ws_ref[...]
        dp = r2 * (dpn - pn * _group_mean(dpn * pn, GROUP_W))
        dyssd_ref[...] = _bf(dp * szs)
        dp_ref[:, O_ZS:O_ZS + 2048] = _bf(dp * ysv * _dsilu(zs))

    tile = pl.BlockSpec((ts, 2048), lambda i: (i, 0))
    col = lambda off: pl.BlockSpec((ts, 2048), lambda i: (i, off // 2048))
    row = pl.BlockSpec((1, 2048), lambda i: (0, 0))
    sds = lambda dt: jax.ShapeDtypeStruct((s, 2048), dt)
    return pl.pallas_call(
        body, name="post_bwd", grid=(s // ts,),
        in_specs=[tile, tile, tile, tile, col(O_O), col(O_ZM), col(O_ZS), row, row, pl.BlockSpec(memory_space=pl.ANY)],
        out_specs=[tile, tile, pl.BlockSpec((ts, O_MG), lambda i: (i, 0)), pl.BlockSpec((8, 2048), lambda i: (0, 0))],
        out_shape=[sds(BF16), sds(BF16), jax.ShapeDtypeStruct(dproj.shape, dproj.dtype), jax.ShapeDtypeStruct((8, 2048), F32)],
        input_output_aliases={9: 2},
        compiler_params=_cparams("arbitrary"))(dym, dys, hm, yssd, proj, proj, proj, ml_norm_w, ssm_norm_w, dproj)


def _merge(x, ym, ys, proj, target, gate, final_w, wpm, wps, wo, ts):
    wpm_t, wps_t, wo_t = wpm.T, wps.T, wo.T
    s, d = x.shape

    def body(x_ref, ym_ref, ys_ref, mg_ref, t_ref, gate_ref, fw_ref, wpm_ref, wps_ref, wo_ref, wpmt_ref, wpst_ref, wot_ref,
             dres_ref, mer_ref, dmo_ref, dpm_ref, dps_ref, dym_ref, dys_ref, dmg_ref, acc_ref):
        @pl.when(pl.program_id(0) == 0)
        def _():
            acc_ref[...] = jnp.zeros_like(acc_ref)
        gm, gs = _sigmoid(mg_ref[:, 0:d].astype(F32)), _sigmoid(mg_ref[:, d:2 * d].astype(F32))
        pm = _nn(ym_ref[...], wpm_ref[...])
        ps = _nn(ys_ref[...], wps_ref[...])
        merged = _bf(gm * pm + gs * ps)
        mer_ref[...] = merged
        mo = _nn(merged, wo_ref[...])
        gate, fw = gate_ref[...], fw_ref[...]
        out = x_ref[...] + gate * mo
        r = lax.rsqrt(jnp.mean(out * out, axis=1, keepdims=True) + EPS)
        on = out * r
        diff = on * fw - t_ref[...]
        acc_ref[0:1, :] += jnp.sum(0.5 * jnp.sum(diff * diff, axis=1, keepdims=True) / d, axis=0, keepdims=True)
        dyv = diff * (1.0 / d)
        acc_ref[1:2, :] += jnp.sum(dyv * on, axis=0, keepdims=True)
        don = dyv * fw
        dout = r * (don - on * jnp.mean(don * on, axis=1, keepdims=True))
        dres_ref[...] = dout
        acc_ref[2:3, :] += jnp.sum(dout * mo, axis=0, keepdims=True)
        dmo = _bf(dout * gate)
        dmo_ref[...] = dmo
        dmer = _nn(dmo, wot_ref[...])
        dpm, dps = _bf(dmer * gm), _bf(dmer * gs)
        dpm_ref[...] = dpm
        dps_ref[...] = dps
        dmg_ref[:, 0:d] = _bf(dmer * pm * gm * (1.0 - gm))
        dmg_ref[:, d:2 * d] = _bf(dmer * ps * gs * (1.0 - gs))
        dym_ref[...] = _bf(_nn(dpm, wpmt_ref[...]))
        dys_ref[...] = _bf(_nn(dps, wpst_ref[...]))

    t1 = pl.BlockSpec((ts, d), lambda i: (i, 0))
    t2 = pl.BlockSpec((ts, 2 * d), lambda i: (i, 0))
    row = pl.BlockSpec((1, d), lambda i: (0, 0))
    whole = pl.BlockSpec(memory_space=pltpu.VMEM)
    sd = lambda w, dt: jax.ShapeDtypeStruct((s, w), dt)
    return pl.pallas_call(
        body, name="merge_fwd_bwd", grid=(s // ts,),
        in_specs=[t1, t2, t2, pl.BlockSpec((ts, 2 * d), lambda i: (i, O_MG // (2 * d))), t1, row, row] + [whole] * 6,
        out_specs=[t1, t1, t1, t1, t1, t2, t2, pl.BlockSpec((ts, 2 * d), lambda i: (i, O_MG // (2 * d))),
                   pl.BlockSpec((8, d), lambda i: (0, 0))],
        out_shape=[sd(d, F32), sd(d, BF16), sd(d, BF16), sd(d, BF16), sd(d, BF16), sd(2 * d, BF16), sd(2 * d, BF16),
                   sd(NP, BF16), jax.ShapeDtypeStruct((8, d), F32)],
        compiler_params=_cparams("arbitrary", vmem=MERGE_VMEM))(x, ym, ys, proj, target, gate, final_w, wpm, wps, wo, wpm_t, wps_t, wo_t)


def _adamw(w, g, m, v, tr):
    if w.ndim == 2 and w.shape[0] % 8:
        tile, steps = pl.BlockSpec((w.shape[0], tr), lambda i: (0, i)), w.shape[1] // tr
    else:
        lead = (None,) * (w.ndim - 2)
        tile, steps = pl.BlockSpec(lead + (tr, w.shape[-1]), lambda i: (0,) * len(lead) + (i, 0)), w.shape[-2] // tr

    def body(w_ref, g_ref, m_ref, v_ref, d_ref, nm_ref, nv_ref):
        gv = g_ref[...]
        m2 = ADAM_B1 * m_ref[...] + (1.0 - ADAM_B1) * gv
        v2 = ADAM_B2 * v_ref[...] + (1.0 - ADAM_B2) * (gv * gv)
        m_hat = m2 / (1.0 - ADAM_B1 ** ADAM_STEP)
        v_hat = v2 / (1.0 - ADAM_B2 ** ADAM_STEP)
        d_ref[...] = -ADAM_LR * (m_hat / (jnp.sqrt(v_hat) + ADAM_EPS) + ADAM_WD * w_ref[...])
        nm_ref[...] = m2
        nv_ref[...] = v2

    return pl.pallas_call(
        body, name="adamw", grid=(steps,), in_specs=[tile] * 4, out_specs=[tile] * 3,
        out_shape=[jax.ShapeDtypeStruct(w.shape, F32)] * 3,
        compiler_params=_cparams("parallel"))(w, g.reshape(w.shape), m, v)


def _sum_parts(own, parts, tr, dtype=F32, slot=None):
    p, rows, cols = parts.shape

    def body(*refs):
        p_ref, o_ref = refs[-2], refs[-1]
        first = None if own is None else refs[-3]
        acc = p_ref[0].astype(F32) if first is None else first[...].astype(F32) + p_ref[0].astype(F32)
        for i in range(1, p):
            acc = acc + p_ref[i].astype(F32)
        o_ref[...] = acc.astype(dtype)

    args = ([] if own is None else [own]) + [parts]
    if slot is None:
        tile = pl.BlockSpec((tr, cols), lambda i: (i, 0))
        ins = ([] if own is None else [tile]) + [pl.BlockSpec((p, tr, cols), lambda i: (0, i, 0))]
        return pl.pallas_call(
            body, name="sum_parts", grid=(rows // tr,), in_specs=ins, out_specs=tile,
            out_shape=jax.ShapeDtypeStruct((rows, cols), dtype), compiler_params=_cparams("parallel"))(*args)
    tile = pl.BlockSpec((tr, cols), lambda i, s: (i, 0))
    ins = ([] if own is None else [tile]) + [pl.BlockSpec((p, tr, cols), lambda i, s: (0, i, 0))]
    return pl.pallas_call(
        body, name="sum_parts_half", out_shape=jax.ShapeDtypeStruct((2, rows, cols), dtype),
        grid_spec=pltpu.PrefetchScalarGridSpec(
            num_scalar_prefetch=1, grid=(rows // tr,), in_specs=ins,
            out_specs=pl.BlockSpec((None, tr, cols), lambda i, s: (s[0], i, 0))),
        compiler_params=_cparams("parallel"))(jnp.reshape(slot, (1,)).astype(jnp.int32), *args)


def _position():
    return lax.axis_index("x"), lax.axis_index("y"), lax.axis_index("c")


def _flip(pos, k):
    return tuple(1 - p if (k >> s) & 1 else p for p, s in zip(pos, (2, 1, 0)))


def _allgather8(block):
    rows, cols = block.shape

    def body(x_ref, o_ref, send_sems, recv_sems, local_sem):
        pos = _position()
        me = 4 * pos[0] + 2 * pos[1] + pos[2]
        mine = pltpu.make_async_copy(x_ref, o_ref.at[me], local_sem)
        mine.start()
        copies = [pltpu.make_async_remote_copy(src_ref=x_ref, dst_ref=o_ref.at[me], send_sem=send_sems.at[k - 1],
                                               recv_sem=recv_sems.at[k - 1], device_id=_flip(pos, k), device_id_type=MESH)
                  for k in range(1, N_DEV)]
        for cp in copies:
            cp.start()
        for cp in copies:
            cp.wait()
        mine.wait()

    vmem = pl.BlockSpec(memory_space=pltpu.VMEM)
    return pl.pallas_call(
        body, name="allgather8", in_specs=[vmem], out_specs=vmem,
        out_shape=jax.ShapeDtypeStruct((N_DEV, rows, cols), block.dtype),
        scratch_shapes=[pltpu.SemaphoreType.DMA((N_DEV - 1,)), pltpu.SemaphoreType.DMA((N_DEV - 1,)),
                        pltpu.SemaphoreType.DMA],
        compiler_params=pltpu.CompilerParams(vmem_limit_bytes=VMEM_LIMIT))(block)


COPY_BYTES = 1 << 20


def _row_chunks(rows, row_bytes):
    n = max(1, min(rows // 16, -(-rows * row_bytes // COPY_BYTES)))
    while rows % (16 * n):
        n -= 1
    return [(i * (rows // n), rows // n) for i in range(n)]


def _split_start(name, make_copies, n_copies, sources, lands):
    n, m = len(sources), len(lands)

    def body(*refs):
        for cp in make_copies(_position(), refs[:n], refs[n:n + m], refs[n + m], refs[n + m + 1]):
            cp.start()
        refs[-1][...] = jnp.zeros((8, LANE), F32)

    hbm = pl.BlockSpec(memory_space=pltpu.HBM)
    sem = pl.BlockSpec(memory_space=pltpu.SEMAPHORE)
    operands = [pltpu.with_memory_space_constraint(t, pltpu.HBM) for t in list(sources) + list(lands)]
    out = pl.pallas_call(
        body, name=name, in_specs=[hbm] * (n + m),
        out_specs=[sem, sem] + [hbm] * (n + m) + [pl.BlockSpec(memory_space=pltpu.VMEM)],
        out_shape=[pltpu.SemaphoreType.DMA((n_copies,)), pltpu.SemaphoreType.DMA((n_copies,))]
        + [pltpu.HBM(t.shape, t.dtype) for t in operands] + [jax.ShapeDtypeStruct((8, LANE), F32)],
        input_output_aliases={i: 2 + i for i in range(n + m)},
        compiler_params=pltpu.CompilerParams(has_side_effects=pltpu.SideEffectType.DATAFLOW_SIDE_EFFECTING))(*operands)
    return out[0], out[1], out[2:2 + n], out[2 + n:2 + n + m], out[-1]


def _split_wait(name, make_copies, send_sems, recv_sems, sources, lands, after):
    n, m = len(sources), len(lands)

    def body(*refs):
        for cp in make_copies(_position(), refs[:n], refs[n:n + m], refs[n + m], refs[n + m + 1]):
            cp.wait_send()
            cp.wait_recv()

    hbm = pl.BlockSpec(memory_space=pltpu.HBM)
    sem = pl.BlockSpec(memory_space=pltpu.SEMAPHORE)
    out = pl.pallas_call(
        body, name=name, in_specs=[hbm] * (n + m) + [sem, sem, pl.BlockSpec(memory_space=pl.ANY)],
        out_specs=[hbm] * (n + m), out_shape=[pltpu.HBM(t.shape, t.dtype) for t in list(sources) + list(lands)],
        input_output_aliases={i: i for i in range(n + m)},
        compiler_params=pltpu.CompilerParams(has_side_effects=pltpu.SideEffectType.DATAFLOW_SIDE_EFFECTING))(
            *sources, *lands, send_sems, recv_sems, after)
    return out[:n], out[n:]


def _gather_copies(pos, halves, lands, send_sems, recv_sems):
    chip, core = 2 * pos[0] + pos[1], pos[2]
    copies = []
    for a in range(len(halves)):
        for k in range(1, N_CHIPS):
            for r0, nr in _row_chunks(halves[a].shape[1], halves[a].shape[2] * halves[a].dtype.itemsize):
                i = len(copies)
                copies.append(pltpu.make_async_remote_copy(
                    src_ref=halves[a].at[core, pl.ds(r0, nr)], dst_ref=lands[a].at[chip, core, pl.ds(r0, nr)],
                    send_sem=send_sems.at[i], recv_sem=recv_sems.at[i], device_id=_flip(pos, 2 * k), device_id_type=MESH))
    return copies


def _gather_pieces(halves):
    return (N_CHIPS - 1) * sum(len(_row_chunks(a.shape[1], a.shape[2] * a.dtype.itemsize)) for a in halves)


def _pair_forward(lands):
    n = len(lands)

    def plan(pos, ins, outs):
        remote = []
        for a in range(n):
            for k in range(1, N_CHIPS):
                there = _flip(pos, 2 * k)
                for r0, nr in _row_chunks(lands[a].shape[2], lands[a].shape[3] * lands[a].dtype.itemsize):
                    slot = (2 * there[0] + there[1], pos[2], pl.ds(r0, nr))
                    remote.append((ins[a].at[slot], outs[a].at[slot], _flip(pos, 1)))
        return remote, []

    return _exchange("pair_forward", lands, [jax.ShapeDtypeStruct(t.shape, t.dtype) for t in lands], plan,
                     _gather_pieces([jax.ShapeDtypeStruct(t.shape[1:], t.dtype) for t in lands]), 0, in_place=True)


def _exchange(name, arrays, out_shapes, plan, n_remote, n_local, in_place=False):
    n, m = len(arrays), len(out_shapes)

    def body(*refs):
        send_sems, recv_sems, local_sems = refs[n + m:]
        remote, local = plan(_position(), refs[:n], refs[n:n + m])
        assert (len(remote), len(local)) == (n_remote, n_local)
        copies = [pltpu.make_async_copy(src, dst, local_sems.at[i]) for i, (src, dst) in enumerate(local)]
        copies += [pltpu.make_async_remote_copy(src_ref=src, dst_ref=dst, send_sem=send_sems.at[i], recv_sem=recv_sems.at[i],
                                                device_id=dev, device_id_type=MESH)
                   for i, (src, dst, dev) in enumerate(remote)]
        for cp in copies:
            cp.start()
        for cp in copies:
            cp.wait()

    hbm = pl.BlockSpec(memory_space=pl.ANY)
    return pl.pallas_call(
        body, name=name, in_specs=[hbm] * n, out_specs=[hbm] * m, out_shape=out_shapes,
        input_output_aliases={i: i for i in range(n)} if in_place else {},
        scratch_shapes=[pltpu.SemaphoreType.DMA((n_remote,)), pltpu.SemaphoreType.DMA((n_remote,)),
                        pltpu.SemaphoreType.DMA((max(n_local, 1),))],
        compiler_params=pltpu.CompilerParams(has_side_effects=True))(*arrays)


def _pair_send(slabs):
    n = len(slabs)
    pieces = [_row_chunks(g.shape[2], g.shape[3] * g.dtype.itemsize) for g in slabs]

    def plan(pos, ins, outs):
        return [(ins[a].at[j, 1 - pos[2], pl.ds(r0, nr)], outs[a].at[j, pl.ds(r0, nr)], _flip(pos, 1))
                for a in range(n) for j in range(N_CHIPS) for r0, nr in pieces[a]], []

    return _exchange("pair_send", slabs, [jax.ShapeDtypeStruct((N_CHIPS,) + g.shape[2:], g.dtype) for g in slabs], plan,
                     N_CHIPS * sum(len(p) for p in pieces), 0)


def _chip_scatter_copies(pos, sums, lands, send_sems, recv_sems):
    copies = []
    for a in range(len(sums)):
        for k in range(1, N_CHIPS):
            to = _flip(pos, 2 * k)
            for r0, nr in _row_chunks(sums[a].shape[1], sums[a].shape[2] * sums[a].dtype.itemsize):
                i = len(copies)
                copies.append(pltpu.make_async_remote_copy(
                    src_ref=sums[a].at[2 * to[0] + to[1], pl.ds(r0, nr)], dst_ref=lands[a].at[k - 1, pl.ds(r0, nr)],
                    send_sem=send_sems.at[i], recv_sem=recv_sems.at[i], device_id=to, device_id_type=MESH))
    return copies


def _chip_scatter_start(sums):
    n_copies = (N_CHIPS - 1) * sum(len(_row_chunks(g.shape[1], g.shape[2] * g.dtype.itemsize)) for g in sums)
    lands = [lax.empty((N_CHIPS - 1,) + g.shape[1:], g.dtype) for g in sums]
    return _split_start("chip_scatter_start", _chip_scatter_copies, n_copies, sums, lands)


def _chip_scatter_wait(send_sems, recv_sems, sums, lands, after):
    return _split_wait("chip_scatter_wait", _chip_scatter_copies, send_sems, recv_sems, sums, lands, after)


def _pair_exchange(pairs):
    n = len(pairs)
    pieces = [_row_chunks(h.shape[1], h.shape[2] * h.dtype.itemsize) for h in pairs]

    def plan(pos, ins, outs):
        return [(ins[a].at[pos[2], pl.ds(r0, nr)], outs[a].at[pos[2], pl.ds(r0, nr)], _flip(pos, 1))
                for a in range(n) for r0, nr in pieces[a]], []

    return _exchange("pair_exchange", pairs, [jax.ShapeDtypeStruct(h.shape, h.dtype) for h in pairs], plan,
                     sum(len(p) for p in pieces), 0, in_place=True)


def _pack(arrays):
    flat = jnp.concatenate([a.reshape(-1).astype(F32) for a in arrays])
    size = -(-flat.shape[0] // (8 * LANE)) * (8 * LANE)
    return jnp.pad(flat, (0, size - flat.shape[0])).reshape(size // LANE, LANE)


def _unpack(buf, shapes):
    flat = buf.reshape(-1)
    out, off = [], 0
    for shp in shapes:
        n = math.prod(shp)
        out.append(flat[off:off + n].reshape(shp))
        off += n
    return out


def _unpack_rows(bufs, shapes):
    flat = bufs.reshape(bufs.shape[0], -1)
    out, off = [], 0
    for shp in shapes:
        n = math.prod(shp)
        out.append(flat[:, off:off + n].reshape((bufs.shape[0],) + shp))
        off += n
    return out


def _taps8(w):
    return jnp.pad(w, ((0, 8 - CONV_K), (0, 0)))


def _local_step(xs, tgt, scale, shift, gate, norm_w, w_in_p, b_in_p, ml_conv_w, ml_conv_b, ml_norm_w, ssm_conv_w,
                ssm_conv_b, ssm_a_log, ssm_d, ssm_norm_w, wpm, wps, wo, final_w, start_exchange=None, late_weights=None,
                u=None):
    s = xs.shape[0]
    ts, tm, tw, tn = min(ROWS_ELEMENTWISE, s), min(ROWS_MATMUL, s), min(ROWS_WIDE_MATMUL, s), COLS_MATMUL
    if u is None:
        u = _prenorm_fwd(xs, norm_w, scale, shift, ts)
    proj = _matmul_bias(u, w_in_p, b_in_p, tw, tn, 0, O_IF, BF16)
    gates = _matmul_bias(u, w_in_p, b_in_p, tw, tn, O_IF, NP - O_IF, F32)
    mlw8, ssw8 = _taps8(ml_conv_w), _taps8(ssm_conv_w)
    qk, qk_dact = _conv_fwd(proj, O_QK, 2048, mlw8, ml_conv_b, ts)
    xbc, xbc_dact = _conv_fwd(proj, O_XBC, 3072, ssw8, ssm_conv_b, ts)
    gt = gates[:, :LANE].T
    dtt = gates[:, O_DT - O_IF:O_DT - O_IF + LANE].T
    hm, cst, nm = _mlstm_fwd(qk, proj, gates, gt)
    alog_row = jnp.pad(ssm_a_log, ((0, 0), (0, LANE - SSM_HEADS)))
    alog_col = alog_row.reshape(LANE, 1)
    dskip_x = jnp.repeat(ssm_d[0], SSM_HEADDIM)[None]
    expand = _head_expand()
    yssd, sst = _ssd_fwd(xbc, gates, dtt, alog_row, alog_col, dskip_x, expand)
    tp = min(ROWS_POST, s)
    ym, ys = _post_fwd(hm, yssd, proj, ml_norm_w, ssm_norm_w, tp)
    if late_weights is not None:
        wpm, wps, wo = late_weights(ym)
    dxres, merged, dmo, dpm, dps, dym, dys, dproj, acc_m = _merge(xs, ym, ys, proj, tgt, gate, final_w, wpm, wps, wo,
                                                                  min(ROWS_MERGE, s))
    dh, dyssd, dproj, acc_p = _post_bwd(dym, dys, hm, yssd, proj, ml_norm_w, ssm_norm_w, dproj, tp)
    dqk, dproj, dif = _mlstm_bwd(qk, proj, gates, gt, hm, dh, cst, nm, dproj)
    dxbc, ddt, accd, acca = _ssd_bwd(xbc, gates, dtt, alog_row, alog_col, dskip_x, expand, expand.T, dyssd, sst)
    dproj, acc_cq = _conv_bwd(proj, O_QK, 2048, mlw8, qk_dact, dqk, dproj, ts)
    dproj, acc_cx = _conv_bwd(proj, O_XBC, 3072, ssw8, xbc_dact, dxbc, dproj, ts)
    dproj = dproj.at[:, O_IF:O_IF + SMALL_W].set(dif).at[:, O_DT:O_DT + SMALL_W].set(ddt)
    gw_in_p, gb_in_p = _matmul_tn(u.T, dproj, tw, tn, with_colsum=True, a_is_transposed=True, dtype=BF16)
    g_wpm = _matmul_tn(ym, dpm, tm, tn, dtype=BF16)
    g_wps = _matmul_tn(ys, dps, tm, tn, dtype=BF16)
    g_wo = _matmul_tn(merged, dmo, tm, tn, dtype=BF16)
    token, in_flight = (None, None) if start_exchange is None else start_exchange(gw_in_p, g_wpm, g_wps, g_wo)
    du = _matmul_nt(dproj, w_in_p, tm, tn, after=token)
    grad_x, acc_n = _prenorm_bwd(du, xs, dxres, norm_w, scale, ts)
    a_coef = -jnp.exp(ssm_a_log[0])
    small = dict(
        mod=jnp.concatenate([acc_n[2], acc_n[1], acc_m[2]]), norm_w=acc_n[0], b_in=_unpad_cols(gb_in_p[0]),
        ml_conv_w=acc_cq[0:CONV_K], ml_conv_b=acc_cq[CONV_K], ml_norm_w=acc_p[0], ssm_conv_w=acc_cx[0:CONV_K],
        ssm_conv_b=acc_cx[CONV_K], ssm_a_log=acca[0, :SSM_HEADS] * a_coef,
        ssm_d=accd[0].reshape(SSM_HEADS, SSM_HEADDIM).sum(axis=1), ssm_norm_w=acc_p[1], final_w=acc_m[1], loss=acc_m[0, 0:1])
    return grad_x, small, gw_in_p, g_wpm, g_wps, g_wo, in_flight


WEIGHTS = ("norm_w", "ada_w", "ada_b", "w_in", "b_in", "ml_conv_w", "ml_conv_b", "ml_norm_w", "ssm_conv_w", "ssm_conv_b",
           "ssm_a_log", "ssm_d", "ssm_norm_w", "w_proj_m", "w_proj_s", "w_out", "final_w")
LARGE = ("ada_w", "w_in", "w_proj_m", "w_proj_s", "w_out")
SMALL_SUMS = (("mod", (3 * D_MODEL,)), ("norm_w", (D_MODEL,)), ("b_in", (IN_WIDTH,)), ("ml_conv_w", (CONV_K, 2048)),
              ("ml_conv_b", (2048,)), ("ml_norm_w", (2048,)), ("ssm_conv_w", (CONV_K, 3072)), ("ssm_conv_b", (3072,)),
              ("ssm_a_log", (SSM_HEADS,)), ("ssm_d", (SSM_HEADS,)), ("ssm_norm_w", (2048,)), ("final_w", (D_MODEL,)),
              ("loss", (1,)))


def kernel(x, c, norm_w, ada_w, ada_b, w_in, b_in, ml_conv_w, ml_conv_b, ml_norm_w, ssm_conv_w, ssm_conv_b, ssm_a_log, ssm_d, ssm_norm_w, w_proj_m, w_proj_s, w_out, final_w, loss_target, m_norm_w, m_ada_w, m_ada_b, m_w_in, m_b_in, m_ml_conv_w, m_ml_conv_b, m_ml_norm_w, m_ssm_conv_w, m_ssm_conv_b, m_ssm_a_log, m_ssm_d, m_ssm_norm_w, m_w_proj_m, m_w_proj_s, m_w_out, m_final_w, v_norm_w, v_ada_w, v_ada_b, v_w_in, v_b_in, v_ml_conv_w, v_ml_conv_b, v_ml_norm_w, v_ssm_conv_w, v_ssm_conv_b, v_ssm_a_log, v_ssm_d, v_ssm_norm_w, v_w_proj_m, v_w_proj_s, v_w_out, v_final_w):
    w = dict(norm_w=norm_w, ada_w=ada_w, ada_b=ada_b, w_in=w_in, b_in=b_in, ml_conv_w=ml_conv_w, ml_conv_b=ml_conv_b,
             ml_norm_w=ml_norm_w, ssm_conv_w=ssm_conv_w, ssm_conv_b=ssm_conv_b, ssm_a_log=ssm_a_log, ssm_d=ssm_d,
             ssm_norm_w=ssm_norm_w, w_proj_m=w_proj_m, w_proj_s=w_proj_s, w_out=w_out, final_w=final_w)
    m = dict(zip(WEIGHTS, (m_norm_w, m_ada_w, m_ada_b, m_w_in, m_b_in, m_ml_conv_w, m_ml_conv_b, m_ml_norm_w, m_ssm_conv_w,
                           m_ssm_conv_b, m_ssm_a_log, m_ssm_d, m_ssm_norm_w, m_w_proj_m, m_w_proj_s, m_w_out, m_final_w)))
    v = dict(zip(WEIGHTS, (v_norm_w, v_ada_w, v_ada_b, v_w_in, v_b_in, v_ml_conv_w, v_ml_conv_b, v_ml_norm_w, v_ssm_conv_w,
                           v_ssm_conv_b, v_ssm_a_log, v_ssm_d, v_ssm_norm_w, v_w_proj_m, v_w_proj_s, v_w_out, v_final_w)))
    pos = _position()
    chip = 2 * pos[0] + pos[1]
    dev = 2 * chip + pos[2]
    mlw_cols, ssw_cols, ada_cols = ml_conv_w.shape[2], ssm_conv_w.shape[2], ada_w.shape[2]

    g0 = _allgather8(_pack([c, ml_conv_w, ssm_conv_w]))
    c_all, mlw_all, ssw_all = _unpack_rows(g0, [(D_MODEL,), (CONV_K, mlw_cols), (CONV_K, ssw_cols)])
    ml_conv_full = mlw_all[0::2].transpose(1, 0, 2).reshape(CONV_K, N_CHIPS * mlw_cols)
    ssm_conv_full = ssw_all[0::2].transpose(1, 0, 2).reshape(CONV_K, N_CHIPS * ssw_cols)

    ada_b_mine = lax.dynamic_slice_in_dim(ada_b, chip * ada_cols, ada_cols, axis=1)
    g1 = _allgather8(_ada_fwd(c_all, ada_w[0], ada_b_mine))
    mod = lax.dynamic_index_in_dim(g1[0::2], dev, axis=1, keepdims=False).reshape(1, 3 * D_MODEL)
    shift, scale, gate = mod[:, :D_MODEL], mod[:, D_MODEL:2 * D_MODEL], mod[:, 2 * D_MODEL:]

    def whole(lands, owns):
        return [lax.dynamic_update_index_in_dim(got, own, chip, 0).reshape(N_CHIPS, -1, own.shape[-1])
                for got, own in zip(_pair_forward(lands), owns)]

    mine = [_bf(a[0]).reshape(2, a.shape[1] // 2, a.shape[2]) for a in (w_in, w_proj_m, w_proj_s, w_out)]
    landing = lambda own: lax.empty((N_CHIPS,) + own.shape, own.dtype)
    w_send, w_recv, w_src, w_land, w_token = _split_start("w_in_gather_start", _gather_copies, _gather_pieces(mine[:1]),
                                                          mine[:1], [landing(mine[0])])
    u = _prenorm_fwd(x[0], norm_w, scale + w_token[0:1, 0:1], shift, ROWS_ELEMENTWISE)
    w_src, w_land = _split_wait("w_in_gather_wait", _gather_copies, w_send, w_recv, w_src, w_land, u)
    behind = (w_src[0][0, 0:1, 0:1] * 0).astype(BF16)
    later = [a + behind for a in mine[1:]]
    p_send, p_recv, p_src, p_land, p_token = _split_start("merge_gather_start", _gather_copies, _gather_pieces(later),
                                                          later, [landing(a) for a in later])
    w_in_p = _shards_to_padded(whole(w_land, w_src)[0])
    b_in_p = _pad_cols(b_in) + p_token[0:1, 0:1]

    def late_weights(after):
        srcs, lands = _split_wait("merge_gather_wait", _gather_copies, p_send, p_recv, p_src, p_land, after)
        return [a.reshape(-1, D_MODEL) for a in whole(lands, srcs)]

    def start_exchange(g_w_in, g_wpm, g_wps, g_wo):
        split = lambda g, rows: _bf(g).reshape(N_CHIPS, 2, rows // (2 * N_CHIPS), g.shape[-1])
        slabs = [split(_padded_to_shards(_bf(g_w_in)), N_CHIPS * D_MODEL),
                 split(g_wpm, g_wpm.shape[0]), split(g_wps, g_wps.shape[0]), split(g_wo, g_wo.shape[0])]
        pair_sums = []
        for slab, rec in zip(slabs, _pair_send(slabs)):
            kept = lax.dynamic_index_in_dim(slab, pos[2], 1, keepdims=False)
            rows = kept.shape[0] * kept.shape[1]
            both = _sum_parts(kept.reshape(rows, -1), rec.reshape(1, rows, -1), ROWS_PAIR_SUM, BF16)
            pair_sums.append(both.reshape(kept.shape))
        send_sems, recv_sems, sums, lands, token = _chip_scatter_start(pair_sums)
        return token, (send_sems, recv_sems, sums, lands)

    grad_x, small, _, _, _, _, in_flight = _local_step(
        x[0], loss_target[0], scale, shift, gate, norm_w, w_in_p, b_in_p, ml_conv_full, ml_conv_b, ml_norm_w,
        ssm_conv_full, ssm_conv_b, ssm_a_log, ssm_d, ssm_norm_w, None, None, None, final_w[None], start_exchange,
        late_weights, u)

    g2 = _allgather8(_pack([small[name] for name, _ in SMALL_SUMS]))
    total = dict(zip([name for name, _ in SMALL_SUMS], _unpack(_sum_parts(None, g2, g2.shape[1]), [s for _, s in SMALL_SUMS])))
    dmod_all = g2[:, :3 * D_MODEL // LANE].reshape(N_DEV, 3 * D_MODEL)
    grads = dict(total)
    grads["ada_b"] = total["mod"]
    grads["ml_conv_w"] = lax.dynamic_slice_in_dim(total["ml_conv_w"], chip * mlw_cols, mlw_cols, axis=1)
    grads["ssm_conv_w"] = lax.dynamic_slice_in_dim(total["ssm_conv_w"], chip * ssw_cols, ssw_cols, axis=1)
    grads["ada_w"] = _ada_bwd(c_all, lax.dynamic_slice_in_dim(dmod_all, chip * ada_cols, ada_cols, axis=1))

    pairs = []
    for both, rec in zip(*_chip_scatter_wait(*in_flight, grad_x)):
        pairs.append(_sum_parts(lax.dynamic_index_in_dim(both, chip, 0, keepdims=False), rec, ROWS_SUM, slot=pos[2]))
    for name, full in zip(("w_in", "w_proj_m", "w_proj_s", "w_out"), _pair_exchange(pairs)):
        grads[name] = full.reshape(-1, full.shape[-1])

    delta, new_m, new_v = {}, {}, {}
    for name in LARGE:
        if w[name].shape[-1] % LANE:
            flat = lambda a: a.reshape(a.shape[-2:]).T
            back = lambda a: a.T.reshape(w[name].shape)
            g_flat = flat(grads[name])
            delta[name], new_m[name], new_v[name] = (back(a) for a in _adamw(flat(w[name]), g_flat, flat(m[name]), flat(v[name]), LANE))
            grads[name] = back(g_flat)
        else:
            delta[name], new_m[name], new_v[name] = _adamw(w[name], grads[name], m[name], v[name], ROWS_ADAMW)
    rest = [name for name in WEIGHTS if name not in LARGE]
    packed = [_pack([t[name] for name in rest]) for t in (w, grads, m, v)]
    for out, buf in zip((delta, new_m, new_v), _adamw(*packed, packed[0].shape[0])):
        out.update(zip(rest, _unpack(buf, [w[name].shape for name in rest])))
    loss = total["loss"][0]
    return (loss, grad_x[None], *[grads[name].reshape(w[name].shape) for name in WEIGHTS], *[delta[name] for name in WEIGHTS],
            *[new_m[name] for name in WEIGHTS], *[new_v[name] for name in WEIGHTS])
```

```python
import math

import jax
import jax.numpy as jnp
from jax import lax
from jax.experimental import pallas as pl
from jax.experimental.pallas import tpu as pltpu

F32 = jnp.float32
BF16 = jnp.bfloat16
MESH = pl.DeviceIdType.MESH

D_MODEL = 1024
EPS = 1e-6
CONV_K = 4
ML_HEADS = 8
ML_DQK = 128
ML_DV = 256
SSM_HEADS = 32
SSM_HEADDIM = 64
SSM_GROUPS = 4
SSM_STATE = 128
IN_WIDTH = 15408
N_CHIPS = 4
N_DEV = 8
ADAM_LR, ADAM_B1, ADAM_B2, ADAM_EPS, ADAM_WD, ADAM_STEP = 0.001, 0.9, 0.999, 1e-08, 0.01, 10

O_O, O_ZM, O_ZS, O_MG, O_QK, O_V, O_XBC, O_IF, O_DT = 0, 2048, 4096, 6144, 8192, 10240, 12288, 15360, 15616
SMALL_W = 256
NP = 15872
LANE = 128
CHUNK = 128
NEG = -1e30
VMEM_LIMIT = 48 * 1024 * 1024
MERGE_VMEM = 60 * 1024 * 1024
ROWS_ELEMENTWISE = 512
ROWS_CONV = 1024
ROWS_MATMUL = 2048
ROWS_WIDE_MATMUL = 4096
COLS_MATMUL = 512
ROWS_POST = 128
ROWS_POST_FWD = 256
ROWS_MERGE = 256
ROWS_SUM = 128
ROWS_ADAMW = 256


def _cparams(*sem, vmem=VMEM_LIMIT):
    return pltpu.CompilerParams(dimension_semantics=sem, vmem_limit_bytes=vmem)


def _pad_cols(w):
    z = lambda n: jnp.zeros(w.shape[:-1] + (n,), w.dtype)
    return jnp.concatenate([w[..., 4096:8192], w[..., 11280:13328], w[..., 13360:15408], w[..., :4096], w[..., 8208:11280],
                            w[..., 8192:8208], z(SMALL_W - 16), w[..., 13328:13360], z(SMALL_W - 32)], axis=-1)


def _unpad_cols(g):
    return jnp.concatenate([g[..., O_QK:O_QK + 4096], g[..., O_O:O_O + 4096], g[..., O_IF:O_IF + 16],
                            g[..., O_XBC:O_XBC + 3072], g[..., O_ZS:O_ZS + 2048], g[..., O_DT:O_DT + 32],
                            g[..., O_MG:O_MG + 2048]], axis=-1)


PADDED_SEGMENTS = ((4096, 8192, 0), (11280, 13328, 0), (13360, 15408, 0), (0, 4096, 0), (8208, 11280, 0),
                   (8192, 8208, SMALL_W - 16), (13328, 13360, SMALL_W - 32))
SHARD_W = IN_WIDTH // N_CHIPS


def _shards_to_padded(shards):
    parts = []
    for first, last, pad in PADDED_SEGMENTS:
        for j in range(N_CHIPS):
            lo, hi = max(first, j * SHARD_W), min(last, (j + 1) * SHARD_W)
            if lo < hi:
                parts.append(shards[j][:, lo - j * SHARD_W:hi - j * SHARD_W])
        if pad:
            parts.append(jnp.zeros((shards.shape[1], pad), shards.dtype))
    return jnp.concatenate(parts, axis=1)


def _padded_to_shards(g):
    offsets, off = {}, 0
    for first, last, pad in PADDED_SEGMENTS:
        offsets[first] = off
        off += last - first + pad
    shards = []
    for j in range(N_CHIPS):
        parts = []
        for first, last, _ in sorted(PADDED_SEGMENTS):
            lo, hi = max(first, j * SHARD_W), min(last, (j + 1) * SHARD_W)
            if lo < hi:
                parts.append(g[:, offsets[first] + lo - first:offsets[first] + hi - first])
        shards.append(jnp.concatenate(parts, axis=1))
    return jnp.stack(shards)


def _sigmoid(x):
    return 0.5 * jnp.tanh(0.5 * x) + 0.5


def _silu(x):
    return x * _sigmoid(x)


def _dsilu(x):
    s = _sigmoid(x)
    return s + x * s * (1.0 - s)


def _softplus(x):
    return jnp.maximum(x, 0.0) + jnp.log(1.0 + jnp.exp(-jnp.abs(x)))


def _logsigmoid(x):
    return jnp.minimum(x, 0.0) - jnp.log(1.0 + jnp.exp(-jnp.abs(x)))


def _dot(a, b, dims):
    return lax.dot_general(a, b, (dims, ((), ())), preferred_element_type=F32)


def _nn(a, b):
    return _dot(a, b, ((1,), (0,)))


def _nt(a, b):
    return _dot(a, b, ((1,), (1,)))


def _tn(a, b):
    return _dot(a, b, ((0,), (0,)))


def _bf(x):
    return x.astype(BF16)


def _split(x, terms):
    parts = []
    for _ in range(terms):
        part = _bf(x)
        parts.append(part)
        x = x - part.astype(F32)
    return parts


def _pick_right(x, pick, terms):
    pick = _bf(pick)
    out = None
    for part in _split(x, terms):
        out = _nn(part, pick) if out is None else out + _nn(part, pick)
    return out


def _pick_left(pick, x, terms):
    pick = _bf(pick)
    out = None
    for part in _split(x, terms):
        out = _nn(pick, part) if out is None else out + _nn(pick, part)
    return out


def _lane_col(x, lane):
    idx = lax.broadcasted_iota(jnp.int32, x.shape, 1)
    return jnp.sum(jnp.where(idx == lane, x, 0.0), axis=1, keepdims=True)


def _tri(n, upper):
    r = lax.broadcasted_iota(jnp.int32, (n, n), 0)
    c = lax.broadcasted_iota(jnp.int32, (n, n), 1)
    return jnp.where((r <= c) if upper else (r >= c), 1.0, 0.0).astype(F32)


def _eye(n):
    return jnp.where(lax.broadcasted_iota(jnp.int32, (n, n), 0) == lax.broadcasted_iota(jnp.int32, (n, n), 1), 1.0, 0.0)


def _sum_all(x):
    return jnp.sum(jnp.sum(x, axis=1, keepdims=True), axis=0, keepdims=True)


def _crossing(p):
    L = p.shape[0]
    below = _nn(_bf(_tri(L, True)), _bf(p))
    strict = lax.broadcasted_iota(jnp.int32, (L, L), 0) > lax.broadcasted_iota(jnp.int32, (L, L), 1)
    return [jnp.sum(jnp.where(strict, below[:, b * L:(b + 1) * L], 0.0), axis=1, keepdims=True)
            for b in range(p.shape[1] // L)]


def _matmul_bias(a, w, bias, tm, tn, col0, ncols, dtype):
    m, k = a.shape
    j0 = col0 // tn

    def body(a_ref, w_ref, b_ref, o_ref):
        o_ref[...] = (_nn(a_ref[...], w_ref[...]) + b_ref[...]).astype(dtype)

    return pl.pallas_call(
        body, name="matmul_bias", grid=(m // tm, ncols // tn),
        in_specs=[pl.BlockSpec((tm, k), lambda i, j: (i, 0)), pl.BlockSpec((k, tn), lambda i, j: (0, j0 + j)),
                  pl.BlockSpec((1, tn), lambda i, j: (0, j0 + j))],
        out_specs=pl.BlockSpec((tm, tn), lambda i, j: (i, j)),
        out_shape=jax.ShapeDtypeStruct((m, ncols), dtype),
        compiler_params=_cparams("parallel", "arbitrary"))(a, w, bias)


def _matmul_nt(a, w, tm, tk, after=None):
    m, n = a.shape
    k = w.shape[0]

    def body(a_ref, w_ref, *rest):
        o_ref = rest[-1]

        @pl.when(pl.program_id(1) == 0)
        def _():
            o_ref[...] = jnp.zeros_like(o_ref)
        o_ref[...] += _nt(a_ref[...], w_ref[...])

    extra = [] if after is None else [after]
    return pl.pallas_call(
        body, name="matmul_nt", grid=(m // tm, n // tk),
        in_specs=[pl.BlockSpec((tm, tk), lambda i, j: (i, j)), pl.BlockSpec((k, tk), lambda i, j: (0, j))]
        + [pl.BlockSpec(memory_space=pl.ANY)] * len(extra),
        out_specs=pl.BlockSpec((tm, k), lambda i, j: (i, 0)),
        out_shape=jax.ShapeDtypeStruct((m, k), F32),
        compiler_params=_cparams("parallel", "arbitrary"))(a, w, *extra)


def _matmul_tn(a, b, tm, tn, with_colsum=False, a_is_transposed=False, dtype=F32):
    k, m = a.shape if a_is_transposed else a.shape[::-1]
    n = b.shape[1]
    steps = m // tm

    def body(a_ref, b_ref, o_ref, *rest):
        acc_ref = o_ref if dtype == F32 else rest[-1]
        first = pl.program_id(1) == 0

        @pl.when(first)
        def _():
            acc_ref[...] = jnp.zeros_like(acc_ref)
        acc_ref[...] += _nn(a_ref[...], b_ref[...]) if a_is_transposed else _tn(a_ref[...], b_ref[...])
        if dtype != F32:
            @pl.when(pl.program_id(1) == steps - 1)
            def _():
                o_ref[...] = acc_ref[...].astype(dtype)
        if with_colsum:
            s_ref = rest[0]

            @pl.when(first)
            def _():
                s_ref[...] = jnp.zeros_like(s_ref)
            s_ref[...] += jnp.sum(b_ref[...].astype(F32), axis=0, keepdims=True)

    out_specs = [pl.BlockSpec((k, tn), lambda j, i: (0, j))]
    out_shape = [jax.ShapeDtypeStruct((k, n), dtype)]
    if with_colsum:
        out_specs.append(pl.BlockSpec((1, tn), lambda j, i: (0, j)))
        out_shape.append(jax.ShapeDtypeStruct((1, n), F32))
    out = pl.pallas_call(
        body, name="matmul_tn", grid=(n // tn, m // tm),
        in_specs=[pl.BlockSpec((k, tm), lambda j, i: (0, i)) if a_is_transposed else pl.BlockSpec((tm, k), lambda j, i: (i, 0)),
                  pl.BlockSpec((tm, tn), lambda j, i: (i, j))],
        out_specs=out_specs, out_shape=out_shape,
        scratch_shapes=[] if dtype == F32 else [pltpu.VMEM((k, tn), F32)],
        compiler_params=_cparams("parallel", "arbitrary"))(a, b)
    return out if with_colsum else out[0]


def _ada_fwd(c_all, ada_w, ada_b):
    def body(c_ref, w_ref, b_ref, o_ref):
        o_ref[...] = _nn(_bf(_silu(c_ref[...])), _bf(w_ref[...])) + b_ref[...]

    return pl.pallas_call(body, name="ada_fwd", out_shape=jax.ShapeDtypeStruct((c_all.shape[0], ada_w.shape[1]), F32),
                          compiler_params=_cparams())(c_all, ada_w, ada_b)


def _ada_bwd(c_all, dmod):
    def body(c_ref, d_ref, o_ref):
        o_ref[...] = _tn(_bf(_silu(c_ref[...])), _bf(d_ref[...]))

    return pl.pallas_call(body, name="ada_bwd", out_shape=jax.ShapeDtypeStruct((c_all.shape[1], dmod.shape[1]), F32),
                          compiler_params=_cparams())(c_all, dmod)


def _prenorm_fwd(x, norm_w, scale, shift, ts):
    s, d = x.shape

    def body(x_ref, nw_ref, sc_ref, sh_ref, u_ref):
        xv = x_ref[...]
        r = lax.rsqrt(jnp.mean(xv * xv, axis=1, keepdims=True) + EPS)
        u_ref[...] = _bf(xv * r * nw_ref[...] * (1.0 + sc_ref[...]) + sh_ref[...])

    row = pl.BlockSpec((1, d), lambda i: (0, 0))
    return pl.pallas_call(
        body, name="prenorm_fwd", grid=(s // ts,),
        in_specs=[pl.BlockSpec((ts, d), lambda i: (i, 0)), row, row, row],
        out_specs=pl.BlockSpec((ts, d), lambda i: (i, 0)), out_shape=jax.ShapeDtypeStruct((s, d), BF16),
        compiler_params=_cparams("parallel"))(x, norm_w, scale, shift)


def _prenorm_bwd(du, x, dxres, norm_w, scale, ts):
    s, d = x.shape

    def body(du_ref, x_ref, dr_ref, nw_ref, sc_ref, gx_ref, acc_ref):
        @pl.when(pl.program_id(0) == 0)
        def _():
            acc_ref[...] = jnp.zeros_like(acc_ref)
        xv, duv = x_ref[...], du_ref[...]
        r = lax.rsqrt(jnp.mean(xv * xv, axis=1, keepdims=True) + EPS)
        xn = xv * r
        nw, sc1 = nw_ref[...], 1.0 + sc_ref[...]
        dxn = duv * (nw * sc1)
        gx_ref[...] = r * (dxn - xn * jnp.mean(dxn * xn, axis=1, keepdims=True)) + dr_ref[...]
        t = duv * xn
        acc_ref[0:1, :] += jnp.sum(t, axis=0, keepdims=True) * sc1
        acc_ref[1:2, :] += jnp.sum(t, axis=0, keepdims=True) * nw
        acc_ref[2:3, :] += jnp.sum(duv, axis=0, keepdims=True)

    tile = pl.BlockSpec((ts, d), lambda i: (i, 0))
    row = pl.BlockSpec((1, d), lambda i: (0, 0))
    return pl.pallas_call(
        body, name="prenorm_bwd", grid=(s // ts,),
        in_specs=[tile, tile, tile, row, row],
        out_specs=[tile, pl.BlockSpec((8, d), lambda i: (0, 0))],
        out_shape=[jax.ShapeDtypeStruct((s, d), F32), jax.ShapeDtypeStruct((8, d), F32)],
        compiler_params=_cparams("arbitrary"))(du, x, dxres, norm_w, scale)


CONV_CB = 512


def _conv_taps(buf_ref, ts):
    return [buf_ref[pl.ds(8 - (CONV_K - 1) + j, ts), :] for j in range(CONV_K)]


def _conv_fwd(proj, col0, width, w8, b, ts):
    s = proj.shape[0]
    cb = CONV_CB
    nt = s // ts

    def body(x_ref, w_ref, b_ref, o_ref, ds_ref, buf_ref):
        @pl.when(pl.program_id(1) == 0)
        def _():
            buf_ref[0:8, :] = jnp.zeros((8, cb), F32)
        buf_ref[pl.ds(8, ts), :] = x_ref[...].astype(F32)
        acc = b_ref[...] + jnp.zeros((ts, cb), F32)
        for j, tap in enumerate(_conv_taps(buf_ref, ts)):
            acc = acc + tap * w_ref[j:j + 1, :]
        sg = _sigmoid(acc)
        o_ref[...] = acc * sg
        ds_ref[...] = _bf(sg + acc * sg * (1.0 - sg))
        buf_ref[0:8, :] = buf_ref[pl.ds(ts, 8), :]

    c0 = col0 // cb
    tile = pl.BlockSpec((ts, cb), lambda c, i: (i, c))
    return pl.pallas_call(
        body, name="conv_fwd", grid=(width // cb, nt),
        in_specs=[pl.BlockSpec((ts, cb), lambda c, i: (i, c0 + c)), pl.BlockSpec((8, cb), lambda c, i: (0, c)),
                  pl.BlockSpec((1, cb), lambda c, i: (0, c))],
        out_specs=[tile, tile],
        out_shape=[jax.ShapeDtypeStruct((s, width), F32), jax.ShapeDtypeStruct((s, width), BF16)],
        scratch_shapes=[pltpu.VMEM((ts + 8, cb), F32)],
        compiler_params=_cparams("parallel", "arbitrary"))(proj, w8, b)


def _conv_bwd(proj, col0, width, w8, dact, dpost, dproj, ts):
    s = proj.shape[0]
    cb = CONV_CB
    nt = s // ts
    c0 = col0 // cb

    def body(x_ref, da_ref, dp_ref, w_ref, _, dx_ref, acc_ref, dbuf_ref):
        @pl.when(pl.program_id(1) == 0)
        def _():
            acc_ref[...] = jnp.zeros_like(acc_ref)
            dbuf_ref[pl.ds(ts, 8), :] = jnp.zeros((8, cb), F32)
        dconv = dp_ref[...].astype(F32) * da_ref[...].astype(F32)
        acc_ref[CONV_K:CONV_K + 1, :] += jnp.sum(dconv, axis=0, keepdims=True)
        dbuf_ref[pl.ds(0, ts), :] = dconv
        xv = x_ref[...].astype(F32)
        dx = jnp.zeros((ts, cb), F32)
        for j in range(CONV_K):
            shifted = dbuf_ref[pl.ds(CONV_K - 1 - j, ts), :]
            dx = dx + shifted * w_ref[j:j + 1, :]
            acc_ref[j:j + 1, :] += jnp.sum(xv * shifted, axis=0, keepdims=True)
        dx_ref[...] = _bf(dx)
        dbuf_ref[pl.ds(ts, 8), :] = dconv[0:8, :]

    tile = pl.BlockSpec((ts, cb), lambda c, i: (nt - 1 - i, c))
    wide = pl.BlockSpec((ts, cb), lambda c, i: (nt - 1 - i, c0 + c))
    return pl.pallas_call(
        body, name="conv_bwd", grid=(width // cb, nt),
        in_specs=[wide, tile, tile, pl.BlockSpec((8, cb), lambda c, i: (0, c)), pl.BlockSpec(memory_space=pl.ANY)],
        out_specs=[wide, pl.BlockSpec((8, cb), lambda c, i: (0, c))],
        out_shape=[jax.ShapeDtypeStruct(dproj.shape, dproj.dtype), jax.ShapeDtypeStruct((8, width), F32)],
        input_output_aliases={4: 0},
        scratch_shapes=[pltpu.VMEM((ts + 8, cb), F32)],
        compiler_params=_cparams("parallel", "arbitrary"))(proj, dact, dpost, w8, dproj)


def _mlstm_gates(gif_ref, gt_ref, a_scr, at_scr):
    L = gif_ref.shape[0]
    fb = _logsigmoid(gif_ref[...])
    a_scr[...] = _pick_left(_tri(L, False), fb, 3)
    at_scr[...] = _pick_right(_logsigmoid(gt_ref[...]), _tri(L, True), 3)
    return jnp.sum(fb, axis=0, keepdims=True)


def _mlstm_head(h, qk_ref, v_ref, gif, gt_ref, a, at_scr, a_last_row, c_mat, n_row, m_prev):
    L = gif.shape[0]
    q = qk_ref[:, h * ML_DQK:(h + 1) * ML_DQK] * (ML_DQK ** -0.5)
    k = qk_ref[:, (ML_HEADS + h) * ML_DQK:(ML_HEADS + h + 1) * ML_DQK]
    v = v_ref[:, h * ML_DV:(h + 1) * ML_DV]
    i_col, a_col = _lane_col(gif, h), _lane_col(a, ML_HEADS + h)
    i_row, a_row = gt_ref[h:h + 1, :], at_scr[ML_HEADS + h:ML_HEADS + h + 1, :]
    causal = lax.broadcasted_iota(jnp.int32, (L, L), 0) >= lax.broadcasted_iota(jnp.int32, (L, L), 1)
    dmat = jnp.where(causal, a_col - a_row + i_row, NEG)
    inter = a_col + m_prev
    m_t = jnp.maximum(inter, jnp.max(dmat, axis=1, keepdims=True))
    w_intra = jnp.exp(dmat - m_t)
    w_inter = jnp.exp(inter - m_t)
    sc = _nt(_bf(q), _bf(k)) * w_intra
    den = jnp.sum(sc, axis=1, keepdims=True) + w_inter * jnp.sum(q * n_row, axis=1, keepdims=True)
    floor = jnp.exp(-m_t)
    a_last = _lane_col(a_last_row, ML_HEADS + h)
    g = a_last - a_col + i_col
    m_new = jnp.maximum(a_last + m_prev, jnp.max(g, axis=0, keepdims=True))
    wk = jnp.exp(g - m_new)
    decay = jnp.exp(a_last + m_prev - m_new)
    return dict(q=q, k=k, v=v, w_intra=w_intra, w_inter=w_inter, sc=sc, den=den, floor=floor, m_new=m_new, wk=wk,
                decay=decay)


def _state_tile(n_row, m11):
    r = lax.broadcasted_iota(jnp.int32, (8, LANE), 0)
    return jnp.where(r == 0, n_row, jnp.where(r == 1, m11, 0.0))


def _mlstm_fwd(qk, proj, gates, gt):
    s = qk.shape[0]
    L = CHUNK
    nc = s // L

    def body(qk_ref, v_ref, gif_ref, gt_ref, h_ref, cst_ref, nm_ref, c_scr, nm_scr, a_scr, at_scr):
        @pl.when(pl.program_id(0) == 0)
        def _():
            c_scr[...] = jnp.zeros_like(c_scr)
            nm_scr[...] = jnp.zeros_like(nm_scr)
        a_last_row = _mlstm_gates(gif_ref, gt_ref, a_scr, at_scr)
        gif, a = gif_ref[...], a_scr[...]
        for h in range(ML_HEADS):
            c_mat, n_row = c_scr[h], nm_scr[h, 0:1, :]
            m_prev = jnp.max(nm_scr[h, 1:2, :], axis=1, keepdims=True)
            cst_ref[0, h] = c_mat
            nm_ref[0, h] = nm_scr[h]
            t = _mlstm_head(h, qk_ref, v_ref, gif, gt_ref, a, at_scr, a_last_row, c_mat, n_row, m_prev)
            num = _nn(_bf(t["sc"]), _bf(t["v"])) + t["w_inter"] * _nn(_bf(t["q"]), _bf(c_mat))
            h_ref[:, h * ML_DV:(h + 1) * ML_DV] = _bf(num * (1.0 / jnp.maximum(jnp.abs(t["den"]), t["floor"])))
            kw = t["k"] * t["wk"]
            c_scr[h] = t["decay"] * c_mat + _tn(_bf(kw), _bf(t["v"]))
            nm_scr[h] = _state_tile(t["decay"] * n_row + jnp.sum(kw, axis=0, keepdims=True), t["m_new"])

    return pl.pallas_call(
        body, name="mlstm_fwd", grid=(nc,),
        in_specs=[pl.BlockSpec((L, 2048), lambda c: (c, 0)), pl.BlockSpec((L, 2048), lambda c: (c, O_V // 2048)),
                  pl.BlockSpec((L, LANE), lambda c: (c, 0)), pl.BlockSpec((LANE, L), lambda c: (0, c))],
        out_specs=[pl.BlockSpec((L, 2048), lambda c: (c, 0)),
                   pl.BlockSpec((1, ML_HEADS, ML_DQK, ML_DV), lambda c: (c, 0, 0, 0)),
                   pl.BlockSpec((1, ML_HEADS, 8, LANE), lambda c: (c, 0, 0, 0))],
        out_shape=[jax.ShapeDtypeStruct((s, 2048), BF16), jax.ShapeDtypeStruct((nc, ML_HEADS, ML_DQK, ML_DV), F32),
                   jax.ShapeDtypeStruct((nc, ML_HEADS, 8, LANE), F32)],
        scratch_shapes=[pltpu.VMEM((ML_HEADS, ML_DQK, ML_DV), F32), pltpu.VMEM((ML_HEADS, 8, LANE), F32),
                        pltpu.VMEM((L, LANE), F32), pltpu.VMEM((LANE, L), F32)],
        compiler_params=_cparams("arbitrary"))(qk, proj, gates, gt)


def _mlstm_bwd(qk, proj, gates, gt, hout, dh, cst, nm, dproj):
    s = qk.shape[0]
    L = CHUNK
    nc = s // L

    def body(qk_ref, v_ref, gif_ref, gt_ref, h_ref, dh_ref, cst_ref, nm_ref, _, dqk_ref, dv_ref, dif_ref,
             dc_scr, dn_scr, a_scr, at_scr):
        @pl.when(pl.program_id(0) == 0)
        def _():
            dc_scr[...] = jnp.zeros_like(dc_scr)
            dn_scr[...] = jnp.zeros_like(dn_scr)
        a_last_row = _mlstm_gates(gif_ref, gt_ref, a_scr, at_scr)
        gif, a = gif_ref[...], a_scr[...]
        lane = lax.broadcasted_iota(jnp.int32, (L, LANE), 1)
        last = lax.broadcasted_iota(jnp.int32, (L, 1), 0) == L - 1
        di_tile = jnp.zeros((L, LANE), F32)
        cross = [jnp.zeros((L, LANE), F32)] * 3
        dlogw = []
        for h in range(ML_HEADS):
            c_mat, n_row = cst_ref[0, h], nm_ref[0, h, 0:1, :]
            m_prev = jnp.max(nm_ref[0, h, 1:2, :], axis=1, keepdims=True)
            t = _mlstm_head(h, qk_ref, v_ref, gif, gt_ref, a, at_scr, a_last_row, c_mat, n_row, m_prev)
            q, k, v, den = t["q"], t["k"], t["v"], t["den"]
            dhh = dh_ref[:, h * ML_DV:(h + 1) * ML_DV].astype(F32)
            hh = h_ref[:, h * ML_DV:(h + 1) * ML_DV].astype(F32)
            dnorm = jnp.maximum(jnp.abs(den), t["floor"])
            inv = 1.0 / dnorm
            dnum = dhh * inv
            d_dn = -jnp.sum(dhh * hh, axis=1, keepdims=True) * inv
            dden = jnp.where(jnp.abs(den) >= t["floor"], jnp.where(den >= 0.0, d_dn, -d_dn), 0.0)
            dsc = _nt(_bf(dnum), _bf(v)) + dden
            ds = dsc * t["w_intra"]
            dq_inter = t["w_inter"] * (_nt(_bf(dnum), _bf(c_mat)) + dden * n_row)
            dq = _nn(_bf(ds), _bf(k)) + dq_inter
            dc, dn_row = dc_scr[h], dn_scr[h, 0:1, :]
            dk_state = t["wk"] * (_nt(_bf(v), _bf(dc)) + dn_row)
            dk = _tn(_bf(ds), _bf(q)) + dk_state
            dv = _tn(_bf(t["sc"]), _bf(dnum)) + t["wk"] * _nn(_bf(k), _bf(dc))
            qi = q * t["w_inter"]
            dc_scr[h] = t["decay"] * dc + _tn(_bf(qi), _bf(dnum))
            dn_scr[h] = jnp.broadcast_to(t["decay"] * dn_row + jnp.sum(qi * dden, axis=0, keepdims=True), (8, LANE))
            dqk_ref[:, h * ML_DQK:(h + 1) * ML_DQK] = _bf(dq * (ML_DQK ** -0.5))
            dqk_ref[:, (ML_HEADS + h) * ML_DQK:(ML_HEADS + h + 1) * ML_DQK] = _bf(dk)
            dv_ref[:, h * ML_DV:(h + 1) * ML_DV] = _bf(dv)
            di_tile = di_tile + jnp.where(lane == h, jnp.sum(k * dk, axis=1, keepdims=True), 0.0)
            carried = t["decay"] * (_sum_all(dc * c_mat) + jnp.sum(dn_row * n_row, axis=1, keepdims=True))
            dlogw.append(dsc * t["sc"])
            parts = (jnp.sum(q * dq_inter, axis=1, keepdims=True) + jnp.where(last, carried, 0.0),
                     jnp.sum(k * dk_state, axis=1, keepdims=True))
            cross[1:] = [c + jnp.where(lane == ML_HEADS + h, p, 0.0) for c, p in zip(cross[1:], parts)]
        for h, col in enumerate(_crossing(jnp.concatenate(dlogw, axis=1))):
            cross[0] = cross[0] + jnp.where(lane == ML_HEADS + h, col, 0.0)
        dfb = cross[0] + _pick_left(_tri(L, True), cross[1], 2) + _pick_left(_tri(L, False) - _eye(L), cross[2], 2)
        dif_ref[:, 0:LANE] = _bf(di_tile + dfb * _sigmoid(-gif))
        dif_ref[:, LANE:SMALL_W] = jnp.zeros((L, SMALL_W - LANE), BF16)

    rev = lambda c: nc - 1 - c
    return pl.pallas_call(
        body, name="mlstm_bwd", grid=(nc,),
        in_specs=[pl.BlockSpec((L, 2048), lambda c: (rev(c), 0)), pl.BlockSpec((L, 2048), lambda c: (rev(c), O_V // 2048)),
                  pl.BlockSpec((L, LANE), lambda c: (rev(c), 0)), pl.BlockSpec((LANE, L), lambda c: (0, rev(c))),
                  pl.BlockSpec((L, 2048), lambda c: (rev(c), 0)), pl.BlockSpec((L, 2048), lambda c: (rev(c), 0)),
                  pl.BlockSpec((1, ML_HEADS, ML_DQK, ML_DV), lambda c: (rev(c), 0, 0, 0)),
                  pl.BlockSpec((1, ML_HEADS, 8, LANE), lambda c: (rev(c), 0, 0, 0)), pl.BlockSpec(memory_space=pl.ANY)],
        out_specs=[pl.BlockSpec((L, 2048), lambda c: (rev(c), 0)), pl.BlockSpec((L, 2048), lambda c: (rev(c), O_V // 2048)),
                   pl.BlockSpec((L, SMALL_W), lambda c: (rev(c), 0))],
        out_shape=[jax.ShapeDtypeStruct((s, 2048), BF16), jax.ShapeDtypeStruct(dproj.shape, dproj.dtype),
                   jax.ShapeDtypeStruct((s, SMALL_W), BF16)],
        input_output_aliases={8: 1},
        scratch_shapes=[pltpu.VMEM((ML_HEADS, ML_DQK, ML_DV), F32), pltpu.VMEM((ML_HEADS, 8, LANE), F32),
                        pltpu.VMEM((L, LANE), F32), pltpu.VMEM((LANE, L), F32)],
        compiler_params=_cparams("arbitrary"))(qk, proj, gates, gt, hout, dh, cst, nm, dproj)


GROUP_W = SSM_HEADS // SSM_GROUPS * SSM_HEADDIM
O_B = SSM_HEADS * SSM_HEADDIM
O_C = O_B + SSM_GROUPS * SSM_STATE


def _head_expand():
    r = jnp.arange(LANE)[:, None]
    c = jnp.arange(SSM_HEADS * SSM_HEADDIM)[None, :] // SSM_HEADDIM
    return (r == c).astype(F32)


def _ssd_gates(dt_ref, dtt_ref, alog_row_ref, alog_col_ref, at_scr):
    L = dt_ref.shape[0]
    dt = _softplus(dt_ref[...])
    acoef = -jnp.exp(alog_row_ref[...])
    a = _pick_left(_tri(L, False), dt * acoef, 3)
    at_scr[...] = _pick_right(_softplus(dtt_ref[...]) * (-jnp.exp(alog_col_ref[...])), _tri(L, True), 3)
    return dt, acoef, a


def _ssd_group(g, xbc_ref, dt, a, e_ref, ax_scr):
    eg = e_ref[:, g * GROUP_W:(g + 1) * GROUP_W]
    ax_scr[...] = _pick_right(a, eg, 3)
    ax = ax_scr[...]
    alx = ax_scr[ax.shape[0] - 1:ax.shape[0], :]
    dtx = _pick_right(dt, eg, 2)
    xg = xbc_ref[:, g * GROUP_W:(g + 1) * GROUP_W]
    bg = xbc_ref[:, O_B + g * SSM_STATE:O_B + (g + 1) * SSM_STATE]
    cg = xbc_ref[:, O_C + g * SSM_STATE:O_C + (g + 1) * SSM_STATE]
    return dict(ax=ax, alx=alx, dtx=dtx, xg=xg, bg=bg, cg=cg, xdt=xg * dtx, gmat=_nt(_bf(cg), _bf(bg)))


def _ssd_decay(hh, a, at_scr):
    L = a.shape[0]
    causal = lax.broadcasted_iota(jnp.int32, (L, L), 0) >= lax.broadcasted_iota(jnp.int32, (L, L), 1)
    return jnp.exp(jnp.where(causal, _lane_col(a, hh) - at_scr[hh:hh + 1, :], NEG))


def _ssd_fwd(xbc, gates, dtt, alog_row, alog_col, dskip_x, expand):
    s = xbc.shape[0]
    L = CHUNK
    nc = s // L
    half = SSM_HEADDIM

    def body(xbc_ref, dt_ref, dtt_ref, ar_ref, ac_ref, dk_ref, e_ref, y_ref, st_ref, st_scr, at_scr, ax_scr):
        @pl.when(pl.program_id(0) == 0)
        def _():
            st_scr[...] = jnp.zeros_like(st_scr)
        dt, _, a = _ssd_gates(dt_ref, dtt_ref, ar_ref, ac_ref, at_scr)
        lane = lax.broadcasted_iota(jnp.int32, (L, LANE), 1)
        for g in range(SSM_GROUPS):
            t = _ssd_group(g, xbc_ref, dt, a, e_ref, ax_scr)
            st = st_scr[g]
            st_ref[0, g] = st
            pairs = []
            for j in range(GROUP_W // LANE):
                xp = _bf(t["xdt"][:, j * LANE:(j + 1) * LANE])
                hh = g * (SSM_HEADS // SSM_GROUPS) + 2 * j
                both = jnp.concatenate([_bf(t["gmat"] * _ssd_decay(hh, a, at_scr)),
                                        _bf(t["gmat"] * _ssd_decay(hh + 1, a, at_scr))], axis=0)
                ys = _nn(both, xp)
                pairs.append(jnp.where(lane < half, ys[0:L], ys[L:2 * L]))
            y = jnp.concatenate(pairs, axis=1) + _nn(_bf(t["cg"]), _bf(st)) * jnp.exp(t["ax"])
            y_ref[:, g * GROUP_W:(g + 1) * GROUP_W] = _bf(y + dk_ref[:, g * GROUP_W:(g + 1) * GROUP_W] * t["xg"])
            wts = jnp.exp(t["alx"] - t["ax"])
            st_scr[g] = jnp.exp(t["alx"]) * st + _tn(_bf(t["bg"]), _bf(t["xdt"] * wts))

    row = lambda w: pl.BlockSpec((1, w), lambda c: (0, 0))
    return pl.pallas_call(
        body, name="ssd_fwd", grid=(nc,),
        in_specs=[pl.BlockSpec((L, 3072), lambda c: (c, 0)), pl.BlockSpec((L, LANE), lambda c: (c, (O_DT - O_IF) // LANE)),
                  pl.BlockSpec((LANE, L), lambda c: (0, c)), row(LANE), pl.BlockSpec((LANE, 1), lambda c: (0, 0)),
                  row(2048), pl.BlockSpec((LANE, 2048), lambda c: (0, 0))],
        out_specs=[pl.BlockSpec((L, 2048), lambda c: (c, 0)),
                   pl.BlockSpec((1, SSM_GROUPS, SSM_STATE, GROUP_W), lambda c: (c, 0, 0, 0))],
        out_shape=[jax.ShapeDtypeStruct((s, 2048), BF16),
                   jax.ShapeDtypeStruct((nc, SSM_GROUPS, SSM_STATE, GROUP_W), F32)],
        scratch_shapes=[pltpu.VMEM((SSM_GROUPS, SSM_STATE, GROUP_W), F32), pltpu.VMEM((LANE, L), F32),
                        pltpu.VMEM((L, GROUP_W), F32)],
        compiler_params=_cparams("arbitrary"))(xbc, gates, dtt, alog_row, alog_col, dskip_x, expand)


def _ssd_bwd(xbc, gates, dtt, alog_row, alog_col, dskip_x, expand, expand_t, dy, states):
    s = xbc.shape[0]
    L = CHUNK
    nc = s // L
    half = SSM_HEADDIM

    def body(xbc_ref, dt_ref, dtt_ref, ar_ref, ac_ref, dk_ref, e_ref, et_ref, dy_ref, st_ref,
             dxbc_ref, ddt_ref, accd_ref, acca_ref, dst_scr, at_scr, ax_scr):
        @pl.when(pl.program_id(0) == 0)
        def _():
            dst_scr[...] = jnp.zeros_like(dst_scr)
            accd_ref[...] = jnp.zeros_like(accd_ref)
            acca_ref[...] = jnp.zeros_like(acca_ref)
        dt, acoef, a = _ssd_gates(dt_ref, dtt_ref, ar_ref, ac_ref, at_scr)
        lane = lax.broadcasted_iota(jnp.int32, (L, LANE), 1)
        low = lane < half
        last = lax.broadcasted_iota(jnp.int32, (L, 1), 0) == L - 1
        cross = [jnp.zeros((L, LANE), F32)] * 3
        ddt_tile = jnp.zeros((L, LANE), F32)
        for g in range(SSM_GROUPS):
            t = _ssd_group(g, xbc_ref, dt, a, e_ref, ax_scr)
            xg, bg, cg, xdt, gmat = t["xg"], t["bg"], t["cg"], t["xdt"], t["gmat"]
            st, dst = st_ref[0, g], dst_scr[g]
            dyg = dy_ref[:, g * GROUP_W:(g + 1) * GROUP_W].astype(F32)
            ea, eal = jnp.exp(t["ax"]), jnp.exp(t["alx"])
            wts = jnp.exp(t["alx"] - t["ax"])
            dyi = dyg * ea
            y_inter = _nn(_bf(cg), _bf(st)) * ea
            dc = _nt(_bf(dyi), _bf(st))
            d_xdt_state = _nn(_bf(bg), _bf(dst)) * wts
            db = _nt(_bf(xdt * wts), _bf(dst))
            dst_scr[g] = eal * dst + _tn(_bf(cg), _bf(dyi))
            dg = jnp.zeros((L, L), F32)
            dx_pairs, dlogw = [], []
            for j in range(GROUP_W // LANE):
                xp = _bf(xdt[:, j * LANE:(j + 1) * LANE])
                dyp = dyg[:, j * LANE:(j + 1) * LANE]
                hh = g * (SSM_HEADS // SSM_GROUPS) + 2 * j
                decs = [_ssd_decay(hh, a, at_scr), _ssd_decay(hh + 1, a, at_scr)]
                ws = [gmat * decs[0], gmat * decs[1]]
                dxs = _tn(_bf(jnp.concatenate(ws, axis=1)), _bf(dyp))
                dws = _nt(_bf(jnp.concatenate([jnp.where(low, dyp, 0.0), jnp.where(low, 0.0, dyp)], axis=0)), xp)
                dw0, dw1 = dws[0:L], dws[L:2 * L]
                dg = dg + dw0 * decs[0] + dw1 * decs[1]
                dlogw += [dw0 * ws[0], dw1 * ws[1]]
                dx_pairs.append(jnp.where(low, dxs[0:L], dxs[L:2 * L]))
            for b, col in enumerate(_crossing(jnp.concatenate(dlogw, axis=1))):
                cross[0] = cross[0] + jnp.where(lane == g * (SSM_HEADS // SSM_GROUPS) + b, col, 0.0)
            d_xdt = d_xdt_state + jnp.concatenate(dx_pairs, axis=1)
            dc = dc + _nn(_bf(dg), _bf(bg))
            db = db + _tn(_bf(dg), _bf(cg))
            etg = et_ref[g * GROUP_W:(g + 1) * GROUP_W, :]
            carried = jnp.sum(dst * st, axis=0, keepdims=True) * eal
            cross[1] = cross[1] + _pick_right(dyg * y_inter + jnp.where(last, carried, 0.0), etg, 2)
            cross[2] = cross[2] + _pick_right(xdt * d_xdt_state, etg, 2)
            ddt_tile = ddt_tile + _pick_right(d_xdt * xg, etg, 2)
            dxbc_ref[:, g * GROUP_W:(g + 1) * GROUP_W] = _bf(d_xdt * t["dtx"] + dk_ref[:, g * GROUP_W:(g + 1) * GROUP_W] * dyg)
            dxbc_ref[:, O_B + g * SSM_STATE:O_B + (g + 1) * SSM_STATE] = _bf(db)
            dxbc_ref[:, O_C + g * SSM_STATE:O_C + (g + 1) * SSM_STATE] = _bf(dc)
            accd_ref[0:1, g * GROUP_W:(g + 1) * GROUP_W] += jnp.sum(dyg * xg, axis=0, keepdims=True)
        d_da = cross[0] + _pick_left(_tri(L, True), cross[1], 2) + _pick_left(_tri(L, False) - _eye(L), cross[2], 2)
        acca_ref[0:1, :] += jnp.sum(d_da * dt, axis=0, keepdims=True)
        ddt_ref[:, 0:LANE] = _bf((ddt_tile + d_da * acoef) * _sigmoid(dt_ref[...]))
        ddt_ref[:, LANE:SMALL_W] = jnp.zeros((L, SMALL_W - LANE), BF16)

    rev = lambda c: nc - 1 - c
    row = lambda w: pl.BlockSpec((1, w), lambda c: (0, 0))
    return pl.pallas_call(
        body, name="ssd_bwd", grid=(nc,),
        in_specs=[pl.BlockSpec((L, 3072), lambda c: (rev(c), 0)), pl.BlockSpec((L, LANE), lambda c: (rev(c), (O_DT - O_IF) // LANE)),
                  pl.BlockSpec((LANE, L), lambda c: (0, rev(c))), row(LANE), pl.BlockSpec((LANE, 1), lambda c: (0, 0)),
                  row(2048), pl.BlockSpec((LANE, 2048), lambda c: (0, 0)), pl.BlockSpec((2048, LANE), lambda c: (0, 0)),
                  pl.BlockSpec((L, 2048), lambda c: (rev(c), 0)),
                  pl.BlockSpec((1, SSM_GROUPS, SSM_STATE, GROUP_W), lambda c: (rev(c), 0, 0, 0))],
        out_specs=[pl.BlockSpec((L, 3072), lambda c: (rev(c), 0)), pl.BlockSpec((L, SMALL_W), lambda c: (rev(c), 0)),
                   pl.BlockSpec((8, 2048), lambda c: (0, 0)), pl.BlockSpec((8, LANE), lambda c: (0, 0))],
        out_shape=[jax.ShapeDtypeStruct((s, 3072), BF16), jax.ShapeDtypeStruct((s, SMALL_W), BF16),
                   jax.ShapeDtypeStruct((8, 2048), F32), jax.ShapeDtypeStruct((8, LANE), F32)],
        scratch_shapes=[pltpu.VMEM((SSM_GROUPS, SSM_STATE, GROUP_W), F32),
                        pltpu.VMEM((LANE, L), F32), pltpu.VMEM((L, GROUP_W), F32)],
        compiler_params=_cparams("arbitrary"))(xbc, gates, dtt, alog_row, alog_col, dskip_x, expand, expand_t, dy, states)


def _group_norm(v, width):
    outs, rs = [], []
    for k in range(v.shape[1] // width):
        blk = v[:, k * width:(k + 1) * width]
        r = lax.rsqrt(jnp.mean(blk * blk, axis=1, keepdims=True) + EPS)
        outs.append(blk * r)
        rs.append(jnp.broadcast_to(r, blk.shape))
    return jnp.concatenate(outs, axis=1), jnp.concatenate(rs, axis=1)


def _group_mean(v, width):
    return jnp.concatenate([jnp.broadcast_to(jnp.mean(v[:, k * width:(k + 1) * width], axis=1, keepdims=True),
                                             (v.shape[0], width)) for k in range(v.shape[1] // width)], axis=1)


def _post_fwd(hm, yssd, proj, ml_norm_w, ssm_norm_w, ts):
    s = hm.shape[0]

    def body(h_ref, ys_ref, o_ref, zm_ref, zs_ref, wm_ref, ws_ref, ym_ref, yso_ref):
        hn, _ = _group_norm(h_ref[...].astype(F32), ML_DV)
        ym_ref[...] = _bf(_sigmoid(o_ref[...].astype(F32)) * hn * wm_ref[...] * _silu(zm_ref[...].astype(F32)))
        pn, _ = _group_norm(ys_ref[...].astype(F32) * _silu(zs_ref[...].astype(F32)), GROUP_W)
        yso_ref[...] = _bf(pn * ws_ref[...])

    tile = pl.BlockSpec((ts, 2048), lambda i: (i, 0))
    col = lambda off: pl.BlockSpec((ts, 2048), lambda i: (i, off // 2048))
    row = pl.BlockSpec((1, 2048), lambda i: (0, 0))
    return pl.pallas_call(
        body, name="post_fwd", grid=(s // ts,),
        in_specs=[tile, tile, col(O_O), col(O_ZM), col(O_ZS), row, row],
        out_specs=[tile, tile],
        out_shape=[jax.ShapeDtypeStruct((s, 2048), BF16)] * 2,
        compiler_params=_cparams("parallel"))(hm, yssd, proj, proj, proj, ml_norm_w, ssm_norm_w)


def _post_bwd(dym, dys, hm, yssd, proj, ml_norm_w, ssm_norm_w, dproj, ts):
    s = hm.shape[0]

    def body(dym_ref, dys_ref, h_ref, ys_ref, o_ref, zm_ref, zs_ref, wm_ref, ws_ref, _,
             dh_ref, dyssd_ref, dp_ref, acc_ref):
        @pl.when(pl.program_id(0) == 0)
        def _():
            acc_ref[...] = jnp.zeros_like(acc_ref)
        hn, r = _group_norm(h_ref[...].astype(F32), ML_DV)
        so, zm, wm, d_ym = _sigmoid(o_ref[...].astype(F32)), zm_ref[...].astype(F32), wm_ref[...], dym_ref[...].astype(F32)
        sz = _silu(zm)
        hnw = hn * wm
        dp_ref[:, O_O:O_O + 2048] = _bf(d_ym * hnw * sz * so * (1.0 - so))
        dp_ref[:, O_ZM:O_ZM + 2048] = _bf(d_ym * so * hnw * _dsilu(zm))
        dhnw = d_ym * so * sz
        acc_ref[0:1, :] += jnp.sum(dhnw * hn, axis=0, keepdims=True)
        dhn = dhnw * wm
        dh_ref[...] = _bf(r * (dhn - hn * _group_mean(dhn * hn, ML_DV)))
        ysv, zs, d_ys = ys_ref[...].astype(F32), zs_ref[...].astype(F32), dys_ref[...].astype(F32)
        szs = _silu(zs)
        pn, r2 = _group_norm(ysv * szs, GROUP_W)
        acc_ref[1:2, :] += jnp.sum(d_ys * pn, axis=0, keepdims=True)
        dpn = d_ys * ws_ref[...]
        dp = r2 * (dpn - pn * _group_mean(dpn * pn, GROUP_W))
        dyssd_ref[...] = _bf(dp * szs)
        dp_ref[:, O_ZS:O_ZS + 2048] = _bf(dp * ysv * _dsilu(zs))

    tile = pl.BlockSpec((ts, 2048), lambda i: (i, 0))
    col = lambda off: pl.BlockSpec((ts, 2048), lambda i: (i, off // 2048))
    row = pl.BlockSpec((1, 2048), lambda i: (0, 0))
    sds = lambda dt: jax.ShapeDtypeStruct((s, 2048), dt)
    return pl.pallas_call(
        body, name="post_bwd", grid=(s // ts,),
        in_specs=[tile, tile, tile, tile, col(O_O), col(O_ZM), col(O_ZS), row, row, pl.BlockSpec(memory_space=pl.ANY)],
        out_specs=[tile, tile, pl.BlockSpec((ts, O_MG), lambda i: (i, 0)), pl.BlockSpec((8, 2048), lambda i: (0, 0))],
        out_shape=[sds(BF16), sds(BF16), jax.ShapeDtypeStruct(dproj.shape, dproj.dtype), jax.ShapeDtypeStruct((8, 2048), F32)],
        input_output_aliases={9: 2},
        compiler_params=_cparams("arbitrary"))(dym, dys, hm, yssd, proj, proj, proj, ml_norm_w, ssm_norm_w, dproj)


def _merge(x, ym, ys, proj, target, gate, final_w, wpm, wps, wo, ts):
    wpm_t, wps_t, wo_t = wpm.T, wps.T, wo.T
    s, d = x.shape

    def body(x_ref, ym_ref, ys_ref, mg_ref, t_ref, gate_ref, fw_ref, wpm_ref, wps_ref, wo_ref, wpmt_ref, wpst_ref, wot_ref,
             dres_ref, mer_ref, dmo_ref, dpm_ref, dps_ref, dym_ref, dys_ref, dmg_ref, acc_ref):
        @pl.when(pl.program_id(0) == 0)
        def _():
            acc_ref[...] = jnp.zeros_like(acc_ref)
        gm, gs = _sigmoid(mg_ref[:, 0:d].astype(F32)), _sigmoid(mg_ref[:, d:2 * d].astype(F32))
        pm = _nn(ym_ref[...], wpm_ref[...])
        ps = _nn(ys_ref[...], wps_ref[...])
        merged = _bf(gm * pm + gs * ps)
        mer_ref[...] = merged
        mo = _nn(merged, wo_ref[...])
        gate, fw = gate_ref[...], fw_ref[...]
        out = x_ref[...] + gate * mo
        r = lax.rsqrt(jnp.mean(out * out, axis=1, keepdims=True) + EPS)
        on = out * r
        diff = on * fw - t_ref[...]
        acc_ref[0:1, :] += jnp.sum(0.5 * jnp.sum(diff * diff, axis=1, keepdims=True) / d, axis=0, keepdims=True)
        dyv = diff * (1.0 / d)
        acc_ref[1:2, :] += jnp.sum(dyv * on, axis=0, keepdims=True)
        don = dyv * fw
        dout = r * (don - on * jnp.mean(don * on, axis=1, keepdims=True))
        dres_ref[...] = dout
        acc_ref[2:3, :] += jnp.sum(dout * mo, axis=0, keepdims=True)
        dmo = _bf(dout * gate)
        dmo_ref[...] = dmo
        dmer = _nn(dmo, wot_ref[...])
        dpm, dps = _bf(dmer * gm), _bf(dmer * gs)
        dpm_ref[...] = dpm
        dps_ref[...] = dps
        dmg_ref[:, 0:d] = _bf(dmer * pm * gm * (1.0 - gm))
        dmg_ref[:, d:2 * d] = _bf(dmer * ps * gs * (1.0 - gs))
        dym_ref[...] = _bf(_nn(dpm, wpmt_ref[...]))
        dys_ref[...] = _bf(_nn(dps, wpst_ref[...]))

    t1 = pl.BlockSpec((ts, d), lambda i: (i, 0))
    t2 = pl.BlockSpec((ts, 2 * d), lambda i: (i, 0))
    row = pl.BlockSpec((1, d), lambda i: (0, 0))
    whole = pl.BlockSpec(memory_space=pltpu.VMEM)
    sd = lambda w, dt: jax.ShapeDtypeStruct((s, w), dt)
    return pl.pallas_call(
        body, name="merge_fwd_bwd", grid=(s // ts,),
        in_specs=[t1, t2, t2, pl.BlockSpec((ts, 2 * d), lambda i: (i, O_MG // (2 * d))), t1, row, row] + [whole] * 6,
        out_specs=[t1, t1, t1, t1, t1, t2, t2, pl.BlockSpec((ts, 2 * d), lambda i: (i, O_MG // (2 * d))),
                   pl.BlockSpec((8, d), lambda i: (0, 0))],
        out_shape=[sd(d, F32), sd(d, BF16), sd(d, BF16), sd(d, BF16), sd(d, BF16), sd(2 * d, BF16), sd(2 * d, BF16),
                   sd(NP, BF16), jax.ShapeDtypeStruct((8, d), F32)],
        compiler_params=_cparams("arbitrary", vmem=MERGE_VMEM))(x, ym, ys, proj, target, gate, final_w, wpm, wps, wo, wpm_t, wps_t, wo_t)


def _adamw(w, g, m, v, tr):
    if w.ndim == 2 and w.shape[0] % 8:
        tile, steps = pl.BlockSpec((w.shape[0], tr), lambda i: (0, i)), w.shape[1] // tr
    else:
        lead = (None,) * (w.ndim - 2)
        tile, steps = pl.BlockSpec(lead + (tr, w.shape[-1]), lambda i: (0,) * len(lead) + (i, 0)), w.shape[-2] // tr

    def body(w_ref, g_ref, m_ref, v_ref, d_ref, nm_ref, nv_ref):
        gv = g_ref[...]
        m2 = ADAM_B1 * m_ref[...] + (1.0 - ADAM_B1) * gv
        v2 = ADAM_B2 * v_ref[...] + (1.0 - ADAM_B2) * (gv * gv)
        m_hat = m2 / (1.0 - ADAM_B1 ** ADAM_STEP)
        v_hat = v2 / (1.0 - ADAM_B2 ** ADAM_STEP)
        d_ref[...] = -ADAM_LR * (m_hat / (jnp.sqrt(v_hat) + ADAM_EPS) + ADAM_WD * w_ref[...])
        nm_ref[...] = m2
        nv_ref[...] = v2

    return pl.pallas_call(
        body, name="adamw", grid=(steps,), in_specs=[tile] * 4, out_specs=[tile] * 3,
        out_shape=[jax.ShapeDtypeStruct(w.shape, F32)] * 3,
        compiler_params=_cparams("parallel"))(w, g.reshape(w.shape), m, v)


def _sum_parts(own, parts, tr, dtype=F32, slot=None):
    p, rows, cols = parts.shape

    def body(*refs):
        p_ref, o_ref = refs[-2], refs[-1]
        first = None if own is None else refs[-3]
        acc = p_ref[0].astype(F32) if first is None else first[...].astype(F32) + p_ref[0].astype(F32)
        for i in range(1, p):
            acc = acc + p_ref[i].astype(F32)
        o_ref[...] = acc.astype(dtype)

    args = ([] if own is None else [own]) + [parts]
    if slot is None:
        tile = pl.BlockSpec((tr, cols), lambda i: (i, 0))
        ins = ([] if own is None else [tile]) + [pl.BlockSpec((p, tr, cols), lambda i: (0, i, 0))]
        return pl.pallas_call(
            body, name="sum_parts", grid=(rows // tr,), in_specs=ins, out_specs=tile,
            out_shape=jax.ShapeDtypeStruct((rows, cols), dtype), compiler_params=_cparams("parallel"))(*args)
    tile = pl.BlockSpec((tr, cols), lambda i, s: (i, 0))
    ins = ([] if own is None else [tile]) + [pl.BlockSpec((p, tr, cols), lambda i, s: (0, i, 0))]
    return pl.pallas_call(
        body, name="sum_parts_half", out_shape=jax.ShapeDtypeStruct((2, rows, cols), dtype),
        grid_spec=pltpu.PrefetchScalarGridSpec(
            num_scalar_prefetch=1, grid=(rows // tr,), in_specs=ins,
            out_specs=pl.BlockSpec((None, tr, cols), lambda i, s: (s[0], i, 0))),
        compiler_params=_cparams("parallel"))(jnp.reshape(slot, (1,)).astype(jnp.int32), *args)


def _position():
    return lax.axis_index("x"), lax.axis_index("y"), lax.axis_index("c")


def _flip(pos, k):
    return tuple(1 - p if (k >> s) & 1 else p for p, s in zip(pos, (2, 1, 0)))


def _allgather8(block):
    rows, cols = block.shape

    def body(x_ref, o_ref, send_sems, recv_sems, local_sem):
        pos = _position()
        me = 4 * pos[0] + 2 * pos[1] + pos[2]
        mine = pltpu.make_async_copy(x_ref, o_ref.at[me], local_sem)
        mine.start()
        copies = [pltpu.make_async_remote_copy(src_ref=x_ref, dst_ref=o_ref.at[me], send_sem=send_sems.at[k - 1],
                                               recv_sem=recv_sems.at[k - 1], device_id=_flip(pos, k), device_id_type=MESH)
                  for k in range(1, N_DEV)]
        for cp in copies:
            cp.start()
        for cp in copies:
            cp.wait()
        mine.wait()

    vmem = pl.BlockSpec(memory_space=pltpu.VMEM)
    return pl.pallas_call(
        body, name="allgather8", in_specs=[vmem], out_specs=vmem,
        out_shape=jax.ShapeDtypeStruct((N_DEV, rows, cols), block.dtype),
        scratch_shapes=[pltpu.SemaphoreType.DMA((N_DEV - 1,)), pltpu.SemaphoreType.DMA((N_DEV - 1,)),
                        pltpu.SemaphoreType.DMA],
        compiler_params=pltpu.CompilerParams(vmem_limit_bytes=VMEM_LIMIT))(block)


COPY_BYTES = 1 << 20


def _row_chunks(rows, row_bytes):
    n = max(1, min(rows // 16, -(-rows * row_bytes // COPY_BYTES)))
    while rows % (16 * n):
        n -= 1
    return [(i * (rows // n), rows // n) for i in range(n)]


def _split_start(name, make_copies, n_copies, sources, lands):
    n, m = len(sources), len(lands)

    def body(*refs):
        for cp in make_copies(_position(), refs[:n], refs[n:n + m], refs[n + m], refs[n + m + 1]):
            cp.start()
        refs[-1][...] = jnp.zeros((8, LANE), F32)

    hbm = pl.BlockSpec(memory_space=pltpu.HBM)
    sem = pl.BlockSpec(memory_space=pltpu.SEMAPHORE)
    operands = [pltpu.with_memory_space_constraint(t, pltpu.HBM) for t in list(sources) + list(lands)]
    out = pl.pallas_call(
        body, name=name, in_specs=[hbm] * (n + m),
        out_specs=[sem, sem] + [hbm] * (n + m) + [pl.BlockSpec(memory_space=pltpu.VMEM)],
        out_shape=[pltpu.SemaphoreType.DMA((n_copies,)), pltpu.SemaphoreType.DMA((n_copies,))]
        + [pltpu.HBM(t.shape, t.dtype) for t in operands] + [jax.ShapeDtypeStruct((8, LANE), F32)],
        input_output_aliases={i: 2 + i for i in range(n + m)},
        compiler_params=pltpu.CompilerParams(has_side_effects=pltpu.SideEffectType.DATAFLOW_SIDE_EFFECTING))(*operands)
    return out[0], out[1], out[2:2 + n], out[2 + n:2 + n + m], out[-1]


def _split_wait(name, make_copies, send_sems, recv_sems, sources, lands, after):
    n, m = len(sources), len(lands)

    def body(*refs):
        for cp in make_copies(_position(), refs[:n], refs[n:n + m], refs[n + m], refs[n + m + 1]):
            cp.wait_send()
            cp.wait_recv()

    hbm = pl.BlockSpec(memory_space=pltpu.HBM)
    sem = pl.BlockSpec(memory_space=pltpu.SEMAPHORE)
    out = pl.pallas_call(
        body, name=name, in_specs=[hbm] * (n + m) + [sem, sem, pl.BlockSpec(memory_space=pl.ANY)],
        out_specs=[hbm] * (n + m), out_shape=[pltpu.HBM(t.shape, t.dtype) for t in list(sources) + list(lands)],
        input_output_aliases={i: i for i in range(n + m)},
        compiler_params=pltpu.CompilerParams(has_side_effects=pltpu.SideEffectType.DATAFLOW_SIDE_EFFECTING))(
            *sources, *lands, send_sems, recv_sems, after)
    return out[:n], out[n:]


def _gather_copies(pos, halves, lands, send_sems, recv_sems):
    chip, core = 2 * pos[0] + pos[1], pos[2]
    copies = []
    for a in range(len(halves)):
        for k in range(1, N_CHIPS):
            for r0, nr in _row_chunks(halves[a].shape[1], halves[a].shape[2] * halves[a].dtype.itemsize):
                i = len(copies)
                copies.append(pltpu.make_async_remote_copy(
                    src_ref=halves[a].at[core, pl.ds(r0, nr)], dst_ref=lands[a].at[chip, core, pl.ds(r0, nr)],
                    send_sem=send_sems.at[i], recv_sem=recv_sems.at[i], device_id=_flip(pos, 2 * k), device_id_type=MESH))
    return copies


def _gather_pieces(halves):
    return (N_CHIPS - 1) * sum(len(_row_chunks(a.shape[1], a.shape[2] * a.dtype.itemsize)) for a in halves)


def _pair_forward(lands):
    n = len(lands)

    def plan(pos, ins, outs):
        remote = []
        for a in range(n):
            for k in range(1, N_CHIPS):
                there = _flip(pos, 2 * k)
                for r0, nr in _row_chunks(lands[a].shape[2], lands[a].shape[3] * lands[a].dtype.itemsize):
                    slot = (2 * there[0] + there[1], pos[2], pl.ds(r0, nr))
                    remote.append((ins[a].at[slot], outs[a].at[slot], _flip(pos, 1)))
        return remote, []

    return _exchange("pair_forward", lands, [jax.ShapeDtypeStruct(t.shape, t.dtype) for t in lands], plan,
                     _gather_pieces([jax.ShapeDtypeStruct(t.shape[1:], t.dtype) for t in lands]), 0, in_place=True)


def _exchange(name, arrays, out_shapes, plan, n_remote, n_local, in_place=False):
    n, m = len(arrays), len(out_shapes)

    def body(*refs):
        send_sems, recv_sems, local_sems = refs[n + m:]
        remote, local = plan(_position(), refs[:n], refs[n:n + m])
        assert (len(remote), len(local)) == (n_remote, n_local)
        copies = [pltpu.make_async_copy(src, dst, local_sems.at[i]) for i, (src, dst) in enumerate(local)]
        copies += [pltpu.make_async_remote_copy(src_ref=src, dst_ref=dst, send_sem=send_sems.at[i], recv_sem=recv_sems.at[i],
                                                device_id=dev, device_id_type=MESH)
                   for i, (src, dst, dev) in enumerate(remote)]
        for cp in copies:
            cp.start()
        for cp in copies:
            cp.wait()

    hbm = pl.BlockSpec(memory_space=pl.ANY)
    return pl.pallas_call(
        body, name=name, in_specs=[hbm] * n, out_specs=[hbm] * m, out_shape=out_shapes,
        input_output_aliases={i: i for i in range(n)} if in_place else {},
        scratch_shapes=[pltpu.SemaphoreType.DMA((n_remote,)), pltpu.SemaphoreType.DMA((n_remote,)),
                        pltpu.SemaphoreType.DMA((max(n_local, 1),))],
        compiler_params=pltpu.CompilerParams(has_side_effects=True))(*arrays)


def _pair_send(slabs):
    n = len(slabs)
    pieces = [_row_chunks(g.shape[2], g.shape[3] * g.dtype.itemsize) for g in slabs]

    def plan(pos, ins, outs):
        return [(ins[a].at[j, 1 - pos[2], pl.ds(r0, nr)], outs[a].at[j, pl.ds(r0, nr)], _flip(pos, 1))
                for a in range(n) for j in range(N_CHIPS) for r0, nr in pieces[a]], []

    return _exchange("pair_send", slabs, [jax.ShapeDtypeStruct((N_CHIPS,) + g.shape[2:], g.dtype) for g in slabs], plan,
                     N_CHIPS * sum(len(p) for p in pieces), 0)


def _chip_scatter_copies(pos, sums, lands, send_sems, recv_sems):
    copies = []
    for a in range(len(sums)):
        for k in range(1, N_CHIPS):
            to = _flip(pos, 2 * k)
            for r0, nr in _row_chunks(sums[a].shape[1], sums[a].shape[2] * sums[a].dtype.itemsize):
                i = len(copies)
                copies.append(pltpu.make_async_remote_copy(
                    src_ref=sums[a].at[2 * to[0] + to[1], pl.ds(r0, nr)], dst_ref=lands[a].at[k - 1, pl.ds(r0, nr)],
                    send_sem=send_sems.at[i], recv_sem=recv_sems.at[i], device_id=to, device_id_type=MESH))
    return copies


def _chip_scatter_start(sums):
    n_copies = (N_CHIPS - 1) * sum(len(_row_chunks(g.shape[1], g.shape[2] * g.dtype.itemsize)) for g in sums)
    lands = [lax.empty((N_CHIPS - 1,) + g.shape[1:], g.dtype) for g in sums]
    return _split_start("chip_scatter_start", _chip_scatter_copies, n_copies, sums, lands)


def _chip_scatter_wait(send_sems, recv_sems, sums, lands, after):
    return _split_wait("chip_scatter_wait", _chip_scatter_copies, send_sems, recv_sems, sums, lands, after)


def _pair_exchange(pairs):
    n = len(pairs)
    pieces = [_row_chunks(h.shape[1], h.shape[2] * h.dtype.itemsize) for h in pairs]

    def plan(pos, ins, outs):
        return [(ins[a].at[pos[2], pl.ds(r0, nr)], outs[a].at[pos[2], pl.ds(r0, nr)], _flip(pos, 1))
                for a in range(n) for r0, nr in pieces[a]], []

    return _exchange("pair_exchange", pairs, [jax.ShapeDtypeStruct(h.shape, h.dtype) for h in pairs], plan,
                     sum(len(p) for p in pieces), 0, in_place=True)


def _pack(arrays):
    flat = jnp.concatenate([a.reshape(-1).astype(F32) for a in arrays])
    size = -(-flat.shape[0] // (8 * LANE)) * (8 * LANE)
    return jnp.pad(flat, (0, size - flat.shape[0])).reshape(size // LANE, LANE)


def _unpack(buf, shapes):
    flat = buf.reshape(-1)
    out, off = [], 0
    for shp in shapes:
        n = math.prod(shp)
        out.append(flat[off:off + n].reshape(shp))
        off += n
    return out


def _unpack_rows(bufs, shapes):
    flat = bufs.reshape(bufs.shape[0], -1)
    out, off = [], 0
    for shp in shapes:
        n = math.prod(shp)
        out.append(flat[:, off:off + n].reshape((bufs.shape[0],) + shp))
        off += n
    return out


def _taps8(w):
    return jnp.pad(w, ((0, 8 - CONV_K), (0, 0)))


def _local_step(xs, tgt, scale, shift, gate, norm_w, w_in_p, b_in_p, ml_conv_w, ml_conv_b, ml_norm_w, ssm_conv_w,
                ssm_conv_b, ssm_a_log, ssm_d, ssm_norm_w, wpm, wps, wo, final_w, start_exchange=None, late_weights=None,
                u=None):
    s = xs.shape[0]
    ts, tm, tw, tn = min(ROWS_ELEMENTWISE, s), min(ROWS_MATMUL, s), min(ROWS_WIDE_MATMUL, s), COLS_MATMUL
    if u is None:
        u = _prenorm_fwd(xs, norm_w, scale, shift, ts)
    proj = _matmul_bias(u, w_in_p, b_in_p, tw, tn, 0, O_IF, BF16)
    gates = _matmul_bias(u, w_in_p, b_in_p, tw, tn, O_IF, NP - O_IF, F32)
    mlw8, ssw8 = _taps8(ml_conv_w), _taps8(ssm_conv_w)
    tc = min(ROWS_CONV, s)
    qk, qk_dact = _conv_fwd(proj, O_QK, 2048, mlw8, ml_conv_b, tc)
    xbc, xbc_dact = _conv_fwd(proj, O_XBC, 3072, ssw8, ssm_conv_b, tc)
    gt = gates[:, :LANE].T
    dtt = gates[:, O_DT - O_IF:O_DT - O_IF + LANE].T
    hm, cst, nm = _mlstm_fwd(qk, proj, gates, gt)
    alog_row = jnp.pad(ssm_a_log, ((0, 0), (0, LANE - SSM_HEADS)))
    alog_col = alog_row.reshape(LANE, 1)
    dskip_x = jnp.repeat(ssm_d[0], SSM_HEADDIM)[None]
    expand = _head_expand()
    yssd, sst = _ssd_fwd(xbc, gates, dtt, alog_row, alog_col, dskip_x, expand)
    tp = min(ROWS_POST, s)
    ym, ys = _post_fwd(hm, yssd, proj, ml_norm_w, ssm_norm_w, min(ROWS_POST_FWD, s))
    if late_weights is not None:
        wpm, wps, wo = late_weights(ym)
    dxres, merged, dmo, dpm, dps, dym, dys, dproj, acc_m = _merge(xs, ym, ys, proj, tgt, gate, final_w, wpm, wps, wo,
                                                                  min(ROWS_MERGE, s))
    dh, dyssd, dproj, acc_p = _post_bwd(dym, dys, hm, yssd, proj, ml_norm_w, ssm_norm_w, dproj, tp)
    dqk, dproj, dif = _mlstm_bwd(qk, proj, gates, gt, hm, dh, cst, nm, dproj)
    dxbc, ddt, accd, acca = _ssd_bwd(xbc, gates, dtt, alog_row, alog_col, dskip_x, expand, expand.T, dyssd, sst)
    dproj, acc_cq = _conv_bwd(proj, O_QK, 2048, mlw8, qk_dact, dqk, dproj, tc)
    dproj, acc_cx = _conv_bwd(proj, O_XBC, 3072, ssw8, xbc_dact, dxbc, dproj, tc)
    dproj = dproj.at[:, O_IF:O_IF + SMALL_W].set(dif).at[:, O_DT:O_DT + SMALL_W].set(ddt)
    gw_in_p, gb_in_p = _matmul_tn(u.T, dproj, tw, tn, with_colsum=True, a_is_transposed=True, dtype=BF16)
    g_wpm = _matmul_tn(ym, dpm, tm, tn, dtype=BF16)
    g_wps = _matmul_tn(ys, dps, tm, tn, dtype=BF16)
    g_wo = _matmul_tn(merged, dmo, tm, tn, dtype=BF16)
    token, in_flight = (None, None) if start_exchange is None else start_exchange(gw_in_p, g_wpm, g_wps, g_wo)
    du = _matmul_nt(dproj, w_in_p, tm, tn, after=token)
    grad_x, acc_n = _prenorm_bwd(du, xs, dxres, norm_w, scale, ts)
    a_coef = -jnp.exp(ssm_a_log[0])
    small = dict(
        mod=jnp.concatenate([acc_n[2], acc_n[1], acc_m[2]]), norm_w=acc_n[0], b_in=_unpad_cols(gb_in_p[0]),
        ml_conv_w=acc_cq[0:CONV_K], ml_conv_b=acc_cq[CONV_K], ml_norm_w=acc_p[0], ssm_conv_w=acc_cx[0:CONV_K],
        ssm_conv_b=acc_cx[CONV_K], ssm_a_log=acca[0, :SSM_HEADS] * a_coef,
        ssm_d=accd[0].reshape(SSM_HEADS, SSM_HEADDIM).sum(axis=1), ssm_norm_w=acc_p[1], final_w=acc_m[1], loss=acc_m[0, 0:1])
    return grad_x, small, gw_in_p, g_wpm, g_wps, g_wo, in_flight


WEIGHTS = ("norm_w", "ada_w", "ada_b", "w_in", "b_in", "ml_conv_w", "ml_conv_b", "ml_norm_w", "ssm_conv_w", "ssm_conv_b",
           "ssm_a_log", "ssm_d", "ssm_norm_w", "w_proj_m", "w_proj_s", "w_out", "final_w")
LARGE = ("ada_w", "w_in", "w_proj_m", "w_proj_s", "w_out")
SMALL_SUMS = (("mod", (3 * D_MODEL,)), ("norm_w", (D_MODEL,)), ("b_in", (IN_WIDTH,)), ("ml_conv_w", (CONV_K, 2048)),
              ("ml_conv_b", (2048,)), ("ml_norm_w", (2048,)), ("ssm_conv_w", (CONV_K, 3072)), ("ssm_conv_b", (3072,)),
              ("ssm_a_log", (SSM_HEADS,)), ("ssm_d", (SSM_HEADS,)), ("ssm_norm_w", (2048,)), ("final_w", (D_MODEL,)),
              ("loss", (1,)))


def kernel(x, c, norm_w, ada_w, ada_b, w_in, b_in, ml_conv_w, ml_conv_b, ml_norm_w, ssm_conv_w, ssm_conv_b, ssm_a_log, ssm_d, ssm_norm_w, w_proj_m, w_proj_s, w_out, final_w, loss_target, m_norm_w, m_ada_w, m_ada_b, m_w_in, m_b_in, m_ml_conv_w, m_ml_conv_b, m_ml_norm_w, m_ssm_conv_w, m_ssm_conv_b, m_ssm_a_log, m_ssm_d, m_ssm_norm_w, m_w_proj_m, m_w_proj_s, m_w_out, m_final_w, v_norm_w, v_ada_w, v_ada_b, v_w_in, v_b_in, v_ml_conv_w, v_ml_conv_b, v_ml_norm_w, v_ssm_conv_w, v_ssm_conv_b, v_ssm_a_log, v_ssm_d, v_ssm_norm_w, v_w_proj_m, v_w_proj_s, v_w_out, v_final_w):
    w = dict(norm_w=norm_w, ada_w=ada_w, ada_b=ada_b, w_in=w_in, b_in=b_in, ml_conv_w=ml_conv_w, ml_conv_b=ml_conv_b,
             ml_norm_w=ml_norm_w, ssm_conv_w=ssm_conv_w, ssm_conv_b=ssm_conv_b, ssm_a_log=ssm_a_log, ssm_d=ssm_d,
             ssm_norm_w=ssm_norm_w, w_proj_m=w_proj_m, w_proj_s=w_proj_s, w_out=w_out, final_w=final_w)
    m = dict(zip(WEIGHTS, (m_norm_w, m_ada_w, m_ada_b, m_w_in, m_b_in, m_ml_conv_w, m_ml_conv_b, m_ml_norm_w, m_ssm_conv_w,
                           m_ssm_conv_b, m_ssm_a_log, m_ssm_d, m_ssm_norm_w, m_w_proj_m, m_w_proj_s, m_w_out, m_final_w)))
    v = dict(zip(WEIGHTS, (v_norm_w, v_ada_w, v_ada_b, v_w_in, v_b_in, v_ml_conv_w, v_ml_conv_b, v_ml_norm_w, v_ssm_conv_w,
                           v_ssm_conv_b, v_ssm_a_log, v_ssm_d, v_ssm_norm_w, v_w_proj_m, v_w_proj_s, v_w_out, v_final_w)))
    pos = _position()
    chip = 2 * pos[0] + pos[1]
    dev = 2 * chip + pos[2]
    mlw_cols, ssw_cols, ada_cols = ml_conv_w.shape[2], ssm_conv_w.shape[2], ada_w.shape[2]

    g0 = _allgather8(_pack([c, ml_conv_w, ssm_conv_w]))
    c_all, mlw_all, ssw_all = _unpack_rows(g0, [(D_MODEL,), (CONV_K, mlw_cols), (CONV_K, ssw_cols)])
    ml_conv_full = mlw_all[0::2].transpose(1, 0, 2).reshape(CONV_K, N_CHIPS * mlw_cols)
    ssm_conv_full = ssw_all[0::2].transpose(1, 0, 2).reshape(CONV_K, N_CHIPS * ssw_cols)

    ada_b_mine = lax.dynamic_slice_in_dim(ada_b, chip * ada_cols, ada_cols, axis=1)
    g1 = _allgather8(_ada_fwd(c_all, ada_w[0], ada_b_mine))
    mod = lax.dynamic_index_in_dim(g1[0::2], dev, axis=1, keepdims=False).reshape(1, 3 * D_MODEL)
    shift, scale, gate = mod[:, :D_MODEL], mod[:, D_MODEL:2 * D_MODEL], mod[:, 2 * D_MODEL:]

    def whole(lands, owns):
        return [lax.dynamic_update_index_in_dim(got, own, chip, 0).reshape(N_CHIPS, -1, own.shape[-1])
                for got, own in zip(_pair_forward(lands), owns)]

    mine = [_bf(a[0]).reshape(2, a.shape[1] // 2, a.shape[2]) for a in (w_in, w_proj_m, w_proj_s, w_out)]
    landing = lambda own: lax.empty((N_CHIPS,) + own.shape, own.dtype)
    w_send, w_recv, w_src, w_land, w_token = _split_start("w_in_gather_start", _gather_copies, _gather_pieces(mine[:1]),
                                                          mine[:1], [landing(mine[0])])
    u = _prenorm_fwd(x[0], norm_w, scale + w_token[0:1, 0:1], shift, ROWS_ELEMENTWISE)
    w_src, w_land = _split_wait("w_in_gather_wait", _gather_copies, w_send, w_recv, w_src, w_land, u)
    behind = (w_src[0][0, 0:1, 0:1] * 0).astype(BF16)
    later = [a + behind for a in mine[1:]]
    p_send, p_recv, p_src, p_land, p_token = _split_start("merge_gather_start", _gather_copies, _gather_pieces(later),
                                                          later, [landing(a) for a in later])
    w_in_p = _shards_to_padded(whole(w_land, w_src)[0])
    b_in_p = _pad_cols(b_in) + p_token[0:1, 0:1]

    def late_weights(after):
        srcs, lands = _split_wait("merge_gather_wait", _gather_copies, p_send, p_recv, p_src, p_land, after)
        return [a.reshape(-1, D_MODEL) for a in whole(lands, srcs)]

    def start_exchange(g_w_in, g_wpm, g_wps, g_wo):
        split = lambda g, rows: _bf(g).reshape(N_CHIPS, 2, rows // (2 * N_CHIPS), g.shape[-1])
        slabs = [split(_padded_to_shards(_bf(g_w_in)), N_CHIPS * D_MODEL),
                 split(g_wpm, g_wpm.shape[0]), split(g_wps, g_wps.shape[0]), split(g_wo, g_wo.shape[0])]
        pair_sums = []
        for slab, rec in zip(slabs, _pair_send(slabs)):
            kept = lax.dynamic_index_in_dim(slab, pos[2], 1, keepdims=False)
            rows = kept.shape[0] * kept.shape[1]
            both = _sum_parts(kept.reshape(rows, -1), rec.reshape(1, rows, -1), ROWS_SUM, BF16)
            pair_sums.append(both.reshape(kept.shape))
        send_sems, recv_sems, sums, lands, token = _chip_scatter_start(pair_sums)
        return token, (send_sems, recv_sems, sums, lands)

    grad_x, small, _, _, _, _, in_flight = _local_step(
        x[0], loss_target[0], scale, shift, gate, norm_w, w_in_p, b_in_p, ml_conv_full, ml_conv_b, ml_norm_w,
        ssm_conv_full, ssm_conv_b, ssm_a_log, ssm_d, ssm_norm_w, None, None, None, final_w[None], start_exchange,
        late_weights, u)

    g2 = _allgather8(_pack([small[name] for name, _ in SMALL_SUMS]))
    total = dict(zip([name for name, _ in SMALL_SUMS], _unpack(_sum_parts(None, g2, g2.shape[1]), [s for _, s in SMALL_SUMS])))
    dmod_all = g2[:, :3 * D_MODEL // LANE].reshape(N_DEV, 3 * D_MODEL)
    grads = dict(total)
    grads["ada_b"] = total["mod"]
    grads["ml_conv_w"] = lax.dynamic_slice_in_dim(total["ml_conv_w"], chip * mlw_cols, mlw_cols, axis=1)
    grads["ssm_conv_w"] = lax.dynamic_slice_in_dim(total["ssm_conv_w"], chip * ssw_cols, ssw_cols, axis=1)
    grads["ada_w"] = _ada_bwd(c_all, lax.dynamic_slice_in_dim(dmod_all, chip * ada_cols, ada_cols, axis=1))

    pairs = []
    for both, rec in zip(*_chip_scatter_wait(*in_flight, grad_x)):
        pairs.append(_sum_parts(lax.dynamic_index_in_dim(both, chip, 0, keepdims=False), rec, ROWS_SUM, slot=pos[2]))
    for name, full in zip(("w_in", "w_proj_m", "w_proj_s", "w_out"), _pair_exchange(pairs)):
        grads[name] = full.reshape(-1, full.shape[-1])

    delta, new_m, new_v = {}, {}, {}
    for name in LARGE:
        if w[name].shape[-1] % LANE:
            flat = lambda a: a.reshape(a.shape[-2:]).T
            back = lambda a: a.T.reshape(w[name].shape)
            g_flat = flat(grads[name])
            delta[name], new_m[name], new_v[name] = (back(a) for a in _adamw(flat(w[name]), g_flat, flat(m[name]), flat(v[name]), LANE))
            grads[name] = back(g_flat)
        else:
            delta[name], new_m[name], new_v[name] = _adamw(w[name], grads[name], m[name], v[name], ROWS_ADAMW)
    rest = [name for name in WEIGHTS if name not in LARGE]
    packed = [_pack([t[name] for name in rest]) for t in (w, grads, m, v)]
    for out, buf in zip((delta, new_m, new_v), _adamw(*packed, packed[0].shape[0])):
        out.update(zip(rest, _unpack(buf, [w[name].shape for name in rest])))
    loss = total["loss"][0]
    return (loss, grad_x[None], *[grads[name].reshape(w[name].shape) for name in WEIGHTS], *[delta[name] for name in WEIGHTS],
            *[new_m[name] for name in WEIGHTS], *[new_v[name] for name in WEIGHTS])
```

```python
import math

import jax
import jax.numpy as jnp
from jax import lax
from jax.experimental import pallas as pl
from jax.experimental.pallas import tpu as pltpu

F32 = jnp.float32
BF16 = jnp.bfloat16
MESH = pl.DeviceIdType.MESH

D_MODEL = 1024
EPS = 1e-6
CONV_K = 4
ML_HEADS = 8
ML_DQK = 128
ML_DV = 256
SSM_HEADS = 32
SSM_HEADDIM = 64
SSM_GROUPS = 4
SSM_STATE = 128
IN_WIDTH = 15408
N_CHIPS = 4
N_DEV = 8
ADAM_LR, ADAM_B1, ADAM_B2, ADAM_EPS, ADAM_WD, ADAM_STEP = 0.001, 0.9, 0.999, 1e-08, 0.01, 10

O_O, O_ZM, O_ZS, O_MG, O_QK, O_V, O_XBC, O_IF, O_DT = 0, 2048, 4096, 6144, 8192, 10240, 12288, 15360, 15616
SMALL_W = 256
NP = 15872
LANE = 128
CHUNK = 128
NEG = -1e30
VMEM_LIMIT = 48 * 1024 * 1024
MERGE_VMEM = 60 * 1024 * 1024
ROWS_ELEMENTWISE = 512
ROWS_CONV = 1024
ROWS_MATMUL = 2048
ROWS_WIDE_MATMUL = 4096
COLS_MATMUL = 512
ROWS_POST = 128
ROWS_POST_FWD = 256
ROWS_MERGE = 256
ROWS_SUM = 128
ROWS_ADAMW = 256


def _cparams(*sem, vmem=VMEM_LIMIT):
    return pltpu.CompilerParams(dimension_semantics=sem, vmem_limit_bytes=vmem)


def _pad_cols(w):
    z = lambda n: jnp.zeros(w.shape[:-1] + (n,), w.dtype)
    return jnp.concatenate([w[..., 4096:8192], w[..., 11280:13328], w[..., 13360:15408], w[..., :4096], w[..., 8208:11280],
                            w[..., 8192:8208], z(SMALL_W - 16), w[..., 13328:13360], z(SMALL_W - 32)], axis=-1)


def _unpad_cols(g):
    return jnp.concatenate([g[..., O_QK:O_QK + 4096], g[..., O_O:O_O + 4096], g[..., O_IF:O_IF + 16],
                            g[..., O_XBC:O_XBC + 3072], g[..., O_ZS:O_ZS + 2048], g[..., O_DT:O_DT + 32],
                            g[..., O_MG:O_MG + 2048]], axis=-1)


PADDED_SEGMENTS = ((4096, 8192, 0), (11280, 13328, 0), (13360, 15408, 0), (0, 4096, 0), (8208, 11280, 0),
                   (8192, 8208, SMALL_W - 16), (13328, 13360, SMALL_W - 32))
SHARD_W = IN_WIDTH // N_CHIPS


def _shards_to_padded(shards):
    parts = []
    for first, last, pad in PADDED_SEGMENTS:
        for j in range(N_CHIPS):
            lo, hi = max(first, j * SHARD_W), min(last, (j + 1) * SHARD_W)
            if lo < hi:
                parts.append(shards[j][:, lo - j * SHARD_W:hi - j * SHARD_W])
        if pad:
            parts.append(jnp.zeros((shards.shape[1], pad), shards.dtype))
    return jnp.concatenate(parts, axis=1)


def _padded_to_shards(g):
    offsets, off = {}, 0
    for first, last, pad in PADDED_SEGMENTS:
        offsets[first] = off
        off += last - first + pad
    shards = []
    for j in range(N_CHIPS):
        parts = []
        for first, last, _ in sorted(PADDED_SEGMENTS):
            lo, hi = max(first, j * SHARD_W), min(last, (j + 1) * SHARD_W)
            if lo < hi:
                parts.append(g[:, offsets[first] + lo - first:offsets[first] + hi - first])
        shards.append(jnp.concatenate(parts, axis=1))
    return jnp.stack(shards)


def _sigmoid(x):
    return 0.5 * jnp.tanh(0.5 * x) + 0.5


def _silu(x):
    return x * _sigmoid(x)


def _dsilu(x):
    s = _sigmoid(x)
    return s + x * s * (1.0 - s)


def _softplus(x):
    return jnp.maximum(x, 0.0) + jnp.log(1.0 + jnp.exp(-jnp.abs(x)))


def _logsigmoid(x):
    return jnp.minimum(x, 0.0) - jnp.log(1.0 + jnp.exp(-jnp.abs(x)))


def _dot(a, b, dims):
    return lax.dot_general(a, b, (dims, ((), ())), preferred_element_type=F32)


def _nn(a, b):
    return _dot(a, b, ((1,), (0,)))


def _nt(a, b):
    return _dot(a, b, ((1,), (1,)))


def _tn(a, b):
    return _dot(a, b, ((0,), (0,)))


def _bf(x):
    return x.astype(BF16)


def _split(x, terms):
    parts = []
    for _ in range(terms):
        part = _bf(x)
        parts.append(part)
        x = x - part.astype(F32)
    return parts


def _pick_right(x, pick, terms):
    pick = _bf(pick)
    out = None
    for part in _split(x, terms):
        out = _nn(part, pick) if out is None else out + _nn(part, pick)
    return out


def _pick_left(pick, x, terms):
    pick = _bf(pick)
    out = None
    for part in _split(x, terms):
        out = _nn(pick, part) if out is None else out + _nn(pick, part)
    return out


def _lane_col(x, lane):
    idx = lax.broadcasted_iota(jnp.int32, x.shape, 1)
    return jnp.sum(jnp.where(idx == lane, x, 0.0), axis=1, keepdims=True)


def _tri(n, upper):
    r = lax.broadcasted_iota(jnp.int32, (n, n), 0)
    c = lax.broadcasted_iota(jnp.int32, (n, n), 1)
    return jnp.where((r <= c) if upper else (r >= c), 1.0, 0.0).astype(F32)


def _eye(n):
    return jnp.where(lax.broadcasted_iota(jnp.int32, (n, n), 0) == lax.broadcasted_iota(jnp.int32, (n, n), 1), 1.0, 0.0)


def _sum_all(x):
    return jnp.sum(jnp.sum(x, axis=1, keepdims=True), axis=0, keepdims=True)


def _crossing(p):
    L = p.shape[0]
    below = _nn(_bf(_tri(L, True)), _bf(p))
    strict = lax.broadcasted_iota(jnp.int32, (L, L), 0) > lax.broadcasted_iota(jnp.int32, (L, L), 1)
    return [jnp.sum(jnp.where(strict, below[:, b * L:(b + 1) * L], 0.0), axis=1, keepdims=True)
            for b in range(p.shape[1] // L)]


def _matmul_bias(a, w, bias, tm, tn, col0, ncols, dtype):
    m, k = a.shape
    j0 = col0 // tn

    def body(a_ref, w_ref, b_ref, o_ref):
        o_ref[...] = (_nn(a_ref[...], w_ref[...]) + b_ref[...]).astype(dtype)

    return pl.pallas_call(
        body, name="matmul_bias", grid=(m // tm, ncols // tn),
        in_specs=[pl.BlockSpec((tm, k), lambda i, j: (i, 0)), pl.BlockSpec((k, tn), lambda i, j: (0, j0 + j)),
                  pl.BlockSpec((1, tn), lambda i, j: (0, j0 + j))],
        out_specs=pl.BlockSpec((tm, tn), lambda i, j: (i, j)),
        out_shape=jax.ShapeDtypeStruct((m, ncols), dtype),
        compiler_params=_cparams("parallel", "arbitrary"))(a, w, bias)


def _matmul_nt(a, w, tm, tk, after=None):
    m, n = a.shape
    k = w.shape[0]

    def body(a_ref, w_ref, *rest):
        o_ref = rest[-1]

        @pl.when(pl.program_id(1) == 0)
        def _():
            o_ref[...] = jnp.zeros_like(o_ref)
        o_ref[...] += _nt(a_ref[...], w_ref[...])

    extra = [] if after is None else [after]
    return pl.pallas_call(
        body, name="matmul_nt", grid=(m // tm, n // tk),
        in_specs=[pl.BlockSpec((tm, tk), lambda i, j: (i, j)), pl.BlockSpec((k, tk), lambda i, j: (0, j))]
        + [pl.BlockSpec(memory_space=pl.ANY)] * len(extra),
        out_specs=pl.BlockSpec((tm, k), lambda i, j: (i, 0)),
        out_shape=jax.ShapeDtypeStruct((m, k), F32),
        compiler_params=_cparams("parallel", "arbitrary"))(a, w, *extra)


def _matmul_tn(a, b, tm, tn, with_colsum=False, a_is_transposed=False, dtype=F32):
    k, m = a.shape if a_is_transposed else a.shape[::-1]
    n = b.shape[1]
    steps = m // tm

    def body(a_ref, b_ref, o_ref, *rest):
        acc_ref = o_ref if dtype == F32 else rest[-1]
        first = pl.program_id(1) == 0

        @pl.when(first)
        def _():
            acc_ref[...] = jnp.zeros_like(acc_ref)
        acc_ref[...] += _nn(a_ref[...], b_ref[...]) if a_is_transposed else _tn(a_ref[...], b_ref[...])
        if dtype != F32:
            @pl.when(pl.program_id(1) == steps - 1)
            def _():
                o_ref[...] = acc_ref[...].astype(dtype)
        if with_colsum:
            s_ref = rest[0]

            @pl.when(first)
            def _():
                s_ref[...] = jnp.zeros_like(s_ref)
            s_ref[...] += jnp.sum(b_ref[...].astype(F32), axis=0, keepdims=True)

    out_specs = [pl.BlockSpec((k, tn), lambda j, i: (0, j))]
    out_shape = [jax.ShapeDtypeStruct((k, n), dtype)]
    if with_colsum:
        out_specs.append(pl.BlockSpec((1, tn), lambda j, i: (0, j)))
        out_shape.append(jax.ShapeDtypeStruct((1, n), F32))
    out = pl.pallas_call(
        body, name="matmul_tn", grid=(n // tn, m // tm),
        in_specs=[pl.BlockSpec((k, tm), lambda j, i: (0, i)) if a_is_transposed else pl.BlockSpec((tm, k), lambda j, i: (i, 0)),
                  pl.BlockSpec((tm, tn), lambda j, i: (i, j))],
        out_specs=out_specs, out_shape=out_shape,
        scratch_shapes=[] if dtype == F32 else [pltpu.VMEM((k, tn), F32)],
        compiler_params=_cparams("parallel", "arbitrary"))(a, b)
    return out if with_colsum else out[0]


def _ada_fwd(c_all, ada_w, ada_b):
    def body(c_ref, w_ref, b_ref, o_ref):
        o_ref[...] = _nn(_bf(_silu(c_ref[...])), _bf(w_ref[...])) + b_ref[...]

    return pl.pallas_call(body, name="ada_fwd", out_shape=jax.ShapeDtypeStruct((c_all.shape[0], ada_w.shape[1]), F32),
                          compiler_params=_cparams())(c_all, ada_w, ada_b)


def _ada_bwd(c_all, dmod):
    def body(c_ref, d_ref, o_ref):
        o_ref[...] = _tn(_bf(_silu(c_ref[...])), _bf(d_ref[...]))

    return pl.pallas_call(body, name="ada_bwd", out_shape=jax.ShapeDtypeStruct((c_all.shape[1], dmod.shape[1]), F32),
                          compiler_params=_cparams())(c_all, dmod)


def _prenorm_fwd(x, norm_w, scale, shift, ts):
    s, d = x.shape

    def body(x_ref, nw_ref, sc_ref, sh_ref, u_ref, ut_ref):
        xv = x_ref[...]
        r = lax.rsqrt(jnp.mean(xv * xv, axis=1, keepdims=True) + EPS)
        u = xv * r * nw_ref[...] * (1.0 + sc_ref[...]) + sh_ref[...]
        u_ref[...] = _bf(u)
        ut_ref[...] = _bf(u.T)

    row = pl.BlockSpec((1, d), lambda i: (0, 0))
    return pl.pallas_call(
        body, name="prenorm_fwd", grid=(s // ts,),
        in_specs=[pl.BlockSpec((ts, d), lambda i: (i, 0)), row, row, row],
        out_specs=[pl.BlockSpec((ts, d), lambda i: (i, 0)), pl.BlockSpec((d, ts), lambda i: (0, i))],
        out_shape=[jax.ShapeDtypeStruct((s, d), BF16), jax.ShapeDtypeStruct((d, s), BF16)],
        compiler_params=_cparams("parallel"))(x, norm_w, scale, shift)


def _prenorm_bwd(du, x, dxres, norm_w, scale, ts):
    s, d = x.shape

    def body(du_ref, x_ref, dr_ref, nw_ref, sc_ref, gx_ref, acc_ref):
        @pl.when(pl.program_id(0) == 0)
        def _():
            acc_ref[...] = jnp.zeros_like(acc_ref)
        xv, duv = x_ref[...], du_ref[...]
        r = lax.rsqrt(jnp.mean(xv * xv, axis=1, keepdims=True) + EPS)
        xn = xv * r
        nw, sc1 = nw_ref[...], 1.0 + sc_ref[...]
        dxn = duv * (nw * sc1)
        gx_ref[...] = r * (dxn - xn * jnp.mean(dxn * xn, axis=1, keepdims=True)) + dr_ref[...]
        t = duv * xn
        acc_ref[0:1, :] += jnp.sum(t, axis=0, keepdims=True) * sc1
        acc_ref[1:2, :] += jnp.sum(t, axis=0, keepdims=True) * nw
        acc_ref[2:3, :] += jnp.sum(duv, axis=0, keepdims=True)

    tile = pl.BlockSpec((ts, d), lambda i: (i, 0))
    row = pl.BlockSpec((1, d), lambda i: (0, 0))
    return pl.pallas_call(
        body, name="prenorm_bwd", grid=(s // ts,),
        in_specs=[tile, tile, tile, row, row],
        out_specs=[tile, pl.BlockSpec((8, d), lambda i: (0, 0))],
        out_shape=[jax.ShapeDtypeStruct((s, d), F32), jax.ShapeDtypeStruct((8, d), F32)],
        compiler_params=_cparams("arbitrary"))(du, x, dxres, norm_w, scale)


CONV_CB = 512


def _conv_taps(buf_ref, ts):
    return [buf_ref[pl.ds(8 - (CONV_K - 1) + j, ts), :] for j in range(CONV_K)]


def _conv_fwd(proj, col0, width, w8, b, ts):
    s = proj.shape[0]
    cb = CONV_CB
    nt = s // ts

    def body(x_ref, w_ref, b_ref, o_ref, ds_ref, buf_ref):
        @pl.when(pl.program_id(1) == 0)
        def _():
            buf_ref[0:8, :] = jnp.zeros((8, cb), F32)
        buf_ref[pl.ds(8, ts), :] = x_ref[...].astype(F32)
        acc = b_ref[...] + jnp.zeros((ts, cb), F32)
        for j, tap in enumerate(_conv_taps(buf_ref, ts)):
            acc = acc + tap * w_ref[j:j + 1, :]
        sg = _sigmoid(acc)
        o_ref[...] = acc * sg
        ds_ref[...] = _bf(sg + acc * sg * (1.0 - sg))
        buf_ref[0:8, :] = buf_ref[pl.ds(ts, 8), :]

    c0 = col0 // cb
    tile = pl.BlockSpec((ts, cb), lambda c, i: (i, c))
    return pl.pallas_call(
        body, name="conv_fwd", grid=(width // cb, nt),
        in_specs=[pl.BlockSpec((ts, cb), lambda c, i: (i, c0 + c)), pl.BlockSpec((8, cb), lambda c, i: (0, c)),
                  pl.BlockSpec((1, cb), lambda c, i: (0, c))],
        out_specs=[tile, tile],
        out_shape=[jax.ShapeDtypeStruct((s, width), F32), jax.ShapeDtypeStruct((s, width), BF16)],
        scratch_shapes=[pltpu.VMEM((ts + 8, cb), F32)],
        compiler_params=_cparams("parallel", "arbitrary"))(proj, w8, b)


def _conv_bwd(proj, col0, width, w8, dact, dpost, dproj, ts):
    s = proj.shape[0]
    cb = CONV_CB
    nt = s // ts
    c0 = col0 // cb

    def body(x_ref, da_ref, dp_ref, w_ref, _, dx_ref, acc_ref, dbuf_ref):
        @pl.when(pl.program_id(1) == 0)
        def _():
            acc_ref[...] = jnp.zeros_like(acc_ref)
            dbuf_ref[pl.ds(ts, 8), :] = jnp.zeros((8, cb), F32)
        dconv = dp_ref[...].astype(F32) * da_ref[...].astype(F32)
        acc_ref[CONV_K:CONV_K + 1, :] += jnp.sum(dconv, axis=0, keepdims=True)
        dbuf_ref[pl.ds(0, ts), :] = dconv
        xv = x_ref[...].astype(F32)
        dx = jnp.zeros((ts, cb), F32)
        for j in range(CONV_K):
            shifted = dbuf_ref[pl.ds(CONV_K - 1 - j, ts), :]
            dx = dx + shifted * w_ref[j:j + 1, :]
            acc_ref[j:j + 1, :] += jnp.sum(xv * shifted, axis=0, keepdims=True)
        dx_ref[...] = _bf(dx)
        dbuf_ref[pl.ds(ts, 8), :] = dconv[0:8, :]

    tile = pl.BlockSpec((ts, cb), lambda c, i: (nt - 1 - i, c))
    wide = pl.BlockSpec((ts, cb), lambda c, i: (nt - 1 - i, c0 + c))
    return pl.pallas_call(
        body, name="conv_bwd", grid=(width // cb, nt),
        in_specs=[wide, tile, tile, pl.BlockSpec((8, cb), lambda c, i: (0, c)), pl.BlockSpec(memory_space=pl.ANY)],
        out_specs=[wide, pl.BlockSpec((8, cb), lambda c, i: (0, c))],
        out_shape=[jax.ShapeDtypeStruct(dproj.shape, dproj.dtype), jax.ShapeDtypeStruct((8, width), F32)],
        input_output_aliases={4: 0},
        scratch_shapes=[pltpu.VMEM((ts + 8, cb), F32)],
        compiler_params=_cparams("parallel", "arbitrary"))(proj, dact, dpost, w8, dproj)


def _mlstm_gates(gif_ref, gt_ref, a_scr, at_scr):
    L = gif_ref.shape[0]
    fb = _logsigmoid(gif_ref[...])
    a_scr[...] = _pick_left(_tri(L, False), fb, 3)
    at_scr[...] = _pick_right(_logsigmoid(gt_ref[...]), _tri(L, True), 3)
    return jnp.sum(fb, axis=0, keepdims=True)


def _mlstm_head(h, qk_ref, v_ref, gif, gt_ref, a, at_scr, a_last_row, c_mat, n_row, m_prev):
    L = gif.shape[0]
    q = qk_ref[:, h * ML_DQK:(h + 1) * ML_DQK] * (ML_DQK ** -0.5)
    k = qk_ref[:, (ML_HEADS + h) * ML_DQK:(ML_HEADS + h + 1) * ML_DQK]
    v = v_ref[:, h * ML_DV:(h + 1) * ML_DV]
    i_col, a_col = _lane_col(gif, h), _lane_col(a, ML_HEADS + h)
    i_row, a_row = gt_ref[h:h + 1, :], at_scr[ML_HEADS + h:ML_HEADS + h + 1, :]
    causal = lax.broadcasted_iota(jnp.int32, (L, L), 0) >= lax.broadcasted_iota(jnp.int32, (L, L), 1)
    dmat = jnp.where(causal, a_col - a_row + i_row, NEG)
    inter = a_col + m_prev
    m_t = jnp.maximum(inter, jnp.max(dmat, axis=1, keepdims=True))
    w_intra = jnp.exp(dmat - m_t)
    w_inter = jnp.exp(inter - m_t)
    sc = _nt(_bf(q), _bf(k)) * w_intra
    den = jnp.sum(sc, axis=1, keepdims=True) + w_inter * jnp.sum(q * n_row, axis=1, keepdims=True)
    floor = jnp.exp(-m_t)
    a_last = _lane_col(a_last_row, ML_HEADS + h)
    g = a_last - a_col + i_col
    m_new = jnp.maximum(a_last + m_prev, jnp.max(g, axis=0, keepdims=True))
    wk = jnp.exp(g - m_new)
    decay = jnp.exp(a_last + m_prev - m_new)
    return dict(q=q, k=k, v=v, w_intra=w_intra, w_inter=w_inter, sc=sc, den=den, floor=floor, m_new=m_new, wk=wk,
                decay=decay)


def _state_tile(n_row, m11):
    r = lax.broadcasted_iota(jnp.int32, (8, LANE), 0)
    return jnp.where(r == 0, n_row, jnp.where(r == 1, m11, 0.0))


def _mlstm_fwd(qk, proj, gates, gt):
    s = qk.shape[0]
    L = CHUNK
    nc = s // L

    def body(qk_ref, v_ref, gif_ref, gt_ref, h_ref, cst_ref, nm_ref, c_scr, nm_scr, a_scr, at_scr):
        @pl.when(pl.program_id(0) == 0)
        def _():
            c_scr[...] = jnp.zeros_like(c_scr)
            nm_scr[...] = jnp.zeros_like(nm_scr)
        a_last_row = _mlstm_gates(gif_ref, gt_ref, a_scr, at_scr)
        gif, a = gif_ref[...], a_scr[...]
        for h in range(ML_HEADS):
            c_mat, n_row = c_scr[h], nm_scr[h, 0:1, :]
            m_prev = jnp.max(nm_scr[h, 1:2, :], axis=1, keepdims=True)
            cst_ref[0, h] = c_mat
            nm_ref[0, h] = nm_scr[h]
            t = _mlstm_head(h, qk_ref, v_ref, gif, gt_ref, a, at_scr, a_last_row, c_mat, n_row, m_prev)
            num = _nn(_bf(t["sc"]), _bf(t["v"])) + t["w_inter"] * _nn(_bf(t["q"]), _bf(c_mat))
            h_ref[:, h * ML_DV:(h + 1) * ML_DV] = _bf(num * (1.0 / jnp.maximum(jnp.abs(t["den"]), t["floor"])))
            kw = t["k"] * t["wk"]
            c_scr[h] = t["decay"] * c_mat + _tn(_bf(kw), _bf(t["v"]))
            nm_scr[h] = _state_tile(t["decay"] * n_row + jnp.sum(kw, axis=0, keepdims=True), t["m_new"])

    return pl.pallas_call(
        body, name="mlstm_fwd", grid=(nc,),
        in_specs=[pl.BlockSpec((L, 2048), lambda c: (c, 0)), pl.BlockSpec((L, 2048), lambda c: (c, O_V // 2048)),
                  pl.BlockSpec((L, LANE), lambda c: (c, 0)), pl.BlockSpec((LANE, L), lambda c: (0, c))],
        out_specs=[pl.BlockSpec((L, 2048), lambda c: (c, 0)),
                   pl.BlockSpec((1, ML_HEADS, ML_DQK, ML_DV), lambda c: (c, 0, 0, 0)),
                   pl.BlockSpec((1, ML_HEADS, 8, LANE), lambda c: (c, 0, 0, 0))],
        out_shape=[jax.ShapeDtypeStruct((s, 2048), BF16), jax.ShapeDtypeStruct((nc, ML_HEADS, ML_DQK, ML_DV), F32),
                   jax.ShapeDtypeStruct((nc, ML_HEADS, 8, LANE), F32)],
        scratch_shapes=[pltpu.VMEM((ML_HEADS, ML_DQK, ML_DV), F32), pltpu.VMEM((ML_HEADS, 8, LANE), F32),
                        pltpu.VMEM((L, LANE), F32), pltpu.VMEM((LANE, L), F32)],
        compiler_params=_cparams("arbitrary"))(qk, proj, gates, gt)


def _mlstm_bwd(qk, proj, gates, gt, hout, dh, cst, nm, dproj):
    s = qk.shape[0]
    L = CHUNK
    nc = s // L

    def body(qk_ref, v_ref, gif_ref, gt_ref, h_ref, dh_ref, cst_ref, nm_ref, _, dqk_ref, dv_ref, dif_ref,
             dc_scr, dn_scr, a_scr, at_scr):
        @pl.when(pl.program_id(0) == 0)
        def _():
            dc_scr[...] = jnp.zeros_like(dc_scr)
            dn_scr[...] = jnp.zeros_like(dn_scr)
        a_last_row = _mlstm_gates(gif_ref, gt_ref, a_scr, at_scr)
        gif, a = gif_ref[...], a_scr[...]
        lane = lax.broadcasted_iota(jnp.int32, (L, LANE), 1)
        last = lax.broadcasted_iota(jnp.int32, (L, 1), 0) == L - 1
        di_tile = jnp.zeros((L, LANE), F32)
        cross = [jnp.zeros((L, LANE), F32)] * 3
        dlogw = []
        for h in range(ML_HEADS):
            c_mat, n_row = cst_ref[0, h], nm_ref[0, h, 0:1, :]
            m_prev = jnp.max(nm_ref[0, h, 1:2, :], axis=1, keepdims=True)
            t = _mlstm_head(h, qk_ref, v_ref, gif, gt_ref, a, at_scr, a_last_row, c_mat, n_row, m_prev)
            q, k, v, den = t["q"], t["k"], t["v"], t["den"]
            dhh = dh_ref[:, h * ML_DV:(h + 1) * ML_DV].astype(F32)
            hh = h_ref[:, h * ML_DV:(h + 1) * ML_DV].astype(F32)
            dnorm = jnp.maximum(jnp.abs(den), t["floor"])
            inv = 1.0 / dnorm
            dnum = dhh * inv
            d_dn = -jnp.sum(dhh * hh, axis=1, keepdims=True) * inv
            dden = jnp.where(jnp.abs(den) >= t["floor"], jnp.where(den >= 0.0, d_dn, -d_dn), 0.0)
            dsc = _nt(_bf(dnum), _bf(v)) + dden
            ds = dsc * t["w_intra"]
            dq_inter = t["w_inter"] * (_nt(_bf(dnum), _bf(c_mat)) + dden * n_row)
            dq = _nn(_bf(ds), _bf(k)) + dq_inter
            dc, dn_row = dc_scr[h], dn_scr[h, 0:1, :]
            dk_state = t["wk"] * (_nt(_bf(v), _bf(dc)) + dn_row)
            dk = _tn(_bf(ds), _bf(q)) + dk_state
            dv = _tn(_bf(t["sc"]), _bf(dnum)) + t["wk"] * _nn(_bf(k), _bf(dc))
            qi = q * t["w_inter"]
            dc_scr[h] = t["decay"] * dc + _tn(_bf(qi), _bf(dnum))
            dn_scr[h] = jnp.broadcast_to(t["decay"] * dn_row + jnp.sum(qi * dden, axis=0, keepdims=True), (8, LANE))
            dqk_ref[:, h * ML_DQK:(h + 1) * ML_DQK] = _bf(dq * (ML_DQK ** -0.5))
            dqk_ref[:, (ML_HEADS + h) * ML_DQK:(ML_HEADS + h + 1) * ML_DQK] = _bf(dk)
            dv_ref[:, h * ML_DV:(h + 1) * ML_DV] = _bf(dv)
            di_tile = di_tile + jnp.where(lane == h, jnp.sum(k * dk, axis=1, keepdims=True), 0.0)
            carried = t["decay"] * (_sum_all(dc * c_mat) + jnp.sum(dn_row * n_row, axis=1, keepdims=True))
            dlogw.append(dsc * t["sc"])
            parts = (jnp.sum(q * dq_inter, axis=1, keepdims=True) + jnp.where(last, carried, 0.0),
                     jnp.sum(k * dk_state, axis=1, keepdims=True))
            cross[1:] = [c + jnp.where(lane == ML_HEADS + h, p, 0.0) for c, p in zip(cross[1:], parts)]
        for h, col in enumerate(_crossing(jnp.concatenate(dlogw, axis=1))):
            cross[0] = cross[0] + jnp.where(lane == ML_HEADS + h, col, 0.0)
        dfb = cross[0] + _pick_left(_tri(L, True), cross[1], 2) + _pick_left(_tri(L, False) - _eye(L), cross[2], 2)
        dif_ref[:, 0:LANE] = _bf(di_tile + dfb * _sigmoid(-gif))
        dif_ref[:, LANE:SMALL_W] = jnp.zeros((L, SMALL_W - LANE), BF16)

    rev = lambda c: nc - 1 - c
    return pl.pallas_call(
        body, name="mlstm_bwd", grid=(nc,),
        in_specs=[pl.BlockSpec((L, 2048), lambda c: (rev(c), 0)), pl.BlockSpec((L, 2048), lambda c: (rev(c), O_V // 2048)),
                  pl.BlockSpec((L, LANE), lambda c: (rev(c), 0)), pl.BlockSpec((LANE, L), lambda c: (0, rev(c))),
                  pl.BlockSpec((L, 2048), lambda c: (rev(c), 0)), pl.BlockSpec((L, 2048), lambda c: (rev(c), 0)),
                  pl.BlockSpec((1, ML_HEADS, ML_DQK, ML_DV), lambda c: (rev(c), 0, 0, 0)),
                  pl.BlockSpec((1, ML_HEADS, 8, LANE), lambda c: (rev(c), 0, 0, 0)), pl.BlockSpec(memory_space=pl.ANY)],
        out_specs=[pl.BlockSpec((L, 2048), lambda c: (rev(c), 0)), pl.BlockSpec((L, 2048), lambda c: (rev(c), O_V // 2048)),
                   pl.BlockSpec((L, SMALL_W), lambda c: (rev(c), 0))],
        out_shape=[jax.ShapeDtypeStruct((s, 2048), BF16), jax.ShapeDtypeStruct(dproj.shape, dproj.dtype),
                   jax.ShapeDtypeStruct((s, SMALL_W), BF16)],
        input_output_aliases={8: 1},
        scratch_shapes=[pltpu.VMEM((ML_HEADS, ML_DQK, ML_DV), F32), pltpu.VMEM((ML_HEADS, 8, LANE), F32),
                        pltpu.VMEM((L, LANE), F32), pltpu.VMEM((LANE, L), F32)],
        compiler_params=_cparams("arbitrary"))(qk, proj, gates, gt, hout, dh, cst, nm, dproj)


GROUP_W = SSM_HEADS // SSM_GROUPS * SSM_HEADDIM
O_B = SSM_HEADS * SSM_HEADDIM
O_C = O_B + SSM_GROUPS * SSM_STATE


def _head_expand():
    r = jnp.arange(LANE)[:, None]
    c = jnp.arange(SSM_HEADS * SSM_HEADDIM)[None, :] // SSM_HEADDIM
    return (r == c).astype(F32)


def _ssd_gates(dt_ref, dtt_ref, alog_row_ref, alog_col_ref, at_scr):
    L = dt_ref.shape[0]
    dt = _softplus(dt_ref[...])
    acoef = -jnp.exp(alog_row_ref[...])
    a = _pick_left(_tri(L, False), dt * acoef, 3)
    at_scr[...] = _pick_right(_softplus(dtt_ref[...]) * (-jnp.exp(alog_col_ref[...])), _tri(L, True), 3)
    return dt, acoef, a


def _ssd_group(g, xbc_ref, dt, a, e_ref, ax_scr):
    eg = e_ref[:, g * GROUP_W:(g + 1) * GROUP_W]
    ax_scr[...] = _pick_right(a, eg, 3)
    ax = ax_scr[...]
    alx = ax_scr[ax.shape[0] - 1:ax.shape[0], :]
    dtx = _pick_right(dt, eg, 2)
    xg = xbc_ref[:, g * GROUP_W:(g + 1) * GROUP_W]
    bg = xbc_ref[:, O_B + g * SSM_STATE:O_B + (g + 1) * SSM_STATE]
    cg = xbc_ref[:, O_C + g * SSM_STATE:O_C + (g + 1) * SSM_STATE]
    return dict(ax=ax, alx=alx, dtx=dtx, xg=xg, bg=bg, cg=cg, xdt=xg * dtx, gmat=_nt(_bf(cg), _bf(bg)))


def _ssd_decay(hh, a, at_scr):
    L = a.shape[0]
    causal = lax.broadcasted_iota(jnp.int32, (L, L), 0) >= lax.broadcasted_iota(jnp.int32, (L, L), 1)
    return jnp.exp(jnp.where(causal, _lane_col(a, hh) - at_scr[hh:hh + 1, :], NEG))


def _ssd_fwd(xbc, gates, dtt, alog_row, alog_col, dskip_x, expand):
    s = xbc.shape[0]
    L = CHUNK
    nc = s // L
    half = SSM_HEADDIM

    def body(xbc_ref, dt_ref, dtt_ref, ar_ref, ac_ref, dk_ref, e_ref, y_ref, st_ref, st_scr, at_scr, ax_scr):
        @pl.when(pl.program_id(0) == 0)
        def _():
            st_scr[...] = jnp.zeros_like(st_scr)
        dt, _, a = _ssd_gates(dt_ref, dtt_ref, ar_ref, ac_ref, at_scr)
        lane = lax.broadcasted_iota(jnp.int32, (L, LANE), 1)
        for g in range(SSM_GROUPS):
            t = _ssd_group(g, xbc_ref, dt, a, e_ref, ax_scr)
            st = st_scr[g]
            st_ref[0, g] = st
            pairs = []
            for j in range(GROUP_W // LANE):
                xp = _bf(t["xdt"][:, j * LANE:(j + 1) * LANE])
                hh = g * (SSM_HEADS // SSM_GROUPS) + 2 * j
                both = jnp.concatenate([_bf(t["gmat"] * _ssd_decay(hh, a, at_scr)),
                                        _bf(t["gmat"] * _ssd_decay(hh + 1, a, at_scr))], axis=0)
                ys = _nn(both, xp)
                pairs.append(jnp.where(lane < half, ys[0:L], ys[L:2 * L]))
            y = jnp.concatenate(pairs, axis=1) + _nn(_bf(t["cg"]), _bf(st)) * jnp.exp(t["ax"])
            y_ref[:, g * GROUP_W:(g + 1) * GROUP_W] = _bf(y + dk_ref[:, g * GROUP_W:(g + 1) * GROUP_W] * t["xg"])
            wts = jnp.exp(t["alx"] - t["ax"])
            st_scr[g] = jnp.exp(t["alx"]) * st + _tn(_bf(t["bg"]), _bf(t["xdt"] * wts))

    row = lambda w: pl.BlockSpec((1, w), lambda c: (0, 0))
    return pl.pallas_call(
        body, name="ssd_fwd", grid=(nc,),
        in_specs=[pl.BlockSpec((L, 3072), lambda c: (c, 0)), pl.BlockSpec((L, LANE), lambda c: (c, (O_DT - O_IF) // LANE)),
                  pl.BlockSpec((LANE, L), lambda c: (0, c)), row(LANE), pl.BlockSpec((LANE, 1), lambda c: (0, 0)),
                  row(2048), pl.BlockSpec((LANE, 2048), lambda c: (0, 0))],
        out_specs=[pl.BlockSpec((L, 2048), lambda c: (c, 0)),
                   pl.BlockSpec((1, SSM_GROUPS, SSM_STATE, GROUP_W), lambda c: (c, 0, 0, 0))],
        out_shape=[jax.ShapeDtypeStruct((s, 2048), BF16),
                   jax.ShapeDtypeStruct((nc, SSM_GROUPS, SSM_STATE, GROUP_W), F32)],
        scratch_shapes=[pltpu.VMEM((SSM_GROUPS, SSM_STATE, GROUP_W), F32), pltpu.VMEM((LANE, L), F32),
                        pltpu.VMEM((L, GROUP_W), F32)],
        compiler_params=_cparams("arbitrary"))(xbc, gates, dtt, alog_row, alog_col, dskip_x, expand)


def _ssd_bwd(xbc, gates, dtt, alog_row, alog_col, dskip_x, expand, expand_t, dy, states):
    s = xbc.shape[0]
    L = CHUNK
    nc = s // L
    half = SSM_HEADDIM

    def body(xbc_ref, dt_ref, dtt_ref, ar_ref, ac_ref, dk_ref, e_ref, et_ref, dy_ref, st_ref,
             dxbc_ref, ddt_ref, accd_ref, acca_ref, dst_scr, at_scr, ax_scr):
        @pl.when(pl.program_id(0) == 0)
        def _():
            dst_scr[...] = jnp.zeros_like(dst_scr)
            accd_ref[...] = jnp.zeros_like(accd_ref)
            acca_ref[...] = jnp.zeros_like(acca_ref)
        dt, acoef, a = _ssd_gates(dt_ref, dtt_ref, ar_ref, ac_ref, at_scr)
        lane = lax.broadcasted_iota(jnp.int32, (L, LANE), 1)
        low = lane < half
        last = lax.broadcasted_iota(jnp.int32, (L, 1), 0) == L - 1
        cross = [jnp.zeros((L, LANE), F32)] * 3
        ddt_tile = jnp.zeros((L, LANE), F32)
        for g in range(SSM_GROUPS):
            t = _ssd_group(g, xbc_ref, dt, a, e_ref, ax_scr)
            xg, bg, cg, xdt, gmat = t["xg"], t["bg"], t["cg"], t["xdt"], t["gmat"]
            st, dst = st_ref[0, g], dst_scr[g]
            dyg = dy_ref[:, g * GROUP_W:(g + 1) * GROUP_W].astype(F32)
            ea, eal = jnp.exp(t["ax"]), jnp.exp(t["alx"])
            wts = jnp.exp(t["alx"] - t["ax"])
            dyi = dyg * ea
            y_inter = _nn(_bf(cg), _bf(st)) * ea
            dc = _nt(_bf(dyi), _bf(st))
            d_xdt_state = _nn(_bf(bg), _bf(dst)) * wts
            db = _nt(_bf(xdt * wts), _bf(dst))
            dst_scr[g] = eal * dst + _tn(_bf(cg), _bf(dyi))
            dg = jnp.zeros((L, L), F32)
            dx_pairs, dlogw = [], []
            for j in range(GROUP_W // LANE):
                xp = _bf(xdt[:, j * LANE:(j + 1) * LANE])
                dyp = dyg[:, j * LANE:(j + 1) * LANE]
                hh = g * (SSM_HEADS // SSM_GROUPS) + 2 * j
                decs = [_ssd_decay(hh, a, at_scr), _ssd_decay(hh + 1, a, at_scr)]
                ws = [gmat * decs[0], gmat * decs[1]]
                dxs = _tn(_bf(jnp.concatenate(ws, axis=1)), _bf(dyp))
                dws = _nt(_bf(jnp.concatenate([jnp.where(low, dyp, 0.0), jnp.where(low, 0.0, dyp)], axis=0)), xp)
                dw0, dw1 = dws[0:L], dws[L:2 * L]
                dg = dg + dw0 * decs[0] + dw1 * decs[1]
                dlogw += [dw0 * ws[0], dw1 * ws[1]]
                dx_pairs.append(jnp.where(low, dxs[0:L], dxs[L:2 * L]))
            for b, col in enumerate(_crossing(jnp.concatenate(dlogw, axis=1))):
                cross[0] = cross[0] + jnp.where(lane == g * (SSM_HEADS // SSM_GROUPS) + b, col, 0.0)
            d_xdt = d_xdt_state + jnp.concatenate(dx_pairs, axis=1)
            dc = dc + _nn(_bf(dg), _bf(bg))
            db = db + _tn(_bf(dg), _bf(cg))
            etg = et_ref[g * GROUP_W:(g + 1) * GROUP_W, :]
            carried = jnp.sum(dst * st, axis=0, keepdims=True) * eal
            cross[1] = cross[1] + _pick_right(dyg * y_inter + jnp.where(last, carried, 0.0), etg, 2)
            cross[2] = cross[2] + _pick_right(xdt * d_xdt_state, etg, 2)
            ddt_tile = ddt_tile + _pick_right(d_xdt * xg, etg, 2)
            dxbc_ref[:, g * GROUP_W:(g + 1) * GROUP_W] = _bf(d_xdt * t["dtx"] + dk_ref[:, g * GROUP_W:(g + 1) * GROUP_W] * dyg)
            dxbc_ref[:, O_B + g * SSM_STATE:O_B + (g + 1) * SSM_STATE] = _bf(db)
            dxbc_ref[:, O_C + g * SSM_STATE:O_C + (g + 1) * SSM_STATE] = _bf(dc)
            accd_ref[0:1, g * GROUP_W:(g + 1) * GROUP_W] += jnp.sum(dyg * xg, axis=0, keepdims=True)
        d_da = cross[0] + _pick_left(_tri(L, True), cross[1], 2) + _pick_left(_tri(L, False) - _eye(L), cross[2], 2)
        acca_ref[0:1, :] += jnp.sum(d_da * dt, axis=0, keepdims=True)
        ddt_ref[:, 0:LANE] = _bf((ddt_tile + d_da * acoef) * _sigmoid(dt_ref[...]))
        ddt_ref[:, LANE:SMALL_W] = jnp.zeros((L, SMALL_W - LANE), BF16)

    rev = lambda c: nc - 1 - c
    row = lambda w: pl.BlockSpec((1, w), lambda c: (0, 0))
    return pl.pallas_call(
        body, name="ssd_bwd", grid=(nc,),
        in_specs=[pl.BlockSpec((L, 3072), lambda c: (rev(c), 0)), pl.BlockSpec((L, LANE), lambda c: (rev(c), (O_DT - O_IF) // LANE)),
                  pl.BlockSpec((LANE, L), lambda c: (0, rev(c))), row(LANE), pl.BlockSpec((LANE, 1), lambda c: (0, 0)),
                  row(2048), pl.BlockSpec((LANE, 2048), lambda c: (0, 0)), pl.BlockSpec((2048, LANE), lambda c: (0, 0)),
                  pl.BlockSpec((L, 2048), lambda c: (rev(c), 0)),
                  pl.BlockSpec((1, SSM_GROUPS, SSM_STATE, GROUP_W), lambda c: (rev(c), 0, 0, 0))],
        out_specs=[pl.BlockSpec((L, 3072), lambda c: (rev(c), 0)), pl.BlockSpec((L, SMALL_W), lambda c: (rev(c), 0)),
                   pl.BlockSpec((8, 2048), lambda c: (0, 0)), pl.BlockSpec((8, LANE), lambda c: (0, 0))],
        out_shape=[jax.ShapeDtypeStruct((s, 3072), BF16), jax.ShapeDtypeStruct((s, SMALL_W), BF16),
                   jax.ShapeDtypeStruct((8, 2048), F32), jax.ShapeDtypeStruct((8, LANE), F32)],
        scratch_shapes=[pltpu.VMEM((SSM_GROUPS, SSM_STATE, GROUP_W), F32),
                        pltpu.VMEM((LANE, L), F32), pltpu.VMEM((L, GROUP_W), F32)],
        compiler_params=_cparams("arbitrary"))(xbc, gates, dtt, alog_row, alog_col, dskip_x, expand, expand_t, dy, states)


def _group_norm(v, width):
    outs, rs = [], []
    for k in range(v.shape[1] // width):
        blk = v[:, k * width:(k + 1) * width]
        r = lax.rsqrt(jnp.mean(blk * blk, axis=1, keepdims=True) + EPS)
        outs.append(blk * r)
        rs.append(jnp.broadcast_to(r, blk.shape))
    return jnp.concatenate(outs, axis=1), jnp.concatenate(rs, axis=1)


def _group_mean(v, width):
    return jnp.concatenate([jnp.broadcast_to(jnp.mean(v[:, k * width:(k + 1) * width], axis=1, keepdims=True),
                                             (v.shape[0], width)) for k in range(v.shape[1] // width)], axis=1)


def _post_fwd(hm, yssd, proj, ml_norm_w, ssm_norm_w, ts):
    s = hm.shape[0]

    def body(h_ref, ys_ref, o_ref, zm_ref, zs_ref, wm_ref, ws_ref, ym_ref, yso_ref):
        hn, _ = _group_norm(h_ref[...].astype(F32), ML_DV)
        ym_ref[...] = _bf(_sigmoid(o_ref[...].astype(F32)) * hn * wm_ref[...] * _silu(zm_ref[...].astype(F32)))
        pn, _ = _group_norm(ys_ref[...].astype(F32) * _silu(zs_ref[...].astype(F32)), GROUP_W)
        yso_ref[...] = _bf(pn * ws_ref[...])

    tile = pl.BlockSpec((ts, 2048), lambda i: (i, 0))
    col = lambda off: pl.BlockSpec((ts, 2048), lambda i: (i, off // 2048))
    row = pl.BlockSpec((1, 2048), lambda i: (0, 0))
    return pl.pallas_call(
        body, name="post_fwd", grid=(s // ts,),
        in_specs=[tile, tile, col(O_O), col(O_ZM), col(O_ZS), row, row],
        out_specs=[tile, tile],
        out_shape=[jax.ShapeDtypeStruct((s, 2048), BF16)] * 2,
        compiler_params=_cparams("parallel"))(hm, yssd, proj, proj, proj, ml_norm_w, ssm_norm_w)


def _post_bwd(dym, dys, hm, yssd, proj, ml_norm_w, ssm_norm_w, dproj, ts):
    s = hm.shape[0]

    def body(dym_ref, dys_ref, h_ref, ys_ref, o_ref, zm_ref, zs_ref, wm_ref, ws_ref, _,
             dh_ref, dyssd_ref, dp_ref, acc_ref):
        @pl.when(pl.program_id(0) == 0)
        def _():
            acc_ref[...] = jnp.zeros_like(acc_ref)
        hn, r = _group_norm(h_ref[...].astype(F32), ML_DV)
        so, zm, wm, d_ym = _sigmoid(o_ref[...].astype(F32)), zm_ref[...].astype(F32), wm_ref[...], dym_ref[...].astype(F32)
        sz = _silu(zm)
        hnw = hn * wm
        dp_ref[:, O_O:O_O + 2048] = _bf(d_ym * hnw * sz * so * (1.0 - so))
        dp_ref[:, O_ZM:O_ZM + 2048] = _bf(d_ym * so * hnw * _dsilu(zm))
        dhnw = d_ym * so * sz
        acc_ref[0:1, :] += jnp.sum(dhnw * hn, axis=0, keepdims=True)
        dhn = dhnw * wm
        dh_ref[...] = _bf(r * (dhn - hn * _group_mean(dhn * hn, ML_DV)))
        ysv, zs, d_ys = ys_ref[...].astype(F32), zs_ref[...].astype(F32), dys_ref[...].astype(F32)
        szs = _silu(zs)
        pn, r2 = _group_norm(ysv * szs, GROUP_W)
        acc_ref[1:2, :] += jnp.sum(d_ys * pn, axis=0, keepdims=True)
        dpn = d_ys * ws_ref[...]
        dp = r2 * (dpn - pn * _group_mean(dpn * pn, GROUP_W))
        dyssd_ref[...] = _bf(dp * szs)
        dp_ref[:, O_ZS:O_ZS + 2048] = _bf(dp * ysv * _dsilu(zs))

    tile = pl.BlockSpec((ts, 2048), lambda i: (i, 0))
    col = lambda off: pl.BlockSpec((ts, 2048), lambda i: (i, off // 2048))
    row = pl.BlockSpec((1, 2048), lambda i: (0, 0))
    sds = lambda dt: jax.ShapeDtypeStruct((s, 2048), dt)
    return pl.pallas_call(
        body, name="post_bwd", grid=(s // ts,),
        in_specs=[tile, tile, tile, tile, col(O_O), col(O_ZM), col(O_ZS), row, row, pl.BlockSpec(memory_space=pl.ANY)],
        out_specs=[tile, tile, pl.BlockSpec((ts, O_MG), lambda i: (i, 0)), pl.BlockSpec((8, 2048), lambda i: (0, 0))],
        out_shape=[sds(BF16), sds(BF16), jax.ShapeDtypeStruct(dproj.shape, dproj.dtype), jax.ShapeDtypeStruct((8, 2048), F32)],
        input_output_aliases={9: 2},
        compiler_params=_cparams("arbitrary"))(dym, dys, hm, yssd, proj, proj, proj, ml_norm_w, ssm_norm_w, dproj)


def _merge(x, ym, ys, proj, target, gate, final_w, wpm, wps, wo, ts):
    wpm_t, wps_t, wo_t = wpm.T, wps.T, wo.T
    s, d = x.shape

    def body(x_ref, ym_ref, ys_ref, mg_ref, t_ref, gate_ref, fw_ref, wpm_ref, wps_ref, wo_ref, wpmt_ref, wpst_ref, wot_ref,
             dres_ref, mer_ref, dmo_ref, dpm_ref, dps_ref, dym_ref, dys_ref, dmg_ref, acc_ref):
        @pl.when(pl.program_id(0) == 0)
        def _():
            acc_ref[...] = jnp.zeros_like(acc_ref)
        gm, gs = _sigmoid(mg_ref[:, 0:d].astype(F32)), _sigmoid(mg_ref[:, d:2 * d].astype(F32))
        pm = _nn(ym_ref[...], wpm_ref[...])
        ps = _nn(ys_ref[...], wps_ref[...])
        merged = _bf(gm * pm + gs * ps)
        mer_ref[...] = merged
        mo = _nn(merged, wo_ref[...])
        gate, fw = gate_ref[...], fw_ref[...]
        out = x_ref[...] + gate * mo
        r = lax.rsqrt(jnp.mean(out * out, axis=1, keepdims=True) + EPS)
        on = out * r
        diff = on * fw - t_ref[...]
        acc_ref[0:1, :] += jnp.sum(0.5 * jnp.sum(diff * diff, axis=1, keepdims=True) / d, axis=0, keepdims=True)
        dyv = diff * (1.0 / d)
        acc_ref[1:2, :] += jnp.sum(dyv * on, axis=0, keepdims=True)
        don = dyv * fw
        dout = r * (don - on * jnp.mean(don * on, axis=1, keepdims=True))
        dres_ref[...] = dout
        acc_ref[2:3, :] += jnp.sum(dout * mo, axis=0, keepdims=True)
        dmo = _bf(dout * gate)
        dmo_ref[...] = dmo
        dmer = _nn(dmo, wot_ref[...])
        dpm, dps = _bf(dmer * gm), _bf(dmer * gs)
        dpm_ref[...] = dpm
        dps_ref[...] = dps
        dmg_ref[:, 0:d] = _bf(dmer * pm * gm * (1.0 - gm))
        dmg_ref[:, d:2 * d] = _bf(dmer * ps * gs * (1.0 - gs))
        dym_ref[...] = _bf(_nn(dpm, wpmt_ref[...]))
        dys_ref[...] = _bf(_nn(dps, wpst_ref[...]))

    t1 = pl.BlockSpec((ts, d), lambda i: (i, 0))
    t2 = pl.BlockSpec((ts, 2 * d), lambda i: (i, 0))
    row = pl.BlockSpec((1, d), lambda i: (0, 0))
    whole = pl.BlockSpec(memory_space=pltpu.VMEM)
    sd = lambda w, dt: jax.ShapeDtypeStruct((s, w), dt)
    return pl.pallas_call(
        body, name="merge_fwd_bwd", grid=(s // ts,),
        in_specs=[t1, t2, t2, pl.BlockSpec((ts, 2 * d), lambda i: (i, O_MG // (2 * d))), t1, row, row] + [whole] * 6,
        out_specs=[t1, t1, t1, t1, t1, t2, t2, pl.BlockSpec((ts, 2 * d), lambda i: (i, O_MG // (2 * d))),
                   pl.BlockSpec((8, d), lambda i: (0, 0))],
        out_shape=[sd(d, F32), sd(d, BF16), sd(d, BF16), sd(d, BF16), sd(d, BF16), sd(2 * d, BF16), sd(2 * d, BF16),
                   sd(NP, BF16), jax.ShapeDtypeStruct((8, d), F32)],
        compiler_params=_cparams("arbitrary", vmem=MERGE_VMEM))(x, ym, ys, proj, target, gate, final_w, wpm, wps, wo, wpm_t, wps_t, wo_t)


def _adamw(w, g, m, v, tr):
    if w.ndim == 2 and w.shape[0] % 8:
        tile, steps = pl.BlockSpec((w.shape[0], tr), lambda i: (0, i)), w.shape[1] // tr
    else:
        lead = (None,) * (w.ndim - 2)
        tile, steps = pl.BlockSpec(lead + (tr, w.shape[-1]), lambda i: (0,) * len(lead) + (i, 0)), w.shape[-2] // tr

    def body(w_ref, g_ref, m_ref, v_ref, d_ref, nm_ref, nv_ref):
        gv = g_ref[...]
        m2 = ADAM_B1 * m_ref[...] + (1.0 - ADAM_B1) * gv
        v2 = ADAM_B2 * v_ref[...] + (1.0 - ADAM_B2) * (gv * gv)
        m_hat = m2 / (1.0 - ADAM_B1 ** ADAM_STEP)
        v_hat = v2 / (1.0 - ADAM_B2 ** ADAM_STEP)
        d_ref[...] = -ADAM_LR * (m_hat / (jnp.sqrt(v_hat) + ADAM_EPS) + ADAM_WD * w_ref[...])
        nm_ref[...] = m2
        nv_ref[...] = v2

    return pl.pallas_call(
        body, name="adamw", grid=(steps,), in_specs=[tile] * 4, out_specs=[tile] * 3,
        out_shape=[jax.ShapeDtypeStruct(w.shape, F32)] * 3,
        compiler_params=_cparams("parallel"))(w, g.reshape(w.shape), m, v)


def _sum_parts(own, parts, tr, dtype=F32, slot=None):
    p, rows, cols = parts.shape

    def body(*refs):
        p_ref, o_ref = refs[-2], refs[-1]
        first = None if own is None else refs[-3]
        acc = p_ref[0].astype(F32) if first is None else first[...].astype(F32) + p_ref[0].astype(F32)
        for i in range(1, p):
            acc = acc + p_ref[i].astype(F32)
        o_ref[...] = acc.astype(dtype)

    args = ([] if own is None else [own]) + [parts]
    if slot is None:
        tile = pl.BlockSpec((tr, cols), lambda i: (i, 0))
        ins = ([] if own is None else [tile]) + [pl.BlockSpec((p, tr, cols), lambda i: (0, i, 0))]
        return pl.pallas_call(
            body, name="sum_parts", grid=(rows // tr,), in_specs=ins, out_specs=tile,
            out_shape=jax.ShapeDtypeStruct((rows, cols), dtype), compiler_params=_cparams("parallel"))(*args)
    tile = pl.BlockSpec((tr, cols), lambda i, s: (i, 0))
    ins = ([] if own is None else [tile]) + [pl.BlockSpec((p, tr, cols), lambda i, s: (0, i, 0))]
    return pl.pallas_call(
        body, name="sum_parts_half", out_shape=jax.ShapeDtypeStruct((2, rows, cols), dtype),
        grid_spec=pltpu.PrefetchScalarGridSpec(
            num_scalar_prefetch=1, grid=(rows // tr,), in_specs=ins,
            out_specs=pl.BlockSpec((None, tr, cols), lambda i, s: (s[0], i, 0))),
        compiler_params=_cparams("parallel"))(jnp.reshape(slot, (1,)).astype(jnp.int32), *args)


def _position():
    return lax.axis_index("x"), lax.axis_index("y"), lax.axis_index("c")


def _flip(pos, k):
    return tuple(1 - p if (k >> s) & 1 else p for p, s in zip(pos, (2, 1, 0)))


def _allgather8(block):
    rows, cols = block.shape

    def body(x_ref, o_ref, send_sems, recv_sems, local_sem):
        pos = _position()
        me = 4 * pos[0] + 2 * pos[1] + pos[2]
        mine = pltpu.make_async_copy(x_ref, o_ref.at[me], local_sem)
        mine.start()
        copies = [pltpu.make_async_remote_copy(src_ref=x_ref, dst_ref=o_ref.at[me], send_sem=send_sems.at[k - 1],
                                               recv_sem=recv_sems.at[k - 1], device_id=_flip(pos, k), device_id_type=MESH)
                  for k in range(1, N_DEV)]
        for cp in copies:
            cp.start()
        for cp in copies:
            cp.wait()
        mine.wait()

    vmem = pl.BlockSpec(memory_space=pltpu.VMEM)
    return pl.pallas_call(
        body, name="allgather8", in_specs=[vmem], out_specs=vmem,
        out_shape=jax.ShapeDtypeStruct((N_DEV, rows, cols), block.dtype),
        scratch_shapes=[pltpu.SemaphoreType.DMA((N_DEV - 1,)), pltpu.SemaphoreType.DMA((N_DEV - 1,)),
                        pltpu.SemaphoreType.DMA],
        compiler_params=pltpu.CompilerParams(vmem_limit_bytes=VMEM_LIMIT))(block)


COPY_BYTES = 1 << 20


def _row_chunks(rows, row_bytes):
    n = max(1, min(rows // 16, -(-rows * row_bytes // COPY_BYTES)))
    while rows % (16 * n):
        n -= 1
    return [(i * (rows // n), rows // n) for i in range(n)]


def _split_start(name, make_copies, n_copies, sources, lands):
    n, m = len(sources), len(lands)

    def body(*refs):
        for cp in make_copies(_position(), refs[:n], refs[n:n + m], refs[n + m], refs[n + m + 1]):
            cp.start()
        refs[-1][...] = jnp.zeros((8, LANE), F32)

    hbm = pl.BlockSpec(memory_space=pltpu.HBM)
    sem = pl.BlockSpec(memory_space=pltpu.SEMAPHORE)
    operands = [pltpu.with_memory_space_constraint(t, pltpu.HBM) for t in list(sources) + list(lands)]
    out = pl.pallas_call(
        body, name=name, in_specs=[hbm] * (n + m),
        out_specs=[sem, sem] + [hbm] * (n + m) + [pl.BlockSpec(memory_space=pltpu.VMEM)],
        out_shape=[pltpu.SemaphoreType.DMA((n_copies,)), pltpu.SemaphoreType.DMA((n_copies,))]
        + [pltpu.HBM(t.shape, t.dtype) for t in operands] + [jax.ShapeDtypeStruct((8, LANE), F32)],
        input_output_aliases={i: 2 + i for i in range(n + m)},
        compiler_params=pltpu.CompilerParams(has_side_effects=pltpu.SideEffectType.DATAFLOW_SIDE_EFFECTING))(*operands)
    return out[0], out[1], out[2:2 + n], out[2 + n:2 + n + m], out[-1]


def _split_wait(name, make_copies, send_sems, recv_sems, sources, lands, after):
    n, m = len(sources), len(lands)

    def body(*refs):
        for cp in make_copies(_position(), refs[:n], refs[n:n + m], refs[n + m], refs[n + m + 1]):
            cp.wait_send()
            cp.wait_recv()

    hbm = pl.BlockSpec(memory_space=pltpu.HBM)
    sem = pl.BlockSpec(memory_space=pltpu.SEMAPHORE)
    out = pl.pallas_call(
        body, name=name, in_specs=[hbm] * (n + m) + [sem, sem, pl.BlockSpec(memory_space=pl.ANY)],
        out_specs=[hbm] * (n + m), out_shape=[pltpu.HBM(t.shape, t.dtype) for t in list(sources) + list(lands)],
        input_output_aliases={i: i for i in range(n + m)},
        compiler_params=pltpu.CompilerParams(has_side_effects=pltpu.SideEffectType.DATAFLOW_SIDE_EFFECTING))(
            *sources, *lands, send_sems, recv_sems, after)
    return out[:n], out[n:]


def _gather_copies(pos, halves, lands, send_sems, recv_sems):
    chip, core = 2 * pos[0] + pos[1], pos[2]
    copies = []
    for a in range(len(halves)):
        for k in range(1, N_CHIPS):
            for r0, nr in _row_chunks(halves[a].shape[1], halves[a].shape[2] * halves[a].dtype.itemsize):
                i = len(copies)
                copies.append(pltpu.make_async_remote_copy(
                    src_ref=halves[a].at[core, pl.ds(r0, nr)], dst_ref=lands[a].at[chip, core, pl.ds(r0, nr)],
                    send_sem=send_sems.at[i], recv_sem=recv_sems.at[i], device_id=_flip(pos, 2 * k), device_id_type=MESH))
    return copies


def _gather_pieces(halves):
    return (N_CHIPS - 1) * sum(len(_row_chunks(a.shape[1], a.shape[2] * a.dtype.itemsize)) for a in halves)


def _pair_forward(lands):
    n = len(lands)

    def plan(pos, ins, outs):
        remote = []
        for a in range(n):
            for k in range(1, N_CHIPS):
                there = _flip(pos, 2 * k)
                for r0, nr in _row_chunks(lands[a].shape[2], lands[a].shape[3] * lands[a].dtype.itemsize):
                    slot = (2 * there[0] + there[1], pos[2], pl.ds(r0, nr))
                    remote.append((ins[a].at[slot], outs[a].at[slot], _flip(pos, 1)))
        return remote, []

    return _exchange("pair_forward", lands, [jax.ShapeDtypeStruct(t.shape, t.dtype) for t in lands], plan,
                     _gather_pieces([jax.ShapeDtypeStruct(t.shape[1:], t.dtype) for t in lands]), 0, in_place=True)


def _exchange(name, arrays, out_shapes, plan, n_remote, n_local, in_place=False):
    n, m = len(arrays), len(out_shapes)

    def body(*refs):
        send_sems, recv_sems, local_sems = refs[n + m:]
        remote, local = plan(_position(), refs[:n], refs[n:n + m])
        assert (len(remote), len(local)) == (n_remote, n_local)
        copies = [pltpu.make_async_copy(src, dst, local_sems.at[i]) for i, (src, dst) in enumerate(local)]
        copies += [pltpu.make_async_remote_copy(src_ref=src, dst_ref=dst, send_sem=send_sems.at[i], recv_sem=recv_sems.at[i],
                                                device_id=dev, device_id_type=MESH)
                   for i, (src, dst, dev) in enumerate(remote)]
        for cp in copies:
            cp.start()
        for cp in copies:
            cp.wait()

    hbm = pl.BlockSpec(memory_space=pl.ANY)
    return pl.pallas_call(
        body, name=name, in_specs=[hbm] * n, out_specs=[hbm] * m, out_shape=out_shapes,
        input_output_aliases={i: i for i in range(n)} if in_place else {},
        scratch_shapes=[pltpu.SemaphoreType.DMA((n_remote,)), pltpu.SemaphoreType.DMA((n_remote,)),
                        pltpu.SemaphoreType.DMA((max(n_local, 1),))],
        compiler_params=pltpu.CompilerParams(has_side_effects=True))(*arrays)


def _pair_send(slabs):
    n = len(slabs)
    pieces = [_row_chunks(g.shape[2], g.shape[3] * g.dtype.itemsize) for g in slabs]

    def plan(pos, ins, outs):
        return [(ins[a].at[j, 1 - pos[2], pl.ds(r0, nr)], outs[a].at[j, pl.ds(r0, nr)], _flip(pos, 1))
                for a in range(n) for j in range(N_CHIPS) for r0, nr in pieces[a]], []

    return _exchange("pair_send", slabs, [jax.ShapeDtypeStruct((N_CHIPS,) + g.shape[2:], g.dtype) for g in slabs], plan,
                     N_CHIPS * sum(len(p) for p in pieces), 0)


def _chip_scatter_copies(pos, sums, lands, send_sems, recv_sems):
    copies = []
    for a in range(len(sums)):
        for k in range(1, N_CHIPS):
            to = _flip(pos, 2 * k)
            for r0, nr in _row_chunks(sums[a].shape[1], sums[a].shape[2] * sums[a].dtype.itemsize):
                i = len(copies)
                copies.append(pltpu.make_async_remote_copy(
                    src_ref=sums[a].at[2 * to[0] + to[1], pl.ds(r0, nr)], dst_ref=lands[a].at[k - 1, pl.ds(r0, nr)],
                    send_sem=send_sems.at[i], recv_sem=recv_sems.at[i], device_id=to, device_id_type=MESH))
    return copies


def _chip_scatter_start(sums):
    n_copies = (N_CHIPS - 1) * sum(len(_row_chunks(g.shape[1], g.shape[2] * g.dtype.itemsize)) for g in sums)
    lands = [lax.empty((N_CHIPS - 1,) + g.shape[1:], g.dtype) for g in sums]
    return _split_start("chip_scatter_start", _chip_scatter_copies, n_copies, sums, lands)


def _chip_scatter_wait(send_sems, recv_sems, sums, lands, after):
    return _split_wait("chip_scatter_wait", _chip_scatter_copies, send_sems, recv_sems, sums, lands, after)


def _pair_exchange(pairs):
    n = len(pairs)
    pieces = [_row_chunks(h.shape[1], h.shape[2] * h.dtype.itemsize) for h in pairs]

    def plan(pos, ins, outs):
        return [(ins[a].at[pos[2], pl.ds(r0, nr)], outs[a].at[pos[2], pl.ds(r0, nr)], _flip(pos, 1))
                for a in range(n) for r0, nr in pieces[a]], []

    return _exchange("pair_exchange", pairs, [jax.ShapeDtypeStruct(h.shape, h.dtype) for h in pairs], plan,
                     sum(len(p) for p in pieces), 0, in_place=True)


def _pack(arrays):
    flat = jnp.concatenate([a.reshape(-1).astype(F32) for a in arrays])
    size = -(-flat.shape[0] // (8 * LANE)) * (8 * LANE)
    return jnp.pad(flat, (0, size - flat.shape[0])).reshape(size // LANE, LANE)


def _unpack(buf, shapes):
    flat = buf.reshape(-1)
    out, off = [], 0
    for shp in shapes:
        n = math.prod(shp)
        out.append(flat[off:off + n].reshape(shp))
        off += n
    return out


def _unpack_rows(bufs, shapes):
    flat = bufs.reshape(bufs.shape[0], -1)
    out, off = [], 0
    for shp in shapes:
        n = math.prod(shp)
        out.append(flat[:, off:off + n].reshape((bufs.shape[0],) + shp))
        off += n
    return out


def _taps8(w):
    return jnp.pad(w, ((0, 8 - CONV_K), (0, 0)))


def _local_step(xs, tgt, scale, shift, gate, norm_w, w_in_p, b_in_p, ml_conv_w, ml_conv_b, ml_norm_w, ssm_conv_w,
                ssm_conv_b, ssm_a_log, ssm_d, ssm_norm_w, wpm, wps, wo, final_w, start_exchange=None, late_weights=None,
                u=None):
    s = xs.shape[0]
    ts, tm, tw, tn = min(ROWS_ELEMENTWISE, s), min(ROWS_MATMUL, s), min(ROWS_WIDE_MATMUL, s), COLS_MATMUL
    if u is None:
        u = _prenorm_fwd(xs, norm_w, scale, shift, ts)
    u, u_t = u
    proj = _matmul_bias(u, w_in_p, b_in_p, tw, tn, 0, O_IF, BF16)
    gates = _matmul_bias(u, w_in_p, b_in_p, tw, tn, O_IF, NP - O_IF, F32)
    mlw8, ssw8 = _taps8(ml_conv_w), _taps8(ssm_conv_w)
    tc = min(ROWS_CONV, s)
    qk, qk_dact = _conv_fwd(proj, O_QK, 2048, mlw8, ml_conv_b, tc)
    xbc, xbc_dact = _conv_fwd(proj, O_XBC, 3072, ssw8, ssm_conv_b, tc)
    gt = gates[:, :LANE].T
    dtt = gates[:, O_DT - O_IF:O_DT - O_IF + LANE].T
    hm, cst, nm = _mlstm_fwd(qk, proj, gates, gt)
    alog_row = jnp.pad(ssm_a_log, ((0, 0), (0, LANE - SSM_HEADS)))
    alog_col = alog_row.reshape(LANE, 1)
    dskip_x = jnp.repeat(ssm_d[0], SSM_HEADDIM)[None]
    expand = _head_expand()
    yssd, sst = _ssd_fwd(xbc, gates, dtt, alog_row, alog_col, dskip_x, expand)
    tp = min(ROWS_POST, s)
    ym, ys = _post_fwd(hm, yssd, proj, ml_norm_w, ssm_norm_w, min(ROWS_POST_FWD, s))
    if late_weights is not None:
        wpm, wps, wo = late_weights(ym)
    dxres, merged, dmo, dpm, dps, dym, dys, dproj, acc_m = _merge(xs, ym, ys, proj, tgt, gate, final_w, wpm, wps, wo,
                                                                  min(ROWS_MERGE, s))
    dh, dyssd, dproj, acc_p = _post_bwd(dym, dys, hm, yssd, proj, ml_norm_w, ssm_norm_w, dproj, tp)
    dqk, dproj, dif = _mlstm_bwd(qk, proj, gates, gt, hm, dh, cst, nm, dproj)
    dxbc, ddt, accd, acca = _ssd_bwd(xbc, gates, dtt, alog_row, alog_col, dskip_x, expand, expand.T, dyssd, sst)
    dproj, acc_cq = _conv_bwd(proj, O_QK, 2048, mlw8, qk_dact, dqk, dproj, tc)
    dproj, acc_cx = _conv_bwd(proj, O_XBC, 3072, ssw8, xbc_dact, dxbc, dproj, tc)
    dproj = dproj.at[:, O_IF:O_IF + SMALL_W].set(dif).at[:, O_DT:O_DT + SMALL_W].set(ddt)
    gw_in_p, gb_in_p = _matmul_tn(u_t, dproj, tw, tn, with_colsum=True, a_is_transposed=True, dtype=BF16)
    g_wpm = _matmul_tn(ym, dpm, tm, tn, dtype=BF16)
    g_wps = _matmul_tn(ys, dps, tm, tn, dtype=BF16)
    g_wo = _matmul_tn(merged, dmo, tm, tn, dtype=BF16)
    token, in_flight = (None, None) if start_exchange is None else start_exchange(gw_in_p, g_wpm, g_wps, g_wo)
    du = _matmul_nt(dproj, w_in_p, tm, tn, after=token)
    grad_x, acc_n = _prenorm_bwd(du, xs, dxres, norm_w, scale, ts)
    a_coef = -jnp.exp(ssm_a_log[0])
    small = dict(
        mod=jnp.concatenate([acc_n[2], acc_n[1], acc_m[2]]), norm_w=acc_n[0], b_in=_unpad_cols(gb_in_p[0]),
        ml_conv_w=acc_cq[0:CONV_K], ml_conv_b=acc_cq[CONV_K], ml_norm_w=acc_p[0], ssm_conv_w=acc_cx[0:CONV_K],
        ssm_conv_b=acc_cx[CONV_K], ssm_a_log=acca[0, :SSM_HEADS] * a_coef,
        ssm_d=accd[0].reshape(SSM_HEADS, SSM_HEADDIM).sum(axis=1), ssm_norm_w=acc_p[1], final_w=acc_m[1], loss=acc_m[0, 0:1])
    return grad_x, small, gw_in_p, g_wpm, g_wps, g_wo, in_flight


WEIGHTS = ("norm_w", "ada_w", "ada_b", "w_in", "b_in", "ml_conv_w", "ml_conv_b", "ml_norm_w", "ssm_conv_w", "ssm_conv_b",
           "ssm_a_log", "ssm_d", "ssm_norm_w", "w_proj_m", "w_proj_s", "w_out", "final_w")
LARGE = ("ada_w", "w_in", "w_proj_m", "w_proj_s", "w_out")
SMALL_SUMS = (("mod", (3 * D_MODEL,)), ("norm_w", (D_MODEL,)), ("b_in", (IN_WIDTH,)), ("ml_conv_w", (CONV_K, 2048)),
              ("ml_conv_b", (2048,)), ("ml_norm_w", (2048,)), ("ssm_conv_w", (CONV_K, 3072)), ("ssm_conv_b", (3072,)),
              ("ssm_a_log", (SSM_HEADS,)), ("ssm_d", (SSM_HEADS,)), ("ssm_norm_w", (2048,)), ("final_w", (D_MODEL,)),
              ("loss", (1,)))


def kernel(x, c, norm_w, ada_w, ada_b, w_in, b_in, ml_conv_w, ml_conv_b, ml_norm_w, ssm_conv_w, ssm_conv_b, ssm_a_log, ssm_d, ssm_norm_w, w_proj_m, w_proj_s, w_out, final_w, loss_target, m_norm_w, m_ada_w, m_ada_b, m_w_in, m_b_in, m_ml_conv_w, m_ml_conv_b, m_ml_norm_w, m_ssm_conv_w, m_ssm_conv_b, m_ssm_a_log, m_ssm_d, m_ssm_norm_w, m_w_proj_m, m_w_proj_s, m_w_out, m_final_w, v_norm_w, v_ada_w, v_ada_b, v_w_in, v_b_in, v_ml_conv_w, v_ml_conv_b, v_ml_norm_w, v_ssm_conv_w, v_ssm_conv_b, v_ssm_a_log, v_ssm_d, v_ssm_norm_w, v_w_proj_m, v_w_proj_s, v_w_out, v_final_w):
    w = dict(norm_w=norm_w, ada_w=ada_w, ada_b=ada_b, w_in=w_in, b_in=b_in, ml_conv_w=ml_conv_w, ml_conv_b=ml_conv_b,
             ml_norm_w=ml_norm_w, ssm_conv_w=ssm_conv_w, ssm_conv_b=ssm_conv_b, ssm_a_log=ssm_a_log, ssm_d=ssm_d,
             ssm_norm_w=ssm_norm_w, w_proj_m=w_proj_m, w_proj_s=w_proj_s, w_out=w_out, final_w=final_w)
    m = dict(zip(WEIGHTS, (m_norm_w, m_ada_w, m_ada_b, m_w_in, m_b_in, m_ml_conv_w, m_ml_conv_b, m_ml_norm_w, m_ssm_conv_w,
                           m_ssm_conv_b, m_ssm_a_log, m_ssm_d, m_ssm_norm_w, m_w_proj_m, m_w_proj_s, m_w_out, m_final_w)))
    v = dict(zip(WEIGHTS, (v_norm_w, v_ada_w, v_ada_b, v_w_in, v_b_in, v_ml_conv_w, v_ml_conv_b, v_ml_norm_w, v_ssm_conv_w,
                           v_ssm_conv_b, v_ssm_a_log, v_ssm_d, v_ssm_norm_w, v_w_proj_m, v_w_proj_s, v_w_out, v_final_w)))
    pos = _position()
    chip = 2 * pos[0] + pos[1]
    dev = 2 * chip + pos[2]
    mlw_cols, ssw_cols, ada_cols = ml_conv_w.shape[2], ssm_conv_w.shape[2], ada_w.shape[2]

    g0 = _allgather8(_pack([c, ml_conv_w, ssm_conv_w]))
    c_all, mlw_all, ssw_all = _unpack_rows(g0, [(D_MODEL,), (CONV_K, mlw_cols), (CONV_K, ssw_cols)])
    ml_conv_full = mlw_all[0::2].transpose(1, 0, 2).reshape(CONV_K, N_CHIPS * mlw_cols)
    ssm_conv_full = ssw_all[0::2].transpose(1, 0, 2).reshape(CONV_K, N_CHIPS * ssw_cols)

    ada_b_mine = lax.dynamic_slice_in_dim(ada_b, chip * ada_cols, ada_cols, axis=1)
    g1 = _allgather8(_ada_fwd(c_all, ada_w[0], ada_b_mine))
    mod = lax.dynamic_index_in_dim(g1[0::2], dev, axis=1, keepdims=False).reshape(1, 3 * D_MODEL)
    shift, scale, gate = mod[:, :D_MODEL], mod[:, D_MODEL:2 * D_MODEL], mod[:, 2 * D_MODEL:]

    def whole(lands, owns):
        return [lax.dynamic_update_index_in_dim(got, own, chip, 0).reshape(N_CHIPS, -1, own.shape[-1])
                for got, own in zip(_pair_forward(lands), owns)]

    mine = [_bf(a[0]).reshape(2, a.shape[1] // 2, a.shape[2]) for a in (w_in, w_proj_m, w_proj_s, w_out)]
    landing = lambda own: lax.empty((N_CHIPS,) + own.shape, own.dtype)
    w_send, w_recv, w_src, w_land, w_token = _split_start("w_in_gather_start", _gather_copies, _gather_pieces(mine[:1]),
                                                          mine[:1], [landing(mine[0])])
    u = _prenorm_fwd(x[0], norm_w, scale + w_token[0:1, 0:1], shift, ROWS_ELEMENTWISE)
    w_src, w_land = _split_wait("w_in_gather_wait", _gather_copies, w_send, w_recv, w_src, w_land, u[0])
    behind = (w_src[0][0, 0:1, 0:1] * 0).astype(BF16)
    later = [a + behind for a in mine[1:]]
    p_send, p_recv, p_src, p_land, p_token = _split_start("merge_gather_start", _gather_copies, _gather_pieces(later),
                                                          later, [landing(a) for a in later])
    w_in_p = _shards_to_padded(whole(w_land, w_src)[0])
    b_in_p = _pad_cols(b_in) + p_token[0:1, 0:1]

    def late_weights(after):
        srcs, lands = _split_wait("merge_gather_wait", _gather_copies, p_send, p_recv, p_src, p_land, after)
        return [a.reshape(-1, D_MODEL) for a in whole(lands, srcs)]

    def start_exchange(g_w_in, g_wpm, g_wps, g_wo):
        split = lambda g, rows: _bf(g).reshape(N_CHIPS, 2, rows // (2 * N_CHIPS), g.shape[-1])
        slabs = [split(_padded_to_shards(_bf(g_w_in)), N_CHIPS * D_MODEL),
                 split(g_wpm, g_wpm.shape[0]), split(g_wps, g_wps.shape[0]), split(g_wo, g_wo.shape[0])]
        pair_sums = []
        for slab, rec in zip(slabs, _pair_send(slabs)):
            kept = lax.dynamic_index_in_dim(slab, pos[2], 1, keepdims=False)
            rows = kept.shape[0] * kept.shape[1]
            both = _sum_parts(kept.reshape(rows, -1), rec.reshape(1, rows, -1), ROWS_SUM, BF16)
            pair_sums.append(both.reshape(kept.shape))
        send_sems, recv_sems, sums, lands, token = _chip_scatter_start(pair_sums)
        return token, (send_sems, recv_sems, sums, lands)

    grad_x, small, _, _, _, _, in_flight = _local_step(
        x[0], loss_target[0], scale, shift, gate, norm_w, w_in_p, b_in_p, ml_conv_full, ml_conv_b, ml_norm_w,
        ssm_conv_full, ssm_conv_b, ssm_a_log, ssm_d, ssm_norm_w, None, None, None, final_w[None], start_exchange,
        late_weights, u)

    g2 = _allgather8(_pack([small[name] for name, _ in SMALL_SUMS]))
    total = dict(zip([name for name, _ in SMALL_SUMS], _unpack(_sum_parts(None, g2, g2.shape[1]), [s for _, s in SMALL_SUMS])))
    dmod_all = g2[:, :3 * D_MODEL // LANE].reshape(N_DEV, 3 * D_MODEL)
    grads = dict(total)
    grads["ada_b"] = total["mod"]
    grads["ml_conv_w"] = lax.dynamic_slice_in_dim(total["ml_conv_w"], chip * mlw_cols, mlw_cols, axis=1)
    grads["ssm_conv_w"] = lax.dynamic_slice_in_dim(total["ssm_conv_w"], chip * ssw_cols, ssw_cols, axis=1)
    grads["ada_w"] = _ada_bwd(c_all, lax.dynamic_slice_in_dim(dmod_all, chip * ada_cols, ada_cols, axis=1))

    pairs = []
    for both, rec in zip(*_chip_scatter_wait(*in_flight, grad_x)):
        pairs.append(_sum_parts(lax.dynamic_index_in_dim(both, chip, 0, keepdims=False), rec, ROWS_SUM, slot=pos[2]))
    for name, full in zip(("w_in", "w_proj_m", "w_proj_s", "w_out"), _pair_exchange(pairs)):
        grads[name] = full.reshape(-1, full.shape[-1])

    delta, new_m, new_v = {}, {}, {}
    for name in LARGE:
        if w[name].shape[-1] % LANE:
            flat = lambda a: a.reshape(a.shape[-2:]).T
            back = lambda a: a.T.reshape(w[name].shape)
            g_flat = flat(grads[name])
            delta[name], new_m[name], new_v[name] = (back(a) for a in _adamw(flat(w[name]), g_flat, flat(m[name]), flat(v[name]), LANE))
            grads[name] = back(g_flat)
        else:
            delta[name], new_m[name], new_v[name] = _adamw(w[name], grads[name], m[name], v[name], ROWS_ADAMW)
    rest = [name for name in WEIGHTS if name not in LARGE]
    packed = [_pack([t[name] for name in rest]) for t in (w, grads, m, v)]
    for out, buf in zip((delta, new_m, new_v), _adamw(*packed, packed[0].shape[0])):
        out.update(zip(rest, _unpack(buf, [w[name].shape for name in rest])))
    loss = total["loss"][0]
    return (loss, grad_x[None], *[grads[name].reshape(w[name].shape) for name in WEIGHTS], *[delta[name] for name in WEIGHTS],
            *[new_m[name] for name in WEIGHTS], *[new_v[name] for name in WEIGHTS])
```
